```python
import jax, jax.numpy as jnp
from jax import lax
import numpy as np

D_MODEL = 2048
BATCH = 8
SEQ = 4096
DEPTH = 1

CHUNK = 64
D_MIX = D_MODEL
D_ATTN = D_MIX // 2
D_POOL = D_MIX - D_ATTN
HEAD_DIM = 128
N_HEADS = D_ATTN // HEAD_DIM
POOL_WINDOWS = (2, 4, 8, 16)
N_POOL_GROUPS = len(POOL_WINDOWS)
POOL_GROUP_DIM = D_POOL // N_POOL_GROUPS
D_FF = ((8 * D_MODEL // 3 + 127) // 128) * 128
Q_BLOCK = 128
N_MOD = 9
D_IN_PROJ = 3 * D_ATTN + N_HEADS + D_POOL
EPS = 1e-6

kernel_name = "hybrid_fox_pool_macaron_block"


def _rmsnorm(x, g):
    xf = x.astype(jnp.float32)
    xf = xf * lax.rsqrt(jnp.mean(xf * xf, axis=-1, keepdims=True) + EPS)
    return xf.astype(x.dtype) * g


def _modulate(h, shift, scale):
    return h * (1.0 + scale[:, None, :]) + shift[:, None, :]


def _swiglu(h, w_in, w_out):
    a, b = jnp.split(h @ w_in, 2, axis=-1)
    return (jax.nn.silu(a) * b) @ w_out


def _forgetting_attention(q, k, v, log_f):
    S = q.shape[2]
    scale = HEAD_DIM ** -0.5
    F = jnp.cumsum(log_f, axis=-1)
    outs = []
    for i in range(S // Q_BLOCK):
        qs, qe = i * Q_BLOCK, (i + 1) * Q_BLOCK
        qb = q[:, :, qs:qe]
        kb = k[:, :, :qe]
        vb = v[:, :, :qe]
        logits = jnp.einsum('bhqd,bhkd->bhqk', qb, kb).astype(jnp.float32) * scale
        logits = logits + (F[:, :, qs:qe, None] - F[:, :, None, :qe])
        causal = (qs + jnp.arange(Q_BLOCK))[:, None] >= jnp.arange(qe)[None, :]
        logits = jnp.where(causal[None, None], logits, -jnp.inf)
        p = jax.nn.softmax(logits, axis=-1)
        outs.append(jnp.einsum('bhqk,bhkd->bhqd', p.astype(vb.dtype), vb))
    return jnp.concatenate(outs, axis=2)


def _multiscale_pool(u, pool_w, pool_scale):
    B, S, _ = u.shape
    ug = u.reshape(B, S, N_POOL_GROUPS, POOL_GROUP_DIM)
    pos = jnp.arange(S)
    pooled = []
    for g, w in enumerate(POOL_WINDOWS):
        xg = ug[:, :, g].astype(jnp.float32)
        cs0 = jnp.pad(jnp.cumsum(xg, axis=1), ((0, 0), (1, 0), (0, 0)))
        lag = jnp.pad(cs0, ((0, 0), (w - 1, 0), (0, 0)))[:, :S]
        count = jnp.minimum(pos + 1, w).astype(jnp.float32)[None, :, None]
        mean = (cs0[:, 1:] - lag) / count
        pooled.append((mean - xg).astype(u.dtype))
    p = jnp.stack(pooled, axis=2)
    p = jnp.einsum('bsgc,gcd->bsgd', p, pool_w)
    return p.reshape(B, S, D_POOL) * pool_scale


def _hybrid_mixer(h, w_in, b_forget, q_norm_g, k_norm_g, pool_w, pool_scale, w_out):
    B, S, _ = h.shape
    proj = h @ w_in
    q, k, v, f_logit, u = jnp.split(
        proj, [D_ATTN, 2 * D_ATTN, 3 * D_ATTN, 3 * D_ATTN + N_HEADS], axis=-1)

    def heads(t):
        return t.reshape(B, S, N_HEADS, HEAD_DIM).transpose(0, 2, 1, 3)

    q = _rmsnorm(heads(q), q_norm_g)
    k = _rmsnorm(heads(k), k_norm_g)
    v = heads(v)
    log_f = jax.nn.log_sigmoid((f_logit + b_forget).astype(jnp.float32)).transpose(0, 2, 1)
    attn = _forgetting_attention(q, k, v, log_f)
    attn = attn.transpose(0, 2, 1, 3).reshape(B, S, D_ATTN)
    pool = _multiscale_pool(u, pool_w, pool_scale)
    return jnp.concatenate([attn, pool], axis=-1) @ w_out


def _nrm(k, shape, scale):
    return jax.random.normal(k, shape, jnp.float32) * scale


def _fwd_setup_inputs(seed: int = 0) -> dict:
    key = jax.random.key(seed)
    ks = jax.random.split(key, 20)
    L = DEPTH
    return {
        "x": _nrm(ks[0], (BATCH, SEQ, D_MODEL), 1.0),
        "c": _nrm(ks[1], (BATCH, D_MODEL), 1.0),
        "w_ada": _nrm(ks[2], (L, D_MODEL, N_MOD * D_MODEL), 0.5 * D_MODEL ** -0.5),
        "b_ada": _nrm(ks[3], (L, N_MOD * D_MODEL), 0.02),
        "ffn1_norm_g": 1.0 + _nrm(ks[4], (L, D_MODEL), 0.05),
        "ffn1_w_in": _nrm(ks[5], (L, D_MODEL, 2 * D_FF), D_MODEL ** -0.5),
        "ffn1_w_out": _nrm(ks[6], (L, D_FF, D_MODEL), D_FF ** -0.5),
        "mix_norm_g": 1.0 + _nrm(ks[7], (L, D_MODEL), 0.05),
        "w_in": _nrm(ks[8], (L, D_MODEL, D_IN_PROJ), D_MODEL ** -0.5),
        "b_forget": jax.random.uniform(ks[9], (L, N_HEADS), jnp.float32, 1.0, 4.0),
        "q_norm_g": 1.0 + _nrm(ks[10], (L, HEAD_DIM), 0.05),
        "k_norm_g": 1.0 + _nrm(ks[11], (L, HEAD_DIM), 0.05),
        "pool_w": _nrm(ks[12], (L, N_POOL_GROUPS, POOL_GROUP_DIM, POOL_GROUP_DIM), POOL_GROUP_DIM ** -0.5),
        "pool_scale": 1.0 + _nrm(ks[13], (L, D_POOL), 0.1),
        "w_out": _nrm(ks[14], (L, D_MIX, D_MODEL), D_MIX ** -0.5),
        "ffn2_norm_g": 1.0 + _nrm(ks[15], (L, D_MODEL), 0.05),
        "ffn2_w_in": _nrm(ks[16], (L, D_MODEL, 2 * D_FF), D_MODEL ** -0.5),
        "ffn2_w_out": _nrm(ks[17], (L, D_FF, D_MODEL), D_FF ** -0.5),
        "final_norm_g": 1.0 + _nrm(ks[18], (D_MODEL,), 0.05),
    }


def _fwd_reference(x, c, w_ada, b_ada, ffn1_norm_g, ffn1_w_in, ffn1_w_out, mix_norm_g,
              w_in, b_forget, q_norm_g, k_norm_g, pool_w, pool_scale, w_out,
              ffn2_norm_g, ffn2_w_in, ffn2_w_out, final_norm_g):
    c_act = jax.nn.silu(c)
    for l in range(DEPTH):
        mod = c_act @ w_ada[l] + b_ada[l]
        sh1, sc1, g1, sh2, sc2, g2, sh3, sc3, g3 = jnp.split(mod, N_MOD, axis=-1)
        h = _modulate(_rmsnorm(x, ffn1_norm_g[l]), sh1, sc1)
        x = x + 0.5 * g1[:, None, :] * _swiglu(h, ffn1_w_in[l], ffn1_w_out[l])
        h = _modulate(_rmsnorm(x, mix_norm_g[l]), sh2, sc2)
        x = x + g2[:, None, :] * _hybrid_mixer(h, w_in[l], b_forget[l], q_norm_g[l], k_norm_g[l],
                                              pool_w[l], pool_scale[l], w_out[l])
        h = _modulate(_rmsnorm(x, ffn2_norm_g[l]), sh3, sc3)
        x = x + 0.5 * g3[:, None, :] * _swiglu(h, ffn2_w_in[l], ffn2_w_out[l])
    return _rmsnorm(x, final_norm_g)


import jax as _jax
import jax.numpy as _jnp

TWIN_FORMAT = 'train_step'
FWD_PARAMS = ['x', 'c', 'w_ada', 'b_ada', 'ffn1_norm_g', 'ffn1_w_in', 'ffn1_w_out', 'mix_norm_g', 'w_in', 'b_forget', 'q_norm_g', 'k_norm_g', 'pool_w', 'pool_scale', 'w_out', 'ffn2_norm_g', 'ffn2_w_in', 'ffn2_w_out', 'final_norm_g']
TWIN_WEIGHTS = ['w_ada', 'b_ada', 'ffn1_norm_g', 'ffn1_w_in', 'ffn1_w_out', 'mix_norm_g', 'w_in', 'b_forget', 'q_norm_g', 'k_norm_g', 'pool_w', 'pool_scale', 'w_out', 'ffn2_norm_g', 'ffn2_w_in', 'ffn2_w_out', 'final_norm_g']
TWIN_DIFF_INPUT = 'x'
TWIN_INPUTS = ['x', 'c', 'w_ada', 'b_ada', 'ffn1_norm_g', 'ffn1_w_in', 'ffn1_w_out', 'mix_norm_g', 'w_in', 'b_forget', 'q_norm_g', 'k_norm_g', 'pool_w', 'pool_scale', 'w_out', 'ffn2_norm_g', 'ffn2_w_in', 'ffn2_w_out', 'final_norm_g', 'loss_target', 'm_w_ada', 'm_b_ada', 'm_ffn1_norm_g', 'm_ffn1_w_in', 'm_ffn1_w_out', 'm_mix_norm_g', 'm_w_in', 'm_b_forget', 'm_q_norm_g', 'm_k_norm_g', 'm_pool_w', 'm_pool_scale', 'm_w_out', 'm_ffn2_norm_g', 'm_ffn2_w_in', 'm_ffn2_w_out', 'm_final_norm_g', 'v_w_ada', 'v_b_ada', 'v_ffn1_norm_g', 'v_ffn1_w_in', 'v_ffn1_w_out', 'v_mix_norm_g', 'v_w_in', 'v_b_forget', 'v_q_norm_g', 'v_k_norm_g', 'v_pool_w', 'v_pool_scale', 'v_w_out', 'v_ffn2_norm_g', 'v_ffn2_w_in', 'v_ffn2_w_out', 'v_final_norm_g']
TWIN_OUTPUTS = ['loss', 'grad_x', 'grad_w_ada', 'grad_b_ada', 'grad_ffn1_norm_g', 'grad_ffn1_w_in', 'grad_ffn1_w_out', 'grad_mix_norm_g', 'grad_w_in', 'grad_b_forget', 'grad_q_norm_g', 'grad_k_norm_g', 'grad_pool_w', 'grad_pool_scale', 'grad_w_out', 'grad_ffn2_norm_g', 'grad_ffn2_w_in', 'grad_ffn2_w_out', 'grad_final_norm_g', 'delta_w_ada', 'delta_b_ada', 'delta_ffn1_norm_g', 'delta_ffn1_w_in', 'delta_ffn1_w_out', 'delta_mix_norm_g', 'delta_w_in', 'delta_b_forget', 'delta_q_norm_g', 'delta_k_norm_g', 'delta_pool_w', 'delta_pool_scale', 'delta_w_out', 'delta_ffn2_norm_g', 'delta_ffn2_w_in', 'delta_ffn2_w_out', 'delta_final_norm_g', 'new_m_w_ada', 'new_m_b_ada', 'new_m_ffn1_norm_g', 'new_m_ffn1_w_in', 'new_m_ffn1_w_out', 'new_m_mix_norm_g', 'new_m_w_in', 'new_m_b_forget', 'new_m_q_norm_g', 'new_m_k_norm_g', 'new_m_pool_w', 'new_m_pool_scale', 'new_m_w_out', 'new_m_ffn2_norm_g', 'new_m_ffn2_w_in', 'new_m_ffn2_w_out', 'new_m_final_norm_g', 'new_v_w_ada', 'new_v_b_ada', 'new_v_ffn1_norm_g', 'new_v_ffn1_w_in', 'new_v_ffn1_w_out', 'new_v_mix_norm_g', 'new_v_w_in', 'new_v_b_forget', 'new_v_q_norm_g', 'new_v_k_norm_g', 'new_v_pool_w', 'new_v_pool_scale', 'new_v_w_out', 'new_v_ffn2_norm_g', 'new_v_ffn2_w_in', 'new_v_ffn2_w_out', 'new_v_final_norm_g']
TWIN_LEAF_KINDS = {'loss': 'loss', 'grad_x': 'grad_x', 'grad_w_ada': 'grad_w', 'grad_b_ada': 'grad_w', 'grad_ffn1_norm_g': 'grad_w', 'grad_ffn1_w_in': 'grad_w', 'grad_ffn1_w_out': 'grad_w', 'grad_mix_norm_g': 'grad_w', 'grad_w_in': 'grad_w', 'grad_b_forget': 'grad_w', 'grad_q_norm_g': 'grad_w', 'grad_k_norm_g': 'grad_w', 'grad_pool_w': 'grad_w', 'grad_pool_scale': 'grad_w', 'grad_w_out': 'grad_w', 'grad_ffn2_norm_g': 'grad_w', 'grad_ffn2_w_in': 'grad_w', 'grad_ffn2_w_out': 'grad_w', 'grad_final_norm_g': 'grad_w', 'delta_w_ada': 'delta_w', 'delta_b_ada': 'delta_w', 'delta_ffn1_norm_g': 'delta_w', 'delta_ffn1_w_in': 'delta_w', 'delta_ffn1_w_out': 'delta_w', 'delta_mix_norm_g': 'delta_w', 'delta_w_in': 'delta_w', 'delta_b_forget': 'delta_w', 'delta_q_norm_g': 'delta_w', 'delta_k_norm_g': 'delta_w', 'delta_pool_w': 'delta_w', 'delta_pool_scale': 'delta_w', 'delta_w_out': 'delta_w', 'delta_ffn2_norm_g': 'delta_w', 'delta_ffn2_w_in': 'delta_w', 'delta_ffn2_w_out': 'delta_w', 'delta_final_norm_g': 'delta_w', 'new_m_w_ada': 'new_m', 'new_m_b_ada': 'new_m', 'new_m_ffn1_norm_g': 'new_m', 'new_m_ffn1_w_in': 'new_m', 'new_m_ffn1_w_out': 'new_m', 'new_m_mix_norm_g': 'new_m', 'new_m_w_in': 'new_m', 'new_m_b_forget': 'new_m', 'new_m_q_norm_g': 'new_m', 'new_m_k_norm_g': 'new_m', 'new_m_pool_w': 'new_m', 'new_m_pool_scale': 'new_m', 'new_m_w_out': 'new_m', 'new_m_ffn2_norm_g': 'new_m', 'new_m_ffn2_w_in': 'new_m', 'new_m_ffn2_w_out': 'new_m', 'new_m_final_norm_g': 'new_m', 'new_v_w_ada': 'new_v', 'new_v_b_ada': 'new_v', 'new_v_ffn1_norm_g': 'new_v', 'new_v_ffn1_w_in': 'new_v', 'new_v_ffn1_w_out': 'new_v', 'new_v_mix_norm_g': 'new_v', 'new_v_w_in': 'new_v', 'new_v_b_forget': 'new_v', 'new_v_q_norm_g': 'new_v', 'new_v_k_norm_g': 'new_v', 'new_v_pool_w': 'new_v', 'new_v_pool_scale': 'new_v', 'new_v_w_out': 'new_v', 'new_v_ffn2_norm_g': 'new_v', 'new_v_ffn2_w_in': 'new_v', 'new_v_ffn2_w_out': 'new_v', 'new_v_final_norm_g': 'new_v'}


def _forward(args):
    return _fwd_reference(*[args[k] for k in FWD_PARAMS])


def _output_shape():
    def fwd():
        inp = _fwd_setup_inputs(0)
        return _fwd_reference(*[inp[k] for k in FWD_PARAMS])
    out = _jax.eval_shape(fwd)
    return out.shape, out.dtype

N_MICROBATCH = 1
ADAM_LR = 0.001
ADAM_B1 = 0.9
ADAM_B2 = 0.999
ADAM_EPS = 1e-08
ADAM_WD = 0.01
ADAM_STEP = 10
PER_EXAMPLE_BATCH_AXIS = {'x': 0, 'c': 0, 'loss_target': 0}
SHARED_INPUTS = []
_WEIGHT_DTYPES = {'w_ada': _jnp.float32, 'b_ada': _jnp.float32, 'ffn1_norm_g': _jnp.float32, 'ffn1_w_in': _jnp.float32, 'ffn1_w_out': _jnp.float32, 'mix_norm_g': _jnp.float32, 'w_in': _jnp.float32, 'b_forget': _jnp.float32, 'q_norm_g': _jnp.float32, 'k_norm_g': _jnp.float32, 'pool_w': _jnp.float32, 'pool_scale': _jnp.float32, 'w_out': _jnp.float32, 'ffn2_norm_g': _jnp.float32, 'ffn2_w_in': _jnp.float32, 'ffn2_w_out': _jnp.float32, 'final_norm_g': _jnp.float32}
MOMENT_SCALE = {'w_ada': 2.182713e-02, 'b_ada': 4.356448e-02, 'ffn1_norm_g': 1.359641e-02, 'ffn1_w_in': 5.975722e-03, 'ffn1_w_out': 9.659757e-03, 'mix_norm_g': 2.010242e-02, 'w_in': 1.470739e-02, 'b_forget': 8.206414e-02, 'q_norm_g': 2.156651e-02, 'k_norm_g': 2.139439e-02, 'pool_w': 2.450219e-02, 'pool_scale': 2.449802e-02, 'w_out': 1.959740e-02, 'ffn2_norm_g': 1.289489e-02, 'ffn2_w_in': 5.758484e-03, 'ffn2_w_out': 9.295037e-03, 'final_norm_g': 1.603709e+01}


def _to_microbatches(a, axis):
    t = _jnp.moveaxis(a, axis, 0)
    t = t.reshape((N_MICROBATCH, t.shape[0] // N_MICROBATCH) + t.shape[1:])
    return _jnp.moveaxis(t, 1, axis + 1)


def setup_inputs(seed: int = 0) -> dict:
    inp = _fwd_setup_inputs(seed)
    key = _jax.random.fold_in(_jax.random.key(seed), 7919)
    shape, _ = _output_shape()
    out = dict(inp)
    out["loss_target"] = _jax.random.normal(_jax.random.fold_in(key, 0), shape, _jnp.float32)
    for i, name in enumerate(TWIN_WEIGHTS):
        w = inp[name].astype(_jnp.float32)
        if MOMENT_SCALE is None:
            s = _jnp.sqrt(_jnp.mean(_jnp.square(w)) + 1e-30)
        else:
            s = MOMENT_SCALE[name]
        km, kv = _jax.random.split(_jax.random.fold_in(key, i + 1))
        out[name] = w
        out["m_" + name] = s * _jax.random.normal(km, w.shape, _jnp.float32)
        out["v_" + name] = (s * s) * _jax.random.uniform(kv, w.shape, _jnp.float32, 0.5, 1.5)
    if N_MICROBATCH > 1:
        for name, axis in PER_EXAMPLE_BATCH_AXIS.items():
            out[name] = _to_microbatches(out[name], axis)
    return {'x': out['x'], 'c': out['c'], 'w_ada': out['w_ada'], 'b_ada': out['b_ada'], 'ffn1_norm_g': out['ffn1_norm_g'], 'ffn1_w_in': out['ffn1_w_in'], 'ffn1_w_out': out['ffn1_w_out'], 'mix_norm_g': out['mix_norm_g'], 'w_in': out['w_in'], 'b_forget': out['b_forget'], 'q_norm_g': out['q_norm_g'], 'k_norm_g': out['k_norm_g'], 'pool_w': out['pool_w'], 'pool_scale': out['pool_scale'], 'w_out': out['w_out'], 'ffn2_norm_g': out['ffn2_norm_g'], 'ffn2_w_in': out['ffn2_w_in'], 'ffn2_w_out': out['ffn2_w_out'], 'final_norm_g': out['final_norm_g'], 'loss_target': out['loss_target'], 'm_w_ada': out['m_w_ada'], 'm_b_ada': out['m_b_ada'], 'm_ffn1_norm_g': out['m_ffn1_norm_g'], 'm_ffn1_w_in': out['m_ffn1_w_in'], 'm_ffn1_w_out': out['m_ffn1_w_out'], 'm_mix_norm_g': out['m_mix_norm_g'], 'm_w_in': out['m_w_in'], 'm_b_forget': out['m_b_forget'], 'm_q_norm_g': out['m_q_norm_g'], 'm_k_norm_g': out['m_k_norm_g'], 'm_pool_w': out['m_pool_w'], 'm_pool_scale': out['m_pool_scale'], 'm_w_out': out['m_w_out'], 'm_ffn2_norm_g': out['m_ffn2_norm_g'], 'm_ffn2_w_in': out['m_ffn2_w_in'], 'm_ffn2_w_out': out['m_ffn2_w_out'], 'm_final_norm_g': out['m_final_norm_g'], 'v_w_ada': out['v_w_ada'], 'v_b_ada': out['v_b_ada'], 'v_ffn1_norm_g': out['v_ffn1_norm_g'], 'v_ffn1_w_in': out['v_ffn1_w_in'], 'v_ffn1_w_out': out['v_ffn1_w_out'], 'v_mix_norm_g': out['v_mix_norm_g'], 'v_w_in': out['v_w_in'], 'v_b_forget': out['v_b_forget'], 'v_q_norm_g': out['v_q_norm_g'], 'v_k_norm_g': out['v_k_norm_g'], 'v_pool_w': out['v_pool_w'], 'v_pool_scale': out['v_pool_scale'], 'v_w_out': out['v_w_out'], 'v_ffn2_norm_g': out['v_ffn2_norm_g'], 'v_ffn2_w_in': out['v_ffn2_w_in'], 'v_ffn2_w_out': out['v_ffn2_w_out'], 'v_final_norm_g': out['v_final_norm_g']}


def _loss(weights, diff, rest, loss_target):
    with _jax.named_scope("forward"):
        args = {**rest, TWIN_DIFF_INPUT: diff, **{k: w.astype(_WEIGHT_DTYPES[k]) for k, w in weights.items()}}
        y = _forward(args)
    with _jax.named_scope("loss_head"):
        err = _jnp.square(y.astype(_jnp.float32) - loss_target)
        return 0.5 * _jnp.sum(_jnp.mean(err, axis=-1)) if err.ndim else 0.5 * err


def _adamw(w, g, m, v):
    m = ADAM_B1 * m + (1.0 - ADAM_B1) * g
    v = ADAM_B2 * v + (1.0 - ADAM_B2) * _jnp.square(g)
    m_hat = m / (1.0 - ADAM_B1 ** ADAM_STEP)
    v_hat = v / (1.0 - ADAM_B2 ** ADAM_STEP)
    delta = -ADAM_LR * (m_hat / (_jnp.sqrt(v_hat) + ADAM_EPS) + ADAM_WD * w)
    return delta, m, v


def reference(x, c, w_ada, b_ada, ffn1_norm_g, ffn1_w_in, ffn1_w_out, mix_norm_g, w_in, b_forget, q_norm_g, k_norm_g, pool_w, pool_scale, w_out, ffn2_norm_g, ffn2_w_in, ffn2_w_out, final_norm_g, loss_target, m_w_ada, m_b_ada, m_ffn1_norm_g, m_ffn1_w_in, m_ffn1_w_out, m_mix_norm_g, m_w_in, m_b_forget, m_q_norm_g, m_k_norm_g, m_pool_w, m_pool_scale, m_w_out, m_ffn2_norm_g, m_ffn2_w_in, m_ffn2_w_out, m_final_norm_g, v_w_ada, v_b_ada, v_ffn1_norm_g, v_ffn1_w_in, v_ffn1_w_out, v_mix_norm_g, v_w_in, v_b_forget, v_q_norm_g, v_k_norm_g, v_pool_w, v_pool_scale, v_w_out, v_ffn2_norm_g, v_ffn2_w_in, v_ffn2_w_out, v_final_norm_g):
    given = dict(x=x, c=c, w_ada=w_ada, b_ada=b_ada, ffn1_norm_g=ffn1_norm_g, ffn1_w_in=ffn1_w_in, ffn1_w_out=ffn1_w_out, mix_norm_g=mix_norm_g, w_in=w_in, b_forget=b_forget, q_norm_g=q_norm_g, k_norm_g=k_norm_g, pool_w=pool_w, pool_scale=pool_scale, w_out=w_out, ffn2_norm_g=ffn2_norm_g, ffn2_w_in=ffn2_w_in, ffn2_w_out=ffn2_w_out, final_norm_g=final_norm_g, loss_target=loss_target, m_w_ada=m_w_ada, m_b_ada=m_b_ada, m_ffn1_norm_g=m_ffn1_norm_g, m_ffn1_w_in=m_ffn1_w_in, m_ffn1_w_out=m_ffn1_w_out, m_mix_norm_g=m_mix_norm_g, m_w_in=m_w_in, m_b_forget=m_b_forget, m_q_norm_g=m_q_norm_g, m_k_norm_g=m_k_norm_g, m_pool_w=m_pool_w, m_pool_scale=m_pool_scale, m_w_out=m_w_out, m_ffn2_norm_g=m_ffn2_norm_g, m_ffn2_w_in=m_ffn2_w_in, m_ffn2_w_out=m_ffn2_w_out, m_final_norm_g=m_final_norm_g, v_w_ada=v_w_ada, v_b_ada=v_b_ada, v_ffn1_norm_g=v_ffn1_norm_g, v_ffn1_w_in=v_ffn1_w_in, v_ffn1_w_out=v_ffn1_w_out, v_mix_norm_g=v_mix_norm_g, v_w_in=v_w_in, v_b_forget=v_b_forget, v_q_norm_g=v_q_norm_g, v_k_norm_g=v_k_norm_g, v_pool_w=v_pool_w, v_pool_scale=v_pool_scale, v_w_out=v_w_out, v_ffn2_norm_g=v_ffn2_norm_g, v_ffn2_w_in=v_ffn2_w_in, v_ffn2_w_out=v_ffn2_w_out, v_final_norm_g=v_final_norm_g)
    weights = {n: given[n] for n in TWIN_WEIGHTS}
    shared = {n: given[n] for n in SHARED_INPUTS}
    per_example = {n: given[n] for n in ['x', 'c']}
    grad_fn = _jax.value_and_grad(_loss, argnums=(0, 1))

    def one_microbatch(ex, loss_target):
        ex = dict(ex)
        diff = ex.pop(TWIN_DIFF_INPUT)
        return grad_fn(weights, diff, {**shared, **ex}, loss_target)

    if N_MICROBATCH == 1:
        loss, (grad_w, grad_x) = one_microbatch(per_example, given["loss_target"])
    else:
        def body(carry, xs):
            loss_sum, grad_sum = carry
            l_k, (gw_k, gx_k) = one_microbatch(xs[0], xs[1])
            with _jax.named_scope("update"):
                return (loss_sum + l_k, _jax.tree.map(_jnp.add, grad_sum, gw_k)), gx_k

        init = (_jnp.zeros((), _jnp.float32), _jax.tree.map(_jnp.zeros_like, weights))
        (loss, grad_w), grad_x = _jax.lax.scan(body, init, (per_example, given["loss_target"]))
    with _jax.named_scope("update"):
        delta_w, new_m, new_v = {}, {}, {}
        for n in TWIN_WEIGHTS:
            delta_w[n], new_m[n], new_v[n] = _adamw(weights[n], grad_w[n], given["m_" + n], given["v_" + n])
    return (loss, grad_x, *[grad_w[n] for n in TWIN_WEIGHTS], *[delta_w[n] for n in TWIN_WEIGHTS],
            *[new_m[n] for n in TWIN_WEIGHTS], *[new_v[n] for n in TWIN_WEIGHTS])
```

```python
import functools

import jax
import jax.numpy as jnp
from jax import lax
from jax.experimental import pallas as pl
from jax.experimental.pallas import tpu as pltpu

F32 = jnp.float32
BF16 = jnp.bfloat16

D = 2048
N_HEADS = 8
HEAD_DIM = 128
D_ATTN = 1024
D_POOL = 1024
POOL_WINDOWS = (2, 4, 8, 16)
GROUP_DIM = 256
HALO = 16
N_MOD = 9
EPS = 1e-6
D_PROJ = 3 * D_ATTN + D_POOL
D_PROJ_PAD = D_PROJ + 128
LANE = 128
ATTN_BLOCK = 512
POOL_BLOCK = 512

ADAM_LR = 0.001
ADAM_B1 = 0.9
ADAM_B2 = 0.999
ADAM_EPS = 1e-08
ADAM_WD = 0.01
ADAM_STEP = 10

VMEM_LIMIT_V7X = 56 * 1024 * 1024
MESH_ID = pl.DeviceIdType.MESH
ANY = pl.BlockSpec(memory_space=pl.ANY)
VMEM = pl.BlockSpec(memory_space=pltpu.VMEM)

NT = (((1,), (1,)), ((), ()))
TN = (((0,), (0,)), ((), ()))


def _pcall(body, *, name, out_shape, grid=None, in_specs=None, out_specs=None, scratch=(), sem=None, prefetch=0):
    params = dict(vmem_limit_bytes=VMEM_LIMIT_V7X)
    if sem is not None:
        params["dimension_semantics"] = sem
    kw = dict(name=name, out_shape=out_shape, compiler_params=pltpu.CompilerParams(**params))
    if prefetch:
        kw["grid_spec"] = pltpu.PrefetchScalarGridSpec(
            num_scalar_prefetch=prefetch, grid=grid, in_specs=in_specs, out_specs=out_specs, scratch_shapes=list(scratch))
    else:
        if grid is not None:
            kw["grid"] = grid
        kw["in_specs"] = in_specs
        kw["out_specs"] = out_specs
        kw["scratch_shapes"] = list(scratch)
    return pl.pallas_call(body, **kw)


def _pick(n, cap, mult):
    best = None
    for d in range(mult, min(n, cap) + 1, mult):
        if n % d == 0:
            best = d
    assert best is not None, (n, cap, mult)
    return best


def _coords():
    return lax.axis_index("x"), lax.axis_index("y"), lax.axis_index("c")


def _dot(a, b, dims=None):
    if dims is None:
        return jnp.dot(a, b, preferred_element_type=F32)
    return lax.dot_general(a, b, dims, preferred_element_type=F32)


def _remote(src, dst, ssem, rsem, dev):
    return pltpu.make_async_remote_copy(src_ref=src, dst_ref=dst, send_sem=ssem, recv_sem=rsem,
                                        device_id=dev, device_id_type=MESH_ID)


def _allgather_small(v, whole_mesh, name):
    masks = list(range(1, 8)) if whole_mesh else [4, 2, 6]
    nslot = 8 if whole_mesh else 4

    def slot(px, py, pc):
        return 4 * px + 2 * py + pc if whole_mesh else 2 * px + py

    def body(v_ref, out_ref, ssem, rsem, lsem):
        x, y, c = _coords()
        mine = slot(x, y, c)
        peers = [(jnp.bitwise_xor(x, (m >> 2) & 1), jnp.bitwise_xor(y, (m >> 1) & 1), jnp.bitwise_xor(c, m & 1))
                 for m in masks]
        loc = pltpu.make_async_copy(v_ref, out_ref.at[mine], lsem)
        loc.start()
        sends = [_remote(v_ref, out_ref.at[mine], ssem.at[k], rsem.at[k], p) for k, p in enumerate(peers)]
        for cp in sends:
            cp.start()
        for k, p in enumerate(peers):
            _remote(v_ref, out_ref.at[slot(*p)], ssem.at[k], rsem.at[k], p).wait_recv()
        for cp in sends:
            cp.wait_send()
        loc.wait()

    return _pcall(body, name=name, out_shape=jax.ShapeDtypeStruct((nslot,) + v.shape, v.dtype),
                  in_specs=[VMEM], out_specs=VMEM,
                  scratch=[pltpu.SemaphoreType.DMA((len(masks),)), pltpu.SemaphoreType.DMA((len(masks),)),
                           pltpu.SemaphoreType.DMA(())])(v)


def _other_chips(x, y):
    return [(1 - x, y), (x, 1 - y), (1 - x, 1 - y)]


def _gather_weights(ws, name):
    n = len(ws)

    def body(*refs):
        w_refs, g_refs = refs[:n], refs[n:2 * n]
        ssem, rsem, lsem = refs[2 * n:]
        x, y, c = _coords()
        me, sib = 2 * x + y, (x, y, 1 - c)
        chips = _other_chips(x, y)

        def rows(a, core):
            h = ws[a].shape[0] // 2
            return pl.ds(pl.multiple_of(core * h, 16), h)

        local, sends = [], []
        for a in range(n):
            cp = pltpu.make_async_copy(w_refs[a], g_refs[a].at[me], lsem.at[a])
            cp.start()
            local.append(cp)
            for j, chip in enumerate(chips):
                cp = _remote(w_refs[a].at[rows(a, c)], g_refs[a].at[me, rows(a, c)],
                             ssem.at[6 * a + j], rsem.at[6 * a + j], (*chip, c))
                cp.start()
                sends.append(cp)
        for a in range(n):
            for j, chip in enumerate(chips):
                got = g_refs[a].at[2 * chip[0] + chip[1], rows(a, c)]
                _remote(got, got, ssem.at[6 * a + j], rsem.at[6 * a + j], (*chip, c)).wait_recv()
                cp = _remote(got, got, ssem.at[6 * a + 3 + j], rsem.at[6 * a + 3 + j], sib)
                cp.start()
                sends.append(cp)
        for a in range(n):
            for j, chip in enumerate(chips):
                got = g_refs[a].at[2 * chip[0] + chip[1], rows(a, 1 - c)]
                _remote(got, got, ssem.at[6 * a + 3 + j], rsem.at[6 * a + 3 + j], sib).wait_recv()
        for cp in sends:
            cp.wait_send()
        for cp in local:
            cp.wait()

    out = _pcall(body, name=name, out_shape=[jax.ShapeDtypeStruct((4,) + w.shape, w.dtype) for w in ws],
                 in_specs=[ANY] * n, out_specs=[ANY] * n,
                 scratch=[pltpu.SemaphoreType.DMA((6 * n,)), pltpu.SemaphoreType.DMA((6 * n,)),
                          pltpu.SemaphoreType.DMA((n,))])(*ws)
    return list(out)


def _reduce_siblings(ps, name):
    n = len(ps)

    def body(*refs):
        p_refs, r_refs = refs[:n], refs[n:2 * n]
        ssem, rsem = refs[2 * n:]
        x, y, c = _coords()
        cps = []
        for a in range(n):
            h = ps[a].shape[1] // 2
            src = p_refs[a].at[:, pl.ds(pl.multiple_of((1 - c) * h, 16), h), :]
            cp = _remote(src, r_refs[a], ssem.at[a], rsem.at[a], (x, y, 1 - c))
            cp.start()
            cps.append(cp)
        for cp in cps:
            cp.wait_recv()
        for cp in cps:
            cp.wait_send()

    out = _pcall(body, name=name,
                 out_shape=[jax.ShapeDtypeStruct((4, p.shape[1] // 2, p.shape[2]), p.dtype) for p in ps],
                 in_specs=[ANY] * n, out_specs=[ANY] * n,
                 scratch=[pltpu.SemaphoreType.DMA((n,)), pltpu.SemaphoreType.DMA((n,))])(*ps)
    return list(out)


def _reduce_chips(hs, name):
    n = len(hs)

    def body(*refs):
        h_refs, o_refs = refs[:n], refs[n:2 * n]
        ssem, rsem = refs[2 * n:]
        x, y, c = _coords()
        cps = []
        for a in range(n):
            for j, chip in enumerate(_other_chips(x, y)):
                cp = _remote(h_refs[a].at[2 * chip[0] + chip[1]], o_refs[a].at[j],
                             ssem.at[3 * a + j], rsem.at[3 * a + j], (*chip, c))
                cp.start()
                cps.append(cp)
        for cp in cps:
            cp.wait_recv()
        for cp in cps:
            cp.wait_send()

    out = _pcall(body, name=name, out_shape=[jax.ShapeDtypeStruct((3,) + h.shape[1:], h.dtype) for h in hs],
                 in_specs=[ANY] * n, out_specs=[ANY] * n,
                 scratch=[pltpu.SemaphoreType.DMA((3 * n,)), pltpu.SemaphoreType.DMA((3 * n,))])(*hs)
    return list(out)


def _share_siblings(gs, name):
    n = len(gs)

    def body(*refs):
        g_refs, o_refs = refs[:n], refs[n:2 * n]
        ssem, rsem, lsem = refs[2 * n:]
        x, y, c = _coords()
        cps, local = [], []
        for a in range(n):
            lc = pltpu.make_async_copy(g_refs[a], o_refs[a].at[c], lsem.at[a])
            lc.start()
            local.append(lc)
            cp = _remote(g_refs[a], o_refs[a].at[c], ssem.at[a], rsem.at[a], (x, y, 1 - c))
            cp.start()
            cps.append(cp)
        for a in range(n):
            _remote(g_refs[a], o_refs[a].at[1 - c], ssem.at[a], rsem.at[a], (x, y, 1 - c)).wait_recv()
        for cp in cps:
            cp.wait_send()
        for lc in local:
            lc.wait()

    out = _pcall(body, name=name, out_shape=[jax.ShapeDtypeStruct((2,) + g.shape, g.dtype) for g in gs],
                 in_specs=[ANY] * n, out_specs=[ANY] * n,
                 scratch=[pltpu.SemaphoreType.DMA((n,)), pltpu.SemaphoreType.DMA((n,)), pltpu.SemaphoreType.DMA((n,))])(*gs)
    return list(out)


def _add_sibling(p, r, core, name):
    _, rr, cc = p.shape
    h = rr // 2
    th = _pick(h, max(16, (2 << 20) // (2 * cc)), 16)
    nb = h // th

    def body(c_ref, p_ref, r_ref, o_ref):
        o_ref[...] = (p_ref[...].astype(F32) + r_ref[...].astype(F32)).astype(BF16)

    return _pcall(body, name=name, out_shape=jax.ShapeDtypeStruct((4, h, cc), BF16), grid=(4, nb),
                  in_specs=[pl.BlockSpec((None, th, cc), lambda k, i, c_ref: (k, c_ref[0] * nb + i, 0)),
                            pl.BlockSpec((None, th, cc), lambda k, i, c_ref: (k, i, 0))],
                  out_specs=pl.BlockSpec((None, th, cc), lambda k, i, c_ref: (k, i, 0)),
                  sem=("parallel", "parallel"), prefetch=1)(core, p, r)


def _add_chips(hh, r, chip, name):
    _, h, cc = hh.shape
    th = _pick(h, max(16, (2 << 20) // (2 * cc)), 16)

    def body(k_ref, h_ref, r_ref, o_ref):
        s = h_ref[...].astype(F32) + r_ref[0].astype(F32)
        s = s + r_ref[1].astype(F32)
        o_ref[...] = s + r_ref[2].astype(F32)

    return _pcall(body, name=name, out_shape=jax.ShapeDtypeStruct((h, cc), F32), grid=(h // th,),
                  in_specs=[pl.BlockSpec((None, th, cc), lambda i, k_ref: (k_ref[0], i, 0)),
                            pl.BlockSpec((3, th, cc), lambda i, k_ref: (0, i, 0))],
                  out_specs=pl.BlockSpec((th, cc), lambda i, k_ref: (i, 0)),
                  sem=("parallel",), prefetch=1)(chip, hh, r)


def _adamw_math(w, g, m, v):
    m = ADAM_B1 * m + (1.0 - ADAM_B1) * g
    v = ADAM_B2 * v + (1.0 - ADAM_B2) * (g * g)
    m_hat = m / (1.0 - ADAM_B1 ** ADAM_STEP)
    v_hat = v / (1.0 - ADAM_B2 ** ADAM_STEP)
    delta = -ADAM_LR * (m_hat / (jnp.sqrt(v_hat) + ADAM_EPS) + ADAM_WD * w)
    return delta, m, v


def _adamw(w, g, m, v, name):
    rr, cc = w.shape
    tr = _pick(rr, max(8, (3 << 20) // (4 * cc)), 8)

    def body(w_ref, g_ref, m_ref, v_ref, d_ref, mo_ref, vo_ref):
        d, mm, vv = _adamw_math(w_ref[...], g_ref[...], m_ref[...], v_ref[...])
        d_ref[...] = d
        mo_ref[...] = mm
        vo_ref[...] = vv

    spec = pl.BlockSpec((tr, cc), lambda i: (i, 0))
    return _pcall(body, name=name, out_shape=[jax.ShapeDtypeStruct(w.shape, F32)] * 3, grid=(rr // tr,),
                  in_specs=[spec] * 4, out_specs=[spec] * 3, sem=("parallel",))(w, g, m, v)


def _adamw_small(w, g8, m, v, name):
    def body(w_ref, g_ref, m_ref, v_ref, go_ref, d_ref, mo_ref, vo_ref):
        g = g_ref[0]
        for k in range(1, 8):
            g = g + g_ref[k]
        d, mm, vv = _adamw_math(w_ref[...], g, m_ref[...], v_ref[...])
        go_ref[...] = g
        d_ref[...] = d
        mo_ref[...] = mm
        vo_ref[...] = vv

    return _pcall(body, name=name, out_shape=[jax.ShapeDtypeStruct(w.shape, F32)] * 4,
                  in_specs=[VMEM] * 4, out_specs=[VMEM] * 4)(w, g8, m, v)


def _ada_fwd(c16, w_ada, b_ada, name):
    n = w_ada.shape[1]
    tn = _pick(n, 512, LANE)

    def body(c_ref, w_ref, b_ref, act_ref, mod_ref):
        cv = c_ref[...]
        act = cv * jax.nn.sigmoid(cv)
        act_ref[...] = act
        mod_ref[...] = _dot(act.astype(BF16), w_ref[...].astype(BF16)) + b_ref[...]

    return _pcall(body, name=name,
                  out_shape=[jax.ShapeDtypeStruct((16, D), F32), jax.ShapeDtypeStruct((16, n), F32)], grid=(n // tn,),
                  in_specs=[pl.BlockSpec((16, D), lambda j: (0, 0)), pl.BlockSpec((D, tn), lambda j: (0, j)),
                            pl.BlockSpec((1, tn), lambda j: (0, j))],
                  out_specs=[pl.BlockSpec((16, D), lambda j: (0, 0)), pl.BlockSpec((16, tn), lambda j: (0, j))],
                  sem=("arbitrary",))(c16, w_ada, b_ada)


def _ada_bwd(act, dmod, name):
    n = dmod.shape[2]
    tm, tn = 256, _pick(n, 512, LANE)

    def body(a_ref, d_ref, o_ref):
        def term(b):
            return a_ref[b].astype(BF16).astype(F32) * d_ref[b].astype(BF16).astype(F32)

        acc = term(0)
        for b in range(1, 8):
            acc = acc + term(b)
        o_ref[...] = acc

    return _pcall(body, name=name, out_shape=jax.ShapeDtypeStruct((D, n), F32), grid=(D // tm, n // tn),
                  in_specs=[pl.BlockSpec((8, tm, 1), lambda i, j: (0, i, 0)), pl.BlockSpec((8, 1, tn), lambda i, j: (0, 0, j))],
                  out_specs=pl.BlockSpec((tm, tn), lambda i, j: (i, j)), sem=("parallel", "parallel"))(act, dmod)


def _norm_mod(x, g, sh, sc, name):
    t = x.shape[0]
    tm = _pick(t, 512, 8)

    def body(x_ref, g_ref, sh_ref, sc_ref, h_ref):
        xf = x_ref[...]
        r = lax.rsqrt(jnp.mean(xf * xf, axis=-1, keepdims=True) + EPS)
        h = (xf * r) * g_ref[...]
        h_ref[...] = (h * (1.0 + sc_ref[...]) + sh_ref[...]).astype(BF16)

    vec = pl.BlockSpec((1, D), lambda i: (0, 0))
    row = pl.BlockSpec((tm, D), lambda i: (i, 0))
    return _pcall(body, name=name, out_shape=jax.ShapeDtypeStruct((t, D), BF16), grid=(t // tm,),
                  in_specs=[row, vec, vec, vec], out_specs=row, sem=("parallel",))(x, g, sh, sc)


def _norm_mod_bwd(dh, x, dxo, g, sc, name):
    t = x.shape[0]
    tm = _pick(t, 256, 8)

    def body(dh_ref, x_ref, dxo_ref, g_ref, sc_ref, dx_ref, dsh_ref, dsc_ref, dg_ref):
        @pl.when(pl.program_id(0) == 0)
        def _():
            dsh_ref[...] = jnp.zeros_like(dsh_ref)
            dsc_ref[...] = jnp.zeros_like(dsc_ref)
            dg_ref[...] = jnp.zeros_like(dg_ref)

        xf, dh_ = x_ref[...], dh_ref[...]
        r = lax.rsqrt(jnp.mean(xf * xf, axis=-1, keepdims=True) + EPS)
        xhat = xf * r
        dsh_ref[...] += jnp.sum(dh_, axis=0, keepdims=True)
        dsc_ref[...] += jnp.sum(dh_ * (xhat * g_ref[...]), axis=0, keepdims=True)
        tt = dh_ * (1.0 + sc_ref[...])
        dg_ref[...] += jnp.sum(tt * xhat, axis=0, keepdims=True)
        dxh = tt * g_ref[...]
        dx_ref[...] = r * (dxh - xhat * jnp.mean(dxh * xhat, axis=-1, keepdims=True)) + dxo_ref[...]

    vec = pl.BlockSpec((1, D), lambda i: (0, 0))
    row = pl.BlockSpec((tm, D), lambda i: (i, 0))
    vshape = jax.ShapeDtypeStruct((1, D), F32)
    return _pcall(body, name=name, out_shape=[jax.ShapeDtypeStruct((t, D), F32), vshape, vshape, vshape], grid=(t // tm,),
                  in_specs=[row, row, row, vec, vec], out_specs=[row, vec, vec, vec], sem=("arbitrary",))(dh, x, dxo, g, sc)


def _gate_bwd(dxo, yy, gate, name):
    t = dxo.shape[0]
    tm = _pick(t, 512, 8)

    def body(dx_ref, y_ref, g_ref, dy_ref, dg_ref):
        @pl.when(pl.program_id(0) == 0)
        def _():
            dg_ref[...] = jnp.zeros_like(dg_ref)

        dx = dx_ref[...]
        dy_ref[...] = (dx * g_ref[...]).astype(BF16)
        dg_ref[...] += jnp.sum(dx * y_ref[...], axis=0, keepdims=True)

    vec = pl.BlockSpec((1, D), lambda i: (0, 0))
    row = pl.BlockSpec((tm, D), lambda i: (i, 0))
    return _pcall(body, name=name, out_shape=[jax.ShapeDtypeStruct((t, D), BF16), jax.ShapeDtypeStruct((1, D), F32)],
                  grid=(t // tm,), in_specs=[row, row, vec], out_specs=[row, vec], sem=("arbitrary",))(dxo, yy, gate)


def _final_loss(x, g, tgt, name):
    t = x.shape[0]
    tm = _pick(t, 256, 8)

    def body(x_ref, g_ref, t_ref, loss_ref, dx_ref, dg_ref):
        @pl.when(pl.program_id(0) == 0)
        def _():
            loss_ref[...] = jnp.zeros_like(loss_ref)
            dg_ref[...] = jnp.zeros_like(dg_ref)

        xf = x_ref[...]
        r = lax.rsqrt(jnp.mean(xf * xf, axis=-1, keepdims=True) + EPS)
        xhat = xf * r
        e = xhat * g_ref[...] - t_ref[...]
        per_tok = jnp.mean(e * e, axis=-1, keepdims=True)
        loss_ref[...] += 0.5 * jnp.sum(per_tok, axis=0, keepdims=True)
        dy = e * (1.0 / D)
        dg_ref[...] += jnp.sum(dy * xhat, axis=0, keepdims=True)
        dxh = dy * g_ref[...]
        dx_ref[...] = r * (dxh - xhat * jnp.mean(dxh * xhat, axis=-1, keepdims=True))

    vec = pl.BlockSpec((1, D), lambda i: (0, 0))
    row = pl.BlockSpec((tm, D), lambda i: (i, 0))
    return _pcall(body, name=name,
                  out_shape=[jax.ShapeDtypeStruct((1, LANE), F32), jax.ShapeDtypeStruct((t, D), F32),
                             jax.ShapeDtypeStruct((1, D), F32)],
                  grid=(t // tm,), in_specs=[row, vec, row],
                  out_specs=[pl.BlockSpec((1, LANE), lambda i: (0, 0)), row, vec], sem=("arbitrary",))(x, g, tgt)


def _mm(a, b, out_dtype, name):
    m, k = a.shape
    n = b.shape[1]
    tm = _pick(m, 1024, 8)
    tn = _pick(n, 1408, LANE)

    def body(a_ref, b_ref, o_ref):
        o_ref[...] = _dot(a_ref[...], b_ref[...]).astype(out_dtype)

    return _pcall(body, name=name, out_shape=jax.ShapeDtypeStruct((m, n), out_dtype), grid=(m // tm, n // tn),
                  in_specs=[pl.BlockSpec((tm, k), lambda i, j: (i, 0)), pl.BlockSpec((k, tn), lambda i, j: (0, j))],
                  out_specs=pl.BlockSpec((tm, tn), lambda i, j: (i, j)), sem=("parallel", "parallel"))(a, b)


def _mm_resid(a, b, resid, gate, name):
    m, k = a.shape
    n = b.shape[1]
    tm, tn, tk = _pick(m, 1024, 8), _pick(n, 1024, LANE), _pick(k, 512, LANE)
    nk = k // tk

    def body(a_ref, b_ref, r_ref, g_ref, y_ref, o_ref, acc):
        kk = pl.program_id(2)

        @pl.when(kk == 0)
        def _():
            acc[...] = jnp.zeros_like(acc)

        acc[...] += _dot(a_ref[...], b_ref[...])

        @pl.when(kk == nk - 1)
        def _():
            y_ref[...] = acc[...]
            o_ref[...] = r_ref[...] + g_ref[...] * acc[...]

    blk = pl.BlockSpec((tm, tn), lambda i, j, kk: (i, j))
    return _pcall(body, name=name, out_shape=[jax.ShapeDtypeStruct((m, n), F32)] * 2, grid=(m // tm, n // tn, nk),
                  in_specs=[pl.BlockSpec((tm, tk), lambda i, j, kk: (i, kk)), pl.BlockSpec((tk, tn), lambda i, j, kk: (kk, j)),
                            blk, pl.BlockSpec((1, tn), lambda i, j, kk: (0, j))],
                  out_specs=[blk, blk], scratch=[pltpu.VMEM((tm, tn), F32)],
                  sem=("parallel", "parallel", "arbitrary"))(a, b, resid, gate)


def _mm_nt(a, b, out_dtype, name):
    m, k = a.shape
    n = b.shape[0]
    tm, tn, tk = _pick(m, 1024, 8), _pick(n, 1024, LANE), _pick(k, 1408, LANE)
    nk = k // tk

    def body(a_ref, b_ref, o_ref, acc):
        kk = pl.program_id(2)

        @pl.when(kk == 0)
        def _():
            acc[...] = jnp.zeros_like(acc)

        acc[...] += _dot(a_ref[...], b_ref[...], NT)

        @pl.when(kk == nk - 1)
        def _():
            o_ref[...] = acc[...].astype(out_dtype)

    return _pcall(body, name=name, out_shape=jax.ShapeDtypeStruct((m, n), out_dtype), grid=(m // tm, n // tn, nk),
                  in_specs=[pl.BlockSpec((tm, tk), lambda i, j, kk: (i, kk)), pl.BlockSpec((tn, tk), lambda i, j, kk: (j, kk))],
                  out_specs=pl.BlockSpec((tm, tn), lambda i, j, kk: (i, j)), scratch=[pltpu.VMEM((tm, tn), F32)],
                  sem=("parallel", "parallel", "arbitrary"))(a, b)


def _mm_tn(a, b, out_dtype, tm_cap, name):
    t, m = a.shape
    n = b.shape[1]
    tm, tn, tk = _pick(m, tm_cap, LANE), _pick(n, 1408, LANE), _pick(t, 512, 16)
    nk = t // tk

    def body(a_ref, b_ref, o_ref, acc):
        kk = pl.program_id(2)

        @pl.when(kk == 0)
        def _():
            acc[...] = jnp.zeros_like(acc)

        acc[...] += _dot(a_ref[...], b_ref[...], TN)

        @pl.when(kk == nk - 1)
        def _():
            o_ref[...] = acc[...].astype(out_dtype)

    return _pcall(body, name=name, out_shape=jax.ShapeDtypeStruct((m, n), out_dtype), grid=(m // tm, n // tn, nk),
                  in_specs=[pl.BlockSpec((tk, tm), lambda i, j, kk: (kk, i)), pl.BlockSpec((tk, tn), lambda i, j, kk: (kk, j))],
                  out_specs=pl.BlockSpec((tm, tn), lambda i, j, kk: (i, j)), scratch=[pltpu.VMEM((tm, tn), F32)],
                  sem=("parallel", "parallel", "arbitrary"))(a, b)


def _ffn_up(h, wg, name):
    t = h.shape[0]
    fp = wg.shape[2]
    tm, tn = _pick(t, 1024, 8), 256
    nn = fp // tn

    def body(h_ref, wa_ref, wb_ref, a_ref, b_ref, u_ref):
        hh = h_ref[...]
        a = _dot(hh, wa_ref[...])
        b = _dot(hh, wb_ref[...])
        a_ref[...] = a.astype(BF16)
        b_ref[...] = b.astype(BF16)
        u_ref[...] = (a * jax.nn.sigmoid(a) * b).astype(BF16)

    out = pl.BlockSpec((tm, tn), lambda i, j, n: (i, j * nn + n))
    return _pcall(body, name=name, out_shape=[jax.ShapeDtypeStruct((t, 2 * fp), BF16)] * 3, grid=(t // tm, 2, nn),
                  in_specs=[pl.BlockSpec((tm, D), lambda i, j, n: (i, 0)),
                            pl.BlockSpec((None, D, tn), lambda i, j, n: (j, 0, n)),
                            pl.BlockSpec((None, D, tn), lambda i, j, n: (j + 2, 0, n))],
                  out_specs=[out, out, out], sem=("parallel", "parallel", "parallel"))(h, wg, wg)


def _ffn_dab(dy, wo, a, b, name):
    t = dy.shape[0]
    f2 = wo.shape[0]
    tm, tn = _pick(t, 1024, 8), 256

    def body(dy_ref, w_ref, a_ref, b_ref, o_ref):
        du = _dot(dy_ref[...], w_ref[...], NT)
        av, bv = a_ref[...].astype(F32), b_ref[...].astype(F32)
        sg = jax.nn.sigmoid(av)
        o_ref[0] = (du * bv * (sg * (1.0 + av * (1.0 - sg)))).astype(BF16)
        o_ref[1] = (du * (av * sg)).astype(BF16)

    blk = pl.BlockSpec((tm, tn), lambda i, n: (i, n))
    return _pcall(body, name=name, out_shape=jax.ShapeDtypeStruct((2, t, f2), BF16), grid=(t // tm, f2 // tn),
                  in_specs=[pl.BlockSpec((tm, D), lambda i, n: (i, 0)), pl.BlockSpec((tn, D), lambda i, n: (n, 0)), blk, blk],
                  out_specs=pl.BlockSpec((2, tm, tn), lambda i, n: (0, i, n)), sem=("parallel", "parallel"))(dy, wo, a, b)


def _ffn_dwin(h, dab, name):
    t = h.shape[0]
    fp = dab.shape[2] // 2
    tm, tn, tk = 1024, _pick(fp, 1408, LANE), _pick(t, 512, 16)
    nn, nk = fp // tn, t // tk

    def body(h_ref, d_ref, o_ref, acc):
        kk = pl.program_id(3)

        @pl.when(kk == 0)
        def _():
            acc[...] = jnp.zeros_like(acc)

        acc[...] += _dot(h_ref[...], d_ref[...], TN)

        @pl.when(kk == nk - 1)
        def _():
            o_ref[...] = acc[...].astype(BF16)

    return _pcall(body, name=name, out_shape=jax.ShapeDtypeStruct((4, D, fp), BF16), grid=(4, D // tm, nn, nk),
                  in_specs=[pl.BlockSpec((tk, tm), lambda u, i, n, kk: (kk, i)),
                            pl.BlockSpec((None, tk, tn), lambda u, i, n, kk: (u // 2, kk, (u % 2) * nn + n))],
                  out_specs=pl.BlockSpec((None, tm, tn), lambda u, i, n, kk: (u, i, n)),
                  scratch=[pltpu.VMEM((tm, tn), F32)],
                  sem=("parallel", "parallel", "parallel", "arbitrary"))(h, dab)


def _ffn_dh(dab, wg, name):
    t = dab.shape[1]
    fp = wg.shape[2]
    tm, tn, tk = _pick(t, 1024, 8), 1024, _pick(fp, 1408, LANE)
    nkk = fp // tk

    def body(d_ref, w_ref, o_ref, acc):
        u, kk = pl.program_id(2), pl.program_id(3)

        @pl.when((u == 0) & (kk == 0))
        def _():
            acc[...] = jnp.zeros_like(acc)

        acc[...] += _dot(d_ref[...], w_ref[...], NT)

        @pl.when((u == 3) & (kk == nkk - 1))
        def _():
            o_ref[...] = acc[...]

    return _pcall(body, name=name, out_shape=jax.ShapeDtypeStruct((t, D), F32), grid=(t // tm, D // tn, 4, nkk),
                  in_specs=[pl.BlockSpec((None, tm, tk), lambda i, j, u, kk: (u // 2, i, (u % 2) * nkk + kk)),
                            pl.BlockSpec((None, tn, tk), lambda i, j, u, kk: (u, j, kk))],
                  out_specs=pl.BlockSpec((tm, tn), lambda i, j, u, kk: (i, j)), scratch=[pltpu.VMEM((tm, tn), F32)],
                  sem=("parallel", "parallel", "arbitrary", "arbitrary"))(dab, wg)


def _split3(v):
    hi = v.astype(BF16)
    r1 = v - hi.astype(F32)
    mid = r1.astype(BF16)
    lo = (r1 - mid.astype(F32)).astype(BF16)
    return hi, mid, lo


def _tri_sum(tri, v):
    hi, mid, lo = _split3(v)
    return (_dot(tri, hi) + _dot(tri, mid)) + _dot(tri, lo)


def _forget_fwd(proj, b_pad, name):
    t = proj.shape[0]
    tb = _pick(t, 256, 8)
    col = D_PROJ // LANE

    def body(f_ref, b_ref, o_ref, carry):
        @pl.when(pl.program_id(0) == 0)
        def _():
            carry[...] = jnp.zeros_like(carry)

        z = f_ref[...] + b_ref[...]
        lf = jnp.minimum(z, 0.0) - jnp.log(1.0 + jnp.exp(-jnp.abs(z)))
        r = lax.broadcasted_iota(jnp.int32, (tb, tb), 0)
        cidx = lax.broadcasted_iota(jnp.int32, (tb, tb), 1)
        tri = (r >= cidx).astype(BF16)
        o_ref[...] = _tri_sum(tri, lf) + carry[...]
        carry[...] += jnp.sum(lf, axis=0, keepdims=True)

    return _pcall(body, name=name, out_shape=jax.ShapeDtypeStruct((t, LANE), F32), grid=(t // tb,),
                  in_specs=[pl.BlockSpec((tb, LANE), lambda i: (i, col)), pl.BlockSpec((1, LANE), lambda i: (0, 0))],
                  out_specs=pl.BlockSpec((tb, LANE), lambda i: (i, 0)), scratch=[pltpu.VMEM((1, LANE), F32)],
                  sem=("arbitrary",))(proj, b_pad)


def _forget_bwd(d_cum, proj, b_pad, name):
    t = proj.shape[0]
    tb = _pick(t, 256, 8)
    nb = t // tb
    col = D_PROJ // LANE

    def body(d_ref, f_ref, b_ref, o_ref, db_ref, carry):
        @pl.when(pl.program_id(0) == 0)
        def _():
            carry[...] = jnp.zeros_like(carry)
            db_ref[...] = jnp.zeros_like(db_ref)

        dc = d_ref[...]
        r = lax.broadcasted_iota(jnp.int32, (tb, tb), 0)
        cidx = lax.broadcasted_iota(jnp.int32, (tb, tb), 1)
        tri = (r <= cidx).astype(BF16)
        dlf = _tri_sum(tri, dc) + carry[...]
        carry[...] += jnp.sum(dc, axis=0, keepdims=True)
        z = f_ref[...] + b_ref[...]
        lane = lax.broadcasted_iota(jnp.int32, (tb, LANE), 1)
        dz = jnp.where(lane < N_HEADS, dlf * jax.nn.sigmoid(-z), 0.0)
        o_ref[...] = dz.astype(BF16)
        db_ref[...] += jnp.sum(dz, axis=0, keepdims=True)

    return _pcall(body, name=name, out_shape=[jax.ShapeDtypeStruct((t, LANE), BF16), jax.ShapeDtypeStruct((1, LANE), F32)],
                  grid=(nb,),
                  in_specs=[pl.BlockSpec((tb, LANE), lambda i: (nb - 1 - i, 0)),
                            pl.BlockSpec((tb, LANE), lambda i: (nb - 1 - i, col)),
                            pl.BlockSpec((1, LANE), lambda i: (0, 0))],
                  out_specs=[pl.BlockSpec((tb, LANE), lambda i: (nb - 1 - i, 0)), pl.BlockSpec((1, LANE), lambda i: (0, 0))],
                  scratch=[pltpu.VMEM((1, LANE), F32)], sem=("arbitrary",))(d_cum, proj, b_pad)


def _head_norm(v, g):
    r = lax.rsqrt(jnp.mean(v * v, axis=-1, keepdims=True) + EPS)
    return v * r, r


def _qkv_prep(proj, qg, kg, name):
    t = proj.shape[0]
    tm = _pick(t, 1024, 8)

    def body(q_ref, k_ref, v_ref, qg_ref, kg_ref, qo_ref, ko_ref, vo_ref):
        qo_ref[...] = (_head_norm(q_ref[...], None)[0] * qg_ref[...]).astype(BF16)
        ko_ref[...] = (_head_norm(k_ref[...], None)[0] * kg_ref[...]).astype(BF16)
        vo_ref[...] = v_ref[...].astype(BF16)

    def blk(off):
        return pl.BlockSpec((tm, HEAD_DIM), lambda i, h: (i, off + h))

    vec = pl.BlockSpec((1, HEAD_DIM), lambda i, h: (0, 0))
    return _pcall(body, name=name, out_shape=[jax.ShapeDtypeStruct((t, D_ATTN), BF16)] * 3, grid=(t // tm, N_HEADS),
                  in_specs=[blk(0), blk(N_HEADS), blk(2 * N_HEADS), vec, vec], out_specs=[blk(0)] * 3,
                  sem=("parallel", "parallel"))(proj, proj, proj, qg, kg)


def _qk_norm_bwd(dqn, dkn, proj, qg, kg, name):
    t = proj.shape[0]
    tm = _pick(t, 1024, 8)

    def one(d_ref, v_ref, g_ref, o_ref, dg_ref):
        xhat, r = _head_norm(v_ref[...], None)
        d = d_ref[...]
        dg_ref[...] += jnp.sum(d * xhat, axis=0, keepdims=True)
        dxh = d * g_ref[...]
        o_ref[...] = (r * (dxh - xhat * jnp.mean(dxh * xhat, axis=-1, keepdims=True))).astype(BF16)

    def body(dq_ref, dk_ref, q_ref, k_ref, qg_ref, kg_ref, qo_ref, ko_ref, dqg_ref, dkg_ref):
        @pl.when((pl.program_id(0) == 0) & (pl.program_id(1) == 0))
        def _():
            dqg_ref[...] = jnp.zeros_like(dqg_ref)
            dkg_ref[...] = jnp.zeros_like(dkg_ref)

        one(dq_ref, q_ref, qg_ref, qo_ref, dqg_ref)
        one(dk_ref, k_ref, kg_ref, ko_ref, dkg_ref)

    def blk(off):
        return pl.BlockSpec((tm, HEAD_DIM), lambda i, h: (i, off + h))

    vec = pl.BlockSpec((1, HEAD_DIM), lambda i, h: (0, 0))
    vshape = jax.ShapeDtypeStruct((1, HEAD_DIM), F32)
    return _pcall(body, name=name, out_shape=[jax.ShapeDtypeStruct((t, D_ATTN), BF16)] * 2 + [vshape, vshape],
                  grid=(t // tm, N_HEADS),
                  in_specs=[blk(0), blk(0), blk(0), blk(N_HEADS), vec, vec], out_specs=[blk(0), blk(0), vec, vec],
                  sem=("arbitrary", "arbitrary"))(dqn, dkn, proj, proj, qg, kg)


ATTN_SCALE = HEAD_DIM ** -0.5


def _logits(q, k, fq, fk, diag, tq, tk):
    s = _dot(q, k, NT) * ATTN_SCALE + (fq - fk)
    if diag:
        r = lax.broadcasted_iota(jnp.int32, (tq, tk), 0)
        cidx = lax.broadcasted_iota(jnp.int32, (tq, tk), 1)
        s = jnp.where(r >= cidx, s, -jnp.inf)
    return s


def _attn_fwd(q, k, v, fq, fk, name):
    t = q.shape[0]
    tq = tk = _pick(t, ATTN_BLOCK, LANE)
    nk = t // tk

    def body(q_ref, k_ref, v_ref, fq_ref, fk_ref, o_ref, lse_ref, m_s, l_s, acc):
        i, j = pl.program_id(1), pl.program_id(2)

        @pl.when(j == 0)
        def _():
            m_s[...] = jnp.full_like(m_s, -jnp.inf)
            l_s[...] = jnp.zeros_like(l_s)
            acc[...] = jnp.zeros_like(acc)

        def step(diag):
            s = _logits(q_ref[...], k_ref[...], fq_ref[...], fk_ref[...], diag, tq, tk)
            m_new = jnp.maximum(m_s[...], jnp.max(s, axis=-1, keepdims=True))
            alpha = jnp.exp(m_s[...] - m_new)
            p = jnp.exp(s - m_new)
            l_s[...] = alpha * l_s[...] + jnp.sum(p, axis=-1, keepdims=True)
            acc[...] = alpha * acc[...] + _dot(p.astype(BF16), v_ref[...])
            m_s[...] = m_new

        @pl.when(j < i)
        def _():
            step(False)

        @pl.when(j == i)
        def _():
            step(True)

        @pl.when(j == nk - 1)
        def _():
            o_ref[...] = (acc[...] / l_s[...]).astype(BF16)
            lse_ref[...] = m_s[...] + jnp.log(l_s[...])

    qb = pl.BlockSpec((tq, HEAD_DIM), lambda h, i, j: (i, h))
    kb = pl.BlockSpec((tk, HEAD_DIM), lambda h, i, j: (jnp.minimum(j, i), h))
    col = pl.BlockSpec((None, tq, 1), lambda h, i, j: (h, i, 0))
    return _pcall(body, name=name,
                  out_shape=[jax.ShapeDtypeStruct((t, D_ATTN), BF16), jax.ShapeDtypeStruct((N_HEADS, t, 1), F32)],
                  grid=(N_HEADS, t // tq, nk),
                  in_specs=[qb, kb, kb, col, pl.BlockSpec((None, 1, tk), lambda h, i, j: (h, 0, jnp.minimum(j, i)))],
                  out_specs=[qb, col],
                  scratch=[pltpu.VMEM((tq, 1), F32), pltpu.VMEM((tq, 1), F32), pltpu.VMEM((tq, HEAD_DIM), F32)],
                  sem=("parallel", "parallel", "arbitrary"))(q, k, v, fq, fk)


def _attn_bwd_q(q, k, v, o, do, lse, fq, fk, name):
    t = q.shape[0]
    tq = tk = _pick(t, ATTN_BLOCK, LANE)
    nk = t // tk

    def body(q_ref, k_ref, v_ref, o_ref, do_ref, lse_ref, fq_ref, fk_ref, dq_ref, dl_ref, dfq_ref, acc, dl_s, df_s):
        i, j = pl.program_id(1), pl.program_id(2)

        @pl.when(j == 0)
        def _():
            acc[...] = jnp.zeros_like(acc)
            df_s[...] = jnp.zeros_like(df_s)
            dl_s[...] = jnp.sum(do_ref[...].astype(F32) * o_ref[...].astype(F32), axis=-1, keepdims=True)

        def step(diag):
            s = _logits(q_ref[...], k_ref[...], fq_ref[...], fk_ref[...], diag, tq, tk)
            p = jnp.exp(s - lse_ref[...])
            dp = _dot(do_ref[...], v_ref[...], NT)
            ds = p * (dp - dl_s[...])
            df_s[...] += jnp.sum(ds, axis=-1, keepdims=True)
            acc[...] += _dot(ds.astype(BF16), k_ref[...])

        @pl.when(j < i)
        def _():
            step(False)

        @pl.when(j == i)
        def _():
            step(True)

        @pl.when(j == nk - 1)
        def _():
            dq_ref[...] = acc[...] * ATTN_SCALE
            dl_ref[...] = dl_s[...]
            dfq_ref[...] = df_s[...]

    qb = pl.BlockSpec((tq, HEAD_DIM), lambda h, i, j: (i, h))
    kb = pl.BlockSpec((tk, HEAD_DIM), lambda h, i, j: (jnp.minimum(j, i), h))
    col = pl.BlockSpec((None, tq, 1), lambda h, i, j: (h, i, 0))
    cshape = jax.ShapeDtypeStruct((N_HEADS, t, 1), F32)
    return _pcall(body, name=name, out_shape=[jax.ShapeDtypeStruct((t, D_ATTN), F32), cshape, cshape],
                  grid=(N_HEADS, t // tq, nk),
                  in_specs=[qb, kb, kb, qb, qb, col, col, pl.BlockSpec((None, 1, tk), lambda h, i, j: (h, 0, jnp.minimum(j, i)))],
                  out_specs=[qb, col, col],
                  scratch=[pltpu.VMEM((tq, HEAD_DIM), F32), pltpu.VMEM((tq, 1), F32), pltpu.VMEM((tq, 1), F32)],
                  sem=("parallel", "parallel", "arbitrary"))(q, k, v, o, do, lse, fq, fk)


def _attn_bwd_kv(q, k, v, do, lse, delta, fq, fk, name):
    t = q.shape[0]
    tq = tk = _pick(t, ATTN_BLOCK, LANE)
    nq = t // tq

    def body(q_ref, k_ref, v_ref, do_ref, lse_ref, dl_ref, fq_ref, fk_ref, dk_ref, dv_ref, dfk_ref, dk_s, dv_s, df_s):
        j, i = pl.program_id(1), pl.program_id(2)

        @pl.when(i == 0)
        def _():
            dk_s[...] = jnp.zeros_like(dk_s)
            dv_s[...] = jnp.zeros_like(dv_s)
            df_s[...] = jnp.zeros_like(df_s)

        def step(diag):
            s = _logits(q_ref[...], k_ref[...], fq_ref[...], fk_ref[...], diag, tq, tk)
            p = jnp.exp(s - lse_ref[...])
            dv_s[...] += _dot(p.astype(BF16), do_ref[...], TN)
            dp = _dot(do_ref[...], v_ref[...], NT)
            ds = p * (dp - dl_ref[...])
            df_s[...] -= jnp.sum(ds, axis=0, keepdims=True)
            dk_s[...] += _dot(ds.astype(BF16), q_ref[...], TN)

        @pl.when(i > j)
        def _():
            step(False)

        @pl.when(i == j)
        def _():
            step(True)

        @pl.when(i == nq - 1)
        def _():
            dk_ref[...] = dk_s[...] * ATTN_SCALE
            dv_ref[...] = dv_s[...].astype(BF16)
            dfk_ref[...] = df_s[...]

    qb = pl.BlockSpec((tq, HEAD_DIM), lambda h, j, i: (jnp.maximum(i, j), h))
    kb = pl.BlockSpec((tk, HEAD_DIM), lambda h, j, i: (j, h))
    col = pl.BlockSpec((None, tq, 1), lambda h, j, i: (h, jnp.maximum(i, j), 0))
    row = pl.BlockSpec((None, 1, tk), lambda h, j, i: (h, 0, j))
    return _pcall(body, name=name,
                  out_shape=[jax.ShapeDtypeStruct((t, D_ATTN), F32), jax.ShapeDtypeStruct((t, D_ATTN), BF16),
                             jax.ShapeDtypeStruct((N_HEADS, 1, t), F32)],
                  grid=(N_HEADS, t // tk, nq),
                  in_specs=[qb, kb, kb, qb, col, col, col, row], out_specs=[kb, kb, row],
                  scratch=[pltpu.VMEM((tk, HEAD_DIM), F32), pltpu.VMEM((tk, HEAD_DIM), F32), pltpu.VMEM((1, tk), F32)],
                  sem=("parallel", "parallel", "arbitrary"))(q, k, v, do, lse, delta, fq, fk)


def _window_sum(v, w, back):
    n = v.shape[0]
    k = 1
    while k < w:
        v = v + pltpu.roll(v, k if back else n - k, axis=0)
        k *= 2
    return v


def _pool_fwd(proj, pw, ps, name):
    t = proj.shape[0]
    tm = _pick(t, POOL_BLOCK, HALO)
    col = 3 * D_ATTN // D_POOL

    def body(u_ref, prev_ref, pw_ref, ps_ref, pooled_ref, out_ref):
        i = pl.program_id(0)
        prev = jnp.where(i > 0, prev_ref[...], 0.0)
        ext = jnp.concatenate([prev, u_ref[...]], axis=0)
        pos = i * tm + lax.broadcasted_iota(jnp.int32, (tm, 1), 0)
        for g, w in enumerate(POOL_WINDOWS):
            cols = slice(g * GROUP_DIM, (g + 1) * GROUP_DIM)
            xg = ext[:, cols]
            sw = _window_sum(xg, w, True)[HALO:, :]
            cnt = jnp.minimum(pos + 1, w).astype(F32)
            pooled = (sw / cnt - xg[HALO:, :]).astype(BF16)
            pooled_ref[:, cols] = pooled
            out_ref[:, cols] = (_dot(pooled, pw_ref[g]) * ps_ref[:, cols]).astype(BF16)

    row = pl.BlockSpec((tm, D_POOL), lambda i: (i, 0))
    return _pcall(body, name=name, out_shape=[jax.ShapeDtypeStruct((t, D_POOL), BF16)] * 2, grid=(t // tm,),
                  in_specs=[pl.BlockSpec((tm, D_POOL), lambda i: (i, col)),
                            pl.BlockSpec((HALO, D_POOL), lambda i: (jnp.maximum(i * (tm // HALO) - 1, 0), col)),
                            pl.BlockSpec((len(POOL_WINDOWS), GROUP_DIM, GROUP_DIM), lambda i: (0, 0, 0)),
                            pl.BlockSpec((1, D_POOL), lambda i: (0, 0))],
                  out_specs=[row, row], sem=("parallel",))(proj, proj, pw, ps)


def _pool_bwd(dout, pooled, pw, ps, name):
    t = pooled.shape[0]
    tm = _pick(t, POOL_BLOCK, HALO)
    nb = t // tm
    ng = len(POOL_WINDOWS)

    def body(d_ref, nxt_ref, p_ref, pw_ref, ps_ref, du_ref, dpw_ref, dps_ref):
        i = pl.program_id(0)

        @pl.when(i == 0)
        def _():
            dpw_ref[...] = jnp.zeros_like(dpw_ref)
            dps_ref[...] = jnp.zeros_like(dps_ref)

        nxt = jnp.where(i < nb - 1, nxt_ref[...].astype(F32), 0.0)
        ext = jnp.concatenate([d_ref[...].astype(F32), nxt], axis=0)
        pos = i * tm + lax.broadcasted_iota(jnp.int32, (tm + HALO, 1), 0)
        for g, w in enumerate(POOL_WINDOWS):
            cols = slice(g * GROUP_DIM, (g + 1) * GROUP_DIM)
            pooled_g = p_ref[:, cols]
            dg = ext[:, cols]
            pm = _dot(pooled_g, pw_ref[g])
            dps_ref[:, cols] += jnp.sum(dg[:tm, :] * pm, axis=0, keepdims=True)
            dpm = (dg * ps_ref[:, cols]).astype(BF16)
            dpw_ref[g] += _dot(pooled_g, dpm[:tm, :], TN)
            dpooled = _dot(dpm, pw_ref[g], NT)
            cnt = jnp.minimum(pos + 1, w).astype(F32)
            fwd = _window_sum(dpooled / cnt, w, False)
            du_ref[:, cols] = (fwd[:tm, :] - dpooled[:tm, :]).astype(BF16)

    row = pl.BlockSpec((tm, D_POOL), lambda i: (i, 0))
    return _pcall(body, name=name,
                  out_shape=[jax.ShapeDtypeStruct((t, D_POOL), BF16), jax.ShapeDtypeStruct((ng, GROUP_DIM, GROUP_DIM), F32),
                             jax.ShapeDtypeStruct((1, D_POOL), F32)],
                  grid=(nb,),
                  in_specs=[pl.BlockSpec((tm, D_POOL), lambda i: (i, 1)),
                            pl.BlockSpec((HALO, D_POOL), lambda i: (jnp.minimum((i + 1) * (tm // HALO), t // HALO - 1), 1)),
                            row, pl.BlockSpec((ng, GROUP_DIM, GROUP_DIM), lambda i: (0, 0, 0)),
                            pl.BlockSpec((1, D_POOL), lambda i: (0, 0))],
                  out_specs=[row, pl.BlockSpec((ng, GROUP_DIM, GROUP_DIM), lambda i: (0, 0, 0)),
                             pl.BlockSpec((1, D_POOL), lambda i: (0, 0))],
                  sem=("arbitrary",))(dout, dout, pooled, pw, ps)


def _pad_cols(w, n):
    return jnp.pad(w, ((0, 0), (0, n - w.shape[1])))


def _ffn_fwd(x, norm_g, sh, sc, gate_half, wg_in, wg_out, tag):
    h = _norm_mod(x, norm_g, sh, sc, f"{tag}_norm")
    a, b, u = _ffn_up(h, wg_in, f"{tag}_up")
    yy, xo = _mm_resid(u, wg_out, x, gate_half, f"{tag}_down")
    return xo, (x, h, a, b, u, yy)


def _ffn_bwd(dxo, saved, norm_g, sc, gate_half, wg_in, wg_out, hu, tag):
    x, h, a, b, u, yy = saved
    dy, dgate = _gate_bwd(dxo, yy, gate_half, f"{tag}_gate_bwd")
    dab = _ffn_dab(dy, wg_out, a, b, f"{tag}_dab")
    d_wout = _mm_tn(u, dy, BF16, hu, f"{tag}_dwout")
    d_win = _ffn_dwin(h, dab, f"{tag}_dwin")
    dh = _ffn_dh(dab, wg_in, f"{tag}_dh")
    dx, dsh, dsc, dng = _norm_mod_bwd(dh, x, dxo, norm_g, sc, f"{tag}_norm_bwd")
    return dx, d_win, d_wout.reshape(4, hu, D), dsh, dsc, 0.5 * dgate, dng


def kernel(x, c, w_ada, b_ada, ffn1_norm_g, ffn1_w_in, ffn1_w_out, mix_norm_g, w_in, b_forget, q_norm_g, k_norm_g, pool_w, pool_scale, w_out, ffn2_norm_g, ffn2_w_in, ffn2_w_out, final_norm_g, loss_target, m_w_ada, m_b_ada, m_ffn1_norm_g, m_ffn1_w_in, m_ffn1_w_out, m_mix_norm_g, m_w_in, m_b_forget, m_q_norm_g, m_k_norm_g, m_pool_w, m_pool_scale, m_w_out, m_ffn2_norm_g, m_ffn2_w_in, m_ffn2_w_out, m_final_norm_g, v_w_ada, v_b_ada, v_ffn1_norm_g, v_ffn1_w_in, v_ffn1_w_out, v_mix_norm_g, v_w_in, v_b_forget, v_q_norm_g, v_k_norm_g, v_pool_w, v_pool_scale, v_w_out, v_ffn2_norm_g, v_ffn2_w_in, v_ffn2_w_out, v_final_norm_g):
    ax, ay, ac = _coords()
    chip = 2 * ax + ay
    me = 2 * chip + ac
    chip_arr = jnp.reshape(chip, (1,)).astype(jnp.int32)
    core_arr = jnp.reshape(ac, (1,)).astype(jnp.int32)

    t = x.shape[1]
    xs = x.reshape(t, D)
    tgt = loss_target.reshape(t, D)
    hu = ffn1_w_out.shape[1]
    hup = -(-hu // LANE) * LANE
    ws_in = w_in.shape[2]
    ws_in_pad = -(-ws_in // LANE) * LANE
    n_ada = w_ada.shape[2]

    def ffn_in_shard(w):
        w = w[0].astype(BF16)
        return jnp.concatenate([_pad_cols(w[:, :hu], hup), _pad_cols(w[:, hu:], hup)], axis=1)

    def ffn_out_shard(w):
        return jnp.pad(w[0].astype(BF16), ((0, hup - hu), (0, 0)))

    shards = [ffn_in_shard(ffn1_w_in), ffn_out_shard(ffn1_w_out), _pad_cols(w_in[0].astype(BF16), ws_in_pad),
              pool_w[0].astype(BF16).reshape(GROUP_DIM, GROUP_DIM), w_out[0].astype(BF16),
              ffn_in_shard(ffn2_w_in), ffn_out_shard(ffn2_w_out)]
    g_in1, g_out1, g_win, g_pw, g_wout, g_in2, g_out2 = _gather_weights(shards, "gather_weights")
    wg_out1 = g_out1.reshape(4 * hup, D)
    wg_out2 = g_out2.reshape(4 * hup, D)
    w_full = jnp.concatenate([g_win[k, :, :ws_in] for k in range(4)], axis=1)
    nf = 3 * D_ATTN
    w_all = jnp.concatenate([w_full[:, :nf], w_full[:, nf + N_HEADS:], w_full[:, nf:nf + N_HEADS],
                             jnp.zeros((D, LANE - N_HEADS), BF16)], axis=1)
    pw_full = g_pw.reshape(4, 4, GROUP_DIM // 4, GROUP_DIM).transpose(1, 0, 2, 3).reshape(4, GROUP_DIM, GROUP_DIM)
    wo_full = g_wout.reshape(4 * g_wout.shape[1], D)

    c_all = _allgather_small(c.reshape(8, D // 8), True, "gather_c").reshape(8, D)
    c16 = jnp.pad(c_all, ((0, 8), (0, 0)))
    b_ada_mine = lax.dynamic_slice(b_ada, (0, chip * n_ada), (1, n_ada))
    act16, mod16 = _ada_fwd(c16, w_ada[0], b_ada_mine, "ada_fwd")
    mod_all = _allgather_small(mod16[:8], False, "gather_mod")
    mod = lax.dynamic_index_in_dim(mod_all, me, axis=1, keepdims=False).reshape(N_MOD, 1, D)
    sh1, sc1, gt1, sh2, sc2, gt2, sh3, sc3, gt3 = [mod[k] for k in range(N_MOD)]

    x1, saved1 = _ffn_fwd(xs, ffn1_norm_g, sh1, sc1, 0.5 * gt1, g_in1, wg_out1, "ffn1")

    h2 = _norm_mod(x1, mix_norm_g, sh2, sc2, "mix_norm")
    proj = _mm(h2, w_all, F32, "mix_proj")
    b_pad = jnp.pad(b_forget, ((0, 0), (0, LANE - N_HEADS)))
    cum = _forget_fwd(proj, b_pad, "forget_fwd")
    cum_t = cum[:, :N_HEADS].T
    fq, fk = cum_t.reshape(N_HEADS, t, 1), cum_t.reshape(N_HEADS, 1, t)
    qn, kn, vb = _qkv_prep(proj, q_norm_g, k_norm_g, "qkv_prep")
    attn, lse = _attn_fwd(qn, kn, vb, fq, fk, "attn_fwd")
    pooled, pool_out = _pool_fwd(proj, pw_full, pool_scale, "pool_fwd")
    cat = jnp.concatenate([attn, pool_out], axis=1)
    y_mix, x2 = _mm_resid(cat, wo_full, x1, gt2, "mix_out")

    x3, saved3 = _ffn_fwd(x2, ffn2_norm_g, sh3, sc3, 0.5 * gt3, g_in2, wg_out2, "ffn2")

    loss_part, dx3, d_final_g = _final_loss(x3, final_norm_g.reshape(1, D), tgt, "final_loss")
    loss = lax.psum(loss_part[0, 0], ("x", "y", "c"))

    dx2, d_in2, d_out2, dsh3, dsc3, dgt3, d_ng3 = _ffn_bwd(dx3, saved3, ffn2_norm_g, sc3, 0.5 * gt3, g_in2, wg_out2, hup, "ffn2")

    dz, dgt2 = _gate_bwd(dx2, y_mix, gt2, "mix_gate_bwd")
    dcat = _mm_nt(dz, wo_full, BF16, "mix_dcat")
    d_wo = _mm_tn(cat, dz, BF16, 1024, "mix_dwout")
    du_pool, d_pw, d_ps = _pool_bwd(dcat, pooled, pw_full, pool_scale, "pool_bwd")
    dqn, delta, dfq = _attn_bwd_q(qn, kn, vb, attn, dcat, lse, fq, fk, "attn_bwd_q")
    dkn, dv, dfk = _attn_bwd_kv(qn, kn, vb, dcat, lse, delta, fq, fk, "attn_bwd_kv")
    d_cum = jnp.pad((dfq.reshape(N_HEADS, t) + dfk.reshape(N_HEADS, t)).T, ((0, 0), (0, LANE - N_HEADS)))
    dfl, d_bf = _forget_bwd(d_cum, proj, b_pad, "forget_bwd")
    dq, dk, d_qg, d_kg = _qk_norm_bwd(dqn, dkn, proj, q_norm_g, k_norm_g, "qk_norm_bwd")
    dproj = jnp.concatenate([dq, dk, dv, du_pool, dfl], axis=1)
    d_wall = _mm_tn(h2, dproj, F32, 1024, "mix_dwin")
    dh2 = _mm_nt(dproj, w_all, F32, "mix_dh")
    dx1, dsh2, dsc2, d_ng2 = _norm_mod_bwd(dh2, x1, dx2, mix_norm_g, sc2, "mix_norm_bwd")

    dx0, d_in1, d_out1, dsh1, dsc1, dgt1, d_ng1 = _ffn_bwd(dx1, saved1, ffn1_norm_g, sc1, 0.5 * gt1, g_in1, wg_out1, hup, "ffn1")
    grad_x = dx0.reshape(1, t, D)

    d_wfull = jnp.concatenate([d_wall[:, :nf], d_wall[:, D_PROJ:D_PROJ + N_HEADS], d_wall[:, nf:D_PROJ]], axis=1)
    p_win = jnp.stack([_pad_cols(d_wfull[:, k * ws_in:(k + 1) * ws_in], ws_in_pad) for k in range(4)]).astype(BF16)
    p_pw = d_pw.reshape(4, 4, GROUP_DIM // 4, GROUP_DIM).transpose(1, 0, 2, 3).reshape(4, GROUP_DIM, GROUP_DIM).astype(BF16)
    parts = [d_in1, d_out1, p_win, p_pw, d_wo.reshape(4, D // 4, D), d_in2, d_out2]
    from_sib = _reduce_siblings(parts, "reduce_siblings")
    halves = [_add_sibling(p, r, core_arr, f"add_sibling_{k}") for k, (p, r) in enumerate(zip(parts, from_sib))]
    from_chips = _reduce_chips(halves, "reduce_chips")
    mine = [_add_chips(hh, r, chip_arr, f"add_chips_{k}") for k, (hh, r) in enumerate(zip(halves, from_chips))]
    full = [g.reshape(2 * g.shape[1], g.shape[2]) for g in _share_siblings(mine, "share_siblings")]
    r_in1, r_out1, r_win, r_pw, r_wo, r_in2, r_out2 = full
    grads = {
        "ffn1_w_in": jnp.concatenate([r_in1[:, :hu], r_in1[:, hup:hup + hu]], axis=1),
        "ffn1_w_out": r_out1[:hu],
        "w_in": r_win[:, :ws_in],
        "pool_w": r_pw,
        "w_out": r_wo,
        "ffn2_w_in": jnp.concatenate([r_in2[:, :hu], r_in2[:, hup:hup + hu]], axis=1),
        "ffn2_w_out": r_out2[:hu],
    }

    dmod = jnp.concatenate([dsh1, dsc1, dgt1, dsh2, dsc2, dgt2, dsh3, dsc3, dgt3], axis=1)
    small_names = ["b_ada", "ffn1_norm_g", "mix_norm_g", "b_forget", "q_norm_g", "k_norm_g", "pool_scale", "ffn2_norm_g",
                   "final_norm_g"]
    small_grads = [dmod, d_ng1, d_ng2, d_bf[:, :N_HEADS], d_qg, d_kg, d_ps, d_ng3, d_final_g]
    small_w = [b_ada, ffn1_norm_g, mix_norm_g, b_forget, q_norm_g, k_norm_g, pool_scale, ffn2_norm_g, final_norm_g.reshape(1, D)]
    small_m = [m_b_ada, m_ffn1_norm_g, m_mix_norm_g, m_b_forget, m_q_norm_g, m_k_norm_g, m_pool_scale, m_ffn2_norm_g,
               m_final_norm_g.reshape(1, D)]
    small_v = [v_b_ada, v_ffn1_norm_g, v_mix_norm_g, v_b_forget, v_q_norm_g, v_k_norm_g, v_pool_scale, v_ffn2_norm_g,
               v_final_norm_g.reshape(1, D)]
    sizes = [g.shape[1] for g in small_grads]
    n_small = sum(sizes)
    n_pack = -(-n_small // (8 * LANE)) * (8 * LANE)

    def pack(vs, fill):
        flat = jnp.concatenate([v.reshape(1, -1) for v in vs], axis=1)
        return jnp.pad(flat, ((0, 0), (0, n_pack - n_small)), constant_values=fill).reshape(8, n_pack // 8)

    g8 = _allgather_small(pack(small_grads, 0.0), True, "gather_small_grads")
    gs, ds, ms, vs = _adamw_small(pack(small_w, 0.0), g8, pack(small_m, 0.0), pack(small_v, 1.0), "adamw_small")

    def unpack(p):
        flat = p.reshape(1, n_pack)
        out, off = {}, 0
        for nme, sz in zip(small_names, sizes):
            out[nme] = flat[:, off:off + sz]
            off += sz
        return out

    small = [unpack(p) for p in (gs, ds, ms, vs)]
    for dct in small:
        dct["final_norm_g"] = dct["final_norm_g"].reshape(D)

    dmod_all = g8.reshape(8, n_pack)[:, :N_MOD * D]
    dmod_mine = lax.dynamic_slice(dmod_all, (0, chip * n_ada), (8, n_ada))
    grads["w_ada"] = _ada_bwd(act16[:8].reshape(8, D, 1), dmod_mine.reshape(8, 1, n_ada), "ada_bwd")

    big = {"w_ada": (w_ada, m_w_ada, v_w_ada), "ffn1_w_in": (ffn1_w_in, m_ffn1_w_in, v_ffn1_w_in),
           "ffn1_w_out": (ffn1_w_out, m_ffn1_w_out, v_ffn1_w_out), "w_in": (w_in, m_w_in, v_w_in),
           "pool_w": (pool_w, m_pool_w, v_pool_w), "w_out": (w_out, m_w_out, v_w_out),
           "ffn2_w_in": (ffn2_w_in, m_ffn2_w_in, v_ffn2_w_in), "ffn2_w_out": (ffn2_w_out, m_ffn2_w_out, v_ffn2_w_out)}
    res = {}
    for nme, (w, m, v) in big.items():
        shp = w.shape
        g2 = grads[nme]
        two = g2.shape
        d, mo, vo = _adamw(w.reshape(two), g2, m.reshape(two), v.reshape(two), f"adamw_{nme}")
        res[nme] = (g2.reshape(shp), d.reshape(shp), mo.reshape(shp), vo.reshape(shp))
    for nme in small_names:
        res[nme] = tuple(dct[nme] for dct in small)

    order = ["w_ada", "b_ada", "ffn1_norm_g", "ffn1_w_in", "ffn1_w_out", "mix_norm_g", "w_in", "b_forget", "q_norm_g",
             "k_norm_g", "pool_w", "pool_scale", "w_out", "ffn2_norm_g", "ffn2_w_in", "ffn2_w_out", "final_norm_g"]
    return (loss, grad_x, *[res[n][0] for n in order], *[res[n][1] for n in order], *[res[n][2] for n in order],
            *[res[n][3] for n in order])
```

```python
import functools

import jax
import jax.numpy as jnp
from jax import lax
from jax.experimental import pallas as pl
from jax.experimental.pallas import tpu as pltpu

F32 = jnp.float32
BF16 = jnp.bfloat16

D = 2048
N_HEADS = 8
HEAD_DIM = 128
D_ATTN = 1024
D_POOL = 1024
POOL_WINDOWS = (2, 4, 8, 16)
GROUP_DIM = 256
HALO = 16
N_MOD = 9
EPS = 1e-6
D_PROJ = 3 * D_ATTN + D_POOL
D_PROJ_PAD = D_PROJ + 128
LANE = 128
ATTN_BLOCK = 512
POOL_BLOCK = 512

ADAM_LR = 0.001
ADAM_B1 = 0.9
ADAM_B2 = 0.999
ADAM_EPS = 1e-08
ADAM_WD = 0.01
ADAM_STEP = 10

VMEM_LIMIT_V7X = 56 * 1024 * 1024
MESH_ID = pl.DeviceIdType.MESH
ANY = pl.BlockSpec(memory_space=pl.ANY)
VMEM = pl.BlockSpec(memory_space=pltpu.VMEM)

NT = (((1,), (1,)), ((), ()))
TN = (((0,), (0,)), ((), ()))


def _pcall(body, *, name, out_shape, grid=None, in_specs=None, out_specs=None, scratch=(), sem=None, prefetch=0):
    params = dict(vmem_limit_bytes=VMEM_LIMIT_V7X)
    if sem is not None:
        params["dimension_semantics"] = sem
    kw = dict(name=name, out_shape=out_shape, compiler_params=pltpu.CompilerParams(**params))
    if prefetch:
        kw["grid_spec"] = pltpu.PrefetchScalarGridSpec(
            num_scalar_prefetch=prefetch, grid=grid, in_specs=in_specs, out_specs=out_specs, scratch_shapes=list(scratch))
    else:
        if grid is not None:
            kw["grid"] = grid
        kw["in_specs"] = in_specs
        kw["out_specs"] = out_specs
        kw["scratch_shapes"] = list(scratch)
    return pl.pallas_call(body, **kw)


def _pick(n, cap, mult):
    best = None
    for d in range(mult, min(n, cap) + 1, mult):
        if n % d == 0:
            best = d
    assert best is not None, (n, cap, mult)
    return best


def _coords():
    return lax.axis_index("x"), lax.axis_index("y"), lax.axis_index("c")


def _dot(a, b, dims=None):
    if dims is None:
        return jnp.dot(a, b, preferred_element_type=F32)
    return lax.dot_general(a, b, dims, preferred_element_type=F32)


def _remote(src, dst, ssem, rsem, dev):
    return pltpu.make_async_remote_copy(src_ref=src, dst_ref=dst, send_sem=ssem, recv_sem=rsem,
                                        device_id=dev, device_id_type=MESH_ID)


def _allgather_small(v, whole_mesh, name):
    masks = list(range(1, 8)) if whole_mesh else [4, 2, 6]
    nslot = 8 if whole_mesh else 4

    def slot(px, py, pc):
        return 4 * px + 2 * py + pc if whole_mesh else 2 * px + py

    def body(v_ref, out_ref, ssem, rsem, lsem):
        x, y, c = _coords()
        mine = slot(x, y, c)
        peers = [(jnp.bitwise_xor(x, (m >> 2) & 1), jnp.bitwise_xor(y, (m >> 1) & 1), jnp.bitwise_xor(c, m & 1))
                 for m in masks]
        loc = pltpu.make_async_copy(v_ref, out_ref.at[mine], lsem)
        loc.start()
        sends = [_remote(v_ref, out_ref.at[mine], ssem.at[k], rsem.at[k], p) for k, p in enumerate(peers)]
        for cp in sends:
            cp.start()
        for k, p in enumerate(peers):
            _remote(v_ref, out_ref.at[slot(*p)], ssem.at[k], rsem.at[k], p).wait_recv()
        for cp in sends:
            cp.wait_send()
        loc.wait()

    return _pcall(body, name=name, out_shape=jax.ShapeDtypeStruct((nslot,) + v.shape, v.dtype),
                  in_specs=[VMEM], out_specs=VMEM,
                  scratch=[pltpu.SemaphoreType.DMA((len(masks),)), pltpu.SemaphoreType.DMA((len(masks),)),
                           pltpu.SemaphoreType.DMA(())])(v)


def _other_chips(x, y):
    return [(1 - x, y), (x, 1 - y), (1 - x, 1 - y)]


def _gather_weights(ws, name):
    n = len(ws)

    def body(*refs):
        w_refs, g_refs = refs[:n], refs[n:2 * n]
        ssem, rsem = refs[2 * n:]
        x, y, c = _coords()
        me, sib = 2 * x + y, (x, y, 1 - c)
        chips = _other_chips(x, y)

        def rows(a, core):
            h = ws[a].shape[0] // 2
            return pl.ds(pl.multiple_of(core * h, 16), h)

        sends = []
        for a in range(n):
            for j, chip in enumerate(chips):
                cp = _remote(w_refs[a].at[rows(a, c)], g_refs[a].at[me, rows(a, c)],
                             ssem.at[6 * a + j], rsem.at[6 * a + j], (*chip, c))
                cp.start()
                sends.append(cp)
        for a in range(n):
            for j, chip in enumerate(chips):
                got = g_refs[a].at[2 * chip[0] + chip[1], rows(a, c)]
                _remote(got, got, ssem.at[6 * a + j], rsem.at[6 * a + j], (*chip, c)).wait_recv()
                cp = _remote(got, got, ssem.at[6 * a + 3 + j], rsem.at[6 * a + 3 + j], sib)
                cp.start()
                sends.append(cp)
        for a in range(n):
            for j, chip in enumerate(chips):
                got = g_refs[a].at[2 * chip[0] + chip[1], rows(a, 1 - c)]
                _remote(got, got, ssem.at[6 * a + 3 + j], rsem.at[6 * a + 3 + j], sib).wait_recv()
        for cp in sends:
            cp.wait_send()

    out = _pcall(body, name=name, out_shape=[jax.ShapeDtypeStruct((4,) + w.shape, w.dtype) for w in ws],
                 in_specs=[ANY] * n, out_specs=[ANY] * n,
                 scratch=[pltpu.SemaphoreType.DMA((6 * n,)), pltpu.SemaphoreType.DMA((6 * n,))])(*ws)
    return list(out)


def _reduce_siblings(ps, name):
    n = len(ps)

    def body(*refs):
        p_refs, r_refs = refs[:n], refs[n:2 * n]
        ssem, rsem = refs[2 * n:]
        x, y, c = _coords()
        cps = []
        for a in range(n):
            h = ps[a].shape[1] // 2
            src = p_refs[a].at[:, pl.ds(pl.multiple_of((1 - c) * h, 16), h), :]
            cp = _remote(src, r_refs[a], ssem.at[a], rsem.at[a], (x, y, 1 - c))
            cp.start()
            cps.append(cp)
        for cp in cps:
            cp.wait_recv()
        for cp in cps:
            cp.wait_send()

    out = _pcall(body, name=name,
                 out_shape=[jax.ShapeDtypeStruct((4, p.shape[1] // 2, p.shape[2]), p.dtype) for p in ps],
                 in_specs=[ANY] * n, out_specs=[ANY] * n,
                 scratch=[pltpu.SemaphoreType.DMA((n,)), pltpu.SemaphoreType.DMA((n,))])(*ps)
    return list(out)


def _reduce_chips(hs, name):
    n = len(hs)

    def body(*refs):
        h_refs, o_refs = refs[:n], refs[n:2 * n]
        ssem, rsem = refs[2 * n:]
        x, y, c = _coords()
        cps = []
        for a in range(n):
            for j, chip in enumerate(_other_chips(x, y)):
                cp = _remote(h_refs[a].at[2 * chip[0] + chip[1]], o_refs[a].at[j],
                             ssem.at[3 * a + j], rsem.at[3 * a + j], (*chip, c))
                cp.start()
                cps.append(cp)
        for cp in cps:
            cp.wait_recv()
        for cp in cps:
            cp.wait_send()

    out = _pcall(body, name=name, out_shape=[jax.ShapeDtypeStruct((3,) + h.shape[1:], h.dtype) for h in hs],
                 in_specs=[ANY] * n, out_specs=[ANY] * n,
                 scratch=[pltpu.SemaphoreType.DMA((3 * n,)), pltpu.SemaphoreType.DMA((3 * n,))])(*hs)
    return list(out)


def _share_siblings(gs, name):
    n = len(gs)

    def body(*refs):
        g_refs, o_refs = refs[:n], refs[n:2 * n]
        ssem, rsem = refs[2 * n:]
        x, y, c = _coords()
        cps = []
        for a in range(n):
            cp = _remote(g_refs[a], o_refs[a], ssem.at[a], rsem.at[a], (x, y, 1 - c))
            cp.start()
            cps.append(cp)
        for cp in cps:
            cp.wait_recv()
        for cp in cps:
            cp.wait_send()

    out = _pcall(body, name=name, out_shape=[jax.ShapeDtypeStruct(g.shape, g.dtype) for g in gs],
                 in_specs=[ANY] * n, out_specs=[ANY] * n,
                 scratch=[pltpu.SemaphoreType.DMA((n,)), pltpu.SemaphoreType.DMA((n,))])(*gs)
    return list(out)


def _add_sibling(p, r, core, name):
    _, rr, cc = p.shape
    h = rr // 2
    th = _pick(h, max(16, (2 << 20) // (2 * cc)), 16)
    nb = h // th

    def body(c_ref, p_ref, r_ref, o_ref):
        o_ref[...] = (p_ref[...].astype(F32) + r_ref[...].astype(F32)).astype(BF16)

    return _pcall(body, name=name, out_shape=jax.ShapeDtypeStruct((4, h, cc), BF16), grid=(4, nb),
                  in_specs=[pl.BlockSpec((None, th, cc), lambda k, i, c_ref: (k, c_ref[0] * nb + i, 0)),
                            pl.BlockSpec((None, th, cc), lambda k, i, c_ref: (k, i, 0))],
                  out_specs=pl.BlockSpec((None, th, cc), lambda k, i, c_ref: (k, i, 0)),
                  sem=("parallel", "parallel"), prefetch=1)(core, p, r)


def _add_chips(hh, r, chip, name):
    _, h, cc = hh.shape
    th = _pick(h, max(16, (2 << 20) // (2 * cc)), 16)

    def body(k_ref, h_ref, r_ref, o_ref):
        s = h_ref[...].astype(F32) + r_ref[0].astype(F32)
        s = s + r_ref[1].astype(F32)
        o_ref[...] = s + r_ref[2].astype(F32)

    return _pcall(body, name=name, out_shape=jax.ShapeDtypeStruct((h, cc), F32), grid=(h // th,),
                  in_specs=[pl.BlockSpec((None, th, cc), lambda i, k_ref: (k_ref[0], i, 0)),
                            pl.BlockSpec((3, th, cc), lambda i, k_ref: (0, i, 0))],
                  out_specs=pl.BlockSpec((th, cc), lambda i, k_ref: (i, 0)),
                  sem=("parallel",), prefetch=1)(chip, hh, r)


def _adamw_math(w, g, m, v):
    m = ADAM_B1 * m + (1.0 - ADAM_B1) * g
    v = ADAM_B2 * v + (1.0 - ADAM_B2) * (g * g)
    m_hat = m / (1.0 - ADAM_B1 ** ADAM_STEP)
    v_hat = v / (1.0 - ADAM_B2 ** ADAM_STEP)
    delta = -ADAM_LR * (m_hat / (jnp.sqrt(v_hat) + ADAM_EPS) + ADAM_WD * w)
    return delta, m, v


def _adamw(w, g, m, v, name):
    rr, cc = w.shape
    tr = _pick(rr, max(8, (3 << 20) // (4 * cc)), 8)

    def body(w_ref, g_ref, m_ref, v_ref, d_ref, mo_ref, vo_ref):
        d, mm, vv = _adamw_math(w_ref[...], g_ref[...], m_ref[...], v_ref[...])
        d_ref[...] = d
        mo_ref[...] = mm
        vo_ref[...] = vv

    spec = pl.BlockSpec((tr, cc), lambda i: (i, 0))
    return _pcall(body, name=name, out_shape=[jax.ShapeDtypeStruct(w.shape, F32)] * 3, grid=(rr // tr,),
                  in_specs=[spec] * 4, out_specs=[spec] * 3, sem=("parallel",))(w, g, m, v)


def _adamw_small(w, g8, m, v, name):
    def body(w_ref, g_ref, m_ref, v_ref, go_ref, d_ref, mo_ref, vo_ref):
        g = g_ref[0]
        for k in range(1, 8):
            g = g + g_ref[k]
        d, mm, vv = _adamw_math(w_ref[...], g, m_ref[...], v_ref[...])
        go_ref[...] = g
        d_ref[...] = d
        mo_ref[...] = mm
        vo_ref[...] = vv

    return _pcall(body, name=name, out_shape=[jax.ShapeDtypeStruct(w.shape, F32)] * 4,
                  in_specs=[VMEM] * 4, out_specs=[VMEM] * 4)(w, g8, m, v)


def _ada_fwd(c16, w_ada, b_ada, name):
    n = w_ada.shape[1]
    tn = _pick(n, 512, LANE)

    def body(c_ref, w_ref, b_ref, act_ref, mod_ref):
        cv = c_ref[...]
        act = cv * jax.nn.sigmoid(cv)
        act_ref[...] = act
        mod_ref[...] = _dot(act.astype(BF16), w_ref[...].astype(BF16)) + b_ref[...]

    return _pcall(body, name=name,
                  out_shape=[jax.ShapeDtypeStruct((16, D), F32), jax.ShapeDtypeStruct((16, n), F32)], grid=(n // tn,),
                  in_specs=[pl.BlockSpec((16, D), lambda j: (0, 0)), pl.BlockSpec((D, tn), lambda j: (0, j)),
                            pl.BlockSpec((1, tn), lambda j: (0, j))],
                  out_specs=[pl.BlockSpec((16, D), lambda j: (0, 0)), pl.BlockSpec((16, tn), lambda j: (0, j))],
                  sem=("arbitrary",))(c16, w_ada, b_ada)


def _ada_bwd(act, dmod, name):
    n = dmod.shape[2]
    tm, tn = 256, _pick(n, 512, LANE)

    def body(a_ref, d_ref, o_ref):
        def term(b):
            return a_ref[b].astype(BF16).astype(F32) * d_ref[b].astype(BF16).astype(F32)

        acc = term(0)
        for b in range(1, 8):
            acc = acc + term(b)
        o_ref[...] = acc

    return _pcall(body, name=name, out_shape=jax.ShapeDtypeStruct((D, n), F32), grid=(D // tm, n // tn),
                  in_specs=[pl.BlockSpec((8, tm, 1), lambda i, j: (0, i, 0)), pl.BlockSpec((8, 1, tn), lambda i, j: (0, 0, j))],
                  out_specs=pl.BlockSpec((tm, tn), lambda i, j: (i, j)), sem=("parallel", "parallel"))(act, dmod)


def _norm_mod(x, g, sh, sc, name):
    t = x.shape[0]
    tm = _pick(t, 512, 8)

    def body(x_ref, g_ref, sh_ref, sc_ref, h_ref):
        xf = x_ref[...]
        r = lax.rsqrt(jnp.mean(xf * xf, axis=-1, keepdims=True) + EPS)
        h = (xf * r) * g_ref[...]
        h_ref[...] = (h * (1.0 + sc_ref[...]) + sh_ref[...]).astype(BF16)

    vec = pl.BlockSpec((1, D), lambda i: (0, 0))
    row = pl.BlockSpec((tm, D), lambda i: (i, 0))
    return _pcall(body, name=name, out_shape=jax.ShapeDtypeStruct((t, D), BF16), grid=(t // tm,),
                  in_specs=[row, vec, vec, vec], out_specs=row, sem=("parallel",))(x, g, sh, sc)


def _norm_mod_bwd(dh, x, dxo, g, sc, name):
    t = x.shape[0]
    tm = _pick(t, 256, 8)

    def body(dh_ref, x_ref, dxo_ref, g_ref, sc_ref, dx_ref, dsh_ref, dsc_ref, dg_ref):
        @pl.when(pl.program_id(0) == 0)
        def _():
            dsh_ref[...] = jnp.zeros_like(dsh_ref)
            dsc_ref[...] = jnp.zeros_like(dsc_ref)
            dg_ref[...] = jnp.zeros_like(dg_ref)

        xf, dh_ = x_ref[...], dh_ref[...]
        r = lax.rsqrt(jnp.mean(xf * xf, axis=-1, keepdims=True) + EPS)
        xhat = xf * r
        dsh_ref[...] += jnp.sum(dh_, axis=0, keepdims=True)
        dsc_ref[...] += jnp.sum(dh_ * (xhat * g_ref[...]), axis=0, keepdims=True)
        tt = dh_ * (1.0 + sc_ref[...])
        dg_ref[...] += jnp.sum(tt * xhat, axis=0, keepdims=True)
        dxh = tt * g_ref[...]
        dx_ref[...] = r * (dxh - xhat * jnp.mean(dxh * xhat, axis=-1, keepdims=True)) + dxo_ref[...]

    vec = pl.BlockSpec((1, D), lambda i: (0, 0))
    row = pl.BlockSpec((tm, D), lambda i: (i, 0))
    vshape = jax.ShapeDtypeStruct((1, D), F32)
    return _pcall(body, name=name, out_shape=[jax.ShapeDtypeStruct((t, D), F32), vshape, vshape, vshape], grid=(t // tm,),
                  in_specs=[row, row, row, vec, vec], out_specs=[row, vec, vec, vec], sem=("arbitrary",))(dh, x, dxo, g, sc)


def _gate_bwd(dxo, yy, gate, name):
    t = dxo.shape[0]
    tm = _pick(t, 512, 8)

    def body(dx_ref, y_ref, g_ref, dy_ref, dg_ref):
        @pl.when(pl.program_id(0) == 0)
        def _():
            dg_ref[...] = jnp.zeros_like(dg_ref)

        dx = dx_ref[...]
        dy_ref[...] = (dx * g_ref[...]).astype(BF16)
        dg_ref[...] += jnp.sum(dx * y_ref[...], axis=0, keepdims=True)

    vec = pl.BlockSpec((1, D), lambda i: (0, 0))
    row = pl.BlockSpec((tm, D), lambda i: (i, 0))
    return _pcall(body, name=name, out_shape=[jax.ShapeDtypeStruct((t, D), BF16), jax.ShapeDtypeStruct((1, D), F32)],
                  grid=(t // tm,), in_specs=[row, row, vec], out_specs=[row, vec], sem=("arbitrary",))(dxo, yy, gate)


def _final_loss(x, g, tgt, name):
    t = x.shape[0]
    tm = _pick(t, 256, 8)

    def body(x_ref, g_ref, t_ref, loss_ref, dx_ref, dg_ref):
        @pl.when(pl.program_id(0) == 0)
        def _():
            loss_ref[...] = jnp.zeros_like(loss_ref)
            dg_ref[...] = jnp.zeros_like(dg_ref)

        xf = x_ref[...]
        r = lax.rsqrt(jnp.mean(xf * xf, axis=-1, keepdims=True) + EPS)
        xhat = xf * r
        e = xhat * g_ref[...] - t_ref[...]
        per_tok = jnp.mean(e * e, axis=-1, keepdims=True)
        loss_ref[...] += 0.5 * jnp.sum(per_tok, axis=0, keepdims=True)
        dy = e * (1.0 / D)
        dg_ref[...] += jnp.sum(dy * xhat, axis=0, keepdims=True)
        dxh = dy * g_ref[...]
        dx_ref[...] = r * (dxh - xhat * jnp.mean(dxh * xhat, axis=-1, keepdims=True))

    vec = pl.BlockSpec((1, D), lambda i: (0, 0))
    row = pl.BlockSpec((tm, D), lambda i: (i, 0))
    return _pcall(body, name=name,
                  out_shape=[jax.ShapeDtypeStruct((1, LANE), F32), jax.ShapeDtypeStruct((t, D), F32),
                             jax.ShapeDtypeStruct((1, D), F32)],
                  grid=(t // tm,), in_specs=[row, vec, row],
                  out_specs=[pl.BlockSpec((1, LANE), lambda i: (0, 0)), row, vec], sem=("arbitrary",))(x, g, tgt)


def _mm(a, b, out_dtype, name):
    m, k = a.shape
    n = b.shape[1]
    tm = _pick(m, 1024, 8)
    tn = _pick(n, 1408, LANE)

    def body(a_ref, b_ref, o_ref):
        o_ref[...] = _dot(a_ref[...], b_ref[...]).astype(out_dtype)

    return _pcall(body, name=name, out_shape=jax.ShapeDtypeStruct((m, n), out_dtype), grid=(m // tm, n // tn),
                  in_specs=[pl.BlockSpec((tm, k), lambda i, j: (i, 0)), pl.BlockSpec((k, tn), lambda i, j: (0, j))],
                  out_specs=pl.BlockSpec((tm, tn), lambda i, j: (i, j)), sem=("parallel", "parallel"))(a, b)


def _mm_resid(a, b, resid, gate, name):
    m, k = a.shape
    n = b.shape[1]
    tm, tn, tk = _pick(m, 1024, 8), _pick(n, 1024, LANE), _pick(k, 512, LANE)
    nk = k // tk

    def body(a_ref, b_ref, r_ref, g_ref, y_ref, o_ref, acc):
        kk = pl.program_id(2)

        @pl.when(kk == 0)
        def _():
            acc[...] = jnp.zeros_like(acc)

        acc[...] += _dot(a_ref[...], b_ref[...])

        @pl.when(kk == nk - 1)
        def _():
            y_ref[...] = acc[...]
            o_ref[...] = r_ref[...] + g_ref[...] * acc[...]

    blk = pl.BlockSpec((tm, tn), lambda i, j, kk: (i, j))
    return _pcall(body, name=name, out_shape=[jax.ShapeDtypeStruct((m, n), F32)] * 2, grid=(m // tm, n // tn, nk),
                  in_specs=[pl.BlockSpec((tm, tk), lambda i, j, kk: (i, kk)), pl.BlockSpec((tk, tn), lambda i, j, kk: (kk, j)),
                            blk, pl.BlockSpec((1, tn), lambda i, j, kk: (0, j))],
                  out_specs=[blk, blk], scratch=[pltpu.VMEM((tm, tn), F32)],
                  sem=("parallel", "parallel", "arbitrary"))(a, b, resid, gate)


def _mm_nt(a, b, out_dtype, name):
    m, k = a.shape
    n = b.shape[0]
    tm, tn, tk = _pick(m, 1024, 8), _pick(n, 1024, LANE), _pick(k, 1408, LANE)
    nk = k // tk

    def body(a_ref, b_ref, o_ref, acc):
        kk = pl.program_id(2)

        @pl.when(kk == 0)
        def _():
            acc[...] = jnp.zeros_like(acc)

        acc[...] += _dot(a_ref[...], b_ref[...], NT)

        @pl.when(kk == nk - 1)
        def _():
            o_ref[...] = acc[...].astype(out_dtype)

    return _pcall(body, name=name, out_shape=jax.ShapeDtypeStruct((m, n), out_dtype), grid=(m // tm, n // tn, nk),
                  in_specs=[pl.BlockSpec((tm, tk), lambda i, j, kk: (i, kk)), pl.BlockSpec((tn, tk), lambda i, j, kk: (j, kk))],
                  out_specs=pl.BlockSpec((tm, tn), lambda i, j, kk: (i, j)), scratch=[pltpu.VMEM((tm, tn), F32)],
                  sem=("parallel", "parallel", "arbitrary"))(a, b)


def _mm_tn(a, b, out_dtype, tm_cap, name):
    t, m = a.shape
    n = b.shape[1]
    tm, tn, tk = _pick(m, tm_cap, LANE), _pick(n, 1408, LANE), _pick(t, 512, 16)
    nk = t // tk

    def body(a_ref, b_ref, o_ref, acc):
        kk = pl.program_id(2)

        @pl.when(kk == 0)
        def _():
            acc[...] = jnp.zeros_like(acc)

        acc[...] += _dot(a_ref[...], b_ref[...], TN)

        @pl.when(kk == nk - 1)
        def _():
            o_ref[...] = acc[...].astype(out_dtype)

    return _pcall(body, name=name, out_shape=jax.ShapeDtypeStruct((m, n), out_dtype), grid=(m // tm, n // tn, nk),
                  in_specs=[pl.BlockSpec((tk, tm), lambda i, j, kk: (kk, i)), pl.BlockSpec((tk, tn), lambda i, j, kk: (kk, j))],
                  out_specs=pl.BlockSpec((tm, tn), lambda i, j, kk: (i, j)), scratch=[pltpu.VMEM((tm, tn), F32)],
                  sem=("parallel", "parallel", "arbitrary"))(a, b)


def _ffn_up(h, wg, name):
    t = h.shape[0]
    fp = wg.shape[2]
    tm, tn = _pick(t, 1024, 8), 256
    nn = fp // tn

    def body(h_ref, wa_ref, wb_ref, a_ref, b_ref, u_ref):
        hh = h_ref[...]
        a = _dot(hh, wa_ref[...])
        b = _dot(hh, wb_ref[...])
        a_ref[...] = a.astype(BF16)
        b_ref[...] = b.astype(BF16)
        u_ref[...] = (a * jax.nn.sigmoid(a) * b).astype(BF16)

    out = pl.BlockSpec((tm, tn), lambda i, j, n: (i, j * nn + n))
    return _pcall(body, name=name, out_shape=[jax.ShapeDtypeStruct((t, 2 * fp), BF16)] * 3, grid=(t // tm, 2, nn),
                  in_specs=[pl.BlockSpec((tm, D), lambda i, j, n: (i, 0)),
                            pl.BlockSpec((None, D, tn), lambda i, j, n: (j, 0, n)),
                            pl.BlockSpec((None, D, tn), lambda i, j, n: (j + 2, 0, n))],
                  out_specs=[out, out, out], sem=("parallel", "parallel", "parallel"))(h, wg, wg)


def _ffn_dab(dy, wo, a, b, name):
    t = dy.shape[0]
    f2 = wo.shape[0]
    tm, tn = _pick(t, 1024, 8), 256

    def body(dy_ref, w_ref, a_ref, b_ref, o_ref):
        du = _dot(dy_ref[...], w_ref[...], NT)
        av, bv = a_ref[...].astype(F32), b_ref[...].astype(F32)
        sg = jax.nn.sigmoid(av)
        o_ref[0] = (du * bv * (sg * (1.0 + av * (1.0 - sg)))).astype(BF16)
        o_ref[1] = (du * (av * sg)).astype(BF16)

    blk = pl.BlockSpec((tm, tn), lambda i, n: (i, n))
    return _pcall(body, name=name, out_shape=jax.ShapeDtypeStruct((2, t, f2), BF16), grid=(t // tm, f2 // tn),
                  in_specs=[pl.BlockSpec((tm, D), lambda i, n: (i, 0)), pl.BlockSpec((tn, D), lambda i, n: (n, 0)), blk, blk],
                  out_specs=pl.BlockSpec((2, tm, tn), lambda i, n: (0, i, n)), sem=("parallel", "parallel"))(dy, wo, a, b)


def _ffn_dwin(h, dab, name):
    t = h.shape[0]
    fp = dab.shape[2] // 2
    tm, tn, tk = 1024, _pick(fp, 1408, LANE), _pick(t, 512, 16)
    nn, nk = fp // tn, t // tk

    def body(h_ref, d_ref, o_ref, acc):
        kk = pl.program_id(3)

        @pl.when(kk == 0)
        def _():
            acc[...] = jnp.zeros_like(acc)

        acc[...] += _dot(h_ref[...], d_ref[...], TN)

        @pl.when(kk == nk - 1)
        def _():
            o_ref[...] = acc[...].astype(BF16)

    return _pcall(body, name=name, out_shape=jax.ShapeDtypeStruct((4, D, fp), BF16), grid=(4, D // tm, nn, nk),
                  in_specs=[pl.BlockSpec((tk, tm), lambda u, i, n, kk: (kk, i)),
                            pl.BlockSpec((None, tk, tn), lambda u, i, n, kk: (u // 2, kk, (u % 2) * nn + n))],
                  out_specs=pl.BlockSpec((None, tm, tn), lambda u, i, n, kk: (u, i, n)),
                  scratch=[pltpu.VMEM((tm, tn), F32)],
                  sem=("parallel", "parallel", "parallel", "arbitrary"))(h, dab)


def _ffn_dh(dab, wg, name):
    t = dab.shape[1]
    fp = wg.shape[2]
    tm, tn, tk = _pick(t, 1024, 8), 1024, _pick(fp, 1408, LANE)
    nkk = fp // tk

    def body(d_ref, w_ref, o_ref, acc):
        u, kk = pl.program_id(2), pl.program_id(3)

        @pl.when((u == 0) & (kk == 0))
        def _():
            acc[...] = jnp.zeros_like(acc)

        acc[...] += _dot(d_ref[...], w_ref[...], NT)

        @pl.when((u == 3) & (kk == nkk - 1))
        def _():
            o_ref[...] = acc[...]

    return _pcall(body, name=name, out_shape=jax.ShapeDtypeStruct((t, D), F32), grid=(t // tm, D // tn, 4, nkk),
                  in_specs=[pl.BlockSpec((None, tm, tk), lambda i, j, u, kk: (u // 2, i, (u % 2) * nkk + kk)),
                            pl.BlockSpec((None, tn, tk), lambda i, j, u, kk: (u, j, kk))],
                  out_specs=pl.BlockSpec((tm, tn), lambda i, j, u, kk: (i, j)), scratch=[pltpu.VMEM((tm, tn), F32)],
                  sem=("parallel", "parallel", "arbitrary", "arbitrary"))(dab, wg)


def _split3(v):
    hi = v.astype(BF16)
    r1 = v - hi.astype(F32)
    mid = r1.astype(BF16)
    lo = (r1 - mid.astype(F32)).astype(BF16)
    return hi, mid, lo


def _tri_sum(tri, v):
    hi, mid, lo = _split3(v)
    return (_dot(tri, hi) + _dot(tri, mid)) + _dot(tri, lo)


def _forget_fwd(proj, b_pad, name):
    t = proj.shape[0]
    tb = _pick(t, 256, 8)
    col = D_PROJ // LANE

    def body(f_ref, b_ref, o_ref, carry):
        @pl.when(pl.program_id(0) == 0)
        def _():
            carry[...] = jnp.zeros_like(carry)

        z = f_ref[...] + b_ref[...]
        lf = jnp.minimum(z, 0.0) - jnp.log(1.0 + jnp.exp(-jnp.abs(z)))
        r = lax.broadcasted_iota(jnp.int32, (tb, tb), 0)
        cidx = lax.broadcasted_iota(jnp.int32, (tb, tb), 1)
        tri = (r >= cidx).astype(BF16)
        o_ref[...] = _tri_sum(tri, lf) + carry[...]
        carry[...] += jnp.sum(lf, axis=0, keepdims=True)

    return _pcall(body, name=name, out_shape=jax.ShapeDtypeStruct((t, LANE), F32), grid=(t // tb,),
                  in_specs=[pl.BlockSpec((tb, LANE), lambda i: (i, col)), pl.BlockSpec((1, LANE), lambda i: (0, 0))],
                  out_specs=pl.BlockSpec((tb, LANE), lambda i: (i, 0)), scratch=[pltpu.VMEM((1, LANE), F32)],
                  sem=("arbitrary",))(proj, b_pad)


def _forget_bwd(d_cum, proj, b_pad, name):
    t = proj.shape[0]
    tb = _pick(t, 256, 8)
    nb = t // tb
    col = D_PROJ // LANE

    def body(d_ref, f_ref, b_ref, o_ref, db_ref, carry):
        @pl.when(pl.program_id(0) == 0)
        def _():
            carry[...] = jnp.zeros_like(carry)
            db_ref[...] = jnp.zeros_like(db_ref)

        dc = d_ref[...]
        r = lax.broadcasted_iota(jnp.int32, (tb, tb), 0)
        cidx = lax.broadcasted_iota(jnp.int32, (tb, tb), 1)
        tri = (r <= cidx).astype(BF16)
        dlf = _tri_sum(tri, dc) + carry[...]
        carry[...] += jnp.sum(dc, axis=0, keepdims=True)
        z = f_ref[...] + b_ref[...]
        lane = lax.broadcasted_iota(jnp.int32, (tb, LANE), 1)
        dz = jnp.where(lane < N_HEADS, dlf * jax.nn.sigmoid(-z), 0.0)
        o_ref[...] = dz.astype(BF16)
        db_ref[...] += jnp.sum(dz, axis=0, keepdims=True)

    return _pcall(body, name=name, out_shape=[jax.ShapeDtypeStruct((t, LANE), BF16), jax.ShapeDtypeStruct((1, LANE), F32)],
                  grid=(nb,),
                  in_specs=[pl.BlockSpec((tb, LANE), lambda i: (nb - 1 - i, 0)),
                            pl.BlockSpec((tb, LANE), lambda i: (nb - 1 - i, col)),
                            pl.BlockSpec((1, LANE), lambda i: (0, 0))],
                  out_specs=[pl.BlockSpec((tb, LANE), lambda i: (nb - 1 - i, 0)), pl.BlockSpec((1, LANE), lambda i: (0, 0))],
                  scratch=[pltpu.VMEM((1, LANE), F32)], sem=("arbitrary",))(d_cum, proj, b_pad)


def _head_norm(v, g):
    r = lax.rsqrt(jnp.mean(v * v, axis=-1, keepdims=True) + EPS)
    return v * r, r


def _qkv_prep(proj, qg, kg, name):
    t = proj.shape[0]
    tm = _pick(t, 1024, 8)

    def body(q_ref, k_ref, v_ref, qg_ref, kg_ref, qo_ref, ko_ref, vo_ref):
        qo_ref[...] = (_head_norm(q_ref[...], None)[0] * qg_ref[...]).astype(BF16)
        ko_ref[...] = (_head_norm(k_ref[...], None)[0] * kg_ref[...]).astype(BF16)
        vo_ref[...] = v_ref[...].astype(BF16)

    def blk(off):
        return pl.BlockSpec((tm, HEAD_DIM), lambda i, h: (i, off + h))

    vec = pl.BlockSpec((1, HEAD_DIM), lambda i, h: (0, 0))
    return _pcall(body, name=name, out_shape=[jax.ShapeDtypeStruct((t, D_ATTN), BF16)] * 3, grid=(t // tm, N_HEADS),
                  in_specs=[blk(0), blk(N_HEADS), blk(2 * N_HEADS), vec, vec], out_specs=[blk(0)] * 3,
                  sem=("parallel", "parallel"))(proj, proj, proj, qg, kg)


def _qk_norm_bwd(dqn, dkn, proj, qg, kg, name):
    t = proj.shape[0]
    tm = _pick(t, 1024, 8)

    def one(d_ref, v_ref, g_ref, o_ref, dg_ref):
        xhat, r = _head_norm(v_ref[...], None)
        d = d_ref[...]
        dg_ref[...] += jnp.sum(d * xhat, axis=0, keepdims=True)
        dxh = d * g_ref[...]
        o_ref[...] = (r * (dxh - xhat * jnp.mean(dxh * xhat, axis=-1, keepdims=True))).astype(BF16)

    def body(dq_ref, dk_ref, q_ref, k_ref, qg_ref, kg_ref, qo_ref, ko_ref, dqg_ref, dkg_ref):
        @pl.when((pl.program_id(0) == 0) & (pl.program_id(1) == 0))
        def _():
            dqg_ref[...] = jnp.zeros_like(dqg_ref)
            dkg_ref[...] = jnp.zeros_like(dkg_ref)

        one(dq_ref, q_ref, qg_ref, qo_ref, dqg_ref)
        one(dk_ref, k_ref, kg_ref, ko_ref, dkg_ref)

    def blk(off):
        return pl.BlockSpec((tm, HEAD_DIM), lambda i, h: (i, off + h))

    vec = pl.BlockSpec((1, HEAD_DIM), lambda i, h: (0, 0))
    vshape = jax.ShapeDtypeStruct((1, HEAD_DIM), F32)
    return _pcall(body, name=name, out_shape=[jax.ShapeDtypeStruct((t, D_ATTN), BF16)] * 2 + [vshape, vshape],
                  grid=(t // tm, N_HEADS),
                  in_specs=[blk(0), blk(0), blk(0), blk(N_HEADS), vec, vec], out_specs=[blk(0), blk(0), vec, vec],
                  sem=("arbitrary", "arbitrary"))(dqn, dkn, proj, proj, qg, kg)


ATTN_SCALE = HEAD_DIM ** -0.5


def _logits(q, k, fq, fk, diag, tq, tk):
    s = _dot(q, k, NT) * ATTN_SCALE + (fq - fk)
    if diag:
        r = lax.broadcasted_iota(jnp.int32, (tq, tk), 0)
        cidx = lax.broadcasted_iota(jnp.int32, (tq, tk), 1)
        s = jnp.where(r >= cidx, s, -jnp.inf)
    return s


def _attn_fwd(q, k, v, fq, fk, name):
    t = q.shape[0]
    tq = tk = _pick(t, ATTN_BLOCK, LANE)
    nk = t // tk

    def body(q_ref, k_ref, v_ref, fq_ref, fk_ref, o_ref, lse_ref, m_s, l_s, acc):
        i, j = pl.program_id(1), pl.program_id(2)

        @pl.when(j == 0)
        def _():
            m_s[...] = jnp.full_like(m_s, -jnp.inf)
            l_s[...] = jnp.zeros_like(l_s)
            acc[...] = jnp.zeros_like(acc)

        def step(diag):
            s = _logits(q_ref[...], k_ref[...], fq_ref[...], fk_ref[...], diag, tq, tk)
            m_new = jnp.maximum(m_s[...], jnp.max(s, axis=-1, keepdims=True))
            alpha = jnp.exp(m_s[...] - m_new)
            p = jnp.exp(s - m_new)
            l_s[...] = alpha * l_s[...] + jnp.sum(p, axis=-1, keepdims=True)
            acc[...] = alpha * acc[...] + _dot(p.astype(BF16), v_ref[...])
            m_s[...] = m_new

        @pl.when(j < i)
        def _():
            step(False)

        @pl.when(j == i)
        def _():
            step(True)

        @pl.when(j == nk - 1)
        def _():
            o_ref[...] = (acc[...] / l_s[...]).astype(BF16)
            lse_ref[...] = m_s[...] + jnp.log(l_s[...])

    qb = pl.BlockSpec((tq, HEAD_DIM), lambda h, i, j: (i, h))
    kb = pl.BlockSpec((tk, HEAD_DIM), lambda h, i, j: (jnp.minimum(j, i), h))
    col = pl.BlockSpec((None, tq, 1), lambda h, i, j: (h, i, 0))
    return _pcall(body, name=name,
                  out_shape=[jax.ShapeDtypeStruct((t, D_ATTN), BF16), jax.ShapeDtypeStruct((N_HEADS, t, 1), F32)],
                  grid=(N_HEADS, t // tq, nk),
                  in_specs=[qb, kb, kb, col, pl.BlockSpec((None, 1, tk), lambda h, i, j: (h, 0, jnp.minimum(j, i)))],
                  out_specs=[qb, col],
                  scratch=[pltpu.VMEM((tq, 1), F32), pltpu.VMEM((tq, 1), F32), pltpu.VMEM((tq, HEAD_DIM), F32)],
                  sem=("parallel", "parallel", "arbitrary"))(q, k, v, fq, fk)


def _attn_bwd_q(q, k, v, o, do, lse, fq, fk, name):
    t = q.shape[0]
    tq = tk = _pick(t, ATTN_BLOCK, LANE)
    nk = t // tk

    def body(q_ref, k_ref, v_ref, o_ref, do_ref, lse_ref, fq_ref, fk_ref, dq_ref, dl_ref, dfq_ref, acc, dl_s, df_s):
        i, j = pl.program_id(1), pl.program_id(2)

        @pl.when(j == 0)
        def _():
            acc[...] = jnp.zeros_like(acc)
            df_s[...] = jnp.zeros_like(df_s)
            dl_s[...] = jnp.sum(do_ref[...].astype(F32) * o_ref[...].astype(F32), axis=-1, keepdims=True)

        def step(diag):
            s = _logits(q_ref[...], k_ref[...], fq_ref[...], fk_ref[...], diag, tq, tk)
            p = jnp.exp(s - lse_ref[...])
            dp = _dot(do_ref[...], v_ref[...], NT)
            ds = p * (dp - dl_s[...])
            df_s[...] += jnp.sum(ds, axis=-1, keepdims=True)
            acc[...] += _dot(ds.astype(BF16), k_ref[...])

        @pl.when(j < i)
        def _():
            step(False)

        @pl.when(j == i)
        def _():
            step(True)

        @pl.when(j == nk - 1)
        def _():
            dq_ref[...] = acc[...] * ATTN_SCALE
            dl_ref[...] = dl_s[...]
            dfq_ref[...] = df_s[...]

    qb = pl.BlockSpec((tq, HEAD_DIM), lambda h, i, j: (i, h))
    kb = pl.BlockSpec((tk, HEAD_DIM), lambda h, i, j: (jnp.minimum(j, i), h))
    col = pl.BlockSpec((None, tq, 1), lambda h, i, j: (h, i, 0))
    cshape = jax.ShapeDtypeStruct((N_HEADS, t, 1), F32)
    return _pcall(body, name=name, out_shape=[jax.ShapeDtypeStruct((t, D_ATTN), F32), cshape, cshape],
                  grid=(N_HEADS, t // tq, nk),
                  in_specs=[qb, kb, kb, qb, qb, col, col, pl.BlockSpec((None, 1, tk), lambda h, i, j: (h, 0, jnp.minimum(j, i)))],
                  out_specs=[qb, col, col],
                  scratch=[pltpu.VMEM((tq, HEAD_DIM), F32), pltpu.VMEM((tq, 1), F32), pltpu.VMEM((tq, 1), F32)],
                  sem=("parallel", "parallel", "arbitrary"))(q, k, v, o, do, lse, fq, fk)


def _attn_bwd_kv(q, k, v, do, lse, delta, fq, fk, name):
    t = q.shape[0]
    tq = tk = _pick(t, ATTN_BLOCK, LANE)
    nq = t // tq

    def body(q_ref, k_ref, v_ref, do_ref, lse_ref, dl_ref, fq_ref, fk_ref, dk_ref, dv_ref, dfk_ref, dk_s, dv_s, df_s):
        j, i = pl.program_id(1), pl.program_id(2)

        @pl.when(i == 0)
        def _():
            dk_s[...] = jnp.zeros_like(dk_s)
            dv_s[...] = jnp.zeros_like(dv_s)
            df_s[...] = jnp.zeros_like(df_s)

        def step(diag):
            s = _logits(q_ref[...], k_ref[...], fq_ref[...], fk_ref[...], diag, tq, tk)
            p = jnp.exp(s - lse_ref[...])
            dv_s[...] += _dot(p.astype(BF16), do_ref[...], TN)
            dp = _dot(do_ref[...], v_ref[...], NT)
            ds = p * (dp - dl_ref[...])
            df_s[...] -= jnp.sum(ds, axis=0, keepdims=True)
            dk_s[...] += _dot(ds.astype(BF16), q_ref[...], TN)

        @pl.when(i > j)
        def _():
            step(False)

        @pl.when(i == j)
        def _():
            step(True)

        @pl.when(i == nq - 1)
        def _():
            dk_ref[...] = dk_s[...] * ATTN_SCALE
            dv_ref[...] = dv_s[...].astype(BF16)
            dfk_ref[...] = df_s[...]

    qb = pl.BlockSpec((tq, HEAD_DIM), lambda h, j, i: (jnp.maximum(i, j), h))
    kb = pl.BlockSpec((tk, HEAD_DIM), lambda h, j, i: (j, h))
    col = pl.BlockSpec((None, tq, 1), lambda h, j, i: (h, jnp.maximum(i, j), 0))
    row = pl.BlockSpec((None, 1, tk), lambda h, j, i: (h, 0, j))
    return _pcall(body, name=name,
                  out_shape=[jax.ShapeDtypeStruct((t, D_ATTN), F32), jax.ShapeDtypeStruct((t, D_ATTN), BF16),
                             jax.ShapeDtypeStruct((N_HEADS, 1, t), F32)],
                  grid=(N_HEADS, t // tk, nq),
                  in_specs=[qb, kb, kb, qb, col, col, col, row], out_specs=[kb, kb, row],
                  scratch=[pltpu.VMEM((tk, HEAD_DIM), F32), pltpu.VMEM((tk, HEAD_DIM), F32), pltpu.VMEM((1, tk), F32)],
                  sem=("parallel", "parallel", "arbitrary"))(q, k, v, do, lse, delta, fq, fk)


def _window_sum(v, w, back):
    n = v.shape[0]
    k = 1
    while k < w:
        v = v + pltpu.roll(v, k if back else n - k, axis=0)
        k *= 2
    return v


def _pool_fwd(proj, pw, ps, name):
    t = proj.shape[0]
    tm = _pick(t, POOL_BLOCK, HALO)
    col = 3 * D_ATTN // D_POOL

    def body(u_ref, prev_ref, pw_ref, ps_ref, pooled_ref, out_ref):
        i = pl.program_id(0)
        prev = jnp.where(i > 0, prev_ref[...], 0.0)
        ext = jnp.concatenate([prev, u_ref[...]], axis=0)
        pos = i * tm + lax.broadcasted_iota(jnp.int32, (tm, 1), 0)
        for g, w in enumerate(POOL_WINDOWS):
            cols = slice(g * GROUP_DIM, (g + 1) * GROUP_DIM)
            xg = ext[:, cols]
            sw = _window_sum(xg, w, True)[HALO:, :]
            cnt = jnp.minimum(pos + 1, w).astype(F32)
            pooled = (sw / cnt - xg[HALO:, :]).astype(BF16)
            pooled_ref[:, cols] = pooled
            out_ref[:, cols] = (_dot(pooled, pw_ref[g]) * ps_ref[:, cols]).astype(BF16)

    row = pl.BlockSpec((tm, D_POOL), lambda i: (i, 0))
    return _pcall(body, name=name, out_shape=[jax.ShapeDtypeStruct((t, D_POOL), BF16)] * 2, grid=(t // tm,),
                  in_specs=[pl.BlockSpec((tm, D_POOL), lambda i: (i, col)),
                            pl.BlockSpec((HALO, D_POOL), lambda i: (jnp.maximum(i * (tm // HALO) - 1, 0), col)),
                            pl.BlockSpec((len(POOL_WINDOWS), GROUP_DIM, GROUP_DIM), lambda i: (0, 0, 0)),
                            pl.BlockSpec((1, D_POOL), lambda i: (0, 0))],
                  out_specs=[row, row], sem=("parallel",))(proj, proj, pw, ps)


def _pool_bwd(dout, pooled, pw, ps, name):
    t = pooled.shape[0]
    tm = _pick(t, POOL_BLOCK, HALO)
    nb = t // tm
    ng = len(POOL_WINDOWS)

    def body(d_ref, nxt_ref, p_ref, pw_ref, ps_ref, du_ref, dpw_ref, dps_ref):
        i = pl.program_id(0)

        @pl.when(i == 0)
        def _():
            dpw_ref[...] = jnp.zeros_like(dpw_ref)
            dps_ref[...] = jnp.zeros_like(dps_ref)

        nxt = jnp.where(i < nb - 1, nxt_ref[...].astype(F32), 0.0)
        ext = jnp.concatenate([d_ref[...].astype(F32), nxt], axis=0)
        pos = i * tm + lax.broadcasted_iota(jnp.int32, (tm + HALO, 1), 0)
        for g, w in enumerate(POOL_WINDOWS):
            cols = slice(g * GROUP_DIM, (g + 1) * GROUP_DIM)
            pooled_g = p_ref[:, cols]
            dg = ext[:, cols]
            pm = _dot(pooled_g, pw_ref[g])
            dps_ref[:, cols] += jnp.sum(dg[:tm, :] * pm, axis=0, keepdims=True)
            dpm = (dg * ps_ref[:, cols]).astype(BF16)
            dpw_ref[g] += _dot(pooled_g, dpm[:tm, :], TN)
            dpooled = _dot(dpm, pw_ref[g], NT)
            cnt = jnp.minimum(pos + 1, w).astype(F32)
            fwd = _window_sum(dpooled / cnt, w, False)
            du_ref[:, cols] = (fwd[:tm, :] - dpooled[:tm, :]).astype(BF16)

    row = pl.BlockSpec((tm, D_POOL), lambda i: (i, 0))
    return _pcall(body, name=name,
                  out_shape=[jax.ShapeDtypeStruct((t, D_POOL), BF16), jax.ShapeDtypeStruct((ng, GROUP_DIM, GROUP_DIM), F32),
                             jax.ShapeDtypeStruct((1, D_POOL), F32)],
                  grid=(nb,),
                  in_specs=[pl.BlockSpec((tm, D_POOL), lambda i: (i, 1)),
                            pl.BlockSpec((HALO, D_POOL), lambda i: (jnp.minimum((i + 1) * (tm // HALO), t // HALO - 1), 1)),
                            row, pl.BlockSpec((ng, GROUP_DIM, GROUP_DIM), lambda i: (0, 0, 0)),
                            pl.BlockSpec((1, D_POOL), lambda i: (0, 0))],
                  out_specs=[row, pl.BlockSpec((ng, GROUP_DIM, GROUP_DIM), lambda i: (0, 0, 0)),
                             pl.BlockSpec((1, D_POOL), lambda i: (0, 0))],
                  sem=("arbitrary",))(dout, dout, pooled, pw, ps)


def _pad_cols(w, n):
    return jnp.pad(w, ((0, 0), (0, n - w.shape[1])))


def _ffn_fwd(x, norm_g, sh, sc, gate_half, wg_in, wg_out, tag):
    h = _norm_mod(x, norm_g, sh, sc, f"{tag}_norm")
    a, b, u = _ffn_up(h, wg_in, f"{tag}_up")
    yy, xo = _mm_resid(u, wg_out, x, gate_half, f"{tag}_down")
    return xo, (x, h, a, b, u, yy)


def _ffn_bwd(dxo, saved, norm_g, sc, gate_half, wg_in, wg_out, hu, tag):
    x, h, a, b, u, yy = saved
    dy, dgate = _gate_bwd(dxo, yy, gate_half, f"{tag}_gate_bwd")
    dab = _ffn_dab(dy, wg_out, a, b, f"{tag}_dab")
    d_wout = _mm_tn(u, dy, BF16, hu, f"{tag}_dwout")
    d_win = _ffn_dwin(h, dab, f"{tag}_dwin")
    dh = _ffn_dh(dab, wg_in, f"{tag}_dh")
    dx, dsh, dsc, dng = _norm_mod_bwd(dh, x, dxo, norm_g, sc, f"{tag}_norm_bwd")
    return dx, d_win, d_wout.reshape(4, hu, D), dsh, dsc, 0.5 * dgate, dng


def kernel(x, c, w_ada, b_ada, ffn1_norm_g, ffn1_w_in, ffn1_w_out, mix_norm_g, w_in, b_forget, q_norm_g, k_norm_g, pool_w, pool_scale, w_out, ffn2_norm_g, ffn2_w_in, ffn2_w_out, final_norm_g, loss_target, m_w_ada, m_b_ada, m_ffn1_norm_g, m_ffn1_w_in, m_ffn1_w_out, m_mix_norm_g, m_w_in, m_b_forget, m_q_norm_g, m_k_norm_g, m_pool_w, m_pool_scale, m_w_out, m_ffn2_norm_g, m_ffn2_w_in, m_ffn2_w_out, m_final_norm_g, v_w_ada, v_b_ada, v_ffn1_norm_g, v_ffn1_w_in, v_ffn1_w_out, v_mix_norm_g, v_w_in, v_b_forget, v_q_norm_g, v_k_norm_g, v_pool_w, v_pool_scale, v_w_out, v_ffn2_norm_g, v_ffn2_w_in, v_ffn2_w_out, v_final_norm_g):
    ax, ay, ac = _coords()
    chip = 2 * ax + ay
    me = 2 * chip + ac
    chip_arr = jnp.reshape(chip, (1,)).astype(jnp.int32)
    core_arr = jnp.reshape(ac, (1,)).astype(jnp.int32)

    t = x.shape[1]
    xs = x.reshape(t, D)
    tgt = loss_target.reshape(t, D)
    hu = ffn1_w_out.shape[1]
    hup = -(-hu // LANE) * LANE
    ws_in = w_in.shape[2]
    ws_in_pad = -(-ws_in // LANE) * LANE
    n_ada = w_ada.shape[2]

    def ffn_in_shard(w):
        w = w[0].astype(BF16)
        return jnp.concatenate([_pad_cols(w[:, :hu], hup), _pad_cols(w[:, hu:], hup)], axis=1)

    def ffn_out_shard(w):
        return jnp.pad(w[0].astype(BF16), ((0, hup - hu), (0, 0)))

    shards = [ffn_in_shard(ffn1_w_in), ffn_out_shard(ffn1_w_out), _pad_cols(w_in[0].astype(BF16), ws_in_pad),
              pool_w[0].astype(BF16).reshape(GROUP_DIM, GROUP_DIM), w_out[0].astype(BF16),
              ffn_in_shard(ffn2_w_in), ffn_out_shard(ffn2_w_out)]
    gathered = _gather_weights(shards, "gather_weights")
    g_in1, g_out1, g_win, g_pw, g_wout, g_in2, g_out2 = [
        lax.dynamic_update_slice(g, w[None], (chip, 0, 0)) for g, w in zip(gathered, shards)]
    wg_out1 = g_out1.reshape(4 * hup, D)
    wg_out2 = g_out2.reshape(4 * hup, D)
    w_full = jnp.concatenate([g_win[k, :, :ws_in] for k in range(4)], axis=1)
    nf = 3 * D_ATTN
    w_all = jnp.concatenate([w_full[:, :nf], w_full[:, nf + N_HEADS:], w_full[:, nf:nf + N_HEADS],
                             jnp.zeros((D, LANE - N_HEADS), BF16)], axis=1)
    pw_full = g_pw.reshape(4, 4, GROUP_DIM // 4, GROUP_DIM).transpose(1, 0, 2, 3).reshape(4, GROUP_DIM, GROUP_DIM)
    wo_full = g_wout.reshape(4 * g_wout.shape[1], D)

    c_all = _allgather_small(c.reshape(8, D // 8), True, "gather_c").reshape(8, D)
    c16 = jnp.pad(c_all, ((0, 8), (0, 0)))
    b_ada_mine = lax.dynamic_slice(b_ada, (0, chip * n_ada), (1, n_ada))
    act16, mod16 = _ada_fwd(c16, w_ada[0], b_ada_mine, "ada_fwd")
    mod_all = _allgather_small(mod16[:8], False, "gather_mod")
    mod = lax.dynamic_index_in_dim(mod_all, me, axis=1, keepdims=False).reshape(N_MOD, 1, D)
    sh1, sc1, gt1, sh2, sc2, gt2, sh3, sc3, gt3 = [mod[k] for k in range(N_MOD)]

    x1, saved1 = _ffn_fwd(xs, ffn1_norm_g, sh1, sc1, 0.5 * gt1, g_in1, wg_out1, "ffn1")

    h2 = _norm_mod(x1, mix_norm_g, sh2, sc2, "mix_norm")
    proj = _mm(h2, w_all, F32, "mix_proj")
    b_pad = jnp.pad(b_forget, ((0, 0), (0, LANE - N_HEADS)))
    cum = _forget_fwd(proj, b_pad, "forget_fwd")
    cum_t = cum[:, :N_HEADS].T
    fq, fk = cum_t.reshape(N_HEADS, t, 1), cum_t.reshape(N_HEADS, 1, t)
    qn, kn, vb = _qkv_prep(proj, q_norm_g, k_norm_g, "qkv_prep")
    attn, lse = _attn_fwd(qn, kn, vb, fq, fk, "attn_fwd")
    pooled, pool_out = _pool_fwd(proj, pw_full, pool_scale, "pool_fwd")
    cat = jnp.concatenate([attn, pool_out], axis=1)
    y_mix, x2 = _mm_resid(cat, wo_full, x1, gt2, "mix_out")

    x3, saved3 = _ffn_fwd(x2, ffn2_norm_g, sh3, sc3, 0.5 * gt3, g_in2, wg_out2, "ffn2")

    loss_part, dx3, d_final_g = _final_loss(x3, final_norm_g.reshape(1, D), tgt, "final_loss")
    loss = lax.psum(loss_part[0, 0], ("x", "y", "c"))

    dx2, d_in2, d_out2, dsh3, dsc3, dgt3, d_ng3 = _ffn_bwd(dx3, saved3, ffn2_norm_g, sc3, 0.5 * gt3, g_in2, wg_out2, hup, "ffn2")

    dz, dgt2 = _gate_bwd(dx2, y_mix, gt2, "mix_gate_bwd")
    dcat = _mm_nt(dz, wo_full, BF16, "mix_dcat")
    d_wo = _mm_tn(cat, dz, BF16, 1024, "mix_dwout")
    du_pool, d_pw, d_ps = _pool_bwd(dcat, pooled, pw_full, pool_scale, "pool_bwd")
    dqn, delta, dfq = _attn_bwd_q(qn, kn, vb, attn, dcat, lse, fq, fk, "attn_bwd_q")
    dkn, dv, dfk = _attn_bwd_kv(qn, kn, vb, dcat, lse, delta, fq, fk, "attn_bwd_kv")
    d_cum = jnp.pad((dfq.reshape(N_HEADS, t) + dfk.reshape(N_HEADS, t)).T, ((0, 0), (0, LANE - N_HEADS)))
    dfl, d_bf = _forget_bwd(d_cum, proj, b_pad, "forget_bwd")
    dq, dk, d_qg, d_kg = _qk_norm_bwd(dqn, dkn, proj, q_norm_g, k_norm_g, "qk_norm_bwd")
    dproj = jnp.concatenate([dq, dk, dv, du_pool, dfl], axis=1)
    d_wall = _mm_tn(h2, dproj, F32, 1024, "mix_dwin")
    dh2 = _mm_nt(dproj, w_all, F32, "mix_dh")
    dx1, dsh2, dsc2, d_ng2 = _norm_mod_bwd(dh2, x1, dx2, mix_norm_g, sc2, "mix_norm_bwd")

    dx0, d_in1, d_out1, dsh1, dsc1, dgt1, d_ng1 = _ffn_bwd(dx1, saved1, ffn1_norm_g, sc1, 0.5 * gt1, g_in1, wg_out1, hup, "ffn1")
    grad_x = dx0.reshape(1, t, D)

    d_wfull = jnp.concatenate([d_wall[:, :nf], d_wall[:, D_PROJ:D_PROJ + N_HEADS], d_wall[:, nf:D_PROJ]], axis=1)
    p_win = jnp.stack([_pad_cols(d_wfull[:, k * ws_in:(k + 1) * ws_in], ws_in_pad) for k in range(4)]).astype(BF16)
    p_pw = d_pw.reshape(4, 4, GROUP_DIM // 4, GROUP_DIM).transpose(1, 0, 2, 3).reshape(4, GROUP_DIM, GROUP_DIM).astype(BF16)
    parts = [d_in1, d_out1, p_win, p_pw, d_wo.reshape(4, D // 4, D), d_in2, d_out2]
    from_sib = _reduce_siblings(parts, "reduce_siblings")
    halves = [_add_sibling(p, r, core_arr, f"add_sibling_{k}") for k, (p, r) in enumerate(zip(parts, from_sib))]
    from_chips = _reduce_chips(halves, "reduce_chips")
    mine = [_add_chips(hh, r, chip_arr, f"add_chips_{k}") for k, (hh, r) in enumerate(zip(halves, from_chips))]
    theirs = _share_siblings(mine, "share_siblings")
    full = [jnp.where(ac == 0, jnp.concatenate([g, r], axis=0), jnp.concatenate([r, g], axis=0)) for g, r in zip(mine, theirs)]
    r_in1, r_out1, r_win, r_pw, r_wo, r_in2, r_out2 = full
    grads = {
        "ffn1_w_in": jnp.concatenate([r_in1[:, :hu], r_in1[:, hup:hup + hu]], axis=1),
        "ffn1_w_out": r_out1[:hu],
        "w_in": r_win[:, :ws_in],
        "pool_w": r_pw,
        "w_out": r_wo,
        "ffn2_w_in": jnp.concatenate([r_in2[:, :hu], r_in2[:, hup:hup + hu]], axis=1),
        "ffn2_w_out": r_out2[:hu],
    }

    dmod = jnp.concatenate([dsh1, dsc1, dgt1, dsh2, dsc2, dgt2, dsh3, dsc3, dgt3], axis=1)
    small_names = ["b_ada", "ffn1_norm_g", "mix_norm_g", "b_forget", "q_norm_g", "k_norm_g", "pool_scale", "ffn2_norm_g",
                   "final_norm_g"]
    small_grads = [dmod, d_ng1, d_ng2, d_bf[:, :N_HEADS], d_qg, d_kg, d_ps, d_ng3, d_final_g]
    small_w = [b_ada, ffn1_norm_g, mix_norm_g, b_forget, q_norm_g, k_norm_g, pool_scale, ffn2_norm_g, final_norm_g.reshape(1, D)]
    small_m = [m_b_ada, m_ffn1_norm_g, m_mix_norm_g, m_b_forget, m_q_norm_g, m_k_norm_g, m_pool_scale, m_ffn2_norm_g,
               m_final_norm_g.reshape(1, D)]
    small_v = [v_b_ada, v_ffn1_norm_g, v_mix_norm_g, v_b_forget, v_q_norm_g, v_k_norm_g, v_pool_scale, v_ffn2_norm_g,
               v_final_norm_g.reshape(1, D)]
    sizes = [g.shape[1] for g in small_grads]
    n_small = sum(sizes)
    n_pack = -(-n_small // (8 * LANE)) * (8 * LANE)

    def pack(vs, fill):
        flat = jnp.concatenate([v.reshape(1, -1) for v in vs], axis=1)
        return jnp.pad(flat, ((0, 0), (0, n_pack - n_small)), constant_values=fill).reshape(8, n_pack // 8)

    g8 = _allgather_small(pack(small_grads, 0.0), True, "gather_small_grads")
    gs, ds, ms, vs = _adamw_small(pack(small_w, 0.0), g8, pack(small_m, 0.0), pack(small_v, 1.0), "adamw_small")

    def unpack(p):
        flat = p.reshape(1, n_pack)
        out, off = {}, 0
        for nme, sz in zip(small_names, sizes):
            out[nme] = flat[:, off:off + sz]
            off += sz
        return out

    small = [unpack(p) for p in (gs, ds, ms, vs)]
    for dct in small:
        dct["final_norm_g"] = dct["final_norm_g"].reshape(D)

    dmod_all = g8.reshape(8, n_pack)[:, :N_MOD * D]
    dmod_mine = lax.dynamic_slice(dmod_all, (0, chip * n_ada), (8, n_ada))
    grads["w_ada"] = _ada_bwd(act16[:8].reshape(8, D, 1), dmod_mine.reshape(8, 1, n_ada), "ada_bwd")

    big = {"w_ada": (w_ada, m_w_ada, v_w_ada), "ffn1_w_in": (ffn1_w_in, m_ffn1_w_in, v_ffn1_w_in),
           "ffn1_w_out": (ffn1_w_out, m_ffn1_w_out, v_ffn1_w_out), "w_in": (w_in, m_w_in, v_w_in),
           "pool_w": (pool_w, m_pool_w, v_pool_w), "w_out": (w_out, m_w_out, v_w_out),
           "ffn2_w_in": (ffn2_w_in, m_ffn2_w_in, v_ffn2_w_in), "ffn2_w_out": (ffn2_w_out, m_ffn2_w_out, v_ffn2_w_out)}
    res = {}
    for nme, (w, m, v) in big.items():
        shp = w.shape
        g2 = grads[nme]
        two = g2.shape
        d, mo, vo = _adamw(w.reshape(two), g2, m.reshape(two), v.reshape(two), f"adamw_{nme}")
        res[nme] = (g2.reshape(shp), d.reshape(shp), mo.reshape(shp), vo.reshape(shp))
    for nme in small_names:
        res[nme] = tuple(dct[nme] for dct in small)

    order = ["w_ada", "b_ada", "ffn1_norm_g", "ffn1_w_in", "ffn1_w_out", "mix_norm_g", "w_in", "b_forget", "q_norm_g",
             "k_norm_g", "pool_w", "pool_scale", "w_out", "ffn2_norm_g", "ffn2_w_in", "ffn2_w_out", "final_norm_g"]
    return (loss, grad_x, *[res[n][0] for n in order], *[res[n][1] for n in order], *[res[n][2] for n in order],
            *[res[n][3] for n in order])
```

```python
import functools

import jax
import jax.numpy as jnp
from jax import lax
from jax.experimental import pallas as pl
from jax.experimental.pallas import tpu as pltpu

F32 = jnp.float32
BF16 = jnp.bfloat16

D = 2048
N_HEADS = 8
HEAD_DIM = 128
D_ATTN = 1024
D_POOL = 1024
POOL_WINDOWS = (2, 4, 8, 16)
GROUP_DIM = 256
HALO = 16
N_MOD = 9
EPS = 1e-6
D_PROJ = 3 * D_ATTN + D_POOL
D_PROJ_PAD = D_PROJ + 128
LANE = 128
ATTN_BLOCK = 512
POOL_BLOCK = 512

ADAM_LR = 0.001
ADAM_B1 = 0.9
ADAM_B2 = 0.999
ADAM_EPS = 1e-08
ADAM_WD = 0.01
ADAM_STEP = 10

VMEM_LIMIT_V7X = 56 * 1024 * 1024
MESH_ID = pl.DeviceIdType.MESH
ANY = pl.BlockSpec(memory_space=pl.ANY)
VMEM = pl.BlockSpec(memory_space=pltpu.VMEM)

NT = (((1,), (1,)), ((), ()))
TN = (((0,), (0,)), ((), ()))


class _Side:
    def __init__(self, ins, outs, nsem, start, wait, alias=False):
        self.ins, self.outs, self.nsem, self.start, self.wait, self.alias = list(ins), list(outs), nsem, start, wait, alias
        self.results = None


def _pcall(body, *, name, out_shape, grid=None, in_specs=None, out_specs=None, scratch=(), sem=None, prefetch=0, sides=()):
    sides = list(sides)
    single = not isinstance(out_shape, (list, tuple))
    shapes = [out_shape] if single else list(out_shape)
    in_specs = list(in_specs)
    ospecs = [out_specs] if single else list(out_specs)
    scratch = list(scratch)
    n_in, n_out, n_scr = len(in_specs), len(shapes), len(scratch)
    assert not (sides and prefetch)
    aliases = {}
    for sd in sides:
        if sd.alias:
            for k in range(len(sd.ins)):
                aliases[len(in_specs) + k] = len(shapes) + k
        in_specs += [ANY] * len(sd.ins)
        shapes += sd.outs
        ospecs += [ANY] * len(sd.outs)
        scratch += [pltpu.SemaphoreType.DMA((sd.nsem,)), pltpu.SemaphoreType.DMA((sd.nsem,))]

    def wrapped(*refs):
        ins, outs, scr = refs[:len(in_specs)], refs[len(in_specs):len(in_specs) + len(shapes)], refs[len(in_specs) + len(shapes):]
        first = last = None
        for ax, g in enumerate(grid or ()):
            p = pl.program_id(ax)
            first = (p == 0) if first is None else first & (p == 0)
            last = (p == g - 1) if last is None else last & (p == g - 1)

        def each(what):
            i0, o0 = n_in, n_out
            for k, sd in enumerate(sides):
                getattr(sd, what)(ins[i0:i0 + len(sd.ins)], outs[o0:o0 + len(sd.outs)], scr[n_scr + 2 * k], scr[n_scr + 2 * k + 1])
                i0, o0 = i0 + len(sd.ins), o0 + len(sd.outs)

        if first is None:
            each("start")
        else:
            pl.when(first)(lambda: each("start"))
        body(*ins[:n_in], *outs[:n_out], *scr[:n_scr])
        if last is None:
            each("wait")
        else:
            pl.when(last)(lambda: each("wait"))

    params = dict(vmem_limit_bytes=VMEM_LIMIT_V7X)
    if sides and grid:
        params["dimension_semantics"] = ("arbitrary",) * len(grid)
    elif sem is not None:
        params["dimension_semantics"] = sem
    kw = dict(name=name, out_shape=shapes if (sides or not single) else shapes[0], compiler_params=pltpu.CompilerParams(**params))
    if aliases:
        kw["input_output_aliases"] = aliases
    final_ospecs = ospecs if (sides or not single) else ospecs[0]
    if prefetch:
        kw["grid_spec"] = pltpu.PrefetchScalarGridSpec(
            num_scalar_prefetch=prefetch, grid=grid, in_specs=in_specs, out_specs=final_ospecs, scratch_shapes=scratch)
    else:
        if grid is not None:
            kw["grid"] = grid
        kw["in_specs"] = in_specs
        kw["out_specs"] = final_ospecs
        kw["scratch_shapes"] = scratch
    call = pl.pallas_call(wrapped if sides else body, **kw)
    if not sides:
        return call

    def run(*operands):
        res = list(call(*operands, *[a for sd in sides for a in sd.ins]))
        o0 = n_out
        for sd in sides:
            sd.results = res[o0:o0 + len(sd.outs)]
            o0 += len(sd.outs)
        return res[0] if single else res[:n_out]

    return run


def _pick(n, cap, mult):
    best = None
    for d in range(mult, min(n, cap) + 1, mult):
        if n % d == 0:
            best = d
    assert best is not None, (n, cap, mult)
    return best


def _coords():
    return lax.axis_index("x"), lax.axis_index("y"), lax.axis_index("c")


def _dot(a, b, dims=None):
    if dims is None:
        return jnp.dot(a, b, preferred_element_type=F32)
    return lax.dot_general(a, b, dims, preferred_element_type=F32)


def _remote(src, dst, ssem, rsem, dev):
    return pltpu.make_async_remote_copy(src_ref=src, dst_ref=dst, send_sem=ssem, recv_sem=rsem,
                                        device_id=dev, device_id_type=MESH_ID)


def _allgather_small(v, whole_mesh, name):
    masks = list(range(1, 8)) if whole_mesh else [4, 2, 6]
    nslot = 8 if whole_mesh else 4

    def slot(px, py, pc):
        return 4 * px + 2 * py + pc if whole_mesh else 2 * px + py

    def body(v_ref, out_ref, ssem, rsem, lsem):
        x, y, c = _coords()
        mine = slot(x, y, c)
        peers = [(jnp.bitwise_xor(x, (m >> 2) & 1), jnp.bitwise_xor(y, (m >> 1) & 1), jnp.bitwise_xor(c, m & 1))
                 for m in masks]
        loc = pltpu.make_async_copy(v_ref, out_ref.at[mine], lsem)
        loc.start()
        sends = [_remote(v_ref, out_ref.at[mine], ssem.at[k], rsem.at[k], p) for k, p in enumerate(peers)]
        for cp in sends:
            cp.start()
        for k, p in enumerate(peers):
            _remote(v_ref, out_ref.at[slot(*p)], ssem.at[k], rsem.at[k], p).wait_recv()
        for cp in sends:
            cp.wait_send()
        loc.wait()

    return _pcall(body, name=name, out_shape=jax.ShapeDtypeStruct((nslot,) + v.shape, v.dtype),
                  in_specs=[VMEM], out_specs=VMEM,
                  scratch=[pltpu.SemaphoreType.DMA((len(masks),)), pltpu.SemaphoreType.DMA((len(masks),)),
                           pltpu.SemaphoreType.DMA(())])(v)


def _other_chips(x, y):
    return [(1 - x, y), (x, 1 - y), (1 - x, 1 - y)]


def _half_rows(shard_rows, core):
    h = shard_rows // 2
    return pl.ds(pl.multiple_of(core * h, 16), h)


def _drain(pairs):
    for _, recv in pairs:
        recv.wait_recv()
    for send, _ in pairs:
        send.wait_send()


def _side_from(ins, outs, nsem, pairs_of, alias=False):
    def start(*refs):
        for send, _ in pairs_of(*refs):
            send.start()

    def wait(*refs):
        _drain(pairs_of(*refs))

    return _Side(ins, outs, nsem, start, wait, alias)


def _gather_ici(ws):
    def pairs_of(w_refs, g_refs, ssem, rsem):
        x, y, c = _coords()
        me, out = 2 * x + y, []
        for a, w in enumerate(ws):
            rows = _half_rows(w.shape[0], c)
            for j, chip in enumerate(_other_chips(x, y)):
                sems = (ssem.at[3 * a + j], rsem.at[3 * a + j], (*chip, c))
                got = g_refs[a].at[2 * chip[0] + chip[1], rows]
                out.append((_remote(w_refs[a].at[rows], g_refs[a].at[me, rows], *sems), _remote(got, got, *sems)))
        return out

    return _side_from(ws, [jax.ShapeDtypeStruct((4,) + w.shape, w.dtype) for w in ws], 3 * len(ws), pairs_of)


def _gather_d2d(gs):
    def pairs_of(_, g_refs, ssem, rsem):
        x, y, c = _coords()
        out = []
        for a, g in enumerate(gs):
            for j, chip in enumerate(_other_chips(x, y)):
                sems = (ssem.at[3 * a + j], rsem.at[3 * a + j], (x, y, 1 - c))
                mine = g_refs[a].at[2 * chip[0] + chip[1], _half_rows(g.shape[1], c)]
                theirs = g_refs[a].at[2 * chip[0] + chip[1], _half_rows(g.shape[1], 1 - c)]
                out.append((_remote(mine, mine, *sems), _remote(theirs, theirs, *sems)))
        return out

    return _side_from(gs, [jax.ShapeDtypeStruct(g.shape, g.dtype) for g in gs], 3 * len(gs), pairs_of, alias=True)


def _reduce_siblings(ps):
    def pairs_of(p_refs, r_refs, ssem, rsem):
        x, y, c = _coords()
        out = []
        for a, p in enumerate(ps):
            src = p_refs[a].at[:, _half_rows(p.shape[1], 1 - c), :]
            cp = _remote(src, r_refs[a], ssem.at[a], rsem.at[a], (x, y, 1 - c))
            out.append((cp, cp))
        return out

    return _side_from(ps, [jax.ShapeDtypeStruct((4, p.shape[1] // 2, p.shape[2]), p.dtype) for p in ps], len(ps), pairs_of)


def _reduce_chips(hs):
    def pairs_of(h_refs, o_refs, ssem, rsem):
        x, y, c = _coords()
        out = []
        for a in range(len(hs)):
            for j, chip in enumerate(_other_chips(x, y)):
                cp = _remote(h_refs[a].at[2 * chip[0] + chip[1]], o_refs[a].at[j], ssem.at[3 * a + j], rsem.at[3 * a + j], (*chip, c))
                out.append((cp, cp))
        return out

    return _side_from(hs, [jax.ShapeDtypeStruct((3,) + h.shape[1:], h.dtype) for h in hs], 3 * len(hs), pairs_of)


def _share_siblings(gs):
    def pairs_of(g_refs, o_refs, ssem, rsem):
        x, y, c = _coords()
        cps = [_remote(g_refs[a], o_refs[a], ssem.at[a], rsem.at[a], (x, y, 1 - c)) for a in range(len(gs))]
        return [(cp, cp) for cp in cps]

    return _side_from(gs, [jax.ShapeDtypeStruct(g.shape, g.dtype) for g in gs], len(gs), pairs_of)


def _exchange(side, name):
    def body():
        pass

    _pcall(body, name=name, out_shape=[], in_specs=[], out_specs=[], sides=[side])()
    return side.results


def _add_sibling(p, r, core, name):
    _, rr, cc = p.shape
    h = rr // 2
    th = _pick(h, max(16, (2 << 20) // (2 * cc)), 16)
    nb = h // th

    def body(c_ref, p_ref, r_ref, o_ref):
        o_ref[...] = (p_ref[...].astype(F32) + r_ref[...].astype(F32)).astype(BF16)

    return _pcall(body, name=name, out_shape=jax.ShapeDtypeStruct((4, h, cc), BF16), grid=(4, nb),
                  in_specs=[pl.BlockSpec((None, th, cc), lambda k, i, c_ref: (k, c_ref[0] * nb + i, 0)),
                            pl.BlockSpec((None, th, cc), lambda k, i, c_ref: (k, i, 0))],
                  out_specs=pl.BlockSpec((None, th, cc), lambda k, i, c_ref: (k, i, 0)),
                  sem=("parallel", "parallel"), prefetch=1)(core, p, r)


def _add_chips(hh, r, chip, name):
    _, h, cc = hh.shape
    th = _pick(h, max(16, (2 << 20) // (2 * cc)), 16)

    def body(k_ref, h_ref, r_ref, o_ref):
        s = h_ref[...].astype(F32) + r_ref[0].astype(F32)
        s = s + r_ref[1].astype(F32)
        o_ref[...] = s + r_ref[2].astype(F32)

    return _pcall(body, name=name, out_shape=jax.ShapeDtypeStruct((h, cc), F32), grid=(h // th,),
                  in_specs=[pl.BlockSpec((None, th, cc), lambda i, k_ref: (k_ref[0], i, 0)),
                            pl.BlockSpec((3, th, cc), lambda i, k_ref: (0, i, 0))],
                  out_specs=pl.BlockSpec((th, cc), lambda i, k_ref: (i, 0)),
                  sem=("parallel",), prefetch=1)(chip, hh, r)


def _adamw_math(w, g, m, v):
    m = ADAM_B1 * m + (1.0 - ADAM_B1) * g
    v = ADAM_B2 * v + (1.0 - ADAM_B2) * (g * g)
    m_hat = m / (1.0 - ADAM_B1 ** ADAM_STEP)
    v_hat = v / (1.0 - ADAM_B2 ** ADAM_STEP)
    delta = -ADAM_LR * (m_hat / (jnp.sqrt(v_hat) + ADAM_EPS) + ADAM_WD * w)
    return delta, m, v


def _adamw(w, g, m, v, name):
    rr, cc = w.shape
    tr = _pick(rr, max(8, (3 << 20) // (4 * cc)), 8)

    def body(w_ref, g_ref, m_ref, v_ref, d_ref, mo_ref, vo_ref):
        d, mm, vv = _adamw_math(w_ref[...], g_ref[...], m_ref[...], v_ref[...])
        d_ref[...] = d
        mo_ref[...] = mm
        vo_ref[...] = vv

    spec = pl.BlockSpec((tr, cc), lambda i: (i, 0))
    return _pcall(body, name=name, out_shape=[jax.ShapeDtypeStruct(w.shape, F32)] * 3, grid=(rr // tr,),
                  in_specs=[spec] * 4, out_specs=[spec] * 3, sem=("parallel",))(w, g, m, v)


def _adamw_small(w, g8, m, v, name):
    def body(w_ref, g_ref, m_ref, v_ref, go_ref, d_ref, mo_ref, vo_ref):
        g = g_ref[0]
        for k in range(1, 8):
            g = g + g_ref[k]
        d, mm, vv = _adamw_math(w_ref[...], g, m_ref[...], v_ref[...])
        go_ref[...] = g
        d_ref[...] = d
        mo_ref[...] = mm
        vo_ref[...] = vv

    return _pcall(body, name=name, out_shape=[jax.ShapeDtypeStruct(w.shape, F32)] * 4,
                  in_specs=[VMEM] * 4, out_specs=[VMEM] * 4)(w, g8, m, v)


def _ada_fwd(c16, w_ada, b_ada, name):
    n = w_ada.shape[1]
    tn = _pick(n, 512, LANE)

    def body(c_ref, w_ref, b_ref, act_ref, mod_ref):
        cv = c_ref[...]
        act = cv * jax.nn.sigmoid(cv)
        act_ref[...] = act
        mod_ref[...] = _dot(act.astype(BF16), w_ref[...].astype(BF16)) + b_ref[...]

    return _pcall(body, name=name,
                  out_shape=[jax.ShapeDtypeStruct((16, D), F32), jax.ShapeDtypeStruct((16, n), F32)], grid=(n // tn,),
                  in_specs=[pl.BlockSpec((16, D), lambda j: (0, 0)), pl.BlockSpec((D, tn), lambda j: (0, j)),
                            pl.BlockSpec((1, tn), lambda j: (0, j))],
                  out_specs=[pl.BlockSpec((16, D), lambda j: (0, 0)), pl.BlockSpec((16, tn), lambda j: (0, j))],
                  sem=("arbitrary",))(c16, w_ada, b_ada)


def _ada_bwd(act, dmod, name):
    n = dmod.shape[2]
    tm, tn = 256, _pick(n, 512, LANE)

    def body(a_ref, d_ref, o_ref):
        def term(b):
            return a_ref[b].astype(BF16).astype(F32) * d_ref[b].astype(BF16).astype(F32)

        acc = term(0)
        for b in range(1, 8):
            acc = acc + term(b)
        o_ref[...] = acc

    return _pcall(body, name=name, out_shape=jax.ShapeDtypeStruct((D, n), F32), grid=(D // tm, n // tn),
                  in_specs=[pl.BlockSpec((8, tm, 1), lambda i, j: (0, i, 0)), pl.BlockSpec((8, 1, tn), lambda i, j: (0, 0, j))],
                  out_specs=pl.BlockSpec((tm, tn), lambda i, j: (i, j)), sem=("parallel", "parallel"))(act, dmod)


def _norm_mod(x, g, sh, sc, name):
    t = x.shape[0]
    tm = _pick(t, 512, 8)

    def body(x_ref, g_ref, sh_ref, sc_ref, h_ref):
        xf = x_ref[...]
        r = lax.rsqrt(jnp.mean(xf * xf, axis=-1, keepdims=True) + EPS)
        h = (xf * r) * g_ref[...]
        h_ref[...] = (h * (1.0 + sc_ref[...]) + sh_ref[...]).astype(BF16)

    vec = pl.BlockSpec((1, D), lambda i: (0, 0))
    row = pl.BlockSpec((tm, D), lambda i: (i, 0))
    return _pcall(body, name=name, out_shape=jax.ShapeDtypeStruct((t, D), BF16), grid=(t // tm,),
                  in_specs=[row, vec, vec, vec], out_specs=row, sem=("parallel",))(x, g, sh, sc)


def _norm_mod_bwd(dh, x, dxo, g, sc, name):
    t = x.shape[0]
    tm = _pick(t, 256, 8)

    def body(dh_ref, x_ref, dxo_ref, g_ref, sc_ref, dx_ref, dsh_ref, dsc_ref, dg_ref):
        @pl.when(pl.program_id(0) == 0)
        def _():
            dsh_ref[...] = jnp.zeros_like(dsh_ref)
            dsc_ref[...] = jnp.zeros_like(dsc_ref)
            dg_ref[...] = jnp.zeros_like(dg_ref)

        xf, dh_ = x_ref[...], dh_ref[...]
        r = lax.rsqrt(jnp.mean(xf * xf, axis=-1, keepdims=True) + EPS)
        xhat = xf * r
        dsh_ref[...] += jnp.sum(dh_, axis=0, keepdims=True)
        dsc_ref[...] += jnp.sum(dh_ * (xhat * g_ref[...]), axis=0, keepdims=True)
        tt = dh_ * (1.0 + sc_ref[...])
        dg_ref[...] += jnp.sum(tt * xhat, axis=0, keepdims=True)
        dxh = tt * g_ref[...]
        dx_ref[...] = r * (dxh - xhat * jnp.mean(dxh * xhat, axis=-1, keepdims=True)) + dxo_ref[...]

    vec = pl.BlockSpec((1, D), lambda i: (0, 0))
    row = pl.BlockSpec((tm, D), lambda i: (i, 0))
    vshape = jax.ShapeDtypeStruct((1, D), F32)
    return _pcall(body, name=name, out_shape=[jax.ShapeDtypeStruct((t, D), F32), vshape, vshape, vshape], grid=(t // tm,),
                  in_specs=[row, row, row, vec, vec], out_specs=[row, vec, vec, vec], sem=("arbitrary",))(dh, x, dxo, g, sc)


def _gate_bwd(dxo, yy, gate, name):
    t = dxo.shape[0]
    tm = _pick(t, 512, 8)

    def body(dx_ref, y_ref, g_ref, dy_ref, dg_ref):
        @pl.when(pl.program_id(0) == 0)
        def _():
            dg_ref[...] = jnp.zeros_like(dg_ref)

        dx = dx_ref[...]
        dy_ref[...] = (dx * g_ref[...]).astype(BF16)
        dg_ref[...] += jnp.sum(dx * y_ref[...], axis=0, keepdims=True)

    vec = pl.BlockSpec((1, D), lambda i: (0, 0))
    row = pl.BlockSpec((tm, D), lambda i: (i, 0))
    return _pcall(body, name=name, out_shape=[jax.ShapeDtypeStruct((t, D), BF16), jax.ShapeDtypeStruct((1, D), F32)],
                  grid=(t // tm,), in_specs=[row, row, vec], out_specs=[row, vec], sem=("arbitrary",))(dxo, yy, gate)


def _final_loss(x, g, tgt, name):
    t = x.shape[0]
    tm = _pick(t, 256, 8)

    def body(x_ref, g_ref, t_ref, loss_ref, dx_ref, dg_ref):
        @pl.when(pl.program_id(0) == 0)
        def _():
            loss_ref[...] = jnp.zeros_like(loss_ref)
            dg_ref[...] = jnp.zeros_like(dg_ref)

        xf = x_ref[...]
        r = lax.rsqrt(jnp.mean(xf * xf, axis=-1, keepdims=True) + EPS)
        xhat = xf * r
        e = xhat * g_ref[...] - t_ref[...]
        per_tok = jnp.mean(e * e, axis=-1, keepdims=True)
        loss_ref[...] += 0.5 * jnp.sum(per_tok, axis=0, keepdims=True)
        dy = e * (1.0 / D)
        dg_ref[...] += jnp.sum(dy * xhat, axis=0, keepdims=True)
        dxh = dy * g_ref[...]
        dx_ref[...] = r * (dxh - xhat * jnp.mean(dxh * xhat, axis=-1, keepdims=True))

    vec = pl.BlockSpec((1, D), lambda i: (0, 0))
    row = pl.BlockSpec((tm, D), lambda i: (i, 0))
    return _pcall(body, name=name,
                  out_shape=[jax.ShapeDtypeStruct((1, LANE), F32), jax.ShapeDtypeStruct((t, D), F32),
                             jax.ShapeDtypeStruct((1, D), F32)],
                  grid=(t // tm,), in_specs=[row, vec, row],
                  out_specs=[pl.BlockSpec((1, LANE), lambda i: (0, 0)), row, vec], sem=("arbitrary",))(x, g, tgt)


def _mm(a, b, out_dtype, name):
    m, k = a.shape
    n = b.shape[1]
    tm = _pick(m, 1024, 8)
    tn = _pick(n, 1408, LANE)

    def body(a_ref, b_ref, o_ref):
        o_ref[...] = _dot(a_ref[...], b_ref[...]).astype(out_dtype)

    return _pcall(body, name=name, out_shape=jax.ShapeDtypeStruct((m, n), out_dtype), grid=(m // tm, n // tn),
                  in_specs=[pl.BlockSpec((tm, k), lambda i, j: (i, 0)), pl.BlockSpec((k, tn), lambda i, j: (0, j))],
                  out_specs=pl.BlockSpec((tm, tn), lambda i, j: (i, j)), sem=("parallel", "parallel"))(a, b)


def _mm_resid(a, b, resid, gate, name, sides=()):
    m, k = a.shape
    n = b.shape[1]
    tm, tn, tk = _pick(m, 1024, 8), _pick(n, 1024, LANE), _pick(k, 512, LANE)
    nk = k // tk

    def body(a_ref, b_ref, r_ref, g_ref, y_ref, o_ref, acc):
        kk = pl.program_id(2)

        @pl.when(kk == 0)
        def _():
            acc[...] = jnp.zeros_like(acc)

        acc[...] += _dot(a_ref[...], b_ref[...])

        @pl.when(kk == nk - 1)
        def _():
            y_ref[...] = acc[...]
            o_ref[...] = r_ref[...] + g_ref[...] * acc[...]

    blk = pl.BlockSpec((tm, tn), lambda i, j, kk: (i, j))
    return _pcall(body, name=name, out_shape=[jax.ShapeDtypeStruct((m, n), F32)] * 2, grid=(m // tm, n // tn, nk),
                  in_specs=[pl.BlockSpec((tm, tk), lambda i, j, kk: (i, kk)), pl.BlockSpec((tk, tn), lambda i, j, kk: (kk, j)),
                            blk, pl.BlockSpec((1, tn), lambda i, j, kk: (0, j))],
                  out_specs=[blk, blk], scratch=[pltpu.VMEM((tm, tn), F32)],
                  sem=("parallel", "parallel", "arbitrary"), sides=sides)(a, b, resid, gate)


def _mm_nt(a, b, out_dtype, name, sides=()):
    m, k = a.shape
    n = b.shape[0]
    tm, tn, tk = _pick(m, 1024, 8), _pick(n, 1024, LANE), _pick(k, 1408, LANE)
    nk = k // tk

    def body(a_ref, b_ref, o_ref, acc):
        kk = pl.program_id(2)

        @pl.when(kk == 0)
        def _():
            acc[...] = jnp.zeros_like(acc)

        acc[...] += _dot(a_ref[...], b_ref[...], NT)

        @pl.when(kk == nk - 1)
        def _():
            o_ref[...] = acc[...].astype(out_dtype)

    return _pcall(body, name=name, out_shape=jax.ShapeDtypeStruct((m, n), out_dtype), grid=(m // tm, n // tn, nk),
                  in_specs=[pl.BlockSpec((tm, tk), lambda i, j, kk: (i, kk)), pl.BlockSpec((tn, tk), lambda i, j, kk: (j, kk))],
                  out_specs=pl.BlockSpec((tm, tn), lambda i, j, kk: (i, j)), scratch=[pltpu.VMEM((tm, tn), F32)],
                  sem=("parallel", "parallel", "arbitrary"), sides=sides)(a, b)


def _mm_tn(a, b, out_dtype, tm_cap, name, sides=()):
    t, m = a.shape
    n = b.shape[1]
    tm, tn, tk = _pick(m, tm_cap, LANE), _pick(n, 1408, LANE), _pick(t, 512, 16)
    nk = t // tk

    def body(a_ref, b_ref, o_ref, acc):
        kk = pl.program_id(2)

        @pl.when(kk == 0)
        def _():
            acc[...] = jnp.zeros_like(acc)

        acc[...] += _dot(a_ref[...], b_ref[...], TN)

        @pl.when(kk == nk - 1)
        def _():
            o_ref[...] = acc[...].astype(out_dtype)

    return _pcall(body, name=name, out_shape=jax.ShapeDtypeStruct((m, n), out_dtype), grid=(m // tm, n // tn, nk),
                  in_specs=[pl.BlockSpec((tk, tm), lambda i, j, kk: (kk, i)), pl.BlockSpec((tk, tn), lambda i, j, kk: (kk, j))],
                  out_specs=pl.BlockSpec((tm, tn), lambda i, j, kk: (i, j)), scratch=[pltpu.VMEM((tm, tn), F32)],
                  sem=("parallel", "parallel", "arbitrary"), sides=sides)(a, b)


def _ffn_up(h, wg, name, sides=()):
    t = h.shape[0]
    fp = wg.shape[2]
    tm, tn = _pick(t, 1024, 8), 256
    nn = fp // tn

    def body(h_ref, wa_ref, wb_ref, a_ref, b_ref, u_ref):
        hh = h_ref[...]
        a = _dot(hh, wa_ref[...])
        b = _dot(hh, wb_ref[...])
        a_ref[...] = a.astype(BF16)
        b_ref[...] = b.astype(BF16)
        u_ref[...] = (a * jax.nn.sigmoid(a) * b).astype(BF16)

    out = pl.BlockSpec((tm, tn), lambda i, j, n: (i, j * nn + n))
    return _pcall(body, name=name, out_shape=[jax.ShapeDtypeStruct((t, 2 * fp), BF16)] * 3, grid=(t // tm, 2, nn),
                  in_specs=[pl.BlockSpec((tm, D), lambda i, j, n: (i, 0)),
                            pl.BlockSpec((None, D, tn), lambda i, j, n: (j, 0, n)),
                            pl.BlockSpec((None, D, tn), lambda i, j, n: (j + 2, 0, n))],
                  out_specs=[out, out, out], sem=("parallel", "parallel", "parallel"), sides=sides)(h, wg, wg)


def _ffn_dab(dy, wo, a, b, name, sides=()):
    t = dy.shape[0]
    f2 = wo.shape[0]
    tm, tn = _pick(t, 1024, 8), 256

    def body(dy_ref, w_ref, a_ref, b_ref, o_ref):
        du = _dot(dy_ref[...], w_ref[...], NT)
        av, bv = a_ref[...].astype(F32), b_ref[...].astype(F32)
        sg = jax.nn.sigmoid(av)
        o_ref[0] = (du * bv * (sg * (1.0 + av * (1.0 - sg)))).astype(BF16)
        o_ref[1] = (du * (av * sg)).astype(BF16)

    blk = pl.BlockSpec((tm, tn), lambda i, n: (i, n))
    return _pcall(body, name=name, out_shape=jax.ShapeDtypeStruct((2, t, f2), BF16), grid=(t // tm, f2 // tn),
                  in_specs=[pl.BlockSpec((tm, D), lambda i, n: (i, 0)), pl.BlockSpec((tn, D), lambda i, n: (n, 0)), blk, blk],
                  out_specs=pl.BlockSpec((2, tm, tn), lambda i, n: (0, i, n)), sem=("parallel", "parallel"), sides=sides)(dy, wo, a, b)


def _ffn_dwin(h, dab, name, sides=()):
    t = h.shape[0]
    fp = dab.shape[2] // 2
    tm, tn, tk = 1024, _pick(fp, 1408, LANE), _pick(t, 512, 16)
    nn, nk = fp // tn, t // tk

    def body(h_ref, d_ref, o_ref, acc):
        kk = pl.program_id(3)

        @pl.when(kk == 0)
        def _():
            acc[...] = jnp.zeros_like(acc)

        acc[...] += _dot(h_ref[...], d_ref[...], TN)

        @pl.when(kk == nk - 1)
        def _():
            o_ref[...] = acc[...].astype(BF16)

    return _pcall(body, name=name, out_shape=jax.ShapeDtypeStruct((4, D, fp), BF16), grid=(4, D // tm, nn, nk),
                  in_specs=[pl.BlockSpec((tk, tm), lambda u, i, n, kk: (kk, i)),
                            pl.BlockSpec((None, tk, tn), lambda u, i, n, kk: (u // 2, kk, (u % 2) * nn + n))],
                  out_specs=pl.BlockSpec((None, tm, tn), lambda u, i, n, kk: (u, i, n)),
                  scratch=[pltpu.VMEM((tm, tn), F32)],
                  sem=("parallel", "parallel", "parallel", "arbitrary"), sides=sides)(h, dab)


def _ffn_dh(dab, wg, name, sides=()):
    t = dab.shape[1]
    fp = wg.shape[2]
    tm, tn, tk = _pick(t, 1024, 8), 1024, _pick(fp, 1408, LANE)
    nkk = fp // tk

    def body(d_ref, w_ref, o_ref, acc):
        u, kk = pl.program_id(2), pl.program_id(3)

        @pl.when((u == 0) & (kk == 0))
        def _():
            acc[...] = jnp.zeros_like(acc)

        acc[...] += _dot(d_ref[...], w_ref[...], NT)

        @pl.when((u == 3) & (kk == nkk - 1))
        def _():
            o_ref[...] = acc[...]

    return _pcall(body, name=name, out_shape=jax.ShapeDtypeStruct((t, D), F32), grid=(t // tm, D // tn, 4, nkk),
                  in_specs=[pl.BlockSpec((None, tm, tk), lambda i, j, u, kk: (u // 2, i, (u % 2) * nkk + kk)),
                            pl.BlockSpec((None, tn, tk), lambda i, j, u, kk: (u, j, kk))],
                  out_specs=pl.BlockSpec((tm, tn), lambda i, j, u, kk: (i, j)), scratch=[pltpu.VMEM((tm, tn), F32)],
                  sem=("parallel", "parallel", "arbitrary", "arbitrary"), sides=sides)(dab, wg)


def _split3(v):
    hi = v.astype(BF16)
    r1 = v - hi.astype(F32)
    mid = r1.astype(BF16)
    lo = (r1 - mid.astype(F32)).astype(BF16)
    return hi, mid, lo


def _tri_sum(tri, v):
    hi, mid, lo = _split3(v)
    return (_dot(tri, hi) + _dot(tri, mid)) + _dot(tri, lo)


def _forget_fwd(proj, b_pad, name):
    t = proj.shape[0]
    tb = _pick(t, 256, 8)
    col = D_PROJ // LANE

    def body(f_ref, b_ref, o_ref, carry):
        @pl.when(pl.program_id(0) == 0)
        def _():
            carry[...] = jnp.zeros_like(carry)

        z = f_ref[...] + b_ref[...]
        lf = jnp.minimum(z, 0.0) - jnp.log(1.0 + jnp.exp(-jnp.abs(z)))
        r = lax.broadcasted_iota(jnp.int32, (tb, tb), 0)
        cidx = lax.broadcasted_iota(jnp.int32, (tb, tb), 1)
        tri = (r >= cidx).astype(BF16)
        o_ref[...] = _tri_sum(tri, lf) + carry[...]
        carry[...] += jnp.sum(lf, axis=0, keepdims=True)

    return _pcall(body, name=name, out_shape=jax.ShapeDtypeStruct((t, LANE), F32), grid=(t // tb,),
                  in_specs=[pl.BlockSpec((tb, LANE), lambda i: (i, col)), pl.BlockSpec((1, LANE), lambda i: (0, 0))],
                  out_specs=pl.BlockSpec((tb, LANE), lambda i: (i, 0)), scratch=[pltpu.VMEM((1, LANE), F32)],
                  sem=("arbitrary",))(proj, b_pad)


def _forget_bwd(d_cum, proj, b_pad, name):
    t = proj.shape[0]
    tb = _pick(t, 256, 8)
    nb = t // tb
    col = D_PROJ // LANE

    def body(d_ref, f_ref, b_ref, o_ref, db_ref, carry):
        @pl.when(pl.program_id(0) == 0)
        def _():
            carry[...] = jnp.zeros_like(carry)
            db_ref[...] = jnp.zeros_like(db_ref)

        dc = d_ref[...]
        r = lax.broadcasted_iota(jnp.int32, (tb, tb), 0)
        cidx = lax.broadcasted_iota(jnp.int32, (tb, tb), 1)
        tri = (r <= cidx).astype(BF16)
        dlf = _tri_sum(tri, dc) + carry[...]
        carry[...] += jnp.sum(dc, axis=0, keepdims=True)
        z = f_ref[...] + b_ref[...]
        lane = lax.broadcasted_iota(jnp.int32, (tb, LANE), 1)
        dz = jnp.where(lane < N_HEADS, dlf * jax.nn.sigmoid(-z), 0.0)
        o_ref[...] = dz.astype(BF16)
        db_ref[...] += jnp.sum(dz, axis=0, keepdims=True)

    return _pcall(body, name=name, out_shape=[jax.ShapeDtypeStruct((t, LANE), BF16), jax.ShapeDtypeStruct((1, LANE), F32)],
                  grid=(nb,),
                  in_specs=[pl.BlockSpec((tb, LANE), lambda i: (nb - 1 - i, 0)),
                            pl.BlockSpec((tb, LANE), lambda i: (nb - 1 - i, col)),
                            pl.BlockSpec((1, LANE), lambda i: (0, 0))],
                  out_specs=[pl.BlockSpec((tb, LANE), lambda i: (nb - 1 - i, 0)), pl.BlockSpec((1, LANE), lambda i: (0, 0))],
                  scratch=[pltpu.VMEM((1, LANE), F32)], sem=("arbitrary",))(d_cum, proj, b_pad)


def _head_norm(v, g):
    r = lax.rsqrt(jnp.mean(v * v, axis=-1, keepdims=True) + EPS)
    return v * r, r


def _qkv_prep(proj, qg, kg, name):
    t = proj.shape[0]
    tm = _pick(t, 1024, 8)

    def body(q_ref, k_ref, v_ref, qg_ref, kg_ref, qo_ref, ko_ref, vo_ref):
        qo_ref[...] = (_head_norm(q_ref[...], None)[0] * qg_ref[...]).astype(BF16)
        ko_ref[...] = (_head_norm(k_ref[...], None)[0] * kg_ref[...]).astype(BF16)
        vo_ref[...] = v_ref[...].astype(BF16)

    def blk(off):
        return pl.BlockSpec((tm, HEAD_DIM), lambda i, h: (i, off + h))

    vec = pl.BlockSpec((1, HEAD_DIM), lambda i, h: (0, 0))
    return _pcall(body, name=name, out_shape=[jax.ShapeDtypeStruct((t, D_ATTN), BF16)] * 3, grid=(t // tm, N_HEADS),
                  in_specs=[blk(0), blk(N_HEADS), blk(2 * N_HEADS), vec, vec], out_specs=[blk(0)] * 3,
                  sem=("parallel", "parallel"))(proj, proj, proj, qg, kg)


def _qk_norm_bwd(dqn, dkn, proj, qg, kg, name):
    t = proj.shape[0]
    tm = _pick(t, 1024, 8)

    def one(d_ref, v_ref, g_ref, o_ref, dg_ref):
        xhat, r = _head_norm(v_ref[...], None)
        d = d_ref[...]
        dg_ref[...] += jnp.sum(d * xhat, axis=0, keepdims=True)
        dxh = d * g_ref[...]
        o_ref[...] = (r * (dxh - xhat * jnp.mean(dxh * xhat, axis=-1, keepdims=True))).astype(BF16)

    def body(dq_ref, dk_ref, q_ref, k_ref, qg_ref, kg_ref, qo_ref, ko_ref, dqg_ref, dkg_ref):
        @pl.when((pl.program_id(0) == 0) & (pl.program_id(1) == 0))
        def _():
            dqg_ref[...] = jnp.zeros_like(dqg_ref)
            dkg_ref[...] = jnp.zeros_like(dkg_ref)

        one(dq_ref, q_ref, qg_ref, qo_ref, dqg_ref)
        one(dk_ref, k_ref, kg_ref, ko_ref, dkg_ref)

    def blk(off):
        return pl.BlockSpec((tm, HEAD_DIM), lambda i, h: (i, off + h))

    vec = pl.BlockSpec((1, HEAD_DIM), lambda i, h: (0, 0))
    vshape = jax.ShapeDtypeStruct((1, HEAD_DIM), F32)
    return _pcall(body, name=name, out_shape=[jax.ShapeDtypeStruct((t, D_ATTN), BF16)] * 2 + [vshape, vshape],
                  grid=(t // tm, N_HEADS),
                  in_specs=[blk(0), blk(0), blk(0), blk(N_HEADS), vec, vec], out_specs=[blk(0), blk(0), vec, vec],
                  sem=("arbitrary", "arbitrary"))(dqn, dkn, proj, proj, qg, kg)


ATTN_SCALE = HEAD_DIM ** -0.5


def _logits(q, k, fq, fk, diag, tq, tk):
    s = _dot(q, k, NT) * ATTN_SCALE + (fq - fk)
    if diag:
        r = lax.broadcasted_iota(jnp.int32, (tq, tk), 0)
        cidx = lax.broadcasted_iota(jnp.int32, (tq, tk), 1)
        s = jnp.where(r >= cidx, s, -jnp.inf)
    return s


def _attn_fwd(q, k, v, fq, fk, name, sides=()):
    t = q.shape[0]
    tq = tk = _pick(t, ATTN_BLOCK, LANE)
    nk = t // tk

    def body(q_ref, k_ref, v_ref, fq_ref, fk_ref, o_ref, lse_ref, m_s, l_s, acc):
        i, j = pl.program_id(1), pl.program_id(2)

        @pl.when(j == 0)
        def _():
            m_s[...] = jnp.full_like(m_s, -jnp.inf)
            l_s[...] = jnp.zeros_like(l_s)
            acc[...] = jnp.zeros_like(acc)

        def step(diag):
            s = _logits(q_ref[...], k_ref[...], fq_ref[...], fk_ref[...], diag, tq, tk)
            m_new = jnp.maximum(m_s[...], jnp.max(s, axis=-1, keepdims=True))
            alpha = jnp.exp(m_s[...] - m_new)
            p = jnp.exp(s - m_new)
            l_s[...] = alpha * l_s[...] + jnp.sum(p, axis=-1, keepdims=True)
            acc[...] = alpha * acc[...] + _dot(p.astype(BF16), v_ref[...])
            m_s[...] = m_new

        @pl.when(j < i)
        def _():
            step(False)

        @pl.when(j == i)
        def _():
            step(True)

        @pl.when(j == nk - 1)
        def _():
            o_ref[...] = (acc[...] / l_s[...]).astype(BF16)
            lse_ref[...] = m_s[...] + jnp.log(l_s[...])

    qb = pl.BlockSpec((tq, HEAD_DIM), lambda h, i, j: (i, h))
    kb = pl.BlockSpec((tk, HEAD_DIM), lambda h, i, j: (jnp.minimum(j, i), h))
    col = pl.BlockSpec((None, tq, 1), lambda h, i, j: (h, i, 0))
    return _pcall(body, name=name,
                  out_shape=[jax.ShapeDtypeStruct((t, D_ATTN), BF16), jax.ShapeDtypeStruct((N_HEADS, t, 1), F32)],
                  grid=(N_HEADS, t // tq, nk),
                  in_specs=[qb, kb, kb, col, pl.BlockSpec((None, 1, tk), lambda h, i, j: (h, 0, jnp.minimum(j, i)))],
                  out_specs=[qb, col],
                  scratch=[pltpu.VMEM((tq, 1), F32), pltpu.VMEM((tq, 1), F32), pltpu.VMEM((tq, HEAD_DIM), F32)],
                  sem=("parallel", "parallel", "arbitrary"), sides=sides)(q, k, v, fq, fk)


def _attn_bwd_q(q, k, v, o, do, lse, fq, fk, name, sides=()):
    t = q.shape[0]
    tq = tk = _pick(t, ATTN_BLOCK, LANE)
    nk = t // tk

    def body(q_ref, k_ref, v_ref, o_ref, do_ref, lse_ref, fq_ref, fk_ref, dq_ref, dl_ref, dfq_ref, acc, dl_s, df_s):
        i, j = pl.program_id(1), pl.program_id(2)

        @pl.when(j == 0)
        def _():
            acc[...] = jnp.zeros_like(acc)
            df_s[...] = jnp.zeros_like(df_s)
            dl_s[...] = jnp.sum(do_ref[...].astype(F32) * o_ref[...].astype(F32), axis=-1, keepdims=True)

        def step(diag):
            s = _logits(q_ref[...], k_ref[...], fq_ref[...], fk_ref[...], diag, tq, tk)
            p = jnp.exp(s - lse_ref[...])
            dp = _dot(do_ref[...], v_ref[...], NT)
            ds = p * (dp - dl_s[...])
            df_s[...] += jnp.sum(ds, axis=-1, keepdims=True)
            acc[...] += _dot(ds.astype(BF16), k_ref[...])

        @pl.when(j < i)
        def _():
            step(False)

        @pl.when(j == i)
        def _():
            step(True)

        @pl.when(j == nk - 1)
        def _():
            dq_ref[...] = acc[...] * ATTN_SCALE
            dl_ref[...] = dl_s[...]
            dfq_ref[...] = df_s[...]

    qb = pl.BlockSpec((tq, HEAD_DIM), lambda h, i, j: (i, h))
    kb = pl.BlockSpec((tk, HEAD_DIM), lambda h, i, j: (jnp.minimum(j, i), h))
    col = pl.BlockSpec((None, tq, 1), lambda h, i, j: (h, i, 0))
    cshape = jax.ShapeDtypeStruct((N_HEADS, t, 1), F32)
    return _pcall(body, name=name, out_shape=[jax.ShapeDtypeStruct((t, D_ATTN), F32), cshape, cshape],
                  grid=(N_HEADS, t // tq, nk),
                  in_specs=[qb, kb, kb, qb, qb, col, col, pl.BlockSpec((None, 1, tk), lambda h, i, j: (h, 0, jnp.minimum(j, i)))],
                  out_specs=[qb, col, col],
                  scratch=[pltpu.VMEM((tq, HEAD_DIM), F32), pltpu.VMEM((tq, 1), F32), pltpu.VMEM((tq, 1), F32)],
                  sem=("parallel", "parallel", "arbitrary"), sides=sides)(q, k, v, o, do, lse, fq, fk)


def _attn_bwd_kv(q, k, v, do, lse, delta, fq, fk, name, sides=()):
    t = q.shape[0]
    tq = tk = _pick(t, ATTN_BLOCK, LANE)
    nq = t // tq

    def body(q_ref, k_ref, v_ref, do_ref, lse_ref, dl_ref, fq_ref, fk_ref, dk_ref, dv_ref, dfk_ref, dk_s, dv_s, df_s):
        j, i = pl.program_id(1), pl.program_id(2)

        @pl.when(i == 0)
        def _():
            dk_s[...] = jnp.zeros_like(dk_s)
            dv_s[...] = jnp.zeros_like(dv_s)
            df_s[...] = jnp.zeros_like(df_s)

        def step(diag):
            s = _logits(q_ref[...], k_ref[...], fq_ref[...], fk_ref[...], diag, tq, tk)
            p = jnp.exp(s - lse_ref[...])
            dv_s[...] += _dot(p.astype(BF16), do_ref[...], TN)
            dp = _dot(do_ref[...], v_ref[...], NT)
            ds = p * (dp - dl_ref[...])
            df_s[...] -= jnp.sum(ds, axis=0, keepdims=True)
            dk_s[...] += _dot(ds.astype(BF16), q_ref[...], TN)

        @pl.when(i > j)
        def _():
            step(False)

        @pl.when(i == j)
        def _():
            step(True)

        @pl.when(i == nq - 1)
        def _():
            dk_ref[...] = dk_s[...] * ATTN_SCALE
            dv_ref[...] = dv_s[...].astype(BF16)
            dfk_ref[...] = df_s[...]

    qb = pl.BlockSpec((tq, HEAD_DIM), lambda h, j, i: (jnp.maximum(i, j), h))
    kb = pl.BlockSpec((tk, HEAD_DIM), lambda h, j, i: (j, h))
    col = pl.BlockSpec((None, tq, 1), lambda h, j, i: (h, jnp.maximum(i, j), 0))
    row = pl.BlockSpec((None, 1, tk), lambda h, j, i: (h, 0, j))
    return _pcall(body, name=name,
                  out_shape=[jax.ShapeDtypeStruct((t, D_ATTN), F32), jax.ShapeDtypeStruct((t, D_ATTN), BF16),
                             jax.ShapeDtypeStruct((N_HEADS, 1, t), F32)],
                  grid=(N_HEADS, t // tk, nq),
                  in_specs=[qb, kb, kb, qb, col, col, col, row], out_specs=[kb, kb, row],
                  scratch=[pltpu.VMEM((tk, HEAD_DIM), F32), pltpu.VMEM((tk, HEAD_DIM), F32), pltpu.VMEM((1, tk), F32)],
                  sem=("parallel", "parallel", "arbitrary"), sides=sides)(q, k, v, do, lse, delta, fq, fk)


def _window_sum(v, w, back):
    n = v.shape[0]
    k = 1
    while k < w:
        v = v + pltpu.roll(v, k if back else n - k, axis=0)
        k *= 2
    return v


def _pool_fwd(proj, pw, ps, name):
    t = proj.shape[0]
    tm = _pick(t, POOL_BLOCK, HALO)
    col = 3 * D_ATTN // D_POOL

    def body(u_ref, prev_ref, pw_ref, ps_ref, pooled_ref, out_ref):
        i = pl.program_id(0)
        prev = jnp.where(i > 0, prev_ref[...], 0.0)
        ext = jnp.concatenate([prev, u_ref[...]], axis=0)
        pos = i * tm + lax.broadcasted_iota(jnp.int32, (tm, 1), 0)
        for g, w in enumerate(POOL_WINDOWS):
            cols = slice(g * GROUP_DIM, (g + 1) * GROUP_DIM)
            xg = ext[:, cols]
            sw = _window_sum(xg, w, True)[HALO:, :]
            cnt = jnp.minimum(pos + 1, w).astype(F32)
            pooled = (sw / cnt - xg[HALO:, :]).astype(BF16)
            pooled_ref[:, cols] = pooled
            out_ref[:, cols] = (_dot(pooled, pw_ref[g]) * ps_ref[:, cols]).astype(BF16)

    row = pl.BlockSpec((tm, D_POOL), lambda i: (i, 0))
    return _pcall(body, name=name, out_shape=[jax.ShapeDtypeStruct((t, D_POOL), BF16)] * 2, grid=(t // tm,),
                  in_specs=[pl.BlockSpec((tm, D_POOL), lambda i: (i, col)),
                            pl.BlockSpec((HALO, D_POOL), lambda i: (jnp.maximum(i * (tm // HALO) - 1, 0), col)),
                            pl.BlockSpec((len(POOL_WINDOWS), GROUP_DIM, GROUP_DIM), lambda i: (0, 0, 0)),
                            pl.BlockSpec((1, D_POOL), lambda i: (0, 0))],
                  out_specs=[row, row], sem=("parallel",))(proj, proj, pw, ps)


def _pool_bwd(dout, pooled, pw, ps, name):
    t = pooled.shape[0]
    tm = _pick(t, POOL_BLOCK, HALO)
    nb = t // tm
    ng = len(POOL_WINDOWS)

    def body(d_ref, nxt_ref, p_ref, pw_ref, ps_ref, du_ref, dpw_ref, dps_ref):
        i = pl.program_id(0)

        @pl.when(i == 0)
        def _():
            dpw_ref[...] = jnp.zeros_like(dpw_ref)
            dps_ref[...] = jnp.zeros_like(dps_ref)

        nxt = jnp.where(i < nb - 1, nxt_ref[...].astype(F32), 0.0)
        ext = jnp.concatenate([d_ref[...].astype(F32), nxt], axis=0)
        pos = i * tm + lax.broadcasted_iota(jnp.int32, (tm + HALO, 1), 0)
        for g, w in enumerate(POOL_WINDOWS):
            cols = slice(g * GROUP_DIM, (g + 1) * GROUP_DIM)
            pooled_g = p_ref[:, cols]
            dg = ext[:, cols]
            pm = _dot(pooled_g, pw_ref[g])
            dps_ref[:, cols] += jnp.sum(dg[:tm, :] * pm, axis=0, keepdims=True)
            dpm = (dg * ps_ref[:, cols]).astype(BF16)
            dpw_ref[g] += _dot(pooled_g, dpm[:tm, :], TN)
            dpooled = _dot(dpm, pw_ref[g], NT)
            cnt = jnp.minimum(pos + 1, w).astype(F32)
            fwd = _window_sum(dpooled / cnt, w, False)
            du_ref[:, cols] = (fwd[:tm, :] - dpooled[:tm, :]).astype(BF16)

    row = pl.BlockSpec((tm, D_POOL), lambda i: (i, 0))
    return _pcall(body, name=name,
                  out_shape=[jax.ShapeDtypeStruct((t, D_POOL), BF16), jax.ShapeDtypeStruct((ng, GROUP_DIM, GROUP_DIM), F32),
                             jax.ShapeDtypeStruct((1, D_POOL), F32)],
                  grid=(nb,),
                  in_specs=[pl.BlockSpec((tm, D_POOL), lambda i: (i, 1)),
                            pl.BlockSpec((HALO, D_POOL), lambda i: (jnp.minimum((i + 1) * (tm // HALO), t // HALO - 1), 1)),
                            row, pl.BlockSpec((ng, GROUP_DIM, GROUP_DIM), lambda i: (0, 0, 0)),
                            pl.BlockSpec((1, D_POOL), lambda i: (0, 0))],
                  out_specs=[row, pl.BlockSpec((ng, GROUP_DIM, GROUP_DIM), lambda i: (0, 0, 0)),
                             pl.BlockSpec((1, D_POOL), lambda i: (0, 0))],
                  sem=("arbitrary",))(dout, dout, pooled, pw, ps)


def _pad_cols(w, n):
    return jnp.pad(w, ((0, 0), (0, n - w.shape[1])))


def kernel(x, c, w_ada, b_ada, ffn1_norm_g, ffn1_w_in, ffn1_w_out, mix_norm_g, w_in, b_forget, q_norm_g, k_norm_g, pool_w, pool_scale, w_out, ffn2_norm_g, ffn2_w_in, ffn2_w_out, final_norm_g, loss_target, m_w_ada, m_b_ada, m_ffn1_norm_g, m_ffn1_w_in, m_ffn1_w_out, m_mix_norm_g, m_w_in, m_b_forget, m_q_norm_g, m_k_norm_g, m_pool_w, m_pool_scale, m_w_out, m_ffn2_norm_g, m_ffn2_w_in, m_ffn2_w_out, m_final_norm_g, v_w_ada, v_b_ada, v_ffn1_norm_g, v_ffn1_w_in, v_ffn1_w_out, v_mix_norm_g, v_w_in, v_b_forget, v_q_norm_g, v_k_norm_g, v_pool_w, v_pool_scale, v_w_out, v_ffn2_norm_g, v_ffn2_w_in, v_ffn2_w_out, v_final_norm_g):
    ax, ay, ac = _coords()
    chip = 2 * ax + ay
    me = 2 * chip + ac
    chip_arr = jnp.reshape(chip, (1,)).astype(jnp.int32)
    core_arr = jnp.reshape(ac, (1,)).astype(jnp.int32)

    t = x.shape[1]
    xs = x.reshape(t, D)
    tgt = loss_target.reshape(t, D)
    hu = ffn1_w_out.shape[1]
    hup = -(-hu // LANE) * LANE
    ws_in = w_in.shape[2]
    ws_in_pad = -(-ws_in // LANE) * LANE
    n_ada = w_ada.shape[2]

    def ffn_in_shard(w):
        w = w[0].astype(BF16)
        return jnp.concatenate([_pad_cols(w[:, :hu], hup), _pad_cols(w[:, hu:], hup)], axis=1)

    def ffn_out_shard(w):
        return jnp.pad(w[0].astype(BF16), ((0, hup - hu), (0, 0)))

    shards1 = [ffn_in_shard(ffn1_w_in), ffn_out_shard(ffn1_w_out)]
    shards_mix = [_pad_cols(w_in[0].astype(BF16), ws_in_pad), pool_w[0].astype(BF16).reshape(GROUP_DIM, GROUP_DIM),
                  w_out[0].astype(BF16)]
    shards2 = [ffn_in_shard(ffn2_w_in), ffn_out_shard(ffn2_w_out)]

    def own_slot(gathered, shards):
        return [lax.dynamic_update_slice(g, w[None], (chip, 0, 0)) for g, w in zip(gathered, shards)]

    def add_siblings(parts, recv, tag):
        return [_add_sibling(p, r, core_arr, f"{tag}_add_sibling_{k}") for k, (p, r) in enumerate(zip(parts, recv))]

    def add_chips(halves, recv, tag):
        return [_add_chips(hh, r, chip_arr, f"{tag}_add_chips_{k}") for k, (hh, r) in enumerate(zip(halves, recv))]

    def both_halves(mine, theirs):
        return [jnp.where(ac == 0, jnp.concatenate([g, r], axis=0), jnp.concatenate([r, g], axis=0)) for g, r in zip(mine, theirs)]

    g1 = _exchange(_gather_d2d(_exchange(_gather_ici(shards1), "gather_ffn1_ici")), "gather_ffn1_d2d")
    g_in1, g_out1 = own_slot(g1, shards1)
    wg_out1 = g_out1.reshape(4 * hup, D)

    c_all = _allgather_small(c.reshape(8, D // 8), True, "gather_c").reshape(8, D)
    c16 = jnp.pad(c_all, ((0, 8), (0, 0)))
    b_ada_mine = lax.dynamic_slice(b_ada, (0, chip * n_ada), (1, n_ada))
    act16, mod16 = _ada_fwd(c16, w_ada[0], b_ada_mine, "ada_fwd")
    mod_all = _allgather_small(mod16[:8], False, "gather_mod")
    mod = lax.dynamic_index_in_dim(mod_all, me, axis=1, keepdims=False).reshape(N_MOD, 1, D)
    sh1, sc1, gt1, sh2, sc2, gt2, sh3, sc3, gt3 = [mod[k] for k in range(N_MOD)]

    gate1, gate3 = 0.5 * gt1, 0.5 * gt3
    h1 = _norm_mod(xs, ffn1_norm_g, sh1, sc1, "ffn1_norm")
    ici = _gather_ici(shards_mix)
    a1, b1, u1 = _ffn_up(h1, g_in1, "ffn1_up", sides=[ici])
    d2d = _gather_d2d(ici.results)
    y1, x1 = _mm_resid(u1, wg_out1, xs, gate1, "ffn1_down", sides=[d2d])
    g_win, g_pw, g_wout = own_slot(d2d.results, shards_mix)
    w_full = jnp.concatenate([g_win[k, :, :ws_in] for k in range(4)], axis=1)
    nf = 3 * D_ATTN
    w_all = jnp.concatenate([w_full[:, :nf], w_full[:, nf + N_HEADS:], w_full[:, nf:nf + N_HEADS],
                             jnp.zeros((D, LANE - N_HEADS), BF16)], axis=1)
    pw_full = g_pw.reshape(4, 4, GROUP_DIM // 4, GROUP_DIM).transpose(1, 0, 2, 3).reshape(4, GROUP_DIM, GROUP_DIM)
    wo_full = g_wout.reshape(4 * g_wout.shape[1], D)

    h2 = _norm_mod(x1, mix_norm_g, sh2, sc2, "mix_norm")
    proj = _mm(h2, w_all, F32, "mix_proj")
    b_pad = jnp.pad(b_forget, ((0, 0), (0, LANE - N_HEADS)))
    cum = _forget_fwd(proj, b_pad, "forget_fwd")
    cum_t = cum[:, :N_HEADS].T
    fq, fk = cum_t.reshape(N_HEADS, t, 1), cum_t.reshape(N_HEADS, 1, t)
    qn, kn, vb = _qkv_prep(proj, q_norm_g, k_norm_g, "qkv_prep")
    ici = _gather_ici(shards2)
    attn, lse = _attn_fwd(qn, kn, vb, fq, fk, "attn_fwd", sides=[ici])
    pooled, pool_out = _pool_fwd(proj, pw_full, pool_scale, "pool_fwd")
    cat = jnp.concatenate([attn, pool_out], axis=1)
    d2d = _gather_d2d(ici.results)
    y_mix, x2 = _mm_resid(cat, wo_full, x1, gt2, "mix_out", sides=[d2d])
    g_in2, g_out2 = own_slot(d2d.results, shards2)
    wg_out2 = g_out2.reshape(4 * hup, D)

    h3 = _norm_mod(x2, ffn2_norm_g, sh3, sc3, "ffn2_norm")
    a3, b3, u3 = _ffn_up(h3, g_in2, "ffn2_up")
    y3, x3 = _mm_resid(u3, wg_out2, x2, gate3, "ffn2_down")

    loss_part, dx3, d_final_g = _final_loss(x3, final_norm_g.reshape(1, D), tgt, "final_loss")
    loss = lax.psum(loss_part[0, 0], ("x", "y", "c"))

    dy3, dgt3 = _gate_bwd(dx3, y3, gate3, "ffn2_gate_bwd")
    dab3 = _ffn_dab(dy3, wg_out2, a3, b3, "ffn2_dab")
    parts2 = [_ffn_dwin(h3, dab3, "ffn2_dwin"), _mm_tn(u3, dy3, BF16, hup, "ffn2_dwout").reshape(4, hup, D)]
    rs = _reduce_siblings(parts2)
    dh3 = _ffn_dh(dab3, g_in2, "ffn2_dh", sides=[rs])
    halves2 = add_siblings(parts2, rs.results, "ffn2")
    dx2, dsh3, dsc3, d_ng3 = _norm_mod_bwd(dh3, x2, dx3, ffn2_norm_g, sc3, "ffn2_norm_bwd")

    dz, dgt2 = _gate_bwd(dx2, y_mix, gt2, "mix_gate_bwd")
    dcat = _mm_nt(dz, wo_full, BF16, "mix_dcat")
    d_wo = _mm_tn(cat, dz, BF16, 1024, "mix_dwout")
    du_pool, d_pw, d_ps = _pool_bwd(dcat, pooled, pw_full, pool_scale, "pool_bwd")
    rs = _reduce_chips(halves2)
    dqn, delta, dfq = _attn_bwd_q(qn, kn, vb, attn, dcat, lse, fq, fk, "attn_bwd_q", sides=[rs])
    mine2 = add_chips(halves2, rs.results, "ffn2")
    rs = _share_siblings(mine2)
    dkn, dv, dfk = _attn_bwd_kv(qn, kn, vb, dcat, lse, delta, fq, fk, "attn_bwd_kv", sides=[rs])
    r_in2, r_out2 = both_halves(mine2, rs.results)
    d_cum = jnp.pad((dfq.reshape(N_HEADS, t) + dfk.reshape(N_HEADS, t)).T, ((0, 0), (0, LANE - N_HEADS)))
    dfl, d_bf = _forget_bwd(d_cum, proj, b_pad, "forget_bwd")
    dq, dk, d_qg, d_kg = _qk_norm_bwd(dqn, dkn, proj, q_norm_g, k_norm_g, "qk_norm_bwd")
    dproj = jnp.concatenate([dq, dk, dv, du_pool, dfl], axis=1)
    d_wall = _mm_tn(h2, dproj, F32, 1024, "mix_dwin")
    d_wfull = jnp.concatenate([d_wall[:, :nf], d_wall[:, D_PROJ:D_PROJ + N_HEADS], d_wall[:, nf:D_PROJ]], axis=1)
    p_win = jnp.stack([_pad_cols(d_wfull[:, k * ws_in:(k + 1) * ws_in], ws_in_pad) for k in range(4)]).astype(BF16)
    p_pw = d_pw.reshape(4, 4, GROUP_DIM // 4, GROUP_DIM).transpose(1, 0, 2, 3).reshape(4, GROUP_DIM, GROUP_DIM).astype(BF16)
    parts_mix = [p_win, p_pw, d_wo.reshape(4, D // 4, D)]
    rs = _reduce_siblings(parts_mix)
    dh2 = _mm_nt(dproj, w_all, F32, "mix_dh", sides=[rs])
    halves_mix = add_siblings(parts_mix, rs.results, "mix")
    dx1, dsh2, dsc2, d_ng2 = _norm_mod_bwd(dh2, x1, dx2, mix_norm_g, sc2, "mix_norm_bwd")

    dy1, dgt1 = _gate_bwd(dx1, y1, gate1, "ffn1_gate_bwd")
    rs = _reduce_chips(halves_mix)
    dab1 = _ffn_dab(dy1, wg_out1, a1, b1, "ffn1_dab", sides=[rs])
    mine_mix = add_chips(halves_mix, rs.results, "mix")
    rs = _share_siblings(mine_mix)
    d_out1 = _mm_tn(u1, dy1, BF16, hup, "ffn1_dwout", sides=[rs]).reshape(4, hup, D)
    r_win, r_pw, r_wo = both_halves(mine_mix, rs.results)
    parts1 = [_ffn_dwin(h1, dab1, "ffn1_dwin"), d_out1]
    rs = _reduce_siblings(parts1)
    dh1 = _ffn_dh(dab1, g_in1, "ffn1_dh", sides=[rs])
    halves1 = add_siblings(parts1, rs.results, "ffn1")
    dx0, dsh1, dsc1, d_ng1 = _norm_mod_bwd(dh1, xs, dx1, ffn1_norm_g, sc1, "ffn1_norm_bwd")
    grad_x = dx0.reshape(1, t, D)
    dgt1, dgt3 = 0.5 * dgt1, 0.5 * dgt3
    mine1 = add_chips(halves1, _exchange(_reduce_chips(halves1), "ffn1_reduce_chips"), "ffn1")
    r_in1, r_out1 = both_halves(mine1, _exchange(_share_siblings(mine1), "ffn1_share_siblings"))
    grads = {
        "ffn1_w_in": jnp.concatenate([r_in1[:, :hu], r_in1[:, hup:hup + hu]], axis=1),
        "ffn1_w_out": r_out1[:hu],
        "w_in": r_win[:, :ws_in],
        "pool_w": r_pw,
        "w_out": r_wo,
        "ffn2_w_in": jnp.concatenate([r_in2[:, :hu], r_in2[:, hup:hup + hu]], axis=1),
        "ffn2_w_out": r_out2[:hu],
    }

    dmod = jnp.concatenate([dsh1, dsc1, dgt1, dsh2, dsc2, dgt2, dsh3, dsc3, dgt3], axis=1)
    small_names = ["b_ada", "ffn1_norm_g", "mix_norm_g", "b_forget", "q_norm_g", "k_norm_g", "pool_scale", "ffn2_norm_g",
                   "final_norm_g"]
    small_grads = [dmod, d_ng1, d_ng2, d_bf[:, :N_HEADS], d_qg, d_kg, d_ps, d_ng3, d_final_g]
    small_w = [b_ada, ffn1_norm_g, mix_norm_g, b_forget, q_norm_g, k_norm_g, pool_scale, ffn2_norm_g, final_norm_g.reshape(1, D)]
    small_m = [m_b_ada, m_ffn1_norm_g, m_mix_norm_g, m_b_forget, m_q_norm_g, m_k_norm_g, m_pool_scale, m_ffn2_norm_g,
               m_final_norm_g.reshape(1, D)]
    small_v = [v_b_ada, v_ffn1_norm_g, v_mix_norm_g, v_b_forget, v_q_norm_g, v_k_norm_g, v_pool_scale, v_ffn2_norm_g,
               v_final_norm_g.reshape(1, D)]
    sizes = [g.shape[1] for g in small_grads]
    n_small = sum(sizes)
    n_pack = -(-n_small // (8 * LANE)) * (8 * LANE)

    def pack(vs, fill):
        flat = jnp.concatenate([v.reshape(1, -1) for v in vs], axis=1)
        return jnp.pad(flat, ((0, 0), (0, n_pack - n_small)), constant_values=fill).reshape(8, n_pack // 8)

    g8 = _allgather_small(pack(small_grads, 0.0), True, "gather_small_grads")
    gs, ds, ms, vs = _adamw_small(pack(small_w, 0.0), g8, pack(small_m, 0.0), pack(small_v, 1.0), "adamw_small")

    def unpack(p):
        flat = p.reshape(1, n_pack)
        out, off = {}, 0
        for nme, sz in zip(small_names, sizes):
            out[nme] = flat[:, off:off + sz]
            off += sz
        return out

    small = [unpack(p) for p in (gs, ds, ms, vs)]
    for dct in small:
        dct["final_norm_g"] = dct["final_norm_g"].reshape(D)

    dmod_all = g8.reshape(8, n_pack)[:, :N_MOD * D]
    dmod_mine = lax.dynamic_slice(dmod_all, (0, chip * n_ada), (8, n_ada))
    grads["w_ada"] = _ada_bwd(act16[:8].reshape(8, D, 1), dmod_mine.reshape(8, 1, n_ada), "ada_bwd")

    big = {"w_ada": (w_ada, m_w_ada, v_w_ada), "ffn1_w_in": (ffn1_w_in, m_ffn1_w_in, v_ffn1_w_in),
           "ffn1_w_out": (ffn1_w_out, m_ffn1_w_out, v_ffn1_w_out), "w_in": (w_in, m_w_in, v_w_in),
           "pool_w": (pool_w, m_pool_w, v_pool_w), "w_out": (w_out, m_w_out, v_w_out),
           "ffn2_w_in": (ffn2_w_in, m_ffn2_w_in, v_ffn2_w_in), "ffn2_w_out": (ffn2_w_out, m_ffn2_w_out, v_ffn2_w_out)}
    res = {}
    for nme, (w, m, v) in big.items():
        shp = w.shape
        g2 = grads[nme]
        two = g2.shape
        d, mo, vo = _adamw(w.reshape(two), g2, m.reshape(two), v.reshape(two), f"adamw_{nme}")
        res[nme] = (g2.reshape(shp), d.reshape(shp), mo.reshape(shp), vo.reshape(shp))
    for nme in small_names:
        res[nme] = tuple(dct[nme] for dct in small)

    order = ["w_ada", "b_ada", "ffn1_norm_g", "ffn1_w_in", "ffn1_w_out", "mix_norm_g", "w_in", "b_forget", "q_norm_g",
             "k_norm_g", "pool_w", "pool_scale", "w_out", "ffn2_norm_g", "ffn2_w_in", "ffn2_w_out", "final_norm_g"]
    return (loss, grad_x, *[res[n][0] for n in order], *[res[n][1] for n in order], *[res[n][2] for n in order],
            *[res[n][3] for n in order])
```

```python
import functools

import jax
import jax.numpy as jnp
from jax import lax
from jax.experimental import pallas as pl
from jax.experimental.pallas import tpu as pltpu

F32 = jnp.float32
BF16 = jnp.bfloat16

D = 2048
N_HEADS = 8
HEAD_DIM = 128
D_ATTN = 1024
D_POOL = 1024
POOL_WINDOWS = (2, 4, 8, 16)
GROUP_DIM = 256
HALO = 16
N_MOD = 9
EPS = 1e-6
D_PROJ = 3 * D_ATTN + D_POOL
D_PROJ_PAD = D_PROJ + 128
LANE = 128
ATTN_BLOCK = 512
POOL_BLOCK = 512

ADAM_LR = 0.001
ADAM_B1 = 0.9
ADAM_B2 = 0.999
ADAM_EPS = 1e-08
ADAM_WD = 0.01
ADAM_STEP = 10

VMEM_LIMIT_V7X = 56 * 1024 * 1024
MESH_ID = pl.DeviceIdType.MESH
ANY = pl.BlockSpec(memory_space=pl.ANY)
VMEM = pl.BlockSpec(memory_space=pltpu.VMEM)

NT = (((1,), (1,)), ((), ()))
TN = (((0,), (0,)), ((), ()))


class _Side:
    def __init__(self, ins, outs, nsem, start, wait, alias=False):
        self.ins, self.outs, self.nsem, self.start, self.wait, self.alias = list(ins), list(outs), nsem, start, wait, alias
        self.results = None


def _pcall(body, *, name, out_shape, grid=None, in_specs=None, out_specs=None, scratch=(), sem=None, prefetch=0, sides=()):
    sides = list(sides)
    single = not isinstance(out_shape, (list, tuple))
    shapes = [out_shape] if single else list(out_shape)
    in_specs = list(in_specs)
    ospecs = [out_specs] if single else list(out_specs)
    scratch = list(scratch)
    n_in, n_out, n_scr = len(in_specs), len(shapes), len(scratch)
    assert not (sides and prefetch)
    aliases = {}
    for sd in sides:
        if sd.alias:
            for k in range(len(sd.ins)):
                aliases[len(in_specs) + k] = len(shapes) + k
        in_specs += [ANY] * len(sd.ins)
        shapes += sd.outs
        ospecs += [ANY] * len(sd.outs)
        scratch += [pltpu.SemaphoreType.DMA((sd.nsem,)), pltpu.SemaphoreType.DMA((sd.nsem,))]

    def wrapped(*refs):
        ins, outs, scr = refs[:len(in_specs)], refs[len(in_specs):len(in_specs) + len(shapes)], refs[len(in_specs) + len(shapes):]
        first = last = None
        for ax, g in enumerate(grid or ()):
            p = pl.program_id(ax)
            first = (p == 0) if first is None else first & (p == 0)
            last = (p == g - 1) if last is None else last & (p == g - 1)

        def each(what):
            i0, o0 = n_in, n_out
            for k, sd in enumerate(sides):
                getattr(sd, what)(ins[i0:i0 + len(sd.ins)], outs[o0:o0 + len(sd.outs)], scr[n_scr + 2 * k], scr[n_scr + 2 * k + 1])
                i0, o0 = i0 + len(sd.ins), o0 + len(sd.outs)

        if first is None:
            each("start")
        else:
            pl.when(first)(lambda: each("start"))
        body(*ins[:n_in], *outs[:n_out], *scr[:n_scr])
        if last is None:
            each("wait")
        else:
            pl.when(last)(lambda: each("wait"))

    params = dict(vmem_limit_bytes=VMEM_LIMIT_V7X)
    if sides and grid:
        params["dimension_semantics"] = ("arbitrary",) * len(grid)
    elif sem is not None:
        params["dimension_semantics"] = sem
    kw = dict(name=name, out_shape=shapes if (sides or not single) else shapes[0], compiler_params=pltpu.CompilerParams(**params))
    if aliases:
        kw["input_output_aliases"] = aliases
    final_ospecs = ospecs if (sides or not single) else ospecs[0]
    if prefetch:
        kw["grid_spec"] = pltpu.PrefetchScalarGridSpec(
            num_scalar_prefetch=prefetch, grid=grid, in_specs=in_specs, out_specs=final_ospecs, scratch_shapes=scratch)
    else:
        if grid is not None:
            kw["grid"] = grid
        kw["in_specs"] = in_specs
        kw["out_specs"] = final_ospecs
        kw["scratch_shapes"] = scratch
    call = pl.pallas_call(wrapped if sides else body, **kw)
    if not sides:
        return call

    def run(*operands):
        res = list(call(*operands, *[a for sd in sides for a in sd.ins]))
        o0 = n_out
        for sd in sides:
            sd.results = res[o0:o0 + len(sd.outs)]
            o0 += len(sd.outs)
        return res[0] if single else res[:n_out]

    return run


def _pick(n, cap, mult):
    best = None
    for d in range(mult, min(n, cap) + 1, mult):
        if n % d == 0:
            best = d
    assert best is not None, (n, cap, mult)
    return best


def _coords():
    return lax.axis_index("x"), lax.axis_index("y"), lax.axis_index("c")


def _dot(a, b, dims=None):
    if dims is None:
        return jnp.dot(a, b, preferred_element_type=F32)
    return lax.dot_general(a, b, dims, preferred_element_type=F32)


def _remote(src, dst, ssem, rsem, dev):
    return pltpu.make_async_remote_copy(src_ref=src, dst_ref=dst, send_sem=ssem, recv_sem=rsem,
                                        device_id=dev, device_id_type=MESH_ID)


def _allgather_small(v, whole_mesh, name):
    masks = list(range(1, 8)) if whole_mesh else [4, 2, 6]
    nslot = 8 if whole_mesh else 4

    def slot(px, py, pc):
        return 4 * px + 2 * py + pc if whole_mesh else 2 * px + py

    def body(v_ref, out_ref, ssem, rsem, lsem):
        x, y, c = _coords()
        mine = slot(x, y, c)
        peers = [(jnp.bitwise_xor(x, (m >> 2) & 1), jnp.bitwise_xor(y, (m >> 1) & 1), jnp.bitwise_xor(c, m & 1))
                 for m in masks]
        loc = pltpu.make_async_copy(v_ref, out_ref.at[mine], lsem)
        loc.start()
        sends = [_remote(v_ref, out_ref.at[mine], ssem.at[k], rsem.at[k], p) for k, p in enumerate(peers)]
        for cp in sends:
            cp.start()
        for k, p in enumerate(peers):
            _remote(v_ref, out_ref.at[slot(*p)], ssem.at[k], rsem.at[k], p).wait_recv()
        for cp in sends:
            cp.wait_send()
        loc.wait()

    return _pcall(body, name=name, out_shape=jax.ShapeDtypeStruct((nslot,) + v.shape, v.dtype),
                  in_specs=[VMEM], out_specs=VMEM,
                  scratch=[pltpu.SemaphoreType.DMA((len(masks),)), pltpu.SemaphoreType.DMA((len(masks),)),
                           pltpu.SemaphoreType.DMA(())])(v)


def _other_chips(x, y):
    return [(1 - x, y), (x, 1 - y), (1 - x, 1 - y)]


def _half_rows(shard_rows, core):
    h = shard_rows // 2
    return pl.ds(pl.multiple_of(core * h, 16), h)


def _later(src, dst, ssem, rsem, dev):
    return functools.partial(_remote, src, dst, ssem, rsem, dev)


def _side_from(ins, outs, nsem, pairs_of, alias=False):
    def start(*refs):
        for send, _ in pairs_of(*refs):
            send().start()

    def wait(*refs):
        pairs = pairs_of(*refs)
        for _, recv in pairs:
            recv().wait_recv()
        for send, _ in pairs:
            send().wait_send()

    return _Side(ins, outs, nsem, start, wait, alias)


def _gather_ici(ws):
    def pairs_of(w_refs, g_refs, ssem, rsem):
        x, y, c = _coords()
        me, out = 2 * x + y, []
        for a, w in enumerate(ws):
            rows = _half_rows(w.shape[0], c)
            for j, chip in enumerate(_other_chips(x, y)):
                sems = (ssem.at[3 * a + j], rsem.at[3 * a + j], (*chip, c))
                got = g_refs[a].at[2 * chip[0] + chip[1], rows]
                out.append((_later(w_refs[a].at[rows], g_refs[a].at[me, rows], *sems), _later(got, got, *sems)))
        return out

    return _side_from(ws, [jax.ShapeDtypeStruct((4,) + w.shape, w.dtype) for w in ws], 3 * len(ws), pairs_of)


def _gather_d2d(gs):
    def pairs_of(_, g_refs, ssem, rsem):
        x, y, c = _coords()
        out = []
        for a, g in enumerate(gs):
            for j, chip in enumerate(_other_chips(x, y)):
                sems = (ssem.at[3 * a + j], rsem.at[3 * a + j], (x, y, 1 - c))
                mine = g_refs[a].at[2 * chip[0] + chip[1], _half_rows(g.shape[1], c)]
                theirs = g_refs[a].at[2 * chip[0] + chip[1], _half_rows(g.shape[1], 1 - c)]
                out.append((_later(mine, mine, *sems), _later(theirs, theirs, *sems)))
        return out

    return _side_from(gs, [jax.ShapeDtypeStruct(g.shape, g.dtype) for g in gs], 3 * len(gs), pairs_of, alias=True)


def _reduce_siblings(ps):
    def pairs_of(p_refs, r_refs, ssem, rsem):
        x, y, c = _coords()
        out = []
        for a, p in enumerate(ps):
            src = p_refs[a].at[:, _half_rows(p.shape[1], 1 - c), :]
            cp = _later(src, r_refs[a], ssem.at[a], rsem.at[a], (x, y, 1 - c))
            out.append((cp, cp))
        return out

    return _side_from(ps, [jax.ShapeDtypeStruct((4, p.shape[1] // 2, p.shape[2]), p.dtype) for p in ps], len(ps), pairs_of)


def _reduce_chips(hs):
    def pairs_of(h_refs, o_refs, ssem, rsem):
        x, y, c = _coords()
        out = []
        for a in range(len(hs)):
            for j, chip in enumerate(_other_chips(x, y)):
                cp = _later(h_refs[a].at[2 * chip[0] + chip[1]], o_refs[a].at[j], ssem.at[3 * a + j], rsem.at[3 * a + j], (*chip, c))
                out.append((cp, cp))
        return out

    return _side_from(hs, [jax.ShapeDtypeStruct((3,) + h.shape[1:], h.dtype) for h in hs], 3 * len(hs), pairs_of)


def _share_siblings(gs):
    def pairs_of(g_refs, o_refs, ssem, rsem):
        x, y, c = _coords()
        cps = [_later(g_refs[a], o_refs[a], ssem.at[a], rsem.at[a], (x, y, 1 - c)) for a in range(len(gs))]
        return [(cp, cp) for cp in cps]

    return _side_from(gs, [jax.ShapeDtypeStruct(g.shape, g.dtype) for g in gs], len(gs), pairs_of)


def _exchange(side, name):
    def body():
        pass

    _pcall(body, name=name, out_shape=[], in_specs=[], out_specs=[], sides=[side])()
    return side.results


def _add_sibling(p, r, core, name):
    _, rr, cc = p.shape
    h = rr // 2
    th = _pick(h, max(16, (2 << 20) // (2 * cc)), 16)
    nb = h // th

    def body(c_ref, p_ref, r_ref, o_ref):
        o_ref[...] = (p_ref[...].astype(F32) + r_ref[...].astype(F32)).astype(BF16)

    return _pcall(body, name=name, out_shape=jax.ShapeDtypeStruct((4, h, cc), BF16), grid=(4, nb),
                  in_specs=[pl.BlockSpec((None, th, cc), lambda k, i, c_ref: (k, c_ref[0] * nb + i, 0)),
                            pl.BlockSpec((None, th, cc), lambda k, i, c_ref: (k, i, 0))],
                  out_specs=pl.BlockSpec((None, th, cc), lambda k, i, c_ref: (k, i, 0)),
                  sem=("parallel", "parallel"), prefetch=1)(core, p, r)


def _add_chips(hh, r, chip, name):
    _, h, cc = hh.shape
    th = _pick(h, max(16, (2 << 20) // (2 * cc)), 16)

    def body(k_ref, h_ref, r_ref, o_ref):
        s = h_ref[...].astype(F32) + r_ref[0].astype(F32)
        s = s + r_ref[1].astype(F32)
        o_ref[...] = s + r_ref[2].astype(F32)

    return _pcall(body, name=name, out_shape=jax.ShapeDtypeStruct((h, cc), F32), grid=(h // th,),
                  in_specs=[pl.BlockSpec((None, th, cc), lambda i, k_ref: (k_ref[0], i, 0)),
                            pl.BlockSpec((3, th, cc), lambda i, k_ref: (0, i, 0))],
                  out_specs=pl.BlockSpec((th, cc), lambda i, k_ref: (i, 0)),
                  sem=("parallel",), prefetch=1)(chip, hh, r)


def _adamw_math(w, g, m, v):
    m = ADAM_B1 * m + (1.0 - ADAM_B1) * g
    v = ADAM_B2 * v + (1.0 - ADAM_B2) * (g * g)
    m_hat = m / (1.0 - ADAM_B1 ** ADAM_STEP)
    v_hat = v / (1.0 - ADAM_B2 ** ADAM_STEP)
    delta = -ADAM_LR * (m_hat / (jnp.sqrt(v_hat) + ADAM_EPS) + ADAM_WD * w)
    return delta, m, v


def _adamw(w, g, m, v, name):
    rr, cc = w.shape
    tr = _pick(rr, max(8, (3 << 20) // (4 * cc)), 8)

    def body(w_ref, g_ref, m_ref, v_ref, d_ref, mo_ref, vo_ref):
        d, mm, vv = _adamw_math(w_ref[...], g_ref[...], m_ref[...], v_ref[...])
        d_ref[...] = d
        mo_ref[...] = mm
        vo_ref[...] = vv

    spec = pl.BlockSpec((tr, cc), lambda i: (i, 0))
    return _pcall(body, name=name, out_shape=[jax.ShapeDtypeStruct(w.shape, F32)] * 3, grid=(rr // tr,),
                  in_specs=[spec] * 4, out_specs=[spec] * 3, sem=("parallel",))(w, g, m, v)


def _adamw_small(w, g8, m, v, name):
    def body(w_ref, g_ref, m_ref, v_ref, go_ref, d_ref, mo_ref, vo_ref):
        g = g_ref[0]
        for k in range(1, 8):
            g = g + g_ref[k]
        d, mm, vv = _adamw_math(w_ref[...], g, m_ref[...], v_ref[...])
        go_ref[...] = g
        d_ref[...] = d
        mo_ref[...] = mm
        vo_ref[...] = vv

    return _pcall(body, name=name, out_shape=[jax.ShapeDtypeStruct(w.shape, F32)] * 4,
                  in_specs=[VMEM] * 4, out_specs=[VMEM] * 4)(w, g8, m, v)


def _ada_fwd(c16, w_ada, b_ada, name):
    n = w_ada.shape[1]
    tn = _pick(n, 512, LANE)

    def body(c_ref, w_ref, b_ref, act_ref, mod_ref):
        cv = c_ref[...]
        act = cv * jax.nn.sigmoid(cv)
        act_ref[...] = act
        mod_ref[...] = _dot(act.astype(BF16), w_ref[...].astype(BF16)) + b_ref[...]

    return _pcall(body, name=name,
                  out_shape=[jax.ShapeDtypeStruct((16, D), F32), jax.ShapeDtypeStruct((16, n), F32)], grid=(n // tn,),
                  in_specs=[pl.BlockSpec((16, D), lambda j: (0, 0)), pl.BlockSpec((D, tn), lambda j: (0, j)),
                            pl.BlockSpec((1, tn), lambda j: (0, j))],
                  out_specs=[pl.BlockSpec((16, D), lambda j: (0, 0)), pl.BlockSpec((16, tn), lambda j: (0, j))],
                  sem=("arbitrary",))(c16, w_ada, b_ada)


def _ada_bwd(act, dmod, name):
    n = dmod.shape[2]
    tm, tn = 256, _pick(n, 512, LANE)

    def body(a_ref, d_ref, o_ref):
        def term(b):
            return a_ref[b].astype(BF16).astype(F32) * d_ref[b].astype(BF16).astype(F32)

        acc = term(0)
        for b in range(1, 8):
            acc = acc + term(b)
        o_ref[...] = acc

    return _pcall(body, name=name, out_shape=jax.ShapeDtypeStruct((D, n), F32), grid=(D // tm, n // tn),
                  in_specs=[pl.BlockSpec((8, tm, 1), lambda i, j: (0, i, 0)), pl.BlockSpec((8, 1, tn), lambda i, j: (0, 0, j))],
                  out_specs=pl.BlockSpec((tm, tn), lambda i, j: (i, j)), sem=("parallel", "parallel"))(act, dmod)


def _norm_mod(x, g, sh, sc, name):
    t = x.shape[0]
    tm = _pick(t, 512, 8)

    def body(x_ref, g_ref, sh_ref, sc_ref, h_ref):
        xf = x_ref[...]
        r = lax.rsqrt(jnp.mean(xf * xf, axis=-1, keepdims=True) + EPS)
        h = (xf * r) * g_ref[...]
        h_ref[...] = (h * (1.0 + sc_ref[...]) + sh_ref[...]).astype(BF16)

    vec = pl.BlockSpec((1, D), lambda i: (0, 0))
    row = pl.BlockSpec((tm, D), lambda i: (i, 0))
    return _pcall(body, name=name, out_shape=jax.ShapeDtypeStruct((t, D), BF16), grid=(t // tm,),
                  in_specs=[row, vec, vec, vec], out_specs=row, sem=("parallel",))(x, g, sh, sc)


def _norm_mod_bwd(dh, x, dxo, g, sc, name):
    t = x.shape[0]
    tm = _pick(t, 256, 8)

    def body(dh_ref, x_ref, dxo_ref, g_ref, sc_ref, dx_ref, dsh_ref, dsc_ref, dg_ref):
        @pl.when(pl.program_id(0) == 0)
        def _():
            dsh_ref[...] = jnp.zeros_like(dsh_ref)
            dsc_ref[...] = jnp.zeros_like(dsc_ref)
            dg_ref[...] = jnp.zeros_like(dg_ref)

        xf, dh_ = x_ref[...], dh_ref[...]
        r = lax.rsqrt(jnp.mean(xf * xf, axis=-1, keepdims=True) + EPS)
        xhat = xf * r
        dsh_ref[...] += jnp.sum(dh_, axis=0, keepdims=True)
        dsc_ref[...] += jnp.sum(dh_ * (xhat * g_ref[...]), axis=0, keepdims=True)
        tt = dh_ * (1.0 + sc_ref[...])
        dg_ref[...] += jnp.sum(tt * xhat, axis=0, keepdims=True)
        dxh = tt * g_ref[...]
        dx_ref[...] = r * (dxh - xhat * jnp.mean(dxh * xhat, axis=-1, keepdims=True)) + dxo_ref[...]

    vec = pl.BlockSpec((1, D), lambda i: (0, 0))
    row = pl.BlockSpec((tm, D), lambda i: (i, 0))
    vshape = jax.ShapeDtypeStruct((1, D), F32)
    return _pcall(body, name=name, out_shape=[jax.ShapeDtypeStruct((t, D), F32), vshape, vshape, vshape], grid=(t // tm,),
                  in_specs=[row, row, row, vec, vec], out_specs=[row, vec, vec, vec], sem=("arbitrary",))(dh, x, dxo, g, sc)


def _gate_bwd(dxo, yy, gate, name):
    t = dxo.shape[0]
    tm = _pick(t, 512, 8)

    def body(dx_ref, y_ref, g_ref, dy_ref, dg_ref):
        @pl.when(pl.program_id(0) == 0)
        def _():
            dg_ref[...] = jnp.zeros_like(dg_ref)

        dx = dx_ref[...]
        dy_ref[...] = (dx * g_ref[...]).astype(BF16)
        dg_ref[...] += jnp.sum(dx * y_ref[...], axis=0, keepdims=True)

    vec = pl.BlockSpec((1, D), lambda i: (0, 0))
    row = pl.BlockSpec((tm, D), lambda i: (i, 0))
    return _pcall(body, name=name, out_shape=[jax.ShapeDtypeStruct((t, D), BF16), jax.ShapeDtypeStruct((1, D), F32)],
                  grid=(t // tm,), in_specs=[row, row, vec], out_specs=[row, vec], sem=("arbitrary",))(dxo, yy, gate)


def _final_loss(x, g, tgt, name):
    t = x.shape[0]
    tm = _pick(t, 256, 8)

    def body(x_ref, g_ref, t_ref, loss_ref, dx_ref, dg_ref):
        @pl.when(pl.program_id(0) == 0)
        def _():
            loss_ref[...] = jnp.zeros_like(loss_ref)
            dg_ref[...] = jnp.zeros_like(dg_ref)

        xf = x_ref[...]
        r = lax.rsqrt(jnp.mean(xf * xf, axis=-1, keepdims=True) + EPS)
        xhat = xf * r
        e = xhat * g_ref[...] - t_ref[...]
        per_tok = jnp.mean(e * e, axis=-1, keepdims=True)
        loss_ref[...] += 0.5 * jnp.sum(per_tok, axis=0, keepdims=True)
        dy = e * (1.0 / D)
        dg_ref[...] += jnp.sum(dy * xhat, axis=0, keepdims=True)
        dxh = dy * g_ref[...]
        dx_ref[...] = r * (dxh - xhat * jnp.mean(dxh * xhat, axis=-1, keepdims=True))

    vec = pl.BlockSpec((1, D), lambda i: (0, 0))
    row = pl.BlockSpec((tm, D), lambda i: (i, 0))
    return _pcall(body, name=name,
                  out_shape=[jax.ShapeDtypeStruct((1, LANE), F32), jax.ShapeDtypeStruct((t, D), F32),
                             jax.ShapeDtypeStruct((1, D), F32)],
                  grid=(t // tm,), in_specs=[row, vec, row],
                  out_specs=[pl.BlockSpec((1, LANE), lambda i: (0, 0)), row, vec], sem=("arbitrary",))(x, g, tgt)


def _mm(a, b, out_dtype, name):
    m, k = a.shape
    n = b.shape[1]
    tm = _pick(m, 1024, 8)
    tn = _pick(n, 1408, LANE)

    def body(a_ref, b_ref, o_ref):
        o_ref[...] = _dot(a_ref[...], b_ref[...]).astype(out_dtype)

    return _pcall(body, name=name, out_shape=jax.ShapeDtypeStruct((m, n), out_dtype), grid=(m // tm, n // tn),
                  in_specs=[pl.BlockSpec((tm, k), lambda i, j: (i, 0)), pl.BlockSpec((k, tn), lambda i, j: (0, j))],
                  out_specs=pl.BlockSpec((tm, tn), lambda i, j: (i, j)), sem=("parallel", "parallel"))(a, b)


def _mm_resid(a, b, resid, gate, name, sides=()):
    m, k = a.shape
    n = b.shape[1]
    tm, tn, tk = _pick(m, 512, 8), _pick(n, 1024, LANE), _pick(k, 2816, LANE)
    nk = k // tk

    def body(a_ref, b_ref, r_ref, g_ref, y_ref, o_ref, acc):
        kk = pl.program_id(2)

        @pl.when(kk == 0)
        def _():
            acc[...] = jnp.zeros_like(acc)

        acc[...] += _dot(a_ref[...], b_ref[...])

        @pl.when(kk == nk - 1)
        def _():
            y_ref[...] = acc[...]
            o_ref[...] = r_ref[...] + g_ref[...] * acc[...]

    blk = pl.BlockSpec((tm, tn), lambda i, j, kk: (i, j))
    return _pcall(body, name=name, out_shape=[jax.ShapeDtypeStruct((m, n), F32)] * 2, grid=(m // tm, n // tn, nk),
                  in_specs=[pl.BlockSpec((tm, tk), lambda i, j, kk: (i, kk)), pl.BlockSpec((tk, tn), lambda i, j, kk: (kk, j)),
                            blk, pl.BlockSpec((1, tn), lambda i, j, kk: (0, j))],
                  out_specs=[blk, blk], scratch=[pltpu.VMEM((tm, tn), F32)],
                  sem=("parallel", "parallel", "arbitrary"), sides=sides)(a, b, resid, gate)


def _mm_nt(a, b, out_dtype, name, sides=()):
    m, k = a.shape
    n = b.shape[0]
    tm, tn, tk = _pick(m, 1024, 8), _pick(n, 1024, LANE), _pick(k, 2816, LANE)
    nk = k // tk

    def body(a_ref, b_ref, o_ref, acc):
        kk = pl.program_id(2)

        @pl.when(kk == 0)
        def _():
            acc[...] = jnp.zeros_like(acc)

        acc[...] += _dot(a_ref[...], b_ref[...], NT)

        @pl.when(kk == nk - 1)
        def _():
            o_ref[...] = acc[...].astype(out_dtype)

    return _pcall(body, name=name, out_shape=jax.ShapeDtypeStruct((m, n), out_dtype), grid=(m // tm, n // tn, nk),
                  in_specs=[pl.BlockSpec((tm, tk), lambda i, j, kk: (i, kk)), pl.BlockSpec((tn, tk), lambda i, j, kk: (j, kk))],
                  out_specs=pl.BlockSpec((tm, tn), lambda i, j, kk: (i, j)), scratch=[pltpu.VMEM((tm, tn), F32)],
                  sem=("parallel", "parallel", "arbitrary"), sides=sides)(a, b)


def _mm_tn(a, b, out_dtype, tm_cap, name, sides=()):
    t, m = a.shape
    n = b.shape[1]
    tm, tn, tk = _pick(m, tm_cap, LANE), _pick(n, 1408, LANE), _pick(t, 2048, 16)
    nk = t // tk

    def body(a_ref, b_ref, o_ref, acc):
        kk = pl.program_id(2)

        @pl.when(kk == 0)
        def _():
            acc[...] = jnp.zeros_like(acc)

        acc[...] += _dot(a_ref[...], b_ref[...], TN)

        @pl.when(kk == nk - 1)
        def _():
            o_ref[...] = acc[...].astype(out_dtype)

    return _pcall(body, name=name, out_shape=jax.ShapeDtypeStruct((m, n), out_dtype), grid=(m // tm, n // tn, nk),
                  in_specs=[pl.BlockSpec((tk, tm), lambda i, j, kk: (kk, i)), pl.BlockSpec((tk, tn), lambda i, j, kk: (kk, j))],
                  out_specs=pl.BlockSpec((tm, tn), lambda i, j, kk: (i, j)), scratch=[pltpu.VMEM((tm, tn), F32)],
                  sem=("parallel", "parallel", "arbitrary"), sides=sides)(a, b)


def _ffn_up(h, wg, name, sides=()):
    t = h.shape[0]
    fp = wg.shape[2]
    tm, tn = _pick(t, 1024, 8), 256
    nn = fp // tn

    def body(h_ref, wa_ref, wb_ref, a_ref, b_ref, u_ref):
        hh = h_ref[...]
        a = _dot(hh, wa_ref[...])
        b = _dot(hh, wb_ref[...])
        a_ref[...] = a.astype(BF16)
        b_ref[...] = b.astype(BF16)
        u_ref[...] = (a * jax.nn.sigmoid(a) * b).astype(BF16)

    out = pl.BlockSpec((tm, tn), lambda i, j, n: (i, j * nn + n))
    return _pcall(body, name=name, out_shape=[jax.ShapeDtypeStruct((t, 2 * fp), BF16)] * 3, grid=(t // tm, 2, nn),
                  in_specs=[pl.BlockSpec((tm, D), lambda i, j, n: (i, 0)),
                            pl.BlockSpec((None, D, tn), lambda i, j, n: (j, 0, n)),
                            pl.BlockSpec((None, D, tn), lambda i, j, n: (j + 2, 0, n))],
                  out_specs=[out, out, out], sem=("parallel", "parallel", "parallel"), sides=sides)(h, wg, wg)


def _ffn_dab(dy, wo, a, b, name, sides=()):
    t = dy.shape[0]
    f2 = wo.shape[0]
    tm, tn = _pick(t, 1024, 8), 256

    def body(dy_ref, w_ref, a_ref, b_ref, o_ref):
        du = _dot(dy_ref[...], w_ref[...], NT)
        av, bv = a_ref[...].astype(F32), b_ref[...].astype(F32)
        sg = jax.nn.sigmoid(av)
        o_ref[0] = (du * bv * (sg * (1.0 + av * (1.0 - sg)))).astype(BF16)
        o_ref[1] = (du * (av * sg)).astype(BF16)

    blk = pl.BlockSpec((tm, tn), lambda i, n: (i, n))
    return _pcall(body, name=name, out_shape=jax.ShapeDtypeStruct((2, t, f2), BF16), grid=(t // tm, f2 // tn),
                  in_specs=[pl.BlockSpec((tm, D), lambda i, n: (i, 0)), pl.BlockSpec((tn, D), lambda i, n: (n, 0)), blk, blk],
                  out_specs=pl.BlockSpec((2, tm, tn), lambda i, n: (0, i, n)), sem=("parallel", "parallel"), sides=sides)(dy, wo, a, b)


def _ffn_dwin(h, dab, name, sides=()):
    t = h.shape[0]
    fp = dab.shape[2] // 2
    tm, tn, tk = _pick(fp, 1408, LANE), 1024, _pick(t, 2048, 16)
    nm, nk = fp // tm, t // tk

    def body(d_ref, h_ref, o_ref, acc):
        kk = pl.program_id(3)

        @pl.when(kk == 0)
        def _():
            acc[...] = jnp.zeros_like(acc)

        acc[...] += _dot(d_ref[...], h_ref[...], TN)

        @pl.when(kk == nk - 1)
        def _():
            o_ref[...] = acc[...].astype(BF16)

    return _pcall(body, name=name, out_shape=jax.ShapeDtypeStruct((4, fp, D), BF16), grid=(4, nm, D // tn, nk),
                  in_specs=[pl.BlockSpec((None, tk, tm), lambda u, i, n, kk: (u // 2, kk, (u % 2) * nm + i)),
                            pl.BlockSpec((tk, tn), lambda u, i, n, kk: (kk, n))],
                  out_specs=pl.BlockSpec((None, tm, tn), lambda u, i, n, kk: (u, i, n)),
                  scratch=[pltpu.VMEM((tm, tn), F32)],
                  sem=("parallel", "parallel", "parallel", "arbitrary"), sides=sides)(dab, h)


def _ffn_dh(dab, wg, name, sides=()):
    t = dab.shape[1]
    fp = wg.shape[2]
    tm, tn, tk = _pick(t, 1024, 8), 1024, _pick(fp, 2816, LANE)
    nkk = fp // tk

    def body(d_ref, w_ref, o_ref, acc):
        u, kk = pl.program_id(2), pl.program_id(3)

        @pl.when((u == 0) & (kk == 0))
        def _():
            acc[...] = jnp.zeros_like(acc)

        acc[...] += _dot(d_ref[...], w_ref[...], NT)

        @pl.when((u == 3) & (kk == nkk - 1))
        def _():
            o_ref[...] = acc[...]

    return _pcall(body, name=name, out_shape=jax.ShapeDtypeStruct((t, D), F32), grid=(t // tm, D // tn, 4, nkk),
                  in_specs=[pl.BlockSpec((None, tm, tk), lambda i, j, u, kk: (u // 2, i, (u % 2) * nkk + kk)),
                            pl.BlockSpec((None, tn, tk), lambda i, j, u, kk: (u, j, kk))],
                  out_specs=pl.BlockSpec((tm, tn), lambda i, j, u, kk: (i, j)), scratch=[pltpu.VMEM((tm, tn), F32)],
                  sem=("parallel", "parallel", "arbitrary", "arbitrary"), sides=sides)(dab, wg)


def _split3(v):
    hi = v.astype(BF16)
    r1 = v - hi.astype(F32)
    mid = r1.astype(BF16)
    lo = (r1 - mid.astype(F32)).astype(BF16)
    return hi, mid, lo


def _tri_sum(tri, v):
    hi, mid, lo = _split3(v)
    return (_dot(tri, hi) + _dot(tri, mid)) + _dot(tri, lo)


def _forget_fwd(proj, b_pad, name):
    t = proj.shape[0]
    tb = _pick(t, 256, 8)
    col = D_PROJ // LANE

    def body(f_ref, b_ref, o_ref, carry):
        @pl.when(pl.program_id(0) == 0)
        def _():
            carry[...] = jnp.zeros_like(carry)

        z = f_ref[...] + b_ref[...]
        lf = jnp.minimum(z, 0.0) - jnp.log(1.0 + jnp.exp(-jnp.abs(z)))
        r = lax.broadcasted_iota(jnp.int32, (tb, tb), 0)
        cidx = lax.broadcasted_iota(jnp.int32, (tb, tb), 1)
        tri = (r >= cidx).astype(BF16)
        o_ref[...] = _tri_sum(tri, lf) + carry[...]
        carry[...] += jnp.sum(lf, axis=0, keepdims=True)

    return _pcall(body, name=name, out_shape=jax.ShapeDtypeStruct((t, LANE), F32), grid=(t // tb,),
                  in_specs=[pl.BlockSpec((tb, LANE), lambda i: (i, col)), pl.BlockSpec((1, LANE), lambda i: (0, 0))],
                  out_specs=pl.BlockSpec((tb, LANE), lambda i: (i, 0)), scratch=[pltpu.VMEM((1, LANE), F32)],
                  sem=("arbitrary",))(proj, b_pad)


def _forget_bwd(d_cum, proj, b_pad, name):
    t = proj.shape[0]
    tb = _pick(t, 256, 8)
    nb = t // tb
    col = D_PROJ // LANE

    def body(d_ref, f_ref, b_ref, o_ref, db_ref, carry):
        @pl.when(pl.program_id(0) == 0)
        def _():
            carry[...] = jnp.zeros_like(carry)
            db_ref[...] = jnp.zeros_like(db_ref)

        dc = d_ref[...]
        r = lax.broadcasted_iota(jnp.int32, (tb, tb), 0)
        cidx = lax.broadcasted_iota(jnp.int32, (tb, tb), 1)
        tri = (r <= cidx).astype(BF16)
        dlf = _tri_sum(tri, dc) + carry[...]
        carry[...] += jnp.sum(dc, axis=0, keepdims=True)
        z = f_ref[...] + b_ref[...]
        lane = lax.broadcasted_iota(jnp.int32, (tb, LANE), 1)
        dz = jnp.where(lane < N_HEADS, dlf * jax.nn.sigmoid(-z), 0.0)
        o_ref[...] = dz.astype(BF16)
        db_ref[...] += jnp.sum(dz, axis=0, keepdims=True)

    return _pcall(body, name=name, out_shape=[jax.ShapeDtypeStruct((t, LANE), BF16), jax.ShapeDtypeStruct((1, LANE), F32)],
                  grid=(nb,),
                  in_specs=[pl.BlockSpec((tb, LANE), lambda i: (nb - 1 - i, 0)),
                            pl.BlockSpec((tb, LANE), lambda i: (nb - 1 - i, col)),
                            pl.BlockSpec((1, LANE), lambda i: (0, 0))],
                  out_specs=[pl.BlockSpec((tb, LANE), lambda i: (nb - 1 - i, 0)), pl.BlockSpec((1, LANE), lambda i: (0, 0))],
                  scratch=[pltpu.VMEM((1, LANE), F32)], sem=("arbitrary",))(d_cum, proj, b_pad)


def _head_norm(v, g):
    r = lax.rsqrt(jnp.mean(v * v, axis=-1, keepdims=True) + EPS)
    return v * r, r


def _qkv_prep(proj, qg, kg, name):
    t = proj.shape[0]
    tm = _pick(t, 1024, 8)

    def body(q_ref, k_ref, v_ref, qg_ref, kg_ref, qo_ref, ko_ref, vo_ref):
        qo_ref[...] = (_head_norm(q_ref[...], None)[0] * qg_ref[...]).astype(BF16)
        ko_ref[...] = (_head_norm(k_ref[...], None)[0] * kg_ref[...]).astype(BF16)
        vo_ref[...] = v_ref[...].astype(BF16)

    def blk(off):
        return pl.BlockSpec((tm, HEAD_DIM), lambda i, h: (i, off + h))

    vec = pl.BlockSpec((1, HEAD_DIM), lambda i, h: (0, 0))
    return _pcall(body, name=name, out_shape=[jax.ShapeDtypeStruct((t, D_ATTN), BF16)] * 3, grid=(t // tm, N_HEADS),
                  in_specs=[blk(0), blk(N_HEADS), blk(2 * N_HEADS), vec, vec], out_specs=[blk(0)] * 3,
                  sem=("parallel", "parallel"))(proj, proj, proj, qg, kg)


def _qk_norm_bwd(dqn, dkn, proj, qg, kg, name):
    t = proj.shape[0]
    tm = _pick(t, 1024, 8)

    def one(d_ref, v_ref, g_ref, o_ref, dg_ref):
        xhat, r = _head_norm(v_ref[...], None)
        d = d_ref[...]
        dg_ref[...] += jnp.sum(d * xhat, axis=0, keepdims=True)
        dxh = d * g_ref[...]
        o_ref[...] = (r * (dxh - xhat * jnp.mean(dxh * xhat, axis=-1, keepdims=True))).astype(BF16)

    def body(dq_ref, dk_ref, q_ref, k_ref, qg_ref, kg_ref, qo_ref, ko_ref, dqg_ref, dkg_ref):
        @pl.when((pl.program_id(0) == 0) & (pl.program_id(1) == 0))
        def _():
            dqg_ref[...] = jnp.zeros_like(dqg_ref)
            dkg_ref[...] = jnp.zeros_like(dkg_ref)

        one(dq_ref, q_ref, qg_ref, qo_ref, dqg_ref)
        one(dk_ref, k_ref, kg_ref, ko_ref, dkg_ref)

    def blk(off):
        return pl.BlockSpec((tm, HEAD_DIM), lambda i, h: (i, off + h))

    vec = pl.BlockSpec((1, HEAD_DIM), lambda i, h: (0, 0))
    vshape = jax.ShapeDtypeStruct((1, HEAD_DIM), F32)
    return _pcall(body, name=name, out_shape=[jax.ShapeDtypeStruct((t, D_ATTN), BF16)] * 2 + [vshape, vshape],
                  grid=(t // tm, N_HEADS),
                  in_specs=[blk(0), blk(0), blk(0), blk(N_HEADS), vec, vec], out_specs=[blk(0), blk(0), vec, vec],
                  sem=("arbitrary", "arbitrary"))(dqn, dkn, proj, proj, qg, kg)


ATTN_SCALE = HEAD_DIM ** -0.5


def _logits(q, k, fq, fk, diag, tq, tk):
    s = _dot(q, k, NT) * ATTN_SCALE + (fq - fk)
    if diag:
        r = lax.broadcasted_iota(jnp.int32, (tq, tk), 0)
        cidx = lax.broadcasted_iota(jnp.int32, (tq, tk), 1)
        s = jnp.where(r >= cidx, s, -jnp.inf)
    return s


def _tri(tt, n, by_row):
    if by_row:
        i = sum((tt >= k * (k + 1) // 2).astype(jnp.int32) for k in range(1, n))
        return i, tt - i * (i + 1) // 2
    j = sum((tt >= k * n - k * (k - 1) // 2).astype(jnp.int32) for k in range(1, n))
    return j + tt - (j * n - j * (j - 1) // 2), j


def _attn_fwd(q, k, v, fq, fk, name, sides=()):
    t = q.shape[0]
    tq = tk = _pick(t, ATTN_BLOCK, LANE)
    nk = t // tk

    def body(q_ref, k_ref, v_ref, fq_ref, fk_ref, o_ref, lse_ref, m_s, l_s, acc):
        i, j = _tri(pl.program_id(1), nk, True)

        @pl.when(j == 0)
        def _():
            m_s[...] = jnp.full_like(m_s, -jnp.inf)
            l_s[...] = jnp.zeros_like(l_s)
            acc[...] = jnp.zeros_like(acc)

        def step(diag):
            s = _logits(q_ref[...], k_ref[...], fq_ref[...], fk_ref[...], diag, tq, tk)
            m_new = jnp.maximum(m_s[...], jnp.max(s, axis=-1, keepdims=True))
            alpha = jnp.exp(m_s[...] - m_new)
            p = jnp.exp(s - m_new)
            l_s[...] = alpha * l_s[...] + jnp.sum(p, axis=-1, keepdims=True)
            acc[...] = alpha * acc[...] + _dot(p.astype(BF16), v_ref[...])
            m_s[...] = m_new

        @pl.when(j < i)
        def _():
            step(False)

        @pl.when(j == i)
        def _():
            step(True)

        @pl.when(j == i)
        def _():
            o_ref[...] = (acc[...] / l_s[...]).astype(BF16)
            lse_ref[...] = m_s[...] + jnp.log(l_s[...])

    qb = pl.BlockSpec((tq, HEAD_DIM), lambda h, tt: (_tri(tt, nk, True)[0], h))
    kb = pl.BlockSpec((tk, HEAD_DIM), lambda h, tt: (_tri(tt, nk, True)[1], h))
    col = pl.BlockSpec((None, tq, 1), lambda h, tt: (h, _tri(tt, nk, True)[0], 0))
    return _pcall(body, name=name,
                  out_shape=[jax.ShapeDtypeStruct((t, D_ATTN), BF16), jax.ShapeDtypeStruct((N_HEADS, t, 1), F32)],
                  grid=(N_HEADS, nk * (nk + 1) // 2),
                  in_specs=[qb, kb, kb, col, pl.BlockSpec((None, 1, tk), lambda h, tt: (h, 0, _tri(tt, nk, True)[1]))],
                  out_specs=[qb, col],
                  scratch=[pltpu.VMEM((tq, 1), F32), pltpu.VMEM((tq, 1), F32), pltpu.VMEM((tq, HEAD_DIM), F32)],
                  sem=("parallel", "arbitrary"), sides=sides)(q, k, v, fq, fk)


def _attn_bwd_q(q, k, v, o, do, lse, fq, fk, name, sides=()):
    t = q.shape[0]
    tq = tk = _pick(t, ATTN_BLOCK, LANE)
    nk = t // tk

    def body(q_ref, k_ref, v_ref, o_ref, do_ref, lse_ref, fq_ref, fk_ref, dq_ref, dl_ref, dfq_ref, acc, dl_s, df_s):
        i, j = _tri(pl.program_id(1), nk, True)

        @pl.when(j == 0)
        def _():
            acc[...] = jnp.zeros_like(acc)
            df_s[...] = jnp.zeros_like(df_s)
            dl_s[...] = jnp.sum(do_ref[...].astype(F32) * o_ref[...].astype(F32), axis=-1, keepdims=True)

        def step(diag):
            s = _logits(q_ref[...], k_ref[...], fq_ref[...], fk_ref[...], diag, tq, tk)
            p = jnp.exp(s - lse_ref[...])
            dp = _dot(do_ref[...], v_ref[...], NT)
            ds = p * (dp - dl_s[...])
            df_s[...] += jnp.sum(ds, axis=-1, keepdims=True)
            acc[...] += _dot(ds.astype(BF16), k_ref[...])

        @pl.when(j < i)
        def _():
            step(False)

        @pl.when(j == i)
        def _():
            step(True)

        @pl.when(j == i)
        def _():
            dq_ref[...] = acc[...] * ATTN_SCALE
            dl_ref[...] = dl_s[...]
            dfq_ref[...] = df_s[...]

    qb = pl.BlockSpec((tq, HEAD_DIM), lambda h, tt: (_tri(tt, nk, True)[0], h))
    kb = pl.BlockSpec((tk, HEAD_DIM), lambda h, tt: (_tri(tt, nk, True)[1], h))
    col = pl.BlockSpec((None, tq, 1), lambda h, tt: (h, _tri(tt, nk, True)[0], 0))
    cshape = jax.ShapeDtypeStruct((N_HEADS, t, 1), F32)
    return _pcall(body, name=name, out_shape=[jax.ShapeDtypeStruct((t, D_ATTN), F32), cshape, cshape],
                  grid=(N_HEADS, nk * (nk + 1) // 2),
                  in_specs=[qb, kb, kb, qb, qb, col, col, pl.BlockSpec((None, 1, tk), lambda h, tt: (h, 0, _tri(tt, nk, True)[1]))],
                  out_specs=[qb, col, col],
                  scratch=[pltpu.VMEM((tq, HEAD_DIM), F32), pltpu.VMEM((tq, 1), F32), pltpu.VMEM((tq, 1), F32)],
                  sem=("parallel", "arbitrary"), sides=sides)(q, k, v, o, do, lse, fq, fk)


def _attn_bwd_kv(q, k, v, do, lse, delta, fq, fk, name, sides=()):
    t = q.shape[0]
    tq = tk = _pick(t, ATTN_BLOCK, LANE)
    nq = t // tq

    def body(q_ref, k_ref, v_ref, do_ref, lse_ref, dl_ref, fq_ref, fk_ref, dk_ref, dv_ref, dfk_ref, dk_s, dv_s, df_s):
        i, j = _tri(pl.program_id(1), nq, False)

        @pl.when(i == j)
        def _():
            dk_s[...] = jnp.zeros_like(dk_s)
            dv_s[...] = jnp.zeros_like(dv_s)
            df_s[...] = jnp.zeros_like(df_s)

        def step(diag):
            s = _logits(q_ref[...], k_ref[...], fq_ref[...], fk_ref[...], diag, tq, tk)
            p = jnp.exp(s - lse_ref[...])
            dv_s[...] += _dot(p.astype(BF16), do_ref[...], TN)
            dp = _dot(do_ref[...], v_ref[...], NT)
            ds = p * (dp - dl_ref[...])
            df_s[...] -= jnp.sum(ds, axis=0, keepdims=True)
            dk_s[...] += _dot(ds.astype(BF16), q_ref[...], TN)

        @pl.when(i > j)
        def _():
            step(False)

        @pl.when(i == j)
        def _():
            step(True)

        @pl.when(i == nq - 1)
        def _():
            dk_ref[...] = dk_s[...] * ATTN_SCALE
            dv_ref[...] = dv_s[...].astype(BF16)
            dfk_ref[...] = df_s[...]

    qb = pl.BlockSpec((tq, HEAD_DIM), lambda h, tt: (_tri(tt, nq, False)[0], h))
    kb = pl.BlockSpec((tk, HEAD_DIM), lambda h, tt: (_tri(tt, nq, False)[1], h))
    col = pl.BlockSpec((None, tq, 1), lambda h, tt: (h, _tri(tt, nq, False)[0], 0))
    row = pl.BlockSpec((None, 1, tk), lambda h, tt: (h, 0, _tri(tt, nq, False)[1]))
    return _pcall(body, name=name,
                  out_shape=[jax.ShapeDtypeStruct((t, D_ATTN), F32), jax.ShapeDtypeStruct((t, D_ATTN), BF16),
                             jax.ShapeDtypeStruct((N_HEADS, 1, t), F32)],
                  grid=(N_HEADS, nq * (nq + 1) // 2),
                  in_specs=[qb, kb, kb, qb, col, col, col, row], out_specs=[kb, kb, row],
                  scratch=[pltpu.VMEM((tk, HEAD_DIM), F32), pltpu.VMEM((tk, HEAD_DIM), F32), pltpu.VMEM((1, tk), F32)],
                  sem=("parallel", "arbitrary"), sides=sides)(q, k, v, do, lse, delta, fq, fk)


def _window_sum(v, w, back):
    n = v.shape[0]
    k = 1
    while k < w:
        v = v + pltpu.roll(v, k if back else n - k, axis=0)
        k *= 2
    return v


def _pool_fwd(proj, pw, ps, name):
    t = proj.shape[0]
    tm = _pick(t, POOL_BLOCK, HALO)
    col = 3 * D_ATTN // D_POOL

    def body(u_ref, prev_ref, pw_ref, ps_ref, pooled_ref, out_ref):
        i = pl.program_id(0)
        prev = jnp.where(i > 0, prev_ref[...], 0.0)
        ext = jnp.concatenate([prev, u_ref[...]], axis=0)
        pos = i * tm + lax.broadcasted_iota(jnp.int32, (tm, 1), 0)
        for g, w in enumerate(POOL_WINDOWS):
            cols = slice(g * GROUP_DIM, (g + 1) * GROUP_DIM)
            xg = ext[:, cols]
            sw = _window_sum(xg, w, True)[HALO:, :]
            cnt = jnp.minimum(pos + 1, w).astype(F32)
            pooled = (sw / cnt - xg[HALO:, :]).astype(BF16)
            pooled_ref[:, cols] = pooled
            out_ref[:, cols] = (_dot(pooled, pw_ref[g]) * ps_ref[:, cols]).astype(BF16)

    row = pl.BlockSpec((tm, D_POOL), lambda i: (i, 0))
    return _pcall(body, name=name, out_shape=[jax.ShapeDtypeStruct((t, D_POOL), BF16)] * 2, grid=(t // tm,),
                  in_specs=[pl.BlockSpec((tm, D_POOL), lambda i: (i, col)),
                            pl.BlockSpec((HALO, D_POOL), lambda i: (jnp.maximum(i * (tm // HALO) - 1, 0), col)),
                            pl.BlockSpec((len(POOL_WINDOWS), GROUP_DIM, GROUP_DIM), lambda i: (0, 0, 0)),
                            pl.BlockSpec((1, D_POOL), lambda i: (0, 0))],
                  out_specs=[row, row], sem=("parallel",))(proj, proj, pw, ps)


def _pool_bwd(dout, pooled, pw, ps, name):
    t = pooled.shape[0]
    tm = _pick(t, POOL_BLOCK, HALO)
    nb = t // tm
    ng = len(POOL_WINDOWS)

    def body(d_ref, nxt_ref, p_ref, pw_ref, ps_ref, du_ref, dpw_ref, dps_ref):
        i = pl.program_id(0)

        @pl.when(i == 0)
        def _():
            dpw_ref[...] = jnp.zeros_like(dpw_ref)
            dps_ref[...] = jnp.zeros_like(dps_ref)

        nxt = jnp.where(i < nb - 1, nxt_ref[...].astype(F32), 0.0)
        ext = jnp.concatenate([d_ref[...].astype(F32), nxt], axis=0)
        pos = i * tm + lax.broadcasted_iota(jnp.int32, (tm + HALO, 1), 0)
        for g, w in enumerate(POOL_WINDOWS):
            cols = slice(g * GROUP_DIM, (g + 1) * GROUP_DIM)
            pooled_g = p_ref[:, cols]
            dg = ext[:, cols]
            pm = _dot(pooled_g, pw_ref[g])
            dps_ref[:, cols] += jnp.sum(dg[:tm, :] * pm, axis=0, keepdims=True)
            dpm = (dg * ps_ref[:, cols]).astype(BF16)
            dpw_ref[g] += _dot(pooled_g, dpm[:tm, :], TN)
            dpooled = _dot(dpm, pw_ref[g], NT)
            cnt = jnp.minimum(pos + 1, w).astype(F32)
            fwd = _window_sum(dpooled / cnt, w, False)
            du_ref[:, cols] = (fwd[:tm, :] - dpooled[:tm, :]).astype(BF16)

    row = pl.BlockSpec((tm, D_POOL), lambda i: (i, 0))
    return _pcall(body, name=name,
                  out_shape=[jax.ShapeDtypeStruct((t, D_POOL), BF16), jax.ShapeDtypeStruct((ng, GROUP_DIM, GROUP_DIM), F32),
                             jax.ShapeDtypeStruct((1, D_POOL), F32)],
                  grid=(nb,),
                  in_specs=[pl.BlockSpec((tm, D_POOL), lambda i: (i, 1)),
                            pl.BlockSpec((HALO, D_POOL), lambda i: (jnp.minimum((i + 1) * (tm // HALO), t // HALO - 1), 1)),
                            row, pl.BlockSpec((ng, GROUP_DIM, GROUP_DIM), lambda i: (0, 0, 0)),
                            pl.BlockSpec((1, D_POOL), lambda i: (0, 0))],
                  out_specs=[row, pl.BlockSpec((ng, GROUP_DIM, GROUP_DIM), lambda i: (0, 0, 0)),
                             pl.BlockSpec((1, D_POOL), lambda i: (0, 0))],
                  sem=("arbitrary",))(dout, dout, pooled, pw, ps)


def _pad_cols(w, n):
    return jnp.pad(w, ((0, 0), (0, n - w.shape[1])))


def kernel(x, c, w_ada, b_ada, ffn1_norm_g, ffn1_w_in, ffn1_w_out, mix_norm_g, w_in, b_forget, q_norm_g, k_norm_g, pool_w, pool_scale, w_out, ffn2_norm_g, ffn2_w_in, ffn2_w_out, final_norm_g, loss_target, m_w_ada, m_b_ada, m_ffn1_norm_g, m_ffn1_w_in, m_ffn1_w_out, m_mix_norm_g, m_w_in, m_b_forget, m_q_norm_g, m_k_norm_g, m_pool_w, m_pool_scale, m_w_out, m_ffn2_norm_g, m_ffn2_w_in, m_ffn2_w_out, m_final_norm_g, v_w_ada, v_b_ada, v_ffn1_norm_g, v_ffn1_w_in, v_ffn1_w_out, v_mix_norm_g, v_w_in, v_b_forget, v_q_norm_g, v_k_norm_g, v_pool_w, v_pool_scale, v_w_out, v_ffn2_norm_g, v_ffn2_w_in, v_ffn2_w_out, v_final_norm_g):
    ax, ay, ac = _coords()
    chip = 2 * ax + ay
    me = 2 * chip + ac
    chip_arr = jnp.reshape(chip, (1,)).astype(jnp.int32)
    core_arr = jnp.reshape(ac, (1,)).astype(jnp.int32)

    t = x.shape[1]
    xs = x.reshape(t, D)
    tgt = loss_target.reshape(t, D)
    hu = ffn1_w_out.shape[1]
    hup = -(-hu // LANE) * LANE
    ws_in = w_in.shape[2]
    ws_in_pad = -(-ws_in // LANE) * LANE
    n_ada = w_ada.shape[2]

    def ffn_in_shard(w):
        w = w[0].astype(BF16)
        return jnp.concatenate([_pad_cols(w[:, :hu], hup), _pad_cols(w[:, hu:], hup)], axis=1)

    def ffn_out_shard(w):
        return jnp.pad(w[0].astype(BF16), ((0, hup - hu), (0, 0)))

    shards1 = [ffn_in_shard(ffn1_w_in), ffn_out_shard(ffn1_w_out)]
    shards_mix = [_pad_cols(w_in[0].astype(BF16), ws_in_pad), pool_w[0].astype(BF16).reshape(GROUP_DIM, GROUP_DIM),
                  w_out[0].astype(BF16)]
    shards2 = [ffn_in_shard(ffn2_w_in), ffn_out_shard(ffn2_w_out)]

    def own_slot(gathered, shards):
        return [lax.dynamic_update_slice(g, w[None], (chip, 0, 0)) for g, w in zip(gathered, shards)]

    def add_siblings(parts, recv, tag):
        return [_add_sibling(p, r, core_arr, f"{tag}_add_sibling_{k}") for k, (p, r) in enumerate(zip(parts, recv))]

    def add_chips(halves, recv, tag):
        return [_add_chips(hh, r, chip_arr, f"{tag}_add_chips_{k}") for k, (hh, r) in enumerate(zip(halves, recv))]

    def both_halves(mine, theirs):
        return [jnp.where(ac == 0, jnp.concatenate([g, r], axis=0), jnp.concatenate([r, g], axis=0)) for g, r in zip(mine, theirs)]

    g1 = _exchange(_gather_d2d(_exchange(_gather_ici(shards1), "gather_ffn1_ici")), "gather_ffn1_d2d")
    g_in1, g_out1 = own_slot(g1, shards1)
    wg_out1 = g_out1.reshape(4 * hup, D)

    c_all = _allgather_small(c.reshape(8, D // 8), True, "gather_c").reshape(8, D)
    c16 = jnp.pad(c_all, ((0, 8), (0, 0)))
    b_ada_mine = lax.dynamic_slice(b_ada, (0, chip * n_ada), (1, n_ada))
    act16, mod16 = _ada_fwd(c16, w_ada[0], b_ada_mine, "ada_fwd")
    mod_all = _allgather_small(mod16[:8], False, "gather_mod")
    mod = lax.dynamic_index_in_dim(mod_all, me, axis=1, keepdims=False).reshape(N_MOD, 1, D)
    sh1, sc1, gt1, sh2, sc2, gt2, sh3, sc3, gt3 = [mod[k] for k in range(N_MOD)]

    gate1, gate3 = 0.5 * gt1, 0.5 * gt3
    h1 = _norm_mod(xs, ffn1_norm_g, sh1, sc1, "ffn1_norm")
    ici = _gather_ici(shards_mix)
    a1, b1, u1 = _ffn_up(h1, g_in1, "ffn1_up", sides=[ici])
    d2d = _gather_d2d(ici.results)
    y1, x1 = _mm_resid(u1, wg_out1, xs, gate1, "ffn1_down", sides=[d2d])
    g_win, g_pw, g_wout = own_slot(d2d.results, shards_mix)
    w_full = jnp.concatenate([g_win[k, :, :ws_in] for k in range(4)], axis=1)
    nf = 3 * D_ATTN
    w_all = jnp.concatenate([w_full[:, :nf], w_full[:, nf + N_HEADS:], w_full[:, nf:nf + N_HEADS],
                             jnp.zeros((D, LANE - N_HEADS), BF16)], axis=1)
    pw_full = g_pw.reshape(4, 4, GROUP_DIM // 4, GROUP_DIM).transpose(1, 0, 2, 3).reshape(4, GROUP_DIM, GROUP_DIM)
    wo_full = g_wout.reshape(4 * g_wout.shape[1], D)

    h2 = _norm_mod(x1, mix_norm_g, sh2, sc2, "mix_norm")
    proj = _mm(h2, w_all, F32, "mix_proj")
    b_pad = jnp.pad(b_forget, ((0, 0), (0, LANE - N_HEADS)))
    cum = _forget_fwd(proj, b_pad, "forget_fwd")
    cum_t = cum[:, :N_HEADS].T
    fq, fk = cum_t.reshape(N_HEADS, t, 1), cum_t.reshape(N_HEADS, 1, t)
    qn, kn, vb = _qkv_prep(proj, q_norm_g, k_norm_g, "qkv_prep")
    ici = _gather_ici(shards2)
    attn, lse = _attn_fwd(qn, kn, vb, fq, fk, "attn_fwd", sides=[ici])
    pooled, pool_out = _pool_fwd(proj, pw_full, pool_scale, "pool_fwd")
    cat = jnp.concatenate([attn, pool_out], axis=1)
    d2d = _gather_d2d(ici.results)
    y_mix, x2 = _mm_resid(cat, wo_full, x1, gt2, "mix_out", sides=[d2d])
    g_in2, g_out2 = own_slot(d2d.results, shards2)
    wg_out2 = g_out2.reshape(4 * hup, D)

    h3 = _norm_mod(x2, ffn2_norm_g, sh3, sc3, "ffn2_norm")
    a3, b3, u3 = _ffn_up(h3, g_in2, "ffn2_up")
    y3, x3 = _mm_resid(u3, wg_out2, x2, gate3, "ffn2_down")

    loss_part, dx3, d_final_g = _final_loss(x3, final_norm_g.reshape(1, D), tgt, "final_loss")
    loss = lax.psum(loss_part[0, 0], ("x", "y", "c"))

    dy3, dgt3 = _gate_bwd(dx3, y3, gate3, "ffn2_gate_bwd")
    dab3 = _ffn_dab(dy3, wg_out2, a3, b3, "ffn2_dab")
    parts2 = [_ffn_dwin(h3, dab3, "ffn2_dwin"), _mm_tn(u3, dy3, BF16, hup, "ffn2_dwout").reshape(4, hup, D)]
    rs = _reduce_siblings(parts2)
    dh3 = _ffn_dh(dab3, g_in2, "ffn2_dh", sides=[rs])
    halves2 = add_siblings(parts2, rs.results, "ffn2")
    dx2, dsh3, dsc3, d_ng3 = _norm_mod_bwd(dh3, x2, dx3, ffn2_norm_g, sc3, "ffn2_norm_bwd")

    dz, dgt2 = _gate_bwd(dx2, y_mix, gt2, "mix_gate_bwd")
    dcat = _mm_nt(dz, wo_full, BF16, "mix_dcat")
    d_wo = _mm_tn(cat, dz, BF16, 1024, "mix_dwout")
    du_pool, d_pw, d_ps = _pool_bwd(dcat, pooled, pw_full, pool_scale, "pool_bwd")
    rs = _reduce_chips(halves2)
    dqn, delta, dfq = _attn_bwd_q(qn, kn, vb, attn, dcat, lse, fq, fk, "attn_bwd_q", sides=[rs])
    mine2 = add_chips(halves2, rs.results, "ffn2")
    rs = _share_siblings(mine2)
    dkn, dv, dfk = _attn_bwd_kv(qn, kn, vb, dcat, lse, delta, fq, fk, "attn_bwd_kv", sides=[rs])
    r_in2, r_out2 = both_halves(mine2, rs.results)
    d_cum = jnp.pad((dfq.reshape(N_HEADS, t) + dfk.reshape(N_HEADS, t)).T, ((0, 0), (0, LANE - N_HEADS)))
    dfl, d_bf = _forget_bwd(d_cum, proj, b_pad, "forget_bwd")
    dq, dk, d_qg, d_kg = _qk_norm_bwd(dqn, dkn, proj, q_norm_g, k_norm_g, "qk_norm_bwd")
    dproj = jnp.concatenate([dq, dk, dv, du_pool, dfl], axis=1)
    d_wall = _mm_tn(h2, dproj, F32, 1024, "mix_dwin")
    d_wfull = jnp.concatenate([d_wall[:, :nf], d_wall[:, D_PROJ:D_PROJ + N_HEADS], d_wall[:, nf:D_PROJ]], axis=1)
    p_win = jnp.stack([_pad_cols(d_wfull[:, k * ws_in:(k + 1) * ws_in], ws_in_pad) for k in range(4)]).astype(BF16)
    p_pw = d_pw.reshape(4, 4, GROUP_DIM // 4, GROUP_DIM).transpose(1, 0, 2, 3).reshape(4, GROUP_DIM, GROUP_DIM).astype(BF16)
    parts_mix = [p_win, p_pw, d_wo.reshape(4, D // 4, D)]
    rs = _reduce_siblings(parts_mix)
    dh2 = _mm_nt(dproj, w_all, F32, "mix_dh", sides=[rs])
    halves_mix = add_siblings(parts_mix, rs.results, "mix")
    dx1, dsh2, dsc2, d_ng2 = _norm_mod_bwd(dh2, x1, dx2, mix_norm_g, sc2, "mix_norm_bwd")

    dy1, dgt1 = _gate_bwd(dx1, y1, gate1, "ffn1_gate_bwd")
    rs = _reduce_chips(halves_mix)
    dab1 = _ffn_dab(dy1, wg_out1, a1, b1, "ffn1_dab", sides=[rs])
    mine_mix = add_chips(halves_mix, rs.results, "mix")
    rs = _share_siblings(mine_mix)
    d_out1 = _mm_tn(u1, dy1, BF16, hup, "ffn1_dwout", sides=[rs]).reshape(4, hup, D)
    r_win, r_pw, r_wo = both_halves(mine_mix, rs.results)
    parts1 = [_ffn_dwin(h1, dab1, "ffn1_dwin"), d_out1]
    rs = _reduce_siblings(parts1)
    dh1 = _ffn_dh(dab1, g_in1, "ffn1_dh", sides=[rs])
    halves1 = add_siblings(parts1, rs.results, "ffn1")
    dx0, dsh1, dsc1, d_ng1 = _norm_mod_bwd(dh1, xs, dx1, ffn1_norm_g, sc1, "ffn1_norm_bwd")
    grad_x = dx0.reshape(1, t, D)
    dgt1, dgt3 = 0.5 * dgt1, 0.5 * dgt3
    mine1 = add_chips(halves1, _exchange(_reduce_chips(halves1), "ffn1_reduce_chips"), "ffn1")
    r_in1, r_out1 = both_halves(mine1, _exchange(_share_siblings(mine1), "ffn1_share_siblings"))
    grads = {
        "ffn1_w_in": jnp.concatenate([r_in1[:hu], r_in1[hup:hup + hu]], axis=0),
        "ffn1_w_out": r_out1[:hu],
        "w_in": r_win[:, :ws_in],
        "pool_w": r_pw,
        "w_out": r_wo,
        "ffn2_w_in": jnp.concatenate([r_in2[:hu], r_in2[hup:hup + hu]], axis=0),
        "ffn2_w_out": r_out2[:hu],
    }
    hidden_in_rows = ("ffn1_w_in", "ffn2_w_in")

    dmod = jnp.concatenate([dsh1, dsc1, dgt1, dsh2, dsc2, dgt2, dsh3, dsc3, dgt3], axis=1)
    small_names = ["b_ada", "ffn1_norm_g", "mix_norm_g", "b_forget", "q_norm_g", "k_norm_g", "pool_scale", "ffn2_norm_g",
                   "final_norm_g"]
    small_grads = [dmod, d_ng1, d_ng2, d_bf[:, :N_HEADS], d_qg, d_kg, d_ps, d_ng3, d_final_g]
    small_w = [b_ada, ffn1_norm_g, mix_norm_g, b_forget, q_norm_g, k_norm_g, pool_scale, ffn2_norm_g, final_norm_g.reshape(1, D)]
    small_m = [m_b_ada, m_ffn1_norm_g, m_mix_norm_g, m_b_forget, m_q_norm_g, m_k_norm_g, m_pool_scale, m_ffn2_norm_g,
               m_final_norm_g.reshape(1, D)]
    small_v = [v_b_ada, v_ffn1_norm_g, v_mix_norm_g, v_b_forget, v_q_norm_g, v_k_norm_g, v_pool_scale, v_ffn2_norm_g,
               v_final_norm_g.reshape(1, D)]
    sizes = [g.shape[1] for g in small_grads]
    n_small = sum(sizes)
    n_pack = -(-n_small // (8 * LANE)) * (8 * LANE)

    def pack(vs, fill):
        flat = jnp.concatenate([v.reshape(1, -1) for v in vs], axis=1)
        return jnp.pad(flat, ((0, 0), (0, n_pack - n_small)), constant_values=fill).reshape(8, n_pack // 8)

    g8 = _allgather_small(pack(small_grads, 0.0), True, "gather_small_grads")
    gs, ds, ms, vs = _adamw_small(pack(small_w, 0.0), g8, pack(small_m, 0.0), pack(small_v, 1.0), "adamw_small")

    def unpack(p):
        flat = p.reshape(1, n_pack)
        out, off = {}, 0
        for nme, sz in zip(small_names, sizes):
            out[nme] = flat[:, off:off + sz]
            off += sz
        return out

    small = [unpack(p) for p in (gs, ds, ms, vs)]
    for dct in small:
        dct["final_norm_g"] = dct["final_norm_g"].reshape(D)

    dmod_all = g8.reshape(8, n_pack)[:, :N_MOD * D]
    dmod_mine = lax.dynamic_slice(dmod_all, (0, chip * n_ada), (8, n_ada))
    grads["w_ada"] = _ada_bwd(act16[:8].reshape(8, D, 1), dmod_mine.reshape(8, 1, n_ada), "ada_bwd")

    big = {"w_ada": (w_ada, m_w_ada, v_w_ada), "ffn1_w_in": (ffn1_w_in, m_ffn1_w_in, v_ffn1_w_in),
           "ffn1_w_out": (ffn1_w_out, m_ffn1_w_out, v_ffn1_w_out), "w_in": (w_in, m_w_in, v_w_in),
           "pool_w": (pool_w, m_pool_w, v_pool_w), "w_out": (w_out, m_w_out, v_w_out),
           "ffn2_w_in": (ffn2_w_in, m_ffn2_w_in, v_ffn2_w_in), "ffn2_w_out": (ffn2_w_out, m_ffn2_w_out, v_ffn2_w_out)}
    res = {}
    for nme, (w, m, v) in big.items():
        shp = w.shape
        g2 = grads[nme]
        if nme in hidden_in_rows:
            d, mo, vo = _adamw(w[0].T, g2, m[0].T, v[0].T, f"adamw_{nme}")
            res[nme] = tuple(o.T.reshape(shp) for o in (g2, d, mo, vo))
            continue
        two = g2.shape
        d, mo, vo = _adamw(w.reshape(two), g2, m.reshape(two), v.reshape(two), f"adamw_{nme}")
        res[nme] = (g2.reshape(shp), d.reshape(shp), mo.reshape(shp), vo.reshape(shp))
    for nme in small_names:
        res[nme] = tuple(dct[nme] for dct in small)

    order = ["w_ada", "b_ada", "ffn1_norm_g", "ffn1_w_in", "ffn1_w_out", "mix_norm_g", "w_in", "b_forget", "q_norm_g",
             "k_norm_g", "pool_w", "pool_scale", "w_out", "ffn2_norm_g", "ffn2_w_in", "ffn2_w_out", "final_norm_g"]
    return (loss, grad_x, *[res[n][0] for n in order], *[res[n][1] for n in order], *[res[n][2] for n in order],
            *[res[n][3] for n in order])
```

```python
import functools

import jax
import jax.numpy as jnp
from jax import lax
from jax.experimental import pallas as pl
from jax.experimental.pallas import tpu as pltpu

F32 = jnp.float32
BF16 = jnp.bfloat16

D = 2048
N_HEADS = 8
HEAD_DIM = 128
D_ATTN = 1024
D_POOL = 1024
POOL_WINDOWS = (2, 4, 8, 16)
GROUP_DIM = 256
HALO = 16
N_MOD = 9
EPS = 1e-6
D_PROJ = 3 * D_ATTN + D_POOL
D_PROJ_PAD = D_PROJ + 128
LANE = 128
ATTN_BLOCK = 512
POOL_BLOCK = 512
ATTN_HEADS = (2, 2, 2)
ATTN_ROW_CHUNKS = (2, 2, 2)

ADAM_LR = 0.001
ADAM_B1 = 0.9
ADAM_B2 = 0.999
ADAM_EPS = 1e-08
ADAM_WD = 0.01
ADAM_STEP = 10

VMEM_LIMIT_V7X = 56 * 1024 * 1024
MESH_ID = pl.DeviceIdType.MESH
ANY = pl.BlockSpec(memory_space=pl.ANY)
VMEM = pl.BlockSpec(memory_space=pltpu.VMEM)

NT = (((1,), (1,)), ((), ()))
TN = (((0,), (0,)), ((), ()))


class _Side:
    def __init__(self, ins, outs, nsem, start, wait, alias=False):
        self.ins, self.outs, self.nsem, self.start, self.wait, self.alias = list(ins), list(outs), nsem, start, wait, alias
        self.results = None


def _pcall(body, *, name, out_shape, grid=None, in_specs=None, out_specs=None, scratch=(), sem=None, prefetch=0, sides=()):
    sides = list(sides)
    single = not isinstance(out_shape, (list, tuple))
    shapes = [out_shape] if single else list(out_shape)
    in_specs = list(in_specs)
    ospecs = [out_specs] if single else list(out_specs)
    scratch = list(scratch)
    n_in, n_out, n_scr = len(in_specs), len(shapes), len(scratch)
    assert not (sides and prefetch)
    aliases = {}
    for sd in sides:
        if sd.alias:
            for k in range(len(sd.ins)):
                aliases[len(in_specs) + k] = len(shapes) + k
        in_specs += [ANY] * len(sd.ins)
        shapes += sd.outs
        ospecs += [ANY] * len(sd.outs)
        scratch += [pltpu.SemaphoreType.DMA((sd.nsem,)), pltpu.SemaphoreType.DMA((sd.nsem,))]

    def wrapped(*refs):
        ins, outs, scr = refs[:len(in_specs)], refs[len(in_specs):len(in_specs) + len(shapes)], refs[len(in_specs) + len(shapes):]
        first = last = None
        for ax, g in enumerate(grid or ()):
            p = pl.program_id(ax)
            first = (p == 0) if first is None else first & (p == 0)
            last = (p == g - 1) if last is None else last & (p == g - 1)

        def each(what):
            i0, o0 = n_in, n_out
            for k, sd in enumerate(sides):
                getattr(sd, what)(ins[i0:i0 + len(sd.ins)], outs[o0:o0 + len(sd.outs)], scr[n_scr + 2 * k], scr[n_scr + 2 * k + 1])
                i0, o0 = i0 + len(sd.ins), o0 + len(sd.outs)

        if first is None:
            each("start")
        else:
            pl.when(first)(lambda: each("start"))
        body(*ins[:n_in], *outs[:n_out], *scr[:n_scr])
        if last is None:
            each("wait")
        else:
            pl.when(last)(lambda: each("wait"))

    params = dict(vmem_limit_bytes=VMEM_LIMIT_V7X)
    if sides and grid:
        params["dimension_semantics"] = ("arbitrary",) * len(grid)
    elif sem is not None:
        params["dimension_semantics"] = sem
    kw = dict(name=name, out_shape=shapes if (sides or not single) else shapes[0], compiler_params=pltpu.CompilerParams(**params))
    if aliases:
        kw["input_output_aliases"] = aliases
    final_ospecs = ospecs if (sides or not single) else ospecs[0]
    if prefetch:
        kw["grid_spec"] = pltpu.PrefetchScalarGridSpec(
            num_scalar_prefetch=prefetch, grid=grid, in_specs=in_specs, out_specs=final_ospecs, scratch_shapes=scratch)
    else:
        if grid is not None:
            kw["grid"] = grid
        kw["in_specs"] = in_specs
        kw["out_specs"] = final_ospecs
        kw["scratch_shapes"] = scratch
    call = pl.pallas_call(wrapped if sides else body, **kw)
    if not sides:
        return call

    def run(*operands):
        res = list(call(*operands, *[a for sd in sides for a in sd.ins]))
        o0 = n_out
        for sd in sides:
            sd.results = res[o0:o0 + len(sd.outs)]
            o0 += len(sd.outs)
        return res[0] if single else res[:n_out]

    return run


def _pick(n, cap, mult):
    best = None
    for d in range(mult, min(n, cap) + 1, mult):
        if n % d == 0:
            best = d
    assert best is not None, (n, cap, mult)
    return best


def _coords():
    return lax.axis_index("x"), lax.axis_index("y"), lax.axis_index("c")


def _dot(a, b, dims=None):
    if dims is None:
        return jnp.dot(a, b, preferred_element_type=F32)
    return lax.dot_general(a, b, dims, preferred_element_type=F32)


def _remote(src, dst, ssem, rsem, dev):
    return pltpu.make_async_remote_copy(src_ref=src, dst_ref=dst, send_sem=ssem, recv_sem=rsem,
                                        device_id=dev, device_id_type=MESH_ID)


def _allgather_small(v, whole_mesh, name):
    masks = list(range(1, 8)) if whole_mesh else [4, 2, 6]
    nslot = 8 if whole_mesh else 4

    def slot(px, py, pc):
        return 4 * px + 2 * py + pc if whole_mesh else 2 * px + py

    def body(v_ref, out_ref, ssem, rsem, lsem):
        x, y, c = _coords()
        mine = slot(x, y, c)
        peers = [(jnp.bitwise_xor(x, (m >> 2) & 1), jnp.bitwise_xor(y, (m >> 1) & 1), jnp.bitwise_xor(c, m & 1))
                 for m in masks]
        loc = pltpu.make_async_copy(v_ref, out_ref.at[mine], lsem)
        loc.start()
        sends = [_remote(v_ref, out_ref.at[mine], ssem.at[k], rsem.at[k], p) for k, p in enumerate(peers)]
        for cp in sends:
            cp.start()
        for k, p in enumerate(peers):
            _remote(v_ref, out_ref.at[slot(*p)], ssem.at[k], rsem.at[k], p).wait_recv()
        for cp in sends:
            cp.wait_send()
        loc.wait()

    return _pcall(body, name=name, out_shape=jax.ShapeDtypeStruct((nslot,) + v.shape, v.dtype),
                  in_specs=[VMEM], out_specs=VMEM,
                  scratch=[pltpu.SemaphoreType.DMA((len(masks),)), pltpu.SemaphoreType.DMA((len(masks),)),
                           pltpu.SemaphoreType.DMA(())])(v)


def _other_chips(x, y):
    return [(1 - x, y), (x, 1 - y), (1 - x, 1 - y)]


def _half_rows(shard_rows, core):
    h = shard_rows // 2
    return pl.ds(pl.multiple_of(core * h, 16), h)


def _later(src, dst, ssem, rsem, dev):
    return functools.partial(_remote, src, dst, ssem, rsem, dev)


def _side_from(ins, outs, nsem, pairs_of, alias=False):
    def start(*refs):
        for send, _ in pairs_of(*refs):
            send().start()

    def wait(*refs):
        pairs = pairs_of(*refs)
        for _, recv in pairs:
            recv().wait_recv()
        for send, _ in pairs:
            send().wait_send()

    return _Side(ins, outs, nsem, start, wait, alias)


def _gather_ici(ws):
    def pairs_of(w_refs, g_refs, ssem, rsem):
        x, y, c = _coords()
        me, out = 2 * x + y, []
        for a, w in enumerate(ws):
            rows = _half_rows(w.shape[0], c)
            for j, chip in enumerate(_other_chips(x, y)):
                sems = (ssem.at[3 * a + j], rsem.at[3 * a + j], (*chip, c))
                got = g_refs[a].at[2 * chip[0] + chip[1], rows]
                out.append((_later(w_refs[a].at[rows], g_refs[a].at[me, rows], *sems), _later(got, got, *sems)))
        return out

    return _side_from(ws, [jax.ShapeDtypeStruct((4,) + w.shape, w.dtype) for w in ws], 3 * len(ws), pairs_of)


def _gather_d2d(gs):
    def pairs_of(_, g_refs, ssem, rsem):
        x, y, c = _coords()
        out = []
        for a, g in enumerate(gs):
            for j, chip in enumerate(_other_chips(x, y)):
                sems = (ssem.at[3 * a + j], rsem.at[3 * a + j], (x, y, 1 - c))
                mine = g_refs[a].at[2 * chip[0] + chip[1], _half_rows(g.shape[1], c)]
                theirs = g_refs[a].at[2 * chip[0] + chip[1], _half_rows(g.shape[1], 1 - c)]
                out.append((_later(mine, mine, *sems), _later(theirs, theirs, *sems)))
        return out

    return _side_from(gs, [jax.ShapeDtypeStruct(g.shape, g.dtype) for g in gs], 3 * len(gs), pairs_of, alias=True)


def _reduce_siblings(ps):
    def pairs_of(p_refs, r_refs, ssem, rsem):
        x, y, c = _coords()
        out = []
        for a, p in enumerate(ps):
            src = p_refs[a].at[:, _half_rows(p.shape[1], 1 - c), :]
            cp = _later(src, r_refs[a], ssem.at[a], rsem.at[a], (x, y, 1 - c))
            out.append((cp, cp))
        return out

    return _side_from(ps, [jax.ShapeDtypeStruct((4, p.shape[1] // 2, p.shape[2]), p.dtype) for p in ps], len(ps), pairs_of)


def _reduce_chips(hs, dests=(0, 1, 2)):
    nd = len(dests)

    def pairs_of(h_refs, o_refs, ssem, rsem):
        x, y, c = _coords()
        chips = _other_chips(x, y)
        out = []
        for a in range(len(hs)):
            for slot, j in enumerate(dests):
                chip = chips[j]
                cp = _later(h_refs[a].at[2 * chip[0] + chip[1]], o_refs[a].at[slot], ssem.at[nd * a + slot], rsem.at[nd * a + slot],
                            (*chip, c))
                out.append((cp, cp))
        return out

    return _side_from(hs, [jax.ShapeDtypeStruct((nd,) + h.shape[1:], h.dtype) for h in hs], nd * len(hs), pairs_of)


def _share_siblings(gs):
    def pairs_of(g_refs, o_refs, ssem, rsem):
        x, y, c = _coords()
        cps = [_later(g_refs[a], o_refs[a], ssem.at[a], rsem.at[a], (x, y, 1 - c)) for a in range(len(gs))]
        return [(cp, cp) for cp in cps]

    return _side_from(gs, [jax.ShapeDtypeStruct(g.shape, g.dtype) for g in gs], len(gs), pairs_of)


def _exchange(side, name):
    def body():
        pass

    _pcall(body, name=name, out_shape=[], in_specs=[], out_specs=[], sides=[side])()
    return side.results


def _add_sibling(p, r, core, name):
    _, rr, cc = p.shape
    h = rr // 2
    th = _pick(h, max(16, (2 << 20) // (2 * cc)), 16)
    nb = h // th

    def body(c_ref, p_ref, r_ref, o_ref):
        o_ref[...] = (p_ref[...].astype(F32) + r_ref[...].astype(F32)).astype(BF16)

    return _pcall(body, name=name, out_shape=jax.ShapeDtypeStruct((4, h, cc), BF16), grid=(4, nb),
                  in_specs=[pl.BlockSpec((None, th, cc), lambda k, i, c_ref: (k, c_ref[0] * nb + i, 0)),
                            pl.BlockSpec((None, th, cc), lambda k, i, c_ref: (k, i, 0))],
                  out_specs=pl.BlockSpec((None, th, cc), lambda k, i, c_ref: (k, i, 0)),
                  sem=("parallel", "parallel"), prefetch=1)(core, p, r)


def _add_chips(hh, pieces, chip, name):
    _, h, cc = hh.shape
    th = _pick(h, max(16, (2 << 20) // (2 * cc)), 16)

    def body(k_ref, h_ref, r0_ref, r1_ref, r2_ref, o_ref):
        s = h_ref[...].astype(F32) + r0_ref[...].astype(F32)
        s = s + r1_ref[...].astype(F32)
        o_ref[...] = s + r2_ref[...].astype(F32)

    def piece(slot):
        return pl.BlockSpec((None, th, cc), lambda i, k_ref: (slot, i, 0))

    return _pcall(body, name=name, out_shape=jax.ShapeDtypeStruct((h, cc), F32), grid=(h // th,),
                  in_specs=[pl.BlockSpec((None, th, cc), lambda i, k_ref: (k_ref[0], i, 0))] + [piece(s) for _, s in pieces],
                  out_specs=pl.BlockSpec((th, cc), lambda i, k_ref: (i, 0)),
                  sem=("parallel",), prefetch=1)(chip, hh, *[a for a, _ in pieces])


def _adamw_math(w, g, m, v):
    m = ADAM_B1 * m + (1.0 - ADAM_B1) * g
    v = ADAM_B2 * v + (1.0 - ADAM_B2) * (g * g)
    m_hat = m / (1.0 - ADAM_B1 ** ADAM_STEP)
    v_hat = v / (1.0 - ADAM_B2 ** ADAM_STEP)
    delta = -ADAM_LR * (m_hat / (jnp.sqrt(v_hat) + ADAM_EPS) + ADAM_WD * w)
    return delta, m, v


def _adamw(w, g, m, v, name, sides=()):
    rr, cc = w.shape
    tr = _pick(rr, max(8, (3 << 20) // (4 * cc)), 8)

    def body(w_ref, g_ref, m_ref, v_ref, d_ref, mo_ref, vo_ref):
        d, mm, vv = _adamw_math(w_ref[...], g_ref[...], m_ref[...], v_ref[...])
        d_ref[...] = d
        mo_ref[...] = mm
        vo_ref[...] = vv

    spec = pl.BlockSpec((tr, cc), lambda i: (i, 0))
    return _pcall(body, name=name, out_shape=[jax.ShapeDtypeStruct(w.shape, F32)] * 3, grid=(rr // tr,),
                  in_specs=[spec] * 4, out_specs=[spec] * 3, sem=("parallel",), sides=sides)(w, g, m, v)


def _adamw_small(w, g8, m, v, name):
    def body(w_ref, g_ref, m_ref, v_ref, go_ref, d_ref, mo_ref, vo_ref):
        g = g_ref[0]
        for k in range(1, 8):
            g = g + g_ref[k]
        d, mm, vv = _adamw_math(w_ref[...], g, m_ref[...], v_ref[...])
        go_ref[...] = g
        d_ref[...] = d
        mo_ref[...] = mm
        vo_ref[...] = vv

    return _pcall(body, name=name, out_shape=[jax.ShapeDtypeStruct(w.shape, F32)] * 4,
                  in_specs=[VMEM] * 4, out_specs=[VMEM] * 4)(w, g8, m, v)


def _ada_fwd(c16, w_ada, b_ada, name):
    n = w_ada.shape[1]
    tn = _pick(n, 512, LANE)

    def body(c_ref, w_ref, b_ref, act_ref, mod_ref):
        cv = c_ref[...]
        act = cv * jax.nn.sigmoid(cv)
        act_ref[...] = act
        mod_ref[...] = _dot(act.astype(BF16), w_ref[...].astype(BF16)) + b_ref[...]

    return _pcall(body, name=name,
                  out_shape=[jax.ShapeDtypeStruct((16, D), F32), jax.ShapeDtypeStruct((16, n), F32)], grid=(n // tn,),
                  in_specs=[pl.BlockSpec((16, D), lambda j: (0, 0)), pl.BlockSpec((D, tn), lambda j: (0, j)),
                            pl.BlockSpec((1, tn), lambda j: (0, j))],
                  out_specs=[pl.BlockSpec((16, D), lambda j: (0, 0)), pl.BlockSpec((16, tn), lambda j: (0, j))],
                  sem=("arbitrary",))(c16, w_ada, b_ada)


def _ada_bwd(act, dmod, name, sides=()):
    n = dmod.shape[2]
    tm, tn = 256, _pick(n, 512, LANE)

    def body(a_ref, d_ref, o_ref):
        def term(b):
            return a_ref[b].astype(BF16).astype(F32) * d_ref[b].astype(BF16).astype(F32)

        acc = term(0)
        for b in range(1, 8):
            acc = acc + term(b)
        o_ref[...] = acc

    return _pcall(body, name=name, out_shape=jax.ShapeDtypeStruct((D, n), F32), grid=(D // tm, n // tn),
                  in_specs=[pl.BlockSpec((8, tm, 1), lambda i, j: (0, i, 0)), pl.BlockSpec((8, 1, tn), lambda i, j: (0, 0, j))],
                  out_specs=pl.BlockSpec((tm, tn), lambda i, j: (i, j)), sem=("parallel", "parallel"), sides=sides)(act, dmod)


def _norm_mod(x, g, sh, sc, name):
    t = x.shape[0]
    tm = _pick(t, 512, 8)

    def body(x_ref, g_ref, sh_ref, sc_ref, h_ref):
        xf = x_ref[...]
        r = lax.rsqrt(jnp.mean(xf * xf, axis=-1, keepdims=True) + EPS)
        h = (xf * r) * g_ref[...]
        h_ref[...] = (h * (1.0 + sc_ref[...]) + sh_ref[...]).astype(BF16)

    vec = pl.BlockSpec((1, D), lambda i: (0, 0))
    row = pl.BlockSpec((tm, D), lambda i: (i, 0))
    return _pcall(body, name=name, out_shape=jax.ShapeDtypeStruct((t, D), BF16), grid=(t // tm,),
                  in_specs=[row, vec, vec, vec], out_specs=row, sem=("parallel",))(x, g, sh, sc)


def _norm_mod_bwd(dh, x, dxo, g, sc, name, sides=()):
    t = x.shape[0]
    tm = _pick(t, 256, 8)

    def body(dh_ref, x_ref, dxo_ref, g_ref, sc_ref, dx_ref, dsh_ref, dsc_ref, dg_ref):
        @pl.when(pl.program_id(0) == 0)
        def _():
            dsh_ref[...] = jnp.zeros_like(dsh_ref)
            dsc_ref[...] = jnp.zeros_like(dsc_ref)
            dg_ref[...] = jnp.zeros_like(dg_ref)

        xf, dh_ = x_ref[...], dh_ref[...]
        r = lax.rsqrt(jnp.mean(xf * xf, axis=-1, keepdims=True) + EPS)
        xhat = xf * r
        dsh_ref[...] += jnp.sum(dh_, axis=0, keepdims=True)
        dsc_ref[...] += jnp.sum(dh_ * (xhat * g_ref[...]), axis=0, keepdims=True)
        tt = dh_ * (1.0 + sc_ref[...])
        dg_ref[...] += jnp.sum(tt * xhat, axis=0, keepdims=True)
        dxh = tt * g_ref[...]
        dx_ref[...] = r * (dxh - xhat * jnp.mean(dxh * xhat, axis=-1, keepdims=True)) + dxo_ref[...]

    vec = pl.BlockSpec((1, D), lambda i: (0, 0))
    row = pl.BlockSpec((tm, D), lambda i: (i, 0))
    vshape = jax.ShapeDtypeStruct((1, D), F32)
    return _pcall(body, name=name, out_shape=[jax.ShapeDtypeStruct((t, D), F32), vshape, vshape, vshape], grid=(t // tm,),
                  in_specs=[row, row, row, vec, vec], out_specs=[row, vec, vec, vec], sem=("arbitrary",), sides=sides)(dh, x, dxo, g, sc)


def _gate_bwd(dxo, yy, gate, name):
    t = dxo.shape[0]
    tm = _pick(t, 512, 8)

    def body(dx_ref, y_ref, g_ref, dy_ref, dg_ref):
        @pl.when(pl.program_id(0) == 0)
        def _():
            dg_ref[...] = jnp.zeros_like(dg_ref)

        dx = dx_ref[...]
        dy_ref[...] = (dx * g_ref[...]).astype(BF16)
        dg_ref[...] += jnp.sum(dx * y_ref[...], axis=0, keepdims=True)

    vec = pl.BlockSpec((1, D), lambda i: (0, 0))
    row = pl.BlockSpec((tm, D), lambda i: (i, 0))
    return _pcall(body, name=name, out_shape=[jax.ShapeDtypeStruct((t, D), BF16), jax.ShapeDtypeStruct((1, D), F32)],
                  grid=(t // tm,), in_specs=[row, row, vec], out_specs=[row, vec], sem=("arbitrary",))(dxo, yy, gate)


def _final_loss(x, g, tgt, name):
    t = x.shape[0]
    tm = _pick(t, 256, 8)

    def body(x_ref, g_ref, t_ref, loss_ref, dx_ref, dg_ref):
        @pl.when(pl.program_id(0) == 0)
        def _():
            loss_ref[...] = jnp.zeros_like(loss_ref)
            dg_ref[...] = jnp.zeros_like(dg_ref)

        xf = x_ref[...]
        r = lax.rsqrt(jnp.mean(xf * xf, axis=-1, keepdims=True) + EPS)
        xhat = xf * r
        e = xhat * g_ref[...] - t_ref[...]
        per_tok = jnp.mean(e * e, axis=-1, keepdims=True)
        loss_ref[...] += 0.5 * jnp.sum(per_tok, axis=0, keepdims=True)
        dy = e * (1.0 / D)
        dg_ref[...] += jnp.sum(dy * xhat, axis=0, keepdims=True)
        dxh = dy * g_ref[...]
        dx_ref[...] = r * (dxh - xhat * jnp.mean(dxh * xhat, axis=-1, keepdims=True))

    vec = pl.BlockSpec((1, D), lambda i: (0, 0))
    row = pl.BlockSpec((tm, D), lambda i: (i, 0))
    return _pcall(body, name=name,
                  out_shape=[jax.ShapeDtypeStruct((1, LANE), F32), jax.ShapeDtypeStruct((t, D), F32),
                             jax.ShapeDtypeStruct((1, D), F32)],
                  grid=(t // tm,), in_specs=[row, vec, row],
                  out_specs=[pl.BlockSpec((1, LANE), lambda i: (0, 0)), row, vec], sem=("arbitrary",))(x, g, tgt)


def _mm(a, b, out_dtype, name):
    m, k = a.shape
    n = b.shape[1]
    tm = _pick(m, 1024, 8)
    tn = _pick(n, 1408, LANE)

    def body(a_ref, b_ref, o_ref):
        o_ref[...] = _dot(a_ref[...], b_ref[...]).astype(out_dtype)

    return _pcall(body, name=name, out_shape=jax.ShapeDtypeStruct((m, n), out_dtype), grid=(m // tm, n // tn),
                  in_specs=[pl.BlockSpec((tm, k), lambda i, j: (i, 0)), pl.BlockSpec((k, tn), lambda i, j: (0, j))],
                  out_specs=pl.BlockSpec((tm, tn), lambda i, j: (i, j)), sem=("parallel", "parallel"))(a, b)


def _mm_resid(a, b, resid, gate, name, sides=()):
    m, k = a.shape
    n = b.shape[1]
    tm, tn, tk = _pick(m, 1024, 8), _pick(n, 1024, LANE), _pick(k, 1408, LANE)
    nk = k // tk

    def body(a_ref, b_ref, r_ref, g_ref, y_ref, o_ref, acc):
        kk = pl.program_id(2)

        @pl.when(kk == 0)
        def _():
            acc[...] = jnp.zeros_like(acc)

        acc[...] += _dot(a_ref[...], b_ref[...])

        @pl.when(kk == nk - 1)
        def _():
            y_ref[...] = acc[...].astype(BF16)
            o_ref[...] = r_ref[...] + g_ref[...] * acc[...]

    blk = pl.BlockSpec((tm, tn), lambda i, j, kk: (i, j))
    return _pcall(body, name=name, out_shape=[jax.ShapeDtypeStruct((m, n), BF16), jax.ShapeDtypeStruct((m, n), F32)],
                  grid=(m // tm, n // tn, nk),
                  in_specs=[pl.BlockSpec((tm, tk), lambda i, j, kk: (i, kk)), pl.BlockSpec((tk, tn), lambda i, j, kk: (kk, j)),
                            blk, pl.BlockSpec((1, tn), lambda i, j, kk: (0, j))],
                  out_specs=[blk, blk], scratch=[pltpu.VMEM((tm, tn), F32)],
                  sem=("parallel", "parallel", "arbitrary"), sides=sides)(a, b, resid, gate)


def _mm_nt(a, b, out_dtype, name, sides=()):
    m, k = a.shape
    n = b.shape[0]
    tm, tn, tk = _pick(m, 1024, 8), _pick(n, 1024, LANE), _pick(k, 2816, LANE)
    nk = k // tk

    def body(a_ref, b_ref, o_ref, acc):
        kk = pl.program_id(2)

        @pl.when(kk == 0)
        def _():
            acc[...] = jnp.zeros_like(acc)

        acc[...] += _dot(a_ref[...], b_ref[...], NT)

        @pl.when(kk == nk - 1)
        def _():
            o_ref[...] = acc[...].astype(out_dtype)

    return _pcall(body, name=name, out_shape=jax.ShapeDtypeStruct((m, n), out_dtype), grid=(m // tm, n // tn, nk),
                  in_specs=[pl.BlockSpec((tm, tk), lambda i, j, kk: (i, kk)), pl.BlockSpec((tn, tk), lambda i, j, kk: (j, kk))],
                  out_specs=pl.BlockSpec((tm, tn), lambda i, j, kk: (i, j)), scratch=[pltpu.VMEM((tm, tn), F32)],
                  sem=("parallel", "parallel", "arbitrary"), sides=sides)(a, b)


def _mm_tn(a, b, out_dtype, tm_cap, name, sides=()):
    t, m = a.shape
    n = b.shape[1]
    tm, tn, tk = _pick(m, tm_cap, LANE), _pick(n, 1408, LANE), _pick(t, 2048, 16)
    nk = t // tk

    def body(a_ref, b_ref, o_ref, acc):
        kk = pl.program_id(2)

        @pl.when(kk == 0)
        def _():
            acc[...] = jnp.zeros_like(acc)

        acc[...] += _dot(a_ref[...], b_ref[...], TN)

        @pl.when(kk == nk - 1)
        def _():
            o_ref[...] = acc[...].astype(out_dtype)

    return _pcall(body, name=name, out_shape=jax.ShapeDtypeStruct((m, n), out_dtype), grid=(m // tm, n // tn, nk),
                  in_specs=[pl.BlockSpec((tk, tm), lambda i, j, kk: (kk, i)), pl.BlockSpec((tk, tn), lambda i, j, kk: (kk, j))],
                  out_specs=pl.BlockSpec((tm, tn), lambda i, j, kk: (i, j)), scratch=[pltpu.VMEM((tm, tn), F32)],
                  sem=("parallel", "parallel", "arbitrary"), sides=sides)(a, b)


def _ffn_up(h, wg, name, sides=()):
    t = h.shape[0]
    fp = wg.shape[2]
    tm, tn = _pick(t, 1024, 8), 256
    nn = fp // tn

    def body(h_ref, wa_ref, wb_ref, a_ref, b_ref, u_ref):
        hh = h_ref[...]
        a = _dot(hh, wa_ref[...])
        b = _dot(hh, wb_ref[...])
        a_ref[...] = a.astype(BF16)
        b_ref[...] = b.astype(BF16)
        u_ref[...] = (a * jax.nn.sigmoid(a) * b).astype(BF16)

    out = pl.BlockSpec((tm, tn), lambda i, j, n: (i, j * nn + n))
    return _pcall(body, name=name, out_shape=[jax.ShapeDtypeStruct((t, 2 * fp), BF16)] * 3, grid=(t // tm, 2, nn),
                  in_specs=[pl.BlockSpec((tm, D), lambda i, j, n: (i, 0)),
                            pl.BlockSpec((None, D, tn), lambda i, j, n: (j, 0, n)),
                            pl.BlockSpec((None, D, tn), lambda i, j, n: (j + 2, 0, n))],
                  out_specs=[out, out, out], sem=("parallel", "parallel", "parallel"), sides=sides)(h, wg, wg)


def _ffn_dab(dy, wo, a, b, name, sides=()):
    t = dy.shape[0]
    f2 = wo.shape[0]
    tm, tn = _pick(t, 2048, 8), 256

    def body(dy_ref, w_ref, a_ref, b_ref, o_ref):
        du = _dot(dy_ref[...], w_ref[...], NT)
        av, bv = a_ref[...].astype(F32), b_ref[...].astype(F32)
        sg = jax.nn.sigmoid(av)
        o_ref[0] = (du * bv * (sg * (1.0 + av * (1.0 - sg)))).astype(BF16)
        o_ref[1] = (du * (av * sg)).astype(BF16)

    blk = pl.BlockSpec((tm, tn), lambda i, n: (i, n))
    return _pcall(body, name=name, out_shape=jax.ShapeDtypeStruct((2, t, f2), BF16), grid=(t // tm, f2 // tn),
                  in_specs=[pl.BlockSpec((tm, D), lambda i, n: (i, 0)), pl.BlockSpec((tn, D), lambda i, n: (n, 0)), blk, blk],
                  out_specs=pl.BlockSpec((2, tm, tn), lambda i, n: (0, i, n)), sem=("parallel", "parallel"), sides=sides)(dy, wo, a, b)


def _ffn_dwin(h, dab, name, sides=()):
    t = h.shape[0]
    fp = dab.shape[2] // 2
    tm, tn, tk = _pick(fp, 1408, LANE), 1024, _pick(t, 2048, 16)
    nm, nk = fp // tm, t // tk

    def body(d_ref, h_ref, o_ref, acc):
        kk = pl.program_id(3)

        @pl.when(kk == 0)
        def _():
            acc[...] = jnp.zeros_like(acc)

        acc[...] += _dot(d_ref[...], h_ref[...], TN)

        @pl.when(kk == nk - 1)
        def _():
            o_ref[...] = acc[...].astype(BF16)

    return _pcall(body, name=name, out_shape=jax.ShapeDtypeStruct((4, fp, D), BF16), grid=(4, nm, D // tn, nk),
                  in_specs=[pl.BlockSpec((None, tk, tm), lambda u, i, n, kk: (u // 2, kk, (u % 2) * nm + i)),
                            pl.BlockSpec((tk, tn), lambda u, i, n, kk: (kk, n))],
                  out_specs=pl.BlockSpec((None, tm, tn), lambda u, i, n, kk: (u, i, n)),
                  scratch=[pltpu.VMEM((tm, tn), F32)],
                  sem=("parallel", "parallel", "parallel", "arbitrary"), sides=sides)(dab, h)


def _ffn_dh(dab, wg, name, sides=()):
    t = dab.shape[1]
    fp = wg.shape[2]
    tm, tn, tk = _pick(t, 1024, 8), 1024, _pick(fp, 2816, LANE)
    nkk = fp // tk

    def body(d_ref, w_ref, o_ref, acc):
        u, kk = pl.program_id(2), pl.program_id(3)

        @pl.when((u == 0) & (kk == 0))
        def _():
            acc[...] = jnp.zeros_like(acc)

        acc[...] += _dot(d_ref[...], w_ref[...], NT)

        @pl.when((u == 3) & (kk == nkk - 1))
        def _():
            o_ref[...] = acc[...]

    return _pcall(body, name=name, out_shape=jax.ShapeDtypeStruct((t, D), F32), grid=(t // tm, D // tn, 4, nkk),
                  in_specs=[pl.BlockSpec((None, tm, tk), lambda i, j, u, kk: (u // 2, i, (u % 2) * nkk + kk)),
                            pl.BlockSpec((None, tn, tk), lambda i, j, u, kk: (u, j, kk))],
                  out_specs=pl.BlockSpec((tm, tn), lambda i, j, u, kk: (i, j)), scratch=[pltpu.VMEM((tm, tn), F32)],
                  sem=("parallel", "parallel", "arbitrary", "arbitrary"), sides=sides)(dab, wg)


def _split3(v):
    hi = v.astype(BF16)
    r1 = v - hi.astype(F32)
    mid = r1.astype(BF16)
    lo = (r1 - mid.astype(F32)).astype(BF16)
    return hi, mid, lo


def _tri_sum(tri, v):
    hi, mid, lo = _split3(v)
    return (_dot(tri, hi) + _dot(tri, mid)) + _dot(tri, lo)


def _forget_fwd(proj, b_pad, name):
    t = proj.shape[0]
    tb = _pick(t, 256, 8)
    col = D_PROJ // LANE

    def body(f_ref, b_ref, o_ref, carry):
        @pl.when(pl.program_id(0) == 0)
        def _():
            carry[...] = jnp.zeros_like(carry)

        z = f_ref[...] + b_ref[...]
        lf = jnp.minimum(z, 0.0) - jnp.log(1.0 + jnp.exp(-jnp.abs(z)))
        r = lax.broadcasted_iota(jnp.int32, (tb, tb), 0)
        cidx = lax.broadcasted_iota(jnp.int32, (tb, tb), 1)
        tri = (r >= cidx).astype(BF16)
        o_ref[...] = _tri_sum(tri, lf) + carry[...]
        carry[...] += jnp.sum(lf, axis=0, keepdims=True)

    return _pcall(body, name=name, out_shape=jax.ShapeDtypeStruct((t, LANE), F32), grid=(t // tb,),
                  in_specs=[pl.BlockSpec((tb, LANE), lambda i: (i, col)), pl.BlockSpec((1, LANE), lambda i: (0, 0))],
                  out_specs=pl.BlockSpec((tb, LANE), lambda i: (i, 0)), scratch=[pltpu.VMEM((1, LANE), F32)],
                  sem=("arbitrary",))(proj, b_pad)


def _forget_bwd(d_cum, proj, b_pad, name):
    t = proj.shape[0]
    tb = _pick(t, 256, 8)
    nb = t // tb
    col = D_PROJ // LANE

    def body(d_ref, f_ref, b_ref, o_ref, db_ref, carry):
        @pl.when(pl.program_id(0) == 0)
        def _():
            carry[...] = jnp.zeros_like(carry)
            db_ref[...] = jnp.zeros_like(db_ref)

        dc = d_ref[...]
        r = lax.broadcasted_iota(jnp.int32, (tb, tb), 0)
        cidx = lax.broadcasted_iota(jnp.int32, (tb, tb), 1)
        tri = (r <= cidx).astype(BF16)
        dlf = _tri_sum(tri, dc) + carry[...]
        carry[...] += jnp.sum(dc, axis=0, keepdims=True)
        z = f_ref[...] + b_ref[...]
        lane = lax.broadcasted_iota(jnp.int32, (tb, LANE), 1)
        dz = jnp.where(lane < N_HEADS, dlf * jax.nn.sigmoid(-z), 0.0)
        o_ref[...] = dz.astype(BF16)
        db_ref[...] += jnp.sum(dz, axis=0, keepdims=True)

    return _pcall(body, name=name, out_shape=[jax.ShapeDtypeStruct((t, LANE), BF16), jax.ShapeDtypeStruct((1, LANE), F32)],
                  grid=(nb,),
                  in_specs=[pl.BlockSpec((tb, LANE), lambda i: (nb - 1 - i, 0)),
                            pl.BlockSpec((tb, LANE), lambda i: (nb - 1 - i, col)),
                            pl.BlockSpec((1, LANE), lambda i: (0, 0))],
                  out_specs=[pl.BlockSpec((tb, LANE), lambda i: (nb - 1 - i, 0)), pl.BlockSpec((1, LANE), lambda i: (0, 0))],
                  scratch=[pltpu.VMEM((1, LANE), F32)], sem=("arbitrary",))(d_cum, proj, b_pad)


def _head_norm(v, g):
    r = lax.rsqrt(jnp.mean(v * v, axis=-1, keepdims=True) + EPS)
    return v * r, r


def _qkv_prep(proj, qg, kg, name):
    t = proj.shape[0]
    tm = _pick(t, 1024, 8)

    def body(q_ref, k_ref, v_ref, qg_ref, kg_ref, qo_ref, ko_ref, vo_ref):
        qo_ref[...] = (_head_norm(q_ref[...], None)[0] * qg_ref[...]).astype(BF16)
        ko_ref[...] = (_head_norm(k_ref[...], None)[0] * kg_ref[...]).astype(BF16)
        vo_ref[...] = v_ref[...].astype(BF16)

    def blk(off):
        return pl.BlockSpec((tm, HEAD_DIM), lambda i, h: (i, off + h))

    vec = pl.BlockSpec((1, HEAD_DIM), lambda i, h: (0, 0))
    return _pcall(body, name=name, out_shape=[jax.ShapeDtypeStruct((t, D_ATTN), BF16)] * 3, grid=(t // tm, N_HEADS),
                  in_specs=[blk(0), blk(N_HEADS), blk(2 * N_HEADS), vec, vec], out_specs=[blk(0)] * 3,
                  sem=("parallel", "parallel"))(proj, proj, proj, qg, kg)


def _qk_norm_bwd(dqn, dkn, proj, qg, kg, name):
    t = proj.shape[0]
    tm = _pick(t, 1024, 8)

    def one(d_ref, v_ref, g_ref, o_ref, dg_ref):
        xhat, r = _head_norm(v_ref[...], None)
        d = d_ref[...]
        dg_ref[...] += jnp.sum(d * xhat, axis=0, keepdims=True)
        dxh = d * g_ref[...]
        o_ref[...] = (r * (dxh - xhat * jnp.mean(dxh * xhat, axis=-1, keepdims=True))).astype(BF16)

    def body(dq_ref, dk_ref, q_ref, k_ref, qg_ref, kg_ref, qo_ref, ko_ref, dqg_ref, dkg_ref):
        @pl.when((pl.program_id(0) == 0) & (pl.program_id(1) == 0))
        def _():
            dqg_ref[...] = jnp.zeros_like(dqg_ref)
            dkg_ref[...] = jnp.zeros_like(dkg_ref)

        one(dq_ref, q_ref, qg_ref, qo_ref, dqg_ref)
        one(dk_ref, k_ref, kg_ref, ko_ref, dkg_ref)

    def blk(off):
        return pl.BlockSpec((tm, HEAD_DIM), lambda i, h: (i, off + h))

    vec = pl.BlockSpec((1, HEAD_DIM), lambda i, h: (0, 0))
    vshape = jax.ShapeDtypeStruct((1, HEAD_DIM), F32)
    return _pcall(body, name=name, out_shape=[jax.ShapeDtypeStruct((t, D_ATTN), BF16)] * 2 + [vshape, vshape],
                  grid=(t // tm, N_HEADS),
                  in_specs=[blk(0), blk(0), blk(0), blk(N_HEADS), vec, vec], out_specs=[blk(0), blk(0), vec, vec],
                  sem=("arbitrary", "arbitrary"))(dqn, dkn, proj, proj, qg, kg)


ATTN_SCALE = HEAD_DIM ** -0.5


def _logits(q, k, fq, fk, diag, r0, tq, tk):
    s = _dot(q, k, NT) * ATTN_SCALE + (fq - fk)
    if diag:
        r = r0 + lax.broadcasted_iota(jnp.int32, (tq, tk), 0)
        cidx = lax.broadcasted_iota(jnp.int32, (tq, tk), 1)
        s = jnp.where(r >= cidx, s, -jnp.inf)
    return s


def _head_cols(hp):
    return [(hh, slice(hh * HEAD_DIM, (hh + 1) * HEAD_DIM)) for hh in range(hp)]


def _tri(tt, n, by_row):
    if by_row:
        i = sum((tt >= k * (k + 1) // 2).astype(jnp.int32) for k in range(1, n))
        return i, tt - i * (i + 1) // 2
    j = sum((tt >= k * n - k * (k - 1) // 2).astype(jnp.int32) for k in range(1, n))
    return j + tt - (j * n - j * (j - 1) // 2), j


def _attn_fwd(q, k, v, fq, fk, name, sides=()):
    t = q.shape[0]
    tq = tk = _pick(t, ATTN_BLOCK, LANE)
    nk = t // tk

    hp, rc = ATTN_HEADS[0], tq // ATTN_ROW_CHUNKS[0]

    def body(q_ref, k_ref, v_ref, fq_ref, fk_ref, o_ref, lse_ref, m_s, l_s, acc):
        i, j = _tri(pl.program_id(1), nk, True)

        @pl.when(j == 0)
        def _():
            m_s[...] = jnp.full_like(m_s, -jnp.inf)
            l_s[...] = jnp.zeros_like(l_s)
            acc[...] = jnp.zeros_like(acc)

        def step(diag):
            for hh, cols in _head_cols(hp):
                for r0 in range(0, tq, rc):
                    rows = slice(r0, r0 + rc)
                    s = _logits(q_ref[rows, cols], k_ref[:, cols], fq_ref[hh, rows, :], fk_ref[hh], diag, r0, rc, tk)
                    m_new = jnp.maximum(m_s[hh, rows, :], jnp.max(s, axis=-1, keepdims=True))
                    alpha = jnp.exp(m_s[hh, rows, :] - m_new)
                    p = jnp.exp(s - m_new)
                    l_s[hh, rows, :] = alpha * l_s[hh, rows, :] + jnp.sum(p, axis=-1, keepdims=True)
                    acc[rows, cols] = alpha * acc[rows, cols] + _dot(p.astype(BF16), v_ref[:, cols])
                    m_s[hh, rows, :] = m_new

        @pl.when(j < i)
        def _():
            step(False)

        @pl.when(j == i)
        def _():
            step(True)

        @pl.when(j == i)
        def _():
            for hh, cols in _head_cols(hp):
                o_ref[:, cols] = (acc[:, cols] / l_s[hh]).astype(BF16)
                lse_ref[hh] = m_s[hh] + jnp.log(l_s[hh])

    qb = pl.BlockSpec((tq, hp * HEAD_DIM), lambda h, tt: (_tri(tt, nk, True)[0], h))
    kb = pl.BlockSpec((tk, hp * HEAD_DIM), lambda h, tt: (_tri(tt, nk, True)[1], h))
    col = pl.BlockSpec((hp, tq, 1), lambda h, tt: (h, _tri(tt, nk, True)[0], 0))
    return _pcall(body, name=name,
                  out_shape=[jax.ShapeDtypeStruct((t, D_ATTN), BF16), jax.ShapeDtypeStruct((N_HEADS, t, 1), F32)],
                  grid=(N_HEADS // hp, nk * (nk + 1) // 2),
                  in_specs=[qb, kb, kb, col, pl.BlockSpec((hp, 1, tk), lambda h, tt: (h, 0, _tri(tt, nk, True)[1]))],
                  out_specs=[qb, col],
                  scratch=[pltpu.VMEM((hp, tq, 1), F32), pltpu.VMEM((hp, tq, 1), F32), pltpu.VMEM((tq, hp * HEAD_DIM), F32)],
                  sem=("parallel", "arbitrary"), sides=sides)(q, k, v, fq, fk)


def _attn_bwd_q(q, k, v, o, do, lse, fq, fk, name, sides=()):
    t = q.shape[0]
    tq = tk = _pick(t, ATTN_BLOCK, LANE)
    nk = t // tk

    hp, rc = ATTN_HEADS[1], tq // ATTN_ROW_CHUNKS[1]

    def body(q_ref, k_ref, v_ref, o_ref, do_ref, lse_ref, fq_ref, fk_ref, dq_ref, dl_ref, dfq_ref, acc, dl_s, df_s):
        i, j = _tri(pl.program_id(1), nk, True)

        @pl.when(j == 0)
        def _():
            acc[...] = jnp.zeros_like(acc)
            df_s[...] = jnp.zeros_like(df_s)
            for hh, cols in _head_cols(hp):
                dl_s[hh] = jnp.sum(do_ref[:, cols].astype(F32) * o_ref[:, cols].astype(F32), axis=-1, keepdims=True)

        def step(diag):
            for hh, cols in _head_cols(hp):
                for r0 in range(0, tq, rc):
                    rows = slice(r0, r0 + rc)
                    s = _logits(q_ref[rows, cols], k_ref[:, cols], fq_ref[hh, rows, :], fk_ref[hh], diag, r0, rc, tk)
                    p = jnp.exp(s - lse_ref[hh, rows, :])
                    dp = _dot(do_ref[rows, cols], v_ref[:, cols], NT)
                    ds = p * (dp - dl_s[hh, rows, :])
                    df_s[hh, rows, :] += jnp.sum(ds, axis=-1, keepdims=True)
                    acc[rows, cols] += _dot(ds.astype(BF16), k_ref[:, cols])

        @pl.when(j < i)
        def _():
            step(False)

        @pl.when(j == i)
        def _():
            step(True)

        @pl.when(j == i)
        def _():
            dq_ref[...] = acc[...] * ATTN_SCALE
            dl_ref[...] = dl_s[...]
            dfq_ref[...] = df_s[...]

    qb = pl.BlockSpec((tq, hp * HEAD_DIM), lambda h, tt: (_tri(tt, nk, True)[0], h))
    kb = pl.BlockSpec((tk, hp * HEAD_DIM), lambda h, tt: (_tri(tt, nk, True)[1], h))
    col = pl.BlockSpec((hp, tq, 1), lambda h, tt: (h, _tri(tt, nk, True)[0], 0))
    cshape = jax.ShapeDtypeStruct((N_HEADS, t, 1), F32)
    return _pcall(body, name=name, out_shape=[jax.ShapeDtypeStruct((t, D_ATTN), F32), cshape, cshape],
                  grid=(N_HEADS // hp, nk * (nk + 1) // 2),
                  in_specs=[qb, kb, kb, qb, qb, col, col, pl.BlockSpec((hp, 1, tk), lambda h, tt: (h, 0, _tri(tt, nk, True)[1]))],
                  out_specs=[qb, col, col],
                  scratch=[pltpu.VMEM((tq, hp * HEAD_DIM), F32), pltpu.VMEM((hp, tq, 1), F32), pltpu.VMEM((hp, tq, 1), F32)],
                  sem=("parallel", "arbitrary"), sides=sides)(q, k, v, o, do, lse, fq, fk)


def _attn_bwd_kv(q, k, v, do, lse, delta, fq, fk, name, sides=()):
    t = q.shape[0]
    tq = tk = _pick(t, ATTN_BLOCK, LANE)
    nq = t // tq

    hp, rc = ATTN_HEADS[2], tq // ATTN_ROW_CHUNKS[2]

    def body(q_ref, k_ref, v_ref, do_ref, lse_ref, dl_ref, fq_ref, fk_ref, dk_ref, dv_ref, dfk_ref, dk_s, dv_s, df_s):
        i, j = _tri(pl.program_id(1), nq, False)

        @pl.when(i == j)
        def _():
            dk_s[...] = jnp.zeros_like(dk_s)
            dv_s[...] = jnp.zeros_like(dv_s)
            df_s[...] = jnp.zeros_like(df_s)

        def step(diag):
            for hh, cols in _head_cols(hp):
                for r0 in range(0, tq, rc):
                    rows = slice(r0, r0 + rc)
                    s = _logits(q_ref[rows, cols], k_ref[:, cols], fq_ref[hh, rows, :], fk_ref[hh], diag, r0, rc, tk)
                    p = jnp.exp(s - lse_ref[hh, rows, :])
                    dv_s[:, cols] += _dot(p.astype(BF16), do_ref[rows, cols], TN)
                    dp = _dot(do_ref[rows, cols], v_ref[:, cols], NT)
                    ds = p * (dp - dl_ref[hh, rows, :])
                    df_s[hh] -= jnp.sum(ds, axis=0, keepdims=True)
                    dk_s[:, cols] += _dot(ds.astype(BF16), q_ref[rows, cols], TN)

        @pl.when(i > j)
        def _():
            step(False)

        @pl.when(i == j)
        def _():
            step(True)

        @pl.when(i == nq - 1)
        def _():
            dk_ref[...] = dk_s[...] * ATTN_SCALE
            dv_ref[...] = dv_s[...].astype(BF16)
            dfk_ref[...] = df_s[...]

    qb = pl.BlockSpec((tq, hp * HEAD_DIM), lambda h, tt: (_tri(tt, nq, False)[0], h))
    kb = pl.BlockSpec((tk, hp * HEAD_DIM), lambda h, tt: (_tri(tt, nq, False)[1], h))
    col = pl.BlockSpec((hp, tq, 1), lambda h, tt: (h, _tri(tt, nq, False)[0], 0))
    row = pl.BlockSpec((hp, 1, tk), lambda h, tt: (h, 0, _tri(tt, nq, False)[1]))
    return _pcall(body, name=name,
                  out_shape=[jax.ShapeDtypeStruct((t, D_ATTN), F32), jax.ShapeDtypeStruct((t, D_ATTN), BF16),
                             jax.ShapeDtypeStruct((N_HEADS, 1, t), F32)],
                  grid=(N_HEADS // hp, nq * (nq + 1) // 2),
                  in_specs=[qb, kb, kb, qb, col, col, col, row], out_specs=[kb, kb, row],
                  scratch=[pltpu.VMEM((tk, hp * HEAD_DIM), F32), pltpu.VMEM((tk, hp * HEAD_DIM), F32), pltpu.VMEM((hp, 1, tk), F32)],
                  sem=("parallel", "arbitrary"), sides=sides)(q, k, v, do, lse, delta, fq, fk)


def _window_sum(v, w, back):
    n = v.shape[0]
    k = 1
    while k < w:
        v = v + pltpu.roll(v, k if back else n - k, axis=0)
        k *= 2
    return v


def _pool_fwd(proj, pw, ps, name):
    t = proj.shape[0]
    tm = _pick(t, POOL_BLOCK, HALO)
    col = 3 * D_ATTN // D_POOL

    def body(u_ref, prev_ref, pw_ref, ps_ref, pooled_ref, out_ref):
        i = pl.program_id(0)
        prev = jnp.where(i > 0, prev_ref[...], 0.0)
        ext = jnp.concatenate([prev, u_ref[...]], axis=0)
        pos = i * tm + lax.broadcasted_iota(jnp.int32, (tm, 1), 0)
        for g, w in enumerate(POOL_WINDOWS):
            cols = slice(g * GROUP_DIM, (g + 1) * GROUP_DIM)
            xg = ext[:, cols]
            sw = _window_sum(xg, w, True)[HALO:, :]
            cnt = jnp.minimum(pos + 1, w).astype(F32)
            pooled = (sw / cnt - xg[HALO:, :]).astype(BF16)
            pooled_ref[:, cols] = pooled
            out_ref[:, cols] = (_dot(pooled, pw_ref[g]) * ps_ref[:, cols]).astype(BF16)

    row = pl.BlockSpec((tm, D_POOL), lambda i: (i, 0))
    return _pcall(body, name=name, out_shape=[jax.ShapeDtypeStruct((t, D_POOL), BF16)] * 2, grid=(t // tm,),
                  in_specs=[pl.BlockSpec((tm, D_POOL), lambda i: (i, col)),
                            pl.BlockSpec((HALO, D_POOL), lambda i: (jnp.maximum(i * (tm // HALO) - 1, 0), col)),
                            pl.BlockSpec((len(POOL_WINDOWS), GROUP_DIM, GROUP_DIM), lambda i: (0, 0, 0)),
                            pl.BlockSpec((1, D_POOL), lambda i: (0, 0))],
                  out_specs=[row, row], sem=("parallel",))(proj, proj, pw, ps)


def _pool_bwd(dout, pooled, pw, ps, name):
    t = pooled.shape[0]
    tm = _pick(t, POOL_BLOCK, HALO)
    nb = t // tm
    ng = len(POOL_WINDOWS)

    def body(d_ref, nxt_ref, p_ref, pw_ref, ps_ref, du_ref, dpw_ref, dps_ref):
        i = pl.program_id(0)

        @pl.when(i == 0)
        def _():
            dpw_ref[...] = jnp.zeros_like(dpw_ref)
            dps_ref[...] = jnp.zeros_like(dps_ref)

        nxt = jnp.where(i < nb - 1, nxt_ref[...].astype(F32), 0.0)
        ext = jnp.concatenate([d_ref[...].astype(F32), nxt], axis=0)
        pos = i * tm + lax.broadcasted_iota(jnp.int32, (tm + HALO, 1), 0)
        for g, w in enumerate(POOL_WINDOWS):
            cols = slice(g * GROUP_DIM, (g + 1) * GROUP_DIM)
            pooled_g = p_ref[:, cols]
            dg = ext[:, cols]
            pm = _dot(pooled_g, pw_ref[g])
            dps_ref[:, cols] += jnp.sum(dg[:tm, :] * pm, axis=0, keepdims=True)
            dpm = (dg * ps_ref[:, cols]).astype(BF16)
            dpw_ref[g] += _dot(pooled_g, dpm[:tm, :], TN)
            dpooled = _dot(dpm, pw_ref[g], NT)
            cnt = jnp.minimum(pos + 1, w).astype(F32)
            fwd = _window_sum(dpooled / cnt, w, False)
            du_ref[:, cols] = (fwd[:tm, :] - dpooled[:tm, :]).astype(BF16)

    row = pl.BlockSpec((tm, D_POOL), lambda i: (i, 0))
    return _pcall(body, name=name,
                  out_shape=[jax.ShapeDtypeStruct((t, D_POOL), BF16), jax.ShapeDtypeStruct((ng, GROUP_DIM, GROUP_DIM), F32),
                             jax.ShapeDtypeStruct((1, D_POOL), F32)],
                  grid=(nb,),
                  in_specs=[pl.BlockSpec((tm, D_POOL), lambda i: (i, 1)),
                            pl.BlockSpec((HALO, D_POOL), lambda i: (jnp.minimum((i + 1) * (tm // HALO), t // HALO - 1), 1)),
                            row, pl.BlockSpec((ng, GROUP_DIM, GROUP_DIM), lambda i: (0, 0, 0)),
                            pl.BlockSpec((1, D_POOL), lambda i: (0, 0))],
                  out_specs=[row, pl.BlockSpec((ng, GROUP_DIM, GROUP_DIM), lambda i: (0, 0, 0)),
                             pl.BlockSpec((1, D_POOL), lambda i: (0, 0))],
                  sem=("arbitrary",))(dout, dout, pooled, pw, ps)


def _pad_cols(w, n):
    return jnp.pad(w, ((0, 0), (0, n - w.shape[1])))


def kernel(x, c, w_ada, b_ada, ffn1_norm_g, ffn1_w_in, ffn1_w_out, mix_norm_g, w_in, b_forget, q_norm_g, k_norm_g, pool_w, pool_scale, w_out, ffn2_norm_g, ffn2_w_in, ffn2_w_out, final_norm_g, loss_target, m_w_ada, m_b_ada, m_ffn1_norm_g, m_ffn1_w_in, m_ffn1_w_out, m_mix_norm_g, m_w_in, m_b_forget, m_q_norm_g, m_k_norm_g, m_pool_w, m_pool_scale, m_w_out, m_ffn2_norm_g, m_ffn2_w_in, m_ffn2_w_out, m_final_norm_g, v_w_ada, v_b_ada, v_ffn1_norm_g, v_ffn1_w_in, v_ffn1_w_out, v_mix_norm_g, v_w_in, v_b_forget, v_q_norm_g, v_k_norm_g, v_pool_w, v_pool_scale, v_w_out, v_ffn2_norm_g, v_ffn2_w_in, v_ffn2_w_out, v_final_norm_g):
    ax, ay, ac = _coords()
    chip = 2 * ax + ay
    me = 2 * chip + ac
    chip_arr = jnp.reshape(chip, (1,)).astype(jnp.int32)
    core_arr = jnp.reshape(ac, (1,)).astype(jnp.int32)

    t = x.shape[1]
    xs = x.reshape(t, D)
    tgt = loss_target.reshape(t, D)
    hu = ffn1_w_out.shape[1]
    hup = -(-hu // LANE) * LANE
    ws_in = w_in.shape[2]
    ws_in_pad = -(-ws_in // LANE) * LANE
    n_ada = w_ada.shape[2]

    def ffn_in_shard(w):
        w = w[0].astype(BF16)
        return jnp.concatenate([_pad_cols(w[:, :hu], hup), _pad_cols(w[:, hu:], hup)], axis=1)

    def ffn_out_shard(w):
        return jnp.pad(w[0].astype(BF16), ((0, hup - hu), (0, 0)))

    shards1 = [ffn_in_shard(ffn1_w_in), ffn_out_shard(ffn1_w_out)]
    shards_mix = [_pad_cols(w_in[0].astype(BF16), ws_in_pad), pool_w[0].astype(BF16).reshape(GROUP_DIM, GROUP_DIM),
                  w_out[0].astype(BF16)]
    shards2 = [ffn_in_shard(ffn2_w_in), ffn_out_shard(ffn2_w_out)]

    def own_slot(gathered, shards):
        return [lax.dynamic_update_slice(g, w[None], (chip, 0, 0)) for g, w in zip(gathered, shards)]

    def add_siblings(parts, recv, tag):
        return [_add_sibling(p, r, core_arr, f"{tag}_add_sibling_{k}") for k, (p, r) in enumerate(zip(parts, recv))]

    def add_chips(halves, recv, tag):
        return [_add_chips(hh, [(r, 0), (r, 1), (r, 2)], chip_arr, f"{tag}_add_chips_{k}") for k, (hh, r) in enumerate(zip(halves, recv))]

    def both_halves(mine, theirs):
        return [jnp.where(ac == 0, jnp.concatenate([g, r], axis=0), jnp.concatenate([r, g], axis=0)) for g, r in zip(mine, theirs)]

    g1 = _exchange(_gather_d2d(_exchange(_gather_ici(shards1[:1]), "gather_ffn1_ici")), "gather_ffn1_d2d")
    g_in1, = own_slot(g1, shards1[:1])

    c_all = _allgather_small(c.reshape(8, D // 8), True, "gather_c").reshape(8, D)
    c16 = jnp.pad(c_all, ((0, 8), (0, 0)))
    b_ada_mine = lax.dynamic_slice(b_ada, (0, chip * n_ada), (1, n_ada))
    act16, mod16 = _ada_fwd(c16, w_ada[0], b_ada_mine, "ada_fwd")
    mod_all = _allgather_small(mod16[:8], False, "gather_mod")
    mod = lax.dynamic_index_in_dim(mod_all, me, axis=1, keepdims=False).reshape(N_MOD, 1, D)
    sh1, sc1, gt1, sh2, sc2, gt2, sh3, sc3, gt3 = [mod[k] for k in range(N_MOD)]

    gate1, gate3 = 0.5 * gt1, 0.5 * gt3
    h1 = _norm_mod(xs, ffn1_norm_g, sh1, sc1, "ffn1_norm")
    ici = _gather_ici(shards1[1:] + shards_mix)
    a1, b1, u1 = _ffn_up(h1, g_in1, "ffn1_up", sides=[ici])
    g_out1, = own_slot(_exchange(_gather_d2d(ici.results[:1]), "gather_ffn1_out_d2d"), shards1[1:])
    wg_out1 = g_out1.reshape(4 * hup, D)
    d2d = _gather_d2d(ici.results[1:])
    y1, x1 = _mm_resid(u1, wg_out1, xs, gate1, "ffn1_down", sides=[d2d])
    g_win, g_pw, g_wout = own_slot(d2d.results, shards_mix)
    w_full = jnp.concatenate([g_win[k, :, :ws_in] for k in range(4)], axis=1)
    nf = 3 * D_ATTN
    w_all = jnp.concatenate([w_full[:, :nf], w_full[:, nf + N_HEADS:], w_full[:, nf:nf + N_HEADS],
                             jnp.zeros((D, LANE - N_HEADS), BF16)], axis=1)
    pw_full = g_pw.reshape(4, 4, GROUP_DIM // 4, GROUP_DIM).transpose(1, 0, 2, 3).reshape(4, GROUP_DIM, GROUP_DIM)
    wo_full = g_wout.reshape(4 * g_wout.shape[1], D)

    h2 = _norm_mod(x1, mix_norm_g, sh2, sc2, "mix_norm")
    proj = _mm(h2, w_all, F32, "mix_proj")
    b_pad = jnp.pad(b_forget, ((0, 0), (0, LANE - N_HEADS)))
    cum = _forget_fwd(proj, b_pad, "forget_fwd")
    cum_t = cum[:, :N_HEADS].T
    fq, fk = cum_t.reshape(N_HEADS, t, 1), cum_t.reshape(N_HEADS, 1, t)
    qn, kn, vb = _qkv_prep(proj, q_norm_g, k_norm_g, "qkv_prep")
    ici = _gather_ici(shards2)
    attn, lse = _attn_fwd(qn, kn, vb, fq, fk, "attn_fwd", sides=[ici])
    pooled, pool_out = _pool_fwd(proj, pw_full, pool_scale, "pool_fwd")
    cat = jnp.concatenate([attn, pool_out], axis=1)
    d2d = _gather_d2d(ici.results)
    y_mix, x2 = _mm_resid(cat, wo_full, x1, gt2, "mix_out", sides=[d2d])
    g_in2, g_out2 = own_slot(d2d.results, shards2)
    wg_out2 = g_out2.reshape(4 * hup, D)

    h3 = _norm_mod(x2, ffn2_norm_g, sh3, sc3, "ffn2_norm")
    a3, b3, u3 = _ffn_up(h3, g_in2, "ffn2_up")
    y3, x3 = _mm_resid(u3, wg_out2, x2, gate3, "ffn2_down")

    loss_part, dx3, d_final_g = _final_loss(x3, final_norm_g.reshape(1, D), tgt, "final_loss")
    loss = lax.psum(loss_part[0, 0], ("x", "y", "c"))

    dy3, dgt3 = _gate_bwd(dx3, y3, gate3, "ffn2_gate_bwd")
    dab3 = _ffn_dab(dy3, wg_out2, a3, b3, "ffn2_dab")
    parts2 = [_ffn_dwin(h3, dab3, "ffn2_dwin"), _mm_tn(u3, dy3, BF16, hup, "ffn2_dwout").reshape(4, hup, D)]
    rs = _reduce_siblings(parts2)
    dh3 = _ffn_dh(dab3, g_in2, "ffn2_dh", sides=[rs])
    halves2 = add_siblings(parts2, rs.results, "ffn2")
    dx2, dsh3, dsc3, d_ng3 = _norm_mod_bwd(dh3, x2, dx3, ffn2_norm_g, sc3, "ffn2_norm_bwd")

    dz, dgt2 = _gate_bwd(dx2, y_mix, gt2, "mix_gate_bwd")
    dcat = _mm_nt(dz, wo_full, BF16, "mix_dcat")
    d_wo = _mm_tn(cat, dz, BF16, 1024, "mix_dwout")
    du_pool, d_pw, d_ps = _pool_bwd(dcat, pooled, pw_full, pool_scale, "pool_bwd")
    rs = _reduce_chips(halves2)
    dqn, delta, dfq = _attn_bwd_q(qn, kn, vb, attn, dcat, lse, fq, fk, "attn_bwd_q", sides=[rs])
    mine2 = add_chips(halves2, rs.results, "ffn2")
    rs = _share_siblings(mine2)
    dkn, dv, dfk = _attn_bwd_kv(qn, kn, vb, dcat, lse, delta, fq, fk, "attn_bwd_kv", sides=[rs])
    r_in2, r_out2 = both_halves(mine2, rs.results)
    d_cum = jnp.pad((dfq.reshape(N_HEADS, t) + dfk.reshape(N_HEADS, t)).T, ((0, 0), (0, LANE - N_HEADS)))
    dfl, d_bf = _forget_bwd(d_cum, proj, b_pad, "forget_bwd")
    dq, dk, d_qg, d_kg = _qk_norm_bwd(dqn, dkn, proj, q_norm_g, k_norm_g, "qk_norm_bwd")
    dproj = jnp.concatenate([dq, dk, dv, du_pool, dfl], axis=1)
    d_wall = _mm_tn(h2, dproj, F32, 1024, "mix_dwin")
    d_wfull = jnp.concatenate([d_wall[:, :nf], d_wall[:, D_PROJ:D_PROJ + N_HEADS], d_wall[:, nf:D_PROJ]], axis=1)
    p_win = jnp.stack([_pad_cols(d_wfull[:, k * ws_in:(k + 1) * ws_in], ws_in_pad) for k in range(4)]).astype(BF16)
    p_pw = d_pw.reshape(4, 4, GROUP_DIM // 4, GROUP_DIM).transpose(1, 0, 2, 3).reshape(4, GROUP_DIM, GROUP_DIM).astype(BF16)
    parts_mix = [p_win, p_pw, d_wo.reshape(4, D // 4, D)]
    rs = _reduce_siblings(parts_mix)
    dh2 = _mm_nt(dproj, w_all, F32, "mix_dh", sides=[rs])
    halves_mix = add_siblings(parts_mix, rs.results, "mix")
    dx1, dsh2, dsc2, d_ng2 = _norm_mod_bwd(dh2, x1, dx2, mix_norm_g, sc2, "mix_norm_bwd")

    dy1, dgt1 = _gate_bwd(dx1, y1, gate1, "ffn1_gate_bwd")
    rs = _reduce_chips(halves_mix)
    dab1 = _ffn_dab(dy1, wg_out1, a1, b1, "ffn1_dab", sides=[rs])
    mine_mix = add_chips(halves_mix, rs.results, "mix")
    rs = _share_siblings(mine_mix)
    d_out1 = _mm_tn(u1, dy1, BF16, hup, "ffn1_dwout", sides=[rs]).reshape(4, hup, D)
    r_win, r_pw, r_wo = both_halves(mine_mix, rs.results)
    parts1 = [_ffn_dwin(h1, dab1, "ffn1_dwin"), d_out1]
    rs = _reduce_siblings(parts1)
    dh1 = _ffn_dh(dab1, g_in1, "ffn1_dh", sides=[rs])
    half_in1, half_out1 = add_siblings(parts1, rs.results, "ffn1")
    rs_out_near = _reduce_chips([half_out1], (0, 1))
    dx0, dsh1, dsc1, d_ng1 = _norm_mod_bwd(dh1, xs, dx1, ffn1_norm_g, sc1, "ffn1_norm_bwd", sides=[rs_out_near])
    grad_x = dx0.reshape(1, t, D)
    dgt1, dgt3 = 0.5 * dgt1, 0.5 * dgt3
    grads = {
        "w_in": r_win[:, :ws_in],
        "pool_w": r_pw,
        "w_out": r_wo,
        "ffn2_w_in": jnp.concatenate([r_in2[:hu], r_in2[hup:hup + hu]], axis=0),
        "ffn2_w_out": r_out2[:hu],
    }
    hidden_in_rows = ("ffn1_w_in", "ffn2_w_in")

    dmod = jnp.concatenate([dsh1, dsc1, dgt1, dsh2, dsc2, dgt2, dsh3, dsc3, dgt3], axis=1)
    small_names = ["b_ada", "ffn1_norm_g", "mix_norm_g", "b_forget", "q_norm_g", "k_norm_g", "pool_scale", "ffn2_norm_g",
                   "final_norm_g"]
    small_grads = [dmod, d_ng1, d_ng2, d_bf[:, :N_HEADS], d_qg, d_kg, d_ps, d_ng3, d_final_g]
    small_w = [b_ada, ffn1_norm_g, mix_norm_g, b_forget, q_norm_g, k_norm_g, pool_scale, ffn2_norm_g, final_norm_g.reshape(1, D)]
    small_m = [m_b_ada, m_ffn1_norm_g, m_mix_norm_g, m_b_forget, m_q_norm_g, m_k_norm_g, m_pool_scale, m_ffn2_norm_g,
               m_final_norm_g.reshape(1, D)]
    small_v = [v_b_ada, v_ffn1_norm_g, v_mix_norm_g, v_b_forget, v_q_norm_g, v_k_norm_g, v_pool_scale, v_ffn2_norm_g,
               v_final_norm_g.reshape(1, D)]
    sizes = [g.shape[1] for g in small_grads]
    n_small = sum(sizes)
    n_pack = -(-n_small // (8 * LANE)) * (8 * LANE)

    def pack(vs, fill):
        flat = jnp.concatenate([v.reshape(1, -1) for v in vs], axis=1)
        return jnp.pad(flat, ((0, 0), (0, n_pack - n_small)), constant_values=fill).reshape(8, n_pack // 8)

    g8 = _allgather_small(pack(small_grads, 0.0), True, "gather_small_grads")
    gs, ds, ms, vs = _adamw_small(pack(small_w, 0.0), g8, pack(small_m, 0.0), pack(small_v, 1.0), "adamw_small")

    def unpack(p):
        flat = p.reshape(1, n_pack)
        out, off = {}, 0
        for nme, sz in zip(small_names, sizes):
            out[nme] = flat[:, off:off + sz]
            off += sz
        return out

    small = [unpack(p) for p in (gs, ds, ms, vs)]
    for dct in small:
        dct["final_norm_g"] = dct["final_norm_g"].reshape(D)

    dmod_all = g8.reshape(8, n_pack)[:, :N_MOD * D]
    dmod_mine = lax.dynamic_slice(dmod_all, (0, chip * n_ada), (8, n_ada))
    rs_out_far = _reduce_chips([half_out1], (2,))
    grads["w_ada"] = _ada_bwd(act16[:8].reshape(8, D, 1), dmod_mine.reshape(8, 1, n_ada), "ada_bwd", sides=[rs_out_far])

    big = {"w_ada": (w_ada, m_w_ada, v_w_ada), "ffn1_w_in": (ffn1_w_in, m_ffn1_w_in, v_ffn1_w_in),
           "ffn1_w_out": (ffn1_w_out, m_ffn1_w_out, v_ffn1_w_out), "w_in": (w_in, m_w_in, v_w_in),
           "pool_w": (pool_w, m_pool_w, v_pool_w), "w_out": (w_out, m_w_out, v_w_out),
           "ffn2_w_in": (ffn2_w_in, m_ffn2_w_in, v_ffn2_w_in), "ffn2_w_out": (ffn2_w_out, m_ffn2_w_out, v_ffn2_w_out)}
    res = {}

    def update(nme, sides=()):
        w, m, v = big[nme]
        shp = w.shape
        g2 = grads[nme]
        if nme in hidden_in_rows:
            d, mo, vo = _adamw(w[0].T, g2, m[0].T, v[0].T, f"adamw_{nme}", sides=sides)
            res[nme] = tuple(o.T.reshape(shp) for o in (g2, d, mo, vo))
            return
        two = g2.shape
        d, mo, vo = _adamw(w.reshape(two), g2, m.reshape(two), v.reshape(two), f"adamw_{nme}", sides=sides)
        res[nme] = (g2.reshape(shp), d.reshape(shp), mo.reshape(shp), vo.reshape(shp))

    rs_in_near = _reduce_chips([half_in1], (0, 1))
    update("w_ada", [rs_in_near])
    rs_in_far = _reduce_chips([half_in1], (2,))
    update("ffn2_w_in", [rs_in_far])
    mine1 = [_add_chips(half_in1, [(rs_in_near.results[0], 0), (rs_in_near.results[0], 1), (rs_in_far.results[0], 0)],
                        chip_arr, "ffn1_add_chips_0"),
             _add_chips(half_out1, [(rs_out_near.results[0], 0), (rs_out_near.results[0], 1), (rs_out_far.results[0], 0)],
                        chip_arr, "ffn1_add_chips_1")]
    rs = _share_siblings(mine1)
    update("ffn2_w_out", [rs])
    r_in1, r_out1 = both_halves(mine1, rs.results)
    grads["ffn1_w_in"] = jnp.concatenate([r_in1[:hu], r_in1[hup:hup + hu]], axis=0)
    grads["ffn1_w_out"] = r_out1[:hu]
    for nme in ("w_in", "pool_w", "w_out", "ffn1_w_in", "ffn1_w_out"):
        update(nme)
    for nme in small_names:
        res[nme] = tuple(dct[nme] for dct in small)

    order = ["w_ada", "b_ada", "ffn1_norm_g", "ffn1_w_in", "ffn1_w_out", "mix_norm_g", "w_in", "b_forget", "q_norm_g",
             "k_norm_g", "pool_w", "pool_scale", "w_out", "ffn2_norm_g", "ffn2_w_in", "ffn2_w_out", "final_norm_g"]
    return (loss, grad_x, *[res[n][0] for n in order], *[res[n][1] for n in order], *[res[n][2] for n in order],
            *[res[n][3] for n in order])
```

```python
import functools

import jax
import jax.numpy as jnp
from jax import lax
from jax.experimental import pallas as pl
from jax.experimental.pallas import tpu as pltpu

F32 = jnp.float32
BF16 = jnp.bfloat16

D = 2048
N_HEADS = 8
HEAD_DIM = 128
D_ATTN = 1024
D_POOL = 1024
POOL_WINDOWS = (2, 4, 8, 16)
GROUP_DIM = 256
HALO = 16
N_MOD = 9
EPS = 1e-6
D_PROJ = 3 * D_ATTN + D_POOL
D_PROJ_PAD = D_PROJ + 128
LANE = 128
ATTN_BLOCK = 512
POOL_BLOCK = 512
ATTN_HEADS = (2, 2, 2)
ATTN_ROW_CHUNKS = (2, 2, 2)

ADAM_LR = 0.001
ADAM_B1 = 0.9
ADAM_B2 = 0.999
ADAM_EPS = 1e-08
ADAM_WD = 0.01
ADAM_STEP = 10

VMEM_LIMIT_V7X = 56 * 1024 * 1024
MESH_ID = pl.DeviceIdType.MESH
ANY = pl.BlockSpec(memory_space=pl.ANY)
VMEM = pl.BlockSpec(memory_space=pltpu.VMEM)

NT = (((1,), (1,)), ((), ()))
TN = (((0,), (0,)), ((), ()))


class _Side:
    def __init__(self, ins, outs, nsem, start, wait, alias=False):
        self.ins, self.outs, self.nsem, self.start, self.wait, self.alias = list(ins), list(outs), nsem, start, wait, alias
        self.results = None


def _pcall(body, *, name, out_shape, grid=None, in_specs=None, out_specs=None, scratch=(), sem=None, prefetch=0, sides=()):
    sides = list(sides)
    single = not isinstance(out_shape, (list, tuple))
    shapes = [out_shape] if single else list(out_shape)
    in_specs = list(in_specs)
    ospecs = [out_specs] if single else list(out_specs)
    scratch = list(scratch)
    n_in, n_out, n_scr = len(in_specs), len(shapes), len(scratch)
    assert not (sides and prefetch)
    aliases = {}
    for sd in sides:
        if sd.alias:
            for k in range(len(sd.ins)):
                aliases[len(in_specs) + k] = len(shapes) + k
        in_specs += [ANY] * len(sd.ins)
        shapes += sd.outs
        ospecs += [ANY] * len(sd.outs)
        scratch += [pltpu.SemaphoreType.DMA((sd.nsem,)), pltpu.SemaphoreType.DMA((sd.nsem,))]

    def wrapped(*refs):
        ins, outs, scr = refs[:len(in_specs)], refs[len(in_specs):len(in_specs) + len(shapes)], refs[len(in_specs) + len(shapes):]
        first = last = None
        for ax, g in enumerate(grid or ()):
            p = pl.program_id(ax)
            first = (p == 0) if first is None else first & (p == 0)
            last = (p == g - 1) if last is None else last & (p == g - 1)

        def each(what):
            i0, o0 = n_in, n_out
            for k, sd in enumerate(sides):
                getattr(sd, what)(ins[i0:i0 + len(sd.ins)], outs[o0:o0 + len(sd.outs)], scr[n_scr + 2 * k], scr[n_scr + 2 * k + 1])
                i0, o0 = i0 + len(sd.ins), o0 + len(sd.outs)

        if first is None:
            each("start")
        else:
            pl.when(first)(lambda: each("start"))
        body(*ins[:n_in], *outs[:n_out], *scr[:n_scr])
        if last is None:
            each("wait")
        else:
            pl.when(last)(lambda: each("wait"))

    params = dict(vmem_limit_bytes=VMEM_LIMIT_V7X)
    if sides and grid:
        params["dimension_semantics"] = ("arbitrary",) * len(grid)
    elif sem is not None:
        params["dimension_semantics"] = sem
    kw = dict(name=name, out_shape=shapes if (sides or not single) else shapes[0], compiler_params=pltpu.CompilerParams(**params))
    if aliases:
        kw["input_output_aliases"] = aliases
    final_ospecs = ospecs if (sides or not single) else ospecs[0]
    if prefetch:
        kw["grid_spec"] = pltpu.PrefetchScalarGridSpec(
            num_scalar_prefetch=prefetch, grid=grid, in_specs=in_specs, out_specs=final_ospecs, scratch_shapes=scratch)
    else:
        if grid is not None:
            kw["grid"] = grid
        kw["in_specs"] = in_specs
        kw["out_specs"] = final_ospecs
        kw["scratch_shapes"] = scratch
    call = pl.pallas_call(wrapped if sides else body, **kw)
    if not sides:
        return call

    def run(*operands):
        res = list(call(*operands, *[a for sd in sides for a in sd.ins]))
        o0 = n_out
        for sd in sides:
            sd.results = res[o0:o0 + len(sd.outs)]
            o0 += len(sd.outs)
        return res[0] if single else res[:n_out]

    return run


def _pick(n, cap, mult):
    best = None
    for d in range(mult, min(n, cap) + 1, mult):
        if n % d == 0:
            best = d
    assert best is not None, (n, cap, mult)
    return best


def _coords():
    return lax.axis_index("x"), lax.axis_index("y"), lax.axis_index("c")


def _dot(a, b, dims=None):
    if dims is None:
        return jnp.dot(a, b, preferred_element_type=F32)
    return lax.dot_general(a, b, dims, preferred_element_type=F32)


def _remote(src, dst, ssem, rsem, dev):
    return pltpu.make_async_remote_copy(src_ref=src, dst_ref=dst, send_sem=ssem, recv_sem=rsem,
                                        device_id=dev, device_id_type=MESH_ID)


def _allgather_small(v, whole_mesh, name):
    masks = list(range(1, 8)) if whole_mesh else [4, 2, 6]
    nslot = 8 if whole_mesh else 4

    def slot(px, py, pc):
        return 4 * px + 2 * py + pc if whole_mesh else 2 * px + py

    def body(v_ref, out_ref, ssem, rsem, lsem):
        x, y, c = _coords()
        mine = slot(x, y, c)
        peers = [(jnp.bitwise_xor(x, (m >> 2) & 1), jnp.bitwise_xor(y, (m >> 1) & 1), jnp.bitwise_xor(c, m & 1))
                 for m in masks]
        loc = pltpu.make_async_copy(v_ref, out_ref.at[mine], lsem)
        loc.start()
        sends = [_remote(v_ref, out_ref.at[mine], ssem.at[k], rsem.at[k], p) for k, p in enumerate(peers)]
        for cp in sends:
            cp.start()
        for k, p in enumerate(peers):
            _remote(v_ref, out_ref.at[slot(*p)], ssem.at[k], rsem.at[k], p).wait_recv()
        for cp in sends:
            cp.wait_send()
        loc.wait()

    return _pcall(body, name=name, out_shape=jax.ShapeDtypeStruct((nslot,) + v.shape, v.dtype),
                  in_specs=[VMEM], out_specs=VMEM,
                  scratch=[pltpu.SemaphoreType.DMA((len(masks),)), pltpu.SemaphoreType.DMA((len(masks),)),
                           pltpu.SemaphoreType.DMA(())])(v)


def _other_chips(x, y):
    return [(1 - x, y), (x, 1 - y), (1 - x, 1 - y)]


def _half_rows(shard_rows, core):
    h = shard_rows // 2
    return pl.ds(pl.multiple_of(core * h, 16), h)


def _later(src, dst, ssem, rsem, dev):
    return functools.partial(_remote, src, dst, ssem, rsem, dev)


def _side_from(ins, outs, nsem, pairs_of, alias=False):
    def start(*refs):
        for send, _ in pairs_of(*refs):
            send().start()

    def wait(*refs):
        pairs = pairs_of(*refs)
        for _, recv in pairs:
            recv().wait_recv()
        for send, _ in pairs:
            send().wait_send()

    return _Side(ins, outs, nsem, start, wait, alias)


def _gather_ici(ws):
    def pairs_of(w_refs, g_refs, ssem, rsem):
        x, y, c = _coords()
        me, out = 2 * x + y, []
        for a, w in enumerate(ws):
            rows = _half_rows(w.shape[0], c)
            for j, chip in enumerate(_other_chips(x, y)):
                sems = (ssem.at[3 * a + j], rsem.at[3 * a + j], (*chip, c))
                got = g_refs[a].at[2 * chip[0] + chip[1], rows]
                out.append((_later(w_refs[a].at[rows], g_refs[a].at[me, rows], *sems), _later(got, got, *sems)))
        return out

    return _side_from(ws, [jax.ShapeDtypeStruct((4,) + w.shape, w.dtype) for w in ws], 3 * len(ws), pairs_of)


def _gather_d2d(gs):
    def pairs_of(_, g_refs, ssem, rsem):
        x, y, c = _coords()
        out = []
        for a, g in enumerate(gs):
            for j, chip in enumerate(_other_chips(x, y)):
                sems = (ssem.at[3 * a + j], rsem.at[3 * a + j], (x, y, 1 - c))
                mine = g_refs[a].at[2 * chip[0] + chip[1], _half_rows(g.shape[1], c)]
                theirs = g_refs[a].at[2 * chip[0] + chip[1], _half_rows(g.shape[1], 1 - c)]
                out.append((_later(mine, mine, *sems), _later(theirs, theirs, *sems)))
        return out

    return _side_from(gs, [jax.ShapeDtypeStruct(g.shape, g.dtype) for g in gs], 3 * len(gs), pairs_of, alias=True)


def _reduce_siblings(ps):
    def pairs_of(p_refs, r_refs, ssem, rsem):
        x, y, c = _coords()
        out = []
        for a, p in enumerate(ps):
            src = p_refs[a].at[:, _half_rows(p.shape[1], 1 - c), :]
            cp = _later(src, r_refs[a], ssem.at[a], rsem.at[a], (x, y, 1 - c))
            out.append((cp, cp))
        return out

    return _side_from(ps, [jax.ShapeDtypeStruct((4, p.shape[1] // 2, p.shape[2]), p.dtype) for p in ps], len(ps), pairs_of)


def _reduce_chips(hs, dests=(0, 1, 2)):
    nd = len(dests)

    def pairs_of(h_refs, o_refs, ssem, rsem):
        x, y, c = _coords()
        chips = _other_chips(x, y)
        out = []
        for a in range(len(hs)):
            for slot, j in enumerate(dests):
                chip = chips[j]
                cp = _later(h_refs[a].at[2 * chip[0] + chip[1]], o_refs[a].at[slot], ssem.at[nd * a + slot], rsem.at[nd * a + slot],
                            (*chip, c))
                out.append((cp, cp))
        return out

    return _side_from(hs, [jax.ShapeDtypeStruct((nd,) + h.shape[1:], h.dtype) for h in hs], nd * len(hs), pairs_of)


def _share_siblings(gs):
    def pairs_of(g_refs, o_refs, ssem, rsem):
        x, y, c = _coords()
        cps = [_later(g_refs[a], o_refs[a], ssem.at[a], rsem.at[a], (x, y, 1 - c)) for a in range(len(gs))]
        return [(cp, cp) for cp in cps]

    return _side_from(gs, [jax.ShapeDtypeStruct(g.shape, g.dtype) for g in gs], len(gs), pairs_of)


def _exchange(side, name):
    def body():
        pass

    _pcall(body, name=name, out_shape=[], in_specs=[], out_specs=[], sides=[side])()
    return side.results


def _add_sibling(p, r, core, name):
    _, rr, cc = p.shape
    h = rr // 2
    th = _pick(h, max(16, (2 << 20) // (2 * cc)), 16)
    nb = h // th

    def body(c_ref, p_ref, r_ref, o_ref):
        o_ref[...] = (p_ref[...].astype(F32) + r_ref[...].astype(F32)).astype(BF16)

    return _pcall(body, name=name, out_shape=jax.ShapeDtypeStruct((4, h, cc), BF16), grid=(4, nb),
                  in_specs=[pl.BlockSpec((None, th, cc), lambda k, i, c_ref: (k, c_ref[0] * nb + i, 0)),
                            pl.BlockSpec((None, th, cc), lambda k, i, c_ref: (k, i, 0))],
                  out_specs=pl.BlockSpec((None, th, cc), lambda k, i, c_ref: (k, i, 0)),
                  sem=("parallel", "parallel"), prefetch=1)(core, p, r)


def _add_chips(hh, pieces, chip, name):
    _, h, cc = hh.shape
    th = _pick(h, max(16, (2 << 20) // (2 * cc)), 16)

    def body(k_ref, h_ref, r0_ref, r1_ref, r2_ref, o_ref):
        s = h_ref[...].astype(F32) + r0_ref[...].astype(F32)
        s = s + r1_ref[...].astype(F32)
        o_ref[...] = s + r2_ref[...].astype(F32)

    def piece(slot):
        return pl.BlockSpec((None, th, cc), lambda i, k_ref: (slot, i, 0))

    return _pcall(body, name=name, out_shape=jax.ShapeDtypeStruct((h, cc), F32), grid=(h // th,),
                  in_specs=[pl.BlockSpec((None, th, cc), lambda i, k_ref: (k_ref[0], i, 0))] + [piece(s) for _, s in pieces],
                  out_specs=pl.BlockSpec((th, cc), lambda i, k_ref: (i, 0)),
                  sem=("parallel",), prefetch=1)(chip, hh, *[a for a, _ in pieces])


def _adamw_math(w, g, m, v):
    m = ADAM_B1 * m + (1.0 - ADAM_B1) * g
    v = ADAM_B2 * v + (1.0 - ADAM_B2) * (g * g)
    m_hat = m / (1.0 - ADAM_B1 ** ADAM_STEP)
    v_hat = v / (1.0 - ADAM_B2 ** ADAM_STEP)
    delta = -ADAM_LR * (m_hat / (jnp.sqrt(v_hat) + ADAM_EPS) + ADAM_WD * w)
    return delta, m, v


def _adamw(w, g, m, v, name, sides=()):
    rr, cc = w.shape
    tr = _pick(rr, max(8, (3 << 20) // (4 * cc)), 8)

    def body(w_ref, g_ref, m_ref, v_ref, d_ref, mo_ref, vo_ref):
        d, mm, vv = _adamw_math(w_ref[...], g_ref[...], m_ref[...], v_ref[...])
        d_ref[...] = d
        mo_ref[...] = mm
        vo_ref[...] = vv

    spec = pl.BlockSpec((tr, cc), lambda i: (i, 0))
    return _pcall(body, name=name, out_shape=[jax.ShapeDtypeStruct(w.shape, F32)] * 3, grid=(rr // tr,),
                  in_specs=[spec] * 4, out_specs=[spec] * 3, sem=("parallel",), sides=sides)(w, g, m, v)


def _adamw_small(w, g8, m, v, name):
    def body(w_ref, g_ref, m_ref, v_ref, go_ref, d_ref, mo_ref, vo_ref):
        g = g_ref[0]
        for k in range(1, 8):
            g = g + g_ref[k]
        d, mm, vv = _adamw_math(w_ref[...], g, m_ref[...], v_ref[...])
        go_ref[...] = g
        d_ref[...] = d
        mo_ref[...] = mm
        vo_ref[...] = vv

    return _pcall(body, name=name, out_shape=[jax.ShapeDtypeStruct(w.shape, F32)] * 4,
                  in_specs=[VMEM] * 4, out_specs=[VMEM] * 4)(w, g8, m, v)


def _ada_fwd(c16, w_ada, b_ada, name):
    n = w_ada.shape[1]
    tn = _pick(n, 512, LANE)

    def body(c_ref, w_ref, b_ref, act_ref, mod_ref):
        cv = c_ref[...]
        act = cv * jax.nn.sigmoid(cv)
        act_ref[...] = act
        mod_ref[...] = _dot(act.astype(BF16), w_ref[...].astype(BF16)) + b_ref[...]

    return _pcall(body, name=name,
                  out_shape=[jax.ShapeDtypeStruct((16, D), F32), jax.ShapeDtypeStruct((16, n), F32)], grid=(n // tn,),
                  in_specs=[pl.BlockSpec((16, D), lambda j: (0, 0)), pl.BlockSpec((D, tn), lambda j: (0, j)),
                            pl.BlockSpec((1, tn), lambda j: (0, j))],
                  out_specs=[pl.BlockSpec((16, D), lambda j: (0, 0)), pl.BlockSpec((16, tn), lambda j: (0, j))],
                  sem=("arbitrary",))(c16, w_ada, b_ada)


def _ada_bwd(act, dmod, name, sides=()):
    n = dmod.shape[2]
    tm, tn = 256, _pick(n, 512, LANE)

    def body(a_ref, d_ref, o_ref):
        def term(b):
            return a_ref[b].astype(BF16).astype(F32) * d_ref[b].astype(BF16).astype(F32)

        acc = term(0)
        for b in range(1, 8):
            acc = acc + term(b)
        o_ref[...] = acc

    return _pcall(body, name=name, out_shape=jax.ShapeDtypeStruct((D, n), F32), grid=(D // tm, n // tn),
                  in_specs=[pl.BlockSpec((8, tm, 1), lambda i, j: (0, i, 0)), pl.BlockSpec((8, 1, tn), lambda i, j: (0, 0, j))],
                  out_specs=pl.BlockSpec((tm, tn), lambda i, j: (i, j)), sem=("parallel", "parallel"), sides=sides)(act, dmod)


def _norm_mod(x, g, sh, sc, name, sides=()):
    t = x.shape[0]
    tm = _pick(t, 512, 8)

    def body(x_ref, g_ref, sh_ref, sc_ref, h_ref):
        xf = x_ref[...]
        r = lax.rsqrt(jnp.mean(xf * xf, axis=-1, keepdims=True) + EPS)
        h = (xf * r) * g_ref[...]
        h_ref[...] = (h * (1.0 + sc_ref[...]) + sh_ref[...]).astype(BF16)

    vec = pl.BlockSpec((1, D), lambda i: (0, 0))
    row = pl.BlockSpec((tm, D), lambda i: (i, 0))
    return _pcall(body, name=name, out_shape=jax.ShapeDtypeStruct((t, D), BF16), grid=(t // tm,),
                  in_specs=[row, vec, vec, vec], out_specs=row, sem=("parallel",), sides=sides)(x, g, sh, sc)


def _norm_mod_bwd(dh, x, dxo, g, sc, name, sides=()):
    t = x.shape[0]
    tm = _pick(t, 256, 8)

    def body(dh_ref, x_ref, dxo_ref, g_ref, sc_ref, dx_ref, dsh_ref, dsc_ref, dg_ref):
        @pl.when(pl.program_id(0) == 0)
        def _():
            dsh_ref[...] = jnp.zeros_like(dsh_ref)
            dsc_ref[...] = jnp.zeros_like(dsc_ref)
            dg_ref[...] = jnp.zeros_like(dg_ref)

        xf, dh_ = x_ref[...], dh_ref[...]
        r = lax.rsqrt(jnp.mean(xf * xf, axis=-1, keepdims=True) + EPS)
        xhat = xf * r
        dsh_ref[...] += jnp.sum(dh_, axis=0, keepdims=True)
        dsc_ref[...] += jnp.sum(dh_ * (xhat * g_ref[...]), axis=0, keepdims=True)
        tt = dh_ * (1.0 + sc_ref[...])
        dg_ref[...] += jnp.sum(tt * xhat, axis=0, keepdims=True)
        dxh = tt * g_ref[...]
        dx_ref[...] = r * (dxh - xhat * jnp.mean(dxh * xhat, axis=-1, keepdims=True)) + dxo_ref[...]

    vec = pl.BlockSpec((1, D), lambda i: (0, 0))
    row = pl.BlockSpec((tm, D), lambda i: (i, 0))
    vshape = jax.ShapeDtypeStruct((1, D), F32)
    return _pcall(body, name=name, out_shape=[jax.ShapeDtypeStruct((t, D), F32), vshape, vshape, vshape], grid=(t // tm,),
                  in_specs=[row, row, row, vec, vec], out_specs=[row, vec, vec, vec], sem=("arbitrary",), sides=sides)(dh, x, dxo, g, sc)


def _gate_bwd(dxo, yy, gate, name):
    t = dxo.shape[0]
    tm = _pick(t, 512, 8)

    def body(dx_ref, y_ref, g_ref, dy_ref, dg_ref):
        @pl.when(pl.program_id(0) == 0)
        def _():
            dg_ref[...] = jnp.zeros_like(dg_ref)

        dx = dx_ref[...]
        dy_ref[...] = (dx * g_ref[...]).astype(BF16)
        dg_ref[...] += jnp.sum(dx * y_ref[...], axis=0, keepdims=True)

    vec = pl.BlockSpec((1, D), lambda i: (0, 0))
    row = pl.BlockSpec((tm, D), lambda i: (i, 0))
    return _pcall(body, name=name, out_shape=[jax.ShapeDtypeStruct((t, D), BF16), jax.ShapeDtypeStruct((1, D), F32)],
                  grid=(t // tm,), in_specs=[row, row, vec], out_specs=[row, vec], sem=("arbitrary",))(dxo, yy, gate)


def _final_loss(x, g, tgt, name):
    t = x.shape[0]
    tm = _pick(t, 256, 8)

    def body(x_ref, g_ref, t_ref, loss_ref, dx_ref, dg_ref):
        @pl.when(pl.program_id(0) == 0)
        def _():
            loss_ref[...] = jnp.zeros_like(loss_ref)
            dg_ref[...] = jnp.zeros_like(dg_ref)

        xf = x_ref[...]
        r = lax.rsqrt(jnp.mean(xf * xf, axis=-1, keepdims=True) + EPS)
        xhat = xf * r
        e = xhat * g_ref[...] - t_ref[...]
        per_tok = jnp.mean(e * e, axis=-1, keepdims=True)
        loss_ref[...] += 0.5 * jnp.sum(per_tok, axis=0, keepdims=True)
        dy = e * (1.0 / D)
        dg_ref[...] += jnp.sum(dy * xhat, axis=0, keepdims=True)
        dxh = dy * g_ref[...]
        dx_ref[...] = r * (dxh - xhat * jnp.mean(dxh * xhat, axis=-1, keepdims=True))

    vec = pl.BlockSpec((1, D), lambda i: (0, 0))
    row = pl.BlockSpec((tm, D), lambda i: (i, 0))
    return _pcall(body, name=name,
                  out_shape=[jax.ShapeDtypeStruct((1, LANE), F32), jax.ShapeDtypeStruct((t, D), F32),
                             jax.ShapeDtypeStruct((1, D), F32)],
                  grid=(t // tm,), in_specs=[row, vec, row],
                  out_specs=[pl.BlockSpec((1, LANE), lambda i: (0, 0)), row, vec], sem=("arbitrary",))(x, g, tgt)


def _mm(a, b, out_dtype, name):
    m, k = a.shape
    n = b.shape[1]
    tm = _pick(m, 1024, 8)
    tn = _pick(n, 1408, LANE)

    def body(a_ref, b_ref, o_ref):
        o_ref[...] = _dot(a_ref[...], b_ref[...]).astype(out_dtype)

    return _pcall(body, name=name, out_shape=jax.ShapeDtypeStruct((m, n), out_dtype), grid=(m // tm, n // tn),
                  in_specs=[pl.BlockSpec((tm, k), lambda i, j: (i, 0)), pl.BlockSpec((k, tn), lambda i, j: (0, j))],
                  out_specs=pl.BlockSpec((tm, tn), lambda i, j: (i, j)), sem=("parallel", "parallel"))(a, b)


def _mm_resid(a, b, resid, gate, name, sides=()):
    m, k = a.shape
    n = b.shape[1]
    tm, tn, tk = _pick(m, 1024, 8), _pick(n, 1024, LANE), _pick(k, 1408, LANE)
    nk = k // tk

    def body(a_ref, b_ref, r_ref, g_ref, y_ref, o_ref, acc):
        kk = pl.program_id(2)

        @pl.when(kk == 0)
        def _():
            acc[...] = jnp.zeros_like(acc)

        acc[...] += _dot(a_ref[...], b_ref[...])

        @pl.when(kk == nk - 1)
        def _():
            y_ref[...] = acc[...].astype(BF16)
            o_ref[...] = r_ref[...] + g_ref[...] * acc[...]

    blk = pl.BlockSpec((tm, tn), lambda i, j, kk: (i, j))
    return _pcall(body, name=name, out_shape=[jax.ShapeDtypeStruct((m, n), BF16), jax.ShapeDtypeStruct((m, n), F32)],
                  grid=(m // tm, n // tn, nk),
                  in_specs=[pl.BlockSpec((tm, tk), lambda i, j, kk: (i, kk)), pl.BlockSpec((tk, tn), lambda i, j, kk: (kk, j)),
                            blk, pl.BlockSpec((1, tn), lambda i, j, kk: (0, j))],
                  out_specs=[blk, blk], scratch=[pltpu.VMEM((tm, tn), F32)],
                  sem=("parallel", "parallel", "arbitrary"), sides=sides)(a, b, resid, gate)


def _mm_nt(a, b, out_dtype, name, sides=()):
    m, k = a.shape
    n = b.shape[0]
    tm, tn, tk = _pick(m, 1024, 8), _pick(n, 1024, LANE), _pick(k, 2816, LANE)
    nk = k // tk

    def body(a_ref, b_ref, o_ref, acc):
        kk = pl.program_id(2)

        @pl.when(kk == 0)
        def _():
            acc[...] = jnp.zeros_like(acc)

        acc[...] += _dot(a_ref[...], b_ref[...], NT)

        @pl.when(kk == nk - 1)
        def _():
            o_ref[...] = acc[...].astype(out_dtype)

    return _pcall(body, name=name, out_shape=jax.ShapeDtypeStruct((m, n), out_dtype), grid=(m // tm, n // tn, nk),
                  in_specs=[pl.BlockSpec((tm, tk), lambda i, j, kk: (i, kk)), pl.BlockSpec((tn, tk), lambda i, j, kk: (j, kk))],
                  out_specs=pl.BlockSpec((tm, tn), lambda i, j, kk: (i, j)), scratch=[pltpu.VMEM((tm, tn), F32)],
                  sem=("parallel", "parallel", "arbitrary"), sides=sides)(a, b)


def _mm_tn(a, b, out_dtype, tm_cap, name, sides=()):
    t, m = a.shape
    n = b.shape[1]
    tm, tn, tk = _pick(m, tm_cap, LANE), _pick(n, 1408, LANE), _pick(t, 2048, 16)
    nk = t // tk

    def body(a_ref, b_ref, o_ref, acc):
        kk = pl.program_id(2)

        @pl.when(kk == 0)
        def _():
            acc[...] = jnp.zeros_like(acc)

        acc[...] += _dot(a_ref[...], b_ref[...], TN)

        @pl.when(kk == nk - 1)
        def _():
            o_ref[...] = acc[...].astype(out_dtype)

    return _pcall(body, name=name, out_shape=jax.ShapeDtypeStruct((m, n), out_dtype), grid=(m // tm, n // tn, nk),
                  in_specs=[pl.BlockSpec((tk, tm), lambda i, j, kk: (kk, i)), pl.BlockSpec((tk, tn), lambda i, j, kk: (kk, j))],
                  out_specs=pl.BlockSpec((tm, tn), lambda i, j, kk: (i, j)), scratch=[pltpu.VMEM((tm, tn), F32)],
                  sem=("parallel", "parallel", "arbitrary"), sides=sides)(a, b)


def _ffn_up(h, wg, name, sides=()):
    t = h.shape[0]
    fp = wg.shape[2]
    tm, tn = _pick(t, 1024, 8), 256
    nn = fp // tn

    def body(h_ref, wa_ref, wb_ref, a_ref, b_ref, u_ref):
        hh = h_ref[...]
        a = _dot(hh, wa_ref[...])
        b = _dot(hh, wb_ref[...])
        a_ref[...] = a.astype(BF16)
        b_ref[...] = b.astype(BF16)
        u_ref[...] = (a * jax.nn.sigmoid(a) * b).astype(BF16)

    out = pl.BlockSpec((tm, tn), lambda i, j, n: (i, j * nn + n))
    return _pcall(body, name=name, out_shape=[jax.ShapeDtypeStruct((t, 2 * fp), BF16)] * 3, grid=(t // tm, 2, nn),
                  in_specs=[pl.BlockSpec((tm, D), lambda i, j, n: (i, 0)),
                            pl.BlockSpec((None, D, tn), lambda i, j, n: (j, 0, n)),
                            pl.BlockSpec((None, D, tn), lambda i, j, n: (j + 2, 0, n))],
                  out_specs=[out, out, out], sem=("parallel", "parallel", "parallel"), sides=sides)(h, wg, wg)


def _ffn_dab(dy, wo, a, b, name, sides=()):
    t = dy.shape[0]
    f2 = wo.shape[0]
    tm, tn = _pick(t, 2048, 8), 256

    def body(dy_ref, w_ref, a_ref, b_ref, o_ref):
        du = _dot(dy_ref[...], w_ref[...], NT)
        av, bv = a_ref[...].astype(F32), b_ref[...].astype(F32)
        sg = jax.nn.sigmoid(av)
        o_ref[0] = (du * bv * (sg * (1.0 + av * (1.0 - sg)))).astype(BF16)
        o_ref[1] = (du * (av * sg)).astype(BF16)

    blk = pl.BlockSpec((tm, tn), lambda i, n: (i, n))
    return _pcall(body, name=name, out_shape=jax.ShapeDtypeStruct((2, t, f2), BF16), grid=(t // tm, f2 // tn),
                  in_specs=[pl.BlockSpec((tm, D), lambda i, n: (i, 0)), pl.BlockSpec((tn, D), lambda i, n: (n, 0)), blk, blk],
                  out_specs=pl.BlockSpec((2, tm, tn), lambda i, n: (0, i, n)), sem=("parallel", "parallel"), sides=sides)(dy, wo, a, b)


def _ffn_dwin(h, dab, col_blocks, name, sides=()):
    t = h.shape[0]
    fp = dab.shape[2] // 2
    tm, tn, tk = _pick(fp, 1408, LANE), 1024, _pick(t, 2048, 16)
    nm, nk = fp // tm, t // tk
    n0, n1 = col_blocks

    def body(d_ref, h_ref, o_ref, acc):
        kk = pl.program_id(3)

        @pl.when(kk == 0)
        def _():
            acc[...] = jnp.zeros_like(acc)

        acc[...] += _dot(d_ref[...], h_ref[...], TN)

        @pl.when(kk == nk - 1)
        def _():
            o_ref[...] = acc[...].astype(BF16)

    return _pcall(body, name=name, out_shape=jax.ShapeDtypeStruct((4, fp, (n1 - n0) * tn), BF16), grid=(4, nm, n1 - n0, nk),
                  in_specs=[pl.BlockSpec((None, tk, tm), lambda u, i, n, kk: (u // 2, kk, (u % 2) * nm + i)),
                            pl.BlockSpec((tk, tn), lambda u, i, n, kk: (kk, n0 + n))],
                  out_specs=pl.BlockSpec((None, tm, tn), lambda u, i, n, kk: (u, i, n)),
                  scratch=[pltpu.VMEM((tm, tn), F32)],
                  sem=("parallel", "parallel", "parallel", "arbitrary"), sides=sides)(dab, h)


def _ffn_dh(dab, wg, name, sides=()):
    t = dab.shape[1]
    fp = wg.shape[2]
    tm, tn, tk = _pick(t, 1024, 8), 1024, _pick(fp, 2816, LANE)
    nkk = fp // tk

    def body(d_ref, w_ref, o_ref, acc):
        u, kk = pl.program_id(2), pl.program_id(3)

        @pl.when((u == 0) & (kk == 0))
        def _():
            acc[...] = jnp.zeros_like(acc)

        acc[...] += _dot(d_ref[...], w_ref[...], NT)

        @pl.when((u == 3) & (kk == nkk - 1))
        def _():
            o_ref[...] = acc[...]

    return _pcall(body, name=name, out_shape=jax.ShapeDtypeStruct((t, D), F32), grid=(t // tm, D // tn, 4, nkk),
                  in_specs=[pl.BlockSpec((None, tm, tk), lambda i, j, u, kk: (u // 2, i, (u % 2) * nkk + kk)),
                            pl.BlockSpec((None, tn, tk), lambda i, j, u, kk: (u, j, kk))],
                  out_specs=pl.BlockSpec((tm, tn), lambda i, j, u, kk: (i, j)), scratch=[pltpu.VMEM((tm, tn), F32)],
                  sem=("parallel", "parallel", "arbitrary", "arbitrary"), sides=sides)(dab, wg)


def _split3(v):
    hi = v.astype(BF16)
    r1 = v - hi.astype(F32)
    mid = r1.astype(BF16)
    lo = (r1 - mid.astype(F32)).astype(BF16)
    return hi, mid, lo


def _tri_sum(tri, v):
    hi, mid, lo = _split3(v)
    return (_dot(tri, hi) + _dot(tri, mid)) + _dot(tri, lo)


def _forget_fwd(proj, b_pad, name):
    t = proj.shape[0]
    tb = _pick(t, 256, 8)
    col = D_PROJ // LANE

    def body(f_ref, b_ref, o_ref, carry):
        @pl.when(pl.program_id(0) == 0)
        def _():
            carry[...] = jnp.zeros_like(carry)

        z = f_ref[...] + b_ref[...]
        lf = jnp.minimum(z, 0.0) - jnp.log(1.0 + jnp.exp(-jnp.abs(z)))
        r = lax.broadcasted_iota(jnp.int32, (tb, tb), 0)
        cidx = lax.broadcasted_iota(jnp.int32, (tb, tb), 1)
        tri = (r >= cidx).astype(BF16)
        o_ref[...] = _tri_sum(tri, lf) + carry[...]
        carry[...] += jnp.sum(lf, axis=0, keepdims=True)

    return _pcall(body, name=name, out_shape=jax.ShapeDtypeStruct((t, LANE), F32), grid=(t // tb,),
                  in_specs=[pl.BlockSpec((tb, LANE), lambda i: (i, col)), pl.BlockSpec((1, LANE), lambda i: (0, 0))],
                  out_specs=pl.BlockSpec((tb, LANE), lambda i: (i, 0)), scratch=[pltpu.VMEM((1, LANE), F32)],
                  sem=("arbitrary",))(proj, b_pad)


def _forget_bwd(d_cum, proj, b_pad, name):
    t = proj.shape[0]
    tb = _pick(t, 256, 8)
    nb = t // tb
    col = D_PROJ // LANE

    def body(d_ref, f_ref, b_ref, o_ref, db_ref, carry):
        @pl.when(pl.program_id(0) == 0)
        def _():
            carry[...] = jnp.zeros_like(carry)
            db_ref[...] = jnp.zeros_like(db_ref)

        dc = d_ref[...]
        r = lax.broadcasted_iota(jnp.int32, (tb, tb), 0)
        cidx = lax.broadcasted_iota(jnp.int32, (tb, tb), 1)
        tri = (r <= cidx).astype(BF16)
        dlf = _tri_sum(tri, dc) + carry[...]
        carry[...] += jnp.sum(dc, axis=0, keepdims=True)
        z = f_ref[...] + b_ref[...]
        lane = lax.broadcasted_iota(jnp.int32, (tb, LANE), 1)
        dz = jnp.where(lane < N_HEADS, dlf * jax.nn.sigmoid(-z), 0.0)
        o_ref[...] = dz.astype(BF16)
        db_ref[...] += jnp.sum(dz, axis=0, keepdims=True)

    return _pcall(body, name=name, out_shape=[jax.ShapeDtypeStruct((t, LANE), BF16), jax.ShapeDtypeStruct((1, LANE), F32)],
                  grid=(nb,),
                  in_specs=[pl.BlockSpec((tb, LANE), lambda i: (nb - 1 - i, 0)),
                            pl.BlockSpec((tb, LANE), lambda i: (nb - 1 - i, col)),
                            pl.BlockSpec((1, LANE), lambda i: (0, 0))],
                  out_specs=[pl.BlockSpec((tb, LANE), lambda i: (nb - 1 - i, 0)), pl.BlockSpec((1, LANE), lambda i: (0, 0))],
                  scratch=[pltpu.VMEM((1, LANE), F32)], sem=("arbitrary",))(d_cum, proj, b_pad)


def _head_norm(v, g):
    r = lax.rsqrt(jnp.mean(v * v, axis=-1, keepdims=True) + EPS)
    return v * r, r


def _qkv_prep(proj, qg, kg, name):
    t = proj.shape[0]
    tm = _pick(t, 1024, 8)

    def body(q_ref, k_ref, v_ref, qg_ref, kg_ref, qo_ref, ko_ref, vo_ref):
        qo_ref[...] = (_head_norm(q_ref[...], None)[0] * qg_ref[...]).astype(BF16)
        ko_ref[...] = (_head_norm(k_ref[...], None)[0] * kg_ref[...]).astype(BF16)
        vo_ref[...] = v_ref[...].astype(BF16)

    def blk(off):
        return pl.BlockSpec((tm, HEAD_DIM), lambda i, h: (i, off + h))

    vec = pl.BlockSpec((1, HEAD_DIM), lambda i, h: (0, 0))
    return _pcall(body, name=name, out_shape=[jax.ShapeDtypeStruct((t, D_ATTN), BF16)] * 3, grid=(t // tm, N_HEADS),
                  in_specs=[blk(0), blk(N_HEADS), blk(2 * N_HEADS), vec, vec], out_specs=[blk(0)] * 3,
                  sem=("parallel", "parallel"))(proj, proj, proj, qg, kg)


def _qk_norm_bwd(dqn, dkn, proj, qg, kg, name):
    t = proj.shape[0]
    tm = _pick(t, 1024, 8)

    def one(d_ref, v_ref, g_ref, o_ref, dg_ref):
        xhat, r = _head_norm(v_ref[...], None)
        d = d_ref[...]
        dg_ref[...] += jnp.sum(d * xhat, axis=0, keepdims=True)
        dxh = d * g_ref[...]
        o_ref[...] = (r * (dxh - xhat * jnp.mean(dxh * xhat, axis=-1, keepdims=True))).astype(BF16)

    def body(dq_ref, dk_ref, q_ref, k_ref, qg_ref, kg_ref, qo_ref, ko_ref, dqg_ref, dkg_ref):
        @pl.when((pl.program_id(0) == 0) & (pl.program_id(1) == 0))
        def _():
            dqg_ref[...] = jnp.zeros_like(dqg_ref)
            dkg_ref[...] = jnp.zeros_like(dkg_ref)

        one(dq_ref, q_ref, qg_ref, qo_ref, dqg_ref)
        one(dk_ref, k_ref, kg_ref, ko_ref, dkg_ref)

    def blk(off):
        return pl.BlockSpec((tm, HEAD_DIM), lambda i, h: (i, off + h))

    vec = pl.BlockSpec((1, HEAD_DIM), lambda i, h: (0, 0))
    vshape = jax.ShapeDtypeStruct((1, HEAD_DIM), F32)
    return _pcall(body, name=name, out_shape=[jax.ShapeDtypeStruct((t, D_ATTN), BF16)] * 2 + [vshape, vshape],
                  grid=(t // tm, N_HEADS),
                  in_specs=[blk(0), blk(0), blk(0), blk(N_HEADS), vec, vec], out_specs=[blk(0), blk(0), vec, vec],
                  sem=("arbitrary", "arbitrary"))(dqn, dkn, proj, proj, qg, kg)


ATTN_SCALE = HEAD_DIM ** -0.5


def _logits(q, k, fq, fk, diag, r0, tq, tk):
    s = _dot(q, k, NT) * ATTN_SCALE + (fq - fk)
    if diag:
        r = r0 + lax.broadcasted_iota(jnp.int32, (tq, tk), 0)
        cidx = lax.broadcasted_iota(jnp.int32, (tq, tk), 1)
        s = jnp.where(r >= cidx, s, -jnp.inf)
    return s


def _head_cols(hp):
    return [(hh, slice(hh * HEAD_DIM, (hh + 1) * HEAD_DIM)) for hh in range(hp)]


def _tri(tt, n, by_row):
    if by_row:
        i = sum((tt >= k * (k + 1) // 2).astype(jnp.int32) for k in range(1, n))
        return i, tt - i * (i + 1) // 2
    j = sum((tt >= k * n - k * (k - 1) // 2).astype(jnp.int32) for k in range(1, n))
    return j + tt - (j * n - j * (j - 1) // 2), j


def _attn_fwd(q, k, v, fq, fk, name, sides=()):
    t = q.shape[0]
    tq = tk = _pick(t, ATTN_BLOCK, LANE)
    nk = t // tk

    hp, rc = ATTN_HEADS[0], tq // ATTN_ROW_CHUNKS[0]

    def body(q_ref, k_ref, v_ref, fq_ref, fk_ref, o_ref, lse_ref, m_s, l_s, acc):
        i, j = _tri(pl.program_id(1), nk, True)

        @pl.when(j == 0)
        def _():
            m_s[...] = jnp.full_like(m_s, -jnp.inf)
            l_s[...] = jnp.zeros_like(l_s)
            acc[...] = jnp.zeros_like(acc)

        def step(diag):
            for hh, cols in _head_cols(hp):
                for r0 in range(0, tq, rc):
                    rows = slice(r0, r0 + rc)
                    s = _logits(q_ref[rows, cols], k_ref[:, cols], fq_ref[hh, rows, :], fk_ref[hh], diag, r0, rc, tk)
                    m_new = jnp.maximum(m_s[hh, rows, :], jnp.max(s, axis=-1, keepdims=True))
                    alpha = jnp.exp(m_s[hh, rows, :] - m_new)
                    p = jnp.exp(s - m_new)
                    l_s[hh, rows, :] = alpha * l_s[hh, rows, :] + jnp.sum(p, axis=-1, keepdims=True)
                    acc[rows, cols] = alpha * acc[rows, cols] + _dot(p.astype(BF16), v_ref[:, cols])
                    m_s[hh, rows, :] = m_new

        @pl.when(j < i)
        def _():
            step(False)

        @pl.when(j == i)
        def _():
            step(True)

        @pl.when(j == i)
        def _():
            for hh, cols in _head_cols(hp):
                o_ref[:, cols] = (acc[:, cols] / l_s[hh]).astype(BF16)
                lse_ref[hh] = m_s[hh] + jnp.log(l_s[hh])

    qb = pl.BlockSpec((tq, hp * HEAD_DIM), lambda h, tt: (_tri(tt, nk, True)[0], h))
    kb = pl.BlockSpec((tk, hp * HEAD_DIM), lambda h, tt: (_tri(tt, nk, True)[1], h))
    col = pl.BlockSpec((hp, tq, 1), lambda h, tt: (h, _tri(tt, nk, True)[0], 0))
    return _pcall(body, name=name,
                  out_shape=[jax.ShapeDtypeStruct((t, D_ATTN), BF16), jax.ShapeDtypeStruct((N_HEADS, t, 1), F32)],
                  grid=(N_HEADS // hp, nk * (nk + 1) // 2),
                  in_specs=[qb, kb, kb, col, pl.BlockSpec((hp, 1, tk), lambda h, tt: (h, 0, _tri(tt, nk, True)[1]))],
                  out_specs=[qb, col],
                  scratch=[pltpu.VMEM((hp, tq, 1), F32), pltpu.VMEM((hp, tq, 1), F32), pltpu.VMEM((tq, hp * HEAD_DIM), F32)],
                  sem=("parallel", "arbitrary"), sides=sides)(q, k, v, fq, fk)


def _attn_bwd_q(q, k, v, o, do, lse, fq, fk, name, sides=()):
    t = q.shape[0]
    tq = tk = _pick(t, ATTN_BLOCK, LANE)
    nk = t // tk

    hp, rc = ATTN_HEADS[1], tq // ATTN_ROW_CHUNKS[1]

    def body(q_ref, k_ref, v_ref, o_ref, do_ref, lse_ref, fq_ref, fk_ref, dq_ref, dl_ref, dfq_ref, acc, dl_s, df_s):
        i, j = _tri(pl.program_id(1), nk, True)

        @pl.when(j == 0)
        def _():
            acc[...] = jnp.zeros_like(acc)
            df_s[...] = jnp.zeros_like(df_s)
            for hh, cols in _head_cols(hp):
                dl_s[hh] = jnp.sum(do_ref[:, cols].astype(F32) * o_ref[:, cols].astype(F32), axis=-1, keepdims=True)

        def step(diag):
            for hh, cols in _head_cols(hp):
                for r0 in range(0, tq, rc):
                    rows = slice(r0, r0 + rc)
                    s = _logits(q_ref[rows, cols], k_ref[:, cols], fq_ref[hh, rows, :], fk_ref[hh], diag, r0, rc, tk)
                    p = jnp.exp(s - lse_ref[hh, rows, :])
                    dp = _dot(do_ref[rows, cols], v_ref[:, cols], NT)
                    ds = p * (dp - dl_s[hh, rows, :])
                    df_s[hh, rows, :] += jnp.sum(ds, axis=-1, keepdims=True)
                    acc[rows, cols] += _dot(ds.astype(BF16), k_ref[:, cols])

        @pl.when(j < i)
        def _():
            step(False)

        @pl.when(j == i)
        def _():
            step(True)

        @pl.when(j == i)
        def _():
            dq_ref[...] = acc[...] * ATTN_SCALE
            dl_ref[...] = dl_s[...]
            dfq_ref[...] = df_s[...]

    qb = pl.BlockSpec((tq, hp * HEAD_DIM), lambda h, tt: (_tri(tt, nk, True)[0], h))
    kb = pl.BlockSpec((tk, hp * HEAD_DIM), lambda h, tt: (_tri(tt, nk, True)[1], h))
    col = pl.BlockSpec((hp, tq, 1), lambda h, tt: (h, _tri(tt, nk, True)[0], 0))
    cshape = jax.ShapeDtypeStruct((N_HEADS, t, 1), F32)
    return _pcall(body, name=name, out_shape=[jax.ShapeDtypeStruct((t, D_ATTN), F32), cshape, cshape],
                  grid=(N_HEADS // hp, nk * (nk + 1) // 2),
                  in_specs=[qb, kb, kb, qb, qb, col, col, pl.BlockSpec((hp, 1, tk), lambda h, tt: (h, 0, _tri(tt, nk, True)[1]))],
                  out_specs=[qb, col, col],
                  scratch=[pltpu.VMEM((tq, hp * HEAD_DIM), F32), pltpu.VMEM((hp, tq, 1), F32), pltpu.VMEM((hp, tq, 1), F32)],
                  sem=("parallel", "arbitrary"), sides=sides)(q, k, v, o, do, lse, fq, fk)


def _attn_bwd_kv(q, k, v, do, lse, delta, fq, fk, name, sides=()):
    t = q.shape[0]
    tq = tk = _pick(t, ATTN_BLOCK, LANE)
    nq = t // tq

    hp, rc = ATTN_HEADS[2], tq // ATTN_ROW_CHUNKS[2]

    def body(q_ref, k_ref, v_ref, do_ref, lse_ref, dl_ref, fq_ref, fk_ref, dk_ref, dv_ref, dfk_ref, dk_s, dv_s, df_s):
        i, j = _tri(pl.program_id(1), nq, False)

        @pl.when(i == j)
        def _():
            dk_s[...] = jnp.zeros_like(dk_s)
            dv_s[...] = jnp.zeros_like(dv_s)
            df_s[...] = jnp.zeros_like(df_s)

        def step(diag):
            for hh, cols in _head_cols(hp):
                for r0 in range(0, tq, rc):
                    rows = slice(r0, r0 + rc)
                    s = _logits(q_ref[rows, cols], k_ref[:, cols], fq_ref[hh, rows, :], fk_ref[hh], diag, r0, rc, tk)
                    p = jnp.exp(s - lse_ref[hh, rows, :])
                    dv_s[:, cols] += _dot(p.astype(BF16), do_ref[rows, cols], TN)
                    dp = _dot(do_ref[rows, cols], v_ref[:, cols], NT)
                    ds = p * (dp - dl_ref[hh, rows, :])
                    df_s[hh] -= jnp.sum(ds, axis=0, keepdims=True)
                    dk_s[:, cols] += _dot(ds.astype(BF16), q_ref[rows, cols], TN)

        @pl.when(i > j)
        def _():
            step(False)

        @pl.when(i == j)
        def _():
            step(True)

        @pl.when(i == nq - 1)
        def _():
            dk_ref[...] = dk_s[...] * ATTN_SCALE
            dv_ref[...] = dv_s[...].astype(BF16)
            dfk_ref[...] = df_s[...]

    qb = pl.BlockSpec((tq, hp * HEAD_DIM), lambda h, tt: (_tri(tt, nq, False)[0], h))
    kb = pl.BlockSpec((tk, hp * HEAD_DIM), lambda h, tt: (_tri(tt, nq, False)[1], h))
    col = pl.BlockSpec((hp, tq, 1), lambda h, tt: (h, _tri(tt, nq, False)[0], 0))
    row = pl.BlockSpec((hp, 1, tk), lambda h, tt: (h, 0, _tri(tt, nq, False)[1]))
    return _pcall(body, name=name,
                  out_shape=[jax.ShapeDtypeStruct((t, D_ATTN), F32), jax.ShapeDtypeStruct((t, D_ATTN), BF16),
                             jax.ShapeDtypeStruct((N_HEADS, 1, t), F32)],
                  grid=(N_HEADS // hp, nq * (nq + 1) // 2),
                  in_specs=[qb, kb, kb, qb, col, col, col, row], out_specs=[kb, kb, row],
                  scratch=[pltpu.VMEM((tk, hp * HEAD_DIM), F32), pltpu.VMEM((tk, hp * HEAD_DIM), F32), pltpu.VMEM((hp, 1, tk), F32)],
                  sem=("parallel", "arbitrary"), sides=sides)(q, k, v, do, lse, delta, fq, fk)


def _window_sum(v, w, back):
    n = v.shape[0]
    k = 1
    while k < w:
        v = v + pltpu.roll(v, k if back else n - k, axis=0)
        k *= 2
    return v


def _pool_fwd(proj, pw, ps, name):
    t = proj.shape[0]
    tm = _pick(t, POOL_BLOCK, HALO)
    col = 3 * D_ATTN // D_POOL

    def body(u_ref, prev_ref, pw_ref, ps_ref, pooled_ref, out_ref):
        i = pl.program_id(0)
        prev = jnp.where(i > 0, prev_ref[...], 0.0)
        ext = jnp.concatenate([prev, u_ref[...]], axis=0)
        pos = i * tm + lax.broadcasted_iota(jnp.int32, (tm, 1), 0)
        for g, w in enumerate(POOL_WINDOWS):
            cols = slice(g * GROUP_DIM, (g + 1) * GROUP_DIM)
            xg = ext[:, cols]
            sw = _window_sum(xg, w, True)[HALO:, :]
            cnt = jnp.minimum(pos + 1, w).astype(F32)
            pooled = (sw / cnt - xg[HALO:, :]).astype(BF16)
            pooled_ref[:, cols] = pooled
            out_ref[:, cols] = (_dot(pooled, pw_ref[g]) * ps_ref[:, cols]).astype(BF16)

    row = pl.BlockSpec((tm, D_POOL), lambda i: (i, 0))
    return _pcall(body, name=name, out_shape=[jax.ShapeDtypeStruct((t, D_POOL), BF16)] * 2, grid=(t // tm,),
                  in_specs=[pl.BlockSpec((tm, D_POOL), lambda i: (i, col)),
                            pl.BlockSpec((HALO, D_POOL), lambda i: (jnp.maximum(i * (tm // HALO) - 1, 0), col)),
                            pl.BlockSpec((len(POOL_WINDOWS), GROUP_DIM, GROUP_DIM), lambda i: (0, 0, 0)),
                            pl.BlockSpec((1, D_POOL), lambda i: (0, 0))],
                  out_specs=[row, row], sem=("parallel",))(proj, proj, pw, ps)


def _pool_bwd(dout, pooled, pw, ps, name):
    t = pooled.shape[0]
    tm = _pick(t, POOL_BLOCK, HALO)
    nb = t // tm
    ng = len(POOL_WINDOWS)

    def body(d_ref, nxt_ref, p_ref, pw_ref, ps_ref, du_ref, dpw_ref, dps_ref):
        i = pl.program_id(0)

        @pl.when(i == 0)
        def _():
            dpw_ref[...] = jnp.zeros_like(dpw_ref)
            dps_ref[...] = jnp.zeros_like(dps_ref)

        nxt = jnp.where(i < nb - 1, nxt_ref[...].astype(F32), 0.0)
        ext = jnp.concatenate([d_ref[...].astype(F32), nxt], axis=0)
        pos = i * tm + lax.broadcasted_iota(jnp.int32, (tm + HALO, 1), 0)
        for g, w in enumerate(POOL_WINDOWS):
            cols = slice(g * GROUP_DIM, (g + 1) * GROUP_DIM)
            pooled_g = p_ref[:, cols]
            dg = ext[:, cols]
            pm = _dot(pooled_g, pw_ref[g])
            dps_ref[:, cols] += jnp.sum(dg[:tm, :] * pm, axis=0, keepdims=True)
            dpm = (dg * ps_ref[:, cols]).astype(BF16)
            dpw_ref[g] += _dot(pooled_g, dpm[:tm, :], TN)
            dpooled = _dot(dpm, pw_ref[g], NT)
            cnt = jnp.minimum(pos + 1, w).astype(F32)
            fwd = _window_sum(dpooled / cnt, w, False)
            du_ref[:, cols] = (fwd[:tm, :] - dpooled[:tm, :]).astype(BF16)

    row = pl.BlockSpec((tm, D_POOL), lambda i: (i, 0))
    return _pcall(body, name=name,
                  out_shape=[jax.ShapeDtypeStruct((t, D_POOL), BF16), jax.ShapeDtypeStruct((ng, GROUP_DIM, GROUP_DIM), F32),
                             jax.ShapeDtypeStruct((1, D_POOL), F32)],
                  grid=(nb,),
                  in_specs=[pl.BlockSpec((tm, D_POOL), lambda i: (i, 1)),
                            pl.BlockSpec((HALO, D_POOL), lambda i: (jnp.minimum((i + 1) * (tm // HALO), t // HALO - 1), 1)),
                            row, pl.BlockSpec((ng, GROUP_DIM, GROUP_DIM), lambda i: (0, 0, 0)),
                            pl.BlockSpec((1, D_POOL), lambda i: (0, 0))],
                  out_specs=[row, pl.BlockSpec((ng, GROUP_DIM, GROUP_DIM), lambda i: (0, 0, 0)),
                             pl.BlockSpec((1, D_POOL), lambda i: (0, 0))],
                  sem=("arbitrary",))(dout, dout, pooled, pw, ps)


def _pad_cols(w, n):
    return jnp.pad(w, ((0, 0), (0, n - w.shape[1])))


def kernel(x, c, w_ada, b_ada, ffn1_norm_g, ffn1_w_in, ffn1_w_out, mix_norm_g, w_in, b_forget, q_norm_g, k_norm_g, pool_w, pool_scale, w_out, ffn2_norm_g, ffn2_w_in, ffn2_w_out, final_norm_g, loss_target, m_w_ada, m_b_ada, m_ffn1_norm_g, m_ffn1_w_in, m_ffn1_w_out, m_mix_norm_g, m_w_in, m_b_forget, m_q_norm_g, m_k_norm_g, m_pool_w, m_pool_scale, m_w_out, m_ffn2_norm_g, m_ffn2_w_in, m_ffn2_w_out, m_final_norm_g, v_w_ada, v_b_ada, v_ffn1_norm_g, v_ffn1_w_in, v_ffn1_w_out, v_mix_norm_g, v_w_in, v_b_forget, v_q_norm_g, v_k_norm_g, v_pool_w, v_pool_scale, v_w_out, v_ffn2_norm_g, v_ffn2_w_in, v_ffn2_w_out, v_final_norm_g):
    ax, ay, ac = _coords()
    chip = 2 * ax + ay
    me = 2 * chip + ac
    chip_arr = jnp.reshape(chip, (1,)).astype(jnp.int32)
    core_arr = jnp.reshape(ac, (1,)).astype(jnp.int32)

    t = x.shape[1]
    xs = x.reshape(t, D)
    tgt = loss_target.reshape(t, D)
    hu = ffn1_w_out.shape[1]
    hup = -(-hu // LANE) * LANE
    ws_in = w_in.shape[2]
    ws_in_pad = -(-ws_in // LANE) * LANE
    n_ada = w_ada.shape[2]

    def ffn_in_shard(w):
        w = w[0].astype(BF16)
        return jnp.concatenate([_pad_cols(w[:, :hu], hup), _pad_cols(w[:, hu:], hup)], axis=1)

    def ffn_out_shard(w):
        return jnp.pad(w[0].astype(BF16), ((0, hup - hu), (0, 0)))

    shards1 = [ffn_in_shard(ffn1_w_in), ffn_out_shard(ffn1_w_out)]
    shards_mix = [_pad_cols(w_in[0].astype(BF16), ws_in_pad), pool_w[0].astype(BF16).reshape(GROUP_DIM, GROUP_DIM),
                  w_out[0].astype(BF16)]
    shards2 = [ffn_in_shard(ffn2_w_in), ffn_out_shard(ffn2_w_out)]

    def own_slot(gathered, shards):
        return [lax.dynamic_update_slice(g, w[None], (chip, 0, 0)) for g, w in zip(gathered, shards)]

    def add_siblings(parts, recv, tag):
        return [_add_sibling(p, r, core_arr, f"{tag}_add_sibling_{k}") for k, (p, r) in enumerate(zip(parts, recv))]

    def add_chips(halves, recv, tag):
        return [_add_chips(hh, [(r, 0), (r, 1), (r, 2)], chip_arr, f"{tag}_add_chips_{k}") for k, (hh, r) in enumerate(zip(halves, recv))]

    def both_halves(mine, theirs):
        return [jnp.where(ac == 0, jnp.concatenate([g, r], axis=0), jnp.concatenate([r, g], axis=0)) for g, r in zip(mine, theirs)]

    g1 = _exchange(_gather_d2d(_exchange(_gather_ici(shards1[:1]), "gather_ffn1_ici")), "gather_ffn1_d2d")
    g_in1, = own_slot(g1, shards1[:1])

    c_all = _allgather_small(c.reshape(8, D // 8), True, "gather_c").reshape(8, D)
    c16 = jnp.pad(c_all, ((0, 8), (0, 0)))
    b_ada_mine = lax.dynamic_slice(b_ada, (0, chip * n_ada), (1, n_ada))
    act16, mod16 = _ada_fwd(c16, w_ada[0], b_ada_mine, "ada_fwd")
    mod_all = _allgather_small(mod16[:8], False, "gather_mod")
    mod = lax.dynamic_index_in_dim(mod_all, me, axis=1, keepdims=False).reshape(N_MOD, 1, D)
    sh1, sc1, gt1, sh2, sc2, gt2, sh3, sc3, gt3 = [mod[k] for k in range(N_MOD)]

    gate1, gate3 = 0.5 * gt1, 0.5 * gt3
    h1 = _norm_mod(xs, ffn1_norm_g, sh1, sc1, "ffn1_norm")
    ici = _gather_ici(shards1[1:])
    a1, b1, u1 = _ffn_up(h1, g_in1, "ffn1_up", sides=[ici])
    g_out1, = own_slot(_exchange(_gather_d2d(ici.results), "gather_ffn1_out_d2d"), shards1[1:])
    wg_out1 = g_out1.reshape(4 * hup, D)
    ici = _gather_ici(shards_mix)
    y1, x1 = _mm_resid(u1, wg_out1, xs, gate1, "ffn1_down", sides=[ici])
    d2d = _gather_d2d(ici.results)
    h2 = _norm_mod(x1, mix_norm_g, sh2, sc2, "mix_norm", sides=[d2d])
    g_win, g_pw, g_wout = own_slot(d2d.results, shards_mix)
    w_full = jnp.concatenate([g_win[k, :, :ws_in] for k in range(4)], axis=1)
    nf = 3 * D_ATTN
    w_all = jnp.concatenate([w_full[:, :nf], w_full[:, nf + N_HEADS:], w_full[:, nf:nf + N_HEADS],
                             jnp.zeros((D, LANE - N_HEADS), BF16)], axis=1)
    pw_full = g_pw.reshape(4, 4, GROUP_DIM // 4, GROUP_DIM).transpose(1, 0, 2, 3).reshape(4, GROUP_DIM, GROUP_DIM)
    wo_full = g_wout.reshape(4 * g_wout.shape[1], D)

    proj = _mm(h2, w_all, F32, "mix_proj")
    b_pad = jnp.pad(b_forget, ((0, 0), (0, LANE - N_HEADS)))
    cum = _forget_fwd(proj, b_pad, "forget_fwd")
    cum_t = cum[:, :N_HEADS].T
    fq, fk = cum_t.reshape(N_HEADS, t, 1), cum_t.reshape(N_HEADS, 1, t)
    qn, kn, vb = _qkv_prep(proj, q_norm_g, k_norm_g, "qkv_prep")
    ici = _gather_ici(shards2[:1])
    attn, lse = _attn_fwd(qn, kn, vb, fq, fk, "attn_fwd", sides=[ici])
    pooled, pool_out = _pool_fwd(proj, pw_full, pool_scale, "pool_fwd")
    cat = jnp.concatenate([attn, pool_out], axis=1)
    d2d = _gather_d2d(ici.results)
    y_mix, x2 = _mm_resid(cat, wo_full, x1, gt2, "mix_out", sides=[d2d])
    g_in2, = own_slot(d2d.results, shards2[:1])

    h3 = _norm_mod(x2, ffn2_norm_g, sh3, sc3, "ffn2_norm")
    ici = _gather_ici(shards2[1:])
    a3, b3, u3 = _ffn_up(h3, g_in2, "ffn2_up", sides=[ici])
    g_out2, = own_slot(_exchange(_gather_d2d(ici.results), "gather_ffn2_out_d2d"), shards2[1:])
    wg_out2 = g_out2.reshape(4 * hup, D)
    y3, x3 = _mm_resid(u3, wg_out2, x2, gate3, "ffn2_down")

    loss_part, dx3, d_final_g = _final_loss(x3, final_norm_g.reshape(1, D), tgt, "final_loss")
    loss = lax.psum(loss_part[0, 0], ("x", "y", "c"))

    all_cols = (0, D // 1024)
    dy3, dgt3 = _gate_bwd(dx3, y3, gate3, "ffn2_gate_bwd")
    dab3 = _ffn_dab(dy3, wg_out2, a3, b3, "ffn2_dab")
    parts2 = [_ffn_dwin(h3, dab3, all_cols, "ffn2_dwin"), _mm_tn(u3, dy3, BF16, hup, "ffn2_dwout").reshape(4, hup, D)]
    rs = _reduce_siblings(parts2)
    dh3 = _ffn_dh(dab3, g_in2, "ffn2_dh", sides=[rs])
    half_in2, half_out2 = add_siblings(parts2, rs.results, "ffn2")
    dx2, dsh3, dsc3, d_ng3 = _norm_mod_bwd(dh3, x2, dx3, ffn2_norm_g, sc3, "ffn2_norm_bwd")

    dz, dgt2 = _gate_bwd(dx2, y_mix, gt2, "mix_gate_bwd")
    dcat = _mm_nt(dz, wo_full, BF16, "mix_dcat")
    d_wo = _mm_tn(cat, dz, BF16, 1024, "mix_dwout")
    du_pool, d_pw, d_ps = _pool_bwd(dcat, pooled, pw_full, pool_scale, "pool_bwd")
    rs = _reduce_chips([half_in2])
    dqn, delta, dfq = _attn_bwd_q(qn, kn, vb, attn, dcat, lse, fq, fk, "attn_bwd_q", sides=[rs])
    mine_in2, = add_chips([half_in2], rs.results, "ffn2_in")
    rs, rs3 = _reduce_chips([half_out2]), _share_siblings([mine_in2])
    dkn, dv, dfk = _attn_bwd_kv(qn, kn, vb, dcat, lse, delta, fq, fk, "attn_bwd_kv", sides=[rs, rs3])
    r_in2, = both_halves([mine_in2], rs3.results)
    mine_out2, = add_chips([half_out2], rs.results, "ffn2_out")
    d_cum = jnp.pad((dfq.reshape(N_HEADS, t) + dfk.reshape(N_HEADS, t)).T, ((0, 0), (0, LANE - N_HEADS)))
    dfl, d_bf = _forget_bwd(d_cum, proj, b_pad, "forget_bwd")
    dq, dk, d_qg, d_kg = _qk_norm_bwd(dqn, dkn, proj, q_norm_g, k_norm_g, "qk_norm_bwd")
    dproj = jnp.concatenate([dq, dk, dv, du_pool, dfl], axis=1)
    rs3 = _share_siblings([mine_out2])
    d_wall = _mm_tn(h2, dproj, F32, 1024, "mix_dwin", sides=[rs3])
    r_out2, = both_halves([mine_out2], rs3.results)
    d_wfull = jnp.concatenate([d_wall[:, :nf], d_wall[:, D_PROJ:D_PROJ + N_HEADS], d_wall[:, nf:D_PROJ]], axis=1)
    p_win = jnp.stack([_pad_cols(d_wfull[:, k * ws_in:(k + 1) * ws_in], ws_in_pad) for k in range(4)]).astype(BF16)
    p_pw = d_pw.reshape(4, 4, GROUP_DIM // 4, GROUP_DIM).transpose(1, 0, 2, 3).reshape(4, GROUP_DIM, GROUP_DIM).astype(BF16)
    parts_mix = [p_win, p_pw, d_wo.reshape(4, D // 4, D)]
    rs = _reduce_siblings(parts_mix)
    dh2 = _mm_nt(dproj, w_all, F32, "mix_dh", sides=[rs])
    halves_mix = add_siblings(parts_mix, rs.results, "mix")
    dx1, dsh2, dsc2, d_ng2 = _norm_mod_bwd(dh2, x1, dx2, mix_norm_g, sc2, "mix_norm_bwd")

    dy1, dgt1 = _gate_bwd(dx1, y1, gate1, "ffn1_gate_bwd")
    rs = _reduce_chips(halves_mix)
    d_out1 = _mm_tn(u1, dy1, BF16, hup, "ffn1_dwout", sides=[rs]).reshape(4, hup, D)
    mine_mix = add_chips(halves_mix, rs.results, "mix")
    rs, rs3 = _reduce_siblings([d_out1]), _share_siblings(mine_mix)
    dab1 = _ffn_dab(dy1, wg_out1, a1, b1, "ffn1_dab", sides=[rs, rs3])
    r_win, r_pw, r_wo = both_halves(mine_mix, rs3.results)
    half_out1, = add_siblings([d_out1], rs.results, "ffn1_out")
    rs = _reduce_chips([half_out1])
    d_in1_lo = _ffn_dwin(h1, dab1, (0, 1), "ffn1_dwin_lo", sides=[rs])
    mine_out1, = add_chips([half_out1], rs.results, "ffn1_out")
    rs, rs3 = _reduce_siblings([d_in1_lo]), _share_siblings([mine_out1])
    d_in1_hi = _ffn_dwin(h1, dab1, (1, 2), "ffn1_dwin_hi", sides=[rs, rs3])
    r_out1, = both_halves([mine_out1], rs3.results)
    half_lo, = add_siblings([d_in1_lo], rs.results, "ffn1_lo")
    rs, rs1 = _reduce_chips([half_lo]), _reduce_siblings([d_in1_hi])
    dh1 = _ffn_dh(dab1, g_in1, "ffn1_dh", sides=[rs, rs1])
    mine_lo, = add_chips([half_lo], rs.results, "ffn1_lo")
    half_hi, = add_siblings([d_in1_hi], rs1.results, "ffn1_hi")
    dx0, dsh1, dsc1, d_ng1 = _norm_mod_bwd(dh1, xs, dx1, ffn1_norm_g, sc1, "ffn1_norm_bwd")
    grad_x = dx0.reshape(1, t, D)
    dgt1, dgt3 = 0.5 * dgt1, 0.5 * dgt3
    mine_hi, = add_chips([half_hi], _exchange(_reduce_chips([half_hi]), "ffn1_hi_reduce_chips"), "ffn1_hi")
    r_lo, r_hi = both_halves([mine_lo, mine_hi], _exchange(_share_siblings([mine_lo, mine_hi]), "ffn1_share_siblings"))
    r_in1 = jnp.concatenate([r_lo, r_hi], axis=1)
    grads = {
        "ffn1_w_in": jnp.concatenate([r_in1[:hu], r_in1[hup:hup + hu]], axis=0),
        "ffn1_w_out": r_out1[:hu],
        "w_in": r_win[:, :ws_in],
        "pool_w": r_pw,
        "w_out": r_wo,
        "ffn2_w_in": jnp.concatenate([r_in2[:hu], r_in2[hup:hup + hu]], axis=0),
        "ffn2_w_out": r_out2[:hu],
    }
    hidden_in_rows = ("ffn1_w_in", "ffn2_w_in")

    dmod = jnp.concatenate([dsh1, dsc1, dgt1, dsh2, dsc2, dgt2, dsh3, dsc3, dgt3], axis=1)
    small_names = ["b_ada", "ffn1_norm_g", "mix_norm_g", "b_forget", "q_norm_g", "k_norm_g", "pool_scale", "ffn2_norm_g",
                   "final_norm_g"]
    small_grads = [dmod, d_ng1, d_ng2, d_bf[:, :N_HEADS], d_qg, d_kg, d_ps, d_ng3, d_final_g]
    small_w = [b_ada, ffn1_norm_g, mix_norm_g, b_forget, q_norm_g, k_norm_g, pool_scale, ffn2_norm_g, final_norm_g.reshape(1, D)]
    small_m = [m_b_ada, m_ffn1_norm_g, m_mix_norm_g, m_b_forget, m_q_norm_g, m_k_norm_g, m_pool_scale, m_ffn2_norm_g,
               m_final_norm_g.reshape(1, D)]
    small_v = [v_b_ada, v_ffn1_norm_g, v_mix_norm_g, v_b_forget, v_q_norm_g, v_k_norm_g, v_pool_scale, v_ffn2_norm_g,
               v_final_norm_g.reshape(1, D)]
    sizes = [g.shape[1] for g in small_grads]
    n_small = sum(sizes)
    n_pack = -(-n_small // (8 * LANE)) * (8 * LANE)

    def pack(vs, fill):
        flat = jnp.concatenate([v.reshape(1, -1) for v in vs], axis=1)
        return jnp.pad(flat, ((0, 0), (0, n_pack - n_small)), constant_values=fill).reshape(8, n_pack // 8)

    g8 = _allgather_small(pack(small_grads, 0.0), True, "gather_small_grads")
    gs, ds, ms, vs = _adamw_small(pack(small_w, 0.0), g8, pack(small_m, 0.0), pack(small_v, 1.0), "adamw_small")

    def unpack(p):
        flat = p.reshape(1, n_pack)
        out, off = {}, 0
        for nme, sz in zip(small_names, sizes):
            out[nme] = flat[:, off:off + sz]
            off += sz
        return out

    small = [unpack(p) for p in (gs, ds, ms, vs)]
    for dct in small:
        dct["final_norm_g"] = dct["final_norm_g"].reshape(D)

    dmod_all = g8.reshape(8, n_pack)[:, :N_MOD * D]
    dmod_mine = lax.dynamic_slice(dmod_all, (0, chip * n_ada), (8, n_ada))
    grads["w_ada"] = _ada_bwd(act16[:8].reshape(8, D, 1), dmod_mine.reshape(8, 1, n_ada), "ada_bwd")

    big = {"w_ada": (w_ada, m_w_ada, v_w_ada), "ffn1_w_in": (ffn1_w_in, m_ffn1_w_in, v_ffn1_w_in),
           "ffn1_w_out": (ffn1_w_out, m_ffn1_w_out, v_ffn1_w_out), "w_in": (w_in, m_w_in, v_w_in),
           "pool_w": (pool_w, m_pool_w, v_pool_w), "w_out": (w_out, m_w_out, v_w_out),
           "ffn2_w_in": (ffn2_w_in, m_ffn2_w_in, v_ffn2_w_in), "ffn2_w_out": (ffn2_w_out, m_ffn2_w_out, v_ffn2_w_out)}
    res = {}

    for nme, (w, m, v) in big.items():
        shp = w.shape
        g2 = grads[nme]
        if nme in hidden_in_rows:
            d, mo, vo = _adamw(w[0].T, g2, m[0].T, v[0].T, f"adamw_{nme}")
            res[nme] = tuple(o.T.reshape(shp) for o in (g2, d, mo, vo))
            continue
        two = g2.shape
        d, mo, vo = _adamw(w.reshape(two), g2, m.reshape(two), v.reshape(two), f"adamw_{nme}")
        res[nme] = (g2.reshape(shp), d.reshape(shp), mo.reshape(shp), vo.reshape(shp))
    for nme in small_names:
        res[nme] = tuple(dct[nme] for dct in small)

    order = ["w_ada", "b_ada", "ffn1_norm_g", "ffn1_w_in", "ffn1_w_out", "mix_norm_g", "w_in", "b_forget", "q_norm_g",
             "k_norm_g", "pool_w", "pool_scale", "w_out", "ffn2_norm_g", "ffn2_w_in", "ffn2_w_out", "final_norm_g"]
    return (loss, grad_x, *[res[n][0] for n in order], *[res[n][1] for n in order], *[res[n][2] for n in order],
            *[res[n][3] for n in order])
```

```python
import functools

import jax
import jax.numpy as jnp
from jax import lax
from jax.experimental import pallas as pl
from jax.experimental.pallas import tpu as pltpu

F32 = jnp.float32
BF16 = jnp.bfloat16

D = 2048
N_HEADS = 8
HEAD_DIM = 128
D_ATTN = 1024
D_POOL = 1024
POOL_WINDOWS = (2, 4, 8, 16)
GROUP_DIM = 256
HALO = 16
N_MOD = 9
EPS = 1e-6
D_PROJ = 3 * D_ATTN + D_POOL
D_PROJ_PAD = D_PROJ + 128
LANE = 128
ATTN_BLOCK = 512
POOL_BLOCK = 512
ATTN_HEADS = (2, 2, 2)
ATTN_ROW_CHUNKS = (2, 2, 2)

ADAM_LR = 0.001
ADAM_B1 = 0.9
ADAM_B2 = 0.999
ADAM_EPS = 1e-08
ADAM_WD = 0.01
ADAM_STEP = 10

VMEM_LIMIT_V7X = 56 * 1024 * 1024
MESH_ID = pl.DeviceIdType.MESH
ANY = pl.BlockSpec(memory_space=pl.ANY)
VMEM = pl.BlockSpec(memory_space=pltpu.VMEM)

NT = (((1,), (1,)), ((), ()))
TN = (((0,), (0,)), ((), ()))


class _Side:
    def __init__(self, ins, outs, nsem, start, wait, alias=False):
        self.ins, self.outs, self.nsem, self.start, self.wait, self.alias = list(ins), list(outs), nsem, start, wait, alias
        self.results = None


def _pcall(body, *, name, out_shape, grid=None, in_specs=None, out_specs=None, scratch=(), sem=None, prefetch=0, sides=()):
    sides = list(sides)
    single = not isinstance(out_shape, (list, tuple))
    shapes = [out_shape] if single else list(out_shape)
    in_specs = list(in_specs)
    ospecs = [out_specs] if single else list(out_specs)
    scratch = list(scratch)
    n_in, n_out, n_scr = len(in_specs), len(shapes), len(scratch)
    assert not (sides and prefetch)
    aliases = {}
    for sd in sides:
        if sd.alias:
            for k in range(len(sd.ins)):
                aliases[len(in_specs) + k] = len(shapes) + k
        in_specs += [ANY] * len(sd.ins)
        shapes += sd.outs
        ospecs += [ANY] * len(sd.outs)
        scratch += [pltpu.SemaphoreType.DMA((sd.nsem,)), pltpu.SemaphoreType.DMA((sd.nsem,))]

    def wrapped(*refs):
        ins, outs, scr = refs[:len(in_specs)], refs[len(in_specs):len(in_specs) + len(shapes)], refs[len(in_specs) + len(shapes):]
        first = last = None
        for ax, g in enumerate(grid or ()):
            p = pl.program_id(ax)
            first = (p == 0) if first is None else first & (p == 0)
            last = (p == g - 1) if last is None else last & (p == g - 1)

        def each(what):
            i0, o0 = n_in, n_out
            for k, sd in enumerate(sides):
                getattr(sd, what)(ins[i0:i0 + len(sd.ins)], outs[o0:o0 + len(sd.outs)], scr[n_scr + 2 * k], scr[n_scr + 2 * k + 1])
                i0, o0 = i0 + len(sd.ins), o0 + len(sd.outs)

        if first is None:
            each("start")
        else:
            pl.when(first)(lambda: each("start"))
        body(*ins[:n_in], *outs[:n_out], *scr[:n_scr])
        if last is None:
            each("wait")
        else:
            pl.when(last)(lambda: each("wait"))

    params = dict(vmem_limit_bytes=VMEM_LIMIT_V7X)
    if sides and grid:
        params["dimension_semantics"] = ("arbitrary",) * len(grid)
    elif sem is not None:
        params["dimension_semantics"] = sem
    kw = dict(name=name, out_shape=shapes if (sides or not single) else shapes[0], compiler_params=pltpu.CompilerParams(**params))
    if aliases:
        kw["input_output_aliases"] = aliases
    final_ospecs = ospecs if (sides or not single) else ospecs[0]
    if prefetch:
        kw["grid_spec"] = pltpu.PrefetchScalarGridSpec(
            num_scalar_prefetch=prefetch, grid=grid, in_specs=in_specs, out_specs=final_ospecs, scratch_shapes=scratch)
    else:
        if grid is not None:
            kw["grid"] = grid
        kw["in_specs"] = in_specs
        kw["out_specs"] = final_ospecs
        kw["scratch_shapes"] = scratch
    call = pl.pallas_call(wrapped if sides else body, **kw)
    if not sides:
        return call

    def run(*operands):
        res = list(call(*operands, *[a for sd in sides for a in sd.ins]))
        o0 = n_out
        for sd in sides:
            sd.results = res[o0:o0 + len(sd.outs)]
            o0 += len(sd.outs)
        return res[0] if single else res[:n_out]

    return run


def _pick(n, cap, mult):
    best = None
    for d in range(mult, min(n, cap) + 1, mult):
        if n % d == 0:
            best = d
    assert best is not None, (n, cap, mult)
    return best


def _coords():
    return lax.axis_index("x"), lax.axis_index("y"), lax.axis_index("c")


def _dot(a, b, dims=None):
    if dims is None:
        return jnp.dot(a, b, preferred_element_type=F32)
    return lax.dot_general(a, b, dims, preferred_element_type=F32)


def _remote(src, dst, ssem, rsem, dev):
    return pltpu.make_async_remote_copy(src_ref=src, dst_ref=dst, send_sem=ssem, recv_sem=rsem,
                                        device_id=dev, device_id_type=MESH_ID)


def _allgather_small(v, whole_mesh, name):
    masks = list(range(1, 8)) if whole_mesh else [4, 2, 6]
    nslot = 8 if whole_mesh else 4

    def slot(px, py, pc):
        return 4 * px + 2 * py + pc if whole_mesh else 2 * px + py

    def body(v_ref, out_ref, ssem, rsem, lsem):
        x, y, c = _coords()
        mine = slot(x, y, c)
        peers = [(jnp.bitwise_xor(x, (m >> 2) & 1), jnp.bitwise_xor(y, (m >> 1) & 1), jnp.bitwise_xor(c, m & 1))
                 for m in masks]
        loc = pltpu.make_async_copy(v_ref, out_ref.at[mine], lsem)
        loc.start()
        sends = [_remote(v_ref, out_ref.at[mine], ssem.at[k], rsem.at[k], p) for k, p in enumerate(peers)]
        for cp in sends:
            cp.start()
        for k, p in enumerate(peers):
            _remote(v_ref, out_ref.at[slot(*p)], ssem.at[k], rsem.at[k], p).wait_recv()
        for cp in sends:
            cp.wait_send()
        loc.wait()

    return _pcall(body, name=name, out_shape=jax.ShapeDtypeStruct((nslot,) + v.shape, v.dtype),
                  in_specs=[VMEM], out_specs=VMEM,
                  scratch=[pltpu.SemaphoreType.DMA((len(masks),)), pltpu.SemaphoreType.DMA((len(masks),)),
                           pltpu.SemaphoreType.DMA(())])(v)


def _other_chips(x, y):
    return [(1 - x, y), (x, 1 - y), (1 - x, 1 - y)]


def _half_rows(shard_rows, core):
    h = shard_rows // 2
    return pl.ds(pl.multiple_of(core * h, 16), h)


def _later(src, dst, ssem, rsem, dev):
    return functools.partial(_remote, src, dst, ssem, rsem, dev)


def _side_from(ins, outs, nsem, pairs_of, alias=False):
    def start(*refs):
        for send, _ in pairs_of(*refs):
            send().start()

    def wait(*refs):
        pairs = pairs_of(*refs)
        for _, recv in pairs:
            recv().wait_recv()
        for send, _ in pairs:
            send().wait_send()

    return _Side(ins, outs, nsem, start, wait, alias)


def _gather_ici(ws):
    def pairs_of(w_refs, g_refs, ssem, rsem):
        x, y, c = _coords()
        me, out = 2 * x + y, []
        for a, w in enumerate(ws):
            rows = _half_rows(w.shape[0], c)
            for j, chip in enumerate(_other_chips(x, y)):
                sems = (ssem.at[3 * a + j], rsem.at[3 * a + j], (*chip, c))
                got = g_refs[a].at[2 * chip[0] + chip[1], rows]
                out.append((_later(w_refs[a].at[rows], g_refs[a].at[me, rows], *sems), _later(got, got, *sems)))
        return out

    return _side_from(ws, [jax.ShapeDtypeStruct((4,) + w.shape, w.dtype) for w in ws], 3 * len(ws), pairs_of)


def _gather_d2d(gs):
    def pairs_of(_, g_refs, ssem, rsem):
        x, y, c = _coords()
        out = []
        for a, g in enumerate(gs):
            for j, chip in enumerate(_other_chips(x, y)):
                sems = (ssem.at[3 * a + j], rsem.at[3 * a + j], (x, y, 1 - c))
                mine = g_refs[a].at[2 * chip[0] + chip[1], _half_rows(g.shape[1], c)]
                theirs = g_refs[a].at[2 * chip[0] + chip[1], _half_rows(g.shape[1], 1 - c)]
                out.append((_later(mine, mine, *sems), _later(theirs, theirs, *sems)))
        return out

    return _side_from(gs, [jax.ShapeDtypeStruct(g.shape, g.dtype) for g in gs], 3 * len(gs), pairs_of, alias=True)


def _reduce_siblings(ps):
    def pairs_of(p_refs, r_refs, ssem, rsem):
        x, y, c = _coords()
        out = []
        for a, p in enumerate(ps):
            src = p_refs[a].at[:, _half_rows(p.shape[1], 1 - c), :]
            cp = _later(src, r_refs[a], ssem.at[a], rsem.at[a], (x, y, 1 - c))
            out.append((cp, cp))
        return out

    return _side_from(ps, [jax.ShapeDtypeStruct((4, p.shape[1] // 2, p.shape[2]), p.dtype) for p in ps], len(ps), pairs_of)


def _reduce_chips(hs, dests=(0, 1, 2)):
    nd = len(dests)

    def pairs_of(h_refs, o_refs, ssem, rsem):
        x, y, c = _coords()
        chips = _other_chips(x, y)
        out = []
        for a in range(len(hs)):
            for slot, j in enumerate(dests):
                chip = chips[j]
                cp = _later(h_refs[a].at[2 * chip[0] + chip[1]], o_refs[a].at[slot], ssem.at[nd * a + slot], rsem.at[nd * a + slot],
                            (*chip, c))
                out.append((cp, cp))
        return out

    return _side_from(hs, [jax.ShapeDtypeStruct((nd,) + h.shape[1:], h.dtype) for h in hs], nd * len(hs), pairs_of)


def _share_siblings(gs):
    def pairs_of(g_refs, o_refs, ssem, rsem):
        x, y, c = _coords()
        cps = [_later(g_refs[a], o_refs[a], ssem.at[a], rsem.at[a], (x, y, 1 - c)) for a in range(len(gs))]
        return [(cp, cp) for cp in cps]

    return _side_from(gs, [jax.ShapeDtypeStruct(g.shape, g.dtype) for g in gs], len(gs), pairs_of)


def _exchange(side, name):
    def body():
        pass

    _pcall(body, name=name, out_shape=[], in_specs=[], out_specs=[], sides=[side])()
    return side.results


def _add_sibling(p, r, core, name):
    _, rr, cc = p.shape
    h = rr // 2
    th = _pick(h, max(16, (2 << 20) // (2 * cc)), 16)
    nb = h // th

    def body(c_ref, p_ref, r_ref, o_ref):
        o_ref[...] = (p_ref[...].astype(F32) + r_ref[...].astype(F32)).astype(BF16)

    return _pcall(body, name=name, out_shape=jax.ShapeDtypeStruct((4, h, cc), BF16), grid=(4, nb),
                  in_specs=[pl.BlockSpec((None, th, cc), lambda k, i, c_ref: (k, c_ref[0] * nb + i, 0)),
                            pl.BlockSpec((None, th, cc), lambda k, i, c_ref: (k, i, 0))],
                  out_specs=pl.BlockSpec((None, th, cc), lambda k, i, c_ref: (k, i, 0)),
                  sem=("parallel", "parallel"), prefetch=1)(core, p, r)


def _add_chips(hh, pieces, chip, name):
    _, h, cc = hh.shape
    th = _pick(h, max(16, (2 << 20) // (2 * cc)), 16)

    def body(k_ref, h_ref, r0_ref, r1_ref, r2_ref, o_ref):
        s = h_ref[...].astype(F32) + r0_ref[...].astype(F32)
        s = s + r1_ref[...].astype(F32)
        o_ref[...] = s + r2_ref[...].astype(F32)

    def piece(slot):
        return pl.BlockSpec((None, th, cc), lambda i, k_ref: (slot, i, 0))

    return _pcall(body, name=name, out_shape=jax.ShapeDtypeStruct((h, cc), F32), grid=(h // th,),
                  in_specs=[pl.BlockSpec((None, th, cc), lambda i, k_ref: (k_ref[0], i, 0))] + [piece(s) for _, s in pieces],
                  out_specs=pl.BlockSpec((th, cc), lambda i, k_ref: (i, 0)),
                  sem=("parallel",), prefetch=1)(chip, hh, *[a for a, _ in pieces])


def _adamw_math(w, g, m, v):
    m = ADAM_B1 * m + (1.0 - ADAM_B1) * g
    v = ADAM_B2 * v + (1.0 - ADAM_B2) * (g * g)
    m_hat = m / (1.0 - ADAM_B1 ** ADAM_STEP)
    v_hat = v / (1.0 - ADAM_B2 ** ADAM_STEP)
    delta = -ADAM_LR * (m_hat / (jnp.sqrt(v_hat) + ADAM_EPS) + ADAM_WD * w)
    return delta, m, v


def _adamw(w, g, m, v, name, sides=()):
    rr, cc = w.shape
    tr = _pick(rr, max(8, (3 << 20) // (4 * cc)), 8)

    def body(w_ref, g_ref, m_ref, v_ref, d_ref, mo_ref, vo_ref):
        d, mm, vv = _adamw_math(w_ref[...], g_ref[...], m_ref[...], v_ref[...])
        d_ref[...] = d
        mo_ref[...] = mm
        vo_ref[...] = vv

    spec = pl.BlockSpec((tr, cc), lambda i: (i, 0))
    return _pcall(body, name=name, out_shape=[jax.ShapeDtypeStruct(w.shape, F32)] * 3, grid=(rr // tr,),
                  in_specs=[spec] * 4, out_specs=[spec] * 3, sem=("parallel",), sides=sides)(w, g, m, v)


def _adamw_small(w, g8, m, v, name):
    def body(w_ref, g_ref, m_ref, v_ref, go_ref, d_ref, mo_ref, vo_ref):
        g = g_ref[0]
        for k in range(1, 8):
            g = g + g_ref[k]
        d, mm, vv = _adamw_math(w_ref[...], g, m_ref[...], v_ref[...])
        go_ref[...] = g
        d_ref[...] = d
        mo_ref[...] = mm
        vo_ref[...] = vv

    return _pcall(body, name=name, out_shape=[jax.ShapeDtypeStruct(w.shape, F32)] * 4,
                  in_specs=[VMEM] * 4, out_specs=[VMEM] * 4)(w, g8, m, v)


def _ada_fwd(c16, w_ada, b_ada, name):
    n = w_ada.shape[1]
    tn = _pick(n, 512, LANE)

    def body(c_ref, w_ref, b_ref, act_ref, mod_ref):
        cv = c_ref[...]
        act = cv * jax.nn.sigmoid(cv)
        act_ref[...] = act
        mod_ref[...] = _dot(act.astype(BF16), w_ref[...].astype(BF16)) + b_ref[...]

    return _pcall(body, name=name,
                  out_shape=[jax.ShapeDtypeStruct((16, D), F32), jax.ShapeDtypeStruct((16, n), F32)], grid=(n // tn,),
                  in_specs=[pl.BlockSpec((16, D), lambda j: (0, 0)), pl.BlockSpec((D, tn), lambda j: (0, j)),
                            pl.BlockSpec((1, tn), lambda j: (0, j))],
                  out_specs=[pl.BlockSpec((16, D), lambda j: (0, 0)), pl.BlockSpec((16, tn), lambda j: (0, j))],
                  sem=("arbitrary",))(c16, w_ada, b_ada)


def _ada_bwd(act, dmod, name, sides=()):
    n = dmod.shape[2]
    tm, tn = 256, _pick(n, 512, LANE)

    def body(a_ref, d_ref, o_ref):
        def term(b):
            return a_ref[b].astype(BF16).astype(F32) * d_ref[b].astype(BF16).astype(F32)

        acc = term(0)
        for b in range(1, 8):
            acc = acc + term(b)
        o_ref[...] = acc

    return _pcall(body, name=name, out_shape=jax.ShapeDtypeStruct((D, n), F32), grid=(D // tm, n // tn),
                  in_specs=[pl.BlockSpec((8, tm, 1), lambda i, j: (0, i, 0)), pl.BlockSpec((8, 1, tn), lambda i, j: (0, 0, j))],
                  out_specs=pl.BlockSpec((tm, tn), lambda i, j: (i, j)), sem=("parallel", "parallel"), sides=sides)(act, dmod)


def _norm_mod(x, g, sh, sc, name, sides=()):
    t = x.shape[0]
    tm = _pick(t, 512, 8)

    def body(x_ref, g_ref, sh_ref, sc_ref, h_ref):
        xf = x_ref[...]
        r = lax.rsqrt(jnp.mean(xf * xf, axis=-1, keepdims=True) + EPS)
        h = (xf * r) * g_ref[...]
        h_ref[...] = (h * (1.0 + sc_ref[...]) + sh_ref[...]).astype(BF16)

    vec = pl.BlockSpec((1, D), lambda i: (0, 0))
    row = pl.BlockSpec((tm, D), lambda i: (i, 0))
    return _pcall(body, name=name, out_shape=jax.ShapeDtypeStruct((t, D), BF16), grid=(t // tm,),
                  in_specs=[row, vec, vec, vec], out_specs=row, sem=("parallel",), sides=sides)(x, g, sh, sc)


def _norm_mod_bwd(dh, x, dxo, g, sc, name, sides=()):
    t = x.shape[0]
    tm = _pick(t, 256, 8)

    def body(dh_ref, x_ref, dxo_ref, g_ref, sc_ref, dx_ref, dsh_ref, dsc_ref, dg_ref):
        @pl.when(pl.program_id(0) == 0)
        def _():
            dsh_ref[...] = jnp.zeros_like(dsh_ref)
            dsc_ref[...] = jnp.zeros_like(dsc_ref)
            dg_ref[...] = jnp.zeros_like(dg_ref)

        xf, dh_ = x_ref[...], dh_ref[...]
        r = lax.rsqrt(jnp.mean(xf * xf, axis=-1, keepdims=True) + EPS)
        xhat = xf * r
        dsh_ref[...] += jnp.sum(dh_, axis=0, keepdims=True)
        dsc_ref[...] += jnp.sum(dh_ * (xhat * g_ref[...]), axis=0, keepdims=True)
        tt = dh_ * (1.0 + sc_ref[...])
        dg_ref[...] += jnp.sum(tt * xhat, axis=0, keepdims=True)
        dxh = tt * g_ref[...]
        dx_ref[...] = r * (dxh - xhat * jnp.mean(dxh * xhat, axis=-1, keepdims=True)) + dxo_ref[...]

    vec = pl.BlockSpec((1, D), lambda i: (0, 0))
    row = pl.BlockSpec((tm, D), lambda i: (i, 0))
    vshape = jax.ShapeDtypeStruct((1, D), F32)
    return _pcall(body, name=name, out_shape=[jax.ShapeDtypeStruct((t, D), F32), vshape, vshape, vshape], grid=(t // tm,),
                  in_specs=[row, row, row, vec, vec], out_specs=[row, vec, vec, vec], sem=("arbitrary",), sides=sides)(dh, x, dxo, g, sc)


def _gate_bwd(dxo, yy, gate, name):
    t = dxo.shape[0]
    tm = _pick(t, 512, 8)

    def body(dx_ref, y_ref, g_ref, dy_ref, dg_ref):
        @pl.when(pl.program_id(0) == 0)
        def _():
            dg_ref[...] = jnp.zeros_like(dg_ref)

        dx = dx_ref[...]
        dy_ref[...] = (dx * g_ref[...]).astype(BF16)
        dg_ref[...] += jnp.sum(dx * y_ref[...], axis=0, keepdims=True)

    vec = pl.BlockSpec((1, D), lambda i: (0, 0))
    row = pl.BlockSpec((tm, D), lambda i: (i, 0))
    return _pcall(body, name=name, out_shape=[jax.ShapeDtypeStruct((t, D), BF16), jax.ShapeDtypeStruct((1, D), F32)],
                  grid=(t // tm,), in_specs=[row, row, vec], out_specs=[row, vec], sem=("arbitrary",))(dxo, yy, gate)


def _final_loss(x, g, tgt, name):
    t = x.shape[0]
    tm = _pick(t, 256, 8)

    def body(x_ref, g_ref, t_ref, loss_ref, dx_ref, dg_ref):
        @pl.when(pl.program_id(0) == 0)
        def _():
            loss_ref[...] = jnp.zeros_like(loss_ref)
            dg_ref[...] = jnp.zeros_like(dg_ref)

        xf = x_ref[...]
        r = lax.rsqrt(jnp.mean(xf * xf, axis=-1, keepdims=True) + EPS)
        xhat = xf * r
        e = xhat * g_ref[...] - t_ref[...]
        per_tok = jnp.mean(e * e, axis=-1, keepdims=True)
        loss_ref[...] += 0.5 * jnp.sum(per_tok, axis=0, keepdims=True)
        dy = e * (1.0 / D)
        dg_ref[...] += jnp.sum(dy * xhat, axis=0, keepdims=True)
        dxh = dy * g_ref[...]
        dx_ref[...] = r * (dxh - xhat * jnp.mean(dxh * xhat, axis=-1, keepdims=True))

    vec = pl.BlockSpec((1, D), lambda i: (0, 0))
    row = pl.BlockSpec((tm, D), lambda i: (i, 0))
    return _pcall(body, name=name,
                  out_shape=[jax.ShapeDtypeStruct((1, LANE), F32), jax.ShapeDtypeStruct((t, D), F32),
                             jax.ShapeDtypeStruct((1, D), F32)],
                  grid=(t // tm,), in_specs=[row, vec, row],
                  out_specs=[pl.BlockSpec((1, LANE), lambda i: (0, 0)), row, vec], sem=("arbitrary",))(x, g, tgt)


def _mm(a, b, out_dtype, name):
    m, k = a.shape
    n = b.shape[1]
    tm = _pick(m, 1024, 8)
    tn = _pick(n, 1408, LANE)

    def body(a_ref, b_ref, o_ref):
        o_ref[...] = _dot(a_ref[...], b_ref[...]).astype(out_dtype)

    return _pcall(body, name=name, out_shape=jax.ShapeDtypeStruct((m, n), out_dtype), grid=(m // tm, n // tn),
                  in_specs=[pl.BlockSpec((tm, k), lambda i, j: (i, 0)), pl.BlockSpec((k, tn), lambda i, j: (0, j))],
                  out_specs=pl.BlockSpec((tm, tn), lambda i, j: (i, j)), sem=("parallel", "parallel"))(a, b)


def _mm_resid(a, b, resid, gate, name, sides=()):
    m, k = a.shape
    n = b.shape[1]
    tm, tn, tk = _pick(m, 1024, 8), _pick(n, 1024, LANE), _pick(k, 1408, LANE)
    nk = k // tk

    def body(a_ref, b_ref, r_ref, g_ref, y_ref, o_ref, acc):
        kk = pl.program_id(2)

        @pl.when(kk == 0)
        def _():
            acc[...] = jnp.zeros_like(acc)

        acc[...] += _dot(a_ref[...], b_ref[...])

        @pl.when(kk == nk - 1)
        def _():
            y_ref[...] = acc[...].astype(BF16)
            o_ref[...] = r_ref[...] + g_ref[...] * acc[...]

    blk = pl.BlockSpec((tm, tn), lambda i, j, kk: (i, j))
    return _pcall(body, name=name, out_shape=[jax.ShapeDtypeStruct((m, n), BF16), jax.ShapeDtypeStruct((m, n), F32)],
                  grid=(m // tm, n // tn, nk),
                  in_specs=[pl.BlockSpec((tm, tk), lambda i, j, kk: (i, kk)), pl.BlockSpec((tk, tn), lambda i, j, kk: (kk, j)),
                            blk, pl.BlockSpec((1, tn), lambda i, j, kk: (0, j))],
                  out_specs=[blk, blk], scratch=[pltpu.VMEM((tm, tn), F32)],
                  sem=("parallel", "parallel", "arbitrary"), sides=sides)(a, b, resid, gate)


def _mm_nt(a, b, out_dtype, name, sides=()):
    m, k = a.shape
    n = b.shape[0]
    tm, tn, tk = _pick(m, 1024, 8), _pick(n, 1024, LANE), _pick(k, 2816, LANE)
    nk = k // tk

    def body(a_ref, b_ref, o_ref, acc):
        kk = pl.program_id(2)

        @pl.when(kk == 0)
        def _():
            acc[...] = jnp.zeros_like(acc)

        acc[...] += _dot(a_ref[...], b_ref[...], NT)

        @pl.when(kk == nk - 1)
        def _():
            o_ref[...] = acc[...].astype(out_dtype)

    return _pcall(body, name=name, out_shape=jax.ShapeDtypeStruct((m, n), out_dtype), grid=(m // tm, n // tn, nk),
                  in_specs=[pl.BlockSpec((tm, tk), lambda i, j, kk: (i, kk)), pl.BlockSpec((tn, tk), lambda i, j, kk: (j, kk))],
                  out_specs=pl.BlockSpec((tm, tn), lambda i, j, kk: (i, j)), scratch=[pltpu.VMEM((tm, tn), F32)],
                  sem=("parallel", "parallel", "arbitrary"), sides=sides)(a, b)


def _mm_tn(a, b, out_dtype, tm_cap, name, sides=()):
    t, m = a.shape
    n = b.shape[1]
    tm, tn, tk = _pick(m, tm_cap, LANE), _pick(n, 1408, LANE), _pick(t, 2048, 16)
    nk = t // tk

    def body(a_ref, b_ref, o_ref, acc):
        kk = pl.program_id(2)

        @pl.when(kk == 0)
        def _():
            acc[...] = jnp.zeros_like(acc)

        acc[...] += _dot(a_ref[...], b_ref[...], TN)

        @pl.when(kk == nk - 1)
        def _():
            o_ref[...] = acc[...].astype(out_dtype)

    return _pcall(body, name=name, out_shape=jax.ShapeDtypeStruct((m, n), out_dtype), grid=(m // tm, n // tn, nk),
                  in_specs=[pl.BlockSpec((tk, tm), lambda i, j, kk: (kk, i)), pl.BlockSpec((tk, tn), lambda i, j, kk: (kk, j))],
                  out_specs=pl.BlockSpec((tm, tn), lambda i, j, kk: (i, j)), scratch=[pltpu.VMEM((tm, tn), F32)],
                  sem=("parallel", "parallel", "arbitrary"), sides=sides)(a, b)


def _ffn_up(h, wg, name, sides=()):
    t = h.shape[0]
    fp = wg.shape[2]
    tm, tn = _pick(t, 2048, 8), 256
    nn = fp // tn

    def body(h_ref, wa_ref, wb_ref, a_ref, b_ref, u_ref):
        hh = h_ref[...]
        a = _dot(hh, wa_ref[...])
        b = _dot(hh, wb_ref[...])
        a_ref[...] = a.astype(BF16)
        b_ref[...] = b.astype(BF16)
        u_ref[...] = (a * jax.nn.sigmoid(a) * b).astype(BF16)

    out = pl.BlockSpec((tm, tn), lambda i, j, n: (i, j * nn + n))
    return _pcall(body, name=name, out_shape=[jax.ShapeDtypeStruct((t, 2 * fp), BF16)] * 3, grid=(t // tm, 2, nn),
                  in_specs=[pl.BlockSpec((tm, D), lambda i, j, n: (i, 0)),
                            pl.BlockSpec((None, D, tn), lambda i, j, n: (j, 0, n)),
                            pl.BlockSpec((None, D, tn), lambda i, j, n: (j + 2, 0, n))],
                  out_specs=[out, out, out], sem=("parallel", "parallel", "parallel"), sides=sides)(h, wg, wg)


def _ffn_dab(dy, wo, a, b, name, sides=()):
    t = dy.shape[0]
    f2 = wo.shape[0]
    tm, tn = _pick(t, 2048, 8), 256

    def body(dy_ref, w_ref, a_ref, b_ref, o_ref):
        du = _dot(dy_ref[...], w_ref[...], NT)
        av, bv = a_ref[...].astype(F32), b_ref[...].astype(F32)
        sg = jax.nn.sigmoid(av)
        o_ref[0] = (du * bv * (sg * (1.0 + av * (1.0 - sg)))).astype(BF16)
        o_ref[1] = (du * (av * sg)).astype(BF16)

    blk = pl.BlockSpec((tm, tn), lambda i, n: (i, n))
    return _pcall(body, name=name, out_shape=jax.ShapeDtypeStruct((2, t, f2), BF16), grid=(t // tm, f2 // tn),
                  in_specs=[pl.BlockSpec((tm, D), lambda i, n: (i, 0)), pl.BlockSpec((tn, D), lambda i, n: (n, 0)), blk, blk],
                  out_specs=pl.BlockSpec((2, tm, tn), lambda i, n: (0, i, n)), sem=("parallel", "parallel"), sides=sides)(dy, wo, a, b)


def _ffn_dwin(h, dab, col_blocks, name, sides=()):
    t = h.shape[0]
    fp = dab.shape[2] // 2
    tm, tn, tk = _pick(fp, 1408, LANE), 1024, _pick(t, 2048, 16)
    nm, nk = fp // tm, t // tk
    n0, n1 = col_blocks

    def body(d_ref, h_ref, o_ref, acc):
        kk = pl.program_id(3)

        @pl.when(kk == 0)
        def _():
            acc[...] = jnp.zeros_like(acc)

        acc[...] += _dot(d_ref[...], h_ref[...], TN)

        @pl.when(kk == nk - 1)
        def _():
            o_ref[...] = acc[...].astype(BF16)

    return _pcall(body, name=name, out_shape=jax.ShapeDtypeStruct((4, fp, (n1 - n0) * tn), BF16), grid=(4, nm, n1 - n0, nk),
                  in_specs=[pl.BlockSpec((None, tk, tm), lambda u, i, n, kk: (u // 2, kk, (u % 2) * nm + i)),
                            pl.BlockSpec((tk, tn), lambda u, i, n, kk: (kk, n0 + n))],
                  out_specs=pl.BlockSpec((None, tm, tn), lambda u, i, n, kk: (u, i, n)),
                  scratch=[pltpu.VMEM((tm, tn), F32)],
                  sem=("parallel", "parallel", "parallel", "arbitrary"), sides=sides)(dab, h)


def _ffn_dh(dab, wg, name, sides=()):
    t = dab.shape[1]
    fp = wg.shape[2]
    tm, tn, tk = _pick(t, 1024, 8), 1024, _pick(fp, 2816, LANE)
    nkk = fp // tk

    def body(d_ref, w_ref, o_ref, acc):
        u, kk = pl.program_id(2), pl.program_id(3)

        @pl.when((u == 0) & (kk == 0))
        def _():
            acc[...] = jnp.zeros_like(acc)

        acc[...] += _dot(d_ref[...], w_ref[...], NT)

        @pl.when((u == 3) & (kk == nkk - 1))
        def _():
            o_ref[...] = acc[...]

    return _pcall(body, name=name, out_shape=jax.ShapeDtypeStruct((t, D), F32), grid=(t // tm, D // tn, 4, nkk),
                  in_specs=[pl.BlockSpec((None, tm, tk), lambda i, j, u, kk: (u // 2, i, (u % 2) * nkk + kk)),
                            pl.BlockSpec((None, tn, tk), lambda i, j, u, kk: (u, j, kk))],
                  out_specs=pl.BlockSpec((tm, tn), lambda i, j, u, kk: (i, j)), scratch=[pltpu.VMEM((tm, tn), F32)],
                  sem=("parallel", "parallel", "arbitrary", "arbitrary"), sides=sides)(dab, wg)


def _split3(v):
    hi = v.astype(BF16)
    r1 = v - hi.astype(F32)
    mid = r1.astype(BF16)
    lo = (r1 - mid.astype(F32)).astype(BF16)
    return hi, mid, lo


def _tri_sum(tri, v):
    hi, mid, lo = _split3(v)
    return (_dot(tri, hi) + _dot(tri, mid)) + _dot(tri, lo)


def _forget_fwd(proj, b_pad, name):
    t = proj.shape[0]
    tb = _pick(t, 256, 8)
    col = D_PROJ // LANE

    def body(f_ref, b_ref, o_ref, carry):
        @pl.when(pl.program_id(0) == 0)
        def _():
            carry[...] = jnp.zeros_like(carry)

        z = f_ref[...] + b_ref[...]
        lf = jnp.minimum(z, 0.0) - jnp.log(1.0 + jnp.exp(-jnp.abs(z)))
        r = lax.broadcasted_iota(jnp.int32, (tb, tb), 0)
        cidx = lax.broadcasted_iota(jnp.int32, (tb, tb), 1)
        tri = (r >= cidx).astype(BF16)
        o_ref[...] = _tri_sum(tri, lf) + carry[...]
        carry[...] += jnp.sum(lf, axis=0, keepdims=True)

    return _pcall(body, name=name, out_shape=jax.ShapeDtypeStruct((t, LANE), F32), grid=(t // tb,),
                  in_specs=[pl.BlockSpec((tb, LANE), lambda i: (i, col)), pl.BlockSpec((1, LANE), lambda i: (0, 0))],
                  out_specs=pl.BlockSpec((tb, LANE), lambda i: (i, 0)), scratch=[pltpu.VMEM((1, LANE), F32)],
                  sem=("arbitrary",))(proj, b_pad)


def _forget_bwd(d_cum, proj, b_pad, name):
    t = proj.shape[0]
    tb = _pick(t, 256, 8)
    nb = t // tb
    col = D_PROJ // LANE

    def body(d_ref, f_ref, b_ref, o_ref, db_ref, carry):
        @pl.when(pl.program_id(0) == 0)
        def _():
            carry[...] = jnp.zeros_like(carry)
            db_ref[...] = jnp.zeros_like(db_ref)

        dc = d_ref[...]
        r = lax.broadcasted_iota(jnp.int32, (tb, tb), 0)
        cidx = lax.broadcasted_iota(jnp.int32, (tb, tb), 1)
        tri = (r <= cidx).astype(BF16)
        dlf = _tri_sum(tri, dc) + carry[...]
        carry[...] += jnp.sum(dc, axis=0, keepdims=True)
        z = f_ref[...] + b_ref[...]
        lane = lax.broadcasted_iota(jnp.int32, (tb, LANE), 1)
        dz = jnp.where(lane < N_HEADS, dlf * jax.nn.sigmoid(-z), 0.0)
        o_ref[...] = dz.astype(BF16)
        db_ref[...] += jnp.sum(dz, axis=0, keepdims=True)

    return _pcall(body, name=name, out_shape=[jax.ShapeDtypeStruct((t, LANE), BF16), jax.ShapeDtypeStruct((1, LANE), F32)],
                  grid=(nb,),
                  in_specs=[pl.BlockSpec((tb, LANE), lambda i: (nb - 1 - i, 0)),
                            pl.BlockSpec((tb, LANE), lambda i: (nb - 1 - i, col)),
                            pl.BlockSpec((1, LANE), lambda i: (0, 0))],
                  out_specs=[pl.BlockSpec((tb, LANE), lambda i: (nb - 1 - i, 0)), pl.BlockSpec((1, LANE), lambda i: (0, 0))],
                  scratch=[pltpu.VMEM((1, LANE), F32)], sem=("arbitrary",))(d_cum, proj, b_pad)


def _head_norm(v, g):
    r = lax.rsqrt(jnp.mean(v * v, axis=-1, keepdims=True) + EPS)
    return v * r, r


def _qkv_prep(proj, qg, kg, name):
    t = proj.shape[0]
    tm = _pick(t, 1024, 8)

    def body(q_ref, k_ref, v_ref, qg_ref, kg_ref, qo_ref, ko_ref, vo_ref):
        qo_ref[...] = (_head_norm(q_ref[...], None)[0] * qg_ref[...]).astype(BF16)
        ko_ref[...] = (_head_norm(k_ref[...], None)[0] * kg_ref[...]).astype(BF16)
        vo_ref[...] = v_ref[...].astype(BF16)

    def blk(off):
        return pl.BlockSpec((tm, HEAD_DIM), lambda i, h: (i, off + h))

    vec = pl.BlockSpec((1, HEAD_DIM), lambda i, h: (0, 0))
    return _pcall(body, name=name, out_shape=[jax.ShapeDtypeStruct((t, D_ATTN), BF16)] * 3, grid=(t // tm, N_HEADS),
                  in_specs=[blk(0), blk(N_HEADS), blk(2 * N_HEADS), vec, vec], out_specs=[blk(0)] * 3,
                  sem=("parallel", "parallel"))(proj, proj, proj, qg, kg)


def _qk_norm_bwd(dqn, dkn, proj, qg, kg, name):
    t = proj.shape[0]
    tm = _pick(t, 1024, 8)

    def one(d_ref, v_ref, g_ref, o_ref, dg_ref):
        xhat, r = _head_norm(v_ref[...], None)
        d = d_ref[...]
        dg_ref[...] += jnp.sum(d * xhat, axis=0, keepdims=True)
        dxh = d * g_ref[...]
        o_ref[...] = (r * (dxh - xhat * jnp.mean(dxh * xhat, axis=-1, keepdims=True))).astype(BF16)

    def body(dq_ref, dk_ref, q_ref, k_ref, qg_ref, kg_ref, qo_ref, ko_ref, dqg_ref, dkg_ref):
        @pl.when((pl.program_id(0) == 0) & (pl.program_id(1) == 0))
        def _():
            dqg_ref[...] = jnp.zeros_like(dqg_ref)
            dkg_ref[...] = jnp.zeros_like(dkg_ref)

        one(dq_ref, q_ref, qg_ref, qo_ref, dqg_ref)
        one(dk_ref, k_ref, kg_ref, ko_ref, dkg_ref)

    def blk(off):
        return pl.BlockSpec((tm, HEAD_DIM), lambda i, h: (i, off + h))

    vec = pl.BlockSpec((1, HEAD_DIM), lambda i, h: (0, 0))
    vshape = jax.ShapeDtypeStruct((1, HEAD_DIM), F32)
    return _pcall(body, name=name, out_shape=[jax.ShapeDtypeStruct((t, D_ATTN), BF16)] * 2 + [vshape, vshape],
                  grid=(t // tm, N_HEADS),
                  in_specs=[blk(0), blk(0), blk(0), blk(N_HEADS), vec, vec], out_specs=[blk(0), blk(0), vec, vec],
                  sem=("arbitrary", "arbitrary"))(dqn, dkn, proj, proj, qg, kg)


ATTN_SCALE = HEAD_DIM ** -0.5


def _logits(q, k, fq, fk, diag, r0, tq, tk):
    s = _dot(q, k, NT) * ATTN_SCALE + (fq - fk)
    if diag:
        r = r0 + lax.broadcasted_iota(jnp.int32, (tq, tk), 0)
        cidx = lax.broadcasted_iota(jnp.int32, (tq, tk), 1)
        s = jnp.where(r >= cidx, s, -jnp.inf)
    return s


def _head_cols(hp):
    return [(hh, slice(hh * HEAD_DIM, (hh + 1) * HEAD_DIM)) for hh in range(hp)]


def _tri(tt, n, by_row):
    if by_row:
        i = sum((tt >= k * (k + 1) // 2).astype(jnp.int32) for k in range(1, n))
        return i, tt - i * (i + 1) // 2
    j = sum((tt >= k * n - k * (k - 1) // 2).astype(jnp.int32) for k in range(1, n))
    return j + tt - (j * n - j * (j - 1) // 2), j


def _attn_fwd(q, k, v, fq, fk, name, sides=()):
    t = q.shape[0]
    tq = tk = _pick(t, ATTN_BLOCK, LANE)
    nk = t // tk

    hp, rc = ATTN_HEADS[0], tq // ATTN_ROW_CHUNKS[0]

    def body(q_ref, k_ref, v_ref, fq_ref, fk_ref, o_ref, lse_ref, m_s, l_s, acc):
        i, j = _tri(pl.program_id(1), nk, True)

        @pl.when(j == 0)
        def _():
            m_s[...] = jnp.full_like(m_s, -jnp.inf)
            l_s[...] = jnp.zeros_like(l_s)
            acc[...] = jnp.zeros_like(acc)

        def step(diag):
            for hh, cols in _head_cols(hp):
                for r0 in range(0, tq, rc):
                    rows = slice(r0, r0 + rc)
                    s = _logits(q_ref[rows, cols], k_ref[:, cols], fq_ref[hh, rows, :], fk_ref[hh], diag, r0, rc, tk)
                    m_new = jnp.maximum(m_s[hh, rows, :], jnp.max(s, axis=-1, keepdims=True))
                    alpha = jnp.exp(m_s[hh, rows, :] - m_new)
                    p = jnp.exp(s - m_new)
                    l_s[hh, rows, :] = alpha * l_s[hh, rows, :] + jnp.sum(p, axis=-1, keepdims=True)
                    acc[rows, cols] = alpha * acc[rows, cols] + _dot(p.astype(BF16), v_ref[:, cols])
                    m_s[hh, rows, :] = m_new

        @pl.when(j < i)
        def _():
            step(False)

        @pl.when(j == i)
        def _():
            step(True)

        @pl.when(j == i)
        def _():
            for hh, cols in _head_cols(hp):
                o_ref[:, cols] = (acc[:, cols] / l_s[hh]).astype(BF16)
                lse_ref[hh] = m_s[hh] + jnp.log(l_s[hh])

    qb = pl.BlockSpec((tq, hp * HEAD_DIM), lambda h, tt: (_tri(tt, nk, True)[0], h))
    kb = pl.BlockSpec((tk, hp * HEAD_DIM), lambda h, tt: (_tri(tt, nk, True)[1], h))
    col = pl.BlockSpec((hp, tq, 1), lambda h, tt: (h, _tri(tt, nk, True)[0], 0))
    return _pcall(body, name=name,
                  out_shape=[jax.ShapeDtypeStruct((t, D_ATTN), BF16), jax.ShapeDtypeStruct((N_HEADS, t, 1), F32)],
                  grid=(N_HEADS // hp, nk * (nk + 1) // 2),
                  in_specs=[qb, kb, kb, col, pl.BlockSpec((hp, 1, tk), lambda h, tt: (h, 0, _tri(tt, nk, True)[1]))],
                  out_specs=[qb, col],
                  scratch=[pltpu.VMEM((hp, tq, 1), F32), pltpu.VMEM((hp, tq, 1), F32), pltpu.VMEM((tq, hp * HEAD_DIM), F32)],
                  sem=("parallel", "arbitrary"), sides=sides)(q, k, v, fq, fk)


def _attn_bwd_q(q, k, v, o, do, lse, fq, fk, name, sides=()):
    t = q.shape[0]
    tq = tk = _pick(t, ATTN_BLOCK, LANE)
    nk = t // tk

    hp, rc = ATTN_HEADS[1], tq // ATTN_ROW_CHUNKS[1]

    def body(q_ref, k_ref, v_ref, o_ref, do_ref, lse_ref, fq_ref, fk_ref, dq_ref, dl_ref, dfq_ref, acc, dl_s, df_s):
        i, j = _tri(pl.program_id(1), nk, True)

        @pl.when(j == 0)
        def _():
            acc[...] = jnp.zeros_like(acc)
            df_s[...] = jnp.zeros_like(df_s)
            for hh, cols in _head_cols(hp):
                dl_s[hh] = jnp.sum(do_ref[:, cols].astype(F32) * o_ref[:, cols].astype(F32), axis=-1, keepdims=True)

        def step(diag):
            for hh, cols in _head_cols(hp):
                for r0 in range(0, tq, rc):
                    rows = slice(r0, r0 + rc)
                    s = _logits(q_ref[rows, cols], k_ref[:, cols], fq_ref[hh, rows, :], fk_ref[hh], diag, r0, rc, tk)
                    p = jnp.exp(s - lse_ref[hh, rows, :])
                    dp = _dot(do_ref[rows, cols], v_ref[:, cols], NT)
                    ds = p * (dp - dl_s[hh, rows, :])
                    df_s[hh, rows, :] += jnp.sum(ds, axis=-1, keepdims=True)
                    acc[rows, cols] += _dot(ds.astype(BF16), k_ref[:, cols])

        @pl.when(j < i)
        def _():
            step(False)

        @pl.when(j == i)
        def _():
            step(True)

        @pl.when(j == i)
        def _():
            dq_ref[...] = acc[...] * ATTN_SCALE
            dl_ref[...] = dl_s[...]
            dfq_ref[...] = df_s[...]

    qb = pl.BlockSpec((tq, hp * HEAD_DIM), lambda h, tt: (_tri(tt, nk, True)[0], h))
    kb = pl.BlockSpec((tk, hp * HEAD_DIM), lambda h, tt: (_tri(tt, nk, True)[1], h))
    col = pl.BlockSpec((hp, tq, 1), lambda h, tt: (h, _tri(tt, nk, True)[0], 0))
    cshape = jax.ShapeDtypeStruct((N_HEADS, t, 1), F32)
    return _pcall(body, name=name, out_shape=[jax.ShapeDtypeStruct((t, D_ATTN), F32), cshape, cshape],
                  grid=(N_HEADS // hp, nk * (nk + 1) // 2),
                  in_specs=[qb, kb, kb, qb, qb, col, col, pl.BlockSpec((hp, 1, tk), lambda h, tt: (h, 0, _tri(tt, nk, True)[1]))],
                  out_specs=[qb, col, col],
                  scratch=[pltpu.VMEM((tq, hp * HEAD_DIM), F32), pltpu.VMEM((hp, tq, 1), F32), pltpu.VMEM((hp, tq, 1), F32)],
                  sem=("parallel", "arbitrary"), sides=sides)(q, k, v, o, do, lse, fq, fk)


def _attn_bwd_kv(q, k, v, do, lse, delta, fq, fk, name, sides=()):
    t = q.shape[0]
    tq = tk = _pick(t, ATTN_BLOCK, LANE)
    nq = t // tq

    hp, rc = ATTN_HEADS[2], tq // ATTN_ROW_CHUNKS[2]

    def body(q_ref, k_ref, v_ref, do_ref, lse_ref, dl_ref, fq_ref, fk_ref, dk_ref, dv_ref, dfk_ref, dk_s, dv_s, df_s):
        i, j = _tri(pl.program_id(1), nq, False)

        @pl.when(i == j)
        def _():
            dk_s[...] = jnp.zeros_like(dk_s)
            dv_s[...] = jnp.zeros_like(dv_s)
            df_s[...] = jnp.zeros_like(df_s)

        def step(diag):
            for hh, cols in _head_cols(hp):
                for r0 in range(0, tq, rc):
                    rows = slice(r0, r0 + rc)
                    s = _logits(q_ref[rows, cols], k_ref[:, cols], fq_ref[hh, rows, :], fk_ref[hh], diag, r0, rc, tk)
                    p = jnp.exp(s - lse_ref[hh, rows, :])
                    dv_s[:, cols] += _dot(p.astype(BF16), do_ref[rows, cols], TN)
                    dp = _dot(do_ref[rows, cols], v_ref[:, cols], NT)
                    ds = p * (dp - dl_ref[hh, rows, :])
                    df_s[hh] -= jnp.sum(ds, axis=0, keepdims=True)
                    dk_s[:, cols] += _dot(ds.astype(BF16), q_ref[rows, cols], TN)

        @pl.when(i > j)
        def _():
            step(False)

        @pl.when(i == j)
        def _():
            step(True)

        @pl.when(i == nq - 1)
        def _():
            dk_ref[...] = dk_s[...] * ATTN_SCALE
            dv_ref[...] = dv_s[...].astype(BF16)
            dfk_ref[...] = df_s[...]

    qb = pl.BlockSpec((tq, hp * HEAD_DIM), lambda h, tt: (_tri(tt, nq, False)[0], h))
    kb = pl.BlockSpec((tk, hp * HEAD_DIM), lambda h, tt: (_tri(tt, nq, False)[1], h))
    col = pl.BlockSpec((hp, tq, 1), lambda h, tt: (h, _tri(tt, nq, False)[0], 0))
    row = pl.BlockSpec((hp, 1, tk), lambda h, tt: (h, 0, _tri(tt, nq, False)[1]))
    return _pcall(body, name=name,
                  out_shape=[jax.ShapeDtypeStruct((t, D_ATTN), F32), jax.ShapeDtypeStruct((t, D_ATTN), BF16),
                             jax.ShapeDtypeStruct((N_HEADS, 1, t), F32)],
                  grid=(N_HEADS // hp, nq * (nq + 1) // 2),
                  in_specs=[qb, kb, kb, qb, col, col, col, row], out_specs=[kb, kb, row],
                  scratch=[pltpu.VMEM((tk, hp * HEAD_DIM), F32), pltpu.VMEM((tk, hp * HEAD_DIM), F32), pltpu.VMEM((hp, 1, tk), F32)],
                  sem=("parallel", "arbitrary"), sides=sides)(q, k, v, do, lse, delta, fq, fk)


def _window_sum(v, w, back):
    n = v.shape[0]
    k = 1
    while k < w:
        v = v + pltpu.roll(v, k if back else n - k, axis=0)
        k *= 2
    return v


def _pool_fwd(proj, pw, ps, name):
    t = proj.shape[0]
    tm = _pick(t, POOL_BLOCK, HALO)
    col = 3 * D_ATTN // D_POOL

    def body(u_ref, prev_ref, pw_ref, ps_ref, pooled_ref, out_ref):
        i = pl.program_id(0)
        prev = jnp.where(i > 0, prev_ref[...], 0.0)
        ext = jnp.concatenate([prev, u_ref[...]], axis=0)
        pos = i * tm + lax.broadcasted_iota(jnp.int32, (tm, 1), 0)
        for g, w in enumerate(POOL_WINDOWS):
            cols = slice(g * GROUP_DIM, (g + 1) * GROUP_DIM)
            xg = ext[:, cols]
            sw = _window_sum(xg, w, True)[HALO:, :]
            cnt = jnp.minimum(pos + 1, w).astype(F32)
            pooled = (sw / cnt - xg[HALO:, :]).astype(BF16)
            pooled_ref[:, cols] = pooled
            out_ref[:, cols] = (_dot(pooled, pw_ref[g]) * ps_ref[:, cols]).astype(BF16)

    row = pl.BlockSpec((tm, D_POOL), lambda i: (i, 0))
    return _pcall(body, name=name, out_shape=[jax.ShapeDtypeStruct((t, D_POOL), BF16)] * 2, grid=(t // tm,),
                  in_specs=[pl.BlockSpec((tm, D_POOL), lambda i: (i, col)),
                            pl.BlockSpec((HALO, D_POOL), lambda i: (jnp.maximum(i * (tm // HALO) - 1, 0), col)),
                            pl.BlockSpec((len(POOL_WINDOWS), GROUP_DIM, GROUP_DIM), lambda i: (0, 0, 0)),
                            pl.BlockSpec((1, D_POOL), lambda i: (0, 0))],
                  out_specs=[row, row], sem=("parallel",))(proj, proj, pw, ps)


def _pool_bwd(dout, pooled, pw, ps, name):
    t = pooled.shape[0]
    tm = _pick(t, POOL_BLOCK, HALO)
    nb = t // tm
    ng = len(POOL_WINDOWS)

    def body(d_ref, nxt_ref, p_ref, pw_ref, ps_ref, du_ref, dpw_ref, dps_ref):
        i = pl.program_id(0)

        @pl.when(i == 0)
        def _():
            dpw_ref[...] = jnp.zeros_like(dpw_ref)
            dps_ref[...] = jnp.zeros_like(dps_ref)

        nxt = jnp.where(i < nb - 1, nxt_ref[...].astype(F32), 0.0)
        ext = jnp.concatenate([d_ref[...].astype(F32), nxt], axis=0)
        pos = i * tm + lax.broadcasted_iota(jnp.int32, (tm + HALO, 1), 0)
        for g, w in enumerate(POOL_WINDOWS):
            cols = slice(g * GROUP_DIM, (g + 1) * GROUP_DIM)
            pooled_g = p_ref[:, cols]
            dg = ext[:, cols]
            pm = _dot(pooled_g, pw_ref[g])
            dps_ref[:, cols] += jnp.sum(dg[:tm, :] * pm, axis=0, keepdims=True)
            dpm = (dg * ps_ref[:, cols]).astype(BF16)
            dpw_ref[g] += _dot(pooled_g, dpm[:tm, :], TN)
            dpooled = _dot(dpm, pw_ref[g], NT)
            cnt = jnp.minimum(pos + 1, w).astype(F32)
            fwd = _window_sum(dpooled / cnt, w, False)
            du_ref[:, cols] = (fwd[:tm, :] - dpooled[:tm, :]).astype(BF16)

    row = pl.BlockSpec((tm, D_POOL), lambda i: (i, 0))
    return _pcall(body, name=name,
                  out_shape=[jax.ShapeDtypeStruct((t, D_POOL), BF16), jax.ShapeDtypeStruct((ng, GROUP_DIM, GROUP_DIM), F32),
                             jax.ShapeDtypeStruct((1, D_POOL), F32)],
                  grid=(nb,),
                  in_specs=[pl.BlockSpec((tm, D_POOL), lambda i: (i, 1)),
                            pl.BlockSpec((HALO, D_POOL), lambda i: (jnp.minimum((i + 1) * (tm // HALO), t // HALO - 1), 1)),
                            row, pl.BlockSpec((ng, GROUP_DIM, GROUP_DIM), lambda i: (0, 0, 0)),
                            pl.BlockSpec((1, D_POOL), lambda i: (0, 0))],
                  out_specs=[row, pl.BlockSpec((ng, GROUP_DIM, GROUP_DIM), lambda i: (0, 0, 0)),
                             pl.BlockSpec((1, D_POOL), lambda i: (0, 0))],
                  sem=("arbitrary",))(dout, dout, pooled, pw, ps)


def _pad_cols(w, n):
    return jnp.pad(w, ((0, 0), (0, n - w.shape[1])))


def kernel(x, c, w_ada, b_ada, ffn1_norm_g, ffn1_w_in, ffn1_w_out, mix_norm_g, w_in, b_forget, q_norm_g, k_norm_g, pool_w, pool_scale, w_out, ffn2_norm_g, ffn2_w_in, ffn2_w_out, final_norm_g, loss_target, m_w_ada, m_b_ada, m_ffn1_norm_g, m_ffn1_w_in, m_ffn1_w_out, m_mix_norm_g, m_w_in, m_b_forget, m_q_norm_g, m_k_norm_g, m_pool_w, m_pool_scale, m_w_out, m_ffn2_norm_g, m_ffn2_w_in, m_ffn2_w_out, m_final_norm_g, v_w_ada, v_b_ada, v_ffn1_norm_g, v_ffn1_w_in, v_ffn1_w_out, v_mix_norm_g, v_w_in, v_b_forget, v_q_norm_g, v_k_norm_g, v_pool_w, v_pool_scale, v_w_out, v_ffn2_norm_g, v_ffn2_w_in, v_ffn2_w_out, v_final_norm_g):
    ax, ay, ac = _coords()
    chip = 2 * ax + ay
    me = 2 * chip + ac
    chip_arr = jnp.reshape(chip, (1,)).astype(jnp.int32)
    core_arr = jnp.reshape(ac, (1,)).astype(jnp.int32)

    t = x.shape[1]
    xs = x.reshape(t, D)
    tgt = loss_target.reshape(t, D)
    hu = ffn1_w_out.shape[1]
    hup = -(-hu // LANE) * LANE
    ws_in = w_in.shape[2]
    ws_in_pad = -(-ws_in // LANE) * LANE
    n_ada = w_ada.shape[2]

    def ffn_in_shard(w):
        w = w[0].astype(BF16)
        return jnp.concatenate([_pad_cols(w[:, :hu], hup), _pad_cols(w[:, hu:], hup)], axis=1)

    def ffn_out_shard(w):
        return jnp.pad(w[0].astype(BF16), ((0, hup - hu), (0, 0)))

    shards1 = [ffn_in_shard(ffn1_w_in), ffn_out_shard(ffn1_w_out)]
    shards_mix = [_pad_cols(w_in[0].astype(BF16), ws_in_pad), pool_w[0].astype(BF16).reshape(GROUP_DIM, GROUP_DIM),
                  w_out[0].astype(BF16)]
    shards2 = [ffn_in_shard(ffn2_w_in), ffn_out_shard(ffn2_w_out)]

    def own_slot(gathered, shards):
        return [lax.dynamic_update_slice(g, w[None], (chip, 0, 0)) for g, w in zip(gathered, shards)]

    def add_siblings(parts, recv, tag):
        return [_add_sibling(p, r, core_arr, f"{tag}_add_sibling_{k}") for k, (p, r) in enumerate(zip(parts, recv))]

    def add_chips(halves, recv, tag):
        return [_add_chips(hh, [(r, 0), (r, 1), (r, 2)], chip_arr, f"{tag}_add_chips_{k}") for k, (hh, r) in enumerate(zip(halves, recv))]

    def both_halves(mine, theirs):
        return [jnp.where(ac == 0, jnp.concatenate([g, r], axis=0), jnp.concatenate([r, g], axis=0)) for g, r in zip(mine, theirs)]

    g1 = _exchange(_gather_d2d(_exchange(_gather_ici(shards1[:1]), "gather_ffn1_ici")), "gather_ffn1_d2d")
    g_in1, = own_slot(g1, shards1[:1])

    c_all = _allgather_small(c.reshape(8, D // 8), True, "gather_c").reshape(8, D)
    c16 = jnp.pad(c_all, ((0, 8), (0, 0)))
    b_ada_mine = lax.dynamic_slice(b_ada, (0, chip * n_ada), (1, n_ada))
    act16, mod16 = _ada_fwd(c16, w_ada[0], b_ada_mine, "ada_fwd")
    mod_all = _allgather_small(mod16[:8], False, "gather_mod")
    mod = lax.dynamic_index_in_dim(mod_all, me, axis=1, keepdims=False).reshape(N_MOD, 1, D)
    sh1, sc1, gt1, sh2, sc2, gt2, sh3, sc3, gt3 = [mod[k] for k in range(N_MOD)]

    gate1, gate3 = 0.5 * gt1, 0.5 * gt3
    h1 = _norm_mod(xs, ffn1_norm_g, sh1, sc1, "ffn1_norm")
    ici = _gather_ici(shards1[1:])
    a1, b1, u1 = _ffn_up(h1, g_in1, "ffn1_up", sides=[ici])
    g_out1, = own_slot(_exchange(_gather_d2d(ici.results), "gather_ffn1_out_d2d"), shards1[1:])
    wg_out1 = g_out1.reshape(4 * hup, D)
    ici = _gather_ici(shards_mix)
    y1, x1 = _mm_resid(u1, wg_out1, xs, gate1, "ffn1_down", sides=[ici])
    d2d = _gather_d2d(ici.results)
    h2 = _norm_mod(x1, mix_norm_g, sh2, sc2, "mix_norm", sides=[d2d])
    g_win, g_pw, g_wout = own_slot(d2d.results, shards_mix)
    w_full = jnp.concatenate([g_win[k, :, :ws_in] for k in range(4)], axis=1)
    nf = 3 * D_ATTN
    w_all = jnp.concatenate([w_full[:, :nf], w_full[:, nf + N_HEADS:], w_full[:, nf:nf + N_HEADS],
                             jnp.zeros((D, LANE - N_HEADS), BF16)], axis=1)
    pw_full = g_pw.reshape(4, 4, GROUP_DIM // 4, GROUP_DIM).transpose(1, 0, 2, 3).reshape(4, GROUP_DIM, GROUP_DIM)
    wo_full = g_wout.reshape(4 * g_wout.shape[1], D)

    proj = _mm(h2, w_all, F32, "mix_proj")
    b_pad = jnp.pad(b_forget, ((0, 0), (0, LANE - N_HEADS)))
    cum = _forget_fwd(proj, b_pad, "forget_fwd")
    cum_t = cum[:, :N_HEADS].T
    fq, fk = cum_t.reshape(N_HEADS, t, 1), cum_t.reshape(N_HEADS, 1, t)
    qn, kn, vb = _qkv_prep(proj, q_norm_g, k_norm_g, "qkv_prep")
    ici = _gather_ici(shards2[:1])
    attn, lse = _attn_fwd(qn, kn, vb, fq, fk, "attn_fwd", sides=[ici])
    pooled, pool_out = _pool_fwd(proj, pw_full, pool_scale, "pool_fwd")
    cat = jnp.concatenate([attn, pool_out], axis=1)
    d2d = _gather_d2d(ici.results)
    y_mix, x2 = _mm_resid(cat, wo_full, x1, gt2, "mix_out", sides=[d2d])
    g_in2, = own_slot(d2d.results, shards2[:1])

    h3 = _norm_mod(x2, ffn2_norm_g, sh3, sc3, "ffn2_norm")
    ici = _gather_ici(shards2[1:])
    a3, b3, u3 = _ffn_up(h3, g_in2, "ffn2_up", sides=[ici])
    g_out2, = own_slot(_exchange(_gather_d2d(ici.results), "gather_ffn2_out_d2d"), shards2[1:])
    wg_out2 = g_out2.reshape(4 * hup, D)
    y3, x3 = _mm_resid(u3, wg_out2, x2, gate3, "ffn2_down")

    loss_part, dx3, d_final_g = _final_loss(x3, final_norm_g.reshape(1, D), tgt, "final_loss")
    loss = lax.psum(loss_part[0, 0], ("x", "y", "c"))

    all_cols = (0, D // 1024)
    dy3, dgt3 = _gate_bwd(dx3, y3, gate3, "ffn2_gate_bwd")
    dab3 = _ffn_dab(dy3, wg_out2, a3, b3, "ffn2_dab")
    parts2 = [_ffn_dwin(h3, dab3, all_cols, "ffn2_dwin"), _mm_tn(u3, dy3, BF16, hup, "ffn2_dwout").reshape(4, hup, D)]
    rs = _reduce_siblings(parts2)
    dh3 = _ffn_dh(dab3, g_in2, "ffn2_dh", sides=[rs])
    half_in2, half_out2 = add_siblings(parts2, rs.results, "ffn2")
    dx2, dsh3, dsc3, d_ng3 = _norm_mod_bwd(dh3, x2, dx3, ffn2_norm_g, sc3, "ffn2_norm_bwd")

    dz, dgt2 = _gate_bwd(dx2, y_mix, gt2, "mix_gate_bwd")
    dcat = _mm_nt(dz, wo_full, BF16, "mix_dcat")
    d_wo = _mm_tn(cat, dz, BF16, 1024, "mix_dwout")
    du_pool, d_pw, d_ps = _pool_bwd(dcat, pooled, pw_full, pool_scale, "pool_bwd")
    rs = _reduce_chips([half_in2])
    dqn, delta, dfq = _attn_bwd_q(qn, kn, vb, attn, dcat, lse, fq, fk, "attn_bwd_q", sides=[rs])
    mine_in2, = add_chips([half_in2], rs.results, "ffn2_in")
    rs, rs3 = _reduce_chips([half_out2]), _share_siblings([mine_in2])
    dkn, dv, dfk = _attn_bwd_kv(qn, kn, vb, dcat, lse, delta, fq, fk, "attn_bwd_kv", sides=[rs, rs3])
    r_in2, = both_halves([mine_in2], rs3.results)
    mine_out2, = add_chips([half_out2], rs.results, "ffn2_out")
    d_cum = jnp.pad((dfq.reshape(N_HEADS, t) + dfk.reshape(N_HEADS, t)).T, ((0, 0), (0, LANE - N_HEADS)))
    dfl, d_bf = _forget_bwd(d_cum, proj, b_pad, "forget_bwd")
    dq, dk, d_qg, d_kg = _qk_norm_bwd(dqn, dkn, proj, q_norm_g, k_norm_g, "qk_norm_bwd")
    dproj = jnp.concatenate([dq, dk, dv, du_pool, dfl], axis=1)
    rs3 = _share_siblings([mine_out2])
    d_wall = _mm_tn(dproj, h2, F32, 1408, "mix_dwin", sides=[rs3])
    r_out2, = both_halves([mine_out2], rs3.results)
    d_wfull = jnp.concatenate([d_wall[:nf], d_wall[D_PROJ:D_PROJ + N_HEADS], d_wall[nf:D_PROJ]], axis=0)
    p_win = jnp.stack([jnp.pad(d_wfull[k * ws_in:(k + 1) * ws_in], ((0, ws_in_pad - ws_in), (0, 0))) for k in range(4)]).astype(BF16)
    p_pw = d_pw.reshape(4, 4, GROUP_DIM // 4, GROUP_DIM).transpose(1, 0, 2, 3).reshape(4, GROUP_DIM, GROUP_DIM).astype(BF16)
    parts_mix = [p_win, p_pw, d_wo.reshape(4, D // 4, D)]
    rs = _reduce_siblings(parts_mix)
    dh2 = _mm_nt(dproj, w_all, F32, "mix_dh", sides=[rs])
    halves_mix = add_siblings(parts_mix, rs.results, "mix")
    dx1, dsh2, dsc2, d_ng2 = _norm_mod_bwd(dh2, x1, dx2, mix_norm_g, sc2, "mix_norm_bwd")

    dy1, dgt1 = _gate_bwd(dx1, y1, gate1, "ffn1_gate_bwd")
    d_out1 = _mm_tn(u1, dy1, BF16, hup, "ffn1_dwout").reshape(4, hup, D)
    rs, rs1 = _reduce_chips(halves_mix), _reduce_siblings([d_out1])
    dab1 = _ffn_dab(dy1, wg_out1, a1, b1, "ffn1_dab", sides=[rs, rs1])
    mine_mix = add_chips(halves_mix, rs.results, "mix")
    half_out1, = add_siblings([d_out1], rs1.results, "ffn1_out")
    rs, rs3 = _reduce_chips([half_out1]), _share_siblings(mine_mix)
    d_in1_lo = _ffn_dwin(h1, dab1, (0, 1), "ffn1_dwin_lo", sides=[rs, rs3])
    r_win, r_pw, r_wo = both_halves(mine_mix, rs3.results)
    mine_out1, = add_chips([half_out1], rs.results, "ffn1_out")
    rs, rs3 = _reduce_siblings([d_in1_lo]), _share_siblings([mine_out1])
    d_in1_hi = _ffn_dwin(h1, dab1, (1, 2), "ffn1_dwin_hi", sides=[rs, rs3])
    r_out1, = both_halves([mine_out1], rs3.results)
    half_lo, = add_siblings([d_in1_lo], rs.results, "ffn1_lo")
    rs, rs1 = _reduce_chips([half_lo]), _reduce_siblings([d_in1_hi])
    dh1 = _ffn_dh(dab1, g_in1, "ffn1_dh", sides=[rs, rs1])
    mine_lo, = add_chips([half_lo], rs.results, "ffn1_lo")
    half_hi, = add_siblings([d_in1_hi], rs1.results, "ffn1_hi")
    dx0, dsh1, dsc1, d_ng1 = _norm_mod_bwd(dh1, xs, dx1, ffn1_norm_g, sc1, "ffn1_norm_bwd")
    grad_x = dx0.reshape(1, t, D)
    dgt1, dgt3 = 0.5 * dgt1, 0.5 * dgt3
    mine_hi, = add_chips([half_hi], _exchange(_reduce_chips([half_hi]), "ffn1_hi_reduce_chips"), "ffn1_hi")
    r_lo, r_hi = both_halves([mine_lo, mine_hi], _exchange(_share_siblings([mine_lo, mine_hi]), "ffn1_share_siblings"))
    r_in1 = jnp.concatenate([r_lo, r_hi], axis=1)
    grads = {
        "ffn1_w_in": jnp.concatenate([r_in1[:hu], r_in1[hup:hup + hu]], axis=0),
        "ffn1_w_out": r_out1[:hu],
        "w_in": r_win[:ws_in],
        "pool_w": r_pw,
        "w_out": r_wo,
        "ffn2_w_in": jnp.concatenate([r_in2[:hu], r_in2[hup:hup + hu]], axis=0),
        "ffn2_w_out": r_out2[:hu],
    }
    hidden_in_rows = ("ffn1_w_in", "ffn2_w_in", "w_in")

    dmod = jnp.concatenate([dsh1, dsc1, dgt1, dsh2, dsc2, dgt2, dsh3, dsc3, dgt3], axis=1)
    small_names = ["b_ada", "ffn1_norm_g", "mix_norm_g", "b_forget", "q_norm_g", "k_norm_g", "pool_scale", "ffn2_norm_g",
                   "final_norm_g"]
    small_grads = [dmod, d_ng1, d_ng2, d_bf[:, :N_HEADS], d_qg, d_kg, d_ps, d_ng3, d_final_g]
    small_w = [b_ada, ffn1_norm_g, mix_norm_g, b_forget, q_norm_g, k_norm_g, pool_scale, ffn2_norm_g, final_norm_g.reshape(1, D)]
    small_m = [m_b_ada, m_ffn1_norm_g, m_mix_norm_g, m_b_forget, m_q_norm_g, m_k_norm_g, m_pool_scale, m_ffn2_norm_g,
               m_final_norm_g.reshape(1, D)]
    small_v = [v_b_ada, v_ffn1_norm_g, v_mix_norm_g, v_b_forget, v_q_norm_g, v_k_norm_g, v_pool_scale, v_ffn2_norm_g,
               v_final_norm_g.reshape(1, D)]
    sizes = [g.shape[1] for g in small_grads]
    n_small = sum(sizes)
    n_pack = -(-n_small // (8 * LANE)) * (8 * LANE)

    def pack(vs, fill):
        flat = jnp.concatenate([v.reshape(1, -1) for v in vs], axis=1)
        return jnp.pad(flat, ((0, 0), (0, n_pack - n_small)), constant_values=fill).reshape(8, n_pack // 8)

    g8 = _allgather_small(pack(small_grads, 0.0), True, "gather_small_grads")
    gs, ds, ms, vs = _adamw_small(pack(small_w, 0.0), g8, pack(small_m, 0.0), pack(small_v, 1.0), "adamw_small")

    def unpack(p):
        flat = p.reshape(1, n_pack)
        out, off = {}, 0
        for nme, sz in zip(small_names, sizes):
            out[nme] = flat[:, off:off + sz]
            off += sz
        return out

    small = [unpack(p) for p in (gs, ds, ms, vs)]
    for dct in small:
        dct["final_norm_g"] = dct["final_norm_g"].reshape(D)

    dmod_all = g8.reshape(8, n_pack)[:, :N_MOD * D]
    dmod_mine = lax.dynamic_slice(dmod_all, (0, chip * n_ada), (8, n_ada))
    grads["w_ada"] = _ada_bwd(act16[:8].reshape(8, D, 1), dmod_mine.reshape(8, 1, n_ada), "ada_bwd")

    big = {"w_ada": (w_ada, m_w_ada, v_w_ada), "ffn1_w_in": (ffn1_w_in, m_ffn1_w_in, v_ffn1_w_in),
           "ffn1_w_out": (ffn1_w_out, m_ffn1_w_out, v_ffn1_w_out), "w_in": (w_in, m_w_in, v_w_in),
           "pool_w": (pool_w, m_pool_w, v_pool_w), "w_out": (w_out, m_w_out, v_w_out),
           "ffn2_w_in": (ffn2_w_in, m_ffn2_w_in, v_ffn2_w_in), "ffn2_w_out": (ffn2_w_out, m_ffn2_w_out, v_ffn2_w_out)}
    res = {}

    for nme, (w, m, v) in big.items():
        shp = w.shape
        g2 = grads[nme]
        if nme in hidden_in_rows:
            flat = (-1, LANE) if nme == "w_in" else g2.shape
            d, mo, vo = _adamw(w[0].T.reshape(flat), g2.reshape(flat), m[0].T.reshape(flat), v[0].T.reshape(flat), f"adamw_{nme}")
            res[nme] = tuple(o.reshape(g2.shape).T.reshape(shp) for o in (g2, d, mo, vo))
            continue
        two = g2.shape
        d, mo, vo = _adamw(w.reshape(two), g2, m.reshape(two), v.reshape(two), f"adamw_{nme}")
        res[nme] = (g2.reshape(shp), d.reshape(shp), mo.reshape(shp), vo.reshape(shp))
    for nme in small_names:
        res[nme] = tuple(dct[nme] for dct in small)

    order = ["w_ada", "b_ada", "ffn1_norm_g", "ffn1_w_in", "ffn1_w_out", "mix_norm_g", "w_in", "b_forget", "q_norm_g",
             "k_norm_g", "pool_w", "pool_scale", "w_out", "ffn2_norm_g", "ffn2_w_in", "ffn2_w_out", "final_norm_g"]
    return (loss, grad_x, *[res[n][0] for n in order], *[res[n][1] for n in order], *[res[n][2] for n in order],
            *[res[n][3] for n in order])
```

```python
import functools

import jax
import jax.numpy as jnp
from jax import lax
from jax.experimental import pallas as pl
from jax.experimental.pallas import tpu as pltpu

F32 = jnp.float32
BF16 = jnp.bfloat16

D = 2048
N_HEADS = 8
HEAD_DIM = 128
D_ATTN = 1024
D_POOL = 1024
POOL_WINDOWS = (2, 4, 8, 16)
GROUP_DIM = 256
HALO = 16
N_MOD = 9
EPS = 1e-6
D_PROJ = 3 * D_ATTN + D_POOL
D_PROJ_PAD = D_PROJ + 128
LANE = 128
ATTN_BLOCK = 512
POOL_BLOCK = 512
FFN_ROW_CHUNKS = 4
ATTN_HEADS = (2, 2, 2)
ATTN_ROW_CHUNKS = (2, 2, 2)

ADAM_LR = 0.001
ADAM_B1 = 0.9
ADAM_B2 = 0.999
ADAM_EPS = 1e-08
ADAM_WD = 0.01
ADAM_STEP = 10

VMEM_LIMIT_V7X = 56 * 1024 * 1024
MESH_ID = pl.DeviceIdType.MESH
ANY = pl.BlockSpec(memory_space=pl.ANY)
VMEM = pl.BlockSpec(memory_space=pltpu.VMEM)

NT = (((1,), (1,)), ((), ()))
TN = (((0,), (0,)), ((), ()))


class _Side:
    def __init__(self, ins, outs, nsem, start, wait, alias=False):
        self.ins, self.outs, self.nsem, self.start, self.wait, self.alias = list(ins), list(outs), nsem, start, wait, alias
        self.results = None


def _pcall(body, *, name, out_shape, grid=None, in_specs=None, out_specs=None, scratch=(), sem=None, prefetch=0, sides=()):
    sides = list(sides)
    single = not isinstance(out_shape, (list, tuple))
    shapes = [out_shape] if single else list(out_shape)
    in_specs = list(in_specs)
    ospecs = [out_specs] if single else list(out_specs)
    scratch = list(scratch)
    n_in, n_out, n_scr = len(in_specs), len(shapes), len(scratch)
    assert not (sides and prefetch)
    aliases = {}
    for sd in sides:
        if sd.alias:
            for k in range(len(sd.outs)):
                aliases[len(in_specs) + k] = len(shapes) + k
        in_specs += [ANY] * len(sd.ins)
        shapes += sd.outs
        ospecs += [ANY] * len(sd.outs)
        scratch += [pltpu.SemaphoreType.DMA((sd.nsem,)), pltpu.SemaphoreType.DMA((sd.nsem,))]

    def wrapped(*refs):
        ins, outs, scr = refs[:len(in_specs)], refs[len(in_specs):len(in_specs) + len(shapes)], refs[len(in_specs) + len(shapes):]
        first = last = None
        for ax, g in enumerate(grid or ()):
            p = pl.program_id(ax)
            first = (p == 0) if first is None else first & (p == 0)
            last = (p == g - 1) if last is None else last & (p == g - 1)

        def each(what):
            i0, o0 = n_in, n_out
            for k, sd in enumerate(sides):
                getattr(sd, what)(ins[i0:i0 + len(sd.ins)], outs[o0:o0 + len(sd.outs)], scr[n_scr + 2 * k], scr[n_scr + 2 * k + 1])
                i0, o0 = i0 + len(sd.ins), o0 + len(sd.outs)

        if first is None:
            each("start")
        else:
            pl.when(first)(lambda: each("start"))
        body(*ins[:n_in], *outs[:n_out], *scr[:n_scr])
        if last is None:
            each("wait")
        else:
            pl.when(last)(lambda: each("wait"))

    params = dict(vmem_limit_bytes=VMEM_LIMIT_V7X)
    if sides and grid:
        params["dimension_semantics"] = ("arbitrary",) * len(grid)
    elif sem is not None:
        params["dimension_semantics"] = sem
    kw = dict(name=name, out_shape=shapes if (sides or not single) else shapes[0], compiler_params=pltpu.CompilerParams(**params))
    if aliases:
        kw["input_output_aliases"] = aliases
    final_ospecs = ospecs if (sides or not single) else ospecs[0]
    if prefetch:
        kw["grid_spec"] = pltpu.PrefetchScalarGridSpec(
            num_scalar_prefetch=prefetch, grid=grid, in_specs=in_specs, out_specs=final_ospecs, scratch_shapes=scratch)
    else:
        if grid is not None:
            kw["grid"] = grid
        kw["in_specs"] = in_specs
        kw["out_specs"] = final_ospecs
        kw["scratch_shapes"] = scratch
    call = pl.pallas_call(wrapped if sides else body, **kw)
    if not sides:
        return call

    def run(*operands):
        res = list(call(*operands, *[a for sd in sides for a in sd.ins]))
        o0 = n_out
        for sd in sides:
            sd.results = res[o0:o0 + len(sd.outs)]
            o0 += len(sd.outs)
        return res[0] if single else res[:n_out]

    return run


def _pick(n, cap, mult):
    best = None
    for d in range(mult, min(n, cap) + 1, mult):
        if n % d == 0:
            best = d
    assert best is not None, (n, cap, mult)
    return best


def _coords():
    return lax.axis_index("x"), lax.axis_index("y"), lax.axis_index("c")


def _dot(a, b, dims=None):
    if dims is None:
        return jnp.dot(a, b, preferred_element_type=F32)
    return lax.dot_general(a, b, dims, preferred_element_type=F32)


def _remote(src, dst, ssem, rsem, dev):
    return pltpu.make_async_remote_copy(src_ref=src, dst_ref=dst, send_sem=ssem, recv_sem=rsem,
                                        device_id=dev, device_id_type=MESH_ID)


def _allgather_small(v, whole_mesh, name):
    masks = list(range(1, 8)) if whole_mesh else [4, 2, 6]
    nslot = 8 if whole_mesh else 4

    def slot(px, py, pc):
        return 4 * px + 2 * py + pc if whole_mesh else 2 * px + py

    def body(v_ref, out_ref, ssem, rsem, lsem):
        x, y, c = _coords()
        mine = slot(x, y, c)
        peers = [(jnp.bitwise_xor(x, (m >> 2) & 1), jnp.bitwise_xor(y, (m >> 1) & 1), jnp.bitwise_xor(c, m & 1))
                 for m in masks]
        loc = pltpu.make_async_copy(v_ref, out_ref.at[mine], lsem)
        loc.start()
        sends = [_remote(v_ref, out_ref.at[mine], ssem.at[k], rsem.at[k], p) for k, p in enumerate(peers)]
        for cp in sends:
            cp.start()
        for k, p in enumerate(peers):
            _remote(v_ref, out_ref.at[slot(*p)], ssem.at[k], rsem.at[k], p).wait_recv()
        for cp in sends:
            cp.wait_send()
        loc.wait()

    return _pcall(body, name=name, out_shape=jax.ShapeDtypeStruct((nslot,) + v.shape, v.dtype),
                  in_specs=[VMEM], out_specs=VMEM,
                  scratch=[pltpu.SemaphoreType.DMA((len(masks),)), pltpu.SemaphoreType.DMA((len(masks),)),
                           pltpu.SemaphoreType.DMA(())])(v)


def _other_chips(x, y):
    return [(1 - x, y), (x, 1 - y), (1 - x, 1 - y)]


def _half_rows(shard_rows, core):
    h = shard_rows // 2
    return pl.ds(pl.multiple_of(core * h, 16), h)


def _later(src, dst, ssem, rsem, dev):
    return functools.partial(_remote, src, dst, ssem, rsem, dev)


def _side_from(ins, outs, nsem, pairs_of, alias=False):
    def start(*refs):
        for send, _ in pairs_of(*refs):
            send().start()

    def wait(*refs):
        pairs = pairs_of(*refs)
        for _, recv in pairs:
            recv().wait_recv()
        for send, _ in pairs:
            send().wait_send()

    return _Side(ins, outs, nsem, start, wait, alias)


def _gather_ici(ws):
    def pairs_of(w_refs, g_refs, ssem, rsem):
        x, y, c = _coords()
        me, out = 2 * x + y, []
        for a, w in enumerate(ws):
            rows = _half_rows(w.shape[0], c)
            for j, chip in enumerate(_other_chips(x, y)):
                sems = (ssem.at[3 * a + j], rsem.at[3 * a + j], (*chip, c))
                got = g_refs[a].at[2 * chip[0] + chip[1], rows]
                out.append((_later(w_refs[a].at[rows], g_refs[a].at[me, rows], *sems), _later(got, got, *sems)))
        return out

    return _side_from(ws, [jax.ShapeDtypeStruct((4,) + w.shape, w.dtype) for w in ws], 3 * len(ws), pairs_of)


def _gather_d2d(gs, ws):
    n = len(gs)

    def pairs_of(in_refs, g_refs, ssem, rsem):
        x, y, c = _coords()
        sib, me, out = (x, y, 1 - c), 2 * x + y, []
        for a, g in enumerate(gs):
            for j, chip in enumerate(_other_chips(x, y)):
                sems = (ssem.at[4 * a + j], rsem.at[4 * a + j], sib)
                mine = g_refs[a].at[2 * chip[0] + chip[1], _half_rows(g.shape[1], c)]
                theirs = g_refs[a].at[2 * chip[0] + chip[1], _half_rows(g.shape[1], 1 - c)]
                out.append((_later(mine, mine, *sems), _later(theirs, theirs, *sems)))
            own = g_refs[a].at[me]
            sems = (ssem.at[4 * a + 3], rsem.at[4 * a + 3], sib)
            out.append((_later(in_refs[n + a], own, *sems), _later(own, own, *sems)))
        return out

    return _side_from(list(gs) + list(ws), [jax.ShapeDtypeStruct(g.shape, g.dtype) for g in gs], 4 * n, pairs_of, alias=True)


def _reduce_siblings(ps):
    def pairs_of(p_refs, r_refs, ssem, rsem):
        x, y, c = _coords()
        out = []
        for a, p in enumerate(ps):
            src = p_refs[a].at[:, _half_rows(p.shape[1], 1 - c), :]
            cp = _later(src, r_refs[a], ssem.at[a], rsem.at[a], (x, y, 1 - c))
            out.append((cp, cp))
        return out

    return _side_from(ps, [jax.ShapeDtypeStruct((4, p.shape[1] // 2, p.shape[2]), p.dtype) for p in ps], len(ps), pairs_of)


def _reduce_chips(hs, dests=(0, 1, 2)):
    nd = len(dests)

    def pairs_of(h_refs, o_refs, ssem, rsem):
        x, y, c = _coords()
        chips = _other_chips(x, y)
        out = []
        for a in range(len(hs)):
            for slot, j in enumerate(dests):
                chip = chips[j]
                cp = _later(h_refs[a].at[2 * chip[0] + chip[1]], o_refs[a].at[slot], ssem.at[nd * a + slot], rsem.at[nd * a + slot],
                            (*chip, c))
                out.append((cp, cp))
        return out

    return _side_from(hs, [jax.ShapeDtypeStruct((nd,) + h.shape[1:], h.dtype) for h in hs], nd * len(hs), pairs_of)


def _share_siblings(gs):
    def pairs_of(_, g_refs, ssem, rsem):
        x, y, c = _coords()
        out = []
        for a, g in enumerate(gs):
            mine, theirs = g_refs[a].at[_half_rows(g.shape[0], c)], g_refs[a].at[_half_rows(g.shape[0], 1 - c)]
            sems = (ssem.at[a], rsem.at[a], (x, y, 1 - c))
            out.append((_later(mine, mine, *sems), _later(theirs, theirs, *sems)))
        return out

    return _side_from(gs, [jax.ShapeDtypeStruct(g.shape, g.dtype) for g in gs], len(gs), pairs_of, alias=True)


def _exchange(side, name):
    def body():
        pass

    _pcall(body, name=name, out_shape=[], in_specs=[], out_specs=[], sides=[side])()
    return side.results


def _add_sibling(p, r, core, name):
    _, rr, cc = p.shape
    h = rr // 2
    th = _pick(h, max(16, (2 << 20) // (2 * cc)), 16)
    nb = h // th

    def body(c_ref, p_ref, r_ref, o_ref):
        o_ref[...] = (p_ref[...].astype(F32) + r_ref[...].astype(F32)).astype(BF16)

    return _pcall(body, name=name, out_shape=jax.ShapeDtypeStruct((4, h, cc), BF16), grid=(4, nb),
                  in_specs=[pl.BlockSpec((None, th, cc), lambda k, i, c_ref: (k, c_ref[0] * nb + i, 0)),
                            pl.BlockSpec((None, th, cc), lambda k, i, c_ref: (k, i, 0))],
                  out_specs=pl.BlockSpec((None, th, cc), lambda k, i, c_ref: (k, i, 0)),
                  sem=("parallel", "parallel"), prefetch=1)(core, p, r)


def _add_chips(hh, pieces, chip_core, name):
    _, h, cc = hh.shape
    th = _pick(h, max(16, (2 << 20) // (2 * cc)), 16)
    nb = h // th

    def body(k_ref, h_ref, r0_ref, r1_ref, r2_ref, o_ref):
        s = h_ref[...].astype(F32) + r0_ref[...].astype(F32)
        s = s + r1_ref[...].astype(F32)
        o_ref[...] = s + r2_ref[...].astype(F32)

    def piece(slot):
        return pl.BlockSpec((None, th, cc), lambda i, k_ref: (slot, i, 0))

    return _pcall(body, name=name, out_shape=jax.ShapeDtypeStruct((2 * h, cc), F32), grid=(nb,),
                  in_specs=[pl.BlockSpec((None, th, cc), lambda i, k_ref: (k_ref[0], i, 0))] + [piece(s) for _, s in pieces],
                  out_specs=pl.BlockSpec((th, cc), lambda i, k_ref: (k_ref[1] * nb + i, 0)),
                  sem=("parallel",), prefetch=1)(chip_core, hh, *[a for a, _ in pieces])


def _adamw_math(w, g, m, v):
    m = ADAM_B1 * m + (1.0 - ADAM_B1) * g
    v = ADAM_B2 * v + (1.0 - ADAM_B2) * (g * g)
    m_hat = m / (1.0 - ADAM_B1 ** ADAM_STEP)
    v_hat = v / (1.0 - ADAM_B2 ** ADAM_STEP)
    delta = -ADAM_LR * (m_hat / (jnp.sqrt(v_hat) + ADAM_EPS) + ADAM_WD * w)
    return delta, m, v


def _adamw(w, g, m, v, name, sides=()):
    rr, cc = w.shape
    tr = _pick(rr, max(8, (3 << 20) // (4 * cc)), 8)

    def body(w_ref, g_ref, m_ref, v_ref, d_ref, mo_ref, vo_ref):
        d, mm, vv = _adamw_math(w_ref[...], g_ref[...], m_ref[...], v_ref[...])
        d_ref[...] = d
        mo_ref[...] = mm
        vo_ref[...] = vv

    spec = pl.BlockSpec((tr, cc), lambda i: (i, 0))
    return _pcall(body, name=name, out_shape=[jax.ShapeDtypeStruct(w.shape, F32)] * 3, grid=(rr // tr,),
                  in_specs=[spec] * 4, out_specs=[spec] * 3, sem=("parallel",), sides=sides)(w, g, m, v)


def _adamw_small(w, g8, m, v, name):
    def body(w_ref, g_ref, m_ref, v_ref, go_ref, d_ref, mo_ref, vo_ref):
        g = g_ref[0]
        for k in range(1, 8):
            g = g + g_ref[k]
        d, mm, vv = _adamw_math(w_ref[...], g, m_ref[...], v_ref[...])
        go_ref[...] = g
        d_ref[...] = d
        mo_ref[...] = mm
        vo_ref[...] = vv

    return _pcall(body, name=name, out_shape=[jax.ShapeDtypeStruct(w.shape, F32)] * 4,
                  in_specs=[VMEM] * 4, out_specs=[VMEM] * 4)(w, g8, m, v)


def _ada_fwd(c16, w_ada, b_ada, name):
    n = w_ada.shape[1]
    tn = _pick(n, 512, LANE)

    def body(c_ref, w_ref, b_ref, act_ref, mod_ref):
        cv = c_ref[...]
        act = cv * jax.nn.sigmoid(cv)
        act_ref[...] = act
        mod_ref[...] = _dot(act.astype(BF16), w_ref[...].astype(BF16)) + b_ref[...]

    return _pcall(body, name=name,
                  out_shape=[jax.ShapeDtypeStruct((16, D), F32), jax.ShapeDtypeStruct((16, n), F32)], grid=(n // tn,),
                  in_specs=[pl.BlockSpec((16, D), lambda j: (0, 0)), pl.BlockSpec((D, tn), lambda j: (0, j)),
                            pl.BlockSpec((1, tn), lambda j: (0, j))],
                  out_specs=[pl.BlockSpec((16, D), lambda j: (0, 0)), pl.BlockSpec((16, tn), lambda j: (0, j))],
                  sem=("arbitrary",))(c16, w_ada, b_ada)


def _ada_bwd(act, dmod, name, sides=()):
    n = dmod.shape[2]
    tm, tn = 256, _pick(n, 512, LANE)

    def body(a_ref, d_ref, o_ref):
        def term(b):
            return a_ref[b].astype(BF16).astype(F32) * d_ref[b].astype(BF16).astype(F32)

        acc = term(0)
        for b in range(1, 8):
            acc = acc + term(b)
        o_ref[...] = acc

    return _pcall(body, name=name, out_shape=jax.ShapeDtypeStruct((D, n), F32), grid=(D // tm, n // tn),
                  in_specs=[pl.BlockSpec((8, tm, 1), lambda i, j: (0, i, 0)), pl.BlockSpec((8, 1, tn), lambda i, j: (0, 0, j))],
                  out_specs=pl.BlockSpec((tm, tn), lambda i, j: (i, j)), sem=("parallel", "parallel"), sides=sides)(act, dmod)


def _norm_mod(x, g, sh, sc, name, sides=()):
    t = x.shape[0]
    tm = _pick(t, 512, 8)

    def body(x_ref, g_ref, sh_ref, sc_ref, h_ref):
        xf = x_ref[...]
        r = lax.rsqrt(jnp.mean(xf * xf, axis=-1, keepdims=True) + EPS)
        h = (xf * r) * g_ref[...]
        h_ref[...] = (h * (1.0 + sc_ref[...]) + sh_ref[...]).astype(BF16)

    vec = pl.BlockSpec((1, D), lambda i: (0, 0))
    row = pl.BlockSpec((tm, D), lambda i: (i, 0))
    return _pcall(body, name=name, out_shape=jax.ShapeDtypeStruct((t, D), BF16), grid=(t // tm,),
                  in_specs=[row, vec, vec, vec], out_specs=row, sem=("parallel",), sides=sides)(x, g, sh, sc)


def _norm_mod_bwd(dh, x, dxo, g, sc, name, sides=()):
    t = x.shape[0]
    tm = _pick(t, 256, 8)

    def body(dh_ref, x_ref, dxo_ref, g_ref, sc_ref, dx_ref, dsh_ref, dsc_ref, dg_ref):
        @pl.when(pl.program_id(0) == 0)
        def _():
            dsh_ref[...] = jnp.zeros_like(dsh_ref)
            dsc_ref[...] = jnp.zeros_like(dsc_ref)
            dg_ref[...] = jnp.zeros_like(dg_ref)

        xf, dh_ = x_ref[...], dh_ref[...]
        r = lax.rsqrt(jnp.mean(xf * xf, axis=-1, keepdims=True) + EPS)
        xhat = xf * r
        dsh_ref[...] += jnp.sum(dh_, axis=0, keepdims=True)
        dsc_ref[...] += jnp.sum(dh_ * (xhat * g_ref[...]), axis=0, keepdims=True)
        tt = dh_ * (1.0 + sc_ref[...])
        dg_ref[...] += jnp.sum(tt * xhat, axis=0, keepdims=True)
        dxh = tt * g_ref[...]
        dx_ref[...] = r * (dxh - xhat * jnp.mean(dxh * xhat, axis=-1, keepdims=True)) + dxo_ref[...]

    vec = pl.BlockSpec((1, D), lambda i: (0, 0))
    row = pl.BlockSpec((tm, D), lambda i: (i, 0))
    vshape = jax.ShapeDtypeStruct((1, D), F32)
    return _pcall(body, name=name, out_shape=[jax.ShapeDtypeStruct((t, D), F32), vshape, vshape, vshape], grid=(t // tm,),
                  in_specs=[row, row, row, vec, vec], out_specs=[row, vec, vec, vec], sem=("arbitrary",), sides=sides)(dh, x, dxo, g, sc)


def _gate_bwd(dxo, yy, gate, name):
    t = dxo.shape[0]
    tm = _pick(t, 512, 8)

    def body(dx_ref, y_ref, g_ref, dy_ref, dg_ref):
        @pl.when(pl.program_id(0) == 0)
        def _():
            dg_ref[...] = jnp.zeros_like(dg_ref)

        dx = dx_ref[...]
        dy_ref[...] = (dx * g_ref[...]).astype(BF16)
        dg_ref[...] += jnp.sum(dx * y_ref[...], axis=0, keepdims=True)

    vec = pl.BlockSpec((1, D), lambda i: (0, 0))
    row = pl.BlockSpec((tm, D), lambda i: (i, 0))
    return _pcall(body, name=name, out_shape=[jax.ShapeDtypeStruct((t, D), BF16), jax.ShapeDtypeStruct((1, D), F32)],
                  grid=(t // tm,), in_specs=[row, row, vec], out_specs=[row, vec], sem=("arbitrary",))(dxo, yy, gate)


def _final_loss(x, g, tgt, name):
    t = x.shape[0]
    tm = _pick(t, 256, 8)

    def body(x_ref, g_ref, t_ref, loss_ref, dx_ref, dg_ref):
        @pl.when(pl.program_id(0) == 0)
        def _():
            loss_ref[...] = jnp.zeros_like(loss_ref)
            dg_ref[...] = jnp.zeros_like(dg_ref)

        xf = x_ref[...]
        r = lax.rsqrt(jnp.mean(xf * xf, axis=-1, keepdims=True) + EPS)
        xhat = xf * r
        e = xhat * g_ref[...] - t_ref[...]
        per_tok = jnp.mean(e * e, axis=-1, keepdims=True)
        loss_ref[...] += 0.5 * jnp.sum(per_tok, axis=0, keepdims=True)
        dy = e * (1.0 / D)
        dg_ref[...] += jnp.sum(dy * xhat, axis=0, keepdims=True)
        dxh = dy * g_ref[...]
        dx_ref[...] = r * (dxh - xhat * jnp.mean(dxh * xhat, axis=-1, keepdims=True))

    vec = pl.BlockSpec((1, D), lambda i: (0, 0))
    row = pl.BlockSpec((tm, D), lambda i: (i, 0))
    return _pcall(body, name=name,
                  out_shape=[jax.ShapeDtypeStruct((1, LANE), F32), jax.ShapeDtypeStruct((t, D), F32),
                             jax.ShapeDtypeStruct((1, D), F32)],
                  grid=(t // tm,), in_specs=[row, vec, row],
                  out_specs=[pl.BlockSpec((1, LANE), lambda i: (0, 0)), row, vec], sem=("arbitrary",))(x, g, tgt)


def _mm(a, b, out_dtype, name):
    m, k = a.shape
    n = b.shape[1]
    tm = _pick(m, 1024, 8)
    tn = _pick(n, 1408, LANE)

    def body(a_ref, b_ref, o_ref):
        o_ref[...] = _dot(a_ref[...], b_ref[...]).astype(out_dtype)

    return _pcall(body, name=name, out_shape=jax.ShapeDtypeStruct((m, n), out_dtype), grid=(m // tm, n // tn),
                  in_specs=[pl.BlockSpec((tm, k), lambda i, j: (i, 0)), pl.BlockSpec((k, tn), lambda i, j: (0, j))],
                  out_specs=pl.BlockSpec((tm, tn), lambda i, j: (i, j)), sem=("parallel", "parallel"))(a, b)


def _mm_resid(a, b, resid, gate, name, sides=()):
    m, k = a.shape
    n = b.shape[1]
    tm, tn, tk = _pick(m, 1024, 8), _pick(n, 1024, LANE), _pick(k, 1408, LANE)
    nk = k // tk

    def body(a_ref, b_ref, r_ref, g_ref, y_ref, o_ref, acc):
        kk = pl.program_id(2)

        @pl.when(kk == 0)
        def _():
            acc[...] = jnp.zeros_like(acc)

        acc[...] += _dot(a_ref[...], b_ref[...])

        @pl.when(kk == nk - 1)
        def _():
            y_ref[...] = acc[...].astype(BF16)
            o_ref[...] = r_ref[...] + g_ref[...] * acc[...]

    blk = pl.BlockSpec((tm, tn), lambda i, j, kk: (i, j))
    return _pcall(body, name=name, out_shape=[jax.ShapeDtypeStruct((m, n), BF16), jax.ShapeDtypeStruct((m, n), F32)],
                  grid=(m // tm, n // tn, nk),
                  in_specs=[pl.BlockSpec((tm, tk), lambda i, j, kk: (i, kk)), pl.BlockSpec((tk, tn), lambda i, j, kk: (kk, j)),
                            blk, pl.BlockSpec((1, tn), lambda i, j, kk: (0, j))],
                  out_specs=[blk, blk], scratch=[pltpu.VMEM((tm, tn), F32)],
                  sem=("parallel", "parallel", "arbitrary"), sides=sides)(a, b, resid, gate)


def _mm_nt(a, b, out_dtype, name, sides=()):
    m, k = a.shape
    n = b.shape[0]
    tm, tn, tk = _pick(m, 1024, 8), _pick(n, 1024, LANE), _pick(k, 2816, LANE)
    nk = k // tk

    def body(a_ref, b_ref, o_ref, acc):
        kk = pl.program_id(2)

        @pl.when(kk == 0)
        def _():
            acc[...] = jnp.zeros_like(acc)

        acc[...] += _dot(a_ref[...], b_ref[...], NT)

        @pl.when(kk == nk - 1)
        def _():
            o_ref[...] = acc[...].astype(out_dtype)

    return _pcall(body, name=name, out_shape=jax.ShapeDtypeStruct((m, n), out_dtype), grid=(m // tm, n // tn, nk),
                  in_specs=[pl.BlockSpec((tm, tk), lambda i, j, kk: (i, kk)), pl.BlockSpec((tn, tk), lambda i, j, kk: (j, kk))],
                  out_specs=pl.BlockSpec((tm, tn), lambda i, j, kk: (i, j)), scratch=[pltpu.VMEM((tm, tn), F32)],
                  sem=("parallel", "parallel", "arbitrary"), sides=sides)(a, b)


def _mm_tn(a, b, out_dtype, tm_cap, name, sides=()):
    t, m = a.shape
    n = b.shape[1]
    tm, tn, tk = _pick(m, tm_cap, LANE), _pick(n, 1408, LANE), _pick(t, 2048, 16)
    nk = t // tk

    def body(a_ref, b_ref, o_ref, acc):
        kk = pl.program_id(2)

        @pl.when(kk == 0)
        def _():
            acc[...] = jnp.zeros_like(acc)

        acc[...] += _dot(a_ref[...], b_ref[...], TN)

        @pl.when(kk == nk - 1)
        def _():
            o_ref[...] = acc[...].astype(out_dtype)

    return _pcall(body, name=name, out_shape=jax.ShapeDtypeStruct((m, n), out_dtype), grid=(m // tm, n // tn, nk),
                  in_specs=[pl.BlockSpec((tk, tm), lambda i, j, kk: (kk, i)), pl.BlockSpec((tk, tn), lambda i, j, kk: (kk, j))],
                  out_specs=pl.BlockSpec((tm, tn), lambda i, j, kk: (i, j)), scratch=[pltpu.VMEM((tm, tn), F32)],
                  sem=("parallel", "parallel", "arbitrary"), sides=sides)(a, b)


def _ffn_up(h, wg, name, sides=()):
    t = h.shape[0]
    fp = wg.shape[2]
    tm, tn = _pick(t, 2048, 8), 256
    nn = fp // tn

    def body(h_ref, wa_ref, wb_ref, a_ref, b_ref, u_ref):
        for r0 in range(0, tm, tm // FFN_ROW_CHUNKS):
            rows = slice(r0, r0 + tm // FFN_ROW_CHUNKS)
            hh = h_ref[rows, :]
            a = _dot(hh, wa_ref[...])
            b = _dot(hh, wb_ref[...])
            a_ref[rows, :] = a.astype(BF16)
            b_ref[rows, :] = b.astype(BF16)
            u_ref[rows, :] = (a * jax.nn.sigmoid(a) * b).astype(BF16)

    out = pl.BlockSpec((tm, tn), lambda i, j, n: (i, j * nn + n))
    return _pcall(body, name=name, out_shape=[jax.ShapeDtypeStruct((t, 2 * fp), BF16)] * 3, grid=(t // tm, 2, nn),
                  in_specs=[pl.BlockSpec((tm, D), lambda i, j, n: (i, 0)),
                            pl.BlockSpec((None, D, tn), lambda i, j, n: (j, 0, n)),
                            pl.BlockSpec((None, D, tn), lambda i, j, n: (j + 2, 0, n))],
                  out_specs=[out, out, out], sem=("parallel", "parallel", "parallel"), sides=sides)(h, wg, wg)


def _ffn_dab(dy, wo, a, b, name, sides=()):
    t = dy.shape[0]
    f2 = wo.shape[0]
    tm, tn = _pick(t, 2048, 8), 256

    def body(dy_ref, w_ref, a_ref, b_ref, o_ref):
        for r0 in range(0, tm, tm // FFN_ROW_CHUNKS):
            rows = slice(r0, r0 + tm // FFN_ROW_CHUNKS)
            du = _dot(dy_ref[rows, :], w_ref[...], NT)
            av, bv = a_ref[rows, :].astype(F32), b_ref[rows, :].astype(F32)
            sg = jax.nn.sigmoid(av)
            o_ref[0, rows, :] = (du * bv * (sg * (1.0 + av * (1.0 - sg)))).astype(BF16)
            o_ref[1, rows, :] = (du * (av * sg)).astype(BF16)

    blk = pl.BlockSpec((tm, tn), lambda i, n: (i, n))
    return _pcall(body, name=name, out_shape=jax.ShapeDtypeStruct((2, t, f2), BF16), grid=(t // tm, f2 // tn),
                  in_specs=[pl.BlockSpec((tm, D), lambda i, n: (i, 0)), pl.BlockSpec((tn, D), lambda i, n: (n, 0)), blk, blk],
                  out_specs=pl.BlockSpec((2, tm, tn), lambda i, n: (0, i, n)), sem=("parallel", "parallel"), sides=sides)(dy, wo, a, b)


def _ffn_dwin(h, dab, col_blocks, name, sides=()):
    t = h.shape[0]
    fp = dab.shape[2] // 2
    tm, tn, tk = _pick(fp, 1408, LANE), 1024, _pick(t, 2048, 16)
    nm, nk = fp // tm, t // tk
    n0, n1 = col_blocks

    def body(d_ref, h_ref, o_ref, acc):
        kk = pl.program_id(3)

        @pl.when(kk == 0)
        def _():
            acc[...] = jnp.zeros_like(acc)

        acc[...] += _dot(d_ref[...], h_ref[...], TN)

        @pl.when(kk == nk - 1)
        def _():
            o_ref[...] = acc[...].astype(BF16)

    return _pcall(body, name=name, out_shape=jax.ShapeDtypeStruct((4, fp, (n1 - n0) * tn), BF16), grid=(4, nm, n1 - n0, nk),
                  in_specs=[pl.BlockSpec((None, tk, tm), lambda u, i, n, kk: (u // 2, kk, (u % 2) * nm + i)),
                            pl.BlockSpec((tk, tn), lambda u, i, n, kk: (kk, n0 + n))],
                  out_specs=pl.BlockSpec((None, tm, tn), lambda u, i, n, kk: (u, i, n)),
                  scratch=[pltpu.VMEM((tm, tn), F32)],
                  sem=("parallel", "parallel", "parallel", "arbitrary"), sides=sides)(dab, h)


def _ffn_dh(dab, wg, name, sides=()):
    t = dab.shape[1]
    fp = wg.shape[2]
    tm, tn, tk = _pick(t, 1024, 8), 1024, _pick(fp, 2816, LANE)
    nkk = fp // tk

    def body(d_ref, w_ref, o_ref, acc):
        u, kk = pl.program_id(2), pl.program_id(3)

        @pl.when((u == 0) & (kk == 0))
        def _():
            acc[...] = jnp.zeros_like(acc)

        acc[...] += _dot(d_ref[...], w_ref[...], NT)

        @pl.when((u == 3) & (kk == nkk - 1))
        def _():
            o_ref[...] = acc[...]

    return _pcall(body, name=name, out_shape=jax.ShapeDtypeStruct((t, D), F32), grid=(t // tm, D // tn, 4, nkk),
                  in_specs=[pl.BlockSpec((None, tm, tk), lambda i, j, u, kk: (u // 2, i, (u % 2) * nkk + kk)),
                            pl.BlockSpec((None, tn, tk), lambda i, j, u, kk: (u, j, kk))],
                  out_specs=pl.BlockSpec((tm, tn), lambda i, j, u, kk: (i, j)), scratch=[pltpu.VMEM((tm, tn), F32)],
                  sem=("parallel", "parallel", "arbitrary", "arbitrary"), sides=sides)(dab, wg)


def _split3(v):
    hi = v.astype(BF16)
    r1 = v - hi.astype(F32)
    mid = r1.astype(BF16)
    lo = (r1 - mid.astype(F32)).astype(BF16)
    return hi, mid, lo


def _tri_sum(tri, v):
    hi, mid, lo = _split3(v)
    return (_dot(tri, hi) + _dot(tri, mid)) + _dot(tri, lo)


def _forget_fwd(proj, b_pad, name):
    t = proj.shape[0]
    tb = _pick(t, 256, 8)
    col = D_PROJ // LANE

    def body(f_ref, b_ref, o_ref, carry):
        @pl.when(pl.program_id(0) == 0)
        def _():
            carry[...] = jnp.zeros_like(carry)

        z = f_ref[...] + b_ref[...]
        lf = jnp.minimum(z, 0.0) - jnp.log(1.0 + jnp.exp(-jnp.abs(z)))
        r = lax.broadcasted_iota(jnp.int32, (tb, tb), 0)
        cidx = lax.broadcasted_iota(jnp.int32, (tb, tb), 1)
        tri = (r >= cidx).astype(BF16)
        o_ref[...] = _tri_sum(tri, lf) + carry[...]
        carry[...] += jnp.sum(lf, axis=0, keepdims=True)

    return _pcall(body, name=name, out_shape=jax.ShapeDtypeStruct((t, LANE), F32), grid=(t // tb,),
                  in_specs=[pl.BlockSpec((tb, LANE), lambda i: (i, col)), pl.BlockSpec((1, LANE), lambda i: (0, 0))],
                  out_specs=pl.BlockSpec((tb, LANE), lambda i: (i, 0)), scratch=[pltpu.VMEM((1, LANE), F32)],
                  sem=("arbitrary",))(proj, b_pad)


def _forget_bwd(d_cum, proj, b_pad, name):
    t = proj.shape[0]
    tb = _pick(t, 256, 8)
    nb = t // tb
    col = D_PROJ // LANE

    def body(d_ref, f_ref, b_ref, o_ref, db_ref, carry):
        @pl.when(pl.program_id(0) == 0)
        def _():
            carry[...] = jnp.zeros_like(carry)
            db_ref[...] = jnp.zeros_like(db_ref)

        dc = d_ref[...]
        r = lax.broadcasted_iota(jnp.int32, (tb, tb), 0)
        cidx = lax.broadcasted_iota(jnp.int32, (tb, tb), 1)
        tri = (r <= cidx).astype(BF16)
        dlf = _tri_sum(tri, dc) + carry[...]
        carry[...] += jnp.sum(dc, axis=0, keepdims=True)
        z = f_ref[...] + b_ref[...]
        lane = lax.broadcasted_iota(jnp.int32, (tb, LANE), 1)
        dz = jnp.where(lane < N_HEADS, dlf * jax.nn.sigmoid(-z), 0.0)
        o_ref[...] = dz.astype(BF16)
        db_ref[...] += jnp.sum(dz, axis=0, keepdims=True)

    return _pcall(body, name=name, out_shape=[jax.ShapeDtypeStruct((t, LANE), BF16), jax.ShapeDtypeStruct((1, LANE), F32)],
                  grid=(nb,),
                  in_specs=[pl.BlockSpec((tb, LANE), lambda i: (nb - 1 - i, 0)),
                            pl.BlockSpec((tb, LANE), lambda i: (nb - 1 - i, col)),
                            pl.BlockSpec((1, LANE), lambda i: (0, 0))],
                  out_specs=[pl.BlockSpec((tb, LANE), lambda i: (nb - 1 - i, 0)), pl.BlockSpec((1, LANE), lambda i: (0, 0))],
                  scratch=[pltpu.VMEM((1, LANE), F32)], sem=("arbitrary",))(d_cum, proj, b_pad)


def _head_norm(v, g):
    r = lax.rsqrt(jnp.mean(v * v, axis=-1, keepdims=True) + EPS)
    return v * r, r


def _qkv_prep(proj, qg, kg, name):
    t = proj.shape[0]
    tm = _pick(t, 1024, 8)

    def body(q_ref, k_ref, v_ref, qg_ref, kg_ref, qo_ref, ko_ref, vo_ref):
        qo_ref[...] = (_head_norm(q_ref[...], None)[0] * qg_ref[...]).astype(BF16)
        ko_ref[...] = (_head_norm(k_ref[...], None)[0] * kg_ref[...]).astype(BF16)
        vo_ref[...] = v_ref[...].astype(BF16)

    def blk(off):
        return pl.BlockSpec((tm, HEAD_DIM), lambda i, h: (i, off + h))

    vec = pl.BlockSpec((1, HEAD_DIM), lambda i, h: (0, 0))
    return _pcall(body, name=name, out_shape=[jax.ShapeDtypeStruct((t, D_ATTN), BF16)] * 3, grid=(t // tm, N_HEADS),
                  in_specs=[blk(0), blk(N_HEADS), blk(2 * N_HEADS), vec, vec], out_specs=[blk(0)] * 3,
                  sem=("parallel", "parallel"))(proj, proj, proj, qg, kg)


def _qk_norm_bwd(dqn, dkn, proj, qg, kg, name):
    t = proj.shape[0]
    tm = _pick(t, 1024, 8)

    def one(d_ref, v_ref, g_ref, o_ref, dg_ref):
        xhat, r = _head_norm(v_ref[...], None)
        d = d_ref[...]
        dg_ref[...] += jnp.sum(d * xhat, axis=0, keepdims=True)
        dxh = d * g_ref[...]
        o_ref[...] = (r * (dxh - xhat * jnp.mean(dxh * xhat, axis=-1, keepdims=True))).astype(BF16)

    def body(dq_ref, dk_ref, q_ref, k_ref, qg_ref, kg_ref, qo_ref, ko_ref, dqg_ref, dkg_ref):
        @pl.when((pl.program_id(0) == 0) & (pl.program_id(1) == 0))
        def _():
            dqg_ref[...] = jnp.zeros_like(dqg_ref)
            dkg_ref[...] = jnp.zeros_like(dkg_ref)

        one(dq_ref, q_ref, qg_ref, qo_ref, dqg_ref)
        one(dk_ref, k_ref, kg_ref, ko_ref, dkg_ref)

    def blk(off):
        return pl.BlockSpec((tm, HEAD_DIM), lambda i, h: (i, off + h))

    vec = pl.BlockSpec((1, HEAD_DIM), lambda i, h: (0, 0))
    vshape = jax.ShapeDtypeStruct((1, HEAD_DIM), F32)
    return _pcall(body, name=name, out_shape=[jax.ShapeDtypeStruct((t, D_ATTN), BF16)] * 2 + [vshape, vshape],
                  grid=(t // tm, N_HEADS),
                  in_specs=[blk(0), blk(0), blk(0), blk(N_HEADS), vec, vec], out_specs=[blk(0), blk(0), vec, vec],
                  sem=("arbitrary", "arbitrary"))(dqn, dkn, proj, proj, qg, kg)


ATTN_SCALE = HEAD_DIM ** -0.5


def _logits(q, k, fq, fk, diag, r0, tq, tk):
    s = _dot(q, k, NT) * ATTN_SCALE + (fq - fk)
    if diag:
        r = r0 + lax.broadcasted_iota(jnp.int32, (tq, tk), 0)
        cidx = lax.broadcasted_iota(jnp.int32, (tq, tk), 1)
        s = jnp.where(r >= cidx, s, -jnp.inf)
    return s


def _head_cols(hp):
    return [(hh, slice(hh * HEAD_DIM, (hh + 1) * HEAD_DIM)) for hh in range(hp)]


def _tri(tt, n, by_row):
    if by_row:
        i = sum((tt >= k * (k + 1) // 2).astype(jnp.int32) for k in range(1, n))
        return i, tt - i * (i + 1) // 2
    j = sum((tt >= k * n - k * (k - 1) // 2).astype(jnp.int32) for k in range(1, n))
    return j + tt - (j * n - j * (j - 1) // 2), j


def _attn_fwd(q, k, v, fq, fk, name, sides=()):
    t = q.shape[0]
    tq = tk = _pick(t, ATTN_BLOCK, LANE)
    nk = t // tk

    hp, rc = ATTN_HEADS[0], tq // ATTN_ROW_CHUNKS[0]

    def body(q_ref, k_ref, v_ref, fq_ref, fk_ref, o_ref, lse_ref, m_s, l_s, acc):
        i, j = _tri(pl.program_id(1), nk, True)

        @pl.when(j == 0)
        def _():
            m_s[...] = jnp.full_like(m_s, -jnp.inf)
            l_s[...] = jnp.zeros_like(l_s)
            acc[...] = jnp.zeros_like(acc)

        def step(diag):
            for hh, cols in _head_cols(hp):
                for r0 in range(0, tq, rc):
                    rows = slice(r0, r0 + rc)
                    s = _logits(q_ref[rows, cols], k_ref[:, cols], fq_ref[hh, rows, :], fk_ref[hh], diag, r0, rc, tk)
                    m_new = jnp.maximum(m_s[hh, rows, :], jnp.max(s, axis=-1, keepdims=True))
                    alpha = jnp.exp(m_s[hh, rows, :] - m_new)
                    p = jnp.exp(s - m_new)
                    l_s[hh, rows, :] = alpha * l_s[hh, rows, :] + jnp.sum(p, axis=-1, keepdims=True)
                    acc[rows, cols] = alpha * acc[rows, cols] + _dot(p.astype(BF16), v_ref[:, cols])
                    m_s[hh, rows, :] = m_new

        @pl.when(j < i)
        def _():
            step(False)

        @pl.when(j == i)
        def _():
            step(True)

        @pl.when(j == i)
        def _():
            for hh, cols in _head_cols(hp):
                o_ref[:, cols] = (acc[:, cols] / l_s[hh]).astype(BF16)
                lse_ref[hh] = m_s[hh] + jnp.log(l_s[hh])

    qb = pl.BlockSpec((tq, hp * HEAD_DIM), lambda h, tt: (_tri(tt, nk, True)[0], h))
    kb = pl.BlockSpec((tk, hp * HEAD_DIM), lambda h, tt: (_tri(tt, nk, True)[1], h))
    col = pl.BlockSpec((hp, tq, 1), lambda h, tt: (h, _tri(tt, nk, True)[0], 0))
    return _pcall(body, name=name,
                  out_shape=[jax.ShapeDtypeStruct((t, D_ATTN), BF16), jax.ShapeDtypeStruct((N_HEADS, t, 1), F32)],
                  grid=(N_HEADS // hp, nk * (nk + 1) // 2),
                  in_specs=[qb, kb, kb, col, pl.BlockSpec((hp, 1, tk), lambda h, tt: (h, 0, _tri(tt, nk, True)[1]))],
                  out_specs=[qb, col],
                  scratch=[pltpu.VMEM((hp, tq, 1), F32), pltpu.VMEM((hp, tq, 1), F32), pltpu.VMEM((tq, hp * HEAD_DIM), F32)],
                  sem=("parallel", "arbitrary"), sides=sides)(q, k, v, fq, fk)


def _attn_bwd_q(q, k, v, o, do, lse, fq, fk, name, sides=()):
    t = q.shape[0]
    tq = tk = _pick(t, ATTN_BLOCK, LANE)
    nk = t // tk

    hp, rc = ATTN_HEADS[1], tq // ATTN_ROW_CHUNKS[1]

    def body(q_ref, k_ref, v_ref, o_ref, do_ref, lse_ref, fq_ref, fk_ref, dq_ref, dl_ref, dfq_ref, acc, dl_s, df_s):
        i, j = _tri(pl.program_id(1), nk, True)

        @pl.when(j == 0)
        def _():
            acc[...] = jnp.zeros_like(acc)
            df_s[...] = jnp.zeros_like(df_s)
            for hh, cols in _head_cols(hp):
                dl_s[hh] = jnp.sum(do_ref[:, cols].astype(F32) * o_ref[:, cols].astype(F32), axis=-1, keepdims=True)

        def step(diag):
            for hh, cols in _head_cols(hp):
                for r0 in range(0, tq, rc):
                    rows = slice(r0, r0 + rc)
                    s = _logits(q_ref[rows, cols], k_ref[:, cols], fq_ref[hh, rows, :], fk_ref[hh], diag, r0, rc, tk)
                    p = jnp.exp(s - lse_ref[hh, rows, :])
                    dp = _dot(do_ref[rows, cols], v_ref[:, cols], NT)
                    ds = p * (dp - dl_s[hh, rows, :])
                    df_s[hh, rows, :] += jnp.sum(ds, axis=-1, keepdims=True)
                    acc[rows, cols] += _dot(ds.astype(BF16), k_ref[:, cols])

        @pl.when(j < i)
        def _():
            step(False)

        @pl.when(j == i)
        def _():
            step(True)

        @pl.when(j == i)
        def _():
            dq_ref[...] = acc[...] * ATTN_SCALE
            dl_ref[...] = dl_s[...]
            dfq_ref[...] = df_s[...]

    qb = pl.BlockSpec((tq, hp * HEAD_DIM), lambda h, tt: (_tri(tt, nk, True)[0], h))
    kb = pl.BlockSpec((tk, hp * HEAD_DIM), lambda h, tt: (_tri(tt, nk, True)[1], h))
    col = pl.BlockSpec((hp, tq, 1), lambda h, tt: (h, _tri(tt, nk, True)[0], 0))
    cshape = jax.ShapeDtypeStruct((N_HEADS, t, 1), F32)
    return _pcall(body, name=name, out_shape=[jax.ShapeDtypeStruct((t, D_ATTN), F32), cshape, cshape],
                  grid=(N_HEADS // hp, nk * (nk + 1) // 2),
                  in_specs=[qb, kb, kb, qb, qb, col, col, pl.BlockSpec((hp, 1, tk), lambda h, tt: (h, 0, _tri(tt, nk, True)[1]))],
                  out_specs=[qb, col, col],
                  scratch=[pltpu.VMEM((tq, hp * HEAD_DIM), F32), pltpu.VMEM((hp, tq, 1), F32), pltpu.VMEM((hp, tq, 1), F32)],
                  sem=("parallel", "arbitrary"), sides=sides)(q, k, v, o, do, lse, fq, fk)


def _attn_bwd_kv(q, k, v, do, lse, delta, fq, fk, name, sides=()):
    t = q.shape[0]
    tq = tk = _pick(t, ATTN_BLOCK, LANE)
    nq = t // tq

    hp, rc = ATTN_HEADS[2], tq // ATTN_ROW_CHUNKS[2]

    def body(q_ref, k_ref, v_ref, do_ref, lse_ref, dl_ref, fq_ref, fk_ref, dk_ref, dv_ref, dfk_ref, dk_s, dv_s, df_s):
        i, j = _tri(pl.program_id(1), nq, False)

        @pl.when(i == j)
        def _():
            dk_s[...] = jnp.zeros_like(dk_s)
            dv_s[...] = jnp.zeros_like(dv_s)
            df_s[...] = jnp.zeros_like(df_s)

        def step(diag):
            for hh, cols in _head_cols(hp):
                for r0 in range(0, tq, rc):
                    rows = slice(r0, r0 + rc)
                    s = _logits(q_ref[rows, cols], k_ref[:, cols], fq_ref[hh, rows, :], fk_ref[hh], diag, r0, rc, tk)
                    p = jnp.exp(s - lse_ref[hh, rows, :])
                    dv_s[:, cols] += _dot(p.astype(BF16), do_ref[rows, cols], TN)
                    dp = _dot(do_ref[rows, cols], v_ref[:, cols], NT)
                    ds = p * (dp - dl_ref[hh, rows, :])
                    df_s[hh] -= jnp.sum(ds, axis=0, keepdims=True)
                    dk_s[:, cols] += _dot(ds.astype(BF16), q_ref[rows, cols], TN)

        @pl.when(i > j)
        def _():
            step(False)

        @pl.when(i == j)
        def _():
            step(True)

        @pl.when(i == nq - 1)
        def _():
            dk_ref[...] = dk_s[...] * ATTN_SCALE
            dv_ref[...] = dv_s[...].astype(BF16)
            dfk_ref[...] = df_s[...]

    qb = pl.BlockSpec((tq, hp * HEAD_DIM), lambda h, tt: (_tri(tt, nq, False)[0], h))
    kb = pl.BlockSpec((tk, hp * HEAD_DIM), lambda h, tt: (_tri(tt, nq, False)[1], h))
    col = pl.BlockSpec((hp, tq, 1), lambda h, tt: (h, _tri(tt, nq, False)[0], 0))
    row = pl.BlockSpec((hp, 1, tk), lambda h, tt: (h, 0, _tri(tt, nq, False)[1]))
    return _pcall(body, name=name,
                  out_shape=[jax.ShapeDtypeStruct((t, D_ATTN), F32), jax.ShapeDtypeStruct((t, D_ATTN), BF16),
                             jax.ShapeDtypeStruct((N_HEADS, 1, t), F32)],
                  grid=(N_HEADS // hp, nq * (nq + 1) // 2),
                  in_specs=[qb, kb, kb, qb, col, col, col, row], out_specs=[kb, kb, row],
                  scratch=[pltpu.VMEM((tk, hp * HEAD_DIM), F32), pltpu.VMEM((tk, hp * HEAD_DIM), F32), pltpu.VMEM((hp, 1, tk), F32)],
                  sem=("parallel", "arbitrary"), sides=sides)(q, k, v, do, lse, delta, fq, fk)


def _window_sum(v, w, back):
    n = v.shape[0]
    k = 1
    while k < w:
        v = v + pltpu.roll(v, k if back else n - k, axis=0)
        k *= 2
    return v


def _pool_fwd(proj, pw, ps, name):
    t = proj.shape[0]
    tm = _pick(t, POOL_BLOCK, HALO)
    col = 3 * D_ATTN // D_POOL

    def body(u_ref, prev_ref, pw_ref, ps_ref, pooled_ref, out_ref):
        i = pl.program_id(0)
        prev = jnp.where(i > 0, prev_ref[...], 0.0)
        ext = jnp.concatenate([prev, u_ref[...]], axis=0)
        pos = i * tm + lax.broadcasted_iota(jnp.int32, (tm, 1), 0)
        for g, w in enumerate(POOL_WINDOWS):
            cols = slice(g * GROUP_DIM, (g + 1) * GROUP_DIM)
            xg = ext[:, cols]
            sw = _window_sum(xg, w, True)[HALO:, :]
            cnt = jnp.minimum(pos + 1, w).astype(F32)
            pooled = (sw / cnt - xg[HALO:, :]).astype(BF16)
            pooled_ref[:, cols] = pooled
            out_ref[:, cols] = (_dot(pooled, pw_ref[g]) * ps_ref[:, cols]).astype(BF16)

    row = pl.BlockSpec((tm, D_POOL), lambda i: (i, 0))
    return _pcall(body, name=name, out_shape=[jax.ShapeDtypeStruct((t, D_POOL), BF16)] * 2, grid=(t // tm,),
                  in_specs=[pl.BlockSpec((tm, D_POOL), lambda i: (i, col)),
                            pl.BlockSpec((HALO, D_POOL), lambda i: (jnp.maximum(i * (tm // HALO) - 1, 0), col)),
                            pl.BlockSpec((len(POOL_WINDOWS), GROUP_DIM, GROUP_DIM), lambda i: (0, 0, 0)),
                            pl.BlockSpec((1, D_POOL), lambda i: (0, 0))],
                  out_specs=[row, row], sem=("parallel",))(proj, proj, pw, ps)


def _pool_bwd(dout, pooled, pw, ps, name):
    t = pooled.shape[0]
    tm = _pick(t, POOL_BLOCK, HALO)
    nb = t // tm
    ng = len(POOL_WINDOWS)

    def body(d_ref, nxt_ref, p_ref, pw_ref, ps_ref, du_ref, dpw_ref, dps_ref):
        i = pl.program_id(0)

        @pl.when(i == 0)
        def _():
            dpw_ref[...] = jnp.zeros_like(dpw_ref)
            dps_ref[...] = jnp.zeros_like(dps_ref)

        nxt = jnp.where(i < nb - 1, nxt_ref[...].astype(F32), 0.0)
        ext = jnp.concatenate([d_ref[...].astype(F32), nxt], axis=0)
        pos = i * tm + lax.broadcasted_iota(jnp.int32, (tm + HALO, 1), 0)
        for g, w in enumerate(POOL_WINDOWS):
            cols = slice(g * GROUP_DIM, (g + 1) * GROUP_DIM)
            pooled_g = p_ref[:, cols]
            dg = ext[:, cols]
            pm = _dot(pooled_g, pw_ref[g])
            dps_ref[:, cols] += jnp.sum(dg[:tm, :] * pm, axis=0, keepdims=True)
            dpm = (dg * ps_ref[:, cols]).astype(BF16)
            dpw_ref[g] += _dot(pooled_g, dpm[:tm, :], TN)
            dpooled = _dot(dpm, pw_ref[g], NT)
            cnt = jnp.minimum(pos + 1, w).astype(F32)
            fwd = _window_sum(dpooled / cnt, w, False)
            du_ref[:, cols] = (fwd[:tm, :] - dpooled[:tm, :]).astype(BF16)

    row = pl.BlockSpec((tm, D_POOL), lambda i: (i, 0))
    return _pcall(body, name=name,
                  out_shape=[jax.ShapeDtypeStruct((t, D_POOL), BF16), jax.ShapeDtypeStruct((ng, GROUP_DIM, GROUP_DIM), F32),
                             jax.ShapeDtypeStruct((1, D_POOL), F32)],
                  grid=(nb,),
                  in_specs=[pl.BlockSpec((tm, D_POOL), lambda i: (i, 1)),
                            pl.BlockSpec((HALO, D_POOL), lambda i: (jnp.minimum((i + 1) * (tm // HALO), t // HALO - 1), 1)),
                            row, pl.BlockSpec((ng, GROUP_DIM, GROUP_DIM), lambda i: (0, 0, 0)),
                            pl.BlockSpec((1, D_POOL), lambda i: (0, 0))],
                  out_specs=[row, pl.BlockSpec((ng, GROUP_DIM, GROUP_DIM), lambda i: (0, 0, 0)),
                             pl.BlockSpec((1, D_POOL), lambda i: (0, 0))],
                  sem=("arbitrary",))(dout, dout, pooled, pw, ps)


def _pad_cols(w, n):
    return jnp.pad(w, ((0, 0), (0, n - w.shape[1])))


def kernel(x, c, w_ada, b_ada, ffn1_norm_g, ffn1_w_in, ffn1_w_out, mix_norm_g, w_in, b_forget, q_norm_g, k_norm_g, pool_w, pool_scale, w_out, ffn2_norm_g, ffn2_w_in, ffn2_w_out, final_norm_g, loss_target, m_w_ada, m_b_ada, m_ffn1_norm_g, m_ffn1_w_in, m_ffn1_w_out, m_mix_norm_g, m_w_in, m_b_forget, m_q_norm_g, m_k_norm_g, m_pool_w, m_pool_scale, m_w_out, m_ffn2_norm_g, m_ffn2_w_in, m_ffn2_w_out, m_final_norm_g, v_w_ada, v_b_ada, v_ffn1_norm_g, v_ffn1_w_in, v_ffn1_w_out, v_mix_norm_g, v_w_in, v_b_forget, v_q_norm_g, v_k_norm_g, v_pool_w, v_pool_scale, v_w_out, v_ffn2_norm_g, v_ffn2_w_in, v_ffn2_w_out, v_final_norm_g):
    ax, ay, ac = _coords()
    chip = 2 * ax + ay
    me = 2 * chip + ac
    chip_core = jnp.stack([chip, ac]).astype(jnp.int32)
    core_arr = jnp.reshape(ac, (1,)).astype(jnp.int32)

    t = x.shape[1]
    xs = x.reshape(t, D)
    tgt = loss_target.reshape(t, D)
    hu = ffn1_w_out.shape[1]
    hup = -(-hu // LANE) * LANE
    ws_in = w_in.shape[2]
    ws_in_pad = -(-ws_in // LANE) * LANE
    n_ada = w_ada.shape[2]

    def ffn_in_shard(w):
        w = w[0].astype(BF16)
        return jnp.concatenate([_pad_cols(w[:, :hu], hup), _pad_cols(w[:, hu:], hup)], axis=1)

    def ffn_out_shard(w):
        return jnp.pad(w[0].astype(BF16), ((0, hup - hu), (0, 0)))

    shards1 = [ffn_in_shard(ffn1_w_in), ffn_out_shard(ffn1_w_out)]
    shards_mix = [_pad_cols(w_in[0].astype(BF16), ws_in_pad), pool_w[0].astype(BF16).reshape(GROUP_DIM, GROUP_DIM),
                  w_out[0].astype(BF16)]
    shards2 = [ffn_in_shard(ffn2_w_in), ffn_out_shard(ffn2_w_out)]

    def add_siblings(parts, recv, tag):
        return [_add_sibling(p, r, core_arr, f"{tag}_add_sibling_{k}") for k, (p, r) in enumerate(zip(parts, recv))]

    def add_chips(halves, recv, tag):
        return [_add_chips(hh, [(r, 0), (r, 1), (r, 2)], chip_core, f"{tag}_add_chips_{k}") for k, (hh, r) in enumerate(zip(halves, recv))]

    g_in1, = _exchange(_gather_d2d(_exchange(_gather_ici(shards1[:1]), "gather_ffn1_ici"), shards1[:1]), "gather_ffn1_d2d")

    c_all = _allgather_small(c.reshape(8, D // 8), True, "gather_c").reshape(8, D)
    c16 = jnp.pad(c_all, ((0, 8), (0, 0)))
    b_ada_mine = lax.dynamic_slice(b_ada, (0, chip * n_ada), (1, n_ada))
    act16, mod16 = _ada_fwd(c16, w_ada[0], b_ada_mine, "ada_fwd")
    mod_all = _allgather_small(mod16[:8], False, "gather_mod")
    mod = lax.dynamic_index_in_dim(mod_all, me, axis=1, keepdims=False).reshape(N_MOD, 1, D)
    sh1, sc1, gt1, sh2, sc2, gt2, sh3, sc3, gt3 = [mod[k] for k in range(N_MOD)]

    gate1, gate3 = 0.5 * gt1, 0.5 * gt3
    h1 = _norm_mod(xs, ffn1_norm_g, sh1, sc1, "ffn1_norm")
    ici = _gather_ici(shards1[1:])
    a1, b1, u1 = _ffn_up(h1, g_in1, "ffn1_up", sides=[ici])
    g_out1, = _exchange(_gather_d2d(ici.results, shards1[1:]), "gather_ffn1_out_d2d")
    wg_out1 = g_out1.reshape(4 * hup, D)
    ici = _gather_ici(shards_mix)
    y1, x1 = _mm_resid(u1, wg_out1, xs, gate1, "ffn1_down", sides=[ici])
    d2d = _gather_d2d(ici.results, shards_mix)
    h2 = _norm_mod(x1, mix_norm_g, sh2, sc2, "mix_norm", sides=[d2d])
    g_win, g_pw, g_wout = d2d.results
    w_full = jnp.concatenate([g_win[k, :, :ws_in] for k in range(4)], axis=1)
    nf = 3 * D_ATTN
    w_all = jnp.concatenate([w_full[:, :nf], w_full[:, nf + N_HEADS:], w_full[:, nf:nf + N_HEADS],
                             jnp.zeros((D, LANE - N_HEADS), BF16)], axis=1)
    pw_full = g_pw.reshape(4, 4, GROUP_DIM // 4, GROUP_DIM).transpose(1, 0, 2, 3).reshape(4, GROUP_DIM, GROUP_DIM)
    wo_full = g_wout.reshape(4 * g_wout.shape[1], D)

    proj = _mm(h2, w_all, F32, "mix_proj")
    b_pad = jnp.pad(b_forget, ((0, 0), (0, LANE - N_HEADS)))
    cum = _forget_fwd(proj, b_pad, "forget_fwd")
    cum_t = cum[:, :N_HEADS].T
    fq, fk = cum_t.reshape(N_HEADS, t, 1), cum_t.reshape(N_HEADS, 1, t)
    qn, kn, vb = _qkv_prep(proj, q_norm_g, k_norm_g, "qkv_prep")
    ici = _gather_ici(shards2[:1])
    attn, lse = _attn_fwd(qn, kn, vb, fq, fk, "attn_fwd", sides=[ici])
    pooled, pool_out = _pool_fwd(proj, pw_full, pool_scale, "pool_fwd")
    cat = jnp.concatenate([attn, pool_out], axis=1)
    d2d = _gather_d2d(ici.results, shards2[:1])
    y_mix, x2 = _mm_resid(cat, wo_full, x1, gt2, "mix_out", sides=[d2d])
    g_in2, = d2d.results

    h3 = _norm_mod(x2, ffn2_norm_g, sh3, sc3, "ffn2_norm")
    ici = _gather_ici(shards2[1:])
    a3, b3, u3 = _ffn_up(h3, g_in2, "ffn2_up", sides=[ici])
    g_out2, = _exchange(_gather_d2d(ici.results, shards2[1:]), "gather_ffn2_out_d2d")
    wg_out2 = g_out2.reshape(4 * hup, D)
    y3, x3 = _mm_resid(u3, wg_out2, x2, gate3, "ffn2_down")

    loss_part, dx3, d_final_g = _final_loss(x3, final_norm_g.reshape(1, D), tgt, "final_loss")
    loss = lax.psum(loss_part[0, 0], ("x", "y", "c"))

    all_cols = (0, D // 1024)
    dy3, dgt3 = _gate_bwd(dx3, y3, gate3, "ffn2_gate_bwd")
    dab3 = _ffn_dab(dy3, wg_out2, a3, b3, "ffn2_dab")
    parts2 = [_ffn_dwin(h3, dab3, all_cols, "ffn2_dwin"), _mm_tn(u3, dy3, BF16, hup, "ffn2_dwout").reshape(4, hup, D)]
    rs = _reduce_siblings(parts2)
    dh3 = _ffn_dh(dab3, g_in2, "ffn2_dh", sides=[rs])
    half_in2, half_out2 = add_siblings(parts2, rs.results, "ffn2")
    dx2, dsh3, dsc3, d_ng3 = _norm_mod_bwd(dh3, x2, dx3, ffn2_norm_g, sc3, "ffn2_norm_bwd")

    dz, dgt2 = _gate_bwd(dx2, y_mix, gt2, "mix_gate_bwd")
    dcat = _mm_nt(dz, wo_full, BF16, "mix_dcat")
    d_wo = _mm_tn(cat, dz, BF16, 1024, "mix_dwout")
    du_pool, d_pw, d_ps = _pool_bwd(dcat, pooled, pw_full, pool_scale, "pool_bwd")
    rs = _reduce_chips([half_in2])
    dqn, delta, dfq = _attn_bwd_q(qn, kn, vb, attn, dcat, lse, fq, fk, "attn_bwd_q", sides=[rs])
    mine_in2, = add_chips([half_in2], rs.results, "ffn2_in")
    rs, rs3 = _reduce_chips([half_out2]), _share_siblings([mine_in2])
    dkn, dv, dfk = _attn_bwd_kv(qn, kn, vb, dcat, lse, delta, fq, fk, "attn_bwd_kv", sides=[rs, rs3])
    r_in2, = rs3.results
    mine_out2, = add_chips([half_out2], rs.results, "ffn2_out")
    d_cum = jnp.pad((dfq.reshape(N_HEADS, t) + dfk.reshape(N_HEADS, t)).T, ((0, 0), (0, LANE - N_HEADS)))
    dfl, d_bf = _forget_bwd(d_cum, proj, b_pad, "forget_bwd")
    dq, dk, d_qg, d_kg = _qk_norm_bwd(dqn, dkn, proj, q_norm_g, k_norm_g, "qk_norm_bwd")
    dproj = jnp.concatenate([dq, dk, dv, du_pool, dfl], axis=1)
    rs3 = _share_siblings([mine_out2])
    d_wall = _mm_tn(dproj, h2, F32, 1408, "mix_dwin", sides=[rs3])
    r_out2, = rs3.results
    d_wfull = jnp.concatenate([d_wall[:nf], d_wall[D_PROJ:D_PROJ + N_HEADS], d_wall[nf:D_PROJ]], axis=0)
    p_win = jnp.stack([jnp.pad(d_wfull[k * ws_in:(k + 1) * ws_in], ((0, ws_in_pad - ws_in), (0, 0))) for k in range(4)]).astype(BF16)
    p_pw = d_pw.reshape(4, 4, GROUP_DIM // 4, GROUP_DIM).transpose(1, 0, 2, 3).reshape(4, GROUP_DIM, GROUP_DIM).astype(BF16)
    parts_mix = [p_win, p_pw, d_wo.reshape(4, D // 4, D)]
    rs = _reduce_siblings(parts_mix)
    dh2 = _mm_nt(dproj, w_all, F32, "mix_dh", sides=[rs])
    halves_mix = add_siblings(parts_mix, rs.results, "mix")
    dx1, dsh2, dsc2, d_ng2 = _norm_mod_bwd(dh2, x1, dx2, mix_norm_g, sc2, "mix_norm_bwd")

    dy1, dgt1 = _gate_bwd(dx1, y1, gate1, "ffn1_gate_bwd")
    d_out1 = _mm_tn(u1, dy1, BF16, hup, "ffn1_dwout").reshape(4, hup, D)
    rs, rs1 = _reduce_chips(halves_mix), _reduce_siblings([d_out1])
    dab1 = _ffn_dab(dy1, wg_out1, a1, b1, "ffn1_dab", sides=[rs, rs1])
    mine_mix = add_chips(halves_mix, rs.results, "mix")
    half_out1, = add_siblings([d_out1], rs1.results, "ffn1_out")
    rs, rs3 = _reduce_chips([half_out1]), _share_siblings(mine_mix)
    d_in1_lo = _ffn_dwin(h1, dab1, (0, 1), "ffn1_dwin_lo", sides=[rs, rs3])
    r_win, r_pw, r_wo = rs3.results
    mine_out1, = add_chips([half_out1], rs.results, "ffn1_out")
    rs, rs3 = _reduce_siblings([d_in1_lo]), _share_siblings([mine_out1])
    d_in1_hi = _ffn_dwin(h1, dab1, (1, 2), "ffn1_dwin_hi", sides=[rs, rs3])
    r_out1, = rs3.results
    half_lo, = add_siblings([d_in1_lo], rs.results, "ffn1_lo")
    rs, rs1 = _reduce_chips([half_lo]), _reduce_siblings([d_in1_hi])
    dh1 = _ffn_dh(dab1, g_in1, "ffn1_dh", sides=[rs, rs1])
    mine_lo, = add_chips([half_lo], rs.results, "ffn1_lo")
    half_hi, = add_siblings([d_in1_hi], rs1.results, "ffn1_hi")
    dx0, dsh1, dsc1, d_ng1 = _norm_mod_bwd(dh1, xs, dx1, ffn1_norm_g, sc1, "ffn1_norm_bwd")
    grad_x = dx0.reshape(1, t, D)
    dgt1, dgt3 = 0.5 * dgt1, 0.5 * dgt3
    mine_hi, = add_chips([half_hi], _exchange(_reduce_chips([half_hi]), "ffn1_hi_reduce_chips"), "ffn1_hi")
    r_lo, r_hi = _exchange(_share_siblings([mine_lo, mine_hi]), "ffn1_share_siblings")
    r_in1 = jnp.concatenate([r_lo, r_hi], axis=1)
    grads = {
        "ffn1_w_in": jnp.concatenate([r_in1[:hu], r_in1[hup:hup + hu]], axis=0),
        "ffn1_w_out": r_out1[:hu],
        "w_in": r_win[:ws_in],
        "pool_w": r_pw,
        "w_out": r_wo,
        "ffn2_w_in": jnp.concatenate([r_in2[:hu], r_in2[hup:hup + hu]], axis=0),
        "ffn2_w_out": r_out2[:hu],
    }
    hidden_in_rows = ("ffn1_w_in", "ffn2_w_in", "w_in")

    dmod = jnp.concatenate([dsh1, dsc1, dgt1, dsh2, dsc2, dgt2, dsh3, dsc3, dgt3], axis=1)
    small_names = ["b_ada", "ffn1_norm_g", "mix_norm_g", "b_forget", "q_norm_g", "k_norm_g", "pool_scale", "ffn2_norm_g",
                   "final_norm_g"]
    small_grads = [dmod, d_ng1, d_ng2, d_bf[:, :N_HEADS], d_qg, d_kg, d_ps, d_ng3, d_final_g]
    small_w = [b_ada, ffn1_norm_g, mix_norm_g, b_forget, q_norm_g, k_norm_g, pool_scale, ffn2_norm_g, final_norm_g.reshape(1, D)]
    small_m = [m_b_ada, m_ffn1_norm_g, m_mix_norm_g, m_b_forget, m_q_norm_g, m_k_norm_g, m_pool_scale, m_ffn2_norm_g,
               m_final_norm_g.reshape(1, D)]
    small_v = [v_b_ada, v_ffn1_norm_g, v_mix_norm_g, v_b_forget, v_q_norm_g, v_k_norm_g, v_pool_scale, v_ffn2_norm_g,
               v_final_norm_g.reshape(1, D)]
    sizes = [g.shape[1] for g in small_grads]
    n_small = sum(sizes)
    n_pack = -(-n_small // (8 * LANE)) * (8 * LANE)

    def pack(vs, fill):
        flat = jnp.concatenate([v.reshape(1, -1) for v in vs], axis=1)
        return jnp.pad(flat, ((0, 0), (0, n_pack - n_small)), constant_values=fill).reshape(8, n_pack // 8)

    g8 = _allgather_small(pack(small_grads, 0.0), True, "gather_small_grads")
    gs, ds, ms, vs = _adamw_small(pack(small_w, 0.0), g8, pack(small_m, 0.0), pack(small_v, 1.0), "adamw_small")

    def unpack(p):
        flat = p.reshape(1, n_pack)
        out, off = {}, 0
        for nme, sz in zip(small_names, sizes):
            out[nme] = flat[:, off:off + sz]
            off += sz
        return out

    small = [unpack(p) for p in (gs, ds, ms, vs)]
    for dct in small:
        dct["final_norm_g"] = dct["final_norm_g"].reshape(D)

    dmod_all = g8.reshape(8, n_pack)[:, :N_MOD * D]
    dmod_mine = lax.dynamic_slice(dmod_all, (0, chip * n_ada), (8, n_ada))
    grads["w_ada"] = _ada_bwd(act16[:8].reshape(8, D, 1), dmod_mine.reshape(8, 1, n_ada), "ada_bwd")

    big = {"w_ada": (w_ada, m_w_ada, v_w_ada), "ffn1_w_in": (ffn1_w_in, m_ffn1_w_in, v_ffn1_w_in),
           "ffn1_w_out": (ffn1_w_out, m_ffn1_w_out, v_ffn1_w_out), "w_in": (w_in, m_w_in, v_w_in),
           "pool_w": (pool_w, m_pool_w, v_pool_w), "w_out": (w_out, m_w_out, v_w_out),
           "ffn2_w_in": (ffn2_w_in, m_ffn2_w_in, v_ffn2_w_in), "ffn2_w_out": (ffn2_w_out, m_ffn2_w_out, v_ffn2_w_out)}
    res = {}

    for nme, (w, m, v) in big.items():
        shp = w.shape
        g2 = grads[nme]
        if nme in hidden_in_rows:
            rows = g2.shape[0]
            more = -rows % 8

            def fit(a):
                return jnp.pad(a, ((0, more), (0, 0))) if more else a

            outs = _adamw(fit(w[0].T), fit(g2), fit(m[0].T), fit(v[0].T), f"adamw_{nme}")
            res[nme] = tuple(o[:rows].T.reshape(shp) for o in (g2, *outs))
            continue
        two = g2.shape
        d, mo, vo = _adamw(w.reshape(two), g2, m.reshape(two), v.reshape(two), f"adamw_{nme}")
        res[nme] = (g2.reshape(shp), d.reshape(shp), mo.reshape(shp), vo.reshape(shp))
    for nme in small_names:
        res[nme] = tuple(dct[nme] for dct in small)

    order = ["w_ada", "b_ada", "ffn1_norm_g", "ffn1_w_in", "ffn1_w_out", "mix_norm_g", "w_in", "b_forget", "q_norm_g",
             "k_norm_g", "pool_w", "pool_scale", "w_out", "ffn2_norm_g", "ffn2_w_in", "ffn2_w_out", "final_norm_g"]
    return (loss, grad_x, *[res[n][0] for n in order], *[res[n][1] for n in order], *[res[n][2] for n in order],
            *[res[n][3] for n in order])
```

```python
import functools

import jax
import jax.numpy as jnp
from jax import lax
from jax.experimental import pallas as pl
from jax.experimental.pallas import tpu as pltpu

F32 = jnp.float32
BF16 = jnp.bfloat16

D = 2048
N_HEADS = 8
HEAD_DIM = 128
D_ATTN = 1024
D_POOL = 1024
POOL_WINDOWS = (2, 4, 8, 16)
GROUP_DIM = 256
HALO = 16
N_MOD = 9
EPS = 1e-6
D_PROJ = 3 * D_ATTN + D_POOL
D_PROJ_PAD = D_PROJ + 128
LANE = 128
ATTN_BLOCK = 512
POOL_BLOCK = 512
FFN_ROW_CHUNKS = 4
ATTN_HEADS = (2, 2, 2)
ATTN_ROW_CHUNKS = (2, 2, 2)

ADAM_LR = 0.001
ADAM_B1 = 0.9
ADAM_B2 = 0.999
ADAM_EPS = 1e-08
ADAM_WD = 0.01
ADAM_STEP = 10

VMEM_LIMIT_V7X = 56 * 1024 * 1024
MESH_ID = pl.DeviceIdType.MESH
ANY = pl.BlockSpec(memory_space=pl.ANY)
VMEM = pl.BlockSpec(memory_space=pltpu.VMEM)

NT = (((1,), (1,)), ((), ()))
TN = (((0,), (0,)), ((), ()))


class _Side:
    def __init__(self, ins, outs, nsem, start, wait, alias=False, mids=()):
        self.ins, self.outs, self.nsem, self.start, self.wait, self.alias = list(ins), list(outs), nsem, start, wait, alias
        self.mids = list(mids)
        self.results = None


def _pcall(body, *, name, out_shape, grid=None, in_specs=None, out_specs=None, scratch=(), sem=None, prefetch=0, sides=()):
    sides = list(sides)
    single = not isinstance(out_shape, (list, tuple))
    shapes = [out_shape] if single else list(out_shape)
    in_specs = list(in_specs)
    ospecs = [out_specs] if single else list(out_specs)
    scratch = list(scratch)
    n_in, n_out, n_scr = len(in_specs), len(shapes), len(scratch)
    assert not (sides and prefetch)
    aliases = {}
    for sd in sides:
        if sd.alias:
            for k in range(len(sd.outs)):
                aliases[len(in_specs) + k] = len(shapes) + k
        in_specs += [ANY] * len(sd.ins)
        shapes += sd.outs
        ospecs += [ANY] * len(sd.outs)
        scratch += [pltpu.SemaphoreType.DMA((sd.nsem,)), pltpu.SemaphoreType.DMA((sd.nsem,))]

    def wrapped(*refs):
        ins, outs, scr = refs[:len(in_specs)], refs[len(in_specs):len(in_specs) + len(shapes)], refs[len(in_specs) + len(shapes):]
        step, steps = 0, 1
        for ax, g in enumerate(grid or ()):
            step, steps = step * g + pl.program_id(ax), steps * g

        def at(when, fn):
            if grid:
                pl.when(step == when)(fn)
            else:
                fn()

        i0, o0 = n_in, n_out
        for k, sd in enumerate(sides):
            refs_k = (ins[i0:i0 + len(sd.ins)], outs[o0:o0 + len(sd.outs)], scr[n_scr + 2 * k], scr[n_scr + 2 * k + 1])
            at(0, functools.partial(sd.start, *refs_k))
            for frac, fn in sd.mids:
                at(min(steps - 1, int(frac * steps)), functools.partial(fn, *refs_k))
            i0, o0 = i0 + len(sd.ins), o0 + len(sd.outs)
        body(*ins[:n_in], *outs[:n_out], *scr[:n_scr])
        i0, o0 = n_in, n_out
        for k, sd in enumerate(sides):
            refs_k = (ins[i0:i0 + len(sd.ins)], outs[o0:o0 + len(sd.outs)], scr[n_scr + 2 * k], scr[n_scr + 2 * k + 1])
            at(steps - 1, functools.partial(sd.wait, *refs_k))
            i0, o0 = i0 + len(sd.ins), o0 + len(sd.outs)

    params = dict(vmem_limit_bytes=VMEM_LIMIT_V7X)
    if sides and grid:
        params["dimension_semantics"] = ("arbitrary",) * len(grid)
    elif sem is not None:
        params["dimension_semantics"] = sem
    kw = dict(name=name, out_shape=shapes if (sides or not single) else shapes[0], compiler_params=pltpu.CompilerParams(**params))
    if aliases:
        kw["input_output_aliases"] = aliases
    final_ospecs = ospecs if (sides or not single) else ospecs[0]
    if prefetch:
        kw["grid_spec"] = pltpu.PrefetchScalarGridSpec(
            num_scalar_prefetch=prefetch, grid=grid, in_specs=in_specs, out_specs=final_ospecs, scratch_shapes=scratch)
    else:
        if grid is not None:
            kw["grid"] = grid
        kw["in_specs"] = in_specs
        kw["out_specs"] = final_ospecs
        kw["scratch_shapes"] = scratch
    call = pl.pallas_call(wrapped if sides else body, **kw)
    if not sides:
        return call

    def run(*operands):
        res = list(call(*operands, *[a for sd in sides for a in sd.ins]))
        o0 = n_out
        for sd in sides:
            sd.results = res[o0:o0 + len(sd.outs)]
            o0 += len(sd.outs)
        return res[0] if single else res[:n_out]

    return run


def _pick(n, cap, mult):
    best = None
    for d in range(mult, min(n, cap) + 1, mult):
        if n % d == 0:
            best = d
    assert best is not None, (n, cap, mult)
    return best


def _coords():
    return lax.axis_index("x"), lax.axis_index("y"), lax.axis_index("c")


def _dot(a, b, dims=None):
    if dims is None:
        return jnp.dot(a, b, preferred_element_type=F32)
    return lax.dot_general(a, b, dims, preferred_element_type=F32)


def _remote(src, dst, ssem, rsem, dev):
    return pltpu.make_async_remote_copy(src_ref=src, dst_ref=dst, send_sem=ssem, recv_sem=rsem,
                                        device_id=dev, device_id_type=MESH_ID)


def _allgather_small(v, whole_mesh, name):
    masks = list(range(1, 8)) if whole_mesh else [4, 2, 6]
    nslot = 8 if whole_mesh else 4

    def slot(px, py, pc):
        return 4 * px + 2 * py + pc if whole_mesh else 2 * px + py

    def body(v_ref, out_ref, ssem, rsem, lsem):
        x, y, c = _coords()
        mine = slot(x, y, c)
        peers = [(jnp.bitwise_xor(x, (m >> 2) & 1), jnp.bitwise_xor(y, (m >> 1) & 1), jnp.bitwise_xor(c, m & 1))
                 for m in masks]
        loc = pltpu.make_async_copy(v_ref, out_ref.at[mine], lsem)
        loc.start()
        sends = [_remote(v_ref, out_ref.at[mine], ssem.at[k], rsem.at[k], p) for k, p in enumerate(peers)]
        for cp in sends:
            cp.start()
        for k, p in enumerate(peers):
            _remote(v_ref, out_ref.at[slot(*p)], ssem.at[k], rsem.at[k], p).wait_recv()
        for cp in sends:
            cp.wait_send()
        loc.wait()

    return _pcall(body, name=name, out_shape=jax.ShapeDtypeStruct((nslot,) + v.shape, v.dtype),
                  in_specs=[VMEM], out_specs=VMEM,
                  scratch=[pltpu.SemaphoreType.DMA((len(masks),)), pltpu.SemaphoreType.DMA((len(masks),)),
                           pltpu.SemaphoreType.DMA(())])(v)


def _other_chips(x, y):
    return [(1 - x, y), (x, 1 - y), (1 - x, 1 - y)]


def _half_rows(shard_rows, core):
    h = shard_rows // 2
    return pl.ds(pl.multiple_of(core * h, 16), h)


def _later(src, dst, ssem, rsem, dev):
    return functools.partial(_remote, src, dst, ssem, rsem, dev)


def _side_from(ins, outs, nsem, pairs_of, alias=False):
    def start(*refs):
        for send, _ in pairs_of(*refs):
            send().start()

    def wait(*refs):
        pairs = pairs_of(*refs)
        for _, recv in pairs:
            recv().wait_recv()
        for send, _ in pairs:
            send().wait_send()

    return _Side(ins, outs, nsem, start, wait, alias)


GATHER_STAGES = (0.45, 0.75)


def _gather(ws):
    def copies(w_refs, g_refs, ssem, rsem):
        x, y, c = _coords()
        me, sib = 2 * x + y, (x, y, 1 - c)
        across_x, across_y, far = 2 * (1 - x) + y, 2 * x + (1 - y), 2 * (1 - x) + (1 - y)
        to_x, to_y = (1 - x, y, c), (x, 1 - y, c)
        out = []
        for a, w in enumerate(ws):
            h = w.shape[0] // 2
            rows, theirs = _half_rows(w.shape[0], c), _half_rows(w.shape[0], 1 - c)
            first = pl.ds(pl.multiple_of(c * h, 16), h // 2)
            second = pl.ds(pl.multiple_of(c * h + h // 2, 16), h // 2)
            g = g_refs[a]

            def pair(k, src, dst, got, dev):
                sems = (ssem.at[8 * a + k], rsem.at[8 * a + k], dev)
                return _later(src, dst, *sems), _later(got, got, *sems)

            out.append(dict(
                x1=pair(0, w_refs[a].at[rows], g.at[me, rows], g.at[across_x, rows], to_x),
                y1=pair(1, w_refs[a].at[rows], g.at[me, rows], g.at[across_y, rows], to_y),
                x2=pair(2, g.at[across_y, first], g.at[across_y, first], g.at[far, first], to_x),
                y2=pair(3, g.at[across_x, second], g.at[across_x, second], g.at[far, second], to_y),
                sx=pair(4, g.at[across_x, rows], g.at[across_x, rows], g.at[across_x, theirs], sib),
                sy=pair(5, g.at[across_y, rows], g.at[across_y, rows], g.at[across_y, theirs], sib),
                sf=pair(6, g.at[far, rows], g.at[far, rows], g.at[far, theirs], sib),
                so=pair(7, w_refs[a], g.at[me], g.at[me], sib)))
        return out

    def start(*refs):
        for cp in copies(*refs):
            cp["x1"][0]().start()
            cp["y1"][0]().start()

    def stage2(*refs):
        for cp in copies(*refs):
            cp["x1"][1]().wait_recv()
            cp["y1"][1]().wait_recv()
            for k in ("x2", "y2", "sx", "sy", "so"):
                cp[k][0]().start()

    def stage3(*refs):
        for cp in copies(*refs):
            cp["x2"][1]().wait_recv()
            cp["y2"][1]().wait_recv()
            cp["sf"][0]().start()

    def wait(*refs):
        cps = copies(*refs)
        for cp in cps:
            for k in ("sx", "sy", "sf", "so"):
                cp[k][1]().wait_recv()
        for cp in cps:
            for send, _ in cp.values():
                send().wait_send()

    return _Side(ws, [jax.ShapeDtypeStruct((4,) + w.shape, w.dtype) for w in ws], 8 * len(ws), start, wait,
                 mids=[(GATHER_STAGES[0], stage2), (GATHER_STAGES[1], stage3)])


def _reduce_siblings(ps):
    def pairs_of(p_refs, r_refs, ssem, rsem):
        x, y, c = _coords()
        out = []
        for a, p in enumerate(ps):
            src = p_refs[a].at[:, _half_rows(p.shape[1], 1 - c), :]
            cp = _later(src, r_refs[a], ssem.at[a], rsem.at[a], (x, y, 1 - c))
            out.append((cp, cp))
        return out

    return _side_from(ps, [jax.ShapeDtypeStruct((4, p.shape[1] // 2, p.shape[2]), p.dtype) for p in ps], len(ps), pairs_of)


def _reduce_chips(hs, dests=(0, 1, 2)):
    nd = len(dests)

    def pairs_of(h_refs, o_refs, ssem, rsem):
        x, y, c = _coords()
        chips = _other_chips(x, y)
        out = []
        for a in range(len(hs)):
            for slot, j in enumerate(dests):
                chip = chips[j]
                cp = _later(h_refs[a].at[2 * chip[0] + chip[1]], o_refs[a].at[slot], ssem.at[nd * a + slot], rsem.at[nd * a + slot],
                            (*chip, c))
                out.append((cp, cp))
        return out

    return _side_from(hs, [jax.ShapeDtypeStruct((nd,) + h.shape[1:], h.dtype) for h in hs], nd * len(hs), pairs_of)


def _share_siblings(gs):
    def pairs_of(_, g_refs, ssem, rsem):
        x, y, c = _coords()
        out = []
        for a, g in enumerate(gs):
            mine, theirs = g_refs[a].at[_half_rows(g.shape[0], c)], g_refs[a].at[_half_rows(g.shape[0], 1 - c)]
            sems = (ssem.at[a], rsem.at[a], (x, y, 1 - c))
            out.append((_later(mine, mine, *sems), _later(theirs, theirs, *sems)))
        return out

    return _side_from(gs, [jax.ShapeDtypeStruct(g.shape, g.dtype) for g in gs], len(gs), pairs_of, alias=True)


def _exchange(side, name):
    def body():
        pass

    _pcall(body, name=name, out_shape=[], in_specs=[], out_specs=[], sides=[side])()
    return side.results


def _add_sibling(p, r, core, name):
    _, rr, cc = p.shape
    h = rr // 2
    th = _pick(h, max(16, (2 << 20) // (2 * cc)), 16)
    nb = h // th

    def body(c_ref, p_ref, r_ref, o_ref):
        o_ref[...] = (p_ref[...].astype(F32) + r_ref[...].astype(F32)).astype(BF16)

    return _pcall(body, name=name, out_shape=jax.ShapeDtypeStruct((4, h, cc), BF16), grid=(4, nb),
                  in_specs=[pl.BlockSpec((None, th, cc), lambda k, i, c_ref: (k, c_ref[0] * nb + i, 0)),
                            pl.BlockSpec((None, th, cc), lambda k, i, c_ref: (k, i, 0))],
                  out_specs=pl.BlockSpec((None, th, cc), lambda k, i, c_ref: (k, i, 0)),
                  sem=("parallel", "parallel"), prefetch=1)(core, p, r)


def _add_chips(hh, pieces, chip_core, name):
    _, h, cc = hh.shape
    th = _pick(h, max(16, (2 << 20) // (2 * cc)), 16)
    nb = h // th

    def body(k_ref, h_ref, r0_ref, r1_ref, r2_ref, o_ref):
        s = h_ref[...].astype(F32) + r0_ref[...].astype(F32)
        s = s + r1_ref[...].astype(F32)
        o_ref[...] = s + r2_ref[...].astype(F32)

    def piece(slot):
        return pl.BlockSpec((None, th, cc), lambda i, k_ref: (slot, i, 0))

    return _pcall(body, name=name, out_shape=jax.ShapeDtypeStruct((2 * h, cc), F32), grid=(nb,),
                  in_specs=[pl.BlockSpec((None, th, cc), lambda i, k_ref: (k_ref[0], i, 0))] + [piece(s) for _, s in pieces],
                  out_specs=pl.BlockSpec((th, cc), lambda i, k_ref: (k_ref[1] * nb + i, 0)),
                  sem=("parallel",), prefetch=1)(chip_core, hh, *[a for a, _ in pieces])


def _adamw_math(w, g, m, v):
    m = ADAM_B1 * m + (1.0 - ADAM_B1) * g
    v = ADAM_B2 * v + (1.0 - ADAM_B2) * (g * g)
    m_hat = m / (1.0 - ADAM_B1 ** ADAM_STEP)
    v_hat = v / (1.0 - ADAM_B2 ** ADAM_STEP)
    delta = -ADAM_LR * (m_hat / (jnp.sqrt(v_hat) + ADAM_EPS) + ADAM_WD * w)
    return delta, m, v


def _adamw(w, g, m, v, name, sides=()):
    rr, cc = w.shape
    tr = _pick(rr, max(8, (3 << 20) // (4 * cc)), 8)

    def body(w_ref, g_ref, m_ref, v_ref, d_ref, mo_ref, vo_ref):
        d, mm, vv = _adamw_math(w_ref[...], g_ref[...], m_ref[...], v_ref[...])
        d_ref[...] = d
        mo_ref[...] = mm
        vo_ref[...] = vv

    spec = pl.BlockSpec((tr, cc), lambda i: (i, 0))
    return _pcall(body, name=name, out_shape=[jax.ShapeDtypeStruct(w.shape, F32)] * 3, grid=(rr // tr,),
                  in_specs=[spec] * 4, out_specs=[spec] * 3, sem=("parallel",), sides=sides)(w, g, m, v)


def _adamw_small(w, g8, m, v, name):
    def body(w_ref, g_ref, m_ref, v_ref, go_ref, d_ref, mo_ref, vo_ref):
        g = g_ref[0]
        for k in range(1, 8):
            g = g + g_ref[k]
        d, mm, vv = _adamw_math(w_ref[...], g, m_ref[...], v_ref[...])
        go_ref[...] = g
        d_ref[...] = d
        mo_ref[...] = mm
        vo_ref[...] = vv

    return _pcall(body, name=name, out_shape=[jax.ShapeDtypeStruct(w.shape, F32)] * 4,
                  in_specs=[VMEM] * 4, out_specs=[VMEM] * 4)(w, g8, m, v)


def _ada_fwd(c16, w_ada, b_ada, name):
    n = w_ada.shape[1]
    tn = _pick(n, 512, LANE)

    def body(c_ref, w_ref, b_ref, act_ref, mod_ref):
        cv = c_ref[...]
        act = cv * jax.nn.sigmoid(cv)
        act_ref[...] = act
        mod_ref[...] = _dot(act.astype(BF16), w_ref[...].astype(BF16)) + b_ref[...]

    return _pcall(body, name=name,
                  out_shape=[jax.ShapeDtypeStruct((16, D), F32), jax.ShapeDtypeStruct((16, n), F32)], grid=(n // tn,),
                  in_specs=[pl.BlockSpec((16, D), lambda j: (0, 0)), pl.BlockSpec((D, tn), lambda j: (0, j)),
                            pl.BlockSpec((1, tn), lambda j: (0, j))],
                  out_specs=[pl.BlockSpec((16, D), lambda j: (0, 0)), pl.BlockSpec((16, tn), lambda j: (0, j))],
                  sem=("arbitrary",))(c16, w_ada, b_ada)


def _ada_bwd(act, dmod, name, sides=()):
    n = dmod.shape[2]
    tm, tn = 256, _pick(n, 512, LANE)

    def body(a_ref, d_ref, o_ref):
        def term(b):
            return a_ref[b].astype(BF16).astype(F32) * d_ref[b].astype(BF16).astype(F32)

        acc = term(0)
        for b in range(1, 8):
            acc = acc + term(b)
        o_ref[...] = acc

    return _pcall(body, name=name, out_shape=jax.ShapeDtypeStruct((D, n), F32), grid=(D // tm, n // tn),
                  in_specs=[pl.BlockSpec((8, tm, 1), lambda i, j: (0, i, 0)), pl.BlockSpec((8, 1, tn), lambda i, j: (0, 0, j))],
                  out_specs=pl.BlockSpec((tm, tn), lambda i, j: (i, j)), sem=("parallel", "parallel"), sides=sides)(act, dmod)


def _norm_mod(x, g, sh, sc, name, sides=()):
    t = x.shape[0]
    tm = _pick(t, 512, 8)

    def body(x_ref, g_ref, sh_ref, sc_ref, h_ref):
        xf = x_ref[...]
        r = lax.rsqrt(jnp.mean(xf * xf, axis=-1, keepdims=True) + EPS)
        h = (xf * r) * g_ref[...]
        h_ref[...] = (h * (1.0 + sc_ref[...]) + sh_ref[...]).astype(BF16)

    vec = pl.BlockSpec((1, D), lambda i: (0, 0))
    row = pl.BlockSpec((tm, D), lambda i: (i, 0))
    return _pcall(body, name=name, out_shape=jax.ShapeDtypeStruct((t, D), BF16), grid=(t // tm,),
                  in_specs=[row, vec, vec, vec], out_specs=row, sem=("parallel",), sides=sides)(x, g, sh, sc)


def _norm_mod_bwd(dh, x, dxo, g, sc, name, sides=()):
    t = x.shape[0]
    tm = _pick(t, 256, 8)

    def body(dh_ref, x_ref, dxo_ref, g_ref, sc_ref, dx_ref, dsh_ref, dsc_ref, dg_ref):
        @pl.when(pl.program_id(0) == 0)
        def _():
            dsh_ref[...] = jnp.zeros_like(dsh_ref)
            dsc_ref[...] = jnp.zeros_like(dsc_ref)
            dg_ref[...] = jnp.zeros_like(dg_ref)

        xf, dh_ = x_ref[...], dh_ref[...]
        r = lax.rsqrt(jnp.mean(xf * xf, axis=-1, keepdims=True) + EPS)
        xhat = xf * r
        dsh_ref[...] += jnp.sum(dh_, axis=0, keepdims=True)
        dsc_ref[...] += jnp.sum(dh_ * (xhat * g_ref[...]), axis=0, keepdims=True)
        tt = dh_ * (1.0 + sc_ref[...])
        dg_ref[...] += jnp.sum(tt * xhat, axis=0, keepdims=True)
        dxh = tt * g_ref[...]
        dx_ref[...] = r * (dxh - xhat * jnp.mean(dxh * xhat, axis=-1, keepdims=True)) + dxo_ref[...]

    vec = pl.BlockSpec((1, D), lambda i: (0, 0))
    row = pl.BlockSpec((tm, D), lambda i: (i, 0))
    vshape = jax.ShapeDtypeStruct((1, D), F32)
    return _pcall(body, name=name, out_shape=[jax.ShapeDtypeStruct((t, D), F32), vshape, vshape, vshape], grid=(t // tm,),
                  in_specs=[row, row, row, vec, vec], out_specs=[row, vec, vec, vec], sem=("arbitrary",), sides=sides)(dh, x, dxo, g, sc)


def _gate_bwd(dxo, yy, gate, name):
    t = dxo.shape[0]
    tm = _pick(t, 512, 8)

    def body(dx_ref, y_ref, g_ref, dy_ref, dg_ref):
        @pl.when(pl.program_id(0) == 0)
        def _():
            dg_ref[...] = jnp.zeros_like(dg_ref)

        dx = dx_ref[...]
        dy_ref[...] = (dx * g_ref[...]).astype(BF16)
        dg_ref[...] += jnp.sum(dx * y_ref[...], axis=0, keepdims=True)

    vec = pl.BlockSpec((1, D), lambda i: (0, 0))
    row = pl.BlockSpec((tm, D), lambda i: (i, 0))
    return _pcall(body, name=name, out_shape=[jax.ShapeDtypeStruct((t, D), BF16), jax.ShapeDtypeStruct((1, D), F32)],
                  grid=(t // tm,), in_specs=[row, row, vec], out_specs=[row, vec], sem=("arbitrary",))(dxo, yy, gate)


def _final_loss(x, g, tgt, name):
    t = x.shape[0]
    tm = _pick(t, 256, 8)

    def body(x_ref, g_ref, t_ref, loss_ref, dx_ref, dg_ref):
        @pl.when(pl.program_id(0) == 0)
        def _():
            loss_ref[...] = jnp.zeros_like(loss_ref)
            dg_ref[...] = jnp.zeros_like(dg_ref)

        xf = x_ref[...]
        r = lax.rsqrt(jnp.mean(xf * xf, axis=-1, keepdims=True) + EPS)
        xhat = xf * r
        e = xhat * g_ref[...] - t_ref[...]
        per_tok = jnp.mean(e * e, axis=-1, keepdims=True)
        loss_ref[...] += 0.5 * jnp.sum(per_tok, axis=0, keepdims=True)
        dy = e * (1.0 / D)
        dg_ref[...] += jnp.sum(dy * xhat, axis=0, keepdims=True)
        dxh = dy * g_ref[...]
        dx_ref[...] = r * (dxh - xhat * jnp.mean(dxh * xhat, axis=-1, keepdims=True))

    vec = pl.BlockSpec((1, D), lambda i: (0, 0))
    row = pl.BlockSpec((tm, D), lambda i: (i, 0))
    return _pcall(body, name=name,
                  out_shape=[jax.ShapeDtypeStruct((1, LANE), F32), jax.ShapeDtypeStruct((t, D), F32),
                             jax.ShapeDtypeStruct((1, D), F32)],
                  grid=(t // tm,), in_specs=[row, vec, row],
                  out_specs=[pl.BlockSpec((1, LANE), lambda i: (0, 0)), row, vec], sem=("arbitrary",))(x, g, tgt)


def _mm(a, b, out_dtype, name):
    m, k = a.shape
    n = b.shape[1]
    tm = _pick(m, 1024, 8)
    tn = _pick(n, 1408, LANE)

    def body(a_ref, b_ref, o_ref):
        o_ref[...] = _dot(a_ref[...], b_ref[...]).astype(out_dtype)

    return _pcall(body, name=name, out_shape=jax.ShapeDtypeStruct((m, n), out_dtype), grid=(m // tm, n // tn),
                  in_specs=[pl.BlockSpec((tm, k), lambda i, j: (i, 0)), pl.BlockSpec((k, tn), lambda i, j: (0, j))],
                  out_specs=pl.BlockSpec((tm, tn), lambda i, j: (i, j)), sem=("parallel", "parallel"))(a, b)


def _mm_resid(a, b, resid, gate, name, sides=()):
    m, k = a.shape
    n = b.shape[1]
    tm, tn, tk = _pick(m, 1024, 8), _pick(n, 1024, LANE), _pick(k, 1408, LANE)
    nk = k // tk

    def body(a_ref, b_ref, r_ref, g_ref, y_ref, o_ref, acc):
        kk = pl.program_id(2)

        @pl.when(kk == 0)
        def _():
            acc[...] = jnp.zeros_like(acc)

        acc[...] += _dot(a_ref[...], b_ref[...])

        @pl.when(kk == nk - 1)
        def _():
            y_ref[...] = acc[...].astype(BF16)
            o_ref[...] = r_ref[...] + g_ref[...] * acc[...]

    blk = pl.BlockSpec((tm, tn), lambda i, j, kk: (i, j))
    return _pcall(body, name=name, out_shape=[jax.ShapeDtypeStruct((m, n), BF16), jax.ShapeDtypeStruct((m, n), F32)],
                  grid=(m // tm, n // tn, nk),
                  in_specs=[pl.BlockSpec((tm, tk), lambda i, j, kk: (i, kk)), pl.BlockSpec((tk, tn), lambda i, j, kk: (kk, j)),
                            blk, pl.BlockSpec((1, tn), lambda i, j, kk: (0, j))],
                  out_specs=[blk, blk], scratch=[pltpu.VMEM((tm, tn), F32)],
                  sem=("parallel", "parallel", "arbitrary"), sides=sides)(a, b, resid, gate)


def _mm_nt(a, b, out_dtype, name, sides=()):
    m, k = a.shape
    n = b.shape[0]
    tm, tn, tk = _pick(m, 1024, 8), _pick(n, 1024, LANE), _pick(k, 2816, LANE)
    nk = k // tk

    def body(a_ref, b_ref, o_ref, acc):
        kk = pl.program_id(2)

        @pl.when(kk == 0)
        def _():
            acc[...] = jnp.zeros_like(acc)

        acc[...] += _dot(a_ref[...], b_ref[...], NT)

        @pl.when(kk == nk - 1)
        def _():
            o_ref[...] = acc[...].astype(out_dtype)

    return _pcall(body, name=name, out_shape=jax.ShapeDtypeStruct((m, n), out_dtype), grid=(m // tm, n // tn, nk),
                  in_specs=[pl.BlockSpec((tm, tk), lambda i, j, kk: (i, kk)), pl.BlockSpec((tn, tk), lambda i, j, kk: (j, kk))],
                  out_specs=pl.BlockSpec((tm, tn), lambda i, j, kk: (i, j)), scratch=[pltpu.VMEM((tm, tn), F32)],
                  sem=("parallel", "parallel", "arbitrary"), sides=sides)(a, b)


def _mm_tn(a, b, out_dtype, tm_cap, name, sides=()):
    t, m = a.shape
    n = b.shape[1]
    tm, tn, tk = _pick(m, tm_cap, LANE), _pick(n, 1408, LANE), _pick(t, 2048, 16)
    nk = t // tk

    def body(a_ref, b_ref, o_ref, acc):
        kk = pl.program_id(2)

        @pl.when(kk == 0)
        def _():
            acc[...] = jnp.zeros_like(acc)

        acc[...] += _dot(a_ref[...], b_ref[...], TN)

        @pl.when(kk == nk - 1)
        def _():
            o_ref[...] = acc[...].astype(out_dtype)

    return _pcall(body, name=name, out_shape=jax.ShapeDtypeStruct((m, n), out_dtype), grid=(m // tm, n // tn, nk),
                  in_specs=[pl.BlockSpec((tk, tm), lambda i, j, kk: (kk, i)), pl.BlockSpec((tk, tn), lambda i, j, kk: (kk, j))],
                  out_specs=pl.BlockSpec((tm, tn), lambda i, j, kk: (i, j)), scratch=[pltpu.VMEM((tm, tn), F32)],
                  sem=("parallel", "parallel", "arbitrary"), sides=sides)(a, b)


def _ffn_up(h, wg, name, sides=()):
    t = h.shape[0]
    fp = wg.shape[2]
    tm, tn = _pick(t, 2048, 8), 256
    nn = fp // tn

    def body(h_ref, wa_ref, wb_ref, a_ref, b_ref, u_ref):
        for r0 in range(0, tm, tm // FFN_ROW_CHUNKS):
            rows = slice(r0, r0 + tm // FFN_ROW_CHUNKS)
            hh = h_ref[rows, :]
            a = _dot(hh, wa_ref[...])
            b = _dot(hh, wb_ref[...])
            a_ref[rows, :] = a.astype(BF16)
            b_ref[rows, :] = b.astype(BF16)
            u_ref[rows, :] = (a * jax.nn.sigmoid(a) * b).astype(BF16)

    out = pl.BlockSpec((tm, tn), lambda i, j, n: (i, j * nn + n))
    return _pcall(body, name=name, out_shape=[jax.ShapeDtypeStruct((t, 2 * fp), BF16)] * 3, grid=(t // tm, 2, nn),
                  in_specs=[pl.BlockSpec((tm, D), lambda i, j, n: (i, 0)),
                            pl.BlockSpec((None, D, tn), lambda i, j, n: (j, 0, n)),
                            pl.BlockSpec((None, D, tn), lambda i, j, n: (j + 2, 0, n))],
                  out_specs=[out, out, out], sem=("parallel", "parallel", "parallel"), sides=sides)(h, wg, wg)


def _ffn_dab(dy, wo, a, b, name, sides=()):
    t = dy.shape[0]
    f2 = wo.shape[0]
    tm, tn = _pick(t, 2048, 8), 256

    def body(dy_ref, w_ref, a_ref, b_ref, o_ref):
        for r0 in range(0, tm, tm // FFN_ROW_CHUNKS):
            rows = slice(r0, r0 + tm // FFN_ROW_CHUNKS)
            du = _dot(dy_ref[rows, :], w_ref[...], NT)
            av, bv = a_ref[rows, :].astype(F32), b_ref[rows, :].astype(F32)
            sg = jax.nn.sigmoid(av)
            o_ref[0, rows, :] = (du * bv * (sg * (1.0 + av * (1.0 - sg)))).astype(BF16)
            o_ref[1, rows, :] = (du * (av * sg)).astype(BF16)

    blk = pl.BlockSpec((tm, tn), lambda i, n: (i, n))
    return _pcall(body, name=name, out_shape=jax.ShapeDtypeStruct((2, t, f2), BF16), grid=(t // tm, f2 // tn),
                  in_specs=[pl.BlockSpec((tm, D), lambda i, n: (i, 0)), pl.BlockSpec((tn, D), lambda i, n: (n, 0)), blk, blk],
                  out_specs=pl.BlockSpec((2, tm, tn), lambda i, n: (0, i, n)), sem=("parallel", "parallel"), sides=sides)(dy, wo, a, b)


def _ffn_dwin(h, dab, col_blocks, name, sides=()):
    t = h.shape[0]
    fp = dab.shape[2] // 2
    tm, tn, tk = _pick(fp, 1408, LANE), 1024, _pick(t, 2048, 16)
    nm, nk = fp // tm, t // tk
    n0, n1 = col_blocks

    def body(d_ref, h_ref, o_ref, acc):
        kk = pl.program_id(3)

        @pl.when(kk == 0)
        def _():
            acc[...] = jnp.zeros_like(acc)

        acc[...] += _dot(d_ref[...], h_ref[...], TN)

        @pl.when(kk == nk - 1)
        def _():
            o_ref[...] = acc[...].astype(BF16)

    return _pcall(body, name=name, out_shape=jax.ShapeDtypeStruct((4, fp, (n1 - n0) * tn), BF16), grid=(4, nm, n1 - n0, nk),
                  in_specs=[pl.BlockSpec((None, tk, tm), lambda u, i, n, kk: (u // 2, kk, (u % 2) * nm + i)),
                            pl.BlockSpec((tk, tn), lambda u, i, n, kk: (kk, n0 + n))],
                  out_specs=pl.BlockSpec((None, tm, tn), lambda u, i, n, kk: (u, i, n)),
                  scratch=[pltpu.VMEM((tm, tn), F32)],
                  sem=("parallel", "parallel", "parallel", "arbitrary"), sides=sides)(dab, h)


def _ffn_dh(dab, wg, name, sides=()):
    t = dab.shape[1]
    fp = wg.shape[2]
    tm, tn, tk = _pick(t, 1024, 8), 1024, _pick(fp, 2816, LANE)
    nkk = fp // tk

    def body(d_ref, w_ref, o_ref, acc):
        u, kk = pl.program_id(2), pl.program_id(3)

        @pl.when((u == 0) & (kk == 0))
        def _():
            acc[...] = jnp.zeros_like(acc)

        acc[...] += _dot(d_ref[...], w_ref[...], NT)

        @pl.when((u == 3) & (kk == nkk - 1))
        def _():
            o_ref[...] = acc[...]

    return _pcall(body, name=name, out_shape=jax.ShapeDtypeStruct((t, D), F32), grid=(t // tm, D // tn, 4, nkk),
                  in_specs=[pl.BlockSpec((None, tm, tk), lambda i, j, u, kk: (u // 2, i, (u % 2) * nkk + kk)),
                            pl.BlockSpec((None, tn, tk), lambda i, j, u, kk: (u, j, kk))],
                  out_specs=pl.BlockSpec((tm, tn), lambda i, j, u, kk: (i, j)), scratch=[pltpu.VMEM((tm, tn), F32)],
                  sem=("parallel", "parallel", "arbitrary", "arbitrary"), sides=sides)(dab, wg)


def _split3(v):
    hi = v.astype(BF16)
    r1 = v - hi.astype(F32)
    mid = r1.astype(BF16)
    lo = (r1 - mid.astype(F32)).astype(BF16)
    return hi, mid, lo


def _tri_sum(tri, v):
    hi, mid, lo = _split3(v)
    return (_dot(tri, hi) + _dot(tri, mid)) + _dot(tri, lo)


def _forget_fwd(proj, b_pad, name):
    t = proj.shape[0]
    tb = _pick(t, 256, 8)
    col = D_PROJ // LANE

    def body(f_ref, b_ref, o_ref, carry):
        @pl.when(pl.program_id(0) == 0)
        def _():
            carry[...] = jnp.zeros_like(carry)

        z = f_ref[...] + b_ref[...]
        lf = jnp.minimum(z, 0.0) - jnp.log(1.0 + jnp.exp(-jnp.abs(z)))
        r = lax.broadcasted_iota(jnp.int32, (tb, tb), 0)
        cidx = lax.broadcasted_iota(jnp.int32, (tb, tb), 1)
        tri = (r >= cidx).astype(BF16)
        o_ref[...] = _tri_sum(tri, lf) + carry[...]
        carry[...] += jnp.sum(lf, axis=0, keepdims=True)

    return _pcall(body, name=name, out_shape=jax.ShapeDtypeStruct((t, LANE), F32), grid=(t // tb,),
                  in_specs=[pl.BlockSpec((tb, LANE), lambda i: (i, col)), pl.BlockSpec((1, LANE), lambda i: (0, 0))],
                  out_specs=pl.BlockSpec((tb, LANE), lambda i: (i, 0)), scratch=[pltpu.VMEM((1, LANE), F32)],
                  sem=("arbitrary",))(proj, b_pad)


def _forget_bwd(d_cum, proj, b_pad, name):
    t = proj.shape[0]
    tb = _pick(t, 256, 8)
    nb = t // tb
    col = D_PROJ // LANE

    def body(d_ref, f_ref, b_ref, o_ref, db_ref, carry):
        @pl.when(pl.program_id(0) == 0)
        def _():
            carry[...] = jnp.zeros_like(carry)
            db_ref[...] = jnp.zeros_like(db_ref)

        dc = d_ref[...]
        r = lax.broadcasted_iota(jnp.int32, (tb, tb), 0)
        cidx = lax.broadcasted_iota(jnp.int32, (tb, tb), 1)
        tri = (r <= cidx).astype(BF16)
        dlf = _tri_sum(tri, dc) + carry[...]
        carry[...] += jnp.sum(dc, axis=0, keepdims=True)
        z = f_ref[...] + b_ref[...]
        lane = lax.broadcasted_iota(jnp.int32, (tb, LANE), 1)
        dz = jnp.where(lane < N_HEADS, dlf * jax.nn.sigmoid(-z), 0.0)
        o_ref[...] = dz.astype(BF16)
        db_ref[...] += jnp.sum(dz, axis=0, keepdims=True)

    return _pcall(body, name=name, out_shape=[jax.ShapeDtypeStruct((t, LANE), BF16), jax.ShapeDtypeStruct((1, LANE), F32)],
                  grid=(nb,),
                  in_specs=[pl.BlockSpec((tb, LANE), lambda i: (nb - 1 - i, 0)),
                            pl.BlockSpec((tb, LANE), lambda i: (nb - 1 - i, col)),
                            pl.BlockSpec((1, LANE), lambda i: (0, 0))],
                  out_specs=[pl.BlockSpec((tb, LANE), lambda i: (nb - 1 - i, 0)), pl.BlockSpec((1, LANE), lambda i: (0, 0))],
                  scratch=[pltpu.VMEM((1, LANE), F32)], sem=("arbitrary",))(d_cum, proj, b_pad)


def _head_norm(v, g):
    r = lax.rsqrt(jnp.mean(v * v, axis=-1, keepdims=True) + EPS)
    return v * r, r


def _qkv_prep(proj, qg, kg, name):
    t = proj.shape[0]
    tm = _pick(t, 1024, 8)

    def body(q_ref, k_ref, v_ref, qg_ref, kg_ref, qo_ref, ko_ref, vo_ref):
        qo_ref[...] = (_head_norm(q_ref[...], None)[0] * qg_ref[...]).astype(BF16)
        ko_ref[...] = (_head_norm(k_ref[...], None)[0] * kg_ref[...]).astype(BF16)
        vo_ref[...] = v_ref[...].astype(BF16)

    def blk(off):
        return pl.BlockSpec((tm, HEAD_DIM), lambda i, h: (i, off + h))

    vec = pl.BlockSpec((1, HEAD_DIM), lambda i, h: (0, 0))
    return _pcall(body, name=name, out_shape=[jax.ShapeDtypeStruct((t, D_ATTN), BF16)] * 3, grid=(t // tm, N_HEADS),
                  in_specs=[blk(0), blk(N_HEADS), blk(2 * N_HEADS), vec, vec], out_specs=[blk(0)] * 3,
                  sem=("parallel", "parallel"))(proj, proj, proj, qg, kg)


def _qk_norm_bwd(dqn, dkn, proj, qg, kg, name):
    t = proj.shape[0]
    tm = _pick(t, 1024, 8)

    def one(d_ref, v_ref, g_ref, o_ref, dg_ref):
        xhat, r = _head_norm(v_ref[...], None)
        d = d_ref[...]
        dg_ref[...] += jnp.sum(d * xhat, axis=0, keepdims=True)
        dxh = d * g_ref[...]
        o_ref[...] = (r * (dxh - xhat * jnp.mean(dxh * xhat, axis=-1, keepdims=True))).astype(BF16)

    def body(dq_ref, dk_ref, q_ref, k_ref, qg_ref, kg_ref, qo_ref, ko_ref, dqg_ref, dkg_ref):
        @pl.when((pl.program_id(0) == 0) & (pl.program_id(1) == 0))
        def _():
            dqg_ref[...] = jnp.zeros_like(dqg_ref)
            dkg_ref[...] = jnp.zeros_like(dkg_ref)

        one(dq_ref, q_ref, qg_ref, qo_ref, dqg_ref)
        one(dk_ref, k_ref, kg_ref, ko_ref, dkg_ref)

    def blk(off):
        return pl.BlockSpec((tm, HEAD_DIM), lambda i, h: (i, off + h))

    vec = pl.BlockSpec((1, HEAD_DIM), lambda i, h: (0, 0))
    vshape = jax.ShapeDtypeStruct((1, HEAD_DIM), F32)
    return _pcall(body, name=name, out_shape=[jax.ShapeDtypeStruct((t, D_ATTN), BF16)] * 2 + [vshape, vshape],
                  grid=(t // tm, N_HEADS),
                  in_specs=[blk(0), blk(0), blk(0), blk(N_HEADS), vec, vec], out_specs=[blk(0), blk(0), vec, vec],
                  sem=("arbitrary", "arbitrary"))(dqn, dkn, proj, proj, qg, kg)


ATTN_SCALE = HEAD_DIM ** -0.5


def _logits(q, k, fq, fk, diag, r0, tq, tk):
    s = _dot(q, k, NT) * ATTN_SCALE + (fq - fk)
    if diag:
        r = r0 + lax.broadcasted_iota(jnp.int32, (tq, tk), 0)
        cidx = lax.broadcasted_iota(jnp.int32, (tq, tk), 1)
        s = jnp.where(r >= cidx, s, -jnp.inf)
    return s


def _head_cols(hp):
    return [(hh, slice(hh * HEAD_DIM, (hh + 1) * HEAD_DIM)) for hh in range(hp)]


def _tri(tt, n, by_row):
    if by_row:
        i = sum((tt >= k * (k + 1) // 2).astype(jnp.int32) for k in range(1, n))
        return i, tt - i * (i + 1) // 2
    j = sum((tt >= k * n - k * (k - 1) // 2).astype(jnp.int32) for k in range(1, n))
    return j + tt - (j * n - j * (j - 1) // 2), j


def _attn_fwd(q, k, v, fq, fk, name, sides=()):
    t = q.shape[0]
    tq = tk = _pick(t, ATTN_BLOCK, LANE)
    nk = t // tk

    hp, rc = ATTN_HEADS[0], tq // ATTN_ROW_CHUNKS[0]

    def body(q_ref, k_ref, v_ref, fq_ref, fk_ref, o_ref, lse_ref, m_s, l_s, acc):
        i, j = _tri(pl.program_id(1), nk, True)

        @pl.when(j == 0)
        def _():
            m_s[...] = jnp.full_like(m_s, -jnp.inf)
            l_s[...] = jnp.zeros_like(l_s)
            acc[...] = jnp.zeros_like(acc)

        def step(diag):
            for hh, cols in _head_cols(hp):
                for r0 in range(0, tq, rc):
                    rows = slice(r0, r0 + rc)
                    s = _logits(q_ref[rows, cols], k_ref[:, cols], fq_ref[hh, rows, :], fk_ref[hh], diag, r0, rc, tk)
                    m_new = jnp.maximum(m_s[hh, rows, :], jnp.max(s, axis=-1, keepdims=True))
                    alpha = jnp.exp(m_s[hh, rows, :] - m_new)
                    p = jnp.exp(s - m_new)
                    l_s[hh, rows, :] = alpha * l_s[hh, rows, :] + jnp.sum(p, axis=-1, keepdims=True)
                    acc[rows, cols] = alpha * acc[rows, cols] + _dot(p.astype(BF16), v_ref[:, cols])
                    m_s[hh, rows, :] = m_new

        @pl.when(j < i)
        def _():
            step(False)

        @pl.when(j == i)
        def _():
            step(True)

        @pl.when(j == i)
        def _():
            for hh, cols in _head_cols(hp):
                o_ref[:, cols] = (acc[:, cols] / l_s[hh]).astype(BF16)
                lse_ref[hh] = m_s[hh] + jnp.log(l_s[hh])

    qb = pl.BlockSpec((tq, hp * HEAD_DIM), lambda h, tt: (_tri(tt, nk, True)[0], h))
    kb = pl.BlockSpec((tk, hp * HEAD_DIM), lambda h, tt: (_tri(tt, nk, True)[1], h))
    col = pl.BlockSpec((hp, tq, 1), lambda h, tt: (h, _tri(tt, nk, True)[0], 0))
    return _pcall(body, name=name,
                  out_shape=[jax.ShapeDtypeStruct((t, D_ATTN), BF16), jax.ShapeDtypeStruct((N_HEADS, t, 1), F32)],
                  grid=(N_HEADS // hp, nk * (nk + 1) // 2),
                  in_specs=[qb, kb, kb, col, pl.BlockSpec((hp, 1, tk), lambda h, tt: (h, 0, _tri(tt, nk, True)[1]))],
                  out_specs=[qb, col],
                  scratch=[pltpu.VMEM((hp, tq, 1), F32), pltpu.VMEM((hp, tq, 1), F32), pltpu.VMEM((tq, hp * HEAD_DIM), F32)],
                  sem=("parallel", "arbitrary"), sides=sides)(q, k, v, fq, fk)


def _attn_bwd_q(q, k, v, o, do, lse, fq, fk, name, sides=()):
    t = q.shape[0]
    tq = tk = _pick(t, ATTN_BLOCK, LANE)
    nk = t // tk

    hp, rc = ATTN_HEADS[1], tq // ATTN_ROW_CHUNKS[1]

    def body(q_ref, k_ref, v_ref, o_ref, do_ref, lse_ref, fq_ref, fk_ref, dq_ref, dl_ref, dfq_ref, acc, dl_s, df_s):
        i, j = _tri(pl.program_id(1), nk, True)

        @pl.when(j == 0)
        def _():
            acc[...] = jnp.zeros_like(acc)
            df_s[...] = jnp.zeros_like(df_s)
            for hh, cols in _head_cols(hp):
                dl_s[hh] = jnp.sum(do_ref[:, cols].astype(F32) * o_ref[:, cols].astype(F32), axis=-1, keepdims=True)

        def step(diag):
            for hh, cols in _head_cols(hp):
                for r0 in range(0, tq, rc):
                    rows = slice(r0, r0 + rc)
                    s = _logits(q_ref[rows, cols], k_ref[:, cols], fq_ref[hh, rows, :], fk_ref[hh], diag, r0, rc, tk)
                    p = jnp.exp(s - lse_ref[hh, rows, :])
                    dp = _dot(do_ref[rows, cols], v_ref[:, cols], NT)
                    ds = p * (dp - dl_s[hh, rows, :])
                    df_s[hh, rows, :] += jnp.sum(ds, axis=-1, keepdims=True)
                    acc[rows, cols] += _dot(ds.astype(BF16), k_ref[:, cols])

        @pl.when(j < i)
        def _():
            step(False)

        @pl.when(j == i)
        def _():
            step(True)

        @pl.when(j == i)
        def _():
            dq_ref[...] = acc[...] * ATTN_SCALE
            dl_ref[...] = dl_s[...]
            dfq_ref[...] = df_s[...]

    qb = pl.BlockSpec((tq, hp * HEAD_DIM), lambda h, tt: (_tri(tt, nk, True)[0], h))
    kb = pl.BlockSpec((tk, hp * HEAD_DIM), lambda h, tt: (_tri(tt, nk, True)[1], h))
    col = pl.BlockSpec((hp, tq, 1), lambda h, tt: (h, _tri(tt, nk, True)[0], 0))
    cshape = jax.ShapeDtypeStruct((N_HEADS, t, 1), F32)
    return _pcall(body, name=name, out_shape=[jax.ShapeDtypeStruct((t, D_ATTN), F32), cshape, cshape],
                  grid=(N_HEADS // hp, nk * (nk + 1) // 2),
                  in_specs=[qb, kb, kb, qb, qb, col, col, pl.BlockSpec((hp, 1, tk), lambda h, tt: (h, 0, _tri(tt, nk, True)[1]))],
                  out_specs=[qb, col, col],
                  scratch=[pltpu.VMEM((tq, hp * HEAD_DIM), F32), pltpu.VMEM((hp, tq, 1), F32), pltpu.VMEM((hp, tq, 1), F32)],
                  sem=("parallel", "arbitrary"), sides=sides)(q, k, v, o, do, lse, fq, fk)


def _attn_bwd_kv(q, k, v, do, lse, delta, fq, fk, name, sides=()):
    t = q.shape[0]
    tq = tk = _pick(t, ATTN_BLOCK, LANE)
    nq = t // tq

    hp, rc = ATTN_HEADS[2], tq // ATTN_ROW_CHUNKS[2]

    def body(q_ref, k_ref, v_ref, do_ref, lse_ref, dl_ref, fq_ref, fk_ref, dk_ref, dv_ref, dfk_ref, dk_s, dv_s, df_s):
        i, j = _tri(pl.program_id(1), nq, False)

        @pl.when(i == j)
        def _():
            dk_s[...] = jnp.zeros_like(dk_s)
            dv_s[...] = jnp.zeros_like(dv_s)
            df_s[...] = jnp.zeros_like(df_s)

        def step(diag):
            for hh, cols in _head_cols(hp):
                for r0 in range(0, tq, rc):
                    rows = slice(r0, r0 + rc)
                    s = _logits(q_ref[rows, cols], k_ref[:, cols], fq_ref[hh, rows, :], fk_ref[hh], diag, r0, rc, tk)
                    p = jnp.exp(s - lse_ref[hh, rows, :])
                    dv_s[:, cols] += _dot(p.astype(BF16), do_ref[rows, cols], TN)
                    dp = _dot(do_ref[rows, cols], v_ref[:, cols], NT)
                    ds = p * (dp - dl_ref[hh, rows, :])
                    df_s[hh] -= jnp.sum(ds, axis=0, keepdims=True)
                    dk_s[:, cols] += _dot(ds.astype(BF16), q_ref[rows, cols], TN)

        @pl.when(i > j)
        def _():
            step(False)

        @pl.when(i == j)
        def _():
            step(True)

        @pl.when(i == nq - 1)
        def _():
            dk_ref[...] = dk_s[...] * ATTN_SCALE
            dv_ref[...] = dv_s[...].astype(BF16)
            dfk_ref[...] = df_s[...]

    qb = pl.BlockSpec((tq, hp * HEAD_DIM), lambda h, tt: (_tri(tt, nq, False)[0], h))
    kb = pl.BlockSpec((tk, hp * HEAD_DIM), lambda h, tt: (_tri(tt, nq, False)[1], h))
    col = pl.BlockSpec((hp, tq, 1), lambda h, tt: (h, _tri(tt, nq, False)[0], 0))
    row = pl.BlockSpec((hp, 1, tk), lambda h, tt: (h, 0, _tri(tt, nq, False)[1]))
    return _pcall(body, name=name,
                  out_shape=[jax.ShapeDtypeStruct((t, D_ATTN), F32), jax.ShapeDtypeStruct((t, D_ATTN), BF16),
                             jax.ShapeDtypeStruct((N_HEADS, 1, t), F32)],
                  grid=(N_HEADS // hp, nq * (nq + 1) // 2),
                  in_specs=[qb, kb, kb, qb, col, col, col, row], out_specs=[kb, kb, row],
                  scratch=[pltpu.VMEM((tk, hp * HEAD_DIM), F32), pltpu.VMEM((tk, hp * HEAD_DIM), F32), pltpu.VMEM((hp, 1, tk), F32)],
                  sem=("parallel", "arbitrary"), sides=sides)(q, k, v, do, lse, delta, fq, fk)


def _window_sum(v, w, back):
    n = v.shape[0]
    k = 1
    while k < w:
        v = v + pltpu.roll(v, k if back else n - k, axis=0)
        k *= 2
    return v


def _pool_fwd(proj, pw, ps, name):
    t = proj.shape[0]
    tm = _pick(t, POOL_BLOCK, HALO)
    col = 3 * D_ATTN // D_POOL

    def body(u_ref, prev_ref, pw_ref, ps_ref, pooled_ref, out_ref):
        i = pl.program_id(0)
        prev = jnp.where(i > 0, prev_ref[...], 0.0)
        ext = jnp.concatenate([prev, u_ref[...]], axis=0)
        pos = i * tm + lax.broadcasted_iota(jnp.int32, (tm, 1), 0)
        for g, w in enumerate(POOL_WINDOWS):
            cols = slice(g * GROUP_DIM, (g + 1) * GROUP_DIM)
            xg = ext[:, cols]
            sw = _window_sum(xg, w, True)[HALO:, :]
            cnt = jnp.minimum(pos + 1, w).astype(F32)
            pooled = (sw / cnt - xg[HALO:, :]).astype(BF16)
            pooled_ref[:, cols] = pooled
            out_ref[:, cols] = (_dot(pooled, pw_ref[g]) * ps_ref[:, cols]).astype(BF16)

    row = pl.BlockSpec((tm, D_POOL), lambda i: (i, 0))
    return _pcall(body, name=name, out_shape=[jax.ShapeDtypeStruct((t, D_POOL), BF16)] * 2, grid=(t // tm,),
                  in_specs=[pl.BlockSpec((tm, D_POOL), lambda i: (i, col)),
                            pl.BlockSpec((HALO, D_POOL), lambda i: (jnp.maximum(i * (tm // HALO) - 1, 0), col)),
                            pl.BlockSpec((len(POOL_WINDOWS), GROUP_DIM, GROUP_DIM), lambda i: (0, 0, 0)),
                            pl.BlockSpec((1, D_POOL), lambda i: (0, 0))],
                  out_specs=[row, row], sem=("parallel",))(proj, proj, pw, ps)


def _pool_bwd(dout, pooled, pw, ps, name):
    t = pooled.shape[0]
    tm = _pick(t, POOL_BLOCK, HALO)
    nb = t // tm
    ng = len(POOL_WINDOWS)

    def body(d_ref, nxt_ref, p_ref, pw_ref, ps_ref, du_ref, dpw_ref, dps_ref):
        i = pl.program_id(0)

        @pl.when(i == 0)
        def _():
            dpw_ref[...] = jnp.zeros_like(dpw_ref)
            dps_ref[...] = jnp.zeros_like(dps_ref)

        nxt = jnp.where(i < nb - 1, nxt_ref[...].astype(F32), 0.0)
        ext = jnp.concatenate([d_ref[...].astype(F32), nxt], axis=0)
        pos = i * tm + lax.broadcasted_iota(jnp.int32, (tm + HALO, 1), 0)
        for g, w in enumerate(POOL_WINDOWS):
            cols = slice(g * GROUP_DIM, (g + 1) * GROUP_DIM)
            pooled_g = p_ref[:, cols]
            dg = ext[:, cols]
            pm = _dot(pooled_g, pw_ref[g])
            dps_ref[:, cols] += jnp.sum(dg[:tm, :] * pm, axis=0, keepdims=True)
            dpm = (dg * ps_ref[:, cols]).astype(BF16)
            dpw_ref[g] += _dot(pooled_g, dpm[:tm, :], TN)
            dpooled = _dot(dpm, pw_ref[g], NT)
            cnt = jnp.minimum(pos + 1, w).astype(F32)
            fwd = _window_sum(dpooled / cnt, w, False)
            du_ref[:, cols] = (fwd[:tm, :] - dpooled[:tm, :]).astype(BF16)

    row = pl.BlockSpec((tm, D_POOL), lambda i: (i, 0))
    return _pcall(body, name=name,
                  out_shape=[jax.ShapeDtypeStruct((t, D_POOL), BF16), jax.ShapeDtypeStruct((ng, GROUP_DIM, GROUP_DIM), F32),
                             jax.ShapeDtypeStruct((1, D_POOL), F32)],
                  grid=(nb,),
                  in_specs=[pl.BlockSpec((tm, D_POOL), lambda i: (i, 1)),
                            pl.BlockSpec((HALO, D_POOL), lambda i: (jnp.minimum((i + 1) * (tm // HALO), t // HALO - 1), 1)),
                            row, pl.BlockSpec((ng, GROUP_DIM, GROUP_DIM), lambda i: (0, 0, 0)),
                            pl.BlockSpec((1, D_POOL), lambda i: (0, 0))],
                  out_specs=[row, pl.BlockSpec((ng, GROUP_DIM, GROUP_DIM), lambda i: (0, 0, 0)),
                             pl.BlockSpec((1, D_POOL), lambda i: (0, 0))],
                  sem=("arbitrary",))(dout, dout, pooled, pw, ps)


def _pad_cols(w, n):
    return jnp.pad(w, ((0, 0), (0, n - w.shape[1])))


def kernel(x, c, w_ada, b_ada, ffn1_norm_g, ffn1_w_in, ffn1_w_out, mix_norm_g, w_in, b_forget, q_norm_g, k_norm_g, pool_w, pool_scale, w_out, ffn2_norm_g, ffn2_w_in, ffn2_w_out, final_norm_g, loss_target, m_w_ada, m_b_ada, m_ffn1_norm_g, m_ffn1_w_in, m_ffn1_w_out, m_mix_norm_g, m_w_in, m_b_forget, m_q_norm_g, m_k_norm_g, m_pool_w, m_pool_scale, m_w_out, m_ffn2_norm_g, m_ffn2_w_in, m_ffn2_w_out, m_final_norm_g, v_w_ada, v_b_ada, v_ffn1_norm_g, v_ffn1_w_in, v_ffn1_w_out, v_mix_norm_g, v_w_in, v_b_forget, v_q_norm_g, v_k_norm_g, v_pool_w, v_pool_scale, v_w_out, v_ffn2_norm_g, v_ffn2_w_in, v_ffn2_w_out, v_final_norm_g):
    ax, ay, ac = _coords()
    chip = 2 * ax + ay
    me = 2 * chip + ac
    chip_core = jnp.stack([chip, ac]).astype(jnp.int32)
    core_arr = jnp.reshape(ac, (1,)).astype(jnp.int32)

    t = x.shape[1]
    xs = x.reshape(t, D)
    tgt = loss_target.reshape(t, D)
    hu = ffn1_w_out.shape[1]
    hup = -(-hu // LANE) * LANE
    ws_in = w_in.shape[2]
    ws_in_pad = -(-ws_in // LANE) * LANE
    n_ada = w_ada.shape[2]

    def ffn_in_shard(w):
        w = w[0].astype(BF16)
        return jnp.concatenate([_pad_cols(w[:, :hu], hup), _pad_cols(w[:, hu:], hup)], axis=1)

    def ffn_out_shard(w):
        return jnp.pad(w[0].astype(BF16), ((0, hup - hu), (0, 0)))

    shards1 = [ffn_in_shard(ffn1_w_in), ffn_out_shard(ffn1_w_out)]
    shards_mix = [_pad_cols(w_in[0].astype(BF16), ws_in_pad), pool_w[0].astype(BF16).reshape(GROUP_DIM, GROUP_DIM),
                  w_out[0].astype(BF16)]
    shards2 = [ffn_in_shard(ffn2_w_in), ffn_out_shard(ffn2_w_out)]

    def add_siblings(parts, recv, tag):
        return [_add_sibling(p, r, core_arr, f"{tag}_add_sibling_{k}") for k, (p, r) in enumerate(zip(parts, recv))]

    def add_chips(halves, recv, tag):
        return [_add_chips(hh, [(r, 0), (r, 1), (r, 2)], chip_core, f"{tag}_add_chips_{k}") for k, (hh, r) in enumerate(zip(halves, recv))]

    g_in1, = _exchange(_gather(shards1[:1]), "gather_ffn1_in")

    c_all = _allgather_small(c.reshape(8, D // 8), True, "gather_c").reshape(8, D)
    c16 = jnp.pad(c_all, ((0, 8), (0, 0)))
    b_ada_mine = lax.dynamic_slice(b_ada, (0, chip * n_ada), (1, n_ada))
    act16, mod16 = _ada_fwd(c16, w_ada[0], b_ada_mine, "ada_fwd")
    mod_all = _allgather_small(mod16[:8], False, "gather_mod")
    mod = lax.dynamic_index_in_dim(mod_all, me, axis=1, keepdims=False).reshape(N_MOD, 1, D)
    sh1, sc1, gt1, sh2, sc2, gt2, sh3, sc3, gt3 = [mod[k] for k in range(N_MOD)]

    gate1, gate3 = 0.5 * gt1, 0.5 * gt3
    h1 = _norm_mod(xs, ffn1_norm_g, sh1, sc1, "ffn1_norm")
    job = _gather(shards1[1:])
    a1, b1, u1 = _ffn_up(h1, g_in1, "ffn1_up", sides=[job])
    wg_out1 = job.results[0].reshape(4 * hup, D)
    job = _gather(shards_mix)
    y1, x1 = _mm_resid(u1, wg_out1, xs, gate1, "ffn1_down", sides=[job])
    g_win, g_pw, g_wout = job.results
    h2 = _norm_mod(x1, mix_norm_g, sh2, sc2, "mix_norm")
    w_full = jnp.concatenate([g_win[k, :, :ws_in] for k in range(4)], axis=1)
    nf = 3 * D_ATTN
    w_all = jnp.concatenate([w_full[:, :nf], w_full[:, nf + N_HEADS:], w_full[:, nf:nf + N_HEADS],
                             jnp.zeros((D, LANE - N_HEADS), BF16)], axis=1)
    pw_full = g_pw.reshape(4, 4, GROUP_DIM // 4, GROUP_DIM).transpose(1, 0, 2, 3).reshape(4, GROUP_DIM, GROUP_DIM)
    wo_full = g_wout.reshape(4 * g_wout.shape[1], D)

    proj = _mm(h2, w_all, F32, "mix_proj")
    b_pad = jnp.pad(b_forget, ((0, 0), (0, LANE - N_HEADS)))
    cum = _forget_fwd(proj, b_pad, "forget_fwd")
    cum_t = cum[:, :N_HEADS].T
    fq, fk = cum_t.reshape(N_HEADS, t, 1), cum_t.reshape(N_HEADS, 1, t)
    qn, kn, vb = _qkv_prep(proj, q_norm_g, k_norm_g, "qkv_prep")
    job = _gather(shards2[:1])
    attn, lse = _attn_fwd(qn, kn, vb, fq, fk, "attn_fwd", sides=[job])
    g_in2, = job.results
    pooled, pool_out = _pool_fwd(proj, pw_full, pool_scale, "pool_fwd")
    cat = jnp.concatenate([attn, pool_out], axis=1)
    y_mix, x2 = _mm_resid(cat, wo_full, x1, gt2, "mix_out")

    h3 = _norm_mod(x2, ffn2_norm_g, sh3, sc3, "ffn2_norm")
    job = _gather(shards2[1:])
    a3, b3, u3 = _ffn_up(h3, g_in2, "ffn2_up", sides=[job])
    g_out2, = job.results
    wg_out2 = g_out2.reshape(4 * hup, D)
    y3, x3 = _mm_resid(u3, wg_out2, x2, gate3, "ffn2_down")

    loss_part, dx3, d_final_g = _final_loss(x3, final_norm_g.reshape(1, D), tgt, "final_loss")
    loss = lax.psum(loss_part[0, 0], ("x", "y", "c"))

    all_cols = (0, D // 1024)
    dy3, dgt3 = _gate_bwd(dx3, y3, gate3, "ffn2_gate_bwd")
    dab3 = _ffn_dab(dy3, wg_out2, a3, b3, "ffn2_dab")
    parts2 = [_ffn_dwin(h3, dab3, all_cols, "ffn2_dwin"), _mm_tn(u3, dy3, BF16, hup, "ffn2_dwout").reshape(4, hup, D)]
    rs = _reduce_siblings(parts2)
    dh3 = _ffn_dh(dab3, g_in2, "ffn2_dh", sides=[rs])
    half_in2, half_out2 = add_siblings(parts2, rs.results, "ffn2")
    dx2, dsh3, dsc3, d_ng3 = _norm_mod_bwd(dh3, x2, dx3, ffn2_norm_g, sc3, "ffn2_norm_bwd")

    dz, dgt2 = _gate_bwd(dx2, y_mix, gt2, "mix_gate_bwd")
    dcat = _mm_nt(dz, wo_full, BF16, "mix_dcat")
    d_wo = _mm_tn(cat, dz, BF16, 1024, "mix_dwout")
    du_pool, d_pw, d_ps = _pool_bwd(dcat, pooled, pw_full, pool_scale, "pool_bwd")
    rs = _reduce_chips([half_in2])
    dqn, delta, dfq = _attn_bwd_q(qn, kn, vb, attn, dcat, lse, fq, fk, "attn_bwd_q", sides=[rs])
    mine_in2, = add_chips([half_in2], rs.results, "ffn2_in")
    rs, rs3 = _reduce_chips([half_out2]), _share_siblings([mine_in2])
    dkn, dv, dfk = _attn_bwd_kv(qn, kn, vb, dcat, lse, delta, fq, fk, "attn_bwd_kv", sides=[rs, rs3])
    r_in2, = rs3.results
    mine_out2, = add_chips([half_out2], rs.results, "ffn2_out")
    d_cum = jnp.pad((dfq.reshape(N_HEADS, t) + dfk.reshape(N_HEADS, t)).T, ((0, 0), (0, LANE - N_HEADS)))
    dfl, d_bf = _forget_bwd(d_cum, proj, b_pad, "forget_bwd")
    dq, dk, d_qg, d_kg = _qk_norm_bwd(dqn, dkn, proj, q_norm_g, k_norm_g, "qk_norm_bwd")
    dproj = jnp.concatenate([dq, dk, dv, du_pool, dfl], axis=1)
    rs3 = _share_siblings([mine_out2])
    d_wall = _mm_tn(dproj, h2, F32, 1408, "mix_dwin", sides=[rs3])
    r_out2, = rs3.results
    d_wfull = jnp.concatenate([d_wall[:nf], d_wall[D_PROJ:D_PROJ + N_HEADS], d_wall[nf:D_PROJ]], axis=0)
    p_win = jnp.stack([jnp.pad(d_wfull[k * ws_in:(k + 1) * ws_in], ((0, ws_in_pad - ws_in), (0, 0))) for k in range(4)]).astype(BF16)
    p_pw = d_pw.reshape(4, 4, GROUP_DIM // 4, GROUP_DIM).transpose(1, 0, 2, 3).reshape(4, GROUP_DIM, GROUP_DIM).astype(BF16)
    parts_mix = [p_win, p_pw, d_wo.reshape(4, D // 4, D)]
    rs = _reduce_siblings(parts_mix)
    dh2 = _mm_nt(dproj, w_all, F32, "mix_dh", sides=[rs])
    halves_mix = add_siblings(parts_mix, rs.results, "mix")
    dx1, dsh2, dsc2, d_ng2 = _norm_mod_bwd(dh2, x1, dx2, mix_norm_g, sc2, "mix_norm_bwd")

    dy1, dgt1 = _gate_bwd(dx1, y1, gate1, "ffn1_gate_bwd")
    d_out1 = _mm_tn(u1, dy1, BF16, hup, "ffn1_dwout").reshape(4, hup, D)
    rs, rs1 = _reduce_chips(halves_mix), _reduce_siblings([d_out1])
    dab1 = _ffn_dab(dy1, wg_out1, a1, b1, "ffn1_dab", sides=[rs, rs1])
    mine_mix = add_chips(halves_mix, rs.results, "mix")
    half_out1, = add_siblings([d_out1], rs1.results, "ffn1_out")
    rs, rs3 = _reduce_chips([half_out1]), _share_siblings(mine_mix)
    d_in1_lo = _ffn_dwin(h1, dab1, (0, 1), "ffn1_dwin_lo", sides=[rs, rs3])
    r_win, r_pw, r_wo = rs3.results
    mine_out1, = add_chips([half_out1], rs.results, "ffn1_out")
    rs, rs3 = _reduce_siblings([d_in1_lo]), _share_siblings([mine_out1])
    d_in1_hi = _ffn_dwin(h1, dab1, (1, 2), "ffn1_dwin_hi", sides=[rs, rs3])
    r_out1, = rs3.results
    half_lo, = add_siblings([d_in1_lo], rs.results, "ffn1_lo")
    rs, rs1 = _reduce_chips([half_lo]), _reduce_siblings([d_in1_hi])
    dh1 = _ffn_dh(dab1, g_in1, "ffn1_dh", sides=[rs, rs1])
    mine_lo, = add_chips([half_lo], rs.results, "ffn1_lo")
    half_hi, = add_siblings([d_in1_hi], rs1.results, "ffn1_hi")
    dx0, dsh1, dsc1, d_ng1 = _norm_mod_bwd(dh1, xs, dx1, ffn1_norm_g, sc1, "ffn1_norm_bwd")
    grad_x = dx0.reshape(1, t, D)
    dgt1, dgt3 = 0.5 * dgt1, 0.5 * dgt3
    mine_hi, = add_chips([half_hi], _exchange(_reduce_chips([half_hi]), "ffn1_hi_reduce_chips"), "ffn1_hi")
    r_lo, r_hi = _exchange(_share_siblings([mine_lo, mine_hi]), "ffn1_share_siblings")
    r_in1 = jnp.concatenate([r_lo, r_hi], axis=1)
    grads = {
        "ffn1_w_in": jnp.concatenate([r_in1[:hu], r_in1[hup:hup + hu]], axis=0),
        "ffn1_w_out": r_out1[:hu],
        "w_in": r_win[:ws_in],
        "pool_w": r_pw,
        "w_out": r_wo,
        "ffn2_w_in": jnp.concatenate([r_in2[:hu], r_in2[hup:hup + hu]], axis=0),
        "ffn2_w_out": r_out2[:hu],
    }
    hidden_in_rows = ("ffn1_w_in", "ffn2_w_in", "w_in")

    dmod = jnp.concatenate([dsh1, dsc1, dgt1, dsh2, dsc2, dgt2, dsh3, dsc3, dgt3], axis=1)
    small_names = ["b_ada", "ffn1_norm_g", "mix_norm_g", "b_forget", "q_norm_g", "k_norm_g", "pool_scale", "ffn2_norm_g",
                   "final_norm_g"]
    small_grads = [dmod, d_ng1, d_ng2, d_bf[:, :N_HEADS], d_qg, d_kg, d_ps, d_ng3, d_final_g]
    small_w = [b_ada, ffn1_norm_g, mix_norm_g, b_forget, q_norm_g, k_norm_g, pool_scale, ffn2_norm_g, final_norm_g.reshape(1, D)]
    small_m = [m_b_ada, m_ffn1_norm_g, m_mix_norm_g, m_b_forget, m_q_norm_g, m_k_norm_g, m_pool_scale, m_ffn2_norm_g,
               m_final_norm_g.reshape(1, D)]
    small_v = [v_b_ada, v_ffn1_norm_g, v_mix_norm_g, v_b_forget, v_q_norm_g, v_k_norm_g, v_pool_scale, v_ffn2_norm_g,
               v_final_norm_g.reshape(1, D)]
    sizes = [g.shape[1] for g in small_grads]
    n_small = sum(sizes)
    n_pack = -(-n_small // (8 * LANE)) * (8 * LANE)

    def pack(vs, fill):
        flat = jnp.concatenate([v.reshape(1, -1) for v in vs], axis=1)
        return jnp.pad(flat, ((0, 0), (0, n_pack - n_small)), constant_values=fill).reshape(8, n_pack // 8)

    g8 = _allgather_small(pack(small_grads, 0.0), True, "gather_small_grads")
    gs, ds, ms, vs = _adamw_small(pack(small_w, 0.0), g8, pack(small_m, 0.0), pack(small_v, 1.0), "adamw_small")

    def unpack(p):
        flat = p.reshape(1, n_pack)
        out, off = {}, 0
        for nme, sz in zip(small_names, sizes):
            out[nme] = flat[:, off:off + sz]
            off += sz
        return out

    small = [unpack(p) for p in (gs, ds, ms, vs)]
    for dct in small:
        dct["final_norm_g"] = dct["final_norm_g"].reshape(D)

    dmod_all = g8.reshape(8, n_pack)[:, :N_MOD * D]
    dmod_mine = lax.dynamic_slice(dmod_all, (0, chip * n_ada), (8, n_ada))
    grads["w_ada"] = _ada_bwd(act16[:8].reshape(8, D, 1), dmod_mine.reshape(8, 1, n_ada), "ada_bwd")

    big = {"w_ada": (w_ada, m_w_ada, v_w_ada), "ffn1_w_in": (ffn1_w_in, m_ffn1_w_in, v_ffn1_w_in),
           "ffn1_w_out": (ffn1_w_out, m_ffn1_w_out, v_ffn1_w_out), "w_in": (w_in, m_w_in, v_w_in),
           "pool_w": (pool_w, m_pool_w, v_pool_w), "w_out": (w_out, m_w_out, v_w_out),
           "ffn2_w_in": (ffn2_w_in, m_ffn2_w_in, v_ffn2_w_in), "ffn2_w_out": (ffn2_w_out, m_ffn2_w_out, v_ffn2_w_out)}
    res = {}

    for nme, (w, m, v) in big.items():
        shp = w.shape
        g2 = grads[nme]
        if nme in hidden_in_rows:
            rows = g2.shape[0]
            more = -rows % 8

            def fit(a):
                return jnp.pad(a, ((0, more), (0, 0))) if more else a

            outs = _adamw(fit(w[0].T), fit(g2), fit(m[0].T), fit(v[0].T), f"adamw_{nme}")
            res[nme] = tuple(o[:rows].T.reshape(shp) for o in (g2, *outs))
            continue
        two = g2.shape
        d, mo, vo = _adamw(w.reshape(two), g2, m.reshape(two), v.reshape(two), f"adamw_{nme}")
        res[nme] = (g2.reshape(shp), d.reshape(shp), mo.reshape(shp), vo.reshape(shp))
    for nme in small_names:
        res[nme] = tuple(dct[nme] for dct in small)

    order = ["w_ada", "b_ada", "ffn1_norm_g", "ffn1_w_in", "ffn1_w_out", "mix_norm_g", "w_in", "b_forget", "q_norm_g",
             "k_norm_g", "pool_w", "pool_scale", "w_out", "ffn2_norm_g", "ffn2_w_in", "ffn2_w_out", "final_norm_g"]
    return (loss, grad_x, *[res[n][0] for n in order], *[res[n][1] for n in order], *[res[n][2] for n in order],
            *[res[n][3] for n in order])
```

```python
import functools

import jax
import jax.numpy as jnp
from jax import lax
from jax.experimental import pallas as pl
from jax.experimental.pallas import tpu as pltpu

F32 = jnp.float32
BF16 = jnp.bfloat16

D = 2048
N_HEADS = 8
HEAD_DIM = 128
D_ATTN = 1024
D_POOL = 1024
POOL_WINDOWS = (2, 4, 8, 16)
GROUP_DIM = 256
HALO = 16
N_MOD = 9
EPS = 1e-6
D_PROJ = 3 * D_ATTN + D_POOL
D_PROJ_PAD = D_PROJ + 128
LANE = 128
ATTN_BLOCK = 512
POOL_BLOCK = 512
FFN_ROW_CHUNKS = 4
ATTN_HEADS = (2, 2, 2)
ATTN_ROW_CHUNKS = (2, 2, 2)

ADAM_LR = 0.001
ADAM_B1 = 0.9
ADAM_B2 = 0.999
ADAM_EPS = 1e-08
ADAM_WD = 0.01
ADAM_STEP = 10

VMEM_LIMIT_V7X = 56 * 1024 * 1024
MESH_ID = pl.DeviceIdType.MESH
ANY = pl.BlockSpec(memory_space=pl.ANY)
VMEM = pl.BlockSpec(memory_space=pltpu.VMEM)

NT = (((1,), (1,)), ((), ()))
TN = (((0,), (0,)), ((), ()))


class _Side:
    def __init__(self, ins, outs, nsem, start, wait, alias=False, mids=()):
        self.ins, self.outs, self.nsem, self.start, self.wait, self.alias = list(ins), list(outs), nsem, start, wait, alias
        self.mids = list(mids)
        self.results = None


def _pcall(body, *, name, out_shape, grid=None, in_specs=None, out_specs=None, scratch=(), sem=None, prefetch=0, sides=()):
    sides = list(sides)
    single = not isinstance(out_shape, (list, tuple))
    shapes = [out_shape] if single else list(out_shape)
    in_specs = list(in_specs)
    ospecs = [out_specs] if single else list(out_specs)
    scratch = list(scratch)
    n_in, n_out, n_scr = len(in_specs), len(shapes), len(scratch)
    assert not (sides and prefetch)
    aliases = {}
    for sd in sides:
        if sd.alias:
            for k in range(len(sd.outs)):
                aliases[len(in_specs) + k] = len(shapes) + k
        in_specs += [ANY] * len(sd.ins)
        shapes += sd.outs
        ospecs += [ANY] * len(sd.outs)
        scratch += [pltpu.SemaphoreType.DMA((sd.nsem,)), pltpu.SemaphoreType.DMA((sd.nsem,))]

    def wrapped(*refs):
        ins, outs, scr = refs[:len(in_specs)], refs[len(in_specs):len(in_specs) + len(shapes)], refs[len(in_specs) + len(shapes):]
        step, steps = 0, 1
        for ax, g in enumerate(grid or ()):
            step, steps = step * g + pl.program_id(ax), steps * g

        def at(when, fn):
            if grid:
                pl.when(step == when)(fn)
            else:
                fn()

        i0, o0 = n_in, n_out
        for k, sd in enumerate(sides):
            refs_k = (ins[i0:i0 + len(sd.ins)], outs[o0:o0 + len(sd.outs)], scr[n_scr + 2 * k], scr[n_scr + 2 * k + 1])
            at(0, functools.partial(sd.start, *refs_k))
            for frac, fn in sd.mids:
                at(min(steps - 1, int(frac * steps)), functools.partial(fn, *refs_k))
            i0, o0 = i0 + len(sd.ins), o0 + len(sd.outs)
        body(*ins[:n_in], *outs[:n_out], *scr[:n_scr])
        i0, o0 = n_in, n_out
        for k, sd in enumerate(sides):
            refs_k = (ins[i0:i0 + len(sd.ins)], outs[o0:o0 + len(sd.outs)], scr[n_scr + 2 * k], scr[n_scr + 2 * k + 1])
            at(steps - 1, functools.partial(sd.wait, *refs_k))
            i0, o0 = i0 + len(sd.ins), o0 + len(sd.outs)

    params = dict(vmem_limit_bytes=VMEM_LIMIT_V7X)
    if sides and grid:
        params["dimension_semantics"] = ("arbitrary",) * len(grid)
    elif sem is not None:
        params["dimension_semantics"] = sem
    kw = dict(name=name, out_shape=shapes if (sides or not single) else shapes[0], compiler_params=pltpu.CompilerParams(**params))
    if aliases:
        kw["input_output_aliases"] = aliases
    final_ospecs = ospecs if (sides or not single) else ospecs[0]
    if prefetch:
        kw["grid_spec"] = pltpu.PrefetchScalarGridSpec(
            num_scalar_prefetch=prefetch, grid=grid, in_specs=in_specs, out_specs=final_ospecs, scratch_shapes=scratch)
    else:
        if grid is not None:
            kw["grid"] = grid
        kw["in_specs"] = in_specs
        kw["out_specs"] = final_ospecs
        kw["scratch_shapes"] = scratch
    call = pl.pallas_call(wrapped if sides else body, **kw)
    if not sides:
        return call

    def run(*operands):
        res = list(call(*operands, *[a for sd in sides for a in sd.ins]))
        o0 = n_out
        for sd in sides:
            sd.results = res[o0:o0 + len(sd.outs)]
            o0 += len(sd.outs)
        return res[0] if single else res[:n_out]

    return run


def _pick(n, cap, mult):
    best = None
    for d in range(mult, min(n, cap) + 1, mult):
        if n % d == 0:
            best = d
    assert best is not None, (n, cap, mult)
    return best


def _coords():
    return lax.axis_index("x"), lax.axis_index("y"), lax.axis_index("c")


def _dot(a, b, dims=None):
    if dims is None:
        return jnp.dot(a, b, preferred_element_type=F32)
    return lax.dot_general(a, b, dims, preferred_element_type=F32)


def _remote(src, dst, ssem, rsem, dev):
    return pltpu.make_async_remote_copy(src_ref=src, dst_ref=dst, send_sem=ssem, recv_sem=rsem,
                                        device_id=dev, device_id_type=MESH_ID)


def _allgather_small(v, whole_mesh, name):
    masks = list(range(1, 8)) if whole_mesh else [4, 2, 6]
    nslot = 8 if whole_mesh else 4

    def slot(px, py, pc):
        return 4 * px + 2 * py + pc if whole_mesh else 2 * px + py

    def body(v_ref, out_ref, ssem, rsem, lsem):
        x, y, c = _coords()
        mine = slot(x, y, c)
        peers = [(jnp.bitwise_xor(x, (m >> 2) & 1), jnp.bitwise_xor(y, (m >> 1) & 1), jnp.bitwise_xor(c, m & 1))
                 for m in masks]
        loc = pltpu.make_async_copy(v_ref, out_ref.at[mine], lsem)
        loc.start()
        sends = [_remote(v_ref, out_ref.at[mine], ssem.at[k], rsem.at[k], p) for k, p in enumerate(peers)]
        for cp in sends:
            cp.start()
        for k, p in enumerate(peers):
            _remote(v_ref, out_ref.at[slot(*p)], ssem.at[k], rsem.at[k], p).wait_recv()
        for cp in sends:
            cp.wait_send()
        loc.wait()

    return _pcall(body, name=name, out_shape=jax.ShapeDtypeStruct((nslot,) + v.shape, v.dtype),
                  in_specs=[VMEM], out_specs=VMEM,
                  scratch=[pltpu.SemaphoreType.DMA((len(masks),)), pltpu.SemaphoreType.DMA((len(masks),)),
                           pltpu.SemaphoreType.DMA(())])(v)


def _other_chips(x, y):
    return [(1 - x, y), (x, 1 - y), (1 - x, 1 - y)]


def _half_rows(shard_rows, core):
    h = shard_rows // 2
    return pl.ds(pl.multiple_of(core * h, 16), h)


def _later(src, dst, ssem, rsem, dev):
    return functools.partial(_remote, src, dst, ssem, rsem, dev)


def _side_from(ins, outs, nsem, pairs_of, alias=False):
    def start(*refs):
        for send, _ in pairs_of(*refs):
            send().start()

    def wait(*refs):
        pairs = pairs_of(*refs)
        for _, recv in pairs:
            recv().wait_recv()
        for send, _ in pairs:
            send().wait_send()

    return _Side(ins, outs, nsem, start, wait, alias)


GATHER_STAGES = (0.55, 0.8)
LAST_PIECE_FROM = 1536


def _gather(ws):
    def copies(w_refs, g_refs, ssem, rsem):
        x, y, c = _coords()
        me, sib = 2 * x + y, (x, y, 1 - c)
        across_x, across_y, far = 2 * (1 - x) + y, 2 * x + (1 - y), 2 * (1 - x) + (1 - y)
        to_x, to_y = (1 - x, y, c), (x, 1 - y, c)
        out = []
        for a, w in enumerate(ws):
            h = w.shape[0] // 2
            rows, theirs = _half_rows(w.shape[0], c), _half_rows(w.shape[0], 1 - c)
            first = pl.ds(pl.multiple_of(c * h, 16), h // 2)
            second = pl.ds(pl.multiple_of(c * h + h // 2, 16), h // 2)
            g = g_refs[a]

            def pair(k, src, dst, got, dev):
                sems = (ssem.at[8 * a + k], rsem.at[8 * a + k], dev)
                return _later(src, dst, *sems), _later(got, got, *sems)

            out.append(dict(
                x1=pair(0, w_refs[a].at[rows], g.at[me, rows], g.at[across_x, rows], to_x),
                y1=pair(1, w_refs[a].at[rows], g.at[me, rows], g.at[across_y, rows], to_y),
                x2=pair(2, g.at[across_y, first], g.at[across_y, first], g.at[far, first], to_x),
                y2=pair(3, g.at[across_x, second], g.at[across_x, second], g.at[far, second], to_y),
                sx=pair(4, g.at[across_x, rows], g.at[across_x, rows], g.at[across_x, theirs], sib),
                sy=pair(5, g.at[across_y, rows], g.at[across_y, rows], g.at[across_y, theirs], sib),
                sf=pair(6, g.at[far, rows], g.at[far, rows], g.at[far, theirs], sib),
                so=pair(7, w_refs[a], g.at[me], g.at[me], sib)))
        return out

    def start(*refs):
        for cp in copies(*refs):
            cp["x1"][0]().start()
            cp["y1"][0]().start()

    def stage2(*refs):
        for cp in copies(*refs):
            cp["x1"][1]().wait_recv()
            cp["y1"][1]().wait_recv()
            for k in ("x2", "y2", "sx", "sy", "so"):
                cp[k][0]().start()

    def stage3(*refs):
        for cp in copies(*refs):
            cp["x2"][1]().wait_recv()
            cp["y2"][1]().wait_recv()
            cp["sf"][0]().start()

    def wait(*refs):
        cps = copies(*refs)
        for cp in cps:
            for k in ("sx", "sy", "sf", "so"):
                cp[k][1]().wait_recv()
        for cp in cps:
            for send, _ in cp.values():
                send().wait_send()

    return _Side(ws, [jax.ShapeDtypeStruct((4,) + w.shape, w.dtype) for w in ws], 8 * len(ws), start, wait,
                 mids=[(GATHER_STAGES[0], stage2), (GATHER_STAGES[1], stage3)])


def _reduce_siblings(ps):
    def pairs_of(p_refs, r_refs, ssem, rsem):
        x, y, c = _coords()
        out = []
        for a, p in enumerate(ps):
            src = p_refs[a].at[:, _half_rows(p.shape[1], 1 - c), :]
            cp = _later(src, r_refs[a], ssem.at[a], rsem.at[a], (x, y, 1 - c))
            out.append((cp, cp))
        return out

    return _side_from(ps, [jax.ShapeDtypeStruct((4, p.shape[1] // 2, p.shape[2]), p.dtype) for p in ps], len(ps), pairs_of)


def _reduce_chips(hs, dests=(0, 1, 2)):
    nd = len(dests)

    def pairs_of(h_refs, o_refs, ssem, rsem):
        x, y, c = _coords()
        chips = _other_chips(x, y)
        out = []
        for a in range(len(hs)):
            for slot, j in enumerate(dests):
                chip = chips[j]
                cp = _later(h_refs[a].at[2 * chip[0] + chip[1]], o_refs[a].at[slot], ssem.at[nd * a + slot], rsem.at[nd * a + slot],
                            (*chip, c))
                out.append((cp, cp))
        return out

    return _side_from(hs, [jax.ShapeDtypeStruct((nd,) + h.shape[1:], h.dtype) for h in hs], nd * len(hs), pairs_of)


def _share_siblings(gs):
    def pairs_of(_, g_refs, ssem, rsem):
        x, y, c = _coords()
        out = []
        for a, g in enumerate(gs):
            mine, theirs = g_refs[a].at[_half_rows(g.shape[0], c)], g_refs[a].at[_half_rows(g.shape[0], 1 - c)]
            sems = (ssem.at[a], rsem.at[a], (x, y, 1 - c))
            out.append((_later(mine, mine, *sems), _later(theirs, theirs, *sems)))
        return out

    return _side_from(gs, [jax.ShapeDtypeStruct(g.shape, g.dtype) for g in gs], len(gs), pairs_of, alias=True)


def _exchange(side, name):
    def body():
        pass

    _pcall(body, name=name, out_shape=[], in_specs=[], out_specs=[], sides=[side])()
    return side.results


def _add_sibling(p, r, core, name):
    _, rr, cc = p.shape
    h = rr // 2
    th = _pick(h, max(16, (2 << 20) // (2 * cc)), 16)
    nb = h // th

    def body(c_ref, p_ref, r_ref, o_ref):
        o_ref[...] = (p_ref[...].astype(F32) + r_ref[...].astype(F32)).astype(BF16)

    return _pcall(body, name=name, out_shape=jax.ShapeDtypeStruct((4, h, cc), BF16), grid=(4, nb),
                  in_specs=[pl.BlockSpec((None, th, cc), lambda k, i, c_ref: (k, c_ref[0] * nb + i, 0)),
                            pl.BlockSpec((None, th, cc), lambda k, i, c_ref: (k, i, 0))],
                  out_specs=pl.BlockSpec((None, th, cc), lambda k, i, c_ref: (k, i, 0)),
                  sem=("parallel", "parallel"), prefetch=1)(core, p, r)


def _add_chips(hh, pieces, chip_core, name):
    _, h, cc = hh.shape
    th = _pick(h, max(16, (2 << 20) // (2 * cc)), 16)
    nb = h // th

    def body(k_ref, h_ref, r0_ref, r1_ref, r2_ref, o_ref):
        s = h_ref[...].astype(F32) + r0_ref[...].astype(F32)
        s = s + r1_ref[...].astype(F32)
        o_ref[...] = s + r2_ref[...].astype(F32)

    def piece(slot):
        return pl.BlockSpec((None, th, cc), lambda i, k_ref: (slot, i, 0))

    return _pcall(body, name=name, out_shape=jax.ShapeDtypeStruct((2 * h, cc), F32), grid=(nb,),
                  in_specs=[pl.BlockSpec((None, th, cc), lambda i, k_ref: (k_ref[0], i, 0))] + [piece(s) for _, s in pieces],
                  out_specs=pl.BlockSpec((th, cc), lambda i, k_ref: (k_ref[1] * nb + i, 0)),
                  sem=("parallel",), prefetch=1)(chip_core, hh, *[a for a, _ in pieces])


def _adamw_math(w, g, m, v):
    m = ADAM_B1 * m + (1.0 - ADAM_B1) * g
    v = ADAM_B2 * v + (1.0 - ADAM_B2) * (g * g)
    m_hat = m / (1.0 - ADAM_B1 ** ADAM_STEP)
    v_hat = v / (1.0 - ADAM_B2 ** ADAM_STEP)
    delta = -ADAM_LR * (m_hat / (jnp.sqrt(v_hat) + ADAM_EPS) + ADAM_WD * w)
    return delta, m, v


def _adamw(w, g, m, v, name, sides=()):
    rr, cc = w.shape
    tr = _pick(rr, max(8, (3 << 20) // (4 * cc)), 8)

    def body(w_ref, g_ref, m_ref, v_ref, d_ref, mo_ref, vo_ref):
        d, mm, vv = _adamw_math(w_ref[...], g_ref[...], m_ref[...], v_ref[...])
        d_ref[...] = d
        mo_ref[...] = mm
        vo_ref[...] = vv

    spec = pl.BlockSpec((tr, cc), lambda i: (i, 0))
    return _pcall(body, name=name, out_shape=[jax.ShapeDtypeStruct(w.shape, F32)] * 3, grid=(rr // tr,),
                  in_specs=[spec] * 4, out_specs=[spec] * 3, sem=("parallel",), sides=sides)(w, g, m, v)


def _adamw_small(w, g8, m, v, name):
    def body(w_ref, g_ref, m_ref, v_ref, go_ref, d_ref, mo_ref, vo_ref):
        g = g_ref[0]
        for k in range(1, 8):
            g = g + g_ref[k]
        d, mm, vv = _adamw_math(w_ref[...], g, m_ref[...], v_ref[...])
        go_ref[...] = g
        d_ref[...] = d
        mo_ref[...] = mm
        vo_ref[...] = vv

    return _pcall(body, name=name, out_shape=[jax.ShapeDtypeStruct(w.shape, F32)] * 4,
                  in_specs=[VMEM] * 4, out_specs=[VMEM] * 4)(w, g8, m, v)


def _ada_fwd(c16, w_ada, b_ada, name):
    n = w_ada.shape[1]
    tn = _pick(n, 512, LANE)

    def body(c_ref, w_ref, b_ref, act_ref, mod_ref):
        cv = c_ref[...]
        act = cv * jax.nn.sigmoid(cv)
        act_ref[...] = act
        mod_ref[...] = _dot(act.astype(BF16), w_ref[...].astype(BF16)) + b_ref[...]

    return _pcall(body, name=name,
                  out_shape=[jax.ShapeDtypeStruct((16, D), F32), jax.ShapeDtypeStruct((16, n), F32)], grid=(n // tn,),
                  in_specs=[pl.BlockSpec((16, D), lambda j: (0, 0)), pl.BlockSpec((D, tn), lambda j: (0, j)),
                            pl.BlockSpec((1, tn), lambda j: (0, j))],
                  out_specs=[pl.BlockSpec((16, D), lambda j: (0, 0)), pl.BlockSpec((16, tn), lambda j: (0, j))],
                  sem=("arbitrary",))(c16, w_ada, b_ada)


def _ada_bwd(act, dmod, name, sides=()):
    n = dmod.shape[2]
    tm, tn = 256, _pick(n, 512, LANE)

    def body(a_ref, d_ref, o_ref):
        def term(b):
            return a_ref[b].astype(BF16).astype(F32) * d_ref[b].astype(BF16).astype(F32)

        acc = term(0)
        for b in range(1, 8):
            acc = acc + term(b)
        o_ref[...] = acc

    return _pcall(body, name=name, out_shape=jax.ShapeDtypeStruct((D, n), F32), grid=(D // tm, n // tn),
                  in_specs=[pl.BlockSpec((8, tm, 1), lambda i, j: (0, i, 0)), pl.BlockSpec((8, 1, tn), lambda i, j: (0, 0, j))],
                  out_specs=pl.BlockSpec((tm, tn), lambda i, j: (i, j)), sem=("parallel", "parallel"), sides=sides)(act, dmod)


def _norm_mod(x, g, sh, sc, name, sides=()):
    t = x.shape[0]
    tm = _pick(t, 512, 8)

    def body(x_ref, g_ref, sh_ref, sc_ref, h_ref):
        xf = x_ref[...]
        r = lax.rsqrt(jnp.mean(xf * xf, axis=-1, keepdims=True) + EPS)
        h = (xf * r) * g_ref[...]
        h_ref[...] = (h * (1.0 + sc_ref[...]) + sh_ref[...]).astype(BF16)

    vec = pl.BlockSpec((1, D), lambda i: (0, 0))
    row = pl.BlockSpec((tm, D), lambda i: (i, 0))
    return _pcall(body, name=name, out_shape=jax.ShapeDtypeStruct((t, D), BF16), grid=(t // tm,),
                  in_specs=[row, vec, vec, vec], out_specs=row, sem=("parallel",), sides=sides)(x, g, sh, sc)


def _norm_mod_bwd(dh, x, dxo, g, sc, name, sides=()):
    t = x.shape[0]
    tm = _pick(t, 256, 8)

    def body(dh_ref, x_ref, dxo_ref, g_ref, sc_ref, dx_ref, dsh_ref, dsc_ref, dg_ref):
        @pl.when(pl.program_id(0) == 0)
        def _():
            dsh_ref[...] = jnp.zeros_like(dsh_ref)
            dsc_ref[...] = jnp.zeros_like(dsc_ref)
            dg_ref[...] = jnp.zeros_like(dg_ref)

        xf, dh_ = x_ref[...], dh_ref[...]
        r = lax.rsqrt(jnp.mean(xf * xf, axis=-1, keepdims=True) + EPS)
        xhat = xf * r
        dsh_ref[...] += jnp.sum(dh_, axis=0, keepdims=True)
        dsc_ref[...] += jnp.sum(dh_ * (xhat * g_ref[...]), axis=0, keepdims=True)
        tt = dh_ * (1.0 + sc_ref[...])
        dg_ref[...] += jnp.sum(tt * xhat, axis=0, keepdims=True)
        dxh = tt * g_ref[...]
        dx_ref[...] = r * (dxh - xhat * jnp.mean(dxh * xhat, axis=-1, keepdims=True)) + dxo_ref[...]

    vec = pl.BlockSpec((1, D), lambda i: (0, 0))
    row = pl.BlockSpec((tm, D), lambda i: (i, 0))
    vshape = jax.ShapeDtypeStruct((1, D), F32)
    return _pcall(body, name=name, out_shape=[jax.ShapeDtypeStruct((t, D), F32), vshape, vshape, vshape], grid=(t // tm,),
                  in_specs=[row, row, row, vec, vec], out_specs=[row, vec, vec, vec], sem=("arbitrary",), sides=sides)(dh, x, dxo, g, sc)


def _gate_bwd(dxo, yy, gate, name):
    t = dxo.shape[0]
    tm = _pick(t, 512, 8)

    def body(dx_ref, y_ref, g_ref, dy_ref, dg_ref):
        @pl.when(pl.program_id(0) == 0)
        def _():
            dg_ref[...] = jnp.zeros_like(dg_ref)

        dx = dx_ref[...]
        dy_ref[...] = (dx * g_ref[...]).astype(BF16)
        dg_ref[...] += jnp.sum(dx * y_ref[...], axis=0, keepdims=True)

    vec = pl.BlockSpec((1, D), lambda i: (0, 0))
    row = pl.BlockSpec((tm, D), lambda i: (i, 0))
    return _pcall(body, name=name, out_shape=[jax.ShapeDtypeStruct((t, D), BF16), jax.ShapeDtypeStruct((1, D), F32)],
                  grid=(t // tm,), in_specs=[row, row, vec], out_specs=[row, vec], sem=("arbitrary",))(dxo, yy, gate)


def _final_loss(x, g, tgt, name):
    t = x.shape[0]
    tm = _pick(t, 256, 8)

    def body(x_ref, g_ref, t_ref, loss_ref, dx_ref, dg_ref):
        @pl.when(pl.program_id(0) == 0)
        def _():
            loss_ref[...] = jnp.zeros_like(loss_ref)
            dg_ref[...] = jnp.zeros_like(dg_ref)

        xf = x_ref[...]
        r = lax.rsqrt(jnp.mean(xf * xf, axis=-1, keepdims=True) + EPS)
        xhat = xf * r
        e = xhat * g_ref[...] - t_ref[...]
        per_tok = jnp.mean(e * e, axis=-1, keepdims=True)
        loss_ref[...] += 0.5 * jnp.sum(per_tok, axis=0, keepdims=True)
        dy = e * (1.0 / D)
        dg_ref[...] += jnp.sum(dy * xhat, axis=0, keepdims=True)
        dxh = dy * g_ref[...]
        dx_ref[...] = r * (dxh - xhat * jnp.mean(dxh * xhat, axis=-1, keepdims=True))

    vec = pl.BlockSpec((1, D), lambda i: (0, 0))
    row = pl.BlockSpec((tm, D), lambda i: (i, 0))
    return _pcall(body, name=name,
                  out_shape=[jax.ShapeDtypeStruct((1, LANE), F32), jax.ShapeDtypeStruct((t, D), F32),
                             jax.ShapeDtypeStruct((1, D), F32)],
                  grid=(t // tm,), in_specs=[row, vec, row],
                  out_specs=[pl.BlockSpec((1, LANE), lambda i: (0, 0)), row, vec], sem=("arbitrary",))(x, g, tgt)


def _mm(a, b, out_dtype, name):
    m, k = a.shape
    n = b.shape[1]
    tm = _pick(m, 1024, 8)
    tn = _pick(n, 1408, LANE)

    def body(a_ref, b_ref, o_ref):
        o_ref[...] = _dot(a_ref[...], b_ref[...]).astype(out_dtype)

    return _pcall(body, name=name, out_shape=jax.ShapeDtypeStruct((m, n), out_dtype), grid=(m // tm, n // tn),
                  in_specs=[pl.BlockSpec((tm, k), lambda i, j: (i, 0)), pl.BlockSpec((k, tn), lambda i, j: (0, j))],
                  out_specs=pl.BlockSpec((tm, tn), lambda i, j: (i, j)), sem=("parallel", "parallel"))(a, b)


def _mm_resid(a, b, resid, gate, name, sides=()):
    m, k = a.shape
    n = b.shape[1]
    tm, tn, tk = _pick(m, 1024, 8), _pick(n, 1024, LANE), _pick(k, 1408, LANE)
    nk = k // tk

    def body(a_ref, b_ref, r_ref, g_ref, y_ref, o_ref, acc):
        kk = pl.program_id(2)

        @pl.when(kk == 0)
        def _():
            acc[...] = jnp.zeros_like(acc)

        acc[...] += _dot(a_ref[...], b_ref[...])

        @pl.when(kk == nk - 1)
        def _():
            y_ref[...] = acc[...].astype(BF16)
            o_ref[...] = r_ref[...] + g_ref[...] * acc[...]

    blk = pl.BlockSpec((tm, tn), lambda i, j, kk: (i, j))
    return _pcall(body, name=name, out_shape=[jax.ShapeDtypeStruct((m, n), BF16), jax.ShapeDtypeStruct((m, n), F32)],
                  grid=(m // tm, n // tn, nk),
                  in_specs=[pl.BlockSpec((tm, tk), lambda i, j, kk: (i, kk)), pl.BlockSpec((tk, tn), lambda i, j, kk: (kk, j)),
                            blk, pl.BlockSpec((1, tn), lambda i, j, kk: (0, j))],
                  out_specs=[blk, blk], scratch=[pltpu.VMEM((tm, tn), F32)],
                  sem=("parallel", "parallel", "arbitrary"), sides=sides)(a, b, resid, gate)


def _mm_nt(a, b, out_dtype, name, sides=()):
    m, k = a.shape
    n = b.shape[0]
    tm, tn, tk = _pick(m, 1024, 8), _pick(n, 1024, LANE), _pick(k, 2816, LANE)
    nk = k // tk

    def body(a_ref, b_ref, o_ref, acc):
        kk = pl.program_id(2)

        @pl.when(kk == 0)
        def _():
            acc[...] = jnp.zeros_like(acc)

        acc[...] += _dot(a_ref[...], b_ref[...], NT)

        @pl.when(kk == nk - 1)
        def _():
            o_ref[...] = acc[...].astype(out_dtype)

    return _pcall(body, name=name, out_shape=jax.ShapeDtypeStruct((m, n), out_dtype), grid=(m // tm, n // tn, nk),
                  in_specs=[pl.BlockSpec((tm, tk), lambda i, j, kk: (i, kk)), pl.BlockSpec((tn, tk), lambda i, j, kk: (j, kk))],
                  out_specs=pl.BlockSpec((tm, tn), lambda i, j, kk: (i, j)), scratch=[pltpu.VMEM((tm, tn), F32)],
                  sem=("parallel", "parallel", "arbitrary"), sides=sides)(a, b)


def _mm_tn(a, b, out_dtype, tm_cap, name, sides=()):
    t, m = a.shape
    n = b.shape[1]
    tm, tn, tk = _pick(m, tm_cap, LANE), _pick(n, 1408, LANE), _pick(t, 2048, 16)
    nk = t // tk

    def body(a_ref, b_ref, o_ref, acc):
        kk = pl.program_id(2)

        @pl.when(kk == 0)
        def _():
            acc[...] = jnp.zeros_like(acc)

        acc[...] += _dot(a_ref[...], b_ref[...], TN)

        @pl.when(kk == nk - 1)
        def _():
            o_ref[...] = acc[...].astype(out_dtype)

    return _pcall(body, name=name, out_shape=jax.ShapeDtypeStruct((m, n), out_dtype), grid=(m // tm, n // tn, nk),
                  in_specs=[pl.BlockSpec((tk, tm), lambda i, j, kk: (kk, i)), pl.BlockSpec((tk, tn), lambda i, j, kk: (kk, j))],
                  out_specs=pl.BlockSpec((tm, tn), lambda i, j, kk: (i, j)), scratch=[pltpu.VMEM((tm, tn), F32)],
                  sem=("parallel", "parallel", "arbitrary"), sides=sides)(a, b)


def _ffn_up(h, wg, name, sides=()):
    t = h.shape[0]
    fp = wg.shape[2]
    tm, tn = _pick(t, 2048, 8), 256
    nn = fp // tn

    def body(h_ref, wa_ref, wb_ref, a_ref, b_ref, u_ref):
        for r0 in range(0, tm, tm // FFN_ROW_CHUNKS):
            rows = slice(r0, r0 + tm // FFN_ROW_CHUNKS)
            hh = h_ref[rows, :]
            a = _dot(hh, wa_ref[...])
            b = _dot(hh, wb_ref[...])
            a_ref[rows, :] = a.astype(BF16)
            b_ref[rows, :] = b.astype(BF16)
            u_ref[rows, :] = (a * jax.nn.sigmoid(a) * b).astype(BF16)

    out = pl.BlockSpec((tm, tn), lambda i, j, n: (i, j * nn + n))
    return _pcall(body, name=name, out_shape=[jax.ShapeDtypeStruct((t, 2 * fp), BF16)] * 3, grid=(t // tm, 2, nn),
                  in_specs=[pl.BlockSpec((tm, D), lambda i, j, n: (i, 0)),
                            pl.BlockSpec((None, D, tn), lambda i, j, n: (j, 0, n)),
                            pl.BlockSpec((None, D, tn), lambda i, j, n: (j + 2, 0, n))],
                  out_specs=[out, out, out], sem=("parallel", "parallel", "parallel"), sides=sides)(h, wg, wg)


def _ffn_dab(dy, wo, a, b, name, sides=()):
    t = dy.shape[0]
    f2 = wo.shape[0]
    tm, tn = _pick(t, 2048, 8), 256

    def body(dy_ref, w_ref, a_ref, b_ref, o_ref):
        for r0 in range(0, tm, tm // FFN_ROW_CHUNKS):
            rows = slice(r0, r0 + tm // FFN_ROW_CHUNKS)
            du = _dot(dy_ref[rows, :], w_ref[...], NT)
            av, bv = a_ref[rows, :].astype(F32), b_ref[rows, :].astype(F32)
            sg = jax.nn.sigmoid(av)
            o_ref[0, rows, :] = (du * bv * (sg * (1.0 + av * (1.0 - sg)))).astype(BF16)
            o_ref[1, rows, :] = (du * (av * sg)).astype(BF16)

    blk = pl.BlockSpec((tm, tn), lambda i, n: (i, n))
    return _pcall(body, name=name, out_shape=jax.ShapeDtypeStruct((2, t, f2), BF16), grid=(t // tm, f2 // tn),
                  in_specs=[pl.BlockSpec((tm, D), lambda i, n: (i, 0)), pl.BlockSpec((tn, D), lambda i, n: (n, 0)), blk, blk],
                  out_specs=pl.BlockSpec((2, tm, tn), lambda i, n: (0, i, n)), sem=("parallel", "parallel"), sides=sides)(dy, wo, a, b)


def _ffn_dwin(h, dab, cols, name, sides=()):
    t = h.shape[0]
    fp = dab.shape[2] // 2
    tn = 1024 if (cols[0] % 1024 == 0 and cols[1] % 1024 == 0) else 512
    tm, tk = _pick(fp, 1408, LANE), _pick(t, 2048, 16)
    nm, nk = fp // tm, t // tk
    n0, n1 = cols[0] // tn, cols[1] // tn

    def body(d_ref, h_ref, o_ref, acc):
        kk = pl.program_id(3)

        @pl.when(kk == 0)
        def _():
            acc[...] = jnp.zeros_like(acc)

        acc[...] += _dot(d_ref[...], h_ref[...], TN)

        @pl.when(kk == nk - 1)
        def _():
            o_ref[...] = acc[...].astype(BF16)

    return _pcall(body, name=name, out_shape=jax.ShapeDtypeStruct((4, fp, (n1 - n0) * tn), BF16), grid=(4, nm, n1 - n0, nk),
                  in_specs=[pl.BlockSpec((None, tk, tm), lambda u, i, n, kk: (u // 2, kk, (u % 2) * nm + i)),
                            pl.BlockSpec((tk, tn), lambda u, i, n, kk: (kk, n0 + n))],
                  out_specs=pl.BlockSpec((None, tm, tn), lambda u, i, n, kk: (u, i, n)),
                  scratch=[pltpu.VMEM((tm, tn), F32)],
                  sem=("parallel", "parallel", "parallel", "arbitrary"), sides=sides)(dab, h)


def _ffn_dh(dab, wg, name, sides=()):
    t = dab.shape[1]
    fp = wg.shape[2]
    tm, tn, tk = _pick(t, 1024, 8), 1024, _pick(fp, 2816, LANE)
    nkk = fp // tk

    def body(d_ref, w_ref, o_ref, acc):
        u, kk = pl.program_id(2), pl.program_id(3)

        @pl.when((u == 0) & (kk == 0))
        def _():
            acc[...] = jnp.zeros_like(acc)

        acc[...] += _dot(d_ref[...], w_ref[...], NT)

        @pl.when((u == 3) & (kk == nkk - 1))
        def _():
            o_ref[...] = acc[...]

    return _pcall(body, name=name, out_shape=jax.ShapeDtypeStruct((t, D), F32), grid=(t // tm, D // tn, 4, nkk),
                  in_specs=[pl.BlockSpec((None, tm, tk), lambda i, j, u, kk: (u // 2, i, (u % 2) * nkk + kk)),
                            pl.BlockSpec((None, tn, tk), lambda i, j, u, kk: (u, j, kk))],
                  out_specs=pl.BlockSpec((tm, tn), lambda i, j, u, kk: (i, j)), scratch=[pltpu.VMEM((tm, tn), F32)],
                  sem=("parallel", "parallel", "arbitrary", "arbitrary"), sides=sides)(dab, wg)


def _split3(v):
    hi = v.astype(BF16)
    r1 = v - hi.astype(F32)
    mid = r1.astype(BF16)
    lo = (r1 - mid.astype(F32)).astype(BF16)
    return hi, mid, lo


def _tri_sum(tri, v):
    hi, mid, lo = _split3(v)
    return (_dot(tri, hi) + _dot(tri, mid)) + _dot(tri, lo)


def _forget_fwd(proj, b_pad, name):
    t = proj.shape[0]
    tb = _pick(t, 256, 8)
    col = D_PROJ // LANE

    def body(f_ref, b_ref, o_ref, carry):
        @pl.when(pl.program_id(0) == 0)
        def _():
            carry[...] = jnp.zeros_like(carry)

        z = f_ref[...] + b_ref[...]
        lf = jnp.minimum(z, 0.0) - jnp.log(1.0 + jnp.exp(-jnp.abs(z)))
        r = lax.broadcasted_iota(jnp.int32, (tb, tb), 0)
        cidx = lax.broadcasted_iota(jnp.int32, (tb, tb), 1)
        tri = (r >= cidx).astype(BF16)
        o_ref[...] = _tri_sum(tri, lf) + carry[...]
        carry[...] += jnp.sum(lf, axis=0, keepdims=True)

    return _pcall(body, name=name, out_shape=jax.ShapeDtypeStruct((t, LANE), F32), grid=(t // tb,),
                  in_specs=[pl.BlockSpec((tb, LANE), lambda i: (i, col)), pl.BlockSpec((1, LANE), lambda i: (0, 0))],
                  out_specs=pl.BlockSpec((tb, LANE), lambda i: (i, 0)), scratch=[pltpu.VMEM((1, LANE), F32)],
                  sem=("arbitrary",))(proj, b_pad)


def _forget_bwd(d_cum, proj, b_pad, name):
    t = proj.shape[0]
    tb = _pick(t, 256, 8)
    nb = t // tb
    col = D_PROJ // LANE

    def body(d_ref, f_ref, b_ref, o_ref, db_ref, carry):
        @pl.when(pl.program_id(0) == 0)
        def _():
            carry[...] = jnp.zeros_like(carry)
            db_ref[...] = jnp.zeros_like(db_ref)

        dc = d_ref[...]
        r = lax.broadcasted_iota(jnp.int32, (tb, tb), 0)
        cidx = lax.broadcasted_iota(jnp.int32, (tb, tb), 1)
        tri = (r <= cidx).astype(BF16)
        dlf = _tri_sum(tri, dc) + carry[...]
        carry[...] += jnp.sum(dc, axis=0, keepdims=True)
        z = f_ref[...] + b_ref[...]
        lane = lax.broadcasted_iota(jnp.int32, (tb, LANE), 1)
        dz = jnp.where(lane < N_HEADS, dlf * jax.nn.sigmoid(-z), 0.0)
        o_ref[...] = dz.astype(BF16)
        db_ref[...] += jnp.sum(dz, axis=0, keepdims=True)

    return _pcall(body, name=name, out_shape=[jax.ShapeDtypeStruct((t, LANE), BF16), jax.ShapeDtypeStruct((1, LANE), F32)],
                  grid=(nb,),
                  in_specs=[pl.BlockSpec((tb, LANE), lambda i: (nb - 1 - i, 0)),
                            pl.BlockSpec((tb, LANE), lambda i: (nb - 1 - i, col)),
                            pl.BlockSpec((1, LANE), lambda i: (0, 0))],
                  out_specs=[pl.BlockSpec((tb, LANE), lambda i: (nb - 1 - i, 0)), pl.BlockSpec((1, LANE), lambda i: (0, 0))],
                  scratch=[pltpu.VMEM((1, LANE), F32)], sem=("arbitrary",))(d_cum, proj, b_pad)


def _head_norm(v, g):
    r = lax.rsqrt(jnp.mean(v * v, axis=-1, keepdims=True) + EPS)
    return v * r, r


def _qkv_prep(proj, qg, kg, name):
    t = proj.shape[0]
    tm = _pick(t, 1024, 8)

    def body(q_ref, k_ref, v_ref, qg_ref, kg_ref, qo_ref, ko_ref, vo_ref):
        qo_ref[...] = (_head_norm(q_ref[...], None)[0] * qg_ref[...] * ATTN_SCALE).astype(BF16)
        ko_ref[...] = (_head_norm(k_ref[...], None)[0] * kg_ref[...]).astype(BF16)
        vo_ref[...] = v_ref[...].astype(BF16)

    def blk(off):
        return pl.BlockSpec((tm, HEAD_DIM), lambda i, h: (i, off + h))

    vec = pl.BlockSpec((1, HEAD_DIM), lambda i, h: (0, 0))
    return _pcall(body, name=name, out_shape=[jax.ShapeDtypeStruct((t, D_ATTN), BF16)] * 3, grid=(t // tm, N_HEADS),
                  in_specs=[blk(0), blk(N_HEADS), blk(2 * N_HEADS), vec, vec], out_specs=[blk(0)] * 3,
                  sem=("parallel", "parallel"))(proj, proj, proj, qg, kg)


def _qk_norm_bwd(dqn, dkn, proj, qg, kg, name):
    t = proj.shape[0]
    tm = _pick(t, 1024, 8)

    def one(d_ref, v_ref, g_ref, o_ref, dg_ref):
        xhat, r = _head_norm(v_ref[...], None)
        d = d_ref[...]
        dg_ref[...] += jnp.sum(d * xhat, axis=0, keepdims=True)
        dxh = d * g_ref[...]
        o_ref[...] = (r * (dxh - xhat * jnp.mean(dxh * xhat, axis=-1, keepdims=True))).astype(BF16)

    def body(dq_ref, dk_ref, q_ref, k_ref, qg_ref, kg_ref, qo_ref, ko_ref, dqg_ref, dkg_ref):
        @pl.when((pl.program_id(0) == 0) & (pl.program_id(1) == 0))
        def _():
            dqg_ref[...] = jnp.zeros_like(dqg_ref)
            dkg_ref[...] = jnp.zeros_like(dkg_ref)

        one(dq_ref, q_ref, qg_ref, qo_ref, dqg_ref)
        one(dk_ref, k_ref, kg_ref, ko_ref, dkg_ref)

    def blk(off):
        return pl.BlockSpec((tm, HEAD_DIM), lambda i, h: (i, off + h))

    vec = pl.BlockSpec((1, HEAD_DIM), lambda i, h: (0, 0))
    vshape = jax.ShapeDtypeStruct((1, HEAD_DIM), F32)
    return _pcall(body, name=name, out_shape=[jax.ShapeDtypeStruct((t, D_ATTN), BF16)] * 2 + [vshape, vshape],
                  grid=(t // tm, N_HEADS),
                  in_specs=[blk(0), blk(0), blk(0), blk(N_HEADS), vec, vec], out_specs=[blk(0), blk(0), vec, vec],
                  sem=("arbitrary", "arbitrary"))(dqn, dkn, proj, proj, qg, kg)


ATTN_SCALE = HEAD_DIM ** -0.5


def _logits(q, k, fq, fk, diag, r0, tq, tk):
    s = _dot(q, k, NT) + (fq - fk)
    if diag:
        r = r0 + lax.broadcasted_iota(jnp.int32, (tq, tk), 0)
        cidx = lax.broadcasted_iota(jnp.int32, (tq, tk), 1)
        s = jnp.where(r >= cidx, s, -jnp.inf)
    return s


def _head_cols(hp):
    return [(hh, slice(hh * HEAD_DIM, (hh + 1) * HEAD_DIM)) for hh in range(hp)]


def _tri(tt, n, by_row):
    if by_row:
        i = sum((tt >= k * (k + 1) // 2).astype(jnp.int32) for k in range(1, n))
        return i, tt - i * (i + 1) // 2
    j = sum((tt >= k * n - k * (k - 1) // 2).astype(jnp.int32) for k in range(1, n))
    return j + tt - (j * n - j * (j - 1) // 2), j


def _attn_fwd(q, k, v, fq, fk, name, sides=()):
    t = q.shape[0]
    tq = tk = _pick(t, ATTN_BLOCK, LANE)
    nk = t // tk

    hp, rc = ATTN_HEADS[0], tq // ATTN_ROW_CHUNKS[0]

    def body(q_ref, k_ref, v_ref, fq_ref, fk_ref, o_ref, lse_ref, m_s, l_s, acc):
        i, j = _tri(pl.program_id(1), nk, True)

        @pl.when(j == 0)
        def _():
            m_s[...] = jnp.full_like(m_s, -jnp.inf)
            l_s[...] = jnp.zeros_like(l_s)
            acc[...] = jnp.zeros_like(acc)

        def step(diag):
            for hh, cols in _head_cols(hp):
                for r0 in range(0, tq, rc):
                    rows = slice(r0, r0 + rc)
                    s = _logits(q_ref[rows, cols], k_ref[:, cols], fq_ref[hh, rows, :], fk_ref[hh], diag, r0, rc, tk)
                    m_new = jnp.maximum(m_s[hh, rows, :], jnp.max(s, axis=-1, keepdims=True))
                    alpha = jnp.exp(m_s[hh, rows, :] - m_new)
                    p = jnp.exp(s - m_new)
                    l_s[hh, rows, :] = alpha * l_s[hh, rows, :] + jnp.sum(p, axis=-1, keepdims=True)
                    acc[rows, cols] = alpha * acc[rows, cols] + _dot(p.astype(BF16), v_ref[:, cols])
                    m_s[hh, rows, :] = m_new

        @pl.when(j < i)
        def _():
            step(False)

        @pl.when(j == i)
        def _():
            step(True)

        @pl.when(j == i)
        def _():
            for hh, cols in _head_cols(hp):
                o_ref[:, cols] = (acc[:, cols] / l_s[hh]).astype(BF16)
                lse_ref[hh] = m_s[hh] + jnp.log(l_s[hh])

    qb = pl.BlockSpec((tq, hp * HEAD_DIM), lambda h, tt: (_tri(tt, nk, True)[0], h))
    kb = pl.BlockSpec((tk, hp * HEAD_DIM), lambda h, tt: (_tri(tt, nk, True)[1], h))
    col = pl.BlockSpec((hp, tq, 1), lambda h, tt: (h, _tri(tt, nk, True)[0], 0))
    return _pcall(body, name=name,
                  out_shape=[jax.ShapeDtypeStruct((t, D_ATTN), BF16), jax.ShapeDtypeStruct((N_HEADS, t, 1), F32)],
                  grid=(N_HEADS // hp, nk * (nk + 1) // 2),
                  in_specs=[qb, kb, kb, col, pl.BlockSpec((hp, 1, tk), lambda h, tt: (h, 0, _tri(tt, nk, True)[1]))],
                  out_specs=[qb, col],
                  scratch=[pltpu.VMEM((hp, tq, 1), F32), pltpu.VMEM((hp, tq, 1), F32), pltpu.VMEM((tq, hp * HEAD_DIM), F32)],
                  sem=("parallel", "arbitrary"), sides=sides)(q, k, v, fq, fk)


def _attn_bwd_q(q, k, v, o, do, lse, fq, fk, name, sides=()):
    t = q.shape[0]
    tq = tk = _pick(t, ATTN_BLOCK, LANE)
    nk = t // tk

    hp, rc = ATTN_HEADS[1], tq // ATTN_ROW_CHUNKS[1]

    def body(q_ref, k_ref, v_ref, o_ref, do_ref, lse_ref, fq_ref, fk_ref, dq_ref, dl_ref, dfq_ref, acc, dl_s, df_s):
        i, j = _tri(pl.program_id(1), nk, True)

        @pl.when(j == 0)
        def _():
            acc[...] = jnp.zeros_like(acc)
            df_s[...] = jnp.zeros_like(df_s)
            for hh, cols in _head_cols(hp):
                dl_s[hh] = jnp.sum(do_ref[:, cols].astype(F32) * o_ref[:, cols].astype(F32), axis=-1, keepdims=True)

        def step(diag):
            for hh, cols in _head_cols(hp):
                for r0 in range(0, tq, rc):
                    rows = slice(r0, r0 + rc)
                    s = _logits(q_ref[rows, cols], k_ref[:, cols], fq_ref[hh, rows, :], fk_ref[hh], diag, r0, rc, tk)
                    p = jnp.exp(s - lse_ref[hh, rows, :])
                    dp = _dot(do_ref[rows, cols], v_ref[:, cols], NT)
                    ds = p * (dp - dl_s[hh, rows, :])
                    df_s[hh, rows, :] += jnp.sum(ds, axis=-1, keepdims=True)
                    acc[rows, cols] += _dot(ds.astype(BF16), k_ref[:, cols])

        @pl.when(j < i)
        def _():
            step(False)

        @pl.when(j == i)
        def _():
            step(True)

        @pl.when(j == i)
        def _():
            dq_ref[...] = acc[...] * ATTN_SCALE
            dl_ref[...] = dl_s[...]
            dfq_ref[...] = df_s[...]

    qb = pl.BlockSpec((tq, hp * HEAD_DIM), lambda h, tt: (_tri(tt, nk, True)[0], h))
    kb = pl.BlockSpec((tk, hp * HEAD_DIM), lambda h, tt: (_tri(tt, nk, True)[1], h))
    col = pl.BlockSpec((hp, tq, 1), lambda h, tt: (h, _tri(tt, nk, True)[0], 0))
    cshape = jax.ShapeDtypeStruct((N_HEADS, t, 1), F32)
    return _pcall(body, name=name, out_shape=[jax.ShapeDtypeStruct((t, D_ATTN), F32), cshape, cshape],
                  grid=(N_HEADS // hp, nk * (nk + 1) // 2),
                  in_specs=[qb, kb, kb, qb, qb, col, col, pl.BlockSpec((hp, 1, tk), lambda h, tt: (h, 0, _tri(tt, nk, True)[1]))],
                  out_specs=[qb, col, col],
                  scratch=[pltpu.VMEM((tq, hp * HEAD_DIM), F32), pltpu.VMEM((hp, tq, 1), F32), pltpu.VMEM((hp, tq, 1), F32)],
                  sem=("parallel", "arbitrary"), sides=sides)(q, k, v, o, do, lse, fq, fk)


def _attn_bwd_kv(q, k, v, do, lse, delta, fq, fk, name, sides=()):
    t = q.shape[0]
    tq = tk = _pick(t, ATTN_BLOCK, LANE)
    nq = t // tq

    hp, rc = ATTN_HEADS[2], tq // ATTN_ROW_CHUNKS[2]

    def body(q_ref, k_ref, v_ref, do_ref, lse_ref, dl_ref, fq_ref, fk_ref, dk_ref, dv_ref, dfk_ref, dk_s, dv_s, df_s):
        i, j = _tri(pl.program_id(1), nq, False)

        @pl.when(i == j)
        def _():
            dk_s[...] = jnp.zeros_like(dk_s)
            dv_s[...] = jnp.zeros_like(dv_s)
            df_s[...] = jnp.zeros_like(df_s)

        def step(diag):
            for hh, cols in _head_cols(hp):
                for r0 in range(0, tq, rc):
                    rows = slice(r0, r0 + rc)
                    s = _logits(q_ref[rows, cols], k_ref[:, cols], fq_ref[hh, rows, :], fk_ref[hh], diag, r0, rc, tk)
                    p = jnp.exp(s - lse_ref[hh, rows, :])
                    dv_s[:, cols] += _dot(p.astype(BF16), do_ref[rows, cols], TN)
                    dp = _dot(do_ref[rows, cols], v_ref[:, cols], NT)
                    ds = p * (dp - dl_ref[hh, rows, :])
                    df_s[hh] -= jnp.sum(ds, axis=0, keepdims=True)
                    dk_s[:, cols] += _dot(ds.astype(BF16), q_ref[rows, cols], TN)

        @pl.when(i > j)
        def _():
            step(False)

        @pl.when(i == j)
        def _():
            step(True)

        @pl.when(i == nq - 1)
        def _():
            dk_ref[...] = dk_s[...]
            dv_ref[...] = dv_s[...].astype(BF16)
            dfk_ref[...] = df_s[...]

    qb = pl.BlockSpec((tq, hp * HEAD_DIM), lambda h, tt: (_tri(tt, nq, False)[0], h))
    kb = pl.BlockSpec((tk, hp * HEAD_DIM), lambda h, tt: (_tri(tt, nq, False)[1], h))
    col = pl.BlockSpec((hp, tq, 1), lambda h, tt: (h, _tri(tt, nq, False)[0], 0))
    row = pl.BlockSpec((hp, 1, tk), lambda h, tt: (h, 0, _tri(tt, nq, False)[1]))
    return _pcall(body, name=name,
                  out_shape=[jax.ShapeDtypeStruct((t, D_ATTN), F32), jax.ShapeDtypeStruct((t, D_ATTN), BF16),
                             jax.ShapeDtypeStruct((N_HEADS, 1, t), F32)],
                  grid=(N_HEADS // hp, nq * (nq + 1) // 2),
                  in_specs=[qb, kb, kb, qb, col, col, col, row], out_specs=[kb, kb, row],
                  scratch=[pltpu.VMEM((tk, hp * HEAD_DIM), F32), pltpu.VMEM((tk, hp * HEAD_DIM), F32), pltpu.VMEM((hp, 1, tk), F32)],
                  sem=("parallel", "arbitrary"), sides=sides)(q, k, v, do, lse, delta, fq, fk)


def _window_sum(v, w, back):
    n = v.shape[0]
    k = 1
    while k < w:
        v = v + pltpu.roll(v, k if back else n - k, axis=0)
        k *= 2
    return v


def _pool_fwd(proj, pw, ps, name):
    t = proj.shape[0]
    tm = _pick(t, POOL_BLOCK, HALO)
    col = 3 * D_ATTN // D_POOL

    def body(u_ref, prev_ref, pw_ref, ps_ref, pooled_ref, out_ref):
        i = pl.program_id(0)
        prev = jnp.where(i > 0, prev_ref[...], 0.0)
        ext = jnp.concatenate([prev, u_ref[...]], axis=0)
        pos = i * tm + lax.broadcasted_iota(jnp.int32, (tm, 1), 0)
        for g, w in enumerate(POOL_WINDOWS):
            cols = slice(g * GROUP_DIM, (g + 1) * GROUP_DIM)
            xg = ext[:, cols]
            sw = _window_sum(xg, w, True)[HALO:, :]
            cnt = jnp.minimum(pos + 1, w).astype(F32)
            pooled = (sw / cnt - xg[HALO:, :]).astype(BF16)
            pooled_ref[:, cols] = pooled
            out_ref[:, cols] = (_dot(pooled, pw_ref[g]) * ps_ref[:, cols]).astype(BF16)

    row = pl.BlockSpec((tm, D_POOL), lambda i: (i, 0))
    return _pcall(body, name=name, out_shape=[jax.ShapeDtypeStruct((t, D_POOL), BF16)] * 2, grid=(t // tm,),
                  in_specs=[pl.BlockSpec((tm, D_POOL), lambda i: (i, col)),
                            pl.BlockSpec((HALO, D_POOL), lambda i: (jnp.maximum(i * (tm // HALO) - 1, 0), col)),
                            pl.BlockSpec((len(POOL_WINDOWS), GROUP_DIM, GROUP_DIM), lambda i: (0, 0, 0)),
                            pl.BlockSpec((1, D_POOL), lambda i: (0, 0))],
                  out_specs=[row, row], sem=("parallel",))(proj, proj, pw, ps)


def _pool_bwd(dout, pooled, pw, ps, name):
    t = pooled.shape[0]
    tm = _pick(t, POOL_BLOCK, HALO)
    nb = t // tm
    ng = len(POOL_WINDOWS)

    def body(d_ref, nxt_ref, p_ref, pw_ref, ps_ref, du_ref, dpw_ref, dps_ref):
        i = pl.program_id(0)

        @pl.when(i == 0)
        def _():
            dpw_ref[...] = jnp.zeros_like(dpw_ref)
            dps_ref[...] = jnp.zeros_like(dps_ref)

        nxt = jnp.where(i < nb - 1, nxt_ref[...].astype(F32), 0.0)
        ext = jnp.concatenate([d_ref[...].astype(F32), nxt], axis=0)
        pos = i * tm + lax.broadcasted_iota(jnp.int32, (tm + HALO, 1), 0)
        for g, w in enumerate(POOL_WINDOWS):
            cols = slice(g * GROUP_DIM, (g + 1) * GROUP_DIM)
            pooled_g = p_ref[:, cols]
            dg = ext[:, cols]
            pm = _dot(pooled_g, pw_ref[g])
            dps_ref[:, cols] += jnp.sum(dg[:tm, :] * pm, axis=0, keepdims=True)
            dpm = (dg * ps_ref[:, cols]).astype(BF16)
            dpw_ref[g] += _dot(pooled_g, dpm[:tm, :], TN)
            dpooled = _dot(dpm, pw_ref[g], NT)
            cnt = jnp.minimum(pos + 1, w).astype(F32)
            fwd = _window_sum(dpooled / cnt, w, False)
            du_ref[:, cols] = (fwd[:tm, :] - dpooled[:tm, :]).astype(BF16)

    row = pl.BlockSpec((tm, D_POOL), lambda i: (i, 0))
    return _pcall(body, name=name,
                  out_shape=[jax.ShapeDtypeStruct((t, D_POOL), BF16), jax.ShapeDtypeStruct((ng, GROUP_DIM, GROUP_DIM), F32),
                             jax.ShapeDtypeStruct((1, D_POOL), F32)],
                  grid=(nb,),
                  in_specs=[pl.BlockSpec((tm, D_POOL), lambda i: (i, 1)),
                            pl.BlockSpec((HALO, D_POOL), lambda i: (jnp.minimum((i + 1) * (tm // HALO), t // HALO - 1), 1)),
                            row, pl.BlockSpec((ng, GROUP_DIM, GROUP_DIM), lambda i: (0, 0, 0)),
                            pl.BlockSpec((1, D_POOL), lambda i: (0, 0))],
                  out_specs=[row, pl.BlockSpec((ng, GROUP_DIM, GROUP_DIM), lambda i: (0, 0, 0)),
                             pl.BlockSpec((1, D_POOL), lambda i: (0, 0))],
                  sem=("arbitrary",))(dout, dout, pooled, pw, ps)


def _pad_cols(w, n):
    return jnp.pad(w, ((0, 0), (0, n - w.shape[1])))


def kernel(x, c, w_ada, b_ada, ffn1_norm_g, ffn1_w_in, ffn1_w_out, mix_norm_g, w_in, b_forget, q_norm_g, k_norm_g, pool_w, pool_scale, w_out, ffn2_norm_g, ffn2_w_in, ffn2_w_out, final_norm_g, loss_target, m_w_ada, m_b_ada, m_ffn1_norm_g, m_ffn1_w_in, m_ffn1_w_out, m_mix_norm_g, m_w_in, m_b_forget, m_q_norm_g, m_k_norm_g, m_pool_w, m_pool_scale, m_w_out, m_ffn2_norm_g, m_ffn2_w_in, m_ffn2_w_out, m_final_norm_g, v_w_ada, v_b_ada, v_ffn1_norm_g, v_ffn1_w_in, v_ffn1_w_out, v_mix_norm_g, v_w_in, v_b_forget, v_q_norm_g, v_k_norm_g, v_pool_w, v_pool_scale, v_w_out, v_ffn2_norm_g, v_ffn2_w_in, v_ffn2_w_out, v_final_norm_g):
    ax, ay, ac = _coords()
    chip = 2 * ax + ay
    me = 2 * chip + ac
    chip_core = jnp.stack([chip, ac]).astype(jnp.int32)
    core_arr = jnp.reshape(ac, (1,)).astype(jnp.int32)

    t = x.shape[1]
    xs = x.reshape(t, D)
    tgt = loss_target.reshape(t, D)
    hu = ffn1_w_out.shape[1]
    hup = -(-hu // LANE) * LANE
    ws_in = w_in.shape[2]
    ws_in_pad = -(-ws_in // LANE) * LANE
    n_ada = w_ada.shape[2]

    def ffn_in_shard(w):
        w = w[0].astype(BF16)
        return jnp.concatenate([_pad_cols(w[:, :hu], hup), _pad_cols(w[:, hu:], hup)], axis=1)

    def ffn_out_shard(w):
        return jnp.pad(w[0].astype(BF16), ((0, hup - hu), (0, 0)))

    shards1 = [ffn_in_shard(ffn1_w_in), ffn_out_shard(ffn1_w_out)]
    shards_mix = [_pad_cols(w_in[0].astype(BF16), ws_in_pad), pool_w[0].astype(BF16).reshape(GROUP_DIM, GROUP_DIM),
                  w_out[0].astype(BF16)]
    shards2 = [ffn_in_shard(ffn2_w_in), ffn_out_shard(ffn2_w_out)]

    def add_siblings(parts, recv, tag):
        return [_add_sibling(p, r, core_arr, f"{tag}_add_sibling_{k}") for k, (p, r) in enumerate(zip(parts, recv))]

    def add_chips(halves, recv, tag):
        return [_add_chips(hh, [(r, 0), (r, 1), (r, 2)], chip_core, f"{tag}_add_chips_{k}") for k, (hh, r) in enumerate(zip(halves, recv))]

    g_in1, = _exchange(_gather(shards1[:1]), "gather_ffn1_in")

    c_all = _allgather_small(c.reshape(8, D // 8), True, "gather_c").reshape(8, D)
    c16 = jnp.pad(c_all, ((0, 8), (0, 0)))
    b_ada_mine = lax.dynamic_slice(b_ada, (0, chip * n_ada), (1, n_ada))
    act16, mod16 = _ada_fwd(c16, w_ada[0], b_ada_mine, "ada_fwd")
    mod_all = _allgather_small(mod16[:8], False, "gather_mod")
    mod = lax.dynamic_index_in_dim(mod_all, me, axis=1, keepdims=False).reshape(N_MOD, 1, D)
    sh1, sc1, gt1, sh2, sc2, gt2, sh3, sc3, gt3 = [mod[k] for k in range(N_MOD)]

    gate1, gate3 = 0.5 * gt1, 0.5 * gt3
    h1 = _norm_mod(xs, ffn1_norm_g, sh1, sc1, "ffn1_norm")
    job = _gather(shards1[1:])
    a1, b1, u1 = _ffn_up(h1, g_in1, "ffn1_up", sides=[job])
    wg_out1 = job.results[0].reshape(4 * hup, D)
    job = _gather(shards_mix)
    y1, x1 = _mm_resid(u1, wg_out1, xs, gate1, "ffn1_down", sides=[job])
    g_win, g_pw, g_wout = job.results
    h2 = _norm_mod(x1, mix_norm_g, sh2, sc2, "mix_norm")
    w_full = jnp.concatenate([g_win[k, :, :ws_in] for k in range(4)], axis=1)
    nf = 3 * D_ATTN
    w_all = jnp.concatenate([w_full[:, :nf], w_full[:, nf + N_HEADS:], w_full[:, nf:nf + N_HEADS],
                             jnp.zeros((D, LANE - N_HEADS), BF16)], axis=1)
    pw_full = g_pw.reshape(4, 4, GROUP_DIM // 4, GROUP_DIM).transpose(1, 0, 2, 3).reshape(4, GROUP_DIM, GROUP_DIM)
    wo_full = g_wout.reshape(4 * g_wout.shape[1], D)

    proj = _mm(h2, w_all, F32, "mix_proj")
    b_pad = jnp.pad(b_forget, ((0, 0), (0, LANE - N_HEADS)))
    cum = _forget_fwd(proj, b_pad, "forget_fwd")
    cum_t = cum[:, :N_HEADS].T
    fq, fk = cum_t.reshape(N_HEADS, t, 1), cum_t.reshape(N_HEADS, 1, t)
    qn, kn, vb = _qkv_prep(proj, q_norm_g, k_norm_g, "qkv_prep")
    job = _gather(shards2[:1])
    attn, lse = _attn_fwd(qn, kn, vb, fq, fk, "attn_fwd", sides=[job])
    g_in2, = job.results
    pooled, pool_out = _pool_fwd(proj, pw_full, pool_scale, "pool_fwd")
    cat = jnp.concatenate([attn, pool_out], axis=1)
    y_mix, x2 = _mm_resid(cat, wo_full, x1, gt2, "mix_out")

    h3 = _norm_mod(x2, ffn2_norm_g, sh3, sc3, "ffn2_norm")
    job = _gather(shards2[1:])
    a3, b3, u3 = _ffn_up(h3, g_in2, "ffn2_up", sides=[job])
    g_out2, = job.results
    wg_out2 = g_out2.reshape(4 * hup, D)
    y3, x3 = _mm_resid(u3, wg_out2, x2, gate3, "ffn2_down")

    loss_part, dx3, d_final_g = _final_loss(x3, final_norm_g.reshape(1, D), tgt, "final_loss")
    loss = lax.psum(loss_part[0, 0], ("x", "y", "c"))

    all_cols = (0, D)
    dy3, dgt3 = _gate_bwd(dx3, y3, gate3, "ffn2_gate_bwd")
    dab3 = _ffn_dab(dy3, wg_out2, a3, b3, "ffn2_dab")
    parts2 = [_ffn_dwin(h3, dab3, all_cols, "ffn2_dwin"), _mm_tn(u3, dy3, BF16, hup, "ffn2_dwout").reshape(4, hup, D)]
    rs = _reduce_siblings(parts2)
    dh3 = _ffn_dh(dab3, g_in2, "ffn2_dh", sides=[rs])
    half_in2, half_out2 = add_siblings(parts2, rs.results, "ffn2")
    dx2, dsh3, dsc3, d_ng3 = _norm_mod_bwd(dh3, x2, dx3, ffn2_norm_g, sc3, "ffn2_norm_bwd")

    dz, dgt2 = _gate_bwd(dx2, y_mix, gt2, "mix_gate_bwd")
    dcat = _mm_nt(dz, wo_full, BF16, "mix_dcat")
    d_wo = _mm_tn(cat, dz, BF16, 1024, "mix_dwout")
    du_pool, d_pw, d_ps = _pool_bwd(dcat, pooled, pw_full, pool_scale, "pool_bwd")
    rs = _reduce_chips([half_in2])
    dqn, delta, dfq = _attn_bwd_q(qn, kn, vb, attn, dcat, lse, fq, fk, "attn_bwd_q", sides=[rs])
    mine_in2, = add_chips([half_in2], rs.results, "ffn2_in")
    rs, rs3 = _reduce_chips([half_out2]), _share_siblings([mine_in2])
    dkn, dv, dfk = _attn_bwd_kv(qn, kn, vb, dcat, lse, delta, fq, fk, "attn_bwd_kv", sides=[rs, rs3])
    r_in2, = rs3.results
    mine_out2, = add_chips([half_out2], rs.results, "ffn2_out")
    d_cum = jnp.pad((dfq.reshape(N_HEADS, t) + dfk.reshape(N_HEADS, t)).T, ((0, 0), (0, LANE - N_HEADS)))
    dfl, d_bf = _forget_bwd(d_cum, proj, b_pad, "forget_bwd")
    dq, dk, d_qg, d_kg = _qk_norm_bwd(dqn, dkn, proj, q_norm_g, k_norm_g, "qk_norm_bwd")
    dproj = jnp.concatenate([dq, dk, dv, du_pool, dfl], axis=1)
    rs3 = _share_siblings([mine_out2])
    d_wall = _mm_tn(dproj, h2, F32, 1408, "mix_dwin", sides=[rs3])
    r_out2, = rs3.results
    d_wfull = jnp.concatenate([d_wall[:nf], d_wall[D_PROJ:D_PROJ + N_HEADS], d_wall[nf:D_PROJ]], axis=0)
    p_win = jnp.stack([jnp.pad(d_wfull[k * ws_in:(k + 1) * ws_in], ((0, ws_in_pad - ws_in), (0, 0))) for k in range(4)]).astype(BF16)
    p_pw = d_pw.reshape(4, 4, GROUP_DIM // 4, GROUP_DIM).transpose(1, 0, 2, 3).reshape(4, GROUP_DIM, GROUP_DIM).astype(BF16)
    parts_mix = [p_win, p_pw, d_wo.reshape(4, D // 4, D)]
    rs = _reduce_siblings(parts_mix)
    dh2 = _mm_nt(dproj, w_all, F32, "mix_dh", sides=[rs])
    halves_mix = add_siblings(parts_mix, rs.results, "mix")
    dx1, dsh2, dsc2, d_ng2 = _norm_mod_bwd(dh2, x1, dx2, mix_norm_g, sc2, "mix_norm_bwd")

    dy1, dgt1 = _gate_bwd(dx1, y1, gate1, "ffn1_gate_bwd")
    d_out1 = _mm_tn(u1, dy1, BF16, hup, "ffn1_dwout").reshape(4, hup, D)
    rs, rs1 = _reduce_chips(halves_mix), _reduce_siblings([d_out1])
    dab1 = _ffn_dab(dy1, wg_out1, a1, b1, "ffn1_dab", sides=[rs, rs1])
    mine_mix = add_chips(halves_mix, rs.results, "mix")
    half_out1, = add_siblings([d_out1], rs1.results, "ffn1_out")
    rs, rs3 = _reduce_chips([half_out1]), _share_siblings(mine_mix)
    d_in1_lo = _ffn_dwin(h1, dab1, (0, LAST_PIECE_FROM), "ffn1_dwin_lo", sides=[rs, rs3])
    r_win, r_pw, r_wo = rs3.results
    mine_out1, = add_chips([half_out1], rs.results, "ffn1_out")
    rs, rs3 = _reduce_siblings([d_in1_lo]), _share_siblings([mine_out1])
    d_in1_hi = _ffn_dwin(h1, dab1, (LAST_PIECE_FROM, D), "ffn1_dwin_hi", sides=[rs, rs3])
    r_out1, = rs3.results
    half_lo, = add_siblings([d_in1_lo], rs.results, "ffn1_lo")
    rs, rs1 = _reduce_chips([half_lo]), _reduce_siblings([d_in1_hi])
    dh1 = _ffn_dh(dab1, g_in1, "ffn1_dh", sides=[rs, rs1])
    mine_lo, = add_chips([half_lo], rs.results, "ffn1_lo")
    half_hi, = add_siblings([d_in1_hi], rs1.results, "ffn1_hi")
    dx0, dsh1, dsc1, d_ng1 = _norm_mod_bwd(dh1, xs, dx1, ffn1_norm_g, sc1, "ffn1_norm_bwd")
    grad_x = dx0.reshape(1, t, D)
    dgt1, dgt3 = 0.5 * dgt1, 0.5 * dgt3
    mine_hi, = add_chips([half_hi], _exchange(_reduce_chips([half_hi]), "ffn1_hi_reduce_chips"), "ffn1_hi")
    r_lo, r_hi = _exchange(_share_siblings([mine_lo, mine_hi]), "ffn1_share_siblings")
    r_in1 = jnp.concatenate([r_lo, r_hi], axis=1)
    grads = {
        "ffn1_w_in": jnp.concatenate([r_in1[:hu], r_in1[hup:hup + hu]], axis=0),
        "ffn1_w_out": r_out1[:hu],
        "w_in": r_win[:ws_in],
        "pool_w": r_pw,
        "w_out": r_wo,
        "ffn2_w_in": jnp.concatenate([r_in2[:hu], r_in2[hup:hup + hu]], axis=0),
        "ffn2_w_out": r_out2[:hu],
    }
    hidden_in_rows = ("ffn1_w_in", "ffn2_w_in", "w_in")

    dmod = jnp.concatenate([dsh1, dsc1, dgt1, dsh2, dsc2, dgt2, dsh3, dsc3, dgt3], axis=1)
    small_names = ["b_ada", "ffn1_norm_g", "mix_norm_g", "b_forget", "q_norm_g", "k_norm_g", "pool_scale", "ffn2_norm_g",
                   "final_norm_g"]
    small_grads = [dmod, d_ng1, d_ng2, d_bf[:, :N_HEADS], d_qg, d_kg, d_ps, d_ng3, d_final_g]
    small_w = [b_ada, ffn1_norm_g, mix_norm_g, b_forget, q_norm_g, k_norm_g, pool_scale, ffn2_norm_g, final_norm_g.reshape(1, D)]
    small_m = [m_b_ada, m_ffn1_norm_g, m_mix_norm_g, m_b_forget, m_q_norm_g, m_k_norm_g, m_pool_scale, m_ffn2_norm_g,
               m_final_norm_g.reshape(1, D)]
    small_v = [v_b_ada, v_ffn1_norm_g, v_mix_norm_g, v_b_forget, v_q_norm_g, v_k_norm_g, v_pool_scale, v_ffn2_norm_g,
               v_final_norm_g.reshape(1, D)]
    sizes = [g.shape[1] for g in small_grads]
    n_small = sum(sizes)
    n_pack = -(-n_small // (8 * LANE)) * (8 * LANE)

    def pack(vs, fill):
        flat = jnp.concatenate([v.reshape(1, -1) for v in vs], axis=1)
        return jnp.pad(flat, ((0, 0), (0, n_pack - n_small)), constant_values=fill).reshape(8, n_pack // 8)

    g8 = _allgather_small(pack(small_grads, 0.0), True, "gather_small_grads")
    gs, ds, ms, vs = _adamw_small(pack(small_w, 0.0), g8, pack(small_m, 0.0), pack(small_v, 1.0), "adamw_small")

    def unpack(p):
        flat = p.reshape(1, n_pack)
        out, off = {}, 0
        for nme, sz in zip(small_names, sizes):
            out[nme] = flat[:, off:off + sz]
            off += sz
        return out

    small = [unpack(p) for p in (gs, ds, ms, vs)]
    for dct in small:
        dct["final_norm_g"] = dct["final_norm_g"].reshape(D)

    dmod_all = g8.reshape(8, n_pack)[:, :N_MOD * D]
    dmod_mine = lax.dynamic_slice(dmod_all, (0, chip * n_ada), (8, n_ada))
    grads["w_ada"] = _ada_bwd(act16[:8].reshape(8, D, 1), dmod_mine.reshape(8, 1, n_ada), "ada_bwd")

    big = {"w_ada": (w_ada, m_w_ada, v_w_ada), "ffn1_w_in": (ffn1_w_in, m_ffn1_w_in, v_ffn1_w_in),
           "ffn1_w_out": (ffn1_w_out, m_ffn1_w_out, v_ffn1_w_out), "w_in": (w_in, m_w_in, v_w_in),
           "pool_w": (pool_w, m_pool_w, v_pool_w), "w_out": (w_out, m_w_out, v_w_out),
           "ffn2_w_in": (ffn2_w_in, m_ffn2_w_in, v_ffn2_w_in), "ffn2_w_out": (ffn2_w_out, m_ffn2_w_out, v_ffn2_w_out)}
    res = {}

    for nme, (w, m, v) in big.items():
        shp = w.shape
        g2 = grads[nme]
        if nme in hidden_in_rows:
            rows = g2.shape[0]
            more = -rows % 8

            def fit(a):
                return jnp.pad(a, ((0, more), (0, 0))) if more else a

            outs = _adamw(fit(w[0].T), fit(g2), fit(m[0].T), fit(v[0].T), f"adamw_{nme}")
            res[nme] = tuple(o[:rows].T.reshape(shp) for o in (g2, *outs))
            continue
        two = g2.shape
        d, mo, vo = _adamw(w.reshape(two), g2, m.reshape(two), v.reshape(two), f"adamw_{nme}")
        res[nme] = (g2.reshape(shp), d.reshape(shp), mo.reshape(shp), vo.reshape(shp))
    for nme in small_names:
        res[nme] = tuple(dct[nme] for dct in small)

    order = ["w_ada", "b_ada", "ffn1_norm_g", "ffn1_w_in", "ffn1_w_out", "mix_norm_g", "w_in", "b_forget", "q_norm_g",
             "k_norm_g", "pool_w", "pool_scale", "w_out", "ffn2_norm_g", "ffn2_w_in", "ffn2_w_out", "final_norm_g"]
    return (loss, grad_x, *[res[n][0] for n in order], *[res[n][1] for n in order], *[res[n][2] for n in order],
            *[res[n][3] for n in order])
```

```python
import functools

import jax
import jax.numpy as jnp
from jax import lax
from jax.experimental import pallas as pl
from jax.experimental.pallas import tpu as pltpu

F32 = jnp.float32
BF16 = jnp.bfloat16

D = 2048
N_HEADS = 8
HEAD_DIM = 128
D_ATTN = 1024
D_POOL = 1024
POOL_WINDOWS = (2, 4, 8, 16)
GROUP_DIM = 256
HALO = 16
N_MOD = 9
EPS = 1e-6
D_PROJ = 3 * D_ATTN + D_POOL
D_PROJ_PAD = D_PROJ + 128
LANE = 128
ATTN_BLOCK = 512
POOL_BLOCK = 512
FFN_ROW_CHUNKS = 4
ATTN_HEADS = (2, 2, 2)
ATTN_ROW_CHUNKS = (2, 2, 2)

ADAM_LR = 0.001
ADAM_B1 = 0.9
ADAM_B2 = 0.999
ADAM_EPS = 1e-08
ADAM_WD = 0.01
ADAM_STEP = 10

VMEM_LIMIT_V7X = 56 * 1024 * 1024
MESH_ID = pl.DeviceIdType.MESH
ANY = pl.BlockSpec(memory_space=pl.ANY)
VMEM = pl.BlockSpec(memory_space=pltpu.VMEM)

NT = (((1,), (1,)), ((), ()))
TN = (((0,), (0,)), ((), ()))


class _Side:
    def __init__(self, ins, outs, nsem, start, wait, alias=False, mids=()):
        self.ins, self.outs, self.nsem, self.start, self.wait, self.alias = list(ins), list(outs), nsem, start, wait, alias
        self.mids = list(mids)
        self.results = None


def _pcall(body, *, name, out_shape, grid=None, in_specs=None, out_specs=None, scratch=(), sem=None, prefetch=0, sides=()):
    sides = list(sides)
    single = not isinstance(out_shape, (list, tuple))
    shapes = [out_shape] if single else list(out_shape)
    in_specs = list(in_specs)
    ospecs = [out_specs] if single else list(out_specs)
    scratch = list(scratch)
    n_in, n_out, n_scr = len(in_specs), len(shapes), len(scratch)
    assert not (sides and prefetch)
    aliases = {}
    for sd in sides:
        if sd.alias:
            for k in range(len(sd.outs)):
                aliases[len(in_specs) + k] = len(shapes) + k
        in_specs += [ANY] * len(sd.ins)
        shapes += sd.outs
        ospecs += [ANY] * len(sd.outs)
        scratch += [pltpu.SemaphoreType.DMA((sd.nsem,)), pltpu.SemaphoreType.DMA((sd.nsem,))]

    def wrapped(*refs):
        ins, outs, scr = refs[:len(in_specs)], refs[len(in_specs):len(in_specs) + len(shapes)], refs[len(in_specs) + len(shapes):]
        step, steps = 0, 1
        for ax, g in enumerate(grid or ()):
            step, steps = step * g + pl.program_id(ax), steps * g

        def at(when, fn):
            if grid:
                pl.when(step == when)(fn)
            else:
                fn()

        i0, o0 = n_in, n_out
        for k, sd in enumerate(sides):
            refs_k = (ins[i0:i0 + len(sd.ins)], outs[o0:o0 + len(sd.outs)], scr[n_scr + 2 * k], scr[n_scr + 2 * k + 1])
            at(0, functools.partial(sd.start, *refs_k))
            for frac, fn in sd.mids:
                at(min(steps - 1, int(frac * steps)), functools.partial(fn, *refs_k))
            i0, o0 = i0 + len(sd.ins), o0 + len(sd.outs)
        body(*ins[:n_in], *outs[:n_out], *scr[:n_scr])
        i0, o0 = n_in, n_out
        for k, sd in enumerate(sides):
            refs_k = (ins[i0:i0 + len(sd.ins)], outs[o0:o0 + len(sd.outs)], scr[n_scr + 2 * k], scr[n_scr + 2 * k + 1])
            at(steps - 1, functools.partial(sd.wait, *refs_k))
            i0, o0 = i0 + len(sd.ins), o0 + len(sd.outs)

    params = dict(vmem_limit_bytes=VMEM_LIMIT_V7X)
    if sides and grid:
        params["dimension_semantics"] = ("arbitrary",) * len(grid)
    elif sem is not None:
        params["dimension_semantics"] = sem
    kw = dict(name=name, out_shape=shapes if (sides or not single) else shapes[0], compiler_params=pltpu.CompilerParams(**params))
    if aliases:
        kw["input_output_aliases"] = aliases
    final_ospecs = ospecs if (sides or not single) else ospecs[0]
    if prefetch:
        kw["grid_spec"] = pltpu.PrefetchScalarGridSpec(
            num_scalar_prefetch=prefetch, grid=grid, in_specs=in_specs, out_specs=final_ospecs, scratch_shapes=scratch)
    else:
        if grid is not None:
            kw["grid"] = grid
        kw["in_specs"] = in_specs
        kw["out_specs"] = final_ospecs
        kw["scratch_shapes"] = scratch
    call = pl.pallas_call(wrapped if sides else body, **kw)
    if not sides:
        return call

    def run(*operands):
        res = list(call(*operands, *[a for sd in sides for a in sd.ins]))
        o0 = n_out
        for sd in sides:
            sd.results = res[o0:o0 + len(sd.outs)]
            o0 += len(sd.outs)
        return res[0] if single else res[:n_out]

    return run


def _pick(n, cap, mult):
    best = None
    for d in range(mult, min(n, cap) + 1, mult):
        if n % d == 0:
            best = d
    assert best is not None, (n, cap, mult)
    return best


def _coords():
    return lax.axis_index("x"), lax.axis_index("y"), lax.axis_index("c")


def _dot(a, b, dims=None):
    if dims is None:
        return jnp.dot(a, b, preferred_element_type=F32)
    return lax.dot_general(a, b, dims, preferred_element_type=F32)


def _remote(src, dst, ssem, rsem, dev):
    return pltpu.make_async_remote_copy(src_ref=src, dst_ref=dst, send_sem=ssem, recv_sem=rsem,
                                        device_id=dev, device_id_type=MESH_ID)


def _allgather_small(v, whole_mesh, name):
    masks = list(range(1, 8)) if whole_mesh else [4, 2, 6]
    nslot = 8 if whole_mesh else 4

    def slot(px, py, pc):
        return 4 * px + 2 * py + pc if whole_mesh else 2 * px + py

    def body(v_ref, out_ref, ssem, rsem, lsem):
        x, y, c = _coords()
        mine = slot(x, y, c)
        peers = [(jnp.bitwise_xor(x, (m >> 2) & 1), jnp.bitwise_xor(y, (m >> 1) & 1), jnp.bitwise_xor(c, m & 1))
                 for m in masks]
        loc = pltpu.make_async_copy(v_ref, out_ref.at[mine], lsem)
        loc.start()
        sends = [_remote(v_ref, out_ref.at[mine], ssem.at[k], rsem.at[k], p) for k, p in enumerate(peers)]
        for cp in sends:
            cp.start()
        for k, p in enumerate(peers):
            _remote(v_ref, out_ref.at[slot(*p)], ssem.at[k], rsem.at[k], p).wait_recv()
        for cp in sends:
            cp.wait_send()
        loc.wait()

    return _pcall(body, name=name, out_shape=jax.ShapeDtypeStruct((nslot,) + v.shape, v.dtype),
                  in_specs=[VMEM], out_specs=VMEM,
                  scratch=[pltpu.SemaphoreType.DMA((len(masks),)), pltpu.SemaphoreType.DMA((len(masks),)),
                           pltpu.SemaphoreType.DMA(())])(v)


def _other_chips(x, y):
    return [(1 - x, y), (x, 1 - y), (1 - x, 1 - y)]


def _half_rows(shard_rows, core):
    h = shard_rows // 2
    return pl.ds(pl.multiple_of(core * h, 16), h)


def _later(src, dst, ssem, rsem, dev):
    return functools.partial(_remote, src, dst, ssem, rsem, dev)


def _side_from(ins, outs, nsem, pairs_of, alias=False):
    def start(*refs):
        for send, _ in pairs_of(*refs):
            send().start()

    def wait(*refs):
        pairs = pairs_of(*refs)
        for _, recv in pairs:
            recv().wait_recv()
        for send, _ in pairs:
            send().wait_send()

    return _Side(ins, outs, nsem, start, wait, alias)


GATHER_STAGES = (0.55, 0.8)
LAST_PIECE_FROM = 1536


def _gather(ws):
    def copies(w_refs, g_refs, ssem, rsem):
        x, y, c = _coords()
        me, sib = 2 * x + y, (x, y, 1 - c)
        across_x, across_y, far = 2 * (1 - x) + y, 2 * x + (1 - y), 2 * (1 - x) + (1 - y)
        to_x, to_y = (1 - x, y, c), (x, 1 - y, c)
        out = []
        for a, w in enumerate(ws):
            h = w.shape[0] // 2
            rows, theirs = _half_rows(w.shape[0], c), _half_rows(w.shape[0], 1 - c)
            first = pl.ds(pl.multiple_of(c * h, 16), h // 2)
            second = pl.ds(pl.multiple_of(c * h + h // 2, 16), h // 2)
            g = g_refs[a]

            def pair(k, src, dst, got, dev):
                sems = (ssem.at[8 * a + k], rsem.at[8 * a + k], dev)
                return _later(src, dst, *sems), _later(got, got, *sems)

            out.append(dict(
                x1=pair(0, w_refs[a].at[rows], g.at[me, rows], g.at[across_x, rows], to_x),
                y1=pair(1, w_refs[a].at[rows], g.at[me, rows], g.at[across_y, rows], to_y),
                x2=pair(2, g.at[across_y, first], g.at[across_y, first], g.at[far, first], to_x),
                y2=pair(3, g.at[across_x, second], g.at[across_x, second], g.at[far, second], to_y),
                sx=pair(4, g.at[across_x, rows], g.at[across_x, rows], g.at[across_x, theirs], sib),
                sy=pair(5, g.at[across_y, rows], g.at[across_y, rows], g.at[across_y, theirs], sib),
                sf=pair(6, g.at[far, rows], g.at[far, rows], g.at[far, theirs], sib),
                so=pair(7, w_refs[a], g.at[me], g.at[me], sib)))
        return out

    def start(*refs):
        for cp in copies(*refs):
            cp["x1"][0]().start()
            cp["y1"][0]().start()

    def stage2(*refs):
        for cp in copies(*refs):
            cp["x1"][1]().wait_recv()
            cp["y1"][1]().wait_recv()
            for k in ("x2", "y2", "sx", "sy", "so"):
                cp[k][0]().start()

    def stage3(*refs):
        for cp in copies(*refs):
            cp["x2"][1]().wait_recv()
            cp["y2"][1]().wait_recv()
            cp["sf"][0]().start()

    def wait(*refs):
        cps = copies(*refs)
        for cp in cps:
            for k in ("sx", "sy", "sf", "so"):
                cp[k][1]().wait_recv()
        for cp in cps:
            for send, _ in cp.values():
                send().wait_send()

    return _Side(ws, [jax.ShapeDtypeStruct((4,) + w.shape, w.dtype) for w in ws], 8 * len(ws), start, wait,
                 mids=[(GATHER_STAGES[0], stage2), (GATHER_STAGES[1], stage3)])


def _reduce_siblings(ps):
    def pairs_of(p_refs, r_refs, ssem, rsem):
        x, y, c = _coords()
        out = []
        for a, p in enumerate(ps):
            src = p_refs[a].at[:, _half_rows(p.shape[1], 1 - c), :]
            cp = _later(src, r_refs[a], ssem.at[a], rsem.at[a], (x, y, 1 - c))
            out.append((cp, cp))
        return out

    return _side_from(ps, [jax.ShapeDtypeStruct((4, p.shape[1] // 2, p.shape[2]), p.dtype) for p in ps], len(ps), pairs_of)


def _reduce_chips(hs, dests=(0, 1, 2)):
    nd = len(dests)

    def pairs_of(h_refs, o_refs, ssem, rsem):
        x, y, c = _coords()
        chips = _other_chips(x, y)
        out = []
        for a in range(len(hs)):
            for slot, j in enumerate(dests):
                chip = chips[j]
                cp = _later(h_refs[a].at[2 * chip[0] + chip[1]], o_refs[a].at[slot], ssem.at[nd * a + slot], rsem.at[nd * a + slot],
                            (*chip, c))
                out.append((cp, cp))
        return out

    return _side_from(hs, [jax.ShapeDtypeStruct((nd,) + h.shape[1:], h.dtype) for h in hs], nd * len(hs), pairs_of)


def _share_siblings(gs):
    def pairs_of(_, g_refs, ssem, rsem):
        x, y, c = _coords()
        out = []
        for a, g in enumerate(gs):
            mine, theirs = g_refs[a].at[_half_rows(g.shape[0], c)], g_refs[a].at[_half_rows(g.shape[0], 1 - c)]
            sems = (ssem.at[a], rsem.at[a], (x, y, 1 - c))
            out.append((_later(mine, mine, *sems), _later(theirs, theirs, *sems)))
        return out

    return _side_from(gs, [jax.ShapeDtypeStruct(g.shape, g.dtype) for g in gs], len(gs), pairs_of, alias=True)


def _exchange(side, name):
    def body():
        pass

    _pcall(body, name=name, out_shape=[], in_specs=[], out_specs=[], sides=[side])()
    return side.results


def _add_sibling(p, r, core, name):
    _, rr, cc = p.shape
    h = rr // 2
    th = _pick(h, max(16, (2 << 20) // (2 * cc)), 16)
    nb = h // th

    def body(c_ref, p_ref, r_ref, o_ref):
        o_ref[...] = (p_ref[...].astype(F32) + r_ref[...].astype(F32)).astype(BF16)

    return _pcall(body, name=name, out_shape=jax.ShapeDtypeStruct((4, h, cc), BF16), grid=(4, nb),
                  in_specs=[pl.BlockSpec((None, th, cc), lambda k, i, c_ref: (k, c_ref[0] * nb + i, 0)),
                            pl.BlockSpec((None, th, cc), lambda k, i, c_ref: (k, i, 0))],
                  out_specs=pl.BlockSpec((None, th, cc), lambda k, i, c_ref: (k, i, 0)),
                  sem=("parallel", "parallel"), prefetch=1)(core, p, r)


def _add_chips(hh, pieces, chip_core, name):
    _, h, cc = hh.shape
    th = _pick(h, max(16, (2 << 20) // (2 * cc)), 16)
    nb = h // th

    def body(k_ref, h_ref, r0_ref, r1_ref, r2_ref, o_ref):
        s = h_ref[...].astype(F32) + r0_ref[...].astype(F32)
        s = s + r1_ref[...].astype(F32)
        o_ref[...] = s + r2_ref[...].astype(F32)

    def piece(slot):
        return pl.BlockSpec((None, th, cc), lambda i, k_ref: (slot, i, 0))

    return _pcall(body, name=name, out_shape=jax.ShapeDtypeStruct((2 * h, cc), F32), grid=(nb,),
                  in_specs=[pl.BlockSpec((None, th, cc), lambda i, k_ref: (k_ref[0], i, 0))] + [piece(s) for _, s in pieces],
                  out_specs=pl.BlockSpec((th, cc), lambda i, k_ref: (k_ref[1] * nb + i, 0)),
                  sem=("parallel",), prefetch=1)(chip_core, hh, *[a for a, _ in pieces])


def _adamw_math(w, g, m, v):
    m = ADAM_B1 * m + (1.0 - ADAM_B1) * g
    v = ADAM_B2 * v + (1.0 - ADAM_B2) * (g * g)
    m_hat = m / (1.0 - ADAM_B1 ** ADAM_STEP)
    v_hat = v / (1.0 - ADAM_B2 ** ADAM_STEP)
    delta = -ADAM_LR * (m_hat / (jnp.sqrt(v_hat) + ADAM_EPS) + ADAM_WD * w)
    return delta, m, v


def _adamw(w, g, m, v, name, sides=()):
    rr, cc = w.shape
    tr = _pick(rr, max(8, (3 << 20) // (4 * cc)), 8)

    def body(w_ref, g_ref, m_ref, v_ref, d_ref, mo_ref, vo_ref):
        d, mm, vv = _adamw_math(w_ref[...], g_ref[...], m_ref[...], v_ref[...])
        d_ref[...] = d
        mo_ref[...] = mm
        vo_ref[...] = vv

    spec = pl.BlockSpec((tr, cc), lambda i: (i, 0))
    return _pcall(body, name=name, out_shape=[jax.ShapeDtypeStruct(w.shape, F32)] * 3, grid=(rr // tr,),
                  in_specs=[spec] * 4, out_specs=[spec] * 3, sem=("parallel",), sides=sides)(w, g, m, v)


def _adamw_small(w, g8, m, v, name):
    def body(w_ref, g_ref, m_ref, v_ref, go_ref, d_ref, mo_ref, vo_ref):
        g = g_ref[0]
        for k in range(1, 8):
            g = g + g_ref[k]
        d, mm, vv = _adamw_math(w_ref[...], g, m_ref[...], v_ref[...])
        go_ref[...] = g
        d_ref[...] = d
        mo_ref[...] = mm
        vo_ref[...] = vv

    return _pcall(body, name=name, out_shape=[jax.ShapeDtypeStruct(w.shape, F32)] * 4,
                  in_specs=[VMEM] * 4, out_specs=[VMEM] * 4)(w, g8, m, v)


def _ada_fwd(c16, w_ada, b_ada, name):
    n = w_ada.shape[1]
    tn = _pick(n, 512, LANE)

    def body(c_ref, w_ref, b_ref, act_ref, mod_ref):
        cv = c_ref[...]
        act = cv * jax.nn.sigmoid(cv)
        act_ref[...] = act
        mod_ref[...] = _dot(act.astype(BF16), w_ref[...].astype(BF16)) + b_ref[...]

    return _pcall(body, name=name,
                  out_shape=[jax.ShapeDtypeStruct((16, D), F32), jax.ShapeDtypeStruct((16, n), F32)], grid=(n // tn,),
                  in_specs=[pl.BlockSpec((16, D), lambda j: (0, 0)), pl.BlockSpec((D, tn), lambda j: (0, j)),
                            pl.BlockSpec((1, tn), lambda j: (0, j))],
                  out_specs=[pl.BlockSpec((16, D), lambda j: (0, 0)), pl.BlockSpec((16, tn), lambda j: (0, j))],
                  sem=("arbitrary",))(c16, w_ada, b_ada)


def _ada_bwd(act, dmod, name, sides=()):
    n = dmod.shape[1]
    tm, tn = 512, _pick(n, 1024, LANE)

    def body(a_ref, d_ref, o_ref):
        o_ref[...] = _dot(a_ref[...].astype(BF16), d_ref[...].astype(BF16), TN)

    return _pcall(body, name=name, out_shape=jax.ShapeDtypeStruct((D, n), F32), grid=(D // tm, n // tn),
                  in_specs=[pl.BlockSpec((16, tm), lambda i, j: (0, i)), pl.BlockSpec((16, tn), lambda i, j: (0, j))],
                  out_specs=pl.BlockSpec((tm, tn), lambda i, j: (i, j)), sem=("parallel", "parallel"), sides=sides)(act, dmod)


def _norm_mod(x, g, sh, sc, name, sides=()):
    t = x.shape[0]
    tm = _pick(t, 512, 8)

    def body(x_ref, g_ref, sh_ref, sc_ref, h_ref):
        xf = x_ref[...]
        r = lax.rsqrt(jnp.mean(xf * xf, axis=-1, keepdims=True) + EPS)
        h = (xf * r) * g_ref[...]
        h_ref[...] = (h * (1.0 + sc_ref[...]) + sh_ref[...]).astype(BF16)

    vec = pl.BlockSpec((1, D), lambda i: (0, 0))
    row = pl.BlockSpec((tm, D), lambda i: (i, 0))
    return _pcall(body, name=name, out_shape=jax.ShapeDtypeStruct((t, D), BF16), grid=(t // tm,),
                  in_specs=[row, vec, vec, vec], out_specs=row, sem=("parallel",), sides=sides)(x, g, sh, sc)


def _norm_mod_bwd(dh, x, dxo, g, sc, name, sides=()):
    t = x.shape[0]
    tm = _pick(t, 256, 8)

    def body(dh_ref, x_ref, dxo_ref, g_ref, sc_ref, dx_ref, dsh_ref, dsc_ref, dg_ref):
        @pl.when(pl.program_id(0) == 0)
        def _():
            dsh_ref[...] = jnp.zeros_like(dsh_ref)
            dsc_ref[...] = jnp.zeros_like(dsc_ref)
            dg_ref[...] = jnp.zeros_like(dg_ref)

        xf, dh_ = x_ref[...], dh_ref[...]
        r = lax.rsqrt(jnp.mean(xf * xf, axis=-1, keepdims=True) + EPS)
        xhat = xf * r
        dsh_ref[...] += jnp.sum(dh_, axis=0, keepdims=True)
        dsc_ref[...] += jnp.sum(dh_ * (xhat * g_ref[...]), axis=0, keepdims=True)
        tt = dh_ * (1.0 + sc_ref[...])
        dg_ref[...] += jnp.sum(tt * xhat, axis=0, keepdims=True)
        dxh = tt * g_ref[...]
        dx_ref[...] = r * (dxh - xhat * jnp.mean(dxh * xhat, axis=-1, keepdims=True)) + dxo_ref[...]

    vec = pl.BlockSpec((1, D), lambda i: (0, 0))
    row = pl.BlockSpec((tm, D), lambda i: (i, 0))
    vshape = jax.ShapeDtypeStruct((1, D), F32)
    return _pcall(body, name=name, out_shape=[jax.ShapeDtypeStruct((t, D), F32), vshape, vshape, vshape], grid=(t // tm,),
                  in_specs=[row, row, row, vec, vec], out_specs=[row, vec, vec, vec], sem=("arbitrary",), sides=sides)(dh, x, dxo, g, sc)


def _gate_bwd(dxo, yy, gate, name):
    t = dxo.shape[0]
    tm = _pick(t, 512, 8)

    def body(dx_ref, y_ref, g_ref, dy_ref, dg_ref):
        @pl.when(pl.program_id(0) == 0)
        def _():
            dg_ref[...] = jnp.zeros_like(dg_ref)

        dx = dx_ref[...]
        dy_ref[...] = (dx * g_ref[...]).astype(BF16)
        dg_ref[...] += jnp.sum(dx * y_ref[...], axis=0, keepdims=True)

    vec = pl.BlockSpec((1, D), lambda i: (0, 0))
    row = pl.BlockSpec((tm, D), lambda i: (i, 0))
    return _pcall(body, name=name, out_shape=[jax.ShapeDtypeStruct((t, D), BF16), jax.ShapeDtypeStruct((1, D), F32)],
                  grid=(t // tm,), in_specs=[row, row, vec], out_specs=[row, vec], sem=("arbitrary",))(dxo, yy, gate)


def _final_loss(x, g, tgt, name):
    t = x.shape[0]
    tm = _pick(t, 256, 8)

    def body(x_ref, g_ref, t_ref, loss_ref, dx_ref, dg_ref):
        @pl.when(pl.program_id(0) == 0)
        def _():
            loss_ref[...] = jnp.zeros_like(loss_ref)
            dg_ref[...] = jnp.zeros_like(dg_ref)

        xf = x_ref[...]
        r = lax.rsqrt(jnp.mean(xf * xf, axis=-1, keepdims=True) + EPS)
        xhat = xf * r
        e = xhat * g_ref[...] - t_ref[...]
        per_tok = jnp.mean(e * e, axis=-1, keepdims=True)
        loss_ref[...] += 0.5 * jnp.sum(per_tok, axis=0, keepdims=True)
        dy = e * (1.0 / D)
        dg_ref[...] += jnp.sum(dy * xhat, axis=0, keepdims=True)
        dxh = dy * g_ref[...]
        dx_ref[...] = r * (dxh - xhat * jnp.mean(dxh * xhat, axis=-1, keepdims=True))

    vec = pl.BlockSpec((1, D), lambda i: (0, 0))
    row = pl.BlockSpec((tm, D), lambda i: (i, 0))
    return _pcall(body, name=name,
                  out_shape=[jax.ShapeDtypeStruct((1, LANE), F32), jax.ShapeDtypeStruct((t, D), F32),
                             jax.ShapeDtypeStruct((1, D), F32)],
                  grid=(t // tm,), in_specs=[row, vec, row],
                  out_specs=[pl.BlockSpec((1, LANE), lambda i: (0, 0)), row, vec], sem=("arbitrary",))(x, g, tgt)


def _mm(a, b, out_dtype, name):
    m, k = a.shape
    n = b.shape[1]
    tm = _pick(m, 1024, 8)
    tn = _pick(n, 1408, LANE)

    def body(a_ref, b_ref, o_ref):
        o_ref[...] = _dot(a_ref[...], b_ref[...]).astype(out_dtype)

    return _pcall(body, name=name, out_shape=jax.ShapeDtypeStruct((m, n), out_dtype), grid=(m // tm, n // tn),
                  in_specs=[pl.BlockSpec((tm, k), lambda i, j: (i, 0)), pl.BlockSpec((k, tn), lambda i, j: (0, j))],
                  out_specs=pl.BlockSpec((tm, tn), lambda i, j: (i, j)), sem=("parallel", "parallel"))(a, b)


def _mm_resid(a, b, resid, gate, name, sides=()):
    m, k = a.shape
    n = b.shape[1]
    tm, tn, tk = _pick(m, 1024, 8), _pick(n, 1024, LANE), _pick(k, 1408, LANE)
    nk = k // tk

    def body(a_ref, b_ref, r_ref, g_ref, y_ref, o_ref, acc):
        kk = pl.program_id(2)

        @pl.when(kk == 0)
        def _():
            acc[...] = jnp.zeros_like(acc)

        acc[...] += _dot(a_ref[...], b_ref[...])

        @pl.when(kk == nk - 1)
        def _():
            y_ref[...] = acc[...].astype(BF16)
            o_ref[...] = r_ref[...] + g_ref[...] * acc[...]

    blk = pl.BlockSpec((tm, tn), lambda i, j, kk: (i, j))
    return _pcall(body, name=name, out_shape=[jax.ShapeDtypeStruct((m, n), BF16), jax.ShapeDtypeStruct((m, n), F32)],
                  grid=(m // tm, n // tn, nk),
                  in_specs=[pl.BlockSpec((tm, tk), lambda i, j, kk: (i, kk)), pl.BlockSpec((tk, tn), lambda i, j, kk: (kk, j)),
                            blk, pl.BlockSpec((1, tn), lambda i, j, kk: (0, j))],
                  out_specs=[blk, blk], scratch=[pltpu.VMEM((tm, tn), F32)],
                  sem=("parallel", "parallel", "arbitrary"), sides=sides)(a, b, resid, gate)


def _mm_nt(a, b, out_dtype, name, sides=()):
    m, k = a.shape
    n = b.shape[0]
    tm, tn, tk = _pick(m, 1024, 8), _pick(n, 1024, LANE), _pick(k, 2816, LANE)
    nk = k // tk

    def body(a_ref, b_ref, o_ref, acc):
        kk = pl.program_id(2)

        @pl.when(kk == 0)
        def _():
            acc[...] = jnp.zeros_like(acc)

        acc[...] += _dot(a_ref[...], b_ref[...], NT)

        @pl.when(kk == nk - 1)
        def _():
            o_ref[...] = acc[...].astype(out_dtype)

    return _pcall(body, name=name, out_shape=jax.ShapeDtypeStruct((m, n), out_dtype), grid=(m // tm, n // tn, nk),
                  in_specs=[pl.BlockSpec((tm, tk), lambda i, j, kk: (i, kk)), pl.BlockSpec((tn, tk), lambda i, j, kk: (j, kk))],
                  out_specs=pl.BlockSpec((tm, tn), lambda i, j, kk: (i, j)), scratch=[pltpu.VMEM((tm, tn), F32)],
                  sem=("parallel", "parallel", "arbitrary"), sides=sides)(a, b)


def _mm_tn(a, b, out_dtype, tm_cap, name, sides=()):
    t, m = a.shape
    n = b.shape[1]
    tm, tn, tk = _pick(m, tm_cap, LANE), _pick(n, 1408, LANE), _pick(t, 2048, 16)
    nk = t // tk

    def body(a_ref, b_ref, o_ref, acc):
        kk = pl.program_id(2)

        @pl.when(kk == 0)
        def _():
            acc[...] = jnp.zeros_like(acc)

        acc[...] += _dot(a_ref[...], b_ref[...], TN)

        @pl.when(kk == nk - 1)
        def _():
            o_ref[...] = acc[...].astype(out_dtype)

    return _pcall(body, name=name, out_shape=jax.ShapeDtypeStruct((m, n), out_dtype), grid=(m // tm, n // tn, nk),
                  in_specs=[pl.BlockSpec((tk, tm), lambda i, j, kk: (kk, i)), pl.BlockSpec((tk, tn), lambda i, j, kk: (kk, j))],
                  out_specs=pl.BlockSpec((tm, tn), lambda i, j, kk: (i, j)), scratch=[pltpu.VMEM((tm, tn), F32)],
                  sem=("parallel", "parallel", "arbitrary"), sides=sides)(a, b)


def _ffn_up(h, wg, name, sides=()):
    t = h.shape[0]
    fp = wg.shape[2]
    tm, tn = _pick(t, 2048, 8), 256
    nn = fp // tn

    def body(h_ref, wa_ref, wb_ref, a_ref, b_ref, u_ref):
        for r0 in range(0, tm, tm // FFN_ROW_CHUNKS):
            rows = slice(r0, r0 + tm // FFN_ROW_CHUNKS)
            hh = h_ref[rows, :]
            a = _dot(hh, wa_ref[...])
            b = _dot(hh, wb_ref[...])
            a_ref[rows, :] = a.astype(BF16)
            b_ref[rows, :] = b.astype(BF16)
            u_ref[rows, :] = (a * jax.nn.sigmoid(a) * b).astype(BF16)

    out = pl.BlockSpec((tm, tn), lambda i, j, n: (i, j * nn + n))
    return _pcall(body, name=name, out_shape=[jax.ShapeDtypeStruct((t, 2 * fp), BF16)] * 3, grid=(t // tm, 2, nn),
                  in_specs=[pl.BlockSpec((tm, D), lambda i, j, n: (i, 0)),
                            pl.BlockSpec((None, D, tn), lambda i, j, n: (j, 0, n)),
                            pl.BlockSpec((None, D, tn), lambda i, j, n: (j + 2, 0, n))],
                  out_specs=[out, out, out], sem=("parallel", "parallel", "parallel"), sides=sides)(h, wg, wg)


def _ffn_dab(dy, wo, a, b, name, sides=()):
    t = dy.shape[0]
    f2 = wo.shape[0]
    tm, tn = _pick(t, 2048, 8), 256

    def body(dy_ref, w_ref, a_ref, b_ref, o_ref):
        for r0 in range(0, tm, tm // FFN_ROW_CHUNKS):
            rows = slice(r0, r0 + tm // FFN_ROW_CHUNKS)
            du = _dot(dy_ref[rows, :], w_ref[...], NT)
            av, bv = a_ref[rows, :].astype(F32), b_ref[rows, :].astype(F32)
            sg = jax.nn.sigmoid(av)
            o_ref[0, rows, :] = (du * bv * (sg * (1.0 + av * (1.0 - sg)))).astype(BF16)
            o_ref[1, rows, :] = (du * (av * sg)).astype(BF16)

    blk = pl.BlockSpec((tm, tn), lambda i, n: (i, n))
    return _pcall(body, name=name, out_shape=jax.ShapeDtypeStruct((2, t, f2), BF16), grid=(t // tm, f2 // tn),
                  in_specs=[pl.BlockSpec((tm, D), lambda i, n: (i, 0)), pl.BlockSpec((tn, D), lambda i, n: (n, 0)), blk, blk],
                  out_specs=pl.BlockSpec((2, tm, tn), lambda i, n: (0, i, n)), sem=("parallel", "parallel"), sides=sides)(dy, wo, a, b)


def _ffn_dwin(h, dab, cols, name, sides=()):
    t = h.shape[0]
    fp = dab.shape[2] // 2
    tn = 1024 if (cols[0] % 1024 == 0 and cols[1] % 1024 == 0) else 512
    tm, tk = _pick(fp, 1408, LANE), _pick(t, 2048, 16)
    nm, nk = fp // tm, t // tk
    n0, n1 = cols[0] // tn, cols[1] // tn

    def body(d_ref, h_ref, o_ref, acc):
        kk = pl.program_id(3)

        @pl.when(kk == 0)
        def _():
            acc[...] = jnp.zeros_like(acc)

        acc[...] += _dot(d_ref[...], h_ref[...], TN)

        @pl.when(kk == nk - 1)
        def _():
            o_ref[...] = acc[...].astype(BF16)

    return _pcall(body, name=name, out_shape=jax.ShapeDtypeStruct((4, fp, (n1 - n0) * tn), BF16), grid=(4, nm, n1 - n0, nk),
                  in_specs=[pl.BlockSpec((None, tk, tm), lambda u, i, n, kk: (u // 2, kk, (u % 2) * nm + i)),
                            pl.BlockSpec((tk, tn), lambda u, i, n, kk: (kk, n0 + n))],
                  out_specs=pl.BlockSpec((None, tm, tn), lambda u, i, n, kk: (u, i, n)),
                  scratch=[pltpu.VMEM((tm, tn), F32)],
                  sem=("parallel", "parallel", "parallel", "arbitrary"), sides=sides)(dab, h)


def _ffn_dh(dab, wg, name, sides=()):
    t = dab.shape[1]
    fp = wg.shape[2]
    tm, tn, tk = _pick(t, 1024, 8), 1024, _pick(fp, 2816, LANE)
    nkk = fp // tk

    def body(d_ref, w_ref, o_ref, acc):
        u, kk = pl.program_id(2), pl.program_id(3)

        @pl.when((u == 0) & (kk == 0))
        def _():
            acc[...] = jnp.zeros_like(acc)

        acc[...] += _dot(d_ref[...], w_ref[...], NT)

        @pl.when((u == 3) & (kk == nkk - 1))
        def _():
            o_ref[...] = acc[...]

    return _pcall(body, name=name, out_shape=jax.ShapeDtypeStruct((t, D), F32), grid=(t // tm, D // tn, 4, nkk),
                  in_specs=[pl.BlockSpec((None, tm, tk), lambda i, j, u, kk: (u // 2, i, (u % 2) * nkk + kk)),
                            pl.BlockSpec((None, tn, tk), lambda i, j, u, kk: (u, j, kk))],
                  out_specs=pl.BlockSpec((tm, tn), lambda i, j, u, kk: (i, j)), scratch=[pltpu.VMEM((tm, tn), F32)],
                  sem=("parallel", "parallel", "arbitrary", "arbitrary"), sides=sides)(dab, wg)


def _split3(v):
    hi = v.astype(BF16)
    r1 = v - hi.astype(F32)
    mid = r1.astype(BF16)
    lo = (r1 - mid.astype(F32)).astype(BF16)
    return hi, mid, lo


def _tri_sum(tri, v):
    hi, mid, lo = _split3(v)
    return (_dot(tri, hi) + _dot(tri, mid)) + _dot(tri, lo)


def _forget_fwd(proj, b_pad, name):
    t = proj.shape[0]
    tb = _pick(t, 256, 8)
    col = D_PROJ // LANE

    def body(f_ref, b_ref, o_ref, carry):
        @pl.when(pl.program_id(0) == 0)
        def _():
            carry[...] = jnp.zeros_like(carry)

        z = f_ref[...] + b_ref[...]
        lf = jnp.minimum(z, 0.0) - jnp.log(1.0 + jnp.exp(-jnp.abs(z)))
        r = lax.broadcasted_iota(jnp.int32, (tb, tb), 0)
        cidx = lax.broadcasted_iota(jnp.int32, (tb, tb), 1)
        tri = (r >= cidx).astype(BF16)
        o_ref[...] = _tri_sum(tri, lf) + carry[...]
        carry[...] += jnp.sum(lf, axis=0, keepdims=True)

    return _pcall(body, name=name, out_shape=jax.ShapeDtypeStruct((t, LANE), F32), grid=(t // tb,),
                  in_specs=[pl.BlockSpec((tb, LANE), lambda i: (i, col)), pl.BlockSpec((1, LANE), lambda i: (0, 0))],
                  out_specs=pl.BlockSpec((tb, LANE), lambda i: (i, 0)), scratch=[pltpu.VMEM((1, LANE), F32)],
                  sem=("arbitrary",))(proj, b_pad)


def _forget_bwd(d_cum, proj, b_pad, name):
    t = proj.shape[0]
    tb = _pick(t, 256, 8)
    nb = t // tb
    col = D_PROJ // LANE

    def body(d_ref, f_ref, b_ref, o_ref, db_ref, carry):
        @pl.when(pl.program_id(0) == 0)
        def _():
            carry[...] = jnp.zeros_like(carry)
            db_ref[...] = jnp.zeros_like(db_ref)

        dc = d_ref[...]
        r = lax.broadcasted_iota(jnp.int32, (tb, tb), 0)
        cidx = lax.broadcasted_iota(jnp.int32, (tb, tb), 1)
        tri = (r <= cidx).astype(BF16)
        dlf = _tri_sum(tri, dc) + carry[...]
        carry[...] += jnp.sum(dc, axis=0, keepdims=True)
        z = f_ref[...] + b_ref[...]
        lane = lax.broadcasted_iota(jnp.int32, (tb, LANE), 1)
        dz = jnp.where(lane < N_HEADS, dlf * jax.nn.sigmoid(-z), 0.0)
        o_ref[...] = dz.astype(BF16)
        db_ref[...] += jnp.sum(dz, axis=0, keepdims=True)

    return _pcall(body, name=name, out_shape=[jax.ShapeDtypeStruct((t, LANE), BF16), jax.ShapeDtypeStruct((1, LANE), F32)],
                  grid=(nb,),
                  in_specs=[pl.BlockSpec((tb, LANE), lambda i: (nb - 1 - i, 0)),
                            pl.BlockSpec((tb, LANE), lambda i: (nb - 1 - i, col)),
                            pl.BlockSpec((1, LANE), lambda i: (0, 0))],
                  out_specs=[pl.BlockSpec((tb, LANE), lambda i: (nb - 1 - i, 0)), pl.BlockSpec((1, LANE), lambda i: (0, 0))],
                  scratch=[pltpu.VMEM((1, LANE), F32)], sem=("arbitrary",))(d_cum, proj, b_pad)


def _head_norm(v, g):
    r = lax.rsqrt(jnp.mean(v * v, axis=-1, keepdims=True) + EPS)
    return v * r, r


def _qkv_prep(proj, qg, kg, name):
    t = proj.shape[0]
    tm = _pick(t, 1024, 8)

    def body(q_ref, k_ref, v_ref, qg_ref, kg_ref, qo_ref, ko_ref, vo_ref):
        qo_ref[...] = (_head_norm(q_ref[...], None)[0] * qg_ref[...] * ATTN_SCALE).astype(BF16)
        ko_ref[...] = (_head_norm(k_ref[...], None)[0] * kg_ref[...]).astype(BF16)
        vo_ref[...] = v_ref[...].astype(BF16)

    def blk(off):
        return pl.BlockSpec((tm, HEAD_DIM), lambda i, h: (i, off + h))

    vec = pl.BlockSpec((1, HEAD_DIM), lambda i, h: (0, 0))
    return _pcall(body, name=name, out_shape=[jax.ShapeDtypeStruct((t, D_ATTN), BF16)] * 3, grid=(t // tm, N_HEADS),
                  in_specs=[blk(0), blk(N_HEADS), blk(2 * N_HEADS), vec, vec], out_specs=[blk(0)] * 3,
                  sem=("parallel", "parallel"))(proj, proj, proj, qg, kg)


def _qk_norm_bwd(dqn, dkn, proj, qg, kg, name):
    t = proj.shape[0]
    tm = _pick(t, 1024, 8)

    def one(d_ref, v_ref, g_ref, o_ref, dg_ref):
        xhat, r = _head_norm(v_ref[...], None)
        d = d_ref[...]
        dg_ref[...] += jnp.sum(d * xhat, axis=0, keepdims=True)
        dxh = d * g_ref[...]
        o_ref[...] = (r * (dxh - xhat * jnp.mean(dxh * xhat, axis=-1, keepdims=True))).astype(BF16)

    def body(dq_ref, dk_ref, q_ref, k_ref, qg_ref, kg_ref, qo_ref, ko_ref, dqg_ref, dkg_ref):
        @pl.when((pl.program_id(0) == 0) & (pl.program_id(1) == 0))
        def _():
            dqg_ref[...] = jnp.zeros_like(dqg_ref)
            dkg_ref[...] = jnp.zeros_like(dkg_ref)

        one(dq_ref, q_ref, qg_ref, qo_ref, dqg_ref)
        one(dk_ref, k_ref, kg_ref, ko_ref, dkg_ref)

    def blk(off):
        return pl.BlockSpec((tm, HEAD_DIM), lambda i, h: (i, off + h))

    vec = pl.BlockSpec((1, HEAD_DIM), lambda i, h: (0, 0))
    vshape = jax.ShapeDtypeStruct((1, HEAD_DIM), F32)
    return _pcall(body, name=name, out_shape=[jax.ShapeDtypeStruct((t, D_ATTN), BF16)] * 2 + [vshape, vshape],
                  grid=(t // tm, N_HEADS),
                  in_specs=[blk(0), blk(0), blk(0), blk(N_HEADS), vec, vec], out_specs=[blk(0), blk(0), vec, vec],
                  sem=("arbitrary", "arbitrary"))(dqn, dkn, proj, proj, qg, kg)


ATTN_SCALE = HEAD_DIM ** -0.5


def _logits(q, k, fq, fk, diag, r0, tq, tk):
    s = _dot(q, k, NT) + (fq - fk)
    if diag:
        r = r0 + lax.broadcasted_iota(jnp.int32, (tq, tk), 0)
        cidx = lax.broadcasted_iota(jnp.int32, (tq, tk), 1)
        s = jnp.where(r >= cidx, s, -jnp.inf)
    return s


def _head_cols(hp):
    return [(hh, slice(hh * HEAD_DIM, (hh + 1) * HEAD_DIM)) for hh in range(hp)]


def _tri(tt, n, by_row):
    if by_row:
        i = sum((tt >= k * (k + 1) // 2).astype(jnp.int32) for k in range(1, n))
        return i, tt - i * (i + 1) // 2
    j = sum((tt >= k * n - k * (k - 1) // 2).astype(jnp.int32) for k in range(1, n))
    return j + tt - (j * n - j * (j - 1) // 2), j


def _attn_fwd(q, k, v, fq, fk, name, sides=()):
    t = q.shape[0]
    tq = tk = _pick(t, ATTN_BLOCK, LANE)
    nk = t // tk

    hp, rc = ATTN_HEADS[0], tq // ATTN_ROW_CHUNKS[0]

    def body(q_ref, k_ref, v_ref, fq_ref, fk_ref, o_ref, lse_ref, m_s, l_s, acc):
        i, j = _tri(pl.program_id(1), nk, True)

        @pl.when(j == 0)
        def _():
            m_s[...] = jnp.full_like(m_s, -jnp.inf)
            l_s[...] = jnp.zeros_like(l_s)
            acc[...] = jnp.zeros_like(acc)

        def step(diag):
            for hh, cols in _head_cols(hp):
                for r0 in range(0, tq, rc):
                    rows = slice(r0, r0 + rc)
                    s = _logits(q_ref[rows, cols], k_ref[:, cols], fq_ref[hh, rows, :], fk_ref[hh], diag, r0, rc, tk)
                    m_new = jnp.maximum(m_s[hh, rows, :], jnp.max(s, axis=-1, keepdims=True))
                    alpha = jnp.exp(m_s[hh, rows, :] - m_new)
                    p = jnp.exp(s - m_new)
                    l_s[hh, rows, :] = alpha * l_s[hh, rows, :] + jnp.sum(p, axis=-1, keepdims=True)
                    acc[rows, cols] = alpha * acc[rows, cols] + _dot(p.astype(BF16), v_ref[:, cols])
                    m_s[hh, rows, :] = m_new

        @pl.when(j < i)
        def _():
            step(False)

        @pl.when(j == i)
        def _():
            step(True)

        @pl.when(j == i)
        def _():
            for hh, cols in _head_cols(hp):
                o_ref[:, cols] = (acc[:, cols] / l_s[hh]).astype(BF16)
                lse_ref[hh] = m_s[hh] + jnp.log(l_s[hh])

    qb = pl.BlockSpec((tq, hp * HEAD_DIM), lambda h, tt: (_tri(tt, nk, True)[0], h))
    kb = pl.BlockSpec((tk, hp * HEAD_DIM), lambda h, tt: (_tri(tt, nk, True)[1], h))
    col = pl.BlockSpec((hp, tq, 1), lambda h, tt: (h, _tri(tt, nk, True)[0], 0))
    return _pcall(body, name=name,
                  out_shape=[jax.ShapeDtypeStruct((t, D_ATTN), BF16), jax.ShapeDtypeStruct((N_HEADS, t, 1), F32)],
                  grid=(N_HEADS // hp, nk * (nk + 1) // 2),
                  in_specs=[qb, kb, kb, col, pl.BlockSpec((hp, 1, tk), lambda h, tt: (h, 0, _tri(tt, nk, True)[1]))],
                  out_specs=[qb, col],
                  scratch=[pltpu.VMEM((hp, tq, 1), F32), pltpu.VMEM((hp, tq, 1), F32), pltpu.VMEM((tq, hp * HEAD_DIM), F32)],
                  sem=("parallel", "arbitrary"), sides=sides)(q, k, v, fq, fk)


def _attn_bwd_q(q, k, v, o, do, lse, fq, fk, name, sides=()):
    t = q.shape[0]
    tq = tk = _pick(t, ATTN_BLOCK, LANE)
    nk = t // tk

    hp, rc = ATTN_HEADS[1], tq // ATTN_ROW_CHUNKS[1]

    def body(q_ref, k_ref, v_ref, o_ref, do_ref, lse_ref, fq_ref, fk_ref, dq_ref, dl_ref, dfq_ref, acc, dl_s, df_s):
        i, j = _tri(pl.program_id(1), nk, True)

        @pl.when(j == 0)
        def _():
            acc[...] = jnp.zeros_like(acc)
            df_s[...] = jnp.zeros_like(df_s)
            for hh, cols in _head_cols(hp):
                dl_s[hh] = jnp.sum(do_ref[:, cols].astype(F32) * o_ref[:, cols].astype(F32), axis=-1, keepdims=True)

        def step(diag):
            for hh, cols in _head_cols(hp):
                for r0 in range(0, tq, rc):
                    rows = slice(r0, r0 + rc)
                    s = _logits(q_ref[rows, cols], k_ref[:, cols], fq_ref[hh, rows, :], fk_ref[hh], diag, r0, rc, tk)
                    p = jnp.exp(s - lse_ref[hh, rows, :])
                    dp = _dot(do_ref[rows, cols], v_ref[:, cols], NT)
                    ds = p * (dp - dl_s[hh, rows, :])
                    df_s[hh, rows, :] += jnp.sum(ds, axis=-1, keepdims=True)
                    acc[rows, cols] += _dot(ds.astype(BF16), k_ref[:, cols])

        @pl.when(j < i)
        def _():
            step(False)

        @pl.when(j == i)
        def _():
            step(True)

        @pl.when(j == i)
        def _():
            dq_ref[...] = acc[...] * ATTN_SCALE
            dl_ref[...] = dl_s[...]
            dfq_ref[...] = df_s[...]

    qb = pl.BlockSpec((tq, hp * HEAD_DIM), lambda h, tt: (_tri(tt, nk, True)[0], h))
    kb = pl.BlockSpec((tk, hp * HEAD_DIM), lambda h, tt: (_tri(tt, nk, True)[1], h))
    col = pl.BlockSpec((hp, tq, 1), lambda h, tt: (h, _tri(tt, nk, True)[0], 0))
    cshape = jax.ShapeDtypeStruct((N_HEADS, t, 1), F32)
    return _pcall(body, name=name, out_shape=[jax.ShapeDtypeStruct((t, D_ATTN), F32), cshape, cshape],
                  grid=(N_HEADS // hp, nk * (nk + 1) // 2),
                  in_specs=[qb, kb, kb, qb, qb, col, col, pl.BlockSpec((hp, 1, tk), lambda h, tt: (h, 0, _tri(tt, nk, True)[1]))],
                  out_specs=[qb, col, col],
                  scratch=[pltpu.VMEM((tq, hp * HEAD_DIM), F32), pltpu.VMEM((hp, tq, 1), F32), pltpu.VMEM((hp, tq, 1), F32)],
                  sem=("parallel", "arbitrary"), sides=sides)(q, k, v, o, do, lse, fq, fk)


def _attn_bwd_kv(q, k, v, do, lse, delta, fq, fk, name, sides=()):
    t = q.shape[0]
    tq = tk = _pick(t, ATTN_BLOCK, LANE)
    nq = t // tq

    hp, rc = ATTN_HEADS[2], tq // ATTN_ROW_CHUNKS[2]

    def body(q_ref, k_ref, v_ref, do_ref, lse_ref, dl_ref, fq_ref, fk_ref, dk_ref, dv_ref, dfk_ref, dk_s, dv_s, df_s):
        i, j = _tri(pl.program_id(1), nq, False)

        @pl.when(i == j)
        def _():
            dk_s[...] = jnp.zeros_like(dk_s)
            dv_s[...] = jnp.zeros_like(dv_s)
            df_s[...] = jnp.zeros_like(df_s)

        def step(diag):
            for hh, cols in _head_cols(hp):
                for r0 in range(0, tq, rc):
                    rows = slice(r0, r0 + rc)
                    s = _logits(q_ref[rows, cols], k_ref[:, cols], fq_ref[hh, rows, :], fk_ref[hh], diag, r0, rc, tk)
                    p = jnp.exp(s - lse_ref[hh, rows, :])
                    dv_s[:, cols] += _dot(p.astype(BF16), do_ref[rows, cols], TN)
                    dp = _dot(do_ref[rows, cols], v_ref[:, cols], NT)
                    ds = p * (dp - dl_ref[hh, rows, :])
                    df_s[hh] -= jnp.sum(ds, axis=0, keepdims=True)
                    dk_s[:, cols] += _dot(ds.astype(BF16), q_ref[rows, cols], TN)

        @pl.when(i > j)
        def _():
            step(False)

        @pl.when(i == j)
        def _():
            step(True)

        @pl.when(i == nq - 1)
        def _():
            dk_ref[...] = dk_s[...]
            dv_ref[...] = dv_s[...].astype(BF16)
            dfk_ref[...] = df_s[...]

    qb = pl.BlockSpec((tq, hp * HEAD_DIM), lambda h, tt: (_tri(tt, nq, False)[0], h))
    kb = pl.BlockSpec((tk, hp * HEAD_DIM), lambda h, tt: (_tri(tt, nq, False)[1], h))
    col = pl.BlockSpec((hp, tq, 1), lambda h, tt: (h, _tri(tt, nq, False)[0], 0))
    row = pl.BlockSpec((hp, 1, tk), lambda h, tt: (h, 0, _tri(tt, nq, False)[1]))
    return _pcall(body, name=name,
                  out_shape=[jax.ShapeDtypeStruct((t, D_ATTN), F32), jax.ShapeDtypeStruct((t, D_ATTN), BF16),
                             jax.ShapeDtypeStruct((N_HEADS, 1, t), F32)],
                  grid=(N_HEADS // hp, nq * (nq + 1) // 2),
                  in_specs=[qb, kb, kb, qb, col, col, col, row], out_specs=[kb, kb, row],
                  scratch=[pltpu.VMEM((tk, hp * HEAD_DIM), F32), pltpu.VMEM((tk, hp * HEAD_DIM), F32), pltpu.VMEM((hp, 1, tk), F32)],
                  sem=("parallel", "arbitrary"), sides=sides)(q, k, v, do, lse, delta, fq, fk)


def _window_sum(v, w, back):
    n = v.shape[0]
    k = 1
    while k < w:
        v = v + pltpu.roll(v, k if back else n - k, axis=0)
        k *= 2
    return v


def _pool_fwd(proj, pw, ps, name):
    t = proj.shape[0]
    tm = _pick(t, POOL_BLOCK, HALO)
    col = 3 * D_ATTN // D_POOL

    def body(u_ref, prev_ref, pw_ref, ps_ref, pooled_ref, out_ref):
        i = pl.program_id(0)
        prev = jnp.where(i > 0, prev_ref[...], 0.0)
        ext = jnp.concatenate([prev, u_ref[...]], axis=0)
        pos = i * tm + lax.broadcasted_iota(jnp.int32, (tm, 1), 0)
        for g, w in enumerate(POOL_WINDOWS):
            cols = slice(g * GROUP_DIM, (g + 1) * GROUP_DIM)
            xg = ext[:, cols]
            sw = _window_sum(xg, w, True)[HALO:, :]
            cnt = jnp.minimum(pos + 1, w).astype(F32)
            pooled = (sw / cnt - xg[HALO:, :]).astype(BF16)
            pooled_ref[:, cols] = pooled
            out_ref[:, cols] = (_dot(pooled, pw_ref[g]) * ps_ref[:, cols]).astype(BF16)

    row = pl.BlockSpec((tm, D_POOL), lambda i: (i, 0))
    return _pcall(body, name=name, out_shape=[jax.ShapeDtypeStruct((t, D_POOL), BF16)] * 2, grid=(t // tm,),
                  in_specs=[pl.BlockSpec((tm, D_POOL), lambda i: (i, col)),
                            pl.BlockSpec((HALO, D_POOL), lambda i: (jnp.maximum(i * (tm // HALO) - 1, 0), col)),
                            pl.BlockSpec((len(POOL_WINDOWS), GROUP_DIM, GROUP_DIM), lambda i: (0, 0, 0)),
                            pl.BlockSpec((1, D_POOL), lambda i: (0, 0))],
                  out_specs=[row, row], sem=("parallel",))(proj, proj, pw, ps)


def _pool_bwd(dout, pooled, pw, ps, name):
    t = pooled.shape[0]
    tm = _pick(t, POOL_BLOCK, HALO)
    nb = t // tm
    ng = len(POOL_WINDOWS)

    def body(d_ref, nxt_ref, p_ref, pw_ref, ps_ref, du_ref, dpw_ref, dps_ref):
        i = pl.program_id(0)

        @pl.when(i == 0)
        def _():
            dpw_ref[...] = jnp.zeros_like(dpw_ref)
            dps_ref[...] = jnp.zeros_like(dps_ref)

        nxt = jnp.where(i < nb - 1, nxt_ref[...].astype(F32), 0.0)
        ext = jnp.concatenate([d_ref[...].astype(F32), nxt], axis=0)
        pos = i * tm + lax.broadcasted_iota(jnp.int32, (tm + HALO, 1), 0)
        for g, w in enumerate(POOL_WINDOWS):
            cols = slice(g * GROUP_DIM, (g + 1) * GROUP_DIM)
            pooled_g = p_ref[:, cols]
            dg = ext[:, cols]
            pm = _dot(pooled_g, pw_ref[g])
            dps_ref[:, cols] += jnp.sum(dg[:tm, :] * pm, axis=0, keepdims=True)
            dpm = (dg * ps_ref[:, cols]).astype(BF16)
            dpw_ref[g] += _dot(pooled_g, dpm[:tm, :], TN)
            dpooled = _dot(dpm, pw_ref[g], NT)
            cnt = jnp.minimum(pos + 1, w).astype(F32)
            fwd = _window_sum(dpooled / cnt, w, False)
            du_ref[:, cols] = (fwd[:tm, :] - dpooled[:tm, :]).astype(BF16)

    row = pl.BlockSpec((tm, D_POOL), lambda i: (i, 0))
    return _pcall(body, name=name,
                  out_shape=[jax.ShapeDtypeStruct((t, D_POOL), BF16), jax.ShapeDtypeStruct((ng, GROUP_DIM, GROUP_DIM), F32),
                             jax.ShapeDtypeStruct((1, D_POOL), F32)],
                  grid=(nb,),
                  in_specs=[pl.BlockSpec((tm, D_POOL), lambda i: (i, 1)),
                            pl.BlockSpec((HALO, D_POOL), lambda i: (jnp.minimum((i + 1) * (tm // HALO), t // HALO - 1), 1)),
                            row, pl.BlockSpec((ng, GROUP_DIM, GROUP_DIM), lambda i: (0, 0, 0)),
                            pl.BlockSpec((1, D_POOL), lambda i: (0, 0))],
                  out_specs=[row, pl.BlockSpec((ng, GROUP_DIM, GROUP_DIM), lambda i: (0, 0, 0)),
                             pl.BlockSpec((1, D_POOL), lambda i: (0, 0))],
                  sem=("arbitrary",))(dout, dout, pooled, pw, ps)


def _pad_cols(w, n):
    return jnp.pad(w, ((0, 0), (0, n - w.shape[1])))


def kernel(x, c, w_ada, b_ada, ffn1_norm_g, ffn1_w_in, ffn1_w_out, mix_norm_g, w_in, b_forget, q_norm_g, k_norm_g, pool_w, pool_scale, w_out, ffn2_norm_g, ffn2_w_in, ffn2_w_out, final_norm_g, loss_target, m_w_ada, m_b_ada, m_ffn1_norm_g, m_ffn1_w_in, m_ffn1_w_out, m_mix_norm_g, m_w_in, m_b_forget, m_q_norm_g, m_k_norm_g, m_pool_w, m_pool_scale, m_w_out, m_ffn2_norm_g, m_ffn2_w_in, m_ffn2_w_out, m_final_norm_g, v_w_ada, v_b_ada, v_ffn1_norm_g, v_ffn1_w_in, v_ffn1_w_out, v_mix_norm_g, v_w_in, v_b_forget, v_q_norm_g, v_k_norm_g, v_pool_w, v_pool_scale, v_w_out, v_ffn2_norm_g, v_ffn2_w_in, v_ffn2_w_out, v_final_norm_g):
    ax, ay, ac = _coords()
    chip = 2 * ax + ay
    me = 2 * chip + ac
    chip_core = jnp.stack([chip, ac]).astype(jnp.int32)
    core_arr = jnp.reshape(ac, (1,)).astype(jnp.int32)

    t = x.shape[1]
    xs = x.reshape(t, D)
    tgt = loss_target.reshape(t, D)
    hu = ffn1_w_out.shape[1]
    hup = -(-hu // LANE) * LANE
    ws_in = w_in.shape[2]
    ws_in_pad = -(-ws_in // LANE) * LANE
    n_ada = w_ada.shape[2]

    def ffn_in_shard(w):
        w = w[0].astype(BF16)
        return jnp.concatenate([_pad_cols(w[:, :hu], hup), _pad_cols(w[:, hu:], hup)], axis=1)

    def ffn_out_shard(w):
        return jnp.pad(w[0].astype(BF16), ((0, hup - hu), (0, 0)))

    shards1 = [ffn_in_shard(ffn1_w_in), ffn_out_shard(ffn1_w_out)]
    shards_mix = [_pad_cols(w_in[0].astype(BF16), ws_in_pad), pool_w[0].astype(BF16).reshape(GROUP_DIM, GROUP_DIM),
                  w_out[0].astype(BF16)]
    shards2 = [ffn_in_shard(ffn2_w_in), ffn_out_shard(ffn2_w_out)]

    def add_siblings(parts, recv, tag):
        return [_add_sibling(p, r, core_arr, f"{tag}_add_sibling_{k}") for k, (p, r) in enumerate(zip(parts, recv))]

    def add_chips(halves, recv, tag):
        return [_add_chips(hh, [(r, 0), (r, 1), (r, 2)], chip_core, f"{tag}_add_chips_{k}") for k, (hh, r) in enumerate(zip(halves, recv))]

    g_in1, = _exchange(_gather(shards1[:1]), "gather_ffn1_in")

    c_all = _allgather_small(c.reshape(8, D // 8), True, "gather_c").reshape(8, D)
    c16 = jnp.pad(c_all, ((0, 8), (0, 0)))
    b_ada_mine = lax.dynamic_slice(b_ada, (0, chip * n_ada), (1, n_ada))
    act16, mod16 = _ada_fwd(c16, w_ada[0], b_ada_mine, "ada_fwd")
    mod_all = _allgather_small(mod16[:8], False, "gather_mod")
    mod = lax.dynamic_index_in_dim(mod_all, me, axis=1, keepdims=False).reshape(N_MOD, 1, D)
    sh1, sc1, gt1, sh2, sc2, gt2, sh3, sc3, gt3 = [mod[k] for k in range(N_MOD)]

    gate1, gate3 = 0.5 * gt1, 0.5 * gt3
    h1 = _norm_mod(xs, ffn1_norm_g, sh1, sc1, "ffn1_norm")
    job = _gather(shards1[1:])
    a1, b1, u1 = _ffn_up(h1, g_in1, "ffn1_up", sides=[job])
    wg_out1 = job.results[0].reshape(4 * hup, D)
    job = _gather(shards_mix)
    y1, x1 = _mm_resid(u1, wg_out1, xs, gate1, "ffn1_down", sides=[job])
    g_win, g_pw, g_wout = job.results
    h2 = _norm_mod(x1, mix_norm_g, sh2, sc2, "mix_norm")
    w_full = jnp.concatenate([g_win[k, :, :ws_in] for k in range(4)], axis=1)
    nf = 3 * D_ATTN
    w_all = jnp.concatenate([w_full[:, :nf], w_full[:, nf + N_HEADS:], w_full[:, nf:nf + N_HEADS],
                             jnp.zeros((D, LANE - N_HEADS), BF16)], axis=1)
    pw_full = g_pw.reshape(4, 4, GROUP_DIM // 4, GROUP_DIM).transpose(1, 0, 2, 3).reshape(4, GROUP_DIM, GROUP_DIM)
    wo_full = g_wout.reshape(4 * g_wout.shape[1], D)

    proj = _mm(h2, w_all, F32, "mix_proj")
    b_pad = jnp.pad(b_forget, ((0, 0), (0, LANE - N_HEADS)))
    cum = _forget_fwd(proj, b_pad, "forget_fwd")
    cum_t = cum[:, :N_HEADS].T
    fq, fk = cum_t.reshape(N_HEADS, t, 1), cum_t.reshape(N_HEADS, 1, t)
    qn, kn, vb = _qkv_prep(proj, q_norm_g, k_norm_g, "qkv_prep")
    job = _gather(shards2[:1])
    attn, lse = _attn_fwd(qn, kn, vb, fq, fk, "attn_fwd", sides=[job])
    g_in2, = job.results
    pooled, pool_out = _pool_fwd(proj, pw_full, pool_scale, "pool_fwd")
    cat = jnp.concatenate([attn, pool_out], axis=1)
    y_mix, x2 = _mm_resid(cat, wo_full, x1, gt2, "mix_out")

    h3 = _norm_mod(x2, ffn2_norm_g, sh3, sc3, "ffn2_norm")
    job = _gather(shards2[1:])
    a3, b3, u3 = _ffn_up(h3, g_in2, "ffn2_up", sides=[job])
    g_out2, = job.results
    wg_out2 = g_out2.reshape(4 * hup, D)
    y3, x3 = _mm_resid(u3, wg_out2, x2, gate3, "ffn2_down")

    loss_part, dx3, d_final_g = _final_loss(x3, final_norm_g.reshape(1, D), tgt, "final_loss")
    loss = lax.psum(loss_part[0, 0], ("x", "y", "c"))

    all_cols = (0, D)
    dy3, dgt3 = _gate_bwd(dx3, y3, gate3, "ffn2_gate_bwd")
    dab3 = _ffn_dab(dy3, wg_out2, a3, b3, "ffn2_dab")
    parts2 = [_ffn_dwin(h3, dab3, all_cols, "ffn2_dwin"), _mm_tn(u3, dy3, BF16, hup, "ffn2_dwout").reshape(4, hup, D)]
    rs = _reduce_siblings(parts2)
    dh3 = _ffn_dh(dab3, g_in2, "ffn2_dh", sides=[rs])
    half_in2, half_out2 = add_siblings(parts2, rs.results, "ffn2")
    dx2, dsh3, dsc3, d_ng3 = _norm_mod_bwd(dh3, x2, dx3, ffn2_norm_g, sc3, "ffn2_norm_bwd")

    dz, dgt2 = _gate_bwd(dx2, y_mix, gt2, "mix_gate_bwd")
    dcat = _mm_nt(dz, wo_full, BF16, "mix_dcat")
    d_wo = _mm_tn(cat, dz, BF16, 1024, "mix_dwout")
    du_pool, d_pw, d_ps = _pool_bwd(dcat, pooled, pw_full, pool_scale, "pool_bwd")
    rs = _reduce_chips([half_in2])
    dqn, delta, dfq = _attn_bwd_q(qn, kn, vb, attn, dcat, lse, fq, fk, "attn_bwd_q", sides=[rs])
    mine_in2, = add_chips([half_in2], rs.results, "ffn2_in")
    rs, rs3 = _reduce_chips([half_out2]), _share_siblings([mine_in2])
    dkn, dv, dfk = _attn_bwd_kv(qn, kn, vb, dcat, lse, delta, fq, fk, "attn_bwd_kv", sides=[rs, rs3])
    r_in2, = rs3.results
    mine_out2, = add_chips([half_out2], rs.results, "ffn2_out")
    d_cum = jnp.pad((dfq.reshape(N_HEADS, t) + dfk.reshape(N_HEADS, t)).T, ((0, 0), (0, LANE - N_HEADS)))
    dfl, d_bf = _forget_bwd(d_cum, proj, b_pad, "forget_bwd")
    dq, dk, d_qg, d_kg = _qk_norm_bwd(dqn, dkn, proj, q_norm_g, k_norm_g, "qk_norm_bwd")
    dproj = jnp.concatenate([dq, dk, dv, du_pool, dfl], axis=1)
    rs3 = _share_siblings([mine_out2])
    d_wall = _mm_tn(dproj, h2, F32, 1408, "mix_dwin", sides=[rs3])
    r_out2, = rs3.results
    d_wfull = jnp.concatenate([d_wall[:nf], d_wall[D_PROJ:D_PROJ + N_HEADS], d_wall[nf:D_PROJ]], axis=0)
    p_win = jnp.stack([jnp.pad(d_wfull[k * ws_in:(k + 1) * ws_in], ((0, ws_in_pad - ws_in), (0, 0))) for k in range(4)]).astype(BF16)
    p_pw = d_pw.reshape(4, 4, GROUP_DIM // 4, GROUP_DIM).transpose(1, 0, 2, 3).reshape(4, GROUP_DIM, GROUP_DIM).astype(BF16)
    parts_mix = [p_win, p_pw, d_wo.reshape(4, D // 4, D)]
    rs = _reduce_siblings(parts_mix)
    dh2 = _mm_nt(dproj, w_all, F32, "mix_dh", sides=[rs])
    halves_mix = add_siblings(parts_mix, rs.results, "mix")
    dx1, dsh2, dsc2, d_ng2 = _norm_mod_bwd(dh2, x1, dx2, mix_norm_g, sc2, "mix_norm_bwd")

    dy1, dgt1 = _gate_bwd(dx1, y1, gate1, "ffn1_gate_bwd")
    d_out1 = _mm_tn(u1, dy1, BF16, hup, "ffn1_dwout").reshape(4, hup, D)
    rs, rs1 = _reduce_chips(halves_mix), _reduce_siblings([d_out1])
    dab1 = _ffn_dab(dy1, wg_out1, a1, b1, "ffn1_dab", sides=[rs, rs1])
    mine_mix = add_chips(halves_mix, rs.results, "mix")
    half_out1, = add_siblings([d_out1], rs1.results, "ffn1_out")
    rs, rs3 = _reduce_chips([half_out1]), _share_siblings(mine_mix)
    d_in1_lo = _ffn_dwin(h1, dab1, (0, LAST_PIECE_FROM), "ffn1_dwin_lo", sides=[rs, rs3])
    r_win, r_pw, r_wo = rs3.results
    mine_out1, = add_chips([half_out1], rs.results, "ffn1_out")
    rs, rs3 = _reduce_siblings([d_in1_lo]), _share_siblings([mine_out1])
    d_in1_hi = _ffn_dwin(h1, dab1, (LAST_PIECE_FROM, D), "ffn1_dwin_hi", sides=[rs, rs3])
    r_out1, = rs3.results
    half_lo, = add_siblings([d_in1_lo], rs.results, "ffn1_lo")
    rs, rs1 = _reduce_chips([half_lo]), _reduce_siblings([d_in1_hi])
    dh1 = _ffn_dh(dab1, g_in1, "ffn1_dh", sides=[rs, rs1])
    mine_lo, = add_chips([half_lo], rs.results, "ffn1_lo")
    half_hi, = add_siblings([d_in1_hi], rs1.results, "ffn1_hi")
    dx0, dsh1, dsc1, d_ng1 = _norm_mod_bwd(dh1, xs, dx1, ffn1_norm_g, sc1, "ffn1_norm_bwd")
    grad_x = dx0.reshape(1, t, D)
    dgt1, dgt3 = 0.5 * dgt1, 0.5 * dgt3
    mine_hi, = add_chips([half_hi], _exchange(_reduce_chips([half_hi]), "ffn1_hi_reduce_chips"), "ffn1_hi")
    r_lo, r_hi = _exchange(_share_siblings([mine_lo, mine_hi]), "ffn1_share_siblings")
    r_in1 = jnp.concatenate([r_lo, r_hi], axis=1)
    grads = {
        "ffn1_w_in": jnp.concatenate([r_in1[:hu], r_in1[hup:hup + hu]], axis=0),
        "ffn1_w_out": r_out1[:hu],
        "w_in": r_win[:ws_in],
        "pool_w": r_pw,
        "w_out": r_wo,
        "ffn2_w_in": jnp.concatenate([r_in2[:hu], r_in2[hup:hup + hu]], axis=0),
        "ffn2_w_out": r_out2[:hu],
    }
    hidden_in_rows = ("ffn1_w_in", "ffn2_w_in", "w_in")

    dmod = jnp.concatenate([dsh1, dsc1, dgt1, dsh2, dsc2, dgt2, dsh3, dsc3, dgt3], axis=1)
    small_names = ["b_ada", "ffn1_norm_g", "mix_norm_g", "b_forget", "q_norm_g", "k_norm_g", "pool_scale", "ffn2_norm_g",
                   "final_norm_g"]
    small_grads = [dmod, d_ng1, d_ng2, d_bf[:, :N_HEADS], d_qg, d_kg, d_ps, d_ng3, d_final_g]
    small_w = [b_ada, ffn1_norm_g, mix_norm_g, b_forget, q_norm_g, k_norm_g, pool_scale, ffn2_norm_g, final_norm_g.reshape(1, D)]
    small_m = [m_b_ada, m_ffn1_norm_g, m_mix_norm_g, m_b_forget, m_q_norm_g, m_k_norm_g, m_pool_scale, m_ffn2_norm_g,
               m_final_norm_g.reshape(1, D)]
    small_v = [v_b_ada, v_ffn1_norm_g, v_mix_norm_g, v_b_forget, v_q_norm_g, v_k_norm_g, v_pool_scale, v_ffn2_norm_g,
               v_final_norm_g.reshape(1, D)]
    sizes = [g.shape[1] for g in small_grads]
    n_small = sum(sizes)
    n_pack = -(-n_small // (8 * LANE)) * (8 * LANE)

    def pack(vs, fill):
        flat = jnp.concatenate([v.reshape(1, -1) for v in vs], axis=1)
        return jnp.pad(flat, ((0, 0), (0, n_pack - n_small)), constant_values=fill).reshape(8, n_pack // 8)

    g8 = _allgather_small(pack(small_grads, 0.0), True, "gather_small_grads")
    gs, ds, ms, vs = _adamw_small(pack(small_w, 0.0), g8, pack(small_m, 0.0), pack(small_v, 1.0), "adamw_small")

    def unpack(p):
        flat = p.reshape(1, n_pack)
        out, off = {}, 0
        for nme, sz in zip(small_names, sizes):
            out[nme] = flat[:, off:off + sz]
            off += sz
        return out

    small = [unpack(p) for p in (gs, ds, ms, vs)]
    for dct in small:
        dct["final_norm_g"] = dct["final_norm_g"].reshape(D)

    dmod_all = g8.reshape(8, n_pack)[:, :N_MOD * D]
    dmod_mine = lax.dynamic_slice(dmod_all, (0, chip * n_ada), (8, n_ada))
    grads["w_ada"] = _ada_bwd(act16, jnp.pad(dmod_mine, ((0, 8), (0, 0))), "ada_bwd")

    big = {"w_ada": (w_ada, m_w_ada, v_w_ada), "ffn1_w_in": (ffn1_w_in, m_ffn1_w_in, v_ffn1_w_in),
           "ffn1_w_out": (ffn1_w_out, m_ffn1_w_out, v_ffn1_w_out), "w_in": (w_in, m_w_in, v_w_in),
           "pool_w": (pool_w, m_pool_w, v_pool_w), "w_out": (w_out, m_w_out, v_w_out),
           "ffn2_w_in": (ffn2_w_in, m_ffn2_w_in, v_ffn2_w_in), "ffn2_w_out": (ffn2_w_out, m_ffn2_w_out, v_ffn2_w_out)}
    res = {}

    for nme, (w, m, v) in big.items():
        shp = w.shape
        g2 = grads[nme]
        if nme in hidden_in_rows:
            rows = g2.shape[0]
            more = -rows % 8

            def fit(a):
                return jnp.pad(a, ((0, more), (0, 0))) if more else a

            outs = _adamw(fit(w[0].T), fit(g2), fit(m[0].T), fit(v[0].T), f"adamw_{nme}")
            res[nme] = tuple(o[:rows].T.reshape(shp) for o in (g2, *outs))
            continue
        two = g2.shape
        d, mo, vo = _adamw(w.reshape(two), g2, m.reshape(two), v.reshape(two), f"adamw_{nme}")
        res[nme] = (g2.reshape(shp), d.reshape(shp), mo.reshape(shp), vo.reshape(shp))
    for nme in small_names:
        res[nme] = tuple(dct[nme] for dct in small)

    order = ["w_ada", "b_ada", "ffn1_norm_g", "ffn1_w_in", "ffn1_w_out", "mix_norm_g", "w_in", "b_forget", "q_norm_g",
             "k_norm_g", "pool_w", "pool_scale", "w_out", "ffn2_norm_g", "ffn2_w_in", "ffn2_w_out", "final_norm_g"]
    return (loss, grad_x, *[res[n][0] for n in order], *[res[n][1] for n in order], *[res[n][2] for n in order],
            *[res[n][3] for n in order])
```

```python
import functools

import jax
import jax.numpy as jnp
from jax import lax
from jax.experimental import pallas as pl
from jax.experimental.pallas import tpu as pltpu

F32 = jnp.float32
BF16 = jnp.bfloat16

D = 2048
N_HEADS = 8
HEAD_DIM = 128
D_ATTN = 1024
D_POOL = 1024
POOL_WINDOWS = (2, 4, 8, 16)
GROUP_DIM = 256
HALO = 16
N_MOD = 9
EPS = 1e-6
D_PROJ = 3 * D_ATTN + D_POOL
D_PROJ_PAD = D_PROJ + 128
LANE = 128
ATTN_BLOCK = 512
POOL_BLOCK = 512
FFN_ROW_CHUNKS = 4
ATTN_HEADS = (2, 2, 2)
ATTN_ROW_CHUNKS = (2, 2, 2)

ADAM_LR = 0.001
ADAM_B1 = 0.9
ADAM_B2 = 0.999
ADAM_EPS = 1e-08
ADAM_WD = 0.01
ADAM_STEP = 10

VMEM_LIMIT_V7X = 56 * 1024 * 1024
MESH_ID = pl.DeviceIdType.MESH
ANY = pl.BlockSpec(memory_space=pl.ANY)
VMEM = pl.BlockSpec(memory_space=pltpu.VMEM)

NT = (((1,), (1,)), ((), ()))
TN = (((0,), (0,)), ((), ()))


class _Side:
    def __init__(self, ins, outs, nsem, start, wait, alias=False, mids=()):
        self.ins, self.outs, self.nsem, self.start, self.wait, self.alias = list(ins), list(outs), nsem, start, wait, alias
        self.mids = list(mids)
        self.results = None


def _pcall(body, *, name, out_shape, grid=None, in_specs=None, out_specs=None, scratch=(), sem=None, prefetch=0, sides=()):
    sides = list(sides)
    single = not isinstance(out_shape, (list, tuple))
    shapes = [out_shape] if single else list(out_shape)
    in_specs = list(in_specs)
    ospecs = [out_specs] if single else list(out_specs)
    scratch = list(scratch)
    n_in, n_out, n_scr = len(in_specs), len(shapes), len(scratch)
    assert not (sides and prefetch)
    aliases = {}
    for sd in sides:
        if sd.alias:
            for k in range(len(sd.outs)):
                aliases[len(in_specs) + k] = len(shapes) + k
        in_specs += [ANY] * len(sd.ins)
        shapes += sd.outs
        ospecs += [ANY] * len(sd.outs)
        scratch += [pltpu.SemaphoreType.DMA((sd.nsem,)), pltpu.SemaphoreType.DMA((sd.nsem,))]

    def wrapped(*refs):
        ins, outs, scr = refs[:len(in_specs)], refs[len(in_specs):len(in_specs) + len(shapes)], refs[len(in_specs) + len(shapes):]
        step, steps = 0, 1
        for ax, g in enumerate(grid or ()):
            step, steps = step * g + pl.program_id(ax), steps * g

        def at(when, fn):
            if grid:
                pl.when(step == when)(fn)
            else:
                fn()

        i0, o0 = n_in, n_out
        for k, sd in enumerate(sides):
            refs_k = (ins[i0:i0 + len(sd.ins)], outs[o0:o0 + len(sd.outs)], scr[n_scr + 2 * k], scr[n_scr + 2 * k + 1])
            at(0, functools.partial(sd.start, *refs_k))
            for frac, fn in sd.mids:
                at(min(steps - 1, int(frac * steps)), functools.partial(fn, *refs_k))
            i0, o0 = i0 + len(sd.ins), o0 + len(sd.outs)
        body(*ins[:n_in], *outs[:n_out], *scr[:n_scr])
        i0, o0 = n_in, n_out
        for k, sd in enumerate(sides):
            refs_k = (ins[i0:i0 + len(sd.ins)], outs[o0:o0 + len(sd.outs)], scr[n_scr + 2 * k], scr[n_scr + 2 * k + 1])
            at(steps - 1, functools.partial(sd.wait, *refs_k))
            i0, o0 = i0 + len(sd.ins), o0 + len(sd.outs)

    params = dict(vmem_limit_bytes=VMEM_LIMIT_V7X)
    if sides and grid:
        params["dimension_semantics"] = ("arbitrary",) * len(grid)
    elif sem is not None:
        params["dimension_semantics"] = sem
    kw = dict(name=name, out_shape=shapes if (sides or not single) else shapes[0], compiler_params=pltpu.CompilerParams(**params))
    if aliases:
        kw["input_output_aliases"] = aliases
    final_ospecs = ospecs if (sides or not single) else ospecs[0]
    if prefetch:
        kw["grid_spec"] = pltpu.PrefetchScalarGridSpec(
            num_scalar_prefetch=prefetch, grid=grid, in_specs=in_specs, out_specs=final_ospecs, scratch_shapes=scratch)
    else:
        if grid is not None:
            kw["grid"] = grid
        kw["in_specs"] = in_specs
        kw["out_specs"] = final_ospecs
        kw["scratch_shapes"] = scratch
    call = pl.pallas_call(wrapped if sides else body, **kw)
    if not sides:
        return call

    def run(*operands):
        res = list(call(*operands, *[a for sd in sides for a in sd.ins]))
        o0 = n_out
        for sd in sides:
            sd.results = res[o0:o0 + len(sd.outs)]
            o0 += len(sd.outs)
        return res[0] if single else res[:n_out]

    return run


def _pick(n, cap, mult):
    best = None
    for d in range(mult, min(n, cap) + 1, mult):
        if n % d == 0:
            best = d
    assert best is not None, (n, cap, mult)
    return best


def _coords():
    return lax.axis_index("x"), lax.axis_index("y"), lax.axis_index("c")


def _dot(a, b, dims=None):
    if dims is None:
        return jnp.dot(a, b, preferred_element_type=F32)
    return lax.dot_general(a, b, dims, preferred_element_type=F32)


def _remote(src, dst, ssem, rsem, dev):
    return pltpu.make_async_remote_copy(src_ref=src, dst_ref=dst, send_sem=ssem, recv_sem=rsem,
                                        device_id=dev, device_id_type=MESH_ID)


def _allgather_small(v, whole_mesh, name):
    masks = list(range(1, 8)) if whole_mesh else [4, 2, 6]
    nslot = 8 if whole_mesh else 4

    def slot(px, py, pc):
        return 4 * px + 2 * py + pc if whole_mesh else 2 * px + py

    def body(v_ref, out_ref, ssem, rsem, lsem):
        x, y, c = _coords()
        mine = slot(x, y, c)
        peers = [(jnp.bitwise_xor(x, (m >> 2) & 1), jnp.bitwise_xor(y, (m >> 1) & 1), jnp.bitwise_xor(c, m & 1))
                 for m in masks]
        loc = pltpu.make_async_copy(v_ref, out_ref.at[mine], lsem)
        loc.start()
        sends = [_remote(v_ref, out_ref.at[mine], ssem.at[k], rsem.at[k], p) for k, p in enumerate(peers)]
        for cp in sends:
            cp.start()
        for k, p in enumerate(peers):
            _remote(v_ref, out_ref.at[slot(*p)], ssem.at[k], rsem.at[k], p).wait_recv()
        for cp in sends:
            cp.wait_send()
        loc.wait()

    return _pcall(body, name=name, out_shape=jax.ShapeDtypeStruct((nslot,) + v.shape, v.dtype),
                  in_specs=[VMEM], out_specs=VMEM,
                  scratch=[pltpu.SemaphoreType.DMA((len(masks),)), pltpu.SemaphoreType.DMA((len(masks),)),
                           pltpu.SemaphoreType.DMA(())])(v)


def _other_chips(x, y):
    return [(1 - x, y), (x, 1 - y), (1 - x, 1 - y)]


def _half_rows(shard_rows, core):
    h = shard_rows // 2
    return pl.ds(pl.multiple_of(core * h, 16), h)


def _later(src, dst, ssem, rsem, dev):
    return functools.partial(_remote, src, dst, ssem, rsem, dev)


def _side_from(ins, outs, nsem, pairs_of, alias=False):
    def start(*refs):
        for send, _ in pairs_of(*refs):
            send().start()

    def wait(*refs):
        pairs = pairs_of(*refs)
        for _, recv in pairs:
            recv().wait_recv()
        for send, _ in pairs:
            send().wait_send()

    return _Side(ins, outs, nsem, start, wait, alias)


GATHER_STAGES = (0.55, 0.8)
LAST_PIECE_FROM = 1536


def _gather(ws):
    def copies(w_refs, g_refs, ssem, rsem):
        x, y, c = _coords()
        me, sib = 2 * x + y, (x, y, 1 - c)
        across_x, across_y, far = 2 * (1 - x) + y, 2 * x + (1 - y), 2 * (1 - x) + (1 - y)
        to_x, to_y = (1 - x, y, c), (x, 1 - y, c)
        out = []
        for a, w in enumerate(ws):
            h = w.shape[0] // 2
            rows, theirs = _half_rows(w.shape[0], c), _half_rows(w.shape[0], 1 - c)
            first = pl.ds(pl.multiple_of(c * h, 16), h // 2)
            second = pl.ds(pl.multiple_of(c * h + h // 2, 16), h // 2)
            g = g_refs[a]

            def pair(k, src, dst, got, dev):
                sems = (ssem.at[8 * a + k], rsem.at[8 * a + k], dev)
                return _later(src, dst, *sems), _later(got, got, *sems)

            out.append(dict(
                x1=pair(0, w_refs[a].at[rows], g.at[me, rows], g.at[across_x, rows], to_x),
                y1=pair(1, w_refs[a].at[rows], g.at[me, rows], g.at[across_y, rows], to_y),
                x2=pair(2, g.at[across_y, first], g.at[across_y, first], g.at[far, first], to_x),
                y2=pair(3, g.at[across_x, second], g.at[across_x, second], g.at[far, second], to_y),
                sx=pair(4, g.at[across_x, rows], g.at[across_x, rows], g.at[across_x, theirs], sib),
                sy=pair(5, g.at[across_y, rows], g.at[across_y, rows], g.at[across_y, theirs], sib),
                sf=pair(6, g.at[far, rows], g.at[far, rows], g.at[far, theirs], sib),
                so=pair(7, w_refs[a], g.at[me], g.at[me], sib)))
        return out

    def start(*refs):
        for cp in copies(*refs):
            cp["x1"][0]().start()
            cp["y1"][0]().start()

    def stage2(*refs):
        for cp in copies(*refs):
            cp["x1"][1]().wait_recv()
            cp["y1"][1]().wait_recv()
            for k in ("x2", "y2", "sx", "sy", "so"):
                cp[k][0]().start()

    def stage3(*refs):
        for cp in copies(*refs):
            cp["x2"][1]().wait_recv()
            cp["y2"][1]().wait_recv()
            cp["sf"][0]().start()

    def wait(*refs):
        cps = copies(*refs)
        for cp in cps:
            for k in ("sx", "sy", "sf", "so"):
                cp[k][1]().wait_recv()
        for cp in cps:
            for send, _ in cp.values():
                send().wait_send()

    return _Side(ws, [jax.ShapeDtypeStruct((4,) + w.shape, w.dtype) for w in ws], 8 * len(ws), start, wait,
                 mids=[(GATHER_STAGES[0], stage2), (GATHER_STAGES[1], stage3)])


def _reduce_siblings(ps):
    def pairs_of(p_refs, r_refs, ssem, rsem):
        x, y, c = _coords()
        out = []
        for a, p in enumerate(ps):
            src = p_refs[a].at[:, _half_rows(p.shape[1], 1 - c), :]
            cp = _later(src, r_refs[a], ssem.at[a], rsem.at[a], (x, y, 1 - c))
            out.append((cp, cp))
        return out

    return _side_from(ps, [jax.ShapeDtypeStruct((4, p.shape[1] // 2, p.shape[2]), p.dtype) for p in ps], len(ps), pairs_of)


def _reduce_chips(hs, dests=(0, 1, 2)):
    nd = len(dests)

    def pairs_of(h_refs, o_refs, ssem, rsem):
        x, y, c = _coords()
        chips = _other_chips(x, y)
        out = []
        for a in range(len(hs)):
            for slot, j in enumerate(dests):
                chip = chips[j]
                cp = _later(h_refs[a].at[2 * chip[0] + chip[1]], o_refs[a].at[slot], ssem.at[nd * a + slot], rsem.at[nd * a + slot],
                            (*chip, c))
                out.append((cp, cp))
        return out

    return _side_from(hs, [jax.ShapeDtypeStruct((nd,) + h.shape[1:], h.dtype) for h in hs], nd * len(hs), pairs_of)


def _share_siblings(gs):
    def pairs_of(_, g_refs, ssem, rsem):
        x, y, c = _coords()
        out = []
        for a, g in enumerate(gs):
            mine, theirs = g_refs[a].at[_half_rows(g.shape[0], c)], g_refs[a].at[_half_rows(g.shape[0], 1 - c)]
            sems = (ssem.at[a], rsem.at[a], (x, y, 1 - c))
            out.append((_later(mine, mine, *sems), _later(theirs, theirs, *sems)))
        return out

    return _side_from(gs, [jax.ShapeDtypeStruct(g.shape, g.dtype) for g in gs], len(gs), pairs_of, alias=True)


def _exchange(side, name):
    def body():
        pass

    _pcall(body, name=name, out_shape=[], in_specs=[], out_specs=[], sides=[side])()
    return side.results


def _add_sibling(p, r, core, name):
    _, rr, cc = p.shape
    h = rr // 2
    th = _pick(h, max(16, (2 << 20) // (2 * cc)), 16)
    nb = h // th

    def body(c_ref, p_ref, r_ref, o_ref):
        o_ref[...] = (p_ref[...].astype(F32) + r_ref[...].astype(F32)).astype(BF16)

    return _pcall(body, name=name, out_shape=jax.ShapeDtypeStruct((4, h, cc), BF16), grid=(4, nb),
                  in_specs=[pl.BlockSpec((None, th, cc), lambda k, i, c_ref: (k, c_ref[0] * nb + i, 0)),
                            pl.BlockSpec((None, th, cc), lambda k, i, c_ref: (k, i, 0))],
                  out_specs=pl.BlockSpec((None, th, cc), lambda k, i, c_ref: (k, i, 0)),
                  sem=("parallel", "parallel"), prefetch=1)(core, p, r)


def _add_chips(hh, pieces, chip_core, name):
    _, h, cc = hh.shape
    th = _pick(h, max(16, (2 << 20) // (2 * cc)), 16)
    nb = h // th

    def body(k_ref, h_ref, r0_ref, r1_ref, r2_ref, o_ref):
        s = h_ref[...].astype(F32) + r0_ref[...].astype(F32)
        s = s + r1_ref[...].astype(F32)
        o_ref[...] = s + r2_ref[...].astype(F32)

    def piece(slot):
        return pl.BlockSpec((None, th, cc), lambda i, k_ref: (slot, i, 0))

    return _pcall(body, name=name, out_shape=jax.ShapeDtypeStruct((2 * h, cc), F32), grid=(nb,),
                  in_specs=[pl.BlockSpec((None, th, cc), lambda i, k_ref: (k_ref[0], i, 0))] + [piece(s) for _, s in pieces],
                  out_specs=pl.BlockSpec((th, cc), lambda i, k_ref: (k_ref[1] * nb + i, 0)),
                  sem=("parallel",), prefetch=1)(chip_core, hh, *[a for a, _ in pieces])


def _adamw_math(w, g, m, v):
    m = ADAM_B1 * m + (1.0 - ADAM_B1) * g
    v = ADAM_B2 * v + (1.0 - ADAM_B2) * (g * g)
    m_hat = m / (1.0 - ADAM_B1 ** ADAM_STEP)
    v_hat = v / (1.0 - ADAM_B2 ** ADAM_STEP)
    delta = -ADAM_LR * (m_hat / (jnp.sqrt(v_hat) + ADAM_EPS) + ADAM_WD * w)
    return delta, m, v


def _adamw(w, g, m, v, name, sides=()):
    rr, cc = w.shape
    tr = _pick(rr, max(8, (3 << 20) // (4 * cc)), 8)

    def body(w_ref, g_ref, m_ref, v_ref, d_ref, mo_ref, vo_ref):
        d, mm, vv = _adamw_math(w_ref[...], g_ref[...], m_ref[...], v_ref[...])
        d_ref[...] = d
        mo_ref[...] = mm
        vo_ref[...] = vv

    spec = pl.BlockSpec((tr, cc), lambda i: (i, 0))
    return _pcall(body, name=name, out_shape=[jax.ShapeDtypeStruct(w.shape, F32)] * 3, grid=(rr // tr,),
                  in_specs=[spec] * 4, out_specs=[spec] * 3, sem=("parallel",), sides=sides)(w, g, m, v)


def _adamw_small(w, g8, m, v, name):
    def body(w_ref, g_ref, m_ref, v_ref, go_ref, d_ref, mo_ref, vo_ref):
        g = g_ref[0]
        for k in range(1, 8):
            g = g + g_ref[k]
        d, mm, vv = _adamw_math(w_ref[...], g, m_ref[...], v_ref[...])
        go_ref[...] = g
        d_ref[...] = d
        mo_ref[...] = mm
        vo_ref[...] = vv

    return _pcall(body, name=name, out_shape=[jax.ShapeDtypeStruct(w.shape, F32)] * 4,
                  in_specs=[VMEM] * 4, out_specs=[VMEM] * 4)(w, g8, m, v)


def _ada_fwd(c16, w_ada, b_ada, name):
    n = w_ada.shape[1]
    tn = _pick(n, 512, LANE)

    def body(c_ref, w_ref, b_ref, act_ref, mod_ref):
        cv = c_ref[...]
        act = cv * jax.nn.sigmoid(cv)
        act_ref[...] = act
        mod_ref[...] = _dot(act.astype(BF16), w_ref[...].astype(BF16)) + b_ref[...]

    return _pcall(body, name=name,
                  out_shape=[jax.ShapeDtypeStruct((16, D), F32), jax.ShapeDtypeStruct((16, n), F32)], grid=(n // tn,),
                  in_specs=[pl.BlockSpec((16, D), lambda j: (0, 0)), pl.BlockSpec((D, tn), lambda j: (0, j)),
                            pl.BlockSpec((1, tn), lambda j: (0, j))],
                  out_specs=[pl.BlockSpec((16, D), lambda j: (0, 0)), pl.BlockSpec((16, tn), lambda j: (0, j))],
                  sem=("arbitrary",))(c16, w_ada, b_ada)


def _ada_bwd(act, dmod, name, sides=()):
    n = dmod.shape[1]
    tm, tn = 512, _pick(n, 1024, LANE)

    def body(a_ref, d_ref, o_ref):
        o_ref[...] = _dot(a_ref[...].astype(BF16), d_ref[...].astype(BF16), TN)

    return _pcall(body, name=name, out_shape=jax.ShapeDtypeStruct((D, n), F32), grid=(D // tm, n // tn),
                  in_specs=[pl.BlockSpec((16, tm), lambda i, j: (0, i)), pl.BlockSpec((16, tn), lambda i, j: (0, j))],
                  out_specs=pl.BlockSpec((tm, tn), lambda i, j: (i, j)), sem=("parallel", "parallel"), sides=sides)(act, dmod)


def _norm_mod(x, g, sh, sc, name, sides=()):
    t = x.shape[0]
    tm = _pick(t, 512, 8)

    def body(x_ref, g_ref, sh_ref, sc_ref, h_ref):
        xf = x_ref[...]
        r = lax.rsqrt(jnp.mean(xf * xf, axis=-1, keepdims=True) + EPS)
        h = (xf * r) * g_ref[...]
        h_ref[...] = (h * (1.0 + sc_ref[...]) + sh_ref[...]).astype(BF16)

    vec = pl.BlockSpec((1, D), lambda i: (0, 0))
    row = pl.BlockSpec((tm, D), lambda i: (i, 0))
    return _pcall(body, name=name, out_shape=jax.ShapeDtypeStruct((t, D), BF16), grid=(t // tm,),
                  in_specs=[row, vec, vec, vec], out_specs=row, sem=("parallel",), sides=sides)(x, g, sh, sc)


def _norm_mod_bwd(dh, x, dxo, g, sc, name, sides=()):
    t = x.shape[0]
    tm = _pick(t, 256, 8)

    def body(dh_ref, x_ref, dxo_ref, g_ref, sc_ref, dx_ref, dsh_ref, dsc_ref, dg_ref):
        @pl.when(pl.program_id(0) == 0)
        def _():
            dsh_ref[...] = jnp.zeros_like(dsh_ref)
            dsc_ref[...] = jnp.zeros_like(dsc_ref)
            dg_ref[...] = jnp.zeros_like(dg_ref)

        xf, dh_ = x_ref[...], dh_ref[...]
        r = lax.rsqrt(jnp.mean(xf * xf, axis=-1, keepdims=True) + EPS)
        xhat = xf * r
        dsh_ref[...] += jnp.sum(dh_, axis=0, keepdims=True)
        dsc_ref[...] += jnp.sum(dh_ * (xhat * g_ref[...]), axis=0, keepdims=True)
        tt = dh_ * (1.0 + sc_ref[...])
        dg_ref[...] += jnp.sum(tt * xhat, axis=0, keepdims=True)
        dxh = tt * g_ref[...]
        dx_ref[...] = r * (dxh - xhat * jnp.mean(dxh * xhat, axis=-1, keepdims=True)) + dxo_ref[...]

    vec = pl.BlockSpec((1, D), lambda i: (0, 0))
    row = pl.BlockSpec((tm, D), lambda i: (i, 0))
    vshape = jax.ShapeDtypeStruct((1, D), F32)
    return _pcall(body, name=name, out_shape=[jax.ShapeDtypeStruct((t, D), F32), vshape, vshape, vshape], grid=(t // tm,),
                  in_specs=[row, row, row, vec, vec], out_specs=[row, vec, vec, vec], sem=("arbitrary",), sides=sides)(dh, x, dxo, g, sc)


def _gate_bwd(dxo, yy, gate, name):
    t = dxo.shape[0]
    tm = _pick(t, 512, 8)

    def body(dx_ref, y_ref, g_ref, dy_ref, dg_ref):
        @pl.when(pl.program_id(0) == 0)
        def _():
            dg_ref[...] = jnp.zeros_like(dg_ref)

        dx = dx_ref[...]
        dy_ref[...] = (dx * g_ref[...]).astype(BF16)
        dg_ref[...] += jnp.sum(dx * y_ref[...], axis=0, keepdims=True)

    vec = pl.BlockSpec((1, D), lambda i: (0, 0))
    row = pl.BlockSpec((tm, D), lambda i: (i, 0))
    return _pcall(body, name=name, out_shape=[jax.ShapeDtypeStruct((t, D), BF16), jax.ShapeDtypeStruct((1, D), F32)],
                  grid=(t // tm,), in_specs=[row, row, vec], out_specs=[row, vec], sem=("arbitrary",))(dxo, yy, gate)


def _final_loss(x, g, tgt, name):
    t = x.shape[0]
    tm = _pick(t, 256, 8)

    def body(x_ref, g_ref, t_ref, loss_ref, dx_ref, dg_ref):
        @pl.when(pl.program_id(0) == 0)
        def _():
            loss_ref[...] = jnp.zeros_like(loss_ref)
            dg_ref[...] = jnp.zeros_like(dg_ref)

        xf = x_ref[...]
        r = lax.rsqrt(jnp.mean(xf * xf, axis=-1, keepdims=True) + EPS)
        xhat = xf * r
        e = xhat * g_ref[...] - t_ref[...]
        per_tok = jnp.mean(e * e, axis=-1, keepdims=True)
        loss_ref[...] += 0.5 * jnp.sum(per_tok, axis=0, keepdims=True)
        dy = e * (1.0 / D)
        dg_ref[...] += jnp.sum(dy * xhat, axis=0, keepdims=True)
        dxh = dy * g_ref[...]
        dx_ref[...] = r * (dxh - xhat * jnp.mean(dxh * xhat, axis=-1, keepdims=True))

    vec = pl.BlockSpec((1, D), lambda i: (0, 0))
    row = pl.BlockSpec((tm, D), lambda i: (i, 0))
    return _pcall(body, name=name,
                  out_shape=[jax.ShapeDtypeStruct((1, LANE), F32), jax.ShapeDtypeStruct((t, D), F32),
                             jax.ShapeDtypeStruct((1, D), F32)],
                  grid=(t // tm,), in_specs=[row, vec, row],
                  out_specs=[pl.BlockSpec((1, LANE), lambda i: (0, 0)), row, vec], sem=("arbitrary",))(x, g, tgt)


def _mm(a, b, out_dtype, name):
    m, k = a.shape
    n = b.shape[1]
    tm = _pick(m, 1024, 8)
    tn = _pick(n, 1408, LANE)

    def body(a_ref, b_ref, o_ref):
        o_ref[...] = _dot(a_ref[...], b_ref[...]).astype(out_dtype)

    return _pcall(body, name=name, out_shape=jax.ShapeDtypeStruct((m, n), out_dtype), grid=(m // tm, n // tn),
                  in_specs=[pl.BlockSpec((tm, k), lambda i, j: (i, 0)), pl.BlockSpec((k, tn), lambda i, j: (0, j))],
                  out_specs=pl.BlockSpec((tm, tn), lambda i, j: (i, j)), sem=("parallel", "parallel"))(a, b)


def _mm_resid(a, b, resid, gate, name, sides=()):
    m, k = a.shape
    n = b.shape[1]
    tm, tn, tk = _pick(m, 1024, 8), _pick(n, 1024, LANE), _pick(k, 2816, LANE)
    nk = k // tk

    def body(a_ref, b_ref, r_ref, g_ref, y_ref, o_ref, acc):
        kk = pl.program_id(2)

        @pl.when(kk == 0)
        def _():
            acc[...] = jnp.zeros_like(acc)

        acc[...] += _dot(a_ref[...], b_ref[...])

        @pl.when(kk == nk - 1)
        def _():
            y_ref[...] = acc[...].astype(BF16)
            o_ref[...] = r_ref[...] + g_ref[...] * acc[...]

    blk = pl.BlockSpec((tm, tn), lambda i, j, kk: (i, j))
    return _pcall(body, name=name, out_shape=[jax.ShapeDtypeStruct((m, n), BF16), jax.ShapeDtypeStruct((m, n), F32)],
                  grid=(m // tm, n // tn, nk),
                  in_specs=[pl.BlockSpec((tm, tk), lambda i, j, kk: (i, kk)), pl.BlockSpec((tk, tn), lambda i, j, kk: (kk, j)),
                            blk, pl.BlockSpec((1, tn), lambda i, j, kk: (0, j))],
                  out_specs=[blk, blk], scratch=[pltpu.VMEM((tm, tn), F32)],
                  sem=("parallel", "parallel", "arbitrary"), sides=sides)(a, b, resid, gate)


def _mm_nt(a, b, out_dtype, name, sides=()):
    m, k = a.shape
    n = b.shape[0]
    tm, tn, tk = _pick(m, 1024, 8), _pick(n, 1024, LANE), _pick(k, 2816, LANE)
    nk = k // tk

    def body(a_ref, b_ref, o_ref, acc):
        kk = pl.program_id(2)

        @pl.when(kk == 0)
        def _():
            acc[...] = jnp.zeros_like(acc)

        acc[...] += _dot(a_ref[...], b_ref[...], NT)

        @pl.when(kk == nk - 1)
        def _():
            o_ref[...] = acc[...].astype(out_dtype)

    return _pcall(body, name=name, out_shape=jax.ShapeDtypeStruct((m, n), out_dtype), grid=(m // tm, n // tn, nk),
                  in_specs=[pl.BlockSpec((tm, tk), lambda i, j, kk: (i, kk)), pl.BlockSpec((tn, tk), lambda i, j, kk: (j, kk))],
                  out_specs=pl.BlockSpec((tm, tn), lambda i, j, kk: (i, j)), scratch=[pltpu.VMEM((tm, tn), F32)],
                  sem=("parallel", "parallel", "arbitrary"), sides=sides)(a, b)


def _mm_tn(a, b, out_dtype, tm_cap, name, sides=()):
    t, m = a.shape
    n = b.shape[1]
    tm, tn, tk = _pick(m, tm_cap, LANE), _pick(n, 1408, LANE), _pick(t, 2048, 16)
    nk = t // tk

    def body(a_ref, b_ref, o_ref, acc):
        kk = pl.program_id(2)

        @pl.when(kk == 0)
        def _():
            acc[...] = jnp.zeros_like(acc)

        acc[...] += _dot(a_ref[...], b_ref[...], TN)

        @pl.when(kk == nk - 1)
        def _():
            o_ref[...] = acc[...].astype(out_dtype)

    return _pcall(body, name=name, out_shape=jax.ShapeDtypeStruct((m, n), out_dtype), grid=(m // tm, n // tn, nk),
                  in_specs=[pl.BlockSpec((tk, tm), lambda i, j, kk: (kk, i)), pl.BlockSpec((tk, tn), lambda i, j, kk: (kk, j))],
                  out_specs=pl.BlockSpec((tm, tn), lambda i, j, kk: (i, j)), scratch=[pltpu.VMEM((tm, tn), F32)],
                  sem=("parallel", "parallel", "arbitrary"), sides=sides)(a, b)


def _ffn_up(h, wg, name, sides=()):
    t = h.shape[0]
    fp = wg.shape[2]
    tm, tn = _pick(t, 2048, 8), 256
    nn = fp // tn

    def body(h_ref, wa_ref, wb_ref, a_ref, b_ref, u_ref):
        for r0 in range(0, tm, tm // FFN_ROW_CHUNKS):
            rows = slice(r0, r0 + tm // FFN_ROW_CHUNKS)
            hh = h_ref[rows, :]
            a = _dot(hh, wa_ref[...])
            b = _dot(hh, wb_ref[...])
            a_ref[rows, :] = a.astype(BF16)
            b_ref[rows, :] = b.astype(BF16)
            u_ref[rows, :] = (a * jax.nn.sigmoid(a) * b).astype(BF16)

    out = pl.BlockSpec((tm, tn), lambda i, j, n: (i, j * nn + n))
    return _pcall(body, name=name, out_shape=[jax.ShapeDtypeStruct((t, 2 * fp), BF16)] * 3, grid=(t // tm, 2, nn),
                  in_specs=[pl.BlockSpec((tm, D), lambda i, j, n: (i, 0)),
                            pl.BlockSpec((None, D, tn), lambda i, j, n: (j, 0, n)),
                            pl.BlockSpec((None, D, tn), lambda i, j, n: (j + 2, 0, n))],
                  out_specs=[out, out, out], sem=("parallel", "parallel", "parallel"), sides=sides)(h, wg, wg)


def _ffn_dab(dy, wo, a, b, name, sides=()):
    t = dy.shape[0]
    f2 = wo.shape[0]
    tm, tn = _pick(t, 2048, 8), 256

    def body(dy_ref, w_ref, a_ref, b_ref, o_ref):
        for r0 in range(0, tm, tm // FFN_ROW_CHUNKS):
            rows = slice(r0, r0 + tm // FFN_ROW_CHUNKS)
            du = _dot(dy_ref[rows, :], w_ref[...], NT)
            av, bv = a_ref[rows, :].astype(F32), b_ref[rows, :].astype(F32)
            sg = jax.nn.sigmoid(av)
            o_ref[0, rows, :] = (du * bv * (sg * (1.0 + av * (1.0 - sg)))).astype(BF16)
            o_ref[1, rows, :] = (du * (av * sg)).astype(BF16)

    blk = pl.BlockSpec((tm, tn), lambda i, n: (i, n))
    return _pcall(body, name=name, out_shape=jax.ShapeDtypeStruct((2, t, f2), BF16), grid=(t // tm, f2 // tn),
                  in_specs=[pl.BlockSpec((tm, D), lambda i, n: (i, 0)), pl.BlockSpec((tn, D), lambda i, n: (n, 0)), blk, blk],
                  out_specs=pl.BlockSpec((2, tm, tn), lambda i, n: (0, i, n)), sem=("parallel", "parallel"), sides=sides)(dy, wo, a, b)


def _ffn_dwin(h, dab, cols, name, sides=()):
    t = h.shape[0]
    fp = dab.shape[2] // 2
    tn = 1024 if (cols[0] % 1024 == 0 and cols[1] % 1024 == 0) else 512
    tm, tk = _pick(fp, 1408, LANE), _pick(t, 2048, 16)
    nm, nk = fp // tm, t // tk
    n0, n1 = cols[0] // tn, cols[1] // tn

    def body(d_ref, h_ref, o_ref, acc):
        kk = pl.program_id(3)

        @pl.when(kk == 0)
        def _():
            acc[...] = jnp.zeros_like(acc)

        acc[...] += _dot(d_ref[...], h_ref[...], TN)

        @pl.when(kk == nk - 1)
        def _():
            o_ref[...] = acc[...].astype(BF16)

    return _pcall(body, name=name, out_shape=jax.ShapeDtypeStruct((4, fp, (n1 - n0) * tn), BF16), grid=(4, nm, n1 - n0, nk),
                  in_specs=[pl.BlockSpec((None, tk, tm), lambda u, i, n, kk: (u // 2, kk, (u % 2) * nm + i)),
                            pl.BlockSpec((tk, tn), lambda u, i, n, kk: (kk, n0 + n))],
                  out_specs=pl.BlockSpec((None, tm, tn), lambda u, i, n, kk: (u, i, n)),
                  scratch=[pltpu.VMEM((tm, tn), F32)],
                  sem=("parallel", "parallel", "parallel", "arbitrary"), sides=sides)(dab, h)


def _ffn_dh(dab, wg, name, sides=()):
    t = dab.shape[1]
    fp = wg.shape[2]
    tm, tn, tk = _pick(t, 1024, 8), 1024, _pick(fp, 2816, LANE)
    nkk = fp // tk

    def body(d_ref, w_ref, o_ref, acc):
        u, kk = pl.program_id(2), pl.program_id(3)

        @pl.when((u == 0) & (kk == 0))
        def _():
            acc[...] = jnp.zeros_like(acc)

        acc[...] += _dot(d_ref[...], w_ref[...], NT)

        @pl.when((u == 3) & (kk == nkk - 1))
        def _():
            o_ref[...] = acc[...]

    return _pcall(body, name=name, out_shape=jax.ShapeDtypeStruct((t, D), F32), grid=(t // tm, D // tn, 4, nkk),
                  in_specs=[pl.BlockSpec((None, tm, tk), lambda i, j, u, kk: (u // 2, i, (u % 2) * nkk + kk)),
                            pl.BlockSpec((None, tn, tk), lambda i, j, u, kk: (u, j, kk))],
                  out_specs=pl.BlockSpec((tm, tn), lambda i, j, u, kk: (i, j)), scratch=[pltpu.VMEM((tm, tn), F32)],
                  sem=("parallel", "parallel", "arbitrary", "arbitrary"), sides=sides)(dab, wg)


def _split3(v):
    hi = v.astype(BF16)
    r1 = v - hi.astype(F32)
    mid = r1.astype(BF16)
    lo = (r1 - mid.astype(F32)).astype(BF16)
    return hi, mid, lo


def _tri_sum(tri, v):
    hi, mid, lo = _split3(v)
    return (_dot(tri, hi) + _dot(tri, mid)) + _dot(tri, lo)


def _forget_fwd(proj, b_pad, name):
    t = proj.shape[0]
    tb = _pick(t, 256, 8)
    col = D_PROJ // LANE

    def body(f_ref, b_ref, o_ref, carry):
        @pl.when(pl.program_id(0) == 0)
        def _():
            carry[...] = jnp.zeros_like(carry)

        z = f_ref[...] + b_ref[...]
        lf = jnp.minimum(z, 0.0) - jnp.log(1.0 + jnp.exp(-jnp.abs(z)))
        r = lax.broadcasted_iota(jnp.int32, (tb, tb), 0)
        cidx = lax.broadcasted_iota(jnp.int32, (tb, tb), 1)
        tri = (r >= cidx).astype(BF16)
        o_ref[...] = _tri_sum(tri, lf) + carry[...]
        carry[...] += jnp.sum(lf, axis=0, keepdims=True)

    return _pcall(body, name=name, out_shape=jax.ShapeDtypeStruct((t, LANE), F32), grid=(t // tb,),
                  in_specs=[pl.BlockSpec((tb, LANE), lambda i: (i, col)), pl.BlockSpec((1, LANE), lambda i: (0, 0))],
                  out_specs=pl.BlockSpec((tb, LANE), lambda i: (i, 0)), scratch=[pltpu.VMEM((1, LANE), F32)],
                  sem=("arbitrary",))(proj, b_pad)


def _forget_bwd(d_cum, proj, b_pad, name):
    t = proj.shape[0]
    tb = _pick(t, 256, 8)
    nb = t // tb
    col = D_PROJ // LANE

    def body(d_ref, f_ref, b_ref, o_ref, db_ref, carry):
        @pl.when(pl.program_id(0) == 0)
        def _():
            carry[...] = jnp.zeros_like(carry)
            db_ref[...] = jnp.zeros_like(db_ref)

        dc = d_ref[...]
        r = lax.broadcasted_iota(jnp.int32, (tb, tb), 0)
        cidx = lax.broadcasted_iota(jnp.int32, (tb, tb), 1)
        tri = (r <= cidx).astype(BF16)
        dlf = _tri_sum(tri, dc) + carry[...]
        carry[...] += jnp.sum(dc, axis=0, keepdims=True)
        z = f_ref[...] + b_ref[...]
        lane = lax.broadcasted_iota(jnp.int32, (tb, LANE), 1)
        dz = jnp.where(lane < N_HEADS, dlf * jax.nn.sigmoid(-z), 0.0)
        o_ref[...] = dz.astype(BF16)
        db_ref[...] += jnp.sum(dz, axis=0, keepdims=True)

    return _pcall(body, name=name, out_shape=[jax.ShapeDtypeStruct((t, LANE), BF16), jax.ShapeDtypeStruct((1, LANE), F32)],
                  grid=(nb,),
                  in_specs=[pl.BlockSpec((tb, LANE), lambda i: (nb - 1 - i, 0)),
                            pl.BlockSpec((tb, LANE), lambda i: (nb - 1 - i, col)),
                            pl.BlockSpec((1, LANE), lambda i: (0, 0))],
                  out_specs=[pl.BlockSpec((tb, LANE), lambda i: (nb - 1 - i, 0)), pl.BlockSpec((1, LANE), lambda i: (0, 0))],
                  scratch=[pltpu.VMEM((1, LANE), F32)], sem=("arbitrary",))(d_cum, proj, b_pad)


def _head_norm(v, g):
    r = lax.rsqrt(jnp.mean(v * v, axis=-1, keepdims=True) + EPS)
    return v * r, r


def _qkv_prep(proj, qg, kg, name):
    t = proj.shape[0]
    tm = _pick(t, 1024, 8)

    def body(q_ref, k_ref, v_ref, qg_ref, kg_ref, qo_ref, ko_ref, vo_ref):
        qo_ref[...] = (_head_norm(q_ref[...], None)[0] * qg_ref[...] * ATTN_SCALE).astype(BF16)
        ko_ref[...] = (_head_norm(k_ref[...], None)[0] * kg_ref[...]).astype(BF16)
        vo_ref[...] = v_ref[...].astype(BF16)

    def blk(off):
        return pl.BlockSpec((tm, HEAD_DIM), lambda i, h: (i, off + h))

    vec = pl.BlockSpec((1, HEAD_DIM), lambda i, h: (0, 0))
    return _pcall(body, name=name, out_shape=[jax.ShapeDtypeStruct((t, D_ATTN), BF16)] * 3, grid=(t // tm, N_HEADS),
                  in_specs=[blk(0), blk(N_HEADS), blk(2 * N_HEADS), vec, vec], out_specs=[blk(0)] * 3,
                  sem=("parallel", "parallel"))(proj, proj, proj, qg, kg)


def _qk_norm_bwd(dqn, dkn, proj, qg, kg, name):
    t = proj.shape[0]
    tm = _pick(t, 1024, 8)

    def one(d_ref, v_ref, g_ref, o_ref, dg_ref):
        xhat, r = _head_norm(v_ref[...], None)
        d = d_ref[...]
        dg_ref[...] += jnp.sum(d * xhat, axis=0, keepdims=True)
        dxh = d * g_ref[...]
        o_ref[...] = (r * (dxh - xhat * jnp.mean(dxh * xhat, axis=-1, keepdims=True))).astype(BF16)

    def body(dq_ref, dk_ref, q_ref, k_ref, qg_ref, kg_ref, qo_ref, ko_ref, dqg_ref, dkg_ref):
        @pl.when((pl.program_id(0) == 0) & (pl.program_id(1) == 0))
        def _():
            dqg_ref[...] = jnp.zeros_like(dqg_ref)
            dkg_ref[...] = jnp.zeros_like(dkg_ref)

        one(dq_ref, q_ref, qg_ref, qo_ref, dqg_ref)
        one(dk_ref, k_ref, kg_ref, ko_ref, dkg_ref)

    def blk(off):
        return pl.BlockSpec((tm, HEAD_DIM), lambda i, h: (i, off + h))

    vec = pl.BlockSpec((1, HEAD_DIM), lambda i, h: (0, 0))
    vshape = jax.ShapeDtypeStruct((1, HEAD_DIM), F32)
    return _pcall(body, name=name, out_shape=[jax.ShapeDtypeStruct((t, D_ATTN), BF16)] * 2 + [vshape, vshape],
                  grid=(t // tm, N_HEADS),
                  in_specs=[blk(0), blk(0), blk(0), blk(N_HEADS), vec, vec], out_specs=[blk(0), blk(0), vec, vec],
                  sem=("arbitrary", "arbitrary"))(dqn, dkn, proj, proj, qg, kg)


ATTN_SCALE = HEAD_DIM ** -0.5


def _logits(q, k, fq, fk, diag, r0, tq, tk):
    s = _dot(q, k, NT) + (fq - fk)
    if diag:
        r = r0 + lax.broadcasted_iota(jnp.int32, (tq, tk), 0)
        cidx = lax.broadcasted_iota(jnp.int32, (tq, tk), 1)
        s = jnp.where(r >= cidx, s, -jnp.inf)
    return s


def _head_cols(hp):
    return [(hh, slice(hh * HEAD_DIM, (hh + 1) * HEAD_DIM)) for hh in range(hp)]


def _tri(tt, n, by_row):
    if by_row:
        i = sum((tt >= k * (k + 1) // 2).astype(jnp.int32) for k in range(1, n))
        return i, tt - i * (i + 1) // 2
    j = sum((tt >= k * n - k * (k - 1) // 2).astype(jnp.int32) for k in range(1, n))
    return j + tt - (j * n - j * (j - 1) // 2), j


def _attn_fwd(q, k, v, fq, fk, name, sides=()):
    t = q.shape[0]
    tq = tk = _pick(t, ATTN_BLOCK, LANE)
    nk = t // tk

    hp, rc = ATTN_HEADS[0], tq // ATTN_ROW_CHUNKS[0]

    def body(q_ref, k_ref, v_ref, fq_ref, fk_ref, o_ref, lse_ref, m_s, l_s, acc):
        i, j = _tri(pl.program_id(1), nk, True)

        @pl.when(j == 0)
        def _():
            m_s[...] = jnp.full_like(m_s, -jnp.inf)
            l_s[...] = jnp.zeros_like(l_s)
            acc[...] = jnp.zeros_like(acc)

        def step(diag):
            for hh, cols in _head_cols(hp):
                for r0 in range(0, tq, rc):
                    rows = slice(r0, r0 + rc)
                    s = _logits(q_ref[rows, cols], k_ref[:, cols], fq_ref[hh, rows, :], fk_ref[hh], diag, r0, rc, tk)
                    m_new = jnp.maximum(m_s[hh, rows, :], jnp.max(s, axis=-1, keepdims=True))
                    alpha = jnp.exp(m_s[hh, rows, :] - m_new)
                    p = jnp.exp(s - m_new)
                    l_s[hh, rows, :] = alpha * l_s[hh, rows, :] + jnp.sum(p, axis=-1, keepdims=True)
                    acc[rows, cols] = alpha * acc[rows, cols] + _dot(p.astype(BF16), v_ref[:, cols])
                    m_s[hh, rows, :] = m_new

        @pl.when(j < i)
        def _():
            step(False)

        @pl.when(j == i)
        def _():
            step(True)

        @pl.when(j == i)
        def _():
            for hh, cols in _head_cols(hp):
                o_ref[:, cols] = (acc[:, cols] / l_s[hh]).astype(BF16)
                lse_ref[hh] = m_s[hh] + jnp.log(l_s[hh])

    qb = pl.BlockSpec((tq, hp * HEAD_DIM), lambda h, tt: (_tri(tt, nk, True)[0], h))
    kb = pl.BlockSpec((tk, hp * HEAD_DIM), lambda h, tt: (_tri(tt, nk, True)[1], h))
    col = pl.BlockSpec((hp, tq, 1), lambda h, tt: (h, _tri(tt, nk, True)[0], 0))
    return _pcall(body, name=name,
                  out_shape=[jax.ShapeDtypeStruct((t, D_ATTN), BF16), jax.ShapeDtypeStruct((N_HEADS, t, 1), F32)],
                  grid=(N_HEADS // hp, nk * (nk + 1) // 2),
                  in_specs=[qb, kb, kb, col, pl.BlockSpec((hp, 1, tk), lambda h, tt: (h, 0, _tri(tt, nk, True)[1]))],
                  out_specs=[qb, col],
                  scratch=[pltpu.VMEM((hp, tq, 1), F32), pltpu.VMEM((hp, tq, 1), F32), pltpu.VMEM((tq, hp * HEAD_DIM), F32)],
                  sem=("parallel", "arbitrary"), sides=sides)(q, k, v, fq, fk)


def _attn_bwd_q(q, k, v, o, do, lse, fq, fk, name, sides=()):
    t = q.shape[0]
    tq = tk = _pick(t, ATTN_BLOCK, LANE)
    nk = t // tk

    hp, rc = ATTN_HEADS[1], tq // ATTN_ROW_CHUNKS[1]

    def body(q_ref, k_ref, v_ref, o_ref, do_ref, lse_ref, fq_ref, fk_ref, dq_ref, dl_ref, dfq_ref, acc, dl_s, df_s):
        i, j = _tri(pl.program_id(1), nk, True)

        @pl.when(j == 0)
        def _():
            acc[...] = jnp.zeros_like(acc)
            df_s[...] = jnp.zeros_like(df_s)
            for hh, cols in _head_cols(hp):
                dl_s[hh] = jnp.sum(do_ref[:, cols].astype(F32) * o_ref[:, cols].astype(F32), axis=-1, keepdims=True)

        def step(diag):
            for hh, cols in _head_cols(hp):
                for r0 in range(0, tq, rc):
                    rows = slice(r0, r0 + rc)
                    s = _logits(q_ref[rows, cols], k_ref[:, cols], fq_ref[hh, rows, :], fk_ref[hh], diag, r0, rc, tk)
                    p = jnp.exp(s - lse_ref[hh, rows, :])
                    dp = _dot(do_ref[rows, cols], v_ref[:, cols], NT)
                    ds = p * (dp - dl_s[hh, rows, :])
                    df_s[hh, rows, :] += jnp.sum(ds, axis=-1, keepdims=True)
                    acc[rows, cols] += _dot(ds.astype(BF16), k_ref[:, cols])

        @pl.when(j < i)
        def _():
            step(False)

        @pl.when(j == i)
        def _():
            step(True)

        @pl.when(j == i)
        def _():
            dq_ref[...] = acc[...] * ATTN_SCALE
            dl_ref[...] = dl_s[...]
            dfq_ref[...] = df_s[...]

    qb = pl.BlockSpec((tq, hp * HEAD_DIM), lambda h, tt: (_tri(tt, nk, True)[0], h))
    kb = pl.BlockSpec((tk, hp * HEAD_DIM), lambda h, tt: (_tri(tt, nk, True)[1], h))
    col = pl.BlockSpec((hp, tq, 1), lambda h, tt: (h, _tri(tt, nk, True)[0], 0))
    cshape = jax.ShapeDtypeStruct((N_HEADS, t, 1), F32)
    return _pcall(body, name=name, out_shape=[jax.ShapeDtypeStruct((t, D_ATTN), F32), cshape, cshape],
                  grid=(N_HEADS // hp, nk * (nk + 1) // 2),
                  in_specs=[qb, kb, kb, qb, qb, col, col, pl.BlockSpec((hp, 1, tk), lambda h, tt: (h, 0, _tri(tt, nk, True)[1]))],
                  out_specs=[qb, col, col],
                  scratch=[pltpu.VMEM((tq, hp * HEAD_DIM), F32), pltpu.VMEM((hp, tq, 1), F32), pltpu.VMEM((hp, tq, 1), F32)],
                  sem=("parallel", "arbitrary"), sides=sides)(q, k, v, o, do, lse, fq, fk)


def _attn_bwd_kv(q, k, v, do, lse, delta, fq, fk, name, sides=()):
    t = q.shape[0]
    tq = tk = _pick(t, ATTN_BLOCK, LANE)
    nq = t // tq

    hp, rc = ATTN_HEADS[2], tq // ATTN_ROW_CHUNKS[2]

    def body(q_ref, k_ref, v_ref, do_ref, lse_ref, dl_ref, fq_ref, fk_ref, dk_ref, dv_ref, dfk_ref, dk_s, dv_s, df_s):
        i, j = _tri(pl.program_id(1), nq, False)

        @pl.when(i == j)
        def _():
            dk_s[...] = jnp.zeros_like(dk_s)
            dv_s[...] = jnp.zeros_like(dv_s)
            df_s[...] = jnp.zeros_like(df_s)

        def step(diag):
            for hh, cols in _head_cols(hp):
                for r0 in range(0, tq, rc):
                    rows = slice(r0, r0 + rc)
                    s = _logits(q_ref[rows, cols], k_ref[:, cols], fq_ref[hh, rows, :], fk_ref[hh], diag, r0, rc, tk)
                    p = jnp.exp(s - lse_ref[hh, rows, :])
                    dv_s[:, cols] += _dot(p.astype(BF16), do_ref[rows, cols], TN)
                    dp = _dot(do_ref[rows, cols], v_ref[:, cols], NT)
                    ds = p * (dp - dl_ref[hh, rows, :])
                    df_s[hh] -= jnp.sum(ds, axis=0, keepdims=True)
                    dk_s[:, cols] += _dot(ds.astype(BF16), q_ref[rows, cols], TN)

        @pl.when(i > j)
        def _():
            step(False)

        @pl.when(i == j)
        def _():
            step(True)

        @pl.when(i == nq - 1)
        def _():
            dk_ref[...] = dk_s[...]
            dv_ref[...] = dv_s[...].astype(BF16)
            dfk_ref[...] = df_s[...]

    qb = pl.BlockSpec((tq, hp * HEAD_DIM), lambda h, tt: (_tri(tt, nq, False)[0], h))
    kb = pl.BlockSpec((tk, hp * HEAD_DIM), lambda h, tt: (_tri(tt, nq, False)[1], h))
    col = pl.BlockSpec((hp, tq, 1), lambda h, tt: (h, _tri(tt, nq, False)[0], 0))
    row = pl.BlockSpec((hp, 1, tk), lambda h, tt: (h, 0, _tri(tt, nq, False)[1]))
    return _pcall(body, name=name,
                  out_shape=[jax.ShapeDtypeStruct((t, D_ATTN), F32), jax.ShapeDtypeStruct((t, D_ATTN), BF16),
                             jax.ShapeDtypeStruct((N_HEADS, 1, t), F32)],
                  grid=(N_HEADS // hp, nq * (nq + 1) // 2),
                  in_specs=[qb, kb, kb, qb, col, col, col, row], out_specs=[kb, kb, row],
                  scratch=[pltpu.VMEM((tk, hp * HEAD_DIM), F32), pltpu.VMEM((tk, hp * HEAD_DIM), F32), pltpu.VMEM((hp, 1, tk), F32)],
                  sem=("parallel", "arbitrary"), sides=sides)(q, k, v, do, lse, delta, fq, fk)


def _window_sum(v, w, back):
    n = v.shape[0]
    k = 1
    while k < w:
        v = v + pltpu.roll(v, k if back else n - k, axis=0)
        k *= 2
    return v


def _pool_fwd(proj, pw, ps, name):
    t = proj.shape[0]
    tm = _pick(t, POOL_BLOCK, HALO)
    col = 3 * D_ATTN // D_POOL

    def body(u_ref, prev_ref, pw_ref, ps_ref, pooled_ref, out_ref):
        i = pl.program_id(0)
        prev = jnp.where(i > 0, prev_ref[...], 0.0)
        ext = jnp.concatenate([prev, u_ref[...]], axis=0)
        pos = i * tm + lax.broadcasted_iota(jnp.int32, (tm, 1), 0)
        for g, w in enumerate(POOL_WINDOWS):
            cols = slice(g * GROUP_DIM, (g + 1) * GROUP_DIM)
            xg = ext[:, cols]
            sw = _window_sum(xg, w, True)[HALO:, :]
            cnt = jnp.minimum(pos + 1, w).astype(F32)
            pooled = (sw / cnt - xg[HALO:, :]).astype(BF16)
            pooled_ref[:, cols] = pooled
            out_ref[:, cols] = (_dot(pooled, pw_ref[g]) * ps_ref[:, cols]).astype(BF16)

    row = pl.BlockSpec((tm, D_POOL), lambda i: (i, 0))
    return _pcall(body, name=name, out_shape=[jax.ShapeDtypeStruct((t, D_POOL), BF16)] * 2, grid=(t // tm,),
                  in_specs=[pl.BlockSpec((tm, D_POOL), lambda i: (i, col)),
                            pl.BlockSpec((HALO, D_POOL), lambda i: (jnp.maximum(i * (tm // HALO) - 1, 0), col)),
                            pl.BlockSpec((len(POOL_WINDOWS), GROUP_DIM, GROUP_DIM), lambda i: (0, 0, 0)),
                            pl.BlockSpec((1, D_POOL), lambda i: (0, 0))],
                  out_specs=[row, row], sem=("parallel",))(proj, proj, pw, ps)


def _pool_bwd(dout, pooled, pw, ps, name):
    t = pooled.shape[0]
    tm = _pick(t, POOL_BLOCK, HALO)
    nb = t // tm
    ng = len(POOL_WINDOWS)

    def body(d_ref, nxt_ref, p_ref, pw_ref, ps_ref, du_ref, dpw_ref, dps_ref):
        i = pl.program_id(0)

        @pl.when(i == 0)
        def _():
            dpw_ref[...] = jnp.zeros_like(dpw_ref)
            dps_ref[...] = jnp.zeros_like(dps_ref)

        nxt = jnp.where(i < nb - 1, nxt_ref[...].astype(F32), 0.0)
        ext = jnp.concatenate([d_ref[...].astype(F32), nxt], axis=0)
        pos = i * tm + lax.broadcasted_iota(jnp.int32, (tm + HALO, 1), 0)
        for g, w in enumerate(POOL_WINDOWS):
            cols = slice(g * GROUP_DIM, (g + 1) * GROUP_DIM)
            pooled_g = p_ref[:, cols]
            dg = ext[:, cols]
            pm = _dot(pooled_g, pw_ref[g])
            dps_ref[:, cols] += jnp.sum(dg[:tm, :] * pm, axis=0, keepdims=True)
            dpm = (dg * ps_ref[:, cols]).astype(BF16)
            dpw_ref[g] += _dot(pooled_g, dpm[:tm, :], TN)
            dpooled = _dot(dpm, pw_ref[g], NT)
            cnt = jnp.minimum(pos + 1, w).astype(F32)
            fwd = _window_sum(dpooled / cnt, w, False)
            du_ref[:, cols] = (fwd[:tm, :] - dpooled[:tm, :]).astype(BF16)

    row = pl.BlockSpec((tm, D_POOL), lambda i: (i, 0))
    return _pcall(body, name=name,
                  out_shape=[jax.ShapeDtypeStruct((t, D_POOL), BF16), jax.ShapeDtypeStruct((ng, GROUP_DIM, GROUP_DIM), F32),
                             jax.ShapeDtypeStruct((1, D_POOL), F32)],
                  grid=(nb,),
                  in_specs=[pl.BlockSpec((tm, D_POOL), lambda i: (i, 1)),
                            pl.BlockSpec((HALO, D_POOL), lambda i: (jnp.minimum((i + 1) * (tm // HALO), t // HALO - 1), 1)),
                            row, pl.BlockSpec((ng, GROUP_DIM, GROUP_DIM), lambda i: (0, 0, 0)),
                            pl.BlockSpec((1, D_POOL), lambda i: (0, 0))],
                  out_specs=[row, pl.BlockSpec((ng, GROUP_DIM, GROUP_DIM), lambda i: (0, 0, 0)),
                             pl.BlockSpec((1, D_POOL), lambda i: (0, 0))],
                  sem=("arbitrary",))(dout, dout, pooled, pw, ps)


def _pad_cols(w, n):
    return jnp.pad(w, ((0, 0), (0, n - w.shape[1])))


def kernel(x, c, w_ada, b_ada, ffn1_norm_g, ffn1_w_in, ffn1_w_out, mix_norm_g, w_in, b_forget, q_norm_g, k_norm_g, pool_w, pool_scale, w_out, ffn2_norm_g, ffn2_w_in, ffn2_w_out, final_norm_g, loss_target, m_w_ada, m_b_ada, m_ffn1_norm_g, m_ffn1_w_in, m_ffn1_w_out, m_mix_norm_g, m_w_in, m_b_forget, m_q_norm_g, m_k_norm_g, m_pool_w, m_pool_scale, m_w_out, m_ffn2_norm_g, m_ffn2_w_in, m_ffn2_w_out, m_final_norm_g, v_w_ada, v_b_ada, v_ffn1_norm_g, v_ffn1_w_in, v_ffn1_w_out, v_mix_norm_g, v_w_in, v_b_forget, v_q_norm_g, v_k_norm_g, v_pool_w, v_pool_scale, v_w_out, v_ffn2_norm_g, v_ffn2_w_in, v_ffn2_w_out, v_final_norm_g):
    ax, ay, ac = _coords()
    chip = 2 * ax + ay
    me = 2 * chip + ac
    chip_core = jnp.stack([chip, ac]).astype(jnp.int32)
    core_arr = jnp.reshape(ac, (1,)).astype(jnp.int32)

    t = x.shape[1]
    xs = x.reshape(t, D)
    tgt = loss_target.reshape(t, D)
    hu = ffn1_w_out.shape[1]
    hup = -(-hu // LANE) * LANE
    ws_in = w_in.shape[2]
    ws_in_pad = -(-ws_in // LANE) * LANE
    n_ada = w_ada.shape[2]

    def ffn_in_shard(w):
        w = w[0].astype(BF16)
        return jnp.concatenate([_pad_cols(w[:, :hu], hup), _pad_cols(w[:, hu:], hup)], axis=1)

    def ffn_out_shard(w):
        return jnp.pad(w[0].astype(BF16), ((0, hup - hu), (0, 0)))

    shards1 = [ffn_in_shard(ffn1_w_in), ffn_out_shard(ffn1_w_out)]
    shards_mix = [_pad_cols(w_in[0].astype(BF16), ws_in_pad), pool_w[0].astype(BF16).reshape(GROUP_DIM, GROUP_DIM),
                  w_out[0].astype(BF16)]
    shards2 = [ffn_in_shard(ffn2_w_in), ffn_out_shard(ffn2_w_out)]

    def add_siblings(parts, recv, tag):
        return [_add_sibling(p, r, core_arr, f"{tag}_add_sibling_{k}") for k, (p, r) in enumerate(zip(parts, recv))]

    def add_chips(halves, recv, tag):
        return [_add_chips(hh, [(r, 0), (r, 1), (r, 2)], chip_core, f"{tag}_add_chips_{k}") for k, (hh, r) in enumerate(zip(halves, recv))]

    g_in1, = _exchange(_gather(shards1[:1]), "gather_ffn1_in")

    c_all = _allgather_small(c.reshape(8, D // 8), True, "gather_c").reshape(8, D)
    c16 = jnp.pad(c_all, ((0, 8), (0, 0)))
    b_ada_mine = lax.dynamic_slice(b_ada, (0, chip * n_ada), (1, n_ada))
    act16, mod16 = _ada_fwd(c16, w_ada[0], b_ada_mine, "ada_fwd")
    mod_all = _allgather_small(mod16[:8], False, "gather_mod")
    mod = lax.dynamic_index_in_dim(mod_all, me, axis=1, keepdims=False).reshape(N_MOD, 1, D)
    sh1, sc1, gt1, sh2, sc2, gt2, sh3, sc3, gt3 = [mod[k] for k in range(N_MOD)]

    gate1, gate3 = 0.5 * gt1, 0.5 * gt3
    h1 = _norm_mod(xs, ffn1_norm_g, sh1, sc1, "ffn1_norm")
    job = _gather(shards1[1:])
    a1, b1, u1 = _ffn_up(h1, g_in1, "ffn1_up", sides=[job])
    wg_out1 = job.results[0].reshape(4 * hup, D)
    job = _gather(shards_mix)
    y1, x1 = _mm_resid(u1, wg_out1, xs, gate1, "ffn1_down", sides=[job])
    g_win, g_pw, g_wout = job.results
    h2 = _norm_mod(x1, mix_norm_g, sh2, sc2, "mix_norm")
    w_full = jnp.concatenate([g_win[k, :, :ws_in] for k in range(4)], axis=1)
    nf = 3 * D_ATTN
    w_all = jnp.concatenate([w_full[:, :nf], w_full[:, nf + N_HEADS:], w_full[:, nf:nf + N_HEADS],
                             jnp.zeros((D, LANE - N_HEADS), BF16)], axis=1)
    pw_full = g_pw.reshape(4, 4, GROUP_DIM // 4, GROUP_DIM).transpose(1, 0, 2, 3).reshape(4, GROUP_DIM, GROUP_DIM)
    wo_full = g_wout.reshape(4 * g_wout.shape[1], D)

    proj = _mm(h2, w_all, F32, "mix_proj")
    b_pad = jnp.pad(b_forget, ((0, 0), (0, LANE - N_HEADS)))
    cum = _forget_fwd(proj, b_pad, "forget_fwd")
    cum_t = cum[:, :N_HEADS].T
    fq, fk = cum_t.reshape(N_HEADS, t, 1), cum_t.reshape(N_HEADS, 1, t)
    qn, kn, vb = _qkv_prep(proj, q_norm_g, k_norm_g, "qkv_prep")
    job = _gather(shards2[:1])
    attn, lse = _attn_fwd(qn, kn, vb, fq, fk, "attn_fwd", sides=[job])
    g_in2, = job.results
    pooled, pool_out = _pool_fwd(proj, pw_full, pool_scale, "pool_fwd")
    cat = jnp.concatenate([attn, pool_out], axis=1)
    y_mix, x2 = _mm_resid(cat, wo_full, x1, gt2, "mix_out")

    h3 = _norm_mod(x2, ffn2_norm_g, sh3, sc3, "ffn2_norm")
    job = _gather(shards2[1:])
    a3, b3, u3 = _ffn_up(h3, g_in2, "ffn2_up", sides=[job])
    g_out2, = job.results
    wg_out2 = g_out2.reshape(4 * hup, D)
    y3, x3 = _mm_resid(u3, wg_out2, x2, gate3, "ffn2_down")

    loss_part, dx3, d_final_g = _final_loss(x3, final_norm_g.reshape(1, D), tgt, "final_loss")
    loss = lax.psum(loss_part[0, 0], ("x", "y", "c"))

    all_cols = (0, D)
    dy3, dgt3 = _gate_bwd(dx3, y3, gate3, "ffn2_gate_bwd")
    dab3 = _ffn_dab(dy3, wg_out2, a3, b3, "ffn2_dab")
    parts2 = [_ffn_dwin(h3, dab3, all_cols, "ffn2_dwin"), _mm_tn(u3, dy3, BF16, hup, "ffn2_dwout").reshape(4, hup, D)]
    rs = _reduce_siblings(parts2)
    dh3 = _ffn_dh(dab3, g_in2, "ffn2_dh", sides=[rs])
    half_in2, half_out2 = add_siblings(parts2, rs.results, "ffn2")
    dx2, dsh3, dsc3, d_ng3 = _norm_mod_bwd(dh3, x2, dx3, ffn2_norm_g, sc3, "ffn2_norm_bwd")

    dz, dgt2 = _gate_bwd(dx2, y_mix, gt2, "mix_gate_bwd")
    dcat = _mm_nt(dz, wo_full, BF16, "mix_dcat")
    d_wo = _mm_tn(cat, dz, BF16, 1024, "mix_dwout")
    du_pool, d_pw, d_ps = _pool_bwd(dcat, pooled, pw_full, pool_scale, "pool_bwd")
    rs = _reduce_chips([half_in2])
    dqn, delta, dfq = _attn_bwd_q(qn, kn, vb, attn, dcat, lse, fq, fk, "attn_bwd_q", sides=[rs])
    mine_in2, = add_chips([half_in2], rs.results, "ffn2_in")
    rs, rs3 = _reduce_chips([half_out2]), _share_siblings([mine_in2])
    dkn, dv, dfk = _attn_bwd_kv(qn, kn, vb, dcat, lse, delta, fq, fk, "attn_bwd_kv", sides=[rs, rs3])
    r_in2, = rs3.results
    mine_out2, = add_chips([half_out2], rs.results, "ffn2_out")
    d_cum = jnp.pad((dfq.reshape(N_HEADS, t) + dfk.reshape(N_HEADS, t)).T, ((0, 0), (0, LANE - N_HEADS)))
    dfl, d_bf = _forget_bwd(d_cum, proj, b_pad, "forget_bwd")
    dq, dk, d_qg, d_kg = _qk_norm_bwd(dqn, dkn, proj, q_norm_g, k_norm_g, "qk_norm_bwd")
    dproj = jnp.concatenate([dq, dk, dv, du_pool, dfl], axis=1)
    rs3 = _share_siblings([mine_out2])
    d_wall = _mm_tn(dproj, h2, F32, 1408, "mix_dwin", sides=[rs3])
    r_out2, = rs3.results
    d_wfull = jnp.concatenate([d_wall[:nf], d_wall[D_PROJ:D_PROJ + N_HEADS], d_wall[nf:D_PROJ]], axis=0)
    p_win = jnp.stack([jnp.pad(d_wfull[k * ws_in:(k + 1) * ws_in], ((0, ws_in_pad - ws_in), (0, 0))) for k in range(4)]).astype(BF16)
    p_pw = d_pw.reshape(4, 4, GROUP_DIM // 4, GROUP_DIM).transpose(1, 0, 2, 3).reshape(4, GROUP_DIM, GROUP_DIM).astype(BF16)
    parts_mix = [p_win, p_pw, d_wo.reshape(4, D // 4, D)]
    rs = _reduce_siblings(parts_mix)
    dh2 = _mm_nt(dproj, w_all, F32, "mix_dh", sides=[rs])
    halves_mix = add_siblings(parts_mix, rs.results, "mix")
    dx1, dsh2, dsc2, d_ng2 = _norm_mod_bwd(dh2, x1, dx2, mix_norm_g, sc2, "mix_norm_bwd")

    dy1, dgt1 = _gate_bwd(dx1, y1, gate1, "ffn1_gate_bwd")
    d_out1 = _mm_tn(u1, dy1, BF16, hup, "ffn1_dwout").reshape(4, hup, D)
    rs, rs1 = _reduce_chips(halves_mix), _reduce_siblings([d_out1])
    dab1 = _ffn_dab(dy1, wg_out1, a1, b1, "ffn1_dab", sides=[rs, rs1])
    mine_mix = add_chips(halves_mix, rs.results, "mix")
    half_out1, = add_siblings([d_out1], rs1.results, "ffn1_out")
    rs, rs3 = _reduce_chips([half_out1]), _share_siblings(mine_mix)
    d_in1_lo = _ffn_dwin(h1, dab1, (0, LAST_PIECE_FROM), "ffn1_dwin_lo", sides=[rs, rs3])
    r_win, r_pw, r_wo = rs3.results
    mine_out1, = add_chips([half_out1], rs.results, "ffn1_out")
    rs, rs3 = _reduce_siblings([d_in1_lo]), _share_siblings([mine_out1])
    d_in1_hi = _ffn_dwin(h1, dab1, (LAST_PIECE_FROM, D), "ffn1_dwin_hi", sides=[rs, rs3])
    r_out1, = rs3.results
    half_lo, = add_siblings([d_in1_lo], rs.results, "ffn1_lo")
    rs, rs1 = _reduce_chips([half_lo]), _reduce_siblings([d_in1_hi])
    dh1 = _ffn_dh(dab1, g_in1, "ffn1_dh", sides=[rs, rs1])
    mine_lo, = add_chips([half_lo], rs.results, "ffn1_lo")
    half_hi, = add_siblings([d_in1_hi], rs1.results, "ffn1_hi")
    dx0, dsh1, dsc1, d_ng1 = _norm_mod_bwd(dh1, xs, dx1, ffn1_norm_g, sc1, "ffn1_norm_bwd")
    grad_x = dx0.reshape(1, t, D)
    dgt1, dgt3 = 0.5 * dgt1, 0.5 * dgt3
    mine_hi, = add_chips([half_hi], _exchange(_reduce_chips([half_hi]), "ffn1_hi_reduce_chips"), "ffn1_hi")
    r_lo, r_hi = _exchange(_share_siblings([mine_lo, mine_hi]), "ffn1_share_siblings")
    r_in1 = jnp.concatenate([r_lo, r_hi], axis=1)
    grads = {
        "ffn1_w_in": jnp.concatenate([r_in1[:hu], r_in1[hup:hup + hu]], axis=0),
        "ffn1_w_out": r_out1[:hu],
        "w_in": r_win[:ws_in],
        "pool_w": r_pw,
        "w_out": r_wo,
        "ffn2_w_in": jnp.concatenate([r_in2[:hu], r_in2[hup:hup + hu]], axis=0),
        "ffn2_w_out": r_out2[:hu],
    }
    hidden_in_rows = ("ffn1_w_in", "ffn2_w_in", "w_in")

    dmod = jnp.concatenate([dsh1, dsc1, dgt1, dsh2, dsc2, dgt2, dsh3, dsc3, dgt3], axis=1)
    small_names = ["b_ada", "ffn1_norm_g", "mix_norm_g", "b_forget", "q_norm_g", "k_norm_g", "pool_scale", "ffn2_norm_g",
                   "final_norm_g"]
    small_grads = [dmod, d_ng1, d_ng2, d_bf[:, :N_HEADS], d_qg, d_kg, d_ps, d_ng3, d_final_g]
    small_w = [b_ada, ffn1_norm_g, mix_norm_g, b_forget, q_norm_g, k_norm_g, pool_scale, ffn2_norm_g, final_norm_g.reshape(1, D)]
    small_m = [m_b_ada, m_ffn1_norm_g, m_mix_norm_g, m_b_forget, m_q_norm_g, m_k_norm_g, m_pool_scale, m_ffn2_norm_g,
               m_final_norm_g.reshape(1, D)]
    small_v = [v_b_ada, v_ffn1_norm_g, v_mix_norm_g, v_b_forget, v_q_norm_g, v_k_norm_g, v_pool_scale, v_ffn2_norm_g,
               v_final_norm_g.reshape(1, D)]
    sizes = [g.shape[1] for g in small_grads]
    n_small = sum(sizes)
    n_pack = -(-n_small // (8 * LANE)) * (8 * LANE)

    def pack(vs, fill):
        flat = jnp.concatenate([v.reshape(1, -1) for v in vs], axis=1)
        return jnp.pad(flat, ((0, 0), (0, n_pack - n_small)), constant_values=fill).reshape(8, n_pack // 8)

    g8 = _allgather_small(pack(small_grads, 0.0), True, "gather_small_grads")
    gs, ds, ms, vs = _adamw_small(pack(small_w, 0.0), g8, pack(small_m, 0.0), pack(small_v, 1.0), "adamw_small")

    def unpack(p):
        flat = p.reshape(1, n_pack)
        out, off = {}, 0
        for nme, sz in zip(small_names, sizes):
            out[nme] = flat[:, off:off + sz]
            off += sz
        return out

    small = [unpack(p) for p in (gs, ds, ms, vs)]
    for dct in small:
        dct["final_norm_g"] = dct["final_norm_g"].reshape(D)

    dmod_all = g8.reshape(8, n_pack)[:, :N_MOD * D]
    dmod_mine = lax.dynamic_slice(dmod_all, (0, chip * n_ada), (8, n_ada))
    grads["w_ada"] = _ada_bwd(act16, jnp.pad(dmod_mine, ((0, 8), (0, 0))), "ada_bwd")

    big = {"w_ada": (w_ada, m_w_ada, v_w_ada), "ffn1_w_in": (ffn1_w_in, m_ffn1_w_in, v_ffn1_w_in),
           "ffn1_w_out": (ffn1_w_out, m_ffn1_w_out, v_ffn1_w_out), "w_in": (w_in, m_w_in, v_w_in),
           "pool_w": (pool_w, m_pool_w, v_pool_w), "w_out": (w_out, m_w_out, v_w_out),
           "ffn2_w_in": (ffn2_w_in, m_ffn2_w_in, v_ffn2_w_in), "ffn2_w_out": (ffn2_w_out, m_ffn2_w_out, v_ffn2_w_out)}
    res = {}

    for nme, (w, m, v) in big.items():
        shp = w.shape
        g2 = grads[nme]
        if nme in hidden_in_rows:
            rows = g2.shape[0]
            more = -rows % 8

            def fit(a):
                return jnp.pad(a, ((0, more), (0, 0))) if more else a

            outs = _adamw(fit(w[0].T), fit(g2), fit(m[0].T), fit(v[0].T), f"adamw_{nme}")
            res[nme] = tuple(o[:rows].T.reshape(shp) for o in (g2, *outs))
            continue
        two = g2.shape
        d, mo, vo = _adamw(w.reshape(two), g2, m.reshape(two), v.reshape(two), f"adamw_{nme}")
        res[nme] = (g2.reshape(shp), d.reshape(shp), mo.reshape(shp), vo.reshape(shp))
    for nme in small_names:
        res[nme] = tuple(dct[nme] for dct in small)

    order = ["w_ada", "b_ada", "ffn1_norm_g", "ffn1_w_in", "ffn1_w_out", "mix_norm_g", "w_in", "b_forget", "q_norm_g",
             "k_norm_g", "pool_w", "pool_scale", "w_out", "ffn2_norm_g", "ffn2_w_in", "ffn2_w_out", "final_norm_g"]
    return (loss, grad_x, *[res[n][0] for n in order], *[res[n][1] for n in order], *[res[n][2] for n in order],
            *[res[n][3] for n in order])
```

```python
import functools

import jax
import jax.numpy as jnp
from jax import lax
from jax.experimental import pallas as pl
from jax.experimental.pallas import tpu as pltpu

F32 = jnp.float32
BF16 = jnp.bfloat16

D = 2048
N_HEADS = 8
HEAD_DIM = 128
D_ATTN = 1024
D_POOL = 1024
POOL_WINDOWS = (2, 4, 8, 16)
GROUP_DIM = 256
HALO = 16
N_MOD = 9
EPS = 1e-6
D_PROJ = 3 * D_ATTN + D_POOL
D_PROJ_PAD = D_PROJ + 128
LANE = 128
ATTN_BLOCK = 512
POOL_BLOCK = 512
FFN_ROW_CHUNKS = 4
ATTN_HEADS = (2, 2, 2)
ATTN_ROW_CHUNKS = (2, 2, 2)

ADAM_LR = 0.001
ADAM_B1 = 0.9
ADAM_B2 = 0.999
ADAM_EPS = 1e-08
ADAM_WD = 0.01
ADAM_STEP = 10

VMEM_LIMIT_V7X = 56 * 1024 * 1024
MESH_ID = pl.DeviceIdType.MESH
ANY = pl.BlockSpec(memory_space=pl.ANY)
VMEM = pl.BlockSpec(memory_space=pltpu.VMEM)

NT = (((1,), (1,)), ((), ()))
TN = (((0,), (0,)), ((), ()))


class _Side:
    def __init__(self, ins, outs, nsem, start, wait, alias=False, mids=()):
        self.ins, self.outs, self.nsem, self.start, self.wait, self.alias = list(ins), list(outs), nsem, start, wait, alias
        self.mids = list(mids)
        self.results = None


def _pcall(body, *, name, out_shape, grid=None, in_specs=None, out_specs=None, scratch=(), sem=None, prefetch=0, sides=()):
    sides = list(sides)
    single = not isinstance(out_shape, (list, tuple))
    shapes = [out_shape] if single else list(out_shape)
    in_specs = list(in_specs)
    ospecs = [out_specs] if single else list(out_specs)
    scratch = list(scratch)
    n_in, n_out, n_scr = len(in_specs), len(shapes), len(scratch)
    assert not (sides and prefetch)
    aliases = {}
    for sd in sides:
        if sd.alias:
            for k in range(len(sd.outs)):
                aliases[len(in_specs) + k] = len(shapes) + k
        in_specs += [ANY] * len(sd.ins)
        shapes += sd.outs
        ospecs += [ANY] * len(sd.outs)
        scratch += [pltpu.SemaphoreType.DMA((sd.nsem,)), pltpu.SemaphoreType.DMA((sd.nsem,))]

    def wrapped(*refs):
        ins, outs, scr = refs[:len(in_specs)], refs[len(in_specs):len(in_specs) + len(shapes)], refs[len(in_specs) + len(shapes):]
        step, steps = 0, 1
        for ax, g in enumerate(grid or ()):
            step, steps = step * g + pl.program_id(ax), steps * g

        def at(when, fn):
            if grid:
                pl.when(step == when)(fn)
            else:
                fn()

        i0, o0 = n_in, n_out
        for k, sd in enumerate(sides):
            refs_k = (ins[i0:i0 + len(sd.ins)], outs[o0:o0 + len(sd.outs)], scr[n_scr + 2 * k], scr[n_scr + 2 * k + 1])
            at(0, functools.partial(sd.start, *refs_k))
            for frac, fn in sd.mids:
                at(min(steps - 1, int(frac * steps)), functools.partial(fn, *refs_k))
            i0, o0 = i0 + len(sd.ins), o0 + len(sd.outs)
        body(*ins[:n_in], *outs[:n_out], *scr[:n_scr])
        i0, o0 = n_in, n_out
        for k, sd in enumerate(sides):
            refs_k = (ins[i0:i0 + len(sd.ins)], outs[o0:o0 + len(sd.outs)], scr[n_scr + 2 * k], scr[n_scr + 2 * k + 1])
            at(steps - 1, functools.partial(sd.wait, *refs_k))
            i0, o0 = i0 + len(sd.ins), o0 + len(sd.outs)

    params = dict(vmem_limit_bytes=VMEM_LIMIT_V7X)
    if sides and grid:
        params["dimension_semantics"] = ("arbitrary",) * len(grid)
    elif sem is not None:
        params["dimension_semantics"] = sem
    kw = dict(name=name, out_shape=shapes if (sides or not single) else shapes[0], compiler_params=pltpu.CompilerParams(**params))
    if aliases:
        kw["input_output_aliases"] = aliases
    final_ospecs = ospecs if (sides or not single) else ospecs[0]
    if prefetch:
        kw["grid_spec"] = pltpu.PrefetchScalarGridSpec(
            num_scalar_prefetch=prefetch, grid=grid, in_specs=in_specs, out_specs=final_ospecs, scratch_shapes=scratch)
    else:
        if grid is not None:
            kw["grid"] = grid
        kw["in_specs"] = in_specs
        kw["out_specs"] = final_ospecs
        kw["scratch_shapes"] = scratch
    call = pl.pallas_call(wrapped if sides else body, **kw)
    if not sides:
        return call

    def run(*operands):
        res = list(call(*operands, *[a for sd in sides for a in sd.ins]))
        o0 = n_out
        for sd in sides:
            sd.results = res[o0:o0 + len(sd.outs)]
            o0 += len(sd.outs)
        return res[0] if single else res[:n_out]

    return run


def _pick(n, cap, mult):
    best = None
    for d in range(mult, min(n, cap) + 1, mult):
        if n % d == 0:
            best = d
    assert best is not None, (n, cap, mult)
    return best


def _coords():
    return lax.axis_index("x"), lax.axis_index("y"), lax.axis_index("c")


def _dot(a, b, dims=None):
    if dims is None:
        return jnp.dot(a, b, preferred_element_type=F32)
    return lax.dot_general(a, b, dims, preferred_element_type=F32)


def _remote(src, dst, ssem, rsem, dev):
    return pltpu.make_async_remote_copy(src_ref=src, dst_ref=dst, send_sem=ssem, recv_sem=rsem,
                                        device_id=dev, device_id_type=MESH_ID)


def _allgather_small(v, whole_mesh, name):
    masks = list(range(1, 8)) if whole_mesh else [4, 2, 6]
    nslot = 8 if whole_mesh else 4

    def slot(px, py, pc):
        return 4 * px + 2 * py + pc if whole_mesh else 2 * px + py

    def body(v_ref, out_ref, ssem, rsem, lsem):
        x, y, c = _coords()
        mine = slot(x, y, c)
        peers = [(jnp.bitwise_xor(x, (m >> 2) & 1), jnp.bitwise_xor(y, (m >> 1) & 1), jnp.bitwise_xor(c, m & 1))
                 for m in masks]
        loc = pltpu.make_async_copy(v_ref, out_ref.at[mine], lsem)
        loc.start()
        sends = [_remote(v_ref, out_ref.at[mine], ssem.at[k], rsem.at[k], p) for k, p in enumerate(peers)]
        for cp in sends:
            cp.start()
        for k, p in enumerate(peers):
            _remote(v_ref, out_ref.at[slot(*p)], ssem.at[k], rsem.at[k], p).wait_recv()
        for cp in sends:
            cp.wait_send()
        loc.wait()

    return _pcall(body, name=name, out_shape=jax.ShapeDtypeStruct((nslot,) + v.shape, v.dtype),
                  in_specs=[VMEM], out_specs=VMEM,
                  scratch=[pltpu.SemaphoreType.DMA((len(masks),)), pltpu.SemaphoreType.DMA((len(masks),)),
                           pltpu.SemaphoreType.DMA(())])(v)


def _other_chips(x, y):
    return [(1 - x, y), (x, 1 - y), (1 - x, 1 - y)]


def _half_rows(shard_rows, core):
    h = shard_rows // 2
    return pl.ds(pl.multiple_of(core * h, 16), h)


def _later(src, dst, ssem, rsem, dev):
    return functools.partial(_remote, src, dst, ssem, rsem, dev)


def _side_from(ins, outs, nsem, pairs_of, alias=False):
    def start(*refs):
        for send, _ in pairs_of(*refs):
            send().start()

    def wait(*refs):
        pairs = pairs_of(*refs)
        for _, recv in pairs:
            recv().wait_recv()
        for send, _ in pairs:
            send().wait_send()

    return _Side(ins, outs, nsem, start, wait, alias)


GATHER_STAGES = (0.55, 0.8)
LAST_PIECE_FROM = 1536


def _gather(ws):
    def copies(w_refs, g_refs, ssem, rsem):
        x, y, c = _coords()
        me, sib = 2 * x + y, (x, y, 1 - c)
        across_x, across_y, far = 2 * (1 - x) + y, 2 * x + (1 - y), 2 * (1 - x) + (1 - y)
        to_x, to_y = (1 - x, y, c), (x, 1 - y, c)
        out = []
        for a, w in enumerate(ws):
            h = w.shape[0] // 2
            rows, theirs = _half_rows(w.shape[0], c), _half_rows(w.shape[0], 1 - c)
            first = pl.ds(pl.multiple_of(c * h, 16), h // 2)
            second = pl.ds(pl.multiple_of(c * h + h // 2, 16), h // 2)
            g = g_refs[a]

            def pair(k, src, dst, got, dev):
                sems = (ssem.at[8 * a + k], rsem.at[8 * a + k], dev)
                return _later(src, dst, *sems), _later(got, got, *sems)

            out.append(dict(
                x1=pair(0, w_refs[a].at[rows], g.at[me, rows], g.at[across_x, rows], to_x),
                y1=pair(1, w_refs[a].at[rows], g.at[me, rows], g.at[across_y, rows], to_y),
                x2=pair(2, g.at[across_y, first], g.at[across_y, first], g.at[far, first], to_x),
                y2=pair(3, g.at[across_x, second], g.at[across_x, second], g.at[far, second], to_y),
                sx=pair(4, g.at[across_x, rows], g.at[across_x, rows], g.at[across_x, theirs], sib),
                sy=pair(5, g.at[across_y, rows], g.at[across_y, rows], g.at[across_y, theirs], sib),
                sf=pair(6, g.at[far, rows], g.at[far, rows], g.at[far, theirs], sib),
                so=pair(7, w_refs[a], g.at[me], g.at[me], sib)))
        return out

    def start(*refs):
        for cp in copies(*refs):
            cp["x1"][0]().start()
            cp["y1"][0]().start()

    def stage2(*refs):
        for cp in copies(*refs):
            cp["x1"][1]().wait_recv()
            cp["y1"][1]().wait_recv()
            for k in ("x2", "y2", "sx", "sy", "so"):
                cp[k][0]().start()

    def stage3(*refs):
        for cp in copies(*refs):
            cp["x2"][1]().wait_recv()
            cp["y2"][1]().wait_recv()
            cp["sf"][0]().start()

    def wait(*refs):
        cps = copies(*refs)
        for cp in cps:
            for k in ("sx", "sy", "sf", "so"):
                cp[k][1]().wait_recv()
        for cp in cps:
            for send, _ in cp.values():
                send().wait_send()

    return _Side(ws, [jax.ShapeDtypeStruct((4,) + w.shape, w.dtype) for w in ws], 8 * len(ws), start, wait,
                 mids=[(GATHER_STAGES[0], stage2), (GATHER_STAGES[1], stage3)])


def _reduce_siblings(ps):
    def pairs_of(p_refs, r_refs, ssem, rsem):
        x, y, c = _coords()
        out = []
        for a, p in enumerate(ps):
            src = p_refs[a].at[:, _half_rows(p.shape[1], 1 - c), :]
            cp = _later(src, r_refs[a], ssem.at[a], rsem.at[a], (x, y, 1 - c))
            out.append((cp, cp))
        return out

    return _side_from(ps, [jax.ShapeDtypeStruct((4, p.shape[1] // 2, p.shape[2]), p.dtype) for p in ps], len(ps), pairs_of)


def _reduce_chips(hs, dests=(0, 1, 2)):
    nd = len(dests)

    def pairs_of(h_refs, o_refs, ssem, rsem):
        x, y, c = _coords()
        chips = _other_chips(x, y)
        out = []
        for a in range(len(hs)):
            for slot, j in enumerate(dests):
                chip = chips[j]
                cp = _later(h_refs[a].at[2 * chip[0] + chip[1]], o_refs[a].at[slot], ssem.at[nd * a + slot], rsem.at[nd * a + slot],
                            (*chip, c))
                out.append((cp, cp))
        return out

    return _side_from(hs, [jax.ShapeDtypeStruct((nd,) + h.shape[1:], h.dtype) for h in hs], nd * len(hs), pairs_of)


def _share_siblings(gs):
    def pairs_of(_, g_refs, ssem, rsem):
        x, y, c = _coords()
        out = []
        for a, g in enumerate(gs):
            mine, theirs = g_refs[a].at[_half_rows(g.shape[0], c)], g_refs[a].at[_half_rows(g.shape[0], 1 - c)]
            sems = (ssem.at[a], rsem.at[a], (x, y, 1 - c))
            out.append((_later(mine, mine, *sems), _later(theirs, theirs, *sems)))
        return out

    return _side_from(gs, [jax.ShapeDtypeStruct(g.shape, g.dtype) for g in gs], len(gs), pairs_of, alias=True)


def _exchange(side, name):
    def body():
        pass

    _pcall(body, name=name, out_shape=[], in_specs=[], out_specs=[], sides=[side])()
    return side.results


def _add_sibling(p, r, core, name):
    _, rr, cc = p.shape
    h = rr // 2
    th = _pick(h, max(16, (2 << 20) // (2 * cc)), 16)
    nb = h // th

    def body(c_ref, p_ref, r_ref, o_ref):
        o_ref[...] = (p_ref[...].astype(F32) + r_ref[...].astype(F32)).astype(BF16)

    return _pcall(body, name=name, out_shape=jax.ShapeDtypeStruct((4, h, cc), BF16), grid=(4, nb),
                  in_specs=[pl.BlockSpec((None, th, cc), lambda k, i, c_ref: (k, c_ref[0] * nb + i, 0)),
                            pl.BlockSpec((None, th, cc), lambda k, i, c_ref: (k, i, 0))],
                  out_specs=pl.BlockSpec((None, th, cc), lambda k, i, c_ref: (k, i, 0)),
                  sem=("parallel", "parallel"), prefetch=1)(core, p, r)


def _add_chips(hh, pieces, chip_core, name):
    _, h, cc = hh.shape
    th = _pick(h, max(16, (2 << 20) // (2 * cc)), 16)
    nb = h // th

    def body(k_ref, h_ref, r0_ref, r1_ref, r2_ref, o_ref):
        s = h_ref[...].astype(F32) + r0_ref[...].astype(F32)
        s = s + r1_ref[...].astype(F32)
        o_ref[...] = s + r2_ref[...].astype(F32)

    def piece(slot):
        return pl.BlockSpec((None, th, cc), lambda i, k_ref: (slot, i, 0))

    return _pcall(body, name=name, out_shape=jax.ShapeDtypeStruct((2 * h, cc), F32), grid=(nb,),
                  in_specs=[pl.BlockSpec((None, th, cc), lambda i, k_ref: (k_ref[0], i, 0))] + [piece(s) for _, s in pieces],
                  out_specs=pl.BlockSpec((th, cc), lambda i, k_ref: (k_ref[1] * nb + i, 0)),
                  sem=("parallel",), prefetch=1)(chip_core, hh, *[a for a, _ in pieces])


def _adamw_math(w, g, m, v):
    m = ADAM_B1 * m + (1.0 - ADAM_B1) * g
    v = ADAM_B2 * v + (1.0 - ADAM_B2) * (g * g)
    m_hat = m / (1.0 - ADAM_B1 ** ADAM_STEP)
    v_hat = v / (1.0 - ADAM_B2 ** ADAM_STEP)
    delta = -ADAM_LR * (m_hat / (jnp.sqrt(v_hat) + ADAM_EPS) + ADAM_WD * w)
    return delta, m, v


def _adamw(w, g, m, v, name, sides=()):
    rr, cc = w.shape
    tr = _pick(rr, max(8, (3 << 20) // (4 * cc)), 8)

    def body(w_ref, g_ref, m_ref, v_ref, d_ref, mo_ref, vo_ref):
        d, mm, vv = _adamw_math(w_ref[...], g_ref[...], m_ref[...], v_ref[...])
        d_ref[...] = d
        mo_ref[...] = mm
        vo_ref[...] = vv

    spec = pl.BlockSpec((tr, cc), lambda i: (i, 0))
    return _pcall(body, name=name, out_shape=[jax.ShapeDtypeStruct(w.shape, F32)] * 3, grid=(rr // tr,),
                  in_specs=[spec] * 4, out_specs=[spec] * 3, sem=("parallel",), sides=sides)(w, g, m, v)


def _adamw_small(w, g8, m, v, name):
    def body(w_ref, g_ref, m_ref, v_ref, go_ref, d_ref, mo_ref, vo_ref):
        g = g_ref[0]
        for k in range(1, 8):
            g = g + g_ref[k]
        d, mm, vv = _adamw_math(w_ref[...], g, m_ref[...], v_ref[...])
        go_ref[...] = g
        d_ref[...] = d
        mo_ref[...] = mm
        vo_ref[...] = vv

    return _pcall(body, name=name, out_shape=[jax.ShapeDtypeStruct(w.shape, F32)] * 4,
                  in_specs=[VMEM] * 4, out_specs=[VMEM] * 4)(w, g8, m, v)


def _ada_fwd(c16, w_ada, b_ada, name):
    n = w_ada.shape[1]
    tn = _pick(n, 512, LANE)

    def body(c_ref, w_ref, b_ref, act_ref, mod_ref):
        cv = c_ref[...]
        act = cv * jax.nn.sigmoid(cv)
        act_ref[...] = act
        mod_ref[...] = _dot(act.astype(BF16), w_ref[...].astype(BF16)) + b_ref[...]

    return _pcall(body, name=name,
                  out_shape=[jax.ShapeDtypeStruct((16, D), F32), jax.ShapeDtypeStruct((16, n), F32)], grid=(n // tn,),
                  in_specs=[pl.BlockSpec((16, D), lambda j: (0, 0)), pl.BlockSpec((D, tn), lambda j: (0, j)),
                            pl.BlockSpec((1, tn), lambda j: (0, j))],
                  out_specs=[pl.BlockSpec((16, D), lambda j: (0, 0)), pl.BlockSpec((16, tn), lambda j: (0, j))],
                  sem=("arbitrary",))(c16, w_ada, b_ada)


def _ada_bwd(act, dmod, name, sides=()):
    n = dmod.shape[1]
    tm, tn = 512, _pick(n, 1024, LANE)

    def body(a_ref, d_ref, o_ref):
        o_ref[...] = _dot(a_ref[...].astype(BF16), d_ref[...].astype(BF16), TN)

    return _pcall(body, name=name, out_shape=jax.ShapeDtypeStruct((D, n), F32), grid=(D // tm, n // tn),
                  in_specs=[pl.BlockSpec((16, tm), lambda i, j: (0, i)), pl.BlockSpec((16, tn), lambda i, j: (0, j))],
                  out_specs=pl.BlockSpec((tm, tn), lambda i, j: (i, j)), sem=("parallel", "parallel"), sides=sides)(act, dmod)


def _norm_mod(x, g, sh, sc, name, sides=()):
    t = x.shape[0]
    tm = _pick(t, 512, 8)

    def body(x_ref, g_ref, sh_ref, sc_ref, h_ref):
        xf = x_ref[...]
        r = lax.rsqrt(jnp.mean(xf * xf, axis=-1, keepdims=True) + EPS)
        h = (xf * r) * g_ref[...]
        h_ref[...] = (h * (1.0 + sc_ref[...]) + sh_ref[...]).astype(BF16)

    vec = pl.BlockSpec((1, D), lambda i: (0, 0))
    row = pl.BlockSpec((tm, D), lambda i: (i, 0))
    return _pcall(body, name=name, out_shape=jax.ShapeDtypeStruct((t, D), BF16), grid=(t // tm,),
                  in_specs=[row, vec, vec, vec], out_specs=row, sem=("parallel",), sides=sides)(x, g, sh, sc)


def _norm_mod_bwd(dh, x, dxo, g, sc, name, sides=()):
    t = x.shape[0]
    tm = _pick(t, 256, 8)

    def body(dh_ref, x_ref, dxo_ref, g_ref, sc_ref, dx_ref, dsh_ref, dsc_ref, dg_ref):
        @pl.when(pl.program_id(0) == 0)
        def _():
            dsh_ref[...] = jnp.zeros_like(dsh_ref)
            dsc_ref[...] = jnp.zeros_like(dsc_ref)
            dg_ref[...] = jnp.zeros_like(dg_ref)

        xf, dh_ = x_ref[...], dh_ref[...]
        r = lax.rsqrt(jnp.mean(xf * xf, axis=-1, keepdims=True) + EPS)
        xhat = xf * r
        dsh_ref[...] += jnp.sum(dh_, axis=0, keepdims=True)
        dsc_ref[...] += jnp.sum(dh_ * (xhat * g_ref[...]), axis=0, keepdims=True)
        tt = dh_ * (1.0 + sc_ref[...])
        dg_ref[...] += jnp.sum(tt * xhat, axis=0, keepdims=True)
        dxh = tt * g_ref[...]
        dx_ref[...] = r * (dxh - xhat * jnp.mean(dxh * xhat, axis=-1, keepdims=True)) + dxo_ref[...]

    vec = pl.BlockSpec((1, D), lambda i: (0, 0))
    row = pl.BlockSpec((tm, D), lambda i: (i, 0))
    vshape = jax.ShapeDtypeStruct((1, D), F32)
    return _pcall(body, name=name, out_shape=[jax.ShapeDtypeStruct((t, D), F32), vshape, vshape, vshape], grid=(t // tm,),
                  in_specs=[row, row, row, vec, vec], out_specs=[row, vec, vec, vec], sem=("arbitrary",), sides=sides)(dh, x, dxo, g, sc)


def _gate_bwd(dxo, yy, gate, name):
    t = dxo.shape[0]
    tm = _pick(t, 512, 8)

    def body(dx_ref, y_ref, g_ref, dy_ref, dg_ref):
        @pl.when(pl.program_id(0) == 0)
        def _():
            dg_ref[...] = jnp.zeros_like(dg_ref)

        dx = dx_ref[...]
        dy_ref[...] = (dx * g_ref[...]).astype(BF16)
        dg_ref[...] += jnp.sum(dx * y_ref[...], axis=0, keepdims=True)

    vec = pl.BlockSpec((1, D), lambda i: (0, 0))
    row = pl.BlockSpec((tm, D), lambda i: (i, 0))
    return _pcall(body, name=name, out_shape=[jax.ShapeDtypeStruct((t, D), BF16), jax.ShapeDtypeStruct((1, D), F32)],
                  grid=(t // tm,), in_specs=[row, row, vec], out_specs=[row, vec], sem=("arbitrary",))(dxo, yy, gate)


def _final_loss(x, g, tgt, name):
    t = x.shape[0]
    tm = _pick(t, 256, 8)

    def body(x_ref, g_ref, t_ref, loss_ref, dx_ref, dg_ref):
        @pl.when(pl.program_id(0) == 0)
        def _():
            loss_ref[...] = jnp.zeros_like(loss_ref)
            dg_ref[...] = jnp.zeros_like(dg_ref)

        xf = x_ref[...]
        r = lax.rsqrt(jnp.mean(xf * xf, axis=-1, keepdims=True) + EPS)
        xhat = xf * r
        e = xhat * g_ref[...] - t_ref[...]
        per_tok = jnp.mean(e * e, axis=-1, keepdims=True)
        loss_ref[...] += 0.5 * jnp.sum(per_tok, axis=0, keepdims=True)
        dy = e * (1.0 / D)
        dg_ref[...] += jnp.sum(dy * xhat, axis=0, keepdims=True)
        dxh = dy * g_ref[...]
        dx_ref[...] = r * (dxh - xhat * jnp.mean(dxh * xhat, axis=-1, keepdims=True))

    vec = pl.BlockSpec((1, D), lambda i: (0, 0))
    row = pl.BlockSpec((tm, D), lambda i: (i, 0))
    return _pcall(body, name=name,
                  out_shape=[jax.ShapeDtypeStruct((1, LANE), F32), jax.ShapeDtypeStruct((t, D), F32),
                             jax.ShapeDtypeStruct((1, D), F32)],
                  grid=(t // tm,), in_specs=[row, vec, row],
                  out_specs=[pl.BlockSpec((1, LANE), lambda i: (0, 0)), row, vec], sem=("arbitrary",))(x, g, tgt)


def _mm(a, b, out_dtype, name):
    m, k = a.shape
    n = b.shape[1]
    tm = _pick(m, 1024, 8)
    tn = _pick(n, 1408, LANE)

    def body(a_ref, b_ref, o_ref):
        o_ref[...] = _dot(a_ref[...], b_ref[...]).astype(out_dtype)

    return _pcall(body, name=name, out_shape=jax.ShapeDtypeStruct((m, n), out_dtype), grid=(m // tm, n // tn),
                  in_specs=[pl.BlockSpec((tm, k), lambda i, j: (i, 0)), pl.BlockSpec((k, tn), lambda i, j: (0, j))],
                  out_specs=pl.BlockSpec((tm, tn), lambda i, j: (i, j)), sem=("parallel", "parallel"))(a, b)


def _mm_resid(a, b, resid, gate, name, sides=()):
    m, k = a.shape
    n = b.shape[1]
    tm, tn, tk = _pick(m, 1024, 8), _pick(n, 1024, LANE), _pick(k, 2816, LANE)
    nk = k // tk

    def body(a_ref, b_ref, r_ref, g_ref, y_ref, o_ref, acc):
        kk = pl.program_id(2)

        @pl.when(kk == 0)
        def _():
            acc[...] = jnp.zeros_like(acc)

        acc[...] += _dot(a_ref[...], b_ref[...])

        @pl.when(kk == nk - 1)
        def _():
            y_ref[...] = acc[...].astype(BF16)
            o_ref[...] = r_ref[...] + g_ref[...] * acc[...]

    blk = pl.BlockSpec((tm, tn), lambda i, j, kk: (i, j))
    return _pcall(body, name=name, out_shape=[jax.ShapeDtypeStruct((m, n), BF16), jax.ShapeDtypeStruct((m, n), F32)],
                  grid=(m // tm, n // tn, nk),
                  in_specs=[pl.BlockSpec((tm, tk), lambda i, j, kk: (i, kk)), pl.BlockSpec((tk, tn), lambda i, j, kk: (kk, j)),
                            blk, pl.BlockSpec((1, tn), lambda i, j, kk: (0, j))],
                  out_specs=[blk, blk], scratch=[pltpu.VMEM((tm, tn), F32)],
                  sem=("parallel", "parallel", "arbitrary"), sides=sides)(a, b, resid, gate)


def _mm_nt(a, b, out_dtype, name, sides=()):
    m, k = a.shape
    n = b.shape[0]
    tm, tn, tk = _pick(m, 1024, 8), _pick(n, 1024, LANE), _pick(k, 2816, LANE)
    nk = k // tk

    def body(a_ref, b_ref, o_ref, acc):
        kk = pl.program_id(2)

        @pl.when(kk == 0)
        def _():
            acc[...] = jnp.zeros_like(acc)

        acc[...] += _dot(a_ref[...], b_ref[...], NT)

        @pl.when(kk == nk - 1)
        def _():
            o_ref[...] = acc[...].astype(out_dtype)

    return _pcall(body, name=name, out_shape=jax.ShapeDtypeStruct((m, n), out_dtype), grid=(m // tm, n // tn, nk),
                  in_specs=[pl.BlockSpec((tm, tk), lambda i, j, kk: (i, kk)), pl.BlockSpec((tn, tk), lambda i, j, kk: (j, kk))],
                  out_specs=pl.BlockSpec((tm, tn), lambda i, j, kk: (i, j)), scratch=[pltpu.VMEM((tm, tn), F32)],
                  sem=("parallel", "parallel", "arbitrary"), sides=sides)(a, b)


def _mm_tn(a, b, out_dtype, tm_cap, name, sides=()):
    t, m = a.shape
    n = b.shape[1]
    tm, tn, tk = _pick(m, tm_cap, LANE), _pick(n, 1408, LANE), _pick(t, 2048, 16)
    nk = t // tk

    def body(a_ref, b_ref, o_ref, acc):
        kk = pl.program_id(2)

        @pl.when(kk == 0)
        def _():
            acc[...] = jnp.zeros_like(acc)

        acc[...] += _dot(a_ref[...], b_ref[...], TN)

        @pl.when(kk == nk - 1)
        def _():
            o_ref[...] = acc[...].astype(out_dtype)

    return _pcall(body, name=name, out_shape=jax.ShapeDtypeStruct((m, n), out_dtype), grid=(m // tm, n // tn, nk),
                  in_specs=[pl.BlockSpec((tk, tm), lambda i, j, kk: (kk, i)), pl.BlockSpec((tk, tn), lambda i, j, kk: (kk, j))],
                  out_specs=pl.BlockSpec((tm, tn), lambda i, j, kk: (i, j)), scratch=[pltpu.VMEM((tm, tn), F32)],
                  sem=("parallel", "parallel", "arbitrary"), sides=sides)(a, b)


def _ffn_up(h, wg, name, sides=()):
    t = h.shape[0]
    fp = wg.shape[2]
    tm, tn = _pick(t, 2048, 8), 256
    nn = fp // tn

    def body(h_ref, wa_ref, wb_ref, a_ref, b_ref, u_ref):
        for r0 in range(0, tm, tm // FFN_ROW_CHUNKS):
            rows = slice(r0, r0 + tm // FFN_ROW_CHUNKS)
            hh = h_ref[rows, :]
            a = _dot(hh, wa_ref[...])
            b = _dot(hh, wb_ref[...])
            a_ref[rows, :] = a.astype(BF16)
            b_ref[rows, :] = b.astype(BF16)
            u_ref[rows, :] = (a * jax.nn.sigmoid(a) * b).astype(BF16)

    out = pl.BlockSpec((tm, tn), lambda i, j, n: (i, j * nn + n))
    return _pcall(body, name=name, out_shape=[jax.ShapeDtypeStruct((t, 2 * fp), BF16)] * 3, grid=(t // tm, 2, nn),
                  in_specs=[pl.BlockSpec((tm, D), lambda i, j, n: (i, 0)),
                            pl.BlockSpec((None, D, tn), lambda i, j, n: (j, 0, n)),
                            pl.BlockSpec((None, D, tn), lambda i, j, n: (j + 2, 0, n))],
                  out_specs=[out, out, out], sem=("parallel", "parallel", "parallel"), sides=sides)(h, wg, wg)


def _ffn_dab(dy, wo, a, b, name, sides=()):
    t = dy.shape[0]
    f2 = wo.shape[0]
    tm, tn = _pick(t, 2048, 8), 512

    def body(dy_ref, w_ref, a_ref, b_ref, o_ref):
        for r0 in range(0, tm, tm // FFN_ROW_CHUNKS):
            rows = slice(r0, r0 + tm // FFN_ROW_CHUNKS)
            du = _dot(dy_ref[rows, :], w_ref[...], NT)
            av, bv = a_ref[rows, :].astype(F32), b_ref[rows, :].astype(F32)
            sg = jax.nn.sigmoid(av)
            o_ref[0, rows, :] = (du * bv * (sg * (1.0 + av * (1.0 - sg)))).astype(BF16)
            o_ref[1, rows, :] = (du * (av * sg)).astype(BF16)

    blk = pl.BlockSpec((tm, tn), lambda i, n: (i, n))
    return _pcall(body, name=name, out_shape=jax.ShapeDtypeStruct((2, t, f2), BF16), grid=(t // tm, f2 // tn),
                  in_specs=[pl.BlockSpec((tm, D), lambda i, n: (i, 0)), pl.BlockSpec((tn, D), lambda i, n: (n, 0)), blk, blk],
                  out_specs=pl.BlockSpec((2, tm, tn), lambda i, n: (0, i, n)), sem=("parallel", "parallel"), sides=sides)(dy, wo, a, b)


def _ffn_dwin(h, dab, cols, name, sides=()):
    t = h.shape[0]
    fp = dab.shape[2] // 2
    tn = 1024 if (cols[0] % 1024 == 0 and cols[1] % 1024 == 0) else 512
    tm, tk = _pick(fp, 1408, LANE), _pick(t, 2048, 16)
    nm, nk = fp // tm, t // tk
    n0, n1 = cols[0] // tn, cols[1] // tn

    def body(d_ref, h_ref, o_ref, acc):
        kk = pl.program_id(3)

        @pl.when(kk == 0)
        def _():
            acc[...] = jnp.zeros_like(acc)

        acc[...] += _dot(d_ref[...], h_ref[...], TN)

        @pl.when(kk == nk - 1)
        def _():
            o_ref[...] = acc[...].astype(BF16)

    return _pcall(body, name=name, out_shape=jax.ShapeDtypeStruct((4, fp, (n1 - n0) * tn), BF16), grid=(4, nm, n1 - n0, nk),
                  in_specs=[pl.BlockSpec((None, tk, tm), lambda u, i, n, kk: (u // 2, kk, (u % 2) * nm + i)),
                            pl.BlockSpec((tk, tn), lambda u, i, n, kk: (kk, n0 + n))],
                  out_specs=pl.BlockSpec((None, tm, tn), lambda u, i, n, kk: (u, i, n)),
                  scratch=[pltpu.VMEM((tm, tn), F32)],
                  sem=("parallel", "parallel", "parallel", "arbitrary"), sides=sides)(dab, h)


def _ffn_dh(dab, wg, name, sides=()):
    t = dab.shape[1]
    fp = wg.shape[2]
    tm, tn, tk = _pick(t, 1024, 8), 1024, _pick(fp, 2816, LANE)
    nkk = fp // tk

    def body(d_ref, w_ref, o_ref, acc):
        u, kk = pl.program_id(2), pl.program_id(3)

        @pl.when((u == 0) & (kk == 0))
        def _():
            acc[...] = jnp.zeros_like(acc)

        acc[...] += _dot(d_ref[...], w_ref[...], NT)

        @pl.when((u == 3) & (kk == nkk - 1))
        def _():
            o_ref[...] = acc[...]

    return _pcall(body, name=name, out_shape=jax.ShapeDtypeStruct((t, D), F32), grid=(t // tm, D // tn, 4, nkk),
                  in_specs=[pl.BlockSpec((None, tm, tk), lambda i, j, u, kk: (u // 2, i, (u % 2) * nkk + kk)),
                            pl.BlockSpec((None, tn, tk), lambda i, j, u, kk: (u, j, kk))],
                  out_specs=pl.BlockSpec((tm, tn), lambda i, j, u, kk: (i, j)), scratch=[pltpu.VMEM((tm, tn), F32)],
                  sem=("parallel", "parallel", "arbitrary", "arbitrary"), sides=sides)(dab, wg)


def _split3(v):
    hi = v.astype(BF16)
    r1 = v - hi.astype(F32)
    mid = r1.astype(BF16)
    lo = (r1 - mid.astype(F32)).astype(BF16)
    return hi, mid, lo


def _tri_sum(tri, v):
    hi, mid, lo = _split3(v)
    return (_dot(tri, hi) + _dot(tri, mid)) + _dot(tri, lo)


def _forget_fwd(proj, b_pad, name):
    t = proj.shape[0]
    tb = _pick(t, 256, 8)
    col = D_PROJ // LANE

    def body(f_ref, b_ref, o_ref, carry):
        @pl.when(pl.program_id(0) == 0)
        def _():
            carry[...] = jnp.zeros_like(carry)

        z = f_ref[...] + b_ref[...]
        lf = jnp.minimum(z, 0.0) - jnp.log(1.0 + jnp.exp(-jnp.abs(z)))
        r = lax.broadcasted_iota(jnp.int32, (tb, tb), 0)
        cidx = lax.broadcasted_iota(jnp.int32, (tb, tb), 1)
        tri = (r >= cidx).astype(BF16)
        o_ref[...] = _tri_sum(tri, lf) + carry[...]
        carry[...] += jnp.sum(lf, axis=0, keepdims=True)

    return _pcall(body, name=name, out_shape=jax.ShapeDtypeStruct((t, LANE), F32), grid=(t // tb,),
                  in_specs=[pl.BlockSpec((tb, LANE), lambda i: (i, col)), pl.BlockSpec((1, LANE), lambda i: (0, 0))],
                  out_specs=pl.BlockSpec((tb, LANE), lambda i: (i, 0)), scratch=[pltpu.VMEM((1, LANE), F32)],
                  sem=("arbitrary",))(proj, b_pad)


def _forget_bwd(d_cum, proj, b_pad, name):
    t = proj.shape[0]
    tb = _pick(t, 256, 8)
    nb = t // tb
    col = D_PROJ // LANE

    def body(d_ref, f_ref, b_ref, o_ref, db_ref, carry):
        @pl.when(pl.program_id(0) == 0)
        def _():
            carry[...] = jnp.zeros_like(carry)
            db_ref[...] = jnp.zeros_like(db_ref)

        dc = d_ref[...]
        r = lax.broadcasted_iota(jnp.int32, (tb, tb), 0)
        cidx = lax.broadcasted_iota(jnp.int32, (tb, tb), 1)
        tri = (r <= cidx).astype(BF16)
        dlf = _tri_sum(tri, dc) + carry[...]
        carry[...] += jnp.sum(dc, axis=0, keepdims=True)
        z = f_ref[...] + b_ref[...]
        lane = lax.broadcasted_iota(jnp.int32, (tb, LANE), 1)
        dz = jnp.where(lane < N_HEADS, dlf * jax.nn.sigmoid(-z), 0.0)
        o_ref[...] = dz.astype(BF16)
        db_ref[...] += jnp.sum(dz, axis=0, keepdims=True)

    return _pcall(body, name=name, out_shape=[jax.ShapeDtypeStruct((t, LANE), BF16), jax.ShapeDtypeStruct((1, LANE), F32)],
                  grid=(nb,),
                  in_specs=[pl.BlockSpec((tb, LANE), lambda i: (nb - 1 - i, 0)),
                            pl.BlockSpec((tb, LANE), lambda i: (nb - 1 - i, col)),
                            pl.BlockSpec((1, LANE), lambda i: (0, 0))],
                  out_specs=[pl.BlockSpec((tb, LANE), lambda i: (nb - 1 - i, 0)), pl.BlockSpec((1, LANE), lambda i: (0, 0))],
                  scratch=[pltpu.VMEM((1, LANE), F32)], sem=("arbitrary",))(d_cum, proj, b_pad)


def _head_norm(v, g):
    r = lax.rsqrt(jnp.mean(v * v, axis=-1, keepdims=True) + EPS)
    return v * r, r


def _qkv_prep(proj, qg, kg, name):
    t = proj.shape[0]
    tm = _pick(t, 1024, 8)

    def body(q_ref, k_ref, v_ref, qg_ref, kg_ref, qo_ref, ko_ref, vo_ref):
        qo_ref[...] = (_head_norm(q_ref[...], None)[0] * qg_ref[...] * ATTN_SCALE).astype(BF16)
        ko_ref[...] = (_head_norm(k_ref[...], None)[0] * kg_ref[...]).astype(BF16)
        vo_ref[...] = v_ref[...].astype(BF16)

    def blk(off):
        return pl.BlockSpec((tm, HEAD_DIM), lambda i, h: (i, off + h))

    vec = pl.BlockSpec((1, HEAD_DIM), lambda i, h: (0, 0))
    return _pcall(body, name=name, out_shape=[jax.ShapeDtypeStruct((t, D_ATTN), BF16)] * 3, grid=(t // tm, N_HEADS),
                  in_specs=[blk(0), blk(N_HEADS), blk(2 * N_HEADS), vec, vec], out_specs=[blk(0)] * 3,
                  sem=("parallel", "parallel"))(proj, proj, proj, qg, kg)


def _qk_norm_bwd(dqn, dkn, proj, qg, kg, name):
    t = proj.shape[0]
    tm = _pick(t, 1024, 8)

    def one(d_ref, v_ref, g_ref, o_ref, dg_ref):
        xhat, r = _head_norm(v_ref[...], None)
        d = d_ref[...]
        dg_ref[...] += jnp.sum(d * xhat, axis=0, keepdims=True)
        dxh = d * g_ref[...]
        o_ref[...] = (r * (dxh - xhat * jnp.mean(dxh * xhat, axis=-1, keepdims=True))).astype(BF16)

    def body(dq_ref, dk_ref, q_ref, k_ref, qg_ref, kg_ref, qo_ref, ko_ref, dqg_ref, dkg_ref):
        @pl.when((pl.program_id(0) == 0) & (pl.program_id(1) == 0))
        def _():
            dqg_ref[...] = jnp.zeros_like(dqg_ref)
            dkg_ref[...] = jnp.zeros_like(dkg_ref)

        one(dq_ref, q_ref, qg_ref, qo_ref, dqg_ref)
        one(dk_ref, k_ref, kg_ref, ko_ref, dkg_ref)

    def blk(off):
        return pl.BlockSpec((tm, HEAD_DIM), lambda i, h: (i, off + h))

    vec = pl.BlockSpec((1, HEAD_DIM), lambda i, h: (0, 0))
    vshape = jax.ShapeDtypeStruct((1, HEAD_DIM), F32)
    return _pcall(body, name=name, out_shape=[jax.ShapeDtypeStruct((t, D_ATTN), BF16)] * 2 + [vshape, vshape],
                  grid=(t // tm, N_HEADS),
                  in_specs=[blk(0), blk(0), blk(0), blk(N_HEADS), vec, vec], out_specs=[blk(0), blk(0), vec, vec],
                  sem=("arbitrary", "arbitrary"))(dqn, dkn, proj, proj, qg, kg)


ATTN_SCALE = HEAD_DIM ** -0.5


def _logits(q, k, fq, fk, diag, r0, tq, tk):
    s = _dot(q, k, NT) + (fq - fk)
    if diag:
        r = r0 + lax.broadcasted_iota(jnp.int32, (tq, tk), 0)
        cidx = lax.broadcasted_iota(jnp.int32, (tq, tk), 1)
        s = jnp.where(r >= cidx, s, -jnp.inf)
    return s


def _head_cols(hp):
    return [(hh, slice(hh * HEAD_DIM, (hh + 1) * HEAD_DIM)) for hh in range(hp)]


def _tri(tt, n, by_row):
    if by_row:
        i = sum((tt >= k * (k + 1) // 2).astype(jnp.int32) for k in range(1, n))
        return i, tt - i * (i + 1) // 2
    j = sum((tt >= k * n - k * (k - 1) // 2).astype(jnp.int32) for k in range(1, n))
    return j + tt - (j * n - j * (j - 1) // 2), j


def _attn_fwd(q, k, v, fq, fk, name, sides=()):
    t = q.shape[0]
    tq = tk = _pick(t, ATTN_BLOCK, LANE)
    nk = t // tk

    hp, rc = ATTN_HEADS[0], tq // ATTN_ROW_CHUNKS[0]

    def body(q_ref, k_ref, v_ref, fq_ref, fk_ref, o_ref, lse_ref, m_s, l_s, acc):
        i, j = _tri(pl.program_id(1), nk, True)

        @pl.when(j == 0)
        def _():
            m_s[...] = jnp.full_like(m_s, -jnp.inf)
            l_s[...] = jnp.zeros_like(l_s)
            acc[...] = jnp.zeros_like(acc)

        def step(diag):
            for hh, cols in _head_cols(hp):
                for r0 in range(0, tq, rc):
                    rows = slice(r0, r0 + rc)
                    s = _logits(q_ref[rows, cols], k_ref[:, cols], fq_ref[hh, rows, :], fk_ref[hh], diag, r0, rc, tk)
                    m_new = jnp.maximum(m_s[hh, rows, :], jnp.max(s, axis=-1, keepdims=True))
                    alpha = jnp.exp(m_s[hh, rows, :] - m_new)
                    p = jnp.exp(s - m_new)
                    l_s[hh, rows, :] = alpha * l_s[hh, rows, :] + jnp.sum(p, axis=-1, keepdims=True)
                    acc[rows, cols] = alpha * acc[rows, cols] + _dot(p.astype(BF16), v_ref[:, cols])
                    m_s[hh, rows, :] = m_new

        @pl.when(j < i)
        def _():
            step(False)

        @pl.when(j == i)
        def _():
            step(True)

        @pl.when(j == i)
        def _():
            for hh, cols in _head_cols(hp):
                o_ref[:, cols] = (acc[:, cols] / l_s[hh]).astype(BF16)
                lse_ref[hh] = m_s[hh] + jnp.log(l_s[hh])

    qb = pl.BlockSpec((tq, hp * HEAD_DIM), lambda h, tt: (_tri(tt, nk, True)[0], h))
    kb = pl.BlockSpec((tk, hp * HEAD_DIM), lambda h, tt: (_tri(tt, nk, True)[1], h))
    col = pl.BlockSpec((hp, tq, 1), lambda h, tt: (h, _tri(tt, nk, True)[0], 0))
    return _pcall(body, name=name,
                  out_shape=[jax.ShapeDtypeStruct((t, D_ATTN), BF16), jax.ShapeDtypeStruct((N_HEADS, t, 1), F32)],
                  grid=(N_HEADS // hp, nk * (nk + 1) // 2),
                  in_specs=[qb, kb, kb, col, pl.BlockSpec((hp, 1, tk), lambda h, tt: (h, 0, _tri(tt, nk, True)[1]))],
                  out_specs=[qb, col],
                  scratch=[pltpu.VMEM((hp, tq, 1), F32), pltpu.VMEM((hp, tq, 1), F32), pltpu.VMEM((tq, hp * HEAD_DIM), F32)],
                  sem=("parallel", "arbitrary"), sides=sides)(q, k, v, fq, fk)


def _attn_bwd_q(q, k, v, o, do, lse, fq, fk, name, sides=()):
    t = q.shape[0]
    tq = tk = _pick(t, ATTN_BLOCK, LANE)
    nk = t // tk

    hp, rc = ATTN_HEADS[1], tq // ATTN_ROW_CHUNKS[1]

    def body(q_ref, k_ref, v_ref, o_ref, do_ref, lse_ref, fq_ref, fk_ref, dq_ref, dl_ref, dfq_ref, acc, dl_s, df_s):
        i, j = _tri(pl.program_id(1), nk, True)

        @pl.when(j == 0)
        def _():
            acc[...] = jnp.zeros_like(acc)
            df_s[...] = jnp.zeros_like(df_s)
            for hh, cols in _head_cols(hp):
                dl_s[hh] = jnp.sum(do_ref[:, cols].astype(F32) * o_ref[:, cols].astype(F32), axis=-1, keepdims=True)

        def step(diag):
            for hh, cols in _head_cols(hp):
                for r0 in range(0, tq, rc):
                    rows = slice(r0, r0 + rc)
                    s = _logits(q_ref[rows, cols], k_ref[:, cols], fq_ref[hh, rows, :], fk_ref[hh], diag, r0, rc, tk)
                    p = jnp.exp(s - lse_ref[hh, rows, :])
                    dp = _dot(do_ref[rows, cols], v_ref[:, cols], NT)
                    ds = p * (dp - dl_s[hh, rows, :])
                    df_s[hh, rows, :] += jnp.sum(ds, axis=-1, keepdims=True)
                    acc[rows, cols] += _dot(ds.astype(BF16), k_ref[:, cols])

        @pl.when(j < i)
        def _():
            step(False)

        @pl.when(j == i)
        def _():
            step(True)

        @pl.when(j == i)
        def _():
            dq_ref[...] = acc[...] * ATTN_SCALE
            dl_ref[...] = dl_s[...]
            dfq_ref[...] = df_s[...]

    qb = pl.BlockSpec((tq, hp * HEAD_DIM), lambda h, tt: (_tri(tt, nk, True)[0], h))
    kb = pl.BlockSpec((tk, hp * HEAD_DIM), lambda h, tt: (_tri(tt, nk, True)[1], h))
    col = pl.BlockSpec((hp, tq, 1), lambda h, tt: (h, _tri(tt, nk, True)[0], 0))
    cshape = jax.ShapeDtypeStruct((N_HEADS, t, 1), F32)
    return _pcall(body, name=name, out_shape=[jax.ShapeDtypeStruct((t, D_ATTN), F32), cshape, cshape],
                  grid=(N_HEADS // hp, nk * (nk + 1) // 2),
                  in_specs=[qb, kb, kb, qb, qb, col, col, pl.BlockSpec((hp, 1, tk), lambda h, tt: (h, 0, _tri(tt, nk, True)[1]))],
                  out_specs=[qb, col, col],
                  scratch=[pltpu.VMEM((tq, hp * HEAD_DIM), F32), pltpu.VMEM((hp, tq, 1), F32), pltpu.VMEM((hp, tq, 1), F32)],
                  sem=("parallel", "arbitrary"), sides=sides)(q, k, v, o, do, lse, fq, fk)


def _attn_bwd_kv(q, k, v, do, lse, delta, fq, fk, name, sides=()):
    t = q.shape[0]
    tq = tk = _pick(t, ATTN_BLOCK, LANE)
    nq = t // tq

    hp, rc = ATTN_HEADS[2], tq // ATTN_ROW_CHUNKS[2]

    def body(q_ref, k_ref, v_ref, do_ref, lse_ref, dl_ref, fq_ref, fk_ref, dk_ref, dv_ref, dfk_ref, dk_s, dv_s, df_s):
        i, j = _tri(pl.program_id(1), nq, False)

        @pl.when(i == j)
        def _():
            dk_s[...] = jnp.zeros_like(dk_s)
            dv_s[...] = jnp.zeros_like(dv_s)
            df_s[...] = jnp.zeros_like(df_s)

        def step(diag):
            for hh, cols in _head_cols(hp):
                for r0 in range(0, tq, rc):
                    rows = slice(r0, r0 + rc)
                    s = _logits(q_ref[rows, cols], k_ref[:, cols], fq_ref[hh, rows, :], fk_ref[hh], diag, r0, rc, tk)
                    p = jnp.exp(s - lse_ref[hh, rows, :])
                    dv_s[:, cols] += _dot(p.astype(BF16), do_ref[rows, cols], TN)
                    dp = _dot(do_ref[rows, cols], v_ref[:, cols], NT)
                    ds = p * (dp - dl_ref[hh, rows, :])
                    df_s[hh] -= jnp.sum(ds, axis=0, keepdims=True)
                    dk_s[:, cols] += _dot(ds.astype(BF16), q_ref[rows, cols], TN)

        @pl.when(i > j)
        def _():
            step(False)

        @pl.when(i == j)
        def _():
            step(True)

        @pl.when(i == nq - 1)
        def _():
            dk_ref[...] = dk_s[...]
            dv_ref[...] = dv_s[...].astype(BF16)
            dfk_ref[...] = df_s[...]

    qb = pl.BlockSpec((tq, hp * HEAD_DIM), lambda h, tt: (_tri(tt, nq, False)[0], h))
    kb = pl.BlockSpec((tk, hp * HEAD_DIM), lambda h, tt: (_tri(tt, nq, False)[1], h))
    col = pl.BlockSpec((hp, tq, 1), lambda h, tt: (h, _tri(tt, nq, False)[0], 0))
    row = pl.BlockSpec((hp, 1, tk), lambda h, tt: (h, 0, _tri(tt, nq, False)[1]))
    return _pcall(body, name=name,
                  out_shape=[jax.ShapeDtypeStruct((t, D_ATTN), F32), jax.ShapeDtypeStruct((t, D_ATTN), BF16),
                             jax.ShapeDtypeStruct((N_HEADS, 1, t), F32)],
                  grid=(N_HEADS // hp, nq * (nq + 1) // 2),
                  in_specs=[qb, kb, kb, qb, col, col, col, row], out_specs=[kb, kb, row],
                  scratch=[pltpu.VMEM((tk, hp * HEAD_DIM), F32), pltpu.VMEM((tk, hp * HEAD_DIM), F32), pltpu.VMEM((hp, 1, tk), F32)],
                  sem=("parallel", "arbitrary"), sides=sides)(q, k, v, do, lse, delta, fq, fk)


def _window_sum(v, w, back):
    n = v.shape[0]
    k = 1
    while k < w:
        v = v + pltpu.roll(v, k if back else n - k, axis=0)
        k *= 2
    return v


def _pool_fwd(proj, pw, ps, name):
    t = proj.shape[0]
    tm = _pick(t, POOL_BLOCK, HALO)
    col = 3 * D_ATTN // D_POOL

    def body(u_ref, prev_ref, pw_ref, ps_ref, pooled_ref, out_ref):
        i = pl.program_id(0)
        prev = jnp.where(i > 0, prev_ref[...], 0.0)
        ext = jnp.concatenate([prev, u_ref[...]], axis=0)
        pos = i * tm + lax.broadcasted_iota(jnp.int32, (tm, 1), 0)
        for g, w in enumerate(POOL_WINDOWS):
            cols = slice(g * GROUP_DIM, (g + 1) * GROUP_DIM)
            xg = ext[:, cols]
            sw = _window_sum(xg, w, True)[HALO:, :]
            cnt = jnp.minimum(pos + 1, w).astype(F32)
            pooled = (sw / cnt - xg[HALO:, :]).astype(BF16)
            pooled_ref[:, cols] = pooled
            out_ref[:, cols] = (_dot(pooled, pw_ref[g]) * ps_ref[:, cols]).astype(BF16)

    row = pl.BlockSpec((tm, D_POOL), lambda i: (i, 0))
    return _pcall(body, name=name, out_shape=[jax.ShapeDtypeStruct((t, D_POOL), BF16)] * 2, grid=(t // tm,),
                  in_specs=[pl.BlockSpec((tm, D_POOL), lambda i: (i, col)),
                            pl.BlockSpec((HALO, D_POOL), lambda i: (jnp.maximum(i * (tm // HALO) - 1, 0), col)),
                            pl.BlockSpec((len(POOL_WINDOWS), GROUP_DIM, GROUP_DIM), lambda i: (0, 0, 0)),
                            pl.BlockSpec((1, D_POOL), lambda i: (0, 0))],
                  out_specs=[row, row], sem=("parallel",))(proj, proj, pw, ps)


def _pool_bwd(dout, pooled, pw, ps, name):
    t = pooled.shape[0]
    tm = _pick(t, POOL_BLOCK, HALO)
    nb = t // tm
    ng = len(POOL_WINDOWS)

    def body(d_ref, nxt_ref, p_ref, pw_ref, ps_ref, du_ref, dpw_ref, dps_ref):
        i = pl.program_id(0)

        @pl.when(i == 0)
        def _():
            dpw_ref[...] = jnp.zeros_like(dpw_ref)
            dps_ref[...] = jnp.zeros_like(dps_ref)

        nxt = jnp.where(i < nb - 1, nxt_ref[...].astype(F32), 0.0)
        ext = jnp.concatenate([d_ref[...].astype(F32), nxt], axis=0)
        pos = i * tm + lax.broadcasted_iota(jnp.int32, (tm + HALO, 1), 0)
        for g, w in enumerate(POOL_WINDOWS):
            cols = slice(g * GROUP_DIM, (g + 1) * GROUP_DIM)
            pooled_g = p_ref[:, cols]
            dg = ext[:, cols]
            pm = _dot(pooled_g, pw_ref[g])
            dps_ref[:, cols] += jnp.sum(dg[:tm, :] * pm, axis=0, keepdims=True)
            dpm = (dg * ps_ref[:, cols]).astype(BF16)
            dpw_ref[g] += _dot(pooled_g, dpm[:tm, :], TN)
            dpooled = _dot(dpm, pw_ref[g], NT)
            cnt = jnp.minimum(pos + 1, w).astype(F32)
            fwd = _window_sum(dpooled / cnt, w, False)
            du_ref[:, cols] = (fwd[:tm, :] - dpooled[:tm, :]).astype(BF16)

    row = pl.BlockSpec((tm, D_POOL), lambda i: (i, 0))
    return _pcall(body, name=name,
                  out_shape=[jax.ShapeDtypeStruct((t, D_POOL), BF16), jax.ShapeDtypeStruct((ng, GROUP_DIM, GROUP_DIM), F32),
                             jax.ShapeDtypeStruct((1, D_POOL), F32)],
                  grid=(nb,),
                  in_specs=[pl.BlockSpec((tm, D_POOL), lambda i: (i, 1)),
                            pl.BlockSpec((HALO, D_POOL), lambda i: (jnp.minimum((i + 1) * (tm // HALO), t // HALO - 1), 1)),
                            row, pl.BlockSpec((ng, GROUP_DIM, GROUP_DIM), lambda i: (0, 0, 0)),
                            pl.BlockSpec((1, D_POOL), lambda i: (0, 0))],
                  out_specs=[row, pl.BlockSpec((ng, GROUP_DIM, GROUP_DIM), lambda i: (0, 0, 0)),
                             pl.BlockSpec((1, D_POOL), lambda i: (0, 0))],
                  sem=("arbitrary",))(dout, dout, pooled, pw, ps)


def _pad_cols(w, n):
    return jnp.pad(w, ((0, 0), (0, n - w.shape[1])))


def kernel(x, c, w_ada, b_ada, ffn1_norm_g, ffn1_w_in, ffn1_w_out, mix_norm_g, w_in, b_forget, q_norm_g, k_norm_g, pool_w, pool_scale, w_out, ffn2_norm_g, ffn2_w_in, ffn2_w_out, final_norm_g, loss_target, m_w_ada, m_b_ada, m_ffn1_norm_g, m_ffn1_w_in, m_ffn1_w_out, m_mix_norm_g, m_w_in, m_b_forget, m_q_norm_g, m_k_norm_g, m_pool_w, m_pool_scale, m_w_out, m_ffn2_norm_g, m_ffn2_w_in, m_ffn2_w_out, m_final_norm_g, v_w_ada, v_b_ada, v_ffn1_norm_g, v_ffn1_w_in, v_ffn1_w_out, v_mix_norm_g, v_w_in, v_b_forget, v_q_norm_g, v_k_norm_g, v_pool_w, v_pool_scale, v_w_out, v_ffn2_norm_g, v_ffn2_w_in, v_ffn2_w_out, v_final_norm_g):
    ax, ay, ac = _coords()
    chip = 2 * ax + ay
    me = 2 * chip + ac
    chip_core = jnp.stack([chip, ac]).astype(jnp.int32)
    core_arr = jnp.reshape(ac, (1,)).astype(jnp.int32)

    t = x.shape[1]
    xs = x.reshape(t, D)
    tgt = loss_target.reshape(t, D)
    hu = ffn1_w_out.shape[1]
    hup = -(-hu // LANE) * LANE
    ws_in = w_in.shape[2]
    ws_in_pad = -(-ws_in // LANE) * LANE
    n_ada = w_ada.shape[2]

    def ffn_in_shard(w):
        w = w[0].astype(BF16)
        return jnp.concatenate([_pad_cols(w[:, :hu], hup), _pad_cols(w[:, hu:], hup)], axis=1)

    def ffn_out_shard(w):
        return jnp.pad(w[0].astype(BF16), ((0, hup - hu), (0, 0)))

    shards1 = [ffn_in_shard(ffn1_w_in), ffn_out_shard(ffn1_w_out)]
    shards_mix = [_pad_cols(w_in[0].astype(BF16), ws_in_pad), pool_w[0].astype(BF16).reshape(GROUP_DIM, GROUP_DIM),
                  w_out[0].astype(BF16)]
    shards2 = [ffn_in_shard(ffn2_w_in), ffn_out_shard(ffn2_w_out)]

    def add_siblings(parts, recv, tag):
        return [_add_sibling(p, r, core_arr, f"{tag}_add_sibling_{k}") for k, (p, r) in enumerate(zip(parts, recv))]

    def add_chips(halves, recv, tag):
        return [_add_chips(hh, [(r, 0), (r, 1), (r, 2)], chip_core, f"{tag}_add_chips_{k}") for k, (hh, r) in enumerate(zip(halves, recv))]

    g_in1, = _exchange(_gather(shards1[:1]), "gather_ffn1_in")

    c_all = _allgather_small(c.reshape(8, D // 8), True, "gather_c").reshape(8, D)
    c16 = jnp.pad(c_all, ((0, 8), (0, 0)))
    b_ada_mine = lax.dynamic_slice(b_ada, (0, chip * n_ada), (1, n_ada))
    act16, mod16 = _ada_fwd(c16, w_ada[0], b_ada_mine, "ada_fwd")
    mod_all = _allgather_small(mod16[:8], False, "gather_mod")
    mod = lax.dynamic_index_in_dim(mod_all, me, axis=1, keepdims=False).reshape(N_MOD, 1, D)
    sh1, sc1, gt1, sh2, sc2, gt2, sh3, sc3, gt3 = [mod[k] for k in range(N_MOD)]

    gate1, gate3 = 0.5 * gt1, 0.5 * gt3
    h1 = _norm_mod(xs, ffn1_norm_g, sh1, sc1, "ffn1_norm")
    job = _gather(shards1[1:])
    a1, b1, u1 = _ffn_up(h1, g_in1, "ffn1_up", sides=[job])
    wg_out1 = job.results[0].reshape(4 * hup, D)
    job = _gather(shards_mix)
    y1, x1 = _mm_resid(u1, wg_out1, xs, gate1, "ffn1_down", sides=[job])
    g_win, g_pw, g_wout = job.results
    h2 = _norm_mod(x1, mix_norm_g, sh2, sc2, "mix_norm")
    w_full = jnp.concatenate([g_win[k, :, :ws_in] for k in range(4)], axis=1)
    nf = 3 * D_ATTN
    w_all = jnp.concatenate([w_full[:, :nf], w_full[:, nf + N_HEADS:], w_full[:, nf:nf + N_HEADS],
                             jnp.zeros((D, LANE - N_HEADS), BF16)], axis=1)
    pw_full = g_pw.reshape(4, 4, GROUP_DIM // 4, GROUP_DIM).transpose(1, 0, 2, 3).reshape(4, GROUP_DIM, GROUP_DIM)
    wo_full = g_wout.reshape(4 * g_wout.shape[1], D)

    proj = _mm(h2, w_all, F32, "mix_proj")
    b_pad = jnp.pad(b_forget, ((0, 0), (0, LANE - N_HEADS)))
    cum = _forget_fwd(proj, b_pad, "forget_fwd")
    cum_t = cum[:, :N_HEADS].T
    fq, fk = cum_t.reshape(N_HEADS, t, 1), cum_t.reshape(N_HEADS, 1, t)
    qn, kn, vb = _qkv_prep(proj, q_norm_g, k_norm_g, "qkv_prep")
    job = _gather(shards2[:1])
    attn, lse = _attn_fwd(qn, kn, vb, fq, fk, "attn_fwd", sides=[job])
    g_in2, = job.results
    pooled, pool_out = _pool_fwd(proj, pw_full, pool_scale, "pool_fwd")
    cat = jnp.concatenate([attn, pool_out], axis=1)
    y_mix, x2 = _mm_resid(cat, wo_full, x1, gt2, "mix_out")

    h3 = _norm_mod(x2, ffn2_norm_g, sh3, sc3, "ffn2_norm")
    job = _gather(shards2[1:])
    a3, b3, u3 = _ffn_up(h3, g_in2, "ffn2_up", sides=[job])
    g_out2, = job.results
    wg_out2 = g_out2.reshape(4 * hup, D)
    y3, x3 = _mm_resid(u3, wg_out2, x2, gate3, "ffn2_down")

    loss_part, dx3, d_final_g = _final_loss(x3, final_norm_g.reshape(1, D), tgt, "final_loss")
    loss = lax.psum(loss_part[0, 0], ("x", "y", "c"))

    all_cols = (0, D)
    dy3, dgt3 = _gate_bwd(dx3, y3, gate3, "ffn2_gate_bwd")
    dab3 = _ffn_dab(dy3, wg_out2, a3, b3, "ffn2_dab")
    parts2 = [_ffn_dwin(h3, dab3, all_cols, "ffn2_dwin"), _mm_tn(u3, dy3, BF16, hup, "ffn2_dwout").reshape(4, hup, D)]
    rs = _reduce_siblings(parts2)
    dh3 = _ffn_dh(dab3, g_in2, "ffn2_dh", sides=[rs])
    half_in2, half_out2 = add_siblings(parts2, rs.results, "ffn2")
    dx2, dsh3, dsc3, d_ng3 = _norm_mod_bwd(dh3, x2, dx3, ffn2_norm_g, sc3, "ffn2_norm_bwd")

    dz, dgt2 = _gate_bwd(dx2, y_mix, gt2, "mix_gate_bwd")
    dcat = _mm_nt(dz, wo_full, BF16, "mix_dcat")
    d_wo = _mm_tn(cat, dz, BF16, 1024, "mix_dwout")
    du_pool, d_pw, d_ps = _pool_bwd(dcat, pooled, pw_full, pool_scale, "pool_bwd")
    rs = _reduce_chips([half_in2])
    dqn, delta, dfq = _attn_bwd_q(qn, kn, vb, attn, dcat, lse, fq, fk, "attn_bwd_q", sides=[rs])
    mine_in2, = add_chips([half_in2], rs.results, "ffn2_in")
    rs, rs3 = _reduce_chips([half_out2]), _share_siblings([mine_in2])
    dkn, dv, dfk = _attn_bwd_kv(qn, kn, vb, dcat, lse, delta, fq, fk, "attn_bwd_kv", sides=[rs, rs3])
    r_in2, = rs3.results
    mine_out2, = add_chips([half_out2], rs.results, "ffn2_out")
    d_cum = jnp.pad((dfq.reshape(N_HEADS, t) + dfk.reshape(N_HEADS, t)).T, ((0, 0), (0, LANE - N_HEADS)))
    dfl, d_bf = _forget_bwd(d_cum, proj, b_pad, "forget_bwd")
    dq, dk, d_qg, d_kg = _qk_norm_bwd(dqn, dkn, proj, q_norm_g, k_norm_g, "qk_norm_bwd")
    dproj = jnp.concatenate([dq, dk, dv, du_pool, dfl], axis=1)
    rs3 = _share_siblings([mine_out2])
    d_wall = _mm_tn(dproj, h2, F32, 1408, "mix_dwin", sides=[rs3])
    r_out2, = rs3.results
    d_wfull = jnp.concatenate([d_wall[:nf], d_wall[D_PROJ:D_PROJ + N_HEADS], d_wall[nf:D_PROJ]], axis=0)
    p_win = jnp.stack([jnp.pad(d_wfull[k * ws_in:(k + 1) * ws_in], ((0, ws_in_pad - ws_in), (0, 0))) for k in range(4)]).astype(BF16)
    p_pw = d_pw.reshape(4, 4, GROUP_DIM // 4, GROUP_DIM).transpose(1, 0, 2, 3).reshape(4, GROUP_DIM, GROUP_DIM).astype(BF16)
    parts_mix = [p_win, p_pw, d_wo.reshape(4, D // 4, D)]
    rs = _reduce_siblings(parts_mix)
    dh2 = _mm_nt(dproj, w_all, F32, "mix_dh", sides=[rs])
    halves_mix = add_siblings(parts_mix, rs.results, "mix")
    dx1, dsh2, dsc2, d_ng2 = _norm_mod_bwd(dh2, x1, dx2, mix_norm_g, sc2, "mix_norm_bwd")

    dy1, dgt1 = _gate_bwd(dx1, y1, gate1, "ffn1_gate_bwd")
    d_out1 = _mm_tn(u1, dy1, BF16, hup, "ffn1_dwout").reshape(4, hup, D)
    rs, rs1 = _reduce_chips(halves_mix), _reduce_siblings([d_out1])
    dab1 = _ffn_dab(dy1, wg_out1, a1, b1, "ffn1_dab", sides=[rs, rs1])
    mine_mix = add_chips(halves_mix, rs.results, "mix")
    half_out1, = add_siblings([d_out1], rs1.results, "ffn1_out")
    rs, rs3 = _reduce_chips([half_out1]), _share_siblings(mine_mix)
    d_in1_lo = _ffn_dwin(h1, dab1, (0, LAST_PIECE_FROM), "ffn1_dwin_lo", sides=[rs, rs3])
    r_win, r_pw, r_wo = rs3.results
    mine_out1, = add_chips([half_out1], rs.results, "ffn1_out")
    rs, rs3 = _reduce_siblings([d_in1_lo]), _share_siblings([mine_out1])
    d_in1_hi = _ffn_dwin(h1, dab1, (LAST_PIECE_FROM, D), "ffn1_dwin_hi", sides=[rs, rs3])
    r_out1, = rs3.results
    half_lo, = add_siblings([d_in1_lo], rs.results, "ffn1_lo")
    rs, rs1 = _reduce_chips([half_lo]), _reduce_siblings([d_in1_hi])
    dh1 = _ffn_dh(dab1, g_in1, "ffn1_dh", sides=[rs, rs1])
    mine_lo, = add_chips([half_lo], rs.results, "ffn1_lo")
    half_hi, = add_siblings([d_in1_hi], rs1.results, "ffn1_hi")
    dx0, dsh1, dsc1, d_ng1 = _norm_mod_bwd(dh1, xs, dx1, ffn1_norm_g, sc1, "ffn1_norm_bwd")
    grad_x = dx0.reshape(1, t, D)
    dgt1, dgt3 = 0.5 * dgt1, 0.5 * dgt3
    mine_hi, = add_chips([half_hi], _exchange(_reduce_chips([half_hi]), "ffn1_hi_reduce_chips"), "ffn1_hi")
    r_lo, r_hi = _exchange(_share_siblings([mine_lo, mine_hi]), "ffn1_share_siblings")
    r_in1 = jnp.concatenate([r_lo, r_hi], axis=1)
    grads = {
        "ffn1_w_in": jnp.concatenate([r_in1[:hu], r_in1[hup:hup + hu]], axis=0),
        "ffn1_w_out": r_out1[:hu],
        "w_in": r_win[:ws_in],
        "pool_w": r_pw,
        "w_out": r_wo,
        "ffn2_w_in": jnp.concatenate([r_in2[:hu], r_in2[hup:hup + hu]], axis=0),
        "ffn2_w_out": r_out2[:hu],
    }
    hidden_in_rows = ("ffn1_w_in", "ffn2_w_in", "w_in")

    dmod = jnp.concatenate([dsh1, dsc1, dgt1, dsh2, dsc2, dgt2, dsh3, dsc3, dgt3], axis=1)
    small_names = ["b_ada", "ffn1_norm_g", "mix_norm_g", "b_forget", "q_norm_g", "k_norm_g", "pool_scale", "ffn2_norm_g",
                   "final_norm_g"]
    small_grads = [dmod, d_ng1, d_ng2, d_bf[:, :N_HEADS], d_qg, d_kg, d_ps, d_ng3, d_final_g]
    small_w = [b_ada, ffn1_norm_g, mix_norm_g, b_forget, q_norm_g, k_norm_g, pool_scale, ffn2_norm_g, final_norm_g.reshape(1, D)]
    small_m = [m_b_ada, m_ffn1_norm_g, m_mix_norm_g, m_b_forget, m_q_norm_g, m_k_norm_g, m_pool_scale, m_ffn2_norm_g,
               m_final_norm_g.reshape(1, D)]
    small_v = [v_b_ada, v_ffn1_norm_g, v_mix_norm_g, v_b_forget, v_q_norm_g, v_k_norm_g, v_pool_scale, v_ffn2_norm_g,
               v_final_norm_g.reshape(1, D)]
    sizes = [g.shape[1] for g in small_grads]
    n_small = sum(sizes)
    n_pack = -(-n_small // (8 * LANE)) * (8 * LANE)

    def pack(vs, fill):
        flat = jnp.concatenate([v.reshape(1, -1) for v in vs], axis=1)
        return jnp.pad(flat, ((0, 0), (0, n_pack - n_small)), constant_values=fill).reshape(8, n_pack // 8)

    g8 = _allgather_small(pack(small_grads, 0.0), True, "gather_small_grads")
    gs, ds, ms, vs = _adamw_small(pack(small_w, 0.0), g8, pack(small_m, 0.0), pack(small_v, 1.0), "adamw_small")

    def unpack(p):
        flat = p.reshape(1, n_pack)
        out, off = {}, 0
        for nme, sz in zip(small_names, sizes):
            out[nme] = flat[:, off:off + sz]
            off += sz
        return out

    small = [unpack(p) for p in (gs, ds, ms, vs)]
    for dct in small:
        dct["final_norm_g"] = dct["final_norm_g"].reshape(D)

    dmod_all = g8.reshape(8, n_pack)[:, :N_MOD * D]
    dmod_mine = lax.dynamic_slice(dmod_all, (0, chip * n_ada), (8, n_ada))
    grads["w_ada"] = _ada_bwd(act16, jnp.pad(dmod_mine, ((0, 8), (0, 0))), "ada_bwd")

    big = {"w_ada": (w_ada, m_w_ada, v_w_ada), "ffn1_w_in": (ffn1_w_in, m_ffn1_w_in, v_ffn1_w_in),
           "ffn1_w_out": (ffn1_w_out, m_ffn1_w_out, v_ffn1_w_out), "w_in": (w_in, m_w_in, v_w_in),
           "pool_w": (pool_w, m_pool_w, v_pool_w), "w_out": (w_out, m_w_out, v_w_out),
           "ffn2_w_in": (ffn2_w_in, m_ffn2_w_in, v_ffn2_w_in), "ffn2_w_out": (ffn2_w_out, m_ffn2_w_out, v_ffn2_w_out)}
    res = {}

    for nme, (w, m, v) in big.items():
        shp = w.shape
        g2 = grads[nme]
        if nme in hidden_in_rows:
            rows = g2.shape[0]
            more = -rows % 8

            def fit(a):
                return jnp.pad(a, ((0, more), (0, 0))) if more else a

            outs = _adamw(fit(w[0].T), fit(g2), fit(m[0].T), fit(v[0].T), f"adamw_{nme}")
            res[nme] = tuple(o[:rows].T.reshape(shp) for o in (g2, *outs))
            continue
        two = g2.shape
        d, mo, vo = _adamw(w.reshape(two), g2, m.reshape(two), v.reshape(two), f"adamw_{nme}")
        res[nme] = (g2.reshape(shp), d.reshape(shp), mo.reshape(shp), vo.reshape(shp))
    for nme in small_names:
        res[nme] = tuple(dct[nme] for dct in small)

    order = ["w_ada", "b_ada", "ffn1_norm_g", "ffn1_w_in", "ffn1_w_out", "mix_norm_g", "w_in", "b_forget", "q_norm_g",
             "k_norm_g", "pool_w", "pool_scale", "w_out", "ffn2_norm_g", "ffn2_w_in", "ffn2_w_out", "final_norm_g"]
    return (loss, grad_x, *[res[n][0] for n in order], *[res[n][1] for n in order], *[res[n][2] for n in order],
            *[res[n][3] for n in order])
```
